```python
import math
import jax
import jax.numpy as jnp
from jax import lax
import numpy as np

D_MODEL = 1024
BATCH = 8
SEQ = 4096
DEPTH = 4

N_MIXERS = 4
GRID_W = 64
EPS = 1e-6
NEG_BIG = -1e30

SSD_DI = 2 * D_MODEL
SSD_HEADDIM = 64
SSD_HEADS = SSD_DI // SSD_HEADDIM
SSD_GROUPS = 4
SSD_HPG = SSD_HEADS // SSD_GROUPS
SSD_STATE = 128
SSD_CONV = 7
SSD_CHUNK = 128
SSD_CONV_CH = SSD_DI + 2 * SSD_GROUPS * SSD_STATE
SSD_IN = SSD_DI + SSD_CONV_CH + 2 * SSD_HEADS

HG_EXPAND = 128
HG_HEADS = D_MODEL // HG_EXPAND
HG_W = HG_HEADS * HG_EXPAND
HG_DV = HG_W // HG_HEADS
HG_CHUNK = 32
HG_IN = 5 * HG_W

AT_HEADS = 16
AT_KV = 8
AT_GRP = AT_HEADS // AT_KV
AT_HD = 128
AT_QBLK = 128
ROPE_THETA = 10000.0
ROPE_AXIS = AT_HD // 2
AT_QW = AT_HEADS * AT_HD
AT_KW = AT_KV * AT_HD
AT_IN = 2 * AT_QW + 2 * AT_KW

DL_PAIRS = ((128, 1), (512, 4), (2048, 16))
DL_HEADS = 16
DL_HD = 64
DL_W = DL_HEADS * DL_HD
DL_IN = 3 * len(DL_PAIRS) * DL_W + DL_W

REL_BUCKETS = 32
REL_MAX_DIST = 1024

N_SSD_LAYERS = (DEPTH - 0 + N_MIXERS - 1) // N_MIXERS
N_HG_LAYERS = (DEPTH - 1 + N_MIXERS - 1) // N_MIXERS
N_AT_LAYERS = (DEPTH - 2 + N_MIXERS - 1) // N_MIXERS
N_DL_LAYERS = (DEPTH - 3 + N_MIXERS - 1) // N_MIXERS

kernel_name = "hybrid_bidir_ssd_hgrn2_axialgqa_dilated"

F32 = jnp.float32


def rms_norm(x, g):
    xf = x.astype(F32)
    y = xf * lax.rsqrt(jnp.mean(xf * xf, axis=-1, keepdims=True) + EPS)
    return (y * g.astype(F32)).astype(x.dtype)


def centred_depthwise_conv(x, w, b):
    C = x.shape[-1]
    pad = w.shape[0] // 2
    y = lax.conv_general_dilated(x, w[:, None, :].astype(x.dtype), window_strides=(1,),
                                 padding=((pad, pad),), dimension_numbers=('NWC', 'WIO', 'NWC'),
                                 feature_group_count=C)
    return y + b.astype(x.dtype)


def exp_segsum(a):
    T = a.shape[-1]
    cs = jnp.cumsum(a, axis=-1)
    diff = cs[..., :, None] - cs[..., None, :]
    mask = jnp.tril(jnp.ones((T, T), dtype=bool))
    return jnp.exp(jnp.where(mask, diff, -jnp.inf))


def ssd_chunked(xdt, dA, Bm, Cm):
    b, S, G, J, P = xdt.shape
    N = Bm.shape[-1]
    Q = SSD_CHUNK
    nc = S // Q
    x = xdt.reshape(b, nc, Q, G, J, P)
    a = dA.reshape(b, nc, Q, G, J).transpose(0, 3, 4, 1, 2)
    Bc = Bm.reshape(b, nc, Q, G, N)
    Cc = Cm.reshape(b, nc, Q, G, N)
    a_cum = jnp.cumsum(a, axis=-1)
    L = exp_segsum(a)
    CB = jnp.einsum('bclgn,bcsgn->bcgls', Cc, Bc)
    y_diag = jnp.einsum('bcgls,bgjcls,bcsgjp->bclgjp', CB, L, x)
    decay_states = jnp.exp(a_cum[..., -1:] - a_cum)
    states = jnp.einsum('bclgn,bgjcl,bclgjp->bcgjpn', Bc, decay_states, x)
    states = jnp.concatenate([jnp.zeros_like(states[:, :1]), states], axis=1)
    chunk_decay = exp_segsum(jnp.pad(a_cum[..., -1], ((0, 0), (0, 0), (0, 0), (1, 0))))
    start_states = jnp.einsum('bgjzc,bcgjpn->bzgjpn', chunk_decay, states)[:, :-1]
    y_off = jnp.einsum('bclgn,bcgjpn,bgjcl->bclgjp', Cc, start_states, jnp.exp(a_cum))
    return (y_diag + y_off).reshape(b, S, G, J, P)


def ssd_mixer(h, w_in, conv_w, conv_b, dt_bias, a_log, d_skip, norm_g, w_out):
    b, S, _ = h.shape
    GN = SSD_GROUPS * SSD_STATE
    u = h @ w_in
    z = u[..., :SSD_DI]
    xbc = u[..., SSD_DI:SSD_DI + SSD_CONV_CH]
    dt_raw = u[..., SSD_DI + SSD_CONV_CH:].reshape(b, S, 2, SSD_HEADS)
    xbc = jax.nn.silu(centred_depthwise_conv(xbc, conv_w, conv_b))
    xs = xbc[..., :SSD_DI].reshape(b, S, SSD_GROUPS, SSD_HPG, SSD_HEADDIM)
    Bm = xbc[..., SSD_DI:SSD_DI + GN].reshape(b, S, SSD_GROUPS, SSD_STATE)
    Cm = xbc[..., SSD_DI + GN:].reshape(b, S, SSD_GROUPS, SSD_STATE)
    dt = jax.nn.softplus(dt_raw.astype(F32) + dt_bias.astype(F32))
    A = -jnp.exp(a_log.astype(F32))
    dA = (dt * A).reshape(b, S, 2, SSD_GROUPS, SSD_HPG)
    dtg = dt.reshape(b, S, 2, SSD_GROUPS, SSD_HPG)
    x_f = xs * dtg[:, :, 0, :, :, None]
    x_b = xs * dtg[:, :, 1, :, :, None]
    y_fwd = ssd_chunked(x_f, dA[:, :, 0], Bm, Cm)
    y_bwd = jnp.flip(ssd_chunked(jnp.flip(x_b, 1), jnp.flip(dA[:, :, 1], 1),
                                 jnp.flip(Bm, 1), jnp.flip(Cm, 1)), 1)
    y = y_fwd + y_bwd + xs * d_skip.reshape(SSD_GROUPS, SSD_HPG)[..., None]
    y = y.reshape(b, S, SSD_DI) * jax.nn.silu(z)
    gs = SSD_DI // SSD_GROUPS
    y = rms_norm(y.reshape(b, S, SSD_GROUPS, gs), norm_g.reshape(SSD_GROUPS, gs))
    return y.reshape(b, S, SSD_DI).astype(h.dtype) @ w_out


def hgrn2_chunked(q, k, v, g):
    b, S, h, dk = q.shape
    dv = v.shape[-1]
    C = HG_CHUNK
    nc = S // C

    def to_chunks(t):
        return t.reshape(b, nc, C, h, t.shape[-1]).transpose(1, 0, 3, 2, 4)

    mask = jnp.tril(jnp.ones((C, C), dtype=bool))

    def step(state, inp):
        qi, ki, vi, gi = inp
        G = jnp.cumsum(gi, axis=-2)
        Gr = G[..., C // 2:C // 2 + 1, :]
        q_t = qi * jnp.exp(G - Gr)
        k_t = ki * jnp.exp(Gr - G)
        att = jnp.where(mask, jnp.einsum('bhid,bhjd->bhij', q_t, k_t), 0.0)
        o = (jnp.einsum('bhij,bhjv->bhiv', att, vi)
             + jnp.einsum('bhid,bhdv->bhiv', qi * jnp.exp(G), state))
        G_last = G[..., -1:, :]
        new_state = (jnp.exp(G_last)[..., 0, :, None] * state
                     + jnp.einsum('bhjd,bhjv->bhdv', ki * jnp.exp(G_last - G), vi))
        return new_state, o

    s0 = jnp.zeros((b, h, dk, dv), F32)
    _, o = lax.scan(step, s0, (to_chunks(q), to_chunks(k), to_chunks(v), to_chunks(g)))
    return o.transpose(1, 0, 3, 2, 4).reshape(b, S, h, dv)


def hgrn2_mixer(h, lb, w_in, norm_g, w_out):
    b, S, _ = h.shape
    u = h @ w_in
    q, f_fwd, f_bwd, inp, gate = jnp.split(u, 5, axis=-1)
    shp = (b, S, HG_HEADS, HG_EXPAND)
    q = jax.nn.silu(q).astype(F32).reshape(shp)
    v = inp.astype(F32).reshape(b, S, HG_HEADS, HG_DV)
    lbh = lb.reshape(HG_HEADS, HG_EXPAND)

    def direction(fpre, reverse):
        f = lbh + (1.0 - lbh) * jax.nn.sigmoid(fpre.astype(F32).reshape(shp))
        args = (q, 1.0 - f, v, jnp.log(f))
        if reverse:
            return jnp.flip(hgrn2_chunked(*[jnp.flip(a, 1) for a in args]), 1)
        return hgrn2_chunked(*args)

    o = direction(f_fwd, False) + direction(f_bwd, True)
    o = rms_norm(o, norm_g.reshape(HG_HEADS, HG_DV)).reshape(b, S, HG_W).astype(h.dtype)
    return (o * jax.nn.silu(gate)) @ w_out


def axial_rope_tables(S):
    rows = S // GRID_W
    row = jnp.repeat(jnp.arange(rows), GRID_W).astype(F32)
    col = (jnp.arange(S) % GRID_W).astype(F32)
    inv = ROPE_THETA ** (-jnp.arange(0, ROPE_AXIS, 2, dtype=F32) / ROPE_AXIS)
    ang = jnp.stack([row[:, None] * inv, col[:, None] * inv], axis=1)
    return jnp.cos(ang), jnp.sin(ang)


def apply_axial_rope(x, cos, sin):
    shp = x.shape
    xf = x.astype(F32).reshape(*shp[:-1], 2, 2, ROPE_AXIS // 2)
    x1, x2 = xf[..., 0, :], xf[..., 1, :]
    c, s = cos[:, None], sin[:, None]
    out = jnp.stack([x1 * c - x2 * s, x2 * c + x1 * s], axis=-2)
    return out.reshape(shp).astype(x.dtype)


def gqa_mixer(h, w_in, q_g, k_g, w_out):
    b, S, _ = h.shape
    u = h @ w_in
    q = u[..., :AT_QW].reshape(b, S, AT_HEADS, AT_HD)
    k = u[..., AT_QW:AT_QW + AT_KW].reshape(b, S, AT_KV, AT_HD)
    v = u[..., AT_QW + AT_KW:AT_QW + 2 * AT_KW].reshape(b, S, AT_KV, AT_HD)
    gate = u[..., AT_QW + 2 * AT_KW:]
    cos, sin = axial_rope_tables(S)
    q = apply_axial_rope(rms_norm(q, q_g), cos, sin) * (AT_HD ** -0.5)
    k = apply_axial_rope(rms_norm(k, k_g), cos, sin)
    nq = S // AT_QBLK
    qb = q.reshape(b, nq, AT_QBLK, AT_KV, AT_GRP, AT_HD).transpose(1, 0, 2, 3, 4, 5)

    def block(qi):
        s = jnp.einsum('bqkgd,bskd->bkgqs', qi, k).astype(F32)
        p = jax.nn.softmax(s, axis=-1).astype(v.dtype)
        return jnp.einsum('bkgqs,bskd->bqkgd', p, v)

    o = lax.map(block, qb)
    o = o.transpose(1, 0, 2, 3, 4, 5).reshape(b, S, AT_QW)
    return (o * jax.nn.silu(gate)) @ w_out


def t5_bucket(rel):
    half = REL_BUCKETS // 2
    exact = half // 2
    n = jnp.abs(rel)
    large = exact + (jnp.log(jnp.maximum(n, 1).astype(F32) / exact)
                     / math.log(REL_MAX_DIST / exact) * (half - exact)).astype(jnp.int32)
    large = jnp.minimum(large, half - 1)
    return jnp.where(rel > 0, half, 0) + jnp.where(n < exact, n, large)


def dilated_group(q, k, v, dil, steps, rel_bias):
    b, S, h, e = q.shape
    Ls = S // dil
    blk = steps
    nb = -(-Ls // blk)
    Lp = nb * blk

    def sub(t):
        return t.reshape(b, Ls, dil, h, e).transpose(0, 2, 1, 3, 4)

    qb = jnp.pad(sub(q), ((0, 0), (0, 0), (0, Lp - Ls), (0, 0), (0, 0))).reshape(b, dil, nb, blk, h, e)
    kpad = ((0, 0), (0, 0), (blk, Lp - Ls + blk), (0, 0), (0, 0))
    kp = jnp.pad(sub(k), kpad).reshape(b, dil, nb + 2, blk, h, e)
    vp = jnp.pad(sub(v), kpad).reshape(b, dil, nb + 2, blk, h, e)
    kb = jnp.concatenate([kp[:, :, 0:nb], kp[:, :, 1:nb + 1], kp[:, :, 2:nb + 2]], axis=3)
    vb = jnp.concatenate([vp[:, :, 0:nb], vp[:, :, 1:nb + 1], vp[:, :, 2:nb + 2]], axis=3)
    i = jnp.arange(blk)[:, None]
    j = jnp.arange(3 * blk)[None, :]
    dm = j - blk - i
    m_k = jnp.arange(nb)[:, None, None] * blk + j[None] - blk
    mask = (jnp.abs(dm) <= steps)[None] & (m_k >= 0) & (m_k < Ls)
    bias = rel_bias[t5_bucket(dm * dil)].transpose(2, 0, 1).astype(F32)
    s = jnp.einsum('bdnqhe,bdnkhe->bdnhqk', qb, kb).astype(F32) + bias[None, None, None]
    s = jnp.where(mask[None, None, :, None], s, NEG_BIG)
    lse = jax.nn.logsumexp(s, axis=-1)
    p = jnp.exp(s - lse[..., None]).astype(v.dtype)
    o = jnp.einsum('bdnhqk,bdnkhe->bdnqhe', p, vb)
    o = o.reshape(b, dil, Lp, h, e)[:, :, :Ls].transpose(0, 2, 1, 3, 4).reshape(b, S, h, e)
    lse = lse.transpose(0, 1, 2, 4, 3).reshape(b, dil, Lp, h)[:, :, :Ls]
    lse = lse.transpose(0, 2, 1, 3).reshape(b, S, h)
    return o, lse


def dilated_mixer(h, rel_bias, w_in, w_out):
    b, S, _ = h.shape
    u = h @ w_in
    outs, lses = [], []
    for gi, (window, dil) in enumerate(DL_PAIRS):
        base = gi * 3 * DL_W
        q, k, v = [u[..., base + c * DL_W: base + (c + 1) * DL_W].reshape(b, S, DL_HEADS, DL_HD)
                   for c in range(3)]
        o, l = dilated_group(q * (DL_HD ** -0.5), k, v, dil, (window // 2) // dil, rel_bias)
        outs.append(o)
        lses.append(l)
    w = jax.nn.softmax(jnp.stack(lses), axis=0)
    o = jnp.sum(w[..., None] * jnp.stack(outs).astype(F32), axis=0)
    o = o.astype(h.dtype).reshape(b, S, DL_W)
    gate = u[..., 3 * len(DL_PAIRS) * DL_W:]
    return (o * jax.nn.silu(gate)) @ w_out


def _dense(key, shape, fan_in):
    return jax.random.normal(key, shape, F32) * (fan_in ** -0.5)


def _gain(key, shape):
    return 1.0 + 0.05 * jax.random.normal(key, shape, F32)


def _fwd_setup_inputs(seed: int = 0) -> dict:
    key = jax.random.key(seed)
    k = jax.random.split(key, 24)
    nA, nB, nC, nD = N_SSD_LAYERS, N_HG_LAYERS, N_AT_LAYERS, N_DL_LAYERS
    dt0 = jnp.exp(jax.random.uniform(k[8], (nA, 2, SSD_HEADS), F32,
                                     minval=math.log(1e-3), maxval=math.log(1e-1)))
    return {
        "x": jax.random.normal(k[0], (BATCH, SEQ, D_MODEL), F32),
        "norm_g": _gain(k[1], (DEPTH, D_MODEL)),
        "final_g": _gain(k[2], (D_MODEL,)),
        "rel_bias": 0.5 * jax.random.normal(k[3], (REL_BUCKETS, DL_HEADS), F32),
        "hgrn_lb": 0.5 * jax.random.normal(k[4], (DEPTH, HG_W), F32),
        "ssd_w_in": _dense(k[5], (nA, D_MODEL, SSD_IN), D_MODEL),
        "ssd_conv_w": _dense(k[6], (nA, SSD_CONV, SSD_CONV_CH), SSD_CONV),
        "ssd_conv_b": 0.02 * jax.random.normal(k[7], (nA, SSD_CONV_CH), F32),
        "ssd_dt_bias": dt0 + jnp.log(-jnp.expm1(-dt0)),
        "ssd_a_log": jnp.log(jax.random.uniform(k[9], (nA, 2, SSD_HEADS), F32, minval=1.0, maxval=16.0)),
        "ssd_d": 1.0 + 0.1 * jax.random.normal(k[10], (nA, SSD_HEADS), F32),
        "ssd_norm_g": _gain(k[11], (nA, SSD_DI)),
        "ssd_w_out": _dense(k[12], (nA, SSD_DI, D_MODEL), SSD_DI),
        "hg_w_in": _dense(k[13], (nB, D_MODEL, HG_IN), D_MODEL),
        "hg_norm_g": _gain(k[14], (nB, HG_W)),
        "hg_w_out": _dense(k[15], (nB, HG_W, D_MODEL), HG_W),
        "at_w_in": _dense(k[16], (nC, D_MODEL, AT_IN), D_MODEL),
        "at_q_norm_g": _gain(k[17], (nC, AT_HD)),
        "at_k_norm_g": _gain(k[18], (nC, AT_HD)),
        "at_w_out": _dense(k[19], (nC, AT_QW, D_MODEL), AT_QW),
        "dl_w_in": _dense(k[20], (nD, D_MODEL, DL_IN), D_MODEL),
        "dl_w_out": _dense(k[21], (nD, DL_W, D_MODEL), DL_W),
    }


def _fwd_reference(x, norm_g, final_g, rel_bias, hgrn_lb,
              ssd_w_in, ssd_conv_w, ssd_conv_b, ssd_dt_bias, ssd_a_log, ssd_d, ssd_norm_g, ssd_w_out,
              hg_w_in, hg_norm_g, hg_w_out,
              at_w_in, at_q_norm_g, at_k_norm_g, at_w_out,
              dl_w_in, dl_w_out):
    lb_sm = jax.nn.softmax(hgrn_lb.astype(F32), axis=0)
    lb_all = jnp.cumsum(lb_sm, axis=0) - lb_sm[0:1]
    for layer in range(DEPTH):
        kind = layer % N_MIXERS
        slot = layer // N_MIXERS
        hn = rms_norm(x, norm_g[layer])
        if kind == 0:
            y = ssd_mixer(hn, ssd_w_in[slot], ssd_conv_w[slot], ssd_conv_b[slot], ssd_dt_bias[slot],
                          ssd_a_log[slot], ssd_d[slot], ssd_norm_g[slot], ssd_w_out[slot])
        elif kind == 1:
            y = hgrn2_mixer(hn, lb_all[layer], hg_w_in[slot], hg_norm_g[slot], hg_w_out[slot])
        elif kind == 2:
            y = gqa_mixer(hn, at_w_in[slot], at_q_norm_g[slot], at_k_norm_g[slot], at_w_out[slot])
        else:
            y = dilated_mixer(hn, rel_bias, dl_w_in[slot], dl_w_out[slot])
        x = x + y.astype(x.dtype)
    return rms_norm(x, final_g)


import jax as _jax
import jax.numpy as _jnp

TWIN_FORMAT = 'train_step'
FWD_PARAMS = ['x', 'norm_g', 'final_g', 'rel_bias', 'hgrn_lb', 'ssd_w_in', 'ssd_conv_w', 'ssd_conv_b', 'ssd_dt_bias', 'ssd_a_log', 'ssd_d', 'ssd_norm_g', 'ssd_w_out', 'hg_w_in', 'hg_norm_g', 'hg_w_out', 'at_w_in', 'at_q_norm_g', 'at_k_norm_g', 'at_w_out', 'dl_w_in', 'dl_w_out']
TWIN_WEIGHTS = ['norm_g', 'final_g', 'rel_bias', 'hgrn_lb', 'ssd_w_in', 'ssd_conv_w', 'ssd_conv_b', 'ssd_dt_bias', 'ssd_a_log', 'ssd_d', 'ssd_norm_g', 'ssd_w_out', 'hg_w_in', 'hg_norm_g', 'hg_w_out', 'at_w_in', 'at_q_norm_g', 'at_k_norm_g', 'at_w_out', 'dl_w_in', 'dl_w_out']
TWIN_DIFF_INPUT = 'x'
TWIN_INPUTS = ['x', 'norm_g', 'final_g', 'rel_bias', 'hgrn_lb', 'ssd_w_in', 'ssd_conv_w', 'ssd_conv_b', 'ssd_dt_bias', 'ssd_a_log', 'ssd_d', 'ssd_norm_g', 'ssd_w_out', 'hg_w_in', 'hg_norm_g', 'hg_w_out', 'at_w_in', 'at_q_norm_g', 'at_k_norm_g', 'at_w_out', 'dl_w_in', 'dl_w_out', 'loss_target', 'm_norm_g', 'm_final_g', 'm_rel_bias', 'm_hgrn_lb', 'm_ssd_w_in', 'm_ssd_conv_w', 'm_ssd_conv_b', 'm_ssd_dt_bias', 'm_ssd_a_log', 'm_ssd_d', 'm_ssd_norm_g', 'm_ssd_w_out', 'm_hg_w_in', 'm_hg_norm_g', 'm_hg_w_out', 'm_at_w_in', 'm_at_q_norm_g', 'm_at_k_norm_g', 'm_at_w_out', 'm_dl_w_in', 'm_dl_w_out', 'v_norm_g', 'v_final_g', 'v_rel_bias', 'v_hgrn_lb', 'v_ssd_w_in', 'v_ssd_conv_w', 'v_ssd_conv_b', 'v_ssd_dt_bias', 'v_ssd_a_log', 'v_ssd_d', 'v_ssd_norm_g', 'v_ssd_w_out', 'v_hg_w_in', 'v_hg_norm_g', 'v_hg_w_out', 'v_at_w_in', 'v_at_q_norm_g', 'v_at_k_norm_g', 'v_at_w_out', 'v_dl_w_in', 'v_dl_w_out']
TWIN_OUTPUTS = ['loss', 'grad_x', 'grad_norm_g', 'grad_final_g', 'grad_rel_bias', 'grad_hgrn_lb', 'grad_ssd_w_in', 'grad_ssd_conv_w', 'grad_ssd_conv_b', 'grad_ssd_dt_bias', 'grad_ssd_a_log', 'grad_ssd_d', 'grad_ssd_norm_g', 'grad_ssd_w_out', 'grad_hg_w_in', 'grad_hg_norm_g', 'grad_hg_w_out', 'grad_at_w_in', 'grad_at_q_norm_g', 'grad_at_k_norm_g', 'grad_at_w_out', 'grad_dl_w_in', 'grad_dl_w_out', 'delta_norm_g', 'delta_final_g', 'delta_rel_bias', 'delta_hgrn_lb', 'delta_ssd_w_in', 'delta_ssd_conv_w', 'delta_ssd_conv_b', 'delta_ssd_dt_bias', 'delta_ssd_a_log', 'delta_ssd_d', 'delta_ssd_norm_g', 'delta_ssd_w_out', 'delta_hg_w_in', 'delta_hg_norm_g', 'delta_hg_w_out', 'delta_at_w_in', 'delta_at_q_norm_g', 'delta_at_k_norm_g', 'delta_at_w_out', 'delta_dl_w_in', 'delta_dl_w_out', 'new_m_norm_g', 'new_m_final_g', 'new_m_rel_bias', 'new_m_hgrn_lb', 'new_m_ssd_w_in', 'new_m_ssd_conv_w', 'new_m_ssd_conv_b', 'new_m_ssd_dt_bias', 'new_m_ssd_a_log', 'new_m_ssd_d', 'new_m_ssd_norm_g', 'new_m_ssd_w_out', 'new_m_hg_w_in', 'new_m_hg_norm_g', 'new_m_hg_w_out', 'new_m_at_w_in', 'new_m_at_q_norm_g', 'new_m_at_k_norm_g', 'new_m_at_w_out', 'new_m_dl_w_in', 'new_m_dl_w_out', 'new_v_norm_g', 'new_v_final_g', 'new_v_rel_bias', 'new_v_hgrn_lb', 'new_v_ssd_w_in', 'new_v_ssd_conv_w', 'new_v_ssd_conv_b', 'new_v_ssd_dt_bias', 'new_v_ssd_a_log', 'new_v_ssd_d', 'new_v_ssd_norm_g', 'new_v_ssd_w_out', 'new_v_hg_w_in', 'new_v_hg_norm_g', 'new_v_hg_w_out', 'new_v_at_w_in', 'new_v_at_q_norm_g', 'new_v_at_k_norm_g', 'new_v_at_w_out', 'new_v_dl_w_in', 'new_v_dl_w_out']
TWIN_LEAF_KINDS = {'loss': 'loss', 'grad_x': 'grad_x', 'grad_norm_g': 'grad_w', 'grad_final_g': 'grad_w', 'grad_rel_bias': 'grad_w', 'grad_hgrn_lb': 'grad_w', 'grad_ssd_w_in': 'grad_w', 'grad_ssd_conv_w': 'grad_w', 'grad_ssd_conv_b': 'grad_w', 'grad_ssd_dt_bias': 'grad_w', 'grad_ssd_a_log': 'grad_w', 'grad_ssd_d': 'grad_w', 'grad_ssd_norm_g': 'grad_w', 'grad_ssd_w_out': 'grad_w', 'grad_hg_w_in': 'grad_w', 'grad_hg_norm_g': 'grad_w', 'grad_hg_w_out': 'grad_w', 'grad_at_w_in': 'grad_w', 'grad_at_q_norm_g': 'grad_w', 'grad_at_k_norm_g': 'grad_w', 'grad_at_w_out': 'grad_w', 'grad_dl_w_in': 'grad_w', 'grad_dl_w_out': 'grad_w', 'delta_norm_g': 'delta_w', 'delta_final_g': 'delta_w', 'delta_rel_bias': 'delta_w', 'delta_hgrn_lb': 'delta_w', 'delta_ssd_w_in': 'delta_w', 'delta_ssd_conv_w': 'delta_w', 'delta_ssd_conv_b': 'delta_w', 'delta_ssd_dt_bias': 'delta_w', 'delta_ssd_a_log': 'delta_w', 'delta_ssd_d': 'delta_w', 'delta_ssd_norm_g': 'delta_w', 'delta_ssd_w_out': 'delta_w', 'delta_hg_w_in': 'delta_w', 'delta_hg_norm_g': 'delta_w', 'delta_hg_w_out': 'delta_w', 'delta_at_w_in': 'delta_w', 'delta_at_q_norm_g': 'delta_w', 'delta_at_k_norm_g': 'delta_w', 'delta_at_w_out': 'delta_w', 'delta_dl_w_in': 'delta_w', 'delta_dl_w_out': 'delta_w', 'new_m_norm_g': 'new_m', 'new_m_final_g': 'new_m', 'new_m_rel_bias': 'new_m', 'new_m_hgrn_lb': 'new_m', 'new_m_ssd_w_in': 'new_m', 'new_m_ssd_conv_w': 'new_m', 'new_m_ssd_conv_b': 'new_m', 'new_m_ssd_dt_bias': 'new_m', 'new_m_ssd_a_log': 'new_m', 'new_m_ssd_d': 'new_m', 'new_m_ssd_norm_g': 'new_m', 'new_m_ssd_w_out': 'new_m', 'new_m_hg_w_in': 'new_m', 'new_m_hg_norm_g': 'new_m', 'new_m_hg_w_out': 'new_m', 'new_m_at_w_in': 'new_m', 'new_m_at_q_norm_g': 'new_m', 'new_m_at_k_norm_g': 'new_m', 'new_m_at_w_out': 'new_m', 'new_m_dl_w_in': 'new_m', 'new_m_dl_w_out': 'new_m', 'new_v_norm_g': 'new_v', 'new_v_final_g': 'new_v', 'new_v_rel_bias': 'new_v', 'new_v_hgrn_lb': 'new_v', 'new_v_ssd_w_in': 'new_v', 'new_v_ssd_conv_w': 'new_v', 'new_v_ssd_conv_b': 'new_v', 'new_v_ssd_dt_bias': 'new_v', 'new_v_ssd_a_log': 'new_v', 'new_v_ssd_d': 'new_v', 'new_v_ssd_norm_g': 'new_v', 'new_v_ssd_w_out': 'new_v', 'new_v_hg_w_in': 'new_v', 'new_v_hg_norm_g': 'new_v', 'new_v_hg_w_out': 'new_v', 'new_v_at_w_in': 'new_v', 'new_v_at_q_norm_g': 'new_v', 'new_v_at_k_norm_g': 'new_v', 'new_v_at_w_out': 'new_v', 'new_v_dl_w_in': 'new_v', 'new_v_dl_w_out': 'new_v'}


def _forward(args):
    return _fwd_reference(*[args[k] for k in FWD_PARAMS])


def _output_shape():
    def fwd():
        inp = _fwd_setup_inputs(0)
        return _fwd_reference(*[inp[k] for k in FWD_PARAMS])
    out = _jax.eval_shape(fwd)
    return out.shape, out.dtype

N_MICROBATCH = 1
ADAM_LR = 0.001
ADAM_B1 = 0.9
ADAM_B2 = 0.999
ADAM_EPS = 1e-08
ADAM_WD = 0.01
ADAM_STEP = 10
PER_EXAMPLE_BATCH_AXIS = {'x': 0, 'loss_target': 0}
SHARED_INPUTS = []
_WEIGHT_DTYPES = {'norm_g': _jnp.float32, 'final_g': _jnp.float32, 'rel_bias': _jnp.float32, 'hgrn_lb': _jnp.float32, 'ssd_w_in': _jnp.float32, 'ssd_conv_w': _jnp.float32, 'ssd_conv_b': _jnp.float32, 'ssd_dt_bias': _jnp.float32, 'ssd_a_log': _jnp.float32, 'ssd_d': _jnp.float32, 'ssd_norm_g': _jnp.float32, 'ssd_w_out': _jnp.float32, 'hg_w_in': _jnp.float32, 'hg_norm_g': _jnp.float32, 'hg_w_out': _jnp.float32, 'at_w_in': _jnp.float32, 'at_q_norm_g': _jnp.float32, 'at_k_norm_g': _jnp.float32, 'at_w_out': _jnp.float32, 'dl_w_in': _jnp.float32, 'dl_w_out': _jnp.float32}
MOMENT_SCALE = {'norm_g': 1.147006e-01, 'final_g': 3.205105e+01, 'rel_bias': 1.031477e-02, 'hgrn_lb': 1.999636e-03, 'ssd_w_in': 8.906407e-02, 'ssd_conv_w': 8.025719e-02, 'ssd_conv_b': 1.714453e-01, 'ssd_dt_bias': 1.803151e-01, 'ssd_a_log': 2.247361e-01, 'ssd_d': 4.414051e-01, 'ssd_norm_g': 1.100004e-01, 'ssd_w_out': 1.532741e-01, 'hg_w_in': 4.604825e-02, 'hg_norm_g': 7.572005e-02, 'hg_w_out': 7.393779e-02, 'at_w_in': 8.795568e-03, 'at_q_norm_g': 1.982187e-02, 'at_k_norm_g': 1.963371e-02, 'at_w_out': 1.389991e-02, 'dl_w_in': 6.542713e-03, 'dl_w_out': 1.182906e-02}


def _to_microbatches(a, axis):
    t = _jnp.moveaxis(a, axis, 0)
    t = t.reshape((N_MICROBATCH, t.shape[0] // N_MICROBATCH) + t.shape[1:])
    return _jnp.moveaxis(t, 1, axis + 1)


def setup_inputs(seed: int = 0) -> dict:
    inp = _fwd_setup_inputs(seed)
    key = _jax.random.fold_in(_jax.random.key(seed), 7919)
    shape, _ = _output_shape()
    out = dict(inp)
    out["loss_target"] = _jax.random.normal(_jax.random.fold_in(key, 0), shape, _jnp.float32)
    for i, name in enumerate(TWIN_WEIGHTS):
        w = inp[name].astype(_jnp.float32)
        if MOMENT_SCALE is None:
            s = _jnp.sqrt(_jnp.mean(_jnp.square(w)) + 1e-30)
        else:
            s = MOMENT_SCALE[name]
        km, kv = _jax.random.split(_jax.random.fold_in(key, i + 1))
        out[name] = w
        out["m_" + name] = s * _jax.random.normal(km, w.shape, _jnp.float32)
        out["v_" + name] = (s * s) * _jax.random.uniform(kv, w.shape, _jnp.float32, 0.5, 1.5)
    if N_MICROBATCH > 1:
        for name, axis in PER_EXAMPLE_BATCH_AXIS.items():
            out[name] = _to_microbatches(out[name], axis)
    return {'x': out['x'], 'norm_g': out['norm_g'], 'final_g': out['final_g'], 'rel_bias': out['rel_bias'], 'hgrn_lb': out['hgrn_lb'], 'ssd_w_in': out['ssd_w_in'], 'ssd_conv_w': out['ssd_conv_w'], 'ssd_conv_b': out['ssd_conv_b'], 'ssd_dt_bias': out['ssd_dt_bias'], 'ssd_a_log': out['ssd_a_log'], 'ssd_d': out['ssd_d'], 'ssd_norm_g': out['ssd_norm_g'], 'ssd_w_out': out['ssd_w_out'], 'hg_w_in': out['hg_w_in'], 'hg_norm_g': out['hg_norm_g'], 'hg_w_out': out['hg_w_out'], 'at_w_in': out['at_w_in'], 'at_q_norm_g': out['at_q_norm_g'], 'at_k_norm_g': out['at_k_norm_g'], 'at_w_out': out['at_w_out'], 'dl_w_in': out['dl_w_in'], 'dl_w_out': out['dl_w_out'], 'loss_target': out['loss_target'], 'm_norm_g': out['m_norm_g'], 'm_final_g': out['m_final_g'], 'm_rel_bias': out['m_rel_bias'], 'm_hgrn_lb': out['m_hgrn_lb'], 'm_ssd_w_in': out['m_ssd_w_in'], 'm_ssd_conv_w': out['m_ssd_conv_w'], 'm_ssd_conv_b': out['m_ssd_conv_b'], 'm_ssd_dt_bias': out['m_ssd_dt_bias'], 'm_ssd_a_log': out['m_ssd_a_log'], 'm_ssd_d': out['m_ssd_d'], 'm_ssd_norm_g': out['m_ssd_norm_g'], 'm_ssd_w_out': out['m_ssd_w_out'], 'm_hg_w_in': out['m_hg_w_in'], 'm_hg_norm_g': out['m_hg_norm_g'], 'm_hg_w_out': out['m_hg_w_out'], 'm_at_w_in': out['m_at_w_in'], 'm_at_q_norm_g': out['m_at_q_norm_g'], 'm_at_k_norm_g': out['m_at_k_norm_g'], 'm_at_w_out': out['m_at_w_out'], 'm_dl_w_in': out['m_dl_w_in'], 'm_dl_w_out': out['m_dl_w_out'], 'v_norm_g': out['v_norm_g'], 'v_final_g': out['v_final_g'], 'v_rel_bias': out['v_rel_bias'], 'v_hgrn_lb': out['v_hgrn_lb'], 'v_ssd_w_in': out['v_ssd_w_in'], 'v_ssd_conv_w': out['v_ssd_conv_w'], 'v_ssd_conv_b': out['v_ssd_conv_b'], 'v_ssd_dt_bias': out['v_ssd_dt_bias'], 'v_ssd_a_log': out['v_ssd_a_log'], 'v_ssd_d': out['v_ssd_d'], 'v_ssd_norm_g': out['v_ssd_norm_g'], 'v_ssd_w_out': out['v_ssd_w_out'], 'v_hg_w_in': out['v_hg_w_in'], 'v_hg_norm_g': out['v_hg_norm_g'], 'v_hg_w_out': out['v_hg_w_out'], 'v_at_w_in': out['v_at_w_in'], 'v_at_q_norm_g': out['v_at_q_norm_g'], 'v_at_k_norm_g': out['v_at_k_norm_g'], 'v_at_w_out': out['v_at_w_out'], 'v_dl_w_in': out['v_dl_w_in'], 'v_dl_w_out': out['v_dl_w_out']}


def _loss(weights, diff, rest, loss_target):
    with _jax.named_scope("forward"):
        args = {**rest, TWIN_DIFF_INPUT: diff, **{k: w.astype(_WEIGHT_DTYPES[k]) for k, w in weights.items()}}
        y = _forward(args)
    with _jax.named_scope("loss_head"):
        err = _jnp.square(y.astype(_jnp.float32) - loss_target)
        return 0.5 * _jnp.sum(_jnp.mean(err, axis=-1)) if err.ndim else 0.5 * err


def _adamw(w, g, m, v):
    m = ADAM_B1 * m + (1.0 - ADAM_B1) * g
    v = ADAM_B2 * v + (1.0 - ADAM_B2) * _jnp.square(g)
    m_hat = m / (1.0 - ADAM_B1 ** ADAM_STEP)
    v_hat = v / (1.0 - ADAM_B2 ** ADAM_STEP)
    delta = -ADAM_LR * (m_hat / (_jnp.sqrt(v_hat) + ADAM_EPS) + ADAM_WD * w)
    return delta, m, v


def reference(x, norm_g, final_g, rel_bias, hgrn_lb, ssd_w_in, ssd_conv_w, ssd_conv_b, ssd_dt_bias, ssd_a_log, ssd_d, ssd_norm_g, ssd_w_out, hg_w_in, hg_norm_g, hg_w_out, at_w_in, at_q_norm_g, at_k_norm_g, at_w_out, dl_w_in, dl_w_out, loss_target, m_norm_g, m_final_g, m_rel_bias, m_hgrn_lb, m_ssd_w_in, m_ssd_conv_w, m_ssd_conv_b, m_ssd_dt_bias, m_ssd_a_log, m_ssd_d, m_ssd_norm_g, m_ssd_w_out, m_hg_w_in, m_hg_norm_g, m_hg_w_out, m_at_w_in, m_at_q_norm_g, m_at_k_norm_g, m_at_w_out, m_dl_w_in, m_dl_w_out, v_norm_g, v_final_g, v_rel_bias, v_hgrn_lb, v_ssd_w_in, v_ssd_conv_w, v_ssd_conv_b, v_ssd_dt_bias, v_ssd_a_log, v_ssd_d, v_ssd_norm_g, v_ssd_w_out, v_hg_w_in, v_hg_norm_g, v_hg_w_out, v_at_w_in, v_at_q_norm_g, v_at_k_norm_g, v_at_w_out, v_dl_w_in, v_dl_w_out):
    given = dict(x=x, norm_g=norm_g, final_g=final_g, rel_bias=rel_bias, hgrn_lb=hgrn_lb, ssd_w_in=ssd_w_in, ssd_conv_w=ssd_conv_w, ssd_conv_b=ssd_conv_b, ssd_dt_bias=ssd_dt_bias, ssd_a_log=ssd_a_log, ssd_d=ssd_d, ssd_norm_g=ssd_norm_g, ssd_w_out=ssd_w_out, hg_w_in=hg_w_in, hg_norm_g=hg_norm_g, hg_w_out=hg_w_out, at_w_in=at_w_in, at_q_norm_g=at_q_norm_g, at_k_norm_g=at_k_norm_g, at_w_out=at_w_out, dl_w_in=dl_w_in, dl_w_out=dl_w_out, loss_target=loss_target, m_norm_g=m_norm_g, m_final_g=m_final_g, m_rel_bias=m_rel_bias, m_hgrn_lb=m_hgrn_lb, m_ssd_w_in=m_ssd_w_in, m_ssd_conv_w=m_ssd_conv_w, m_ssd_conv_b=m_ssd_conv_b, m_ssd_dt_bias=m_ssd_dt_bias, m_ssd_a_log=m_ssd_a_log, m_ssd_d=m_ssd_d, m_ssd_norm_g=m_ssd_norm_g, m_ssd_w_out=m_ssd_w_out, m_hg_w_in=m_hg_w_in, m_hg_norm_g=m_hg_norm_g, m_hg_w_out=m_hg_w_out, m_at_w_in=m_at_w_in, m_at_q_norm_g=m_at_q_norm_g, m_at_k_norm_g=m_at_k_norm_g, m_at_w_out=m_at_w_out, m_dl_w_in=m_dl_w_in, m_dl_w_out=m_dl_w_out, v_norm_g=v_norm_g, v_final_g=v_final_g, v_rel_bias=v_rel_bias, v_hgrn_lb=v_hgrn_lb, v_ssd_w_in=v_ssd_w_in, v_ssd_conv_w=v_ssd_conv_w, v_ssd_conv_b=v_ssd_conv_b, v_ssd_dt_bias=v_ssd_dt_bias, v_ssd_a_log=v_ssd_a_log, v_ssd_d=v_ssd_d, v_ssd_norm_g=v_ssd_norm_g, v_ssd_w_out=v_ssd_w_out, v_hg_w_in=v_hg_w_in, v_hg_norm_g=v_hg_norm_g, v_hg_w_out=v_hg_w_out, v_at_w_in=v_at_w_in, v_at_q_norm_g=v_at_q_norm_g, v_at_k_norm_g=v_at_k_norm_g, v_at_w_out=v_at_w_out, v_dl_w_in=v_dl_w_in, v_dl_w_out=v_dl_w_out)
    weights = {n: given[n] for n in TWIN_WEIGHTS}
    shared = {n: given[n] for n in SHARED_INPUTS}
    per_example = {n: given[n] for n in ['x']}
    grad_fn = _jax.value_and_grad(_loss, argnums=(0, 1))

    def one_microbatch(ex, loss_target):
        ex = dict(ex)
        diff = ex.pop(TWIN_DIFF_INPUT)
        return grad_fn(weights, diff, {**shared, **ex}, loss_target)

    if N_MICROBATCH == 1:
        loss, (grad_w, grad_x) = one_microbatch(per_example, given["loss_target"])
    else:
        def body(carry, xs):
            loss_sum, grad_sum = carry
            l_k, (gw_k, gx_k) = one_microbatch(xs[0], xs[1])
            with _jax.named_scope("update"):
                return (loss_sum + l_k, _jax.tree.map(_jnp.add, grad_sum, gw_k)), gx_k

        init = (_jnp.zeros((), _jnp.float32), _jax.tree.map(_jnp.zeros_like, weights))
        (loss, grad_w), grad_x = _jax.lax.scan(body, init, (per_example, given["loss_target"]))
    with _jax.named_scope("update"):
        delta_w, new_m, new_v = {}, {}, {}
        for n in TWIN_WEIGHTS:
            delta_w[n], new_m[n], new_v[n] = _adamw(weights[n], grad_w[n], given["m_" + n], given["v_" + n])
    return (loss, grad_x, *[grad_w[n] for n in TWIN_WEIGHTS], *[delta_w[n] for n in TWIN_WEIGHTS],
            *[new_m[n] for n in TWIN_WEIGHTS], *[new_v[n] for n in TWIN_WEIGHTS])
```

```python
import functools
import math

import numpy as np
import jax
import jax.numpy as jnp
from jax import lax
from jax.experimental import pallas as pl
from jax.experimental.pallas import tpu as pltpu

F32 = jnp.float32
BF16 = jnp.bfloat16
MESH = pl.DeviceIdType.MESH

D_MODEL = 1024
DEPTH = 4
GRID_W = 64
EPS = 1e-6
NEG_BIG = -1e30

SSD_DI = 2048
SSD_HEADDIM = 64
SSD_HEADS = 32
SSD_GROUPS = 4
SSD_HPG = 8
SSD_STATE = 128
SSD_CONV = 7
SSD_CHUNK = 128
SSD_CONV_CH = SSD_DI + 2 * SSD_GROUPS * SSD_STATE
SSD_MAIN = SSD_DI + SSD_CONV_CH
SSD_IN = SSD_MAIN + 2 * SSD_HEADS

HG_HEADS = 8
HG_EXPAND = 128
HG_W = 1024
HG_CHUNK = 32
HG_ROWS = 128
HG_IN = 5 * HG_W

AT_HEADS = 16
AT_KV = 8
AT_GRP = 2
AT_HD = 128
ROPE_THETA = 10000.0
ROPE_AXIS = 64
AT_QW = AT_HEADS * AT_HD
AT_KW = AT_KV * AT_HD
AT_IN = 2 * AT_QW + 2 * AT_KW

DL_PAIRS = ((128, 1), (512, 4), (2048, 16))
DL_HEADS = 16
DL_HD = 64
DL_W = 1024
DL_STEPS = 64
DL_IN = 10 * DL_W
REL_BUCKETS = 32
REL_MAX_DIST = 1024

ADAM_LR = 0.001
ADAM_B1 = 0.9
ADAM_B2 = 0.999
ADAM_EPS = 1e-08
ADAM_WD = 0.01
ADAM_STEP = 10

VMEM_LIMIT = 56 * 1024 * 1024
LANE = 128
SUBLANE = 8


def _cp(sem=None):
    return pltpu.CompilerParams(dimension_semantics=sem, vmem_limit_bytes=VMEM_LIMIT)


def _mm(a, b):
    return jnp.dot(a.astype(BF16), b.astype(BF16), preferred_element_type=F32)


def _mm_nt(a, b):
    return lax.dot_general(a.astype(BF16), b.astype(BF16), (((1,), (1,)), ((), ())), preferred_element_type=F32)


def _mm_tn(a, b):
    return lax.dot_general(a.astype(BF16), b.astype(BF16), (((0,), (0,)), ((), ())), preferred_element_type=F32)


def _mm_exact(a, b):
    return jnp.dot(a, b, preferred_element_type=F32, precision=lax.Precision.HIGHEST)


def _mm_nt_exact(a, b):
    return lax.dot_general(a, b, (((1,), (1,)), ((), ())), preferred_element_type=F32,
                           precision=lax.Precision.HIGHEST)


def _silu(x):
    return x * jax.nn.sigmoid(x)


def _softplus(z):
    return jnp.maximum(z, 0.0) + jnp.log(1.0 + jnp.exp(-jnp.abs(z)))


def _pick(dim, pref):
    best = None
    t = LANE
    while t <= min(dim, pref):
        if dim % t == 0:
            best = t
        t += LANE
    return best if best is not None else dim


def _const_map(n):
    return lambda *_: (0,) * n


def matmul(name, a, b, mode="nn", res=None, out_dtype=F32, tm=512, tn=1024, tk=1024):
    if mode == "tn":
        K, M = a.shape
    else:
        M, K = a.shape
    N = b.shape[0] if mode == "nt" else b.shape[1]
    tm, tn, tk = _pick(M, tm), _pick(N, tn), _pick(K, tk)
    nk = K // tk
    a_spec = (pl.BlockSpec((tk, tm), lambda i, j, k: (k, i)) if mode == "tn"
              else pl.BlockSpec((tm, tk), lambda i, j, k: (i, k)))
    b_spec = (pl.BlockSpec((tn, tk), lambda i, j, k: (j, k)) if mode == "nt"
              else pl.BlockSpec((tk, tn), lambda i, j, k: (k, j)))
    dot = {"nn": _mm, "nt": _mm_nt, "tn": _mm_tn}[mode]
    has_res = res is not None

    def body(*refs):
        if has_res:
            a_ref, b_ref, r_ref, o_ref, acc = refs
        else:
            a_ref, b_ref, o_ref, acc = refs
        k = pl.program_id(2)

        @pl.when(k == 0)
        def _():
            acc[...] = jnp.zeros_like(acc)

        acc[...] += dot(a_ref[...], b_ref[...])

        @pl.when(k == nk - 1)
        def _():
            out = acc[...]
            if has_res:
                out = out + r_ref[...].astype(F32)
            o_ref[...] = out.astype(o_ref.dtype)

    in_specs = [a_spec, b_spec]
    args = [a, b]
    if has_res:
        in_specs.append(pl.BlockSpec((tm, tn), lambda i, j, k: (i, j)))
        args.append(res)
    return pl.pallas_call(
        body, name=name, grid=(M // tm, N // tn, nk), in_specs=in_specs,
        out_specs=pl.BlockSpec((tm, tn), lambda i, j, k: (i, j)),
        out_shape=jax.ShapeDtypeStruct((M, N), out_dtype),
        scratch_shapes=[pltpu.VMEM((tm, tn), F32)],
        compiler_params=_cp(("parallel", "parallel", "arbitrary")),
    )(*args)


def _col(arr, width, idx):
    return (arr, width, idx)


def rowcall(name, fn, rows, bcs, row_outs, bc_outs=(), tb=256, halo=()):
    rows = [r if isinstance(r, tuple) else (r, r.shape[1], 0) for r in rows]
    S = rows[0][0].shape[0]
    tb = min(tb, S)
    nb = S // tb
    n_r, n_h, n_b, n_ro, n_bo = len(rows), len(halo), len(bcs), len(row_outs), len(bc_outs)
    hb = tb // SUBLANE
    last = S // SUBLANE - 1

    def body(*refs):
        i = pl.program_id(0)
        pos = 0
        r_in = [r[...] for r in refs[pos:pos + n_r]]
        pos += n_r
        h_in = []
        for _ in range(n_h):
            prev = refs[pos][...] * (i > 0).astype(F32)
            nxt = refs[pos + 1][...] * (i < nb - 1).astype(F32)
            h_in += [prev, nxt]
            pos += 2
        b_in = [r[...] for r in refs[pos:pos + n_b]]
        pos += n_b
        ro = refs[pos:pos + n_ro]
        bo = refs[pos + n_ro:]
        outs_r, outs_b = fn(*r_in, *h_in, *b_in)
        for ref, val in zip(ro, outs_r, strict=True):
            ref[...] = val.astype(ref.dtype)
        if n_bo:
            @pl.when(i == 0)
            def _():
                for ref in bo:
                    ref[...] = jnp.zeros_like(ref)

            for ref, val in zip(bo, outs_b, strict=True):
                ref[...] += val

    in_specs = [pl.BlockSpec((tb, w), functools.partial(lambda i, c: (i, c), c=c)) for (_, w, c) in rows]
    args = [a for (a, _, _) in rows]
    for h in halo:
        a, w, c = rows[h]
        in_specs.append(pl.BlockSpec((SUBLANE, w), functools.partial(
            lambda i, c: (jnp.maximum(i * hb - 1, 0), c), c=c)))
        in_specs.append(pl.BlockSpec((SUBLANE, w), functools.partial(
            lambda i, c: (jnp.minimum((i + 1) * hb, last), c), c=c)))
        args += [a, a]
    for b in bcs:
        in_specs.append(pl.BlockSpec(b.shape, _const_map(b.ndim)))
        args.append(b)
    out_specs = [pl.BlockSpec((tb, w), lambda i: (i, 0)) for (w, _) in row_outs]
    out_shape = [jax.ShapeDtypeStruct((S, w), dt) for (w, dt) in row_outs]
    for shp in bc_outs:
        out_specs.append(pl.BlockSpec(shp, _const_map(len(shp))))
        out_shape.append(jax.ShapeDtypeStruct(shp, F32))
    outs = pl.pallas_call(
        body, name=name, grid=(nb,), in_specs=in_specs, out_specs=out_specs, out_shape=out_shape,
        compiler_params=_cp(("arbitrary",) if n_bo else ("parallel",)),
    )(*args)
    return list(outs[:n_ro]), list(outs[n_ro:])


def smallcall(name, fn, ins, out_shapes):
    n_in = len(ins)

    def body(*refs):
        outs = fn(*[r[...] for r in refs[:n_in]])
        for ref, val in zip(refs[n_in:], outs, strict=True):
            ref[...] = val.astype(ref.dtype)

    return pl.pallas_call(
        body, name=name, out_shape=[jax.ShapeDtypeStruct(s, F32) for s in out_shapes],
        compiler_params=_cp(),
    )(*ins)


def _rms(x, g):
    return x * lax.rsqrt(jnp.mean(x * x, axis=-1, keepdims=True) + EPS) * g


def _rms_groups(y, g, width):
    outs = []
    for j in range(y.shape[1] // width):
        sl = slice(j * width, (j + 1) * width)
        outs.append(_rms(y[:, sl], g[:, sl]))
    return jnp.concatenate(outs, axis=1)


def norm_fwd(name, x, g):
    (hn,), _ = rowcall(name, lambda x, g: ((_rms(x, g),), ()), [x], [g], [(D_MODEL, BF16)], tb=512)
    return hn


def norm_bwd(name, x, g, dhn, dres):
    def fn(x, dhn, dres, g):
        _, vjp = jax.vjp(_rms, x, g)
        dx, dg = vjp(dhn)
        return (dx + dres,), (dg,)

    (dx,), (dg,) = rowcall(name, fn, [x, dhn, dres], [g], [(D_MODEL, F32)], [(1, D_MODEL)], tb=512)
    return dx, dg


def loss_head(x, tgt, g):
    def fn(x, tgt, g):
        y, vjp = jax.vjp(_rms, x, g)
        diff = y - tgt
        loss = 0.5 * jnp.sum(jnp.mean(diff * diff, axis=-1, keepdims=True), axis=0, keepdims=True)
        dx, dg = vjp(diff * (1.0 / D_MODEL))
        return (dx,), (jnp.broadcast_to(loss, (1, LANE)), dg)

    (dx,), (loss, dg) = rowcall("loss_head", fn, [x, tgt], [g], [(D_MODEL, F32)],
                                [(1, LANE), (1, D_MODEL)], tb=512)
    return loss, dx, dg


def _shift_rows(x, s):
    if s == 0:
        return x
    return pltpu.roll(x, (-s) % x.shape[0], 0)


def _conv_ext(x, prev, nxt, w):
    xe = jnp.concatenate([prev, x, nxt], axis=0)
    pad = SSD_CONV // 2
    c = jnp.zeros_like(xe)
    for k in range(SSD_CONV):
        c = c + w[k:k + 1, :] * _shift_rows(xe, k - pad)
    return xe, c


def ssd_conv_fwd(u, conv_w, conv_b):
    def fn(x0, x1, x2, p0, n0, p1, n1, p2, n2, w, b):
        tb = x0.shape[0]
        outs = []
        for j, (x, p, n) in enumerate(((x0, p0, n0), (x1, p1, n1), (x2, p2, n2))):
            sl = slice(j * 1024, (j + 1) * 1024)
            _, c = _conv_ext(x, p, n, w[:, sl])
            outs.append(_silu(c[SUBLANE:SUBLANE + tb] + b[:, sl]))
        return (jnp.concatenate(outs, axis=1),), ()

    (xbc,), _ = rowcall("ssd_conv_fwd", fn, [_col(u, 1024, 2), _col(u, 1024, 3), _col(u, 1024, 4)],
                        [conv_w, conv_b], [(SSD_CONV_CH, F32)], tb=256, halo=(0, 1, 2))
    return xbc


def ssd_conv_bwd(u, dxbc, dz, conv_w, conv_b):
    pad = SSD_CONV // 2

    def fn(x0, x1, x2, g0, g1, g2, dz, xp0, xn0, xp1, xn1, xp2, xn2, gp0, gn0, gp1, gn1, gp2, gn2, w, b):
        tb = x0.shape[0]
        blk = slice(SUBLANE, SUBLANE + tb)
        dpre, dws, dbs = [], [], []
        xs = ((x0, xp0, xn0), (x1, xp1, xn1), (x2, xp2, xn2))
        gs = ((g0, gp0, gn0), (g1, gp1, gn1), (g2, gp2, gn2))
        for j in range(3):
            sl = slice(j * 1024, (j + 1) * 1024)
            wj = w[:, sl]
            xe, c = _conv_ext(*xs[j], wj)
            ce = c + b[:, sl]
            sig = jax.nn.sigmoid(ce)
            ge = jnp.concatenate([gs[j][1], gs[j][0], gs[j][2]], axis=0)
            dce = ge * (sig * (1.0 + ce * (1.0 - sig)))
            dx = jnp.zeros_like(xe)
            dw_rows = []
            for k in range(SSD_CONV):
                dx = dx + wj[k:k + 1, :] * _shift_rows(dce, pad - k)
                dw_rows.append(jnp.sum(dce[blk] * _shift_rows(xe, k - pad)[blk], axis=0, keepdims=True))
            dw_rows.append(jnp.zeros_like(dw_rows[0]))
            dpre.append(dx[blk])
            dws.append(jnp.concatenate(dw_rows, axis=0))
            dbs.append(jnp.sum(dce[blk], axis=0, keepdims=True))
        du = jnp.concatenate([dz] + dpre, axis=1)
        return (du,), (jnp.concatenate(dws, axis=1), jnp.concatenate(dbs, axis=1))

    rows = [_col(u, 1024, 2), _col(u, 1024, 3), _col(u, 1024, 4),
            _col(dxbc, 1024, 0), _col(dxbc, 1024, 1), _col(dxbc, 1024, 2), dz]
    (du,), (dw, db) = rowcall("ssd_conv_bwd", fn, rows, [conv_w, conv_b], [(SSD_MAIN, BF16)],
                              [(SUBLANE, SSD_CONV_CH), (1, SSD_CONV_CH)], tb=128, halo=(0, 1, 2, 3, 4, 5))
    return du, dw, db


def _ssd_chunk(rev, st_in, xs, udt, dtb, alog, B, C):
    Q = B.shape[0]
    dt = _softplus(udt + dtb)
    a = dt * (-jnp.exp(alog))
    r = lax.broadcasted_iota(jnp.int32, (Q, Q), 0)
    c = lax.broadcasted_iota(jnp.int32, (Q, Q), 1)
    mask = (r <= c) if rev else (r >= c)
    p = _mm_exact(mask.astype(F32), a)
    pT = p.T
    p_tot = p[0:1] if rev else p[Q - 1:Q]
    e_off = jnp.exp(p)
    e_dec = jnp.exp(p_tot - p)
    e_tot = jnp.exp(p_tot)
    CB = _mm_nt(C, B)
    ys, st_out = [], []
    for j in range(SSD_HPG):
        L = jnp.exp(jnp.where(mask, p[:, j:j + 1] - pT[j:j + 1, :], NEG_BIG))
        xdt = xs[j] * dt[:, j:j + 1]
        ys.append(_mm(CB * L, xdt) + _mm(C, st_in[j]) * e_off[:, j:j + 1])
        st_out.append(st_in[j] * e_tot[:, j:j + 1] + _mm_tn(B * e_dec[:, j:j + 1], xdt))
    return ys, st_out


def _ssd_specs(nc, rev_order):
    Q = SSD_CHUNK
    N, P, H = SSD_STATE, SSD_HEADDIM, SSD_HPG
    gw = H * P

    def cidx(s):
        return nc - 1 - s if rev_order else s

    xs = pl.BlockSpec((Q, gw), lambda g, s: (cidx(s), g))
    Bs = pl.BlockSpec((Q, N), lambda g, s: (cidx(s), SSD_DI // N + g))
    Cs = pl.BlockSpec((Q, N), lambda g, s: (cidx(s), SSD_DI // N + SSD_GROUPS + g))
    udt = pl.BlockSpec((1, Q, H), lambda g, s: (g, cidx(s), 0))
    small = pl.BlockSpec((1, 1, H), lambda g, s: (g, 0, 0))
    st = pl.BlockSpec((1, 1, H, N, P), lambda g, s: (g, cidx(s), 0, 0, 0))
    return xs, Bs, Cs, udt, small, st


def ssd_scan_fwd(name, xbc, udt, dtb, alog, rev):
    S = xbc.shape[0]
    Q, N, P, H = SSD_CHUNK, SSD_STATE, SSD_HEADDIM, SSD_HPG
    nc = S // Q
    xs_s, B_s, C_s, udt_s, small_s, st_s = _ssd_specs(nc, rev)

    def body(xs_ref, B_ref, C_ref, udt_ref, dtb_ref, alog_ref, y_ref, st_ref, state):
        @pl.when(pl.program_id(1) == 0)
        def _():
            state[...] = jnp.zeros_like(state)

        st_ref[0, 0] = state[...]
        st_in = [state[j] for j in range(H)]
        xs = [xs_ref[:, j * P:(j + 1) * P] for j in range(H)]
        ys, st_out = _ssd_chunk(rev, st_in, xs, udt_ref[0], dtb_ref[0], alog_ref[0], B_ref[...], C_ref[...])
        y_ref[...] = jnp.concatenate(ys, axis=1)
        for j in range(H):
            state[j] = st_out[j]

    return pl.pallas_call(
        body, name=name, grid=(SSD_GROUPS, nc),
        in_specs=[xs_s, B_s, C_s, udt_s, small_s, small_s],
        out_specs=[xs_s, st_s],
        out_shape=[jax.ShapeDtypeStruct((S, SSD_DI), F32),
                   jax.ShapeDtypeStruct((SSD_GROUPS, nc, H, N, P), F32)],
        scratch_shapes=[pltpu.VMEM((H, N, P), F32)],
        compiler_params=_cp(("parallel", "arbitrary")),
    )(xbc, xbc, xbc, udt, dtb, alog)


def ssd_scan_bwd(name, xbc, udt, dtb, alog, states, dy, rev):
    S = xbc.shape[0]
    Q, N, P, H = SSD_CHUNK, SSD_STATE, SSD_HEADDIM, SSD_HPG
    nc = S // Q
    xs_s, B_s, C_s, udt_s, small_s, st_s = _ssd_specs(nc, not rev)
    BC_out = pl.BlockSpec((Q, N), lambda g, s: ((s if rev else nc - 1 - s), g))

    def body(xs_ref, B_ref, C_ref, udt_ref, dtb_ref, alog_ref, st_ref, dy_ref,
             dx_ref, dB_ref, dC_ref, dudt_ref, ddtb_ref, dalog_ref, dstate):
        @pl.when(pl.program_id(1) == 0)
        def _():
            dstate[...] = jnp.zeros_like(dstate)
            ddtb_ref[...] = jnp.zeros_like(ddtb_ref)
            dalog_ref[...] = jnp.zeros_like(dalog_ref)

        st_in = [st_ref[0, 0, j] for j in range(H)]
        xs = [xs_ref[:, j * P:(j + 1) * P] for j in range(H)]
        _, vjp = jax.vjp(functools.partial(_ssd_chunk, rev), st_in, xs, udt_ref[0], dtb_ref[0], alog_ref[0],
                         B_ref[...], C_ref[...])
        dys = [dy_ref[:, j * P:(j + 1) * P] for j in range(H)]
        dst_in, dxs, dudt, ddtb, dalog, dB, dC = vjp((dys, [dstate[j] for j in range(H)]))
        dx_ref[...] = jnp.concatenate(dxs, axis=1)
        dB_ref[...] = dB
        dC_ref[...] = dC
        dudt_ref[0] = dudt
        ddtb_ref[0] += ddtb
        dalog_ref[0] += dalog
        for j in range(H):
            dstate[j] = dst_in[j]

    return pl.pallas_call(
        body, name=name, grid=(SSD_GROUPS, nc),
        in_specs=[xs_s, B_s, C_s, udt_s, small_s, small_s, st_s, xs_s],
        out_specs=[xs_s, BC_out, BC_out, udt_s, small_s, small_s],
        out_shape=[jax.ShapeDtypeStruct((S, SSD_DI), F32),
                   jax.ShapeDtypeStruct((S, SSD_GROUPS * N), F32),
                   jax.ShapeDtypeStruct((S, SSD_GROUPS * N), F32),
                   jax.ShapeDtypeStruct((SSD_GROUPS, S, H), F32),
                   jax.ShapeDtypeStruct((SSD_GROUPS, 1, H), F32),
                   jax.ShapeDtypeStruct((SSD_GROUPS, 1, H), F32)],
        scratch_shapes=[pltpu.VMEM((H, N, P), F32)],
        compiler_params=_cp(("parallel", "arbitrary")),
    )(xbc, xbc, xbc, udt, dtb, alog, states, dy)


def _ssd_combine(yf, yb, xs, z, dexp, ng):
    y = (yf + yb + xs * dexp) * _silu(z)
    return _rms_groups(y, ng, SSD_DI // SSD_GROUPS)


def ssd_forward(x, hn, w):
    S = x.shape[0]
    u = matmul("ssd_in", hn, w["ssd_w_main"])
    udt = matmul("ssd_in_dt", hn, w["ssd_w_dt"])
    xbc = ssd_conv_fwd(u, w["ssd_conv_w8"], w["ssd_conv_b"])
    udt_t = udt.reshape(S, 2, SSD_GROUPS, SSD_HPG).transpose(1, 2, 0, 3)
    dtb = w["ssd_dt_bias"].reshape(2, SSD_GROUPS, 1, SSD_HPG)
    alog = w["ssd_a_log"].reshape(2, SSD_GROUPS, 1, SSD_HPG)
    yf, stf = ssd_scan_fwd("ssd_scan_f", xbc, udt_t[0], dtb[0], alog[0], False)
    yb, stb = ssd_scan_fwd("ssd_scan_b", xbc, udt_t[1], dtb[1], alog[1], True)
    dexp = jnp.repeat(w["ssd_d"].reshape(1, SSD_HEADS), SSD_HEADDIM, axis=1)
    (yn,), _ = rowcall("ssd_combine", lambda yf, yb, xs, z, d, g: ((_ssd_combine(yf, yb, xs, z, d, g),), ()),
                       [yf, yb, _col(xbc, SSD_DI, 0), _col(u, SSD_DI, 0)], [dexp, w["ssd_norm_g"]],
                       [(SSD_DI, BF16)], tb=256)
    out = matmul("ssd_out", yn, w["ssd_w_out"], res=x)
    saved = dict(hn=hn, u=u, xbc=xbc, udt_t=udt_t, dtb=dtb, alog=alog, yf=yf, yb=yb, stf=stf, stb=stb,
                 dexp=dexp, yn=yn)
    return out, saved


def ssd_backward(dy, sv, w):
    S = dy.shape[0]
    u, xbc = sv["u"], sv["xbc"]
    dyn = matmul("ssd_out_dx", dy, w["ssd_w_out"], mode="nt")
    g_w_out = matmul("ssd_out_dw", sv["yn"], dy, mode="tn")

    def comb_bwd(yf, yb, xs, z, dyn, dexp, ng):
        _, vjp = jax.vjp(_ssd_combine, yf, yb, xs, z, dexp, ng)
        dyf, _, dxs, dz, ddexp, dng = vjp(dyn)
        return (dyf, dxs, dz), (ddexp, dng)

    (dyc, dskip, dz), (ddexp, g_norm) = rowcall(
        "ssd_combine_bwd", comb_bwd, [sv["yf"], sv["yb"], _col(xbc, SSD_DI, 0), _col(u, SSD_DI, 0), dyn],
        [sv["dexp"], w["ssd_norm_g"]], [(SSD_DI, F32)] * 3, [(1, SSD_DI), (1, SSD_DI)], tb=256)
    udt_t, dtb, alog = sv["udt_t"], sv["dtb"], sv["alog"]
    dxf, dBf, dCf, dudt_f, ddtb_f, dalog_f = ssd_scan_bwd("ssd_scan_f_bwd", xbc, udt_t[0], dtb[0], alog[0],
                                                          sv["stf"], dyc, False)
    dxb, dBb, dCb, dudt_b, ddtb_b, dalog_b = ssd_scan_bwd("ssd_scan_b_bwd", xbc, udt_t[1], dtb[1], alog[1],
                                                          sv["stb"], dyc, True)

    def gather(dxf, dxb, dskip, dBf, dBb, dCf, dCb):
        return (jnp.concatenate([dxf + dxb + dskip, dBf + dBb, dCf + dCb], axis=1),), ()

    (dxbc,), _ = rowcall("ssd_dxbc", gather, [dxf, dxb, dskip, dBf, dBb, dCf, dCb], [], [(SSD_CONV_CH, F32)], tb=256)
    du, g_conv_w8, g_conv_b = ssd_conv_bwd(u, dxbc, dz, w["ssd_conv_w8"], w["ssd_conv_b"])
    dudt = jnp.stack([dudt_f, dudt_b]).transpose(2, 0, 1, 3).reshape(S, 2 * SSD_HEADS)
    hn = sv["hn"]
    g_main = matmul("ssd_in_dw", hn, du, mode="tn")
    g_dt = matmul("ssd_in_dt_dw", hn, dudt, mode="tn")
    dhn = matmul("ssd_in_dt_dx", dudt, w["ssd_w_dt"], mode="nt")
    dhn = matmul("ssd_in_dx", du, w["ssd_w_main"], mode="nt", res=dhn)
    grads = dict(
        ssd_w_in=jnp.concatenate([g_main, g_dt], axis=1)[None],
        ssd_conv_w=g_conv_w8[None, :SSD_CONV],
        ssd_conv_b=g_conv_b,
        ssd_dt_bias=jnp.stack([ddtb_f, ddtb_b]).reshape(1, 2, SSD_HEADS),
        ssd_a_log=jnp.stack([dalog_f, dalog_b]).reshape(1, 2, SSD_HEADS),
        ssd_d_exp=ddexp,
        ssd_norm_g=g_norm,
        ssd_w_out=g_w_out[None],
    )
    return dhn, grads


def _hg_block(rev, stT, uq, uf, ui, lb):
    C = HG_CHUNK
    n = uq.shape[0] // C
    q = _silu(uq)
    f = lb + (1.0 - lb) * jax.nn.sigmoid(uf)
    k = 1.0 - f
    g = jnp.log(f)
    r = lax.broadcasted_iota(jnp.int32, (C, C), 0)
    c = lax.broadcasted_iota(jnp.int32, (C, C), 1)
    mask = (r <= c) if rev else (r >= c)
    Tm = mask.astype(F32)
    outs = [None] * n
    for i in (reversed(range(n)) if rev else range(n)):
        sl = slice(i * C, (i + 1) * C)
        qi, ki, vi = q[sl], k[sl], ui[sl]
        G = _mm_exact(Tm, g[sl])
        Gr = G[C // 2:C // 2 + 1]
        Gl = G[0:1] if rev else G[C - 1:C]
        att = jnp.where(mask, _mm_nt(qi * jnp.exp(G - Gr), ki * jnp.exp(Gr - G)), 0.0)
        outs[i] = _mm(att, vi) + _mm_nt(qi * jnp.exp(G), stT)
        stT = stT * jnp.exp(Gl) + _mm_tn(vi, ki * jnp.exp(Gl - G))
    return jnp.concatenate(outs, axis=0), stT


def _hg_specs(nb, rev_order, f_col):
    R = HG_ROWS

    def bidx(s):
        return nb - 1 - s if rev_order else s

    def col(base):
        return pl.BlockSpec((R, HG_EXPAND), lambda h, s: (bidx(s), base * HG_HEADS + h))

    out = pl.BlockSpec((R, HG_EXPAND), lambda h, s: (bidx(s), h))
    lb = pl.BlockSpec((1, HG_EXPAND), lambda h, s: (0, h))
    st = pl.BlockSpec((1, 1, HG_EXPAND, HG_EXPAND), lambda h, s: (h, bidx(s), 0, 0))
    return col(0), col(f_col), col(3), out, lb, st


def hg_scan_fwd(name, u, lb, rev):
    S = u.shape[0]
    nb = S // HG_ROWS
    q_s, f_s, i_s, o_s, lb_s, st_s = _hg_specs(nb, rev, 2 if rev else 1)

    def body(uq, uf, ui, lb_ref, o_ref, st_ref, state):
        @pl.when(pl.program_id(1) == 0)
        def _():
            state[...] = jnp.zeros_like(state)

        st_ref[0, 0] = state[...]
        o, st = _hg_block(rev, state[...], uq[...], uf[...], ui[...], lb_ref[...])
        o_ref[...] = o
        state[...] = st

    return pl.pallas_call(
        body, name=name, grid=(HG_HEADS, nb), in_specs=[q_s, f_s, i_s, lb_s], out_specs=[o_s, st_s],
        out_shape=[jax.ShapeDtypeStruct((S, HG_W), F32),
                   jax.ShapeDtypeStruct((HG_HEADS, nb, HG_EXPAND, HG_EXPAND), F32)],
        scratch_shapes=[pltpu.VMEM((HG_EXPAND, HG_EXPAND), F32)],
        compiler_params=_cp(("parallel", "arbitrary")),
    )(u, u, u, lb)


def hg_scan_bwd(name, u, lb, states, do, rev):
    S = u.shape[0]
    nb = S // HG_ROWS
    q_s, f_s, i_s, o_s, lb_s, st_s = _hg_specs(nb, not rev, 2 if rev else 1)

    def body(uq, uf, ui, lb_ref, st_ref, do_ref, dq_ref, df_ref, di_ref, dlb_ref, dstate):
        @pl.when(pl.program_id(1) == 0)
        def _():
            dstate[...] = jnp.zeros_like(dstate)
            dlb_ref[...] = jnp.zeros_like(dlb_ref)

        _, vjp = jax.vjp(functools.partial(_hg_block, rev), st_ref[0, 0], uq[...], uf[...], ui[...], lb_ref[...])
        dst, dq, df, di, dlb = vjp((do_ref[...], dstate[...]))
        dq_ref[...] = dq
        df_ref[...] = df
        di_ref[...] = di
        dlb_ref[...] += dlb
        dstate[...] = dst

    return pl.pallas_call(
        body, name=name, grid=(HG_HEADS, nb), in_specs=[q_s, f_s, i_s, lb_s, st_s, o_s],
        out_specs=[o_s, o_s, o_s, lb_s],
        out_shape=[jax.ShapeDtypeStruct((S, HG_W), F32)] * 3 + [jax.ShapeDtypeStruct((1, HG_W), F32)],
        scratch_shapes=[pltpu.VMEM((HG_EXPAND, HG_EXPAND), F32)],
        compiler_params=_cp(("parallel", "arbitrary")),
    )(u, u, u, lb, states, do)


def _hg_lb(hgrn_lb, layer):
    m = jnp.max(hgrn_lb, axis=0, keepdims=True)
    e = jnp.exp(hgrn_lb - m)
    sm = e / jnp.sum(e, axis=0, keepdims=True)
    lb = jnp.zeros_like(sm[0:1])
    for i in range(1, layer + 1):
        lb = lb + sm[i:i + 1]
    return lb


def _hg_combine(of, ob, gate, ng):
    return _rms_groups(of + ob, ng, HG_EXPAND) * _silu(gate)


def hg_forward(x, hn, w, layer):
    u = matmul("hg_in", hn, w["hg_w_in"])
    (lb,) = smallcall("hg_lb", lambda t: (_hg_lb(t, layer),), [w["hgrn_lb"]], [(1, HG_W)])
    of, stf = hg_scan_fwd("hg_scan_f", u, lb, False)
    ob, stb = hg_scan_fwd("hg_scan_b", u, lb, True)
    (og,), _ = rowcall("hg_combine", lambda of, ob, gate, ng: ((_hg_combine(of, ob, gate, ng),), ()),
                       [of, ob, _col(u, HG_W, 4)], [w["hg_norm_g"]], [(HG_W, BF16)], tb=256)
    out = matmul("hg_out", og, w["hg_w_out"], res=x)
    return out, dict(hn=hn, u=u, lb=lb, of=of, ob=ob, stf=stf, stb=stb, og=og)


def hg_backward(dy, sv, w, layer):
    u, lb = sv["u"], sv["lb"]
    dog = matmul("hg_out_dx", dy, w["hg_w_out"], mode="nt")
    g_w_out = matmul("hg_out_dw", sv["og"], dy, mode="tn")

    def comb_bwd(of, ob, gate, dog, ng):
        _, vjp = jax.vjp(_hg_combine, of, ob, gate, ng)
        dof, _, dgate, dng = vjp(dog)
        return (dof, dgate), (dng,)

    (do, dgate), (g_norm,) = rowcall("hg_combine_bwd", comb_bwd, [sv["of"], sv["ob"], _col(u, HG_W, 4), dog],
                                     [w["hg_norm_g"]], [(HG_W, F32)] * 2, [(1, HG_W)], tb=256)
    dqf, dff, dif, dlbf = hg_scan_bwd("hg_scan_f_bwd", u, lb, sv["stf"], do, False)
    dqb, dfb, dib, dlbb = hg_scan_bwd("hg_scan_b_bwd", u, lb, sv["stb"], do, True)

    def gather(dqf, dqb, dff, dfb, dif, dib, dgate):
        return (jnp.concatenate([dqf + dqb, dff, dfb, dif + dib, dgate], axis=1),), ()

    (du,), _ = rowcall("hg_du", gather, [dqf, dqb, dff, dfb, dif, dib, dgate], [], [(HG_IN, BF16)], tb=256)

    def lb_bwd(t, dlbf, dlbb):
        _, vjp = jax.vjp(lambda t: _hg_lb(t, layer), t)
        return vjp(dlbf + dlbb)

    (g_lb,) = smallcall("hg_lb_bwd", lb_bwd, [w["hgrn_lb"], dlbf, dlbb], [(DEPTH, HG_W)])
    hn = sv["hn"]
    g_w_in = matmul("hg_in_dw", hn, du, mode="tn")
    dhn = matmul("hg_in_dx", du, w["hg_w_in"], mode="nt")
    return dhn, dict(hg_w_in=g_w_in[None], hg_norm_g=g_norm, hg_w_out=g_w_out[None], hgrn_lb=g_lb)


def _rope_tables(S):
    t = np.arange(S)
    row = (t // GRID_W).astype(np.float32)
    col = (t % GRID_W).astype(np.float32)
    inv = (ROPE_THETA ** (-np.arange(0, ROPE_AXIS, 2, dtype=np.float32) / ROPE_AXIS)).astype(np.float32)
    ar = jnp.asarray(row)[:, None] * jnp.asarray(inv)[None, :]
    ac = jnp.asarray(col)[:, None] * jnp.asarray(inv)[None, :]
    cos = jnp.concatenate([jnp.cos(ar), jnp.cos(ar), jnp.cos(ac), jnp.cos(ac)], axis=1)
    sin = jnp.concatenate([-jnp.sin(ar), jnp.sin(ar), -jnp.sin(ac), jnp.sin(ac)], axis=1)
    return cos.astype(F32), sin.astype(F32)


def _rope(x, cos, sin):
    h = ROPE_AXIS // 2
    sw = jnp.concatenate([x[:, h:2 * h], x[:, 0:h], x[:, 3 * h:4 * h], x[:, 2 * h:3 * h]], axis=1)
    return x * cos + sw * sin


def _at_pre(uq, uk, cos, sin, qg, kg):
    qs, ks = [], []
    for h in range(AT_HEADS):
        qs.append(_rope(_rms(uq[:, h * AT_HD:(h + 1) * AT_HD], qg), cos, sin) * (AT_HD ** -0.5))
    for h in range(AT_KV):
        ks.append(_rope(_rms(uk[:, h * AT_HD:(h + 1) * AT_HD], kg), cos, sin))
    return jnp.concatenate(qs, axis=1), jnp.concatenate(ks, axis=1)


def _stack_heads(x):
    return jnp.concatenate([x[:, :AT_HD], x[:, AT_HD:]], axis=0)


def _unstack_heads(x):
    t = x.shape[0] // 2
    return jnp.concatenate([x[:t], x[t:]], axis=1)


def at_flash_fwd(q, k, u):
    S = q.shape[0]
    tq, tk = _pick(S, 512), _pick(S, 1024)
    nq, nk = S // tq, S // tk
    gw = AT_GRP * AT_HD

    def body(q_ref, k_ref, v_ref, o_ref, lse_ref, m_s, l_s, acc):
        j = pl.program_id(2)

        @pl.when(j == 0)
        def _():
            m_s[...] = jnp.full_like(m_s, NEG_BIG)
            l_s[...] = jnp.zeros_like(l_s)
            acc[...] = jnp.zeros_like(acc)

        s = _mm_nt(_stack_heads(q_ref[...]), k_ref[...])
        m_new = jnp.maximum(m_s[...], jnp.max(s, axis=-1, keepdims=True))
        alpha = jnp.exp(m_s[...] - m_new)
        p = jnp.exp(s - m_new)
        l_s[...] = alpha * l_s[...] + jnp.sum(p, axis=-1, keepdims=True)
        acc[...] = alpha * acc[...] + _mm(p, v_ref[...])
        m_s[...] = m_new

        @pl.when(j == nk - 1)
        def _():
            o_ref[...] = _unstack_heads(acc[...] / l_s[...])
            lse_ref[0, 0] = m_s[...] + jnp.log(l_s[...])

    return pl.pallas_call(
        body, name="at_flash_fwd", grid=(AT_KV, nq, nk),
        in_specs=[pl.BlockSpec((tq, gw), lambda h, i, j: (i, h)),
                  pl.BlockSpec((tk, AT_HD), lambda h, i, j: (j, h)),
                  pl.BlockSpec((tk, AT_HD), lambda h, i, j: (j, (AT_QW + AT_KW) // AT_HD + h))],
        out_specs=[pl.BlockSpec((tq, gw), lambda h, i, j: (i, h)),
                   pl.BlockSpec((1, 1, 2 * tq, 1), lambda h, i, j: (h, i, 0, 0))],
        out_shape=[jax.ShapeDtypeStruct((S, AT_QW), F32), jax.ShapeDtypeStruct((AT_KV, nq, 2 * tq, 1), F32)],
        scratch_shapes=[pltpu.VMEM((2 * tq, 1), F32), pltpu.VMEM((2 * tq, 1), F32), pltpu.VMEM((2 * tq, AT_HD), F32)],
        compiler_params=_cp(("parallel", "parallel", "arbitrary")),
    )(q, k, u)


def at_flash_bwd(q, k, u, o, lse, do):
    S = q.shape[0]
    tq, tk = _pick(S, 512), _pick(S, 1024)
    nq, nk = S // tq, S // tk
    gw = AT_GRP * AT_HD

    def body(q_ref, k_ref, v_ref, o_ref, lse_ref, do_ref, dq_ref, dk_ref, dv_ref, dk_acc, dv_acc):
        j, i = pl.program_id(1), pl.program_id(2)

        @pl.when(i == 0)
        def _():
            dk_acc[...] = jnp.zeros_like(dk_acc)
            dv_acc[...] = jnp.zeros_like(dv_acc)

        q2 = _stack_heads(q_ref[...])
        do_blk = do_ref[...]
        do2 = _stack_heads(do_blk)
        delta = _stack_heads(do_blk * o_ref[...])
        delta = jnp.sum(delta, axis=-1, keepdims=True)
        kb, vb = k_ref[...], v_ref[...]
        p = jnp.exp(_mm_nt(q2, kb) - lse_ref[0, 0])
        dv_acc[...] += _mm_tn(p, do2)
        ds = p * (_mm_nt(do2, vb) - delta)
        dk_acc[...] += _mm_tn(ds, q2)
        dq = _unstack_heads(_mm(ds, kb))
        rows = pl.ds(pl.multiple_of(i * tq, tq), tq)

        @pl.when(j == 0)
        def _():
            dq_ref[rows, :] = dq

        @pl.when(j > 0)
        def _():
            dq_ref[rows, :] += dq

        @pl.when(i == nq - 1)
        def _():
            dk_ref[...] = dk_acc[...]
            dv_ref[...] = dv_acc[...]

    return pl.pallas_call(
        body, name="at_flash_bwd", grid=(AT_KV, nk, nq),
        in_specs=[pl.BlockSpec((tq, gw), lambda h, j, i: (i, h)),
                  pl.BlockSpec((tk, AT_HD), lambda h, j, i: (j, h)),
                  pl.BlockSpec((tk, AT_HD), lambda h, j, i: (j, (AT_QW + AT_KW) // AT_HD + h)),
                  pl.BlockSpec((tq, gw), lambda h, j, i: (i, h)),
                  pl.BlockSpec((1, 1, 2 * tq, 1), lambda h, j, i: (h, i, 0, 0)),
                  pl.BlockSpec((tq, gw), lambda h, j, i: (i, h))],
        out_specs=[pl.BlockSpec((S, gw), lambda h, j, i: (0, h)),
                   pl.BlockSpec((tk, AT_HD), lambda h, j, i: (j, h)),
                   pl.BlockSpec((tk, AT_HD), lambda h, j, i: (j, h))],
        out_shape=[jax.ShapeDtypeStruct((S, AT_QW), F32), jax.ShapeDtypeStruct((S, AT_KW), F32),
                   jax.ShapeDtypeStruct((S, AT_KW), F32)],
        scratch_shapes=[pltpu.VMEM((tk, AT_HD), F32), pltpu.VMEM((tk, AT_HD), F32)],
        compiler_params=_cp(("parallel", "arbitrary", "arbitrary")),
    )(q, k, u, o, lse, do)


def at_forward(x, hn, w):
    S = x.shape[0]
    u = matmul("at_in", hn, w["at_w_in"])
    cos, sin = _rope_tables(S)
    (q, k), _ = rowcall("at_pre", lambda uq, uk, c, s, qg, kg: (_at_pre(uq, uk, c, s, qg, kg), ()),
                        [_col(u, AT_QW, 0), _col(u, AT_KW, 2), cos, sin], [w["at_q_norm_g"], w["at_k_norm_g"]],
                        [(AT_QW, BF16), (AT_KW, BF16)], tb=256)
    o, lse = at_flash_fwd(q, k, u)
    (og,), _ = rowcall("at_gate", lambda o, gate: ((o * _silu(gate),), ()), [o, _col(u, AT_QW, 2)], [],
                       [(AT_QW, BF16)], tb=256)
    out = matmul("at_out", og, w["at_w_out"], res=x)
    return out, dict(hn=hn, u=u, cos=cos, sin=sin, q=q, k=k, o=o, lse=lse, og=og)


def at_backward(dy, sv, w):
    u = sv["u"]
    dog = matmul("at_out_dx", dy, w["at_w_out"], mode="nt")
    g_w_out = matmul("at_out_dw", sv["og"], dy, mode="tn")

    def gate_bwd(o, gate, dog):
        _, vjp = jax.vjp(lambda o, gate: o * _silu(gate), o, gate)
        return vjp(dog), ()

    (do, dgate), _ = rowcall("at_gate_bwd", gate_bwd, [sv["o"], _col(u, AT_QW, 2), dog], [],
                             [(AT_QW, F32)] * 2, tb=256)
    dq, dk, dv = at_flash_bwd(sv["q"], sv["k"], u, sv["o"], sv["lse"], do)

    def pre_bwd(uq, uk, cos, sin, dq, dk, dv, dgate, qg, kg):
        _, vjp = jax.vjp(lambda uq, uk, qg, kg: _at_pre(uq, uk, cos, sin, qg, kg), uq, uk, qg, kg)
        duq, duk, dqg, dkg = vjp((dq, dk))
        return (jnp.concatenate([duq, duk, dv, dgate], axis=1),), (dqg, dkg)

    (du,), (g_qg, g_kg) = rowcall(
        "at_pre_bwd", pre_bwd, [_col(u, AT_QW, 0), _col(u, AT_KW, 2), sv["cos"], sv["sin"], dq, dk, dv, dgate],
        [w["at_q_norm_g"], w["at_k_norm_g"]], [(AT_IN, BF16)], [(1, AT_HD), (1, AT_HD)], tb=128)
    hn = sv["hn"]
    g_w_in = matmul("at_in_dw", hn, du, mode="tn")
    dhn = matmul("at_in_dx", du, w["at_w_in"], mode="nt")
    return dhn, dict(at_w_in=g_w_in[None], at_q_norm_g=g_qg, at_k_norm_g=g_kg, at_w_out=g_w_out[None])


def _t5_bucket_np(rel):
    half = REL_BUCKETS // 2
    exact = half // 2
    n = np.abs(rel)
    large = exact + (np.log(np.maximum(n, 1).astype(np.float32) / exact)
                     / math.log(REL_MAX_DIST / exact) * (half - exact)).astype(np.int32)
    large = np.minimum(large, half - 1)
    return np.where(rel > 0, half, 0) + np.where(n < exact, n, large)


def _dl_tq(S, dil):
    return min(128, S // dil)


def _dl_bias_maps(tq, dil):
    W = tq + 2 * DL_STEPS
    i = np.arange(tq)[:, None]
    wdx = np.arange(W)[None, :]
    dm = wdx - DL_STEPS - i
    bucket = _t5_bucket_np(dm * dil).reshape(-1).astype(np.int32)
    band = np.where(np.abs(dm) <= DL_STEPS, 0.0, NEG_BIG).reshape(1, -1).astype(np.float32)
    onehot = (jnp.asarray(bucket)[None, :] == jnp.arange(REL_BUCKETS, dtype=jnp.int32)[:, None]).astype(F32)
    return onehot, jnp.asarray(band)


def _dl_attend(q, kwin, vwin, T, valid):
    tq = q.shape[0]
    os, ls = [], []
    for h in range(DL_HEADS):
        sl = slice(h * DL_HD, (h + 1) * DL_HD)
        s = _mm_nt(q[:, sl] * (DL_HD ** -0.5), kwin[:, sl]) + T[h]
        s = jnp.where(valid, s, NEG_BIG)
        m = lax.stop_gradient(jnp.max(s, axis=-1, keepdims=True))
        lse = m + jnp.log(jnp.sum(jnp.exp(s - m), axis=-1, keepdims=True))
        p = jnp.exp(s - lse)
        os.append(_mm(p, vwin[:, sl]))
        ls.append(jnp.broadcast_to(lse, (tq, DL_HD)))
    return jnp.concatenate(os, axis=1), jnp.concatenate(ls, axis=1)


def _dl_specs(gi, dil, tq, Ls):
    nb = Ls // tq
    hs = DL_STEPS
    per = tq // hs
    ncol = DL_IN // DL_W

    def main(c):
        return pl.BlockSpec((tq, DL_W), lambda r, i: (i, r * ncol + gi * 3 + c))

    def prev(c):
        return pl.BlockSpec((hs, DL_W), lambda r, i: (jnp.maximum(i * per - 1, 0), r * ncol + gi * 3 + c))

    def nxt(c):
        return pl.BlockSpec((hs, DL_W), lambda r, i: (jnp.minimum((i + 1) * per, Ls // hs - 1), r * ncol + gi * 3 + c))

    out = pl.BlockSpec((tq, DL_W), lambda r, i: (i, r))
    return nb, main, prev, nxt, out


def _dl_valid(i, tq, Ls):
    W = tq + 2 * DL_STEPS
    mk = i * tq - DL_STEPS + lax.broadcasted_iota(jnp.int32, (1, W), 1)
    return (mk >= 0) & (mk < Ls)


def dl_attn_fwd(gi, dil, u, T):
    S = u.shape[0]
    Ls = S // dil
    tq = _dl_tq(S, dil)
    nb, main, prev, nxt, out = _dl_specs(gi, dil, tq, Ls)
    uv = u.reshape(Ls, dil * DL_IN)

    def body(q_ref, kp, kc, kn, vp, vc, vn, T_ref, o_ref, l_ref):
        kwin = jnp.concatenate([kp[...], kc[...], kn[...]], axis=0)
        vwin = jnp.concatenate([vp[...], vc[...], vn[...]], axis=0)
        o, l = _dl_attend(q_ref[...], kwin, vwin, T_ref[...], _dl_valid(pl.program_id(1), tq, Ls))
        o_ref[...] = o
        l_ref[...] = l

    o, l = pl.pallas_call(
        body, name=f"dl_attn_fwd{gi}", grid=(dil, nb),
        in_specs=[main(0), prev(1), main(1), nxt(1), prev(2), main(2), nxt(2),
                  pl.BlockSpec(T.shape, _const_map(3))],
        out_specs=[out, out],
        out_shape=[jax.ShapeDtypeStruct((Ls, dil * DL_W), F32)] * 2,
        compiler_params=_cp(("parallel", "parallel")),
    )(uv, uv, uv, uv, uv, uv, uv, T)
    return o.reshape(S, DL_W), l.reshape(S, DL_W)


def dl_attn_bwd(gi, dil, u, T, do, dl):
    S = u.shape[0]
    Ls = S // dil
    tq = _dl_tq(S, dil)
    hs = DL_STEPS
    W = tq + 2 * hs
    nb, main, prev, nxt, out = _dl_specs(gi, dil, tq, Ls)
    uv = u.reshape(Ls, dil * DL_IN)
    win = pl.BlockSpec((1, W, DL_W), lambda r, i: (r * nb + i, 0, 0))

    def body(q_ref, kp, kc, kn, vp, vc, vn, T_ref, do_ref, dl_ref, dq_ref, dkw_ref, dvw_ref, dT_ref):
        first = (pl.program_id(0) == 0) & (pl.program_id(1) == 0)

        @pl.when(first)
        def _():
            dT_ref[...] = jnp.zeros_like(dT_ref)

        kwin = jnp.concatenate([kp[...], kc[...], kn[...]], axis=0)
        vwin = jnp.concatenate([vp[...], vc[...], vn[...]], axis=0)
        valid = _dl_valid(pl.program_id(1), tq, Ls)
        _, vjp = jax.vjp(lambda q, k, v, T: _dl_attend(q, k, v, T, valid), q_ref[...], kwin, vwin, T_ref[...])
        dq, dkw, dvw, dT = vjp((do_ref[...], dl_ref[...]))
        dq_ref[...] = dq
        dkw_ref[0] = dkw
        dvw_ref[0] = dvw
        dT_ref[...] += dT

    dq, dkw, dvw, dT = pl.pallas_call(
        body, name=f"dl_attn_bwd{gi}", grid=(dil, nb),
        in_specs=[main(0), prev(1), main(1), nxt(1), prev(2), main(2), nxt(2),
                  pl.BlockSpec(T.shape, _const_map(3)), out, out],
        out_specs=[out, win, win, pl.BlockSpec(T.shape, _const_map(3))],
        out_shape=[jax.ShapeDtypeStruct((Ls, dil * DL_W), F32),
                   jax.ShapeDtypeStruct((dil * nb, W, DL_W), F32),
                   jax.ShapeDtypeStruct((dil * nb, W, DL_W), F32),
                   jax.ShapeDtypeStruct(T.shape, F32)],
        compiler_params=_cp(("arbitrary", "arbitrary")),
    )(uv, uv, uv, uv, uv, uv, uv, T, do.reshape(Ls, dil * DL_W), dl.reshape(Ls, dil * DL_W))

    per = tq // hs

    def fold(kc, kp, kn, vc, vp, vn, dk_ref, dv_ref):
        i = pl.program_id(1)
        has_p = (i > 0).astype(F32)
        has_n = (i < nb - 1).astype(F32)
        for c_ref, p_ref, n_ref, o_ref in ((kc, kp, kn, dk_ref), (vc, vp, vn, dv_ref)):
            o_ref[...] = c_ref[0, hs:hs + tq, :]
            o_ref[0:hs, :] += p_ref[0] * has_p
            o_ref[tq - hs:tq, :] += n_ref[0] * has_n

    wfull = pl.BlockSpec((1, W, DL_W), lambda r, i: (r * nb + i, 0, 0))
    wprev = pl.BlockSpec((1, hs, DL_W), lambda r, i: (r * nb + jnp.maximum(i - 1, 0), per + 1, 0))
    wnext = pl.BlockSpec((1, hs, DL_W), lambda r, i: (r * nb + jnp.minimum(i + 1, nb - 1), 0, 0))
    dk, dv = pl.pallas_call(
        fold, name=f"dl_fold{gi}", grid=(dil, nb),
        in_specs=[wfull, wprev, wnext, wfull, wprev, wnext], out_specs=[out, out],
        out_shape=[jax.ShapeDtypeStruct((Ls, dil * DL_W), F32)] * 2,
        compiler_params=_cp(("parallel", "parallel")),
    )(dkw, dkw, dkw, dvw, dvw, dvw)
    return dq.reshape(S, DL_W), dk.reshape(S, DL_W), dv.reshape(S, DL_W), dT


def _dl_merge(o0, o1, o2, l0, l1, l2, gate):
    m = jnp.maximum(jnp.maximum(l0, l1), l2)
    e0, e1, e2 = jnp.exp(l0 - m), jnp.exp(l1 - m), jnp.exp(l2 - m)
    den = e0 + e1 + e2
    return ((e0 * o0 + e1 * o1 + e2 * o2) / den) * _silu(gate)


def dl_forward(x, hn, w):
    S = x.shape[0]
    u = matmul("dl_in", hn, w["dl_w_in"])
    rbT = w["rel_bias"].T
    os, ls, Ts, maps = [], [], [], []
    for gi, (_, dil) in enumerate(DL_PAIRS):
        tq = _dl_tq(S, dil)
        W = tq + 2 * DL_STEPS
        onehot, band = _dl_bias_maps(tq, dil)
        (T,) = smallcall(f"dl_bias{gi}", lambda rbT, oh, band: (_mm_exact(rbT, oh) + band,), [rbT, onehot, band],
                         [(DL_HEADS, tq * W)])
        T = T.reshape(DL_HEADS, tq, W)
        o, l = dl_attn_fwd(gi, dil, u, T)
        os.append(o)
        ls.append(l)
        Ts.append(T)
        maps.append(onehot)
    (og,), _ = rowcall("dl_merge", lambda *a: ((_dl_merge(*a),), ()), os + ls + [_col(u, DL_W, 9)], [],
                       [(DL_W, BF16)], tb=256)
    out = matmul("dl_out", og, w["dl_w_out"], res=x)
    return out, dict(hn=hn, u=u, os=os, ls=ls, Ts=Ts, maps=maps, og=og)


def dl_backward(dy, sv, w):
    u = sv["u"]
    dog = matmul("dl_out_dx", dy, w["dl_w_out"], mode="nt")
    g_w_out = matmul("dl_out_dw", sv["og"], dy, mode="tn")

    def merge_bwd(o0, o1, o2, l0, l1, l2, gate, dog):
        _, vjp = jax.vjp(_dl_merge, o0, o1, o2, l0, l1, l2, gate)
        return vjp(dog), ()

    grads7, _ = rowcall("dl_merge_bwd", merge_bwd, sv["os"] + sv["ls"] + [_col(u, DL_W, 9), dog], [],
                        [(DL_W, F32)] * 7, tb=256)
    dos, dls, dgate = grads7[0:3], grads7[3:6], grads7[6]
    pieces, g_rbT = [], None
    for gi, (_, dil) in enumerate(DL_PAIRS):
        dq, dk, dv, dT = dl_attn_bwd(gi, dil, u, sv["Ts"][gi], dos[gi], dls[gi])
        pieces += [dq, dk, dv]
        (g,) = smallcall(f"dl_bias_bwd{gi}", lambda dT, oh: (_mm_nt_exact(dT, oh),),
                         [dT.reshape(DL_HEADS, -1), sv["maps"][gi]], [(DL_HEADS, REL_BUCKETS)])
        g_rbT = g if g_rbT is None else g_rbT + g
    pieces.append(dgate)
    (du,), _ = rowcall("dl_du", lambda *a: ((jnp.concatenate(a, axis=1),), ()), pieces, [], [(DL_IN, BF16)], tb=128)
    hn = sv["hn"]
    g_w_in = matmul("dl_in_dw", hn, du, mode="tn")
    dhn = matmul("dl_in_dx", du, w["dl_w_in"], mode="nt")
    return dhn, dict(dl_w_in=g_w_in[None], dl_w_out=g_w_out[None], rel_bias=g_rbT.T)


_FWD = (ssd_forward, hg_forward, at_forward, dl_forward)
_BWD = (ssd_backward, hg_backward, at_backward, dl_backward)


def local_step(x, tgt, w):
    saved = []
    h = x
    for layer in range(DEPTH):
        hn = norm_fwd(f"norm{layer}", h, w["norm_g"][layer:layer + 1])
        extra = (layer,) if layer % 4 == 1 else ()
        h_next, sv = _FWD[layer % 4](h, hn, w, *extra)
        saved.append((h, sv))
        h = h_next
    loss, dh, g_final = loss_head(h, tgt, w["final_g"].reshape(1, D_MODEL))
    grads = {}
    g_norm = [None] * DEPTH
    for layer in reversed(range(DEPTH)):
        h_in, sv = saved[layer]
        extra = (layer,) if layer % 4 == 1 else ()
        dhn, g = _BWD[layer % 4](dh, sv, w, *extra)
        grads.update(g)
        dh, g_norm[layer] = norm_bwd(f"norm{layer}_bwd", h_in, w["norm_g"][layer:layer + 1], dhn, dh)
    grads["norm_g"] = jnp.concatenate(g_norm, axis=0)
    grads["final_g"] = g_final.reshape(D_MODEL)
    grads["ssd_d"] = jnp.sum(grads.pop("ssd_d_exp").reshape(SSD_HEADS, SSD_HEADDIM), axis=1)[None]
    return loss, dh, grads


IN_NAMES = ("ssd_w_in", "hg_w_in", "at_w_in", "dl_w_in")
OUT_NAMES = ("ssd_w_out", "hg_w_out", "at_w_out", "dl_w_out")
IN_COLS = (SSD_IN // 4, HG_IN // 4, AT_IN // 4, DL_IN // 4)
OUT_ROWS = (SSD_DI // 4, HG_W // 4, AT_QW // 4, DL_W // 4)
PACK_IN = sum(IN_COLS)
PACK_OUT = sum(OUT_ROWS)
N_CHIPS = 4
N_DEV = 8
HBM = pl.BlockSpec(memory_space=pl.ANY)


def _mesh_pos():
    return lax.axis_index("x"), lax.axis_index("y"), lax.axis_index("c")


def _other_chips(x, y):
    return [(1 - x, y), (x, 1 - y), (1 - x, 1 - y)]


def gather_weights(p_in, p_out, p_small):
    h_in, h_out = p_in.shape[0] // 2, p_out.shape[0] // 2

    def body(pin, pout, psm, gin, gout, gsm, send, recv, lsem):
        x, y, c = _mesh_pos()
        me = 2 * x + y
        sib = (x, y, 1 - c)
        chips = _other_chips(x, y)

        def rows(half, n):
            return pl.ds(pl.multiple_of(half * n, n), n)

        def rc(src, dst, k, to):
            return pltpu.make_async_remote_copy(src_ref=src, dst_ref=dst, send_sem=send.at[k], recv_sem=recv.at[k],
                                                device_id=to, device_id_type=MESH)

        local = [pltpu.make_async_copy(pin, gin.at[me], lsem.at[0]),
                 pltpu.make_async_copy(pout, gout.at[me], lsem.at[1]),
                 pltpu.make_async_copy(psm, gsm.at[me], lsem.at[2])]
        for cp in local:
            cp.start()
        started = []
        for j, (px, py) in enumerate(chips):
            to = (px, py, c)
            started += [rc(pin.at[rows(c, h_in)], gin.at[me, rows(c, h_in)], 3 * j, to),
                        rc(pout.at[rows(c, h_out)], gout.at[me, rows(c, h_out)], 3 * j + 1, to),
                        rc(psm, gsm.at[me], 3 * j + 2, to)]
        for cp in started:
            cp.start()
        for j, (px, py) in enumerate(chips):
            kp = 2 * px + py
            frm = (px, py, c)
            rc(pin.at[rows(c, h_in)], gin.at[kp, rows(c, h_in)], 3 * j, frm).wait_recv()
            f_in = rc(gin.at[kp, rows(c, h_in)], gin.at[kp, rows(c, h_in)], 9 + 2 * j, sib)
            f_in.start()
            rc(pout.at[rows(c, h_out)], gout.at[kp, rows(c, h_out)], 3 * j + 1, frm).wait_recv()
            f_out = rc(gout.at[kp, rows(c, h_out)], gout.at[kp, rows(c, h_out)], 10 + 2 * j, sib)
            f_out.start()
            rc(psm, gsm.at[kp], 3 * j + 2, frm).wait_recv()
            started += [f_in, f_out]
        for j, (px, py) in enumerate(chips):
            kp = 2 * px + py
            rc(gin.at[kp, rows(1 - c, h_in)], gin.at[kp, rows(1 - c, h_in)], 9 + 2 * j, sib).wait_recv()
            rc(gout.at[kp, rows(1 - c, h_out)], gout.at[kp, rows(1 - c, h_out)], 10 + 2 * j, sib).wait_recv()
        for cp in started:
            cp.wait_send()
        for cp in local:
            cp.wait()

    return pl.pallas_call(
        body, name="gather_weights", in_specs=[HBM, HBM, HBM], out_specs=[HBM, HBM, HBM],
        out_shape=[jax.ShapeDtypeStruct((N_CHIPS,) + p_in.shape, p_in.dtype),
                   jax.ShapeDtypeStruct((N_CHIPS,) + p_out.shape, p_out.dtype),
                   jax.ShapeDtypeStruct((N_CHIPS,) + p_small.shape, p_small.dtype)],
        scratch_shapes=[pltpu.SemaphoreType.DMA((15,)), pltpu.SemaphoreType.DMA((15,)), pltpu.SemaphoreType.DMA((3,))],
        compiler_params=pltpu.CompilerParams(has_side_effects=True),
    )(p_in, p_out, p_small)


def swap_halves(g_in, g_out):
    h_in, h_out = g_in.shape[1] // 2, g_out.shape[1] // 2

    def body(gi, go, ri, ro, send, recv):
        x, y, c = _mesh_pos()
        sib = (x, y, 1 - c)

        def rows(half, n):
            return pl.ds(pl.multiple_of(half * n, n), n)

        cps = [pltpu.make_async_remote_copy(src_ref=gi.at[:, rows(1 - c, h_in)], dst_ref=ri, send_sem=send.at[0],
                                            recv_sem=recv.at[0], device_id=sib, device_id_type=MESH),
               pltpu.make_async_remote_copy(src_ref=go.at[:, rows(1 - c, h_out)], dst_ref=ro, send_sem=send.at[1],
                                            recv_sem=recv.at[1], device_id=sib, device_id_type=MESH)]
        for cp in cps:
            cp.start()
        for cp in cps:
            cp.wait()

    return pl.pallas_call(
        body, name="swap_halves", in_specs=[HBM, HBM], out_specs=[HBM, HBM],
        out_shape=[jax.ShapeDtypeStruct((N_CHIPS, h_in, g_in.shape[2]), g_in.dtype),
                   jax.ShapeDtypeStruct((N_CHIPS, h_out, g_out.shape[2]), g_out.dtype)],
        scratch_shapes=[pltpu.SemaphoreType.DMA((2,)), pltpu.SemaphoreType.DMA((2,))],
        compiler_params=pltpu.CompilerParams(has_side_effects=True),
    )(g_in, g_out)


def half_add(name, g, r, c_idx, tb):
    _, rows2, C = g.shape
    h = rows2 // 2
    nb = h // tb

    def body(c_ref, g_ref, r_ref, f_ref, b_ref):
        s = g_ref[...] + r_ref[...]
        f_ref[...] = s
        b_ref[...] = s.astype(BF16)

    grid_spec = pltpu.PrefetchScalarGridSpec(
        num_scalar_prefetch=1, grid=(N_CHIPS, nb),
        in_specs=[pl.BlockSpec((1, tb, C), lambda k, i, c: (k, c[0] * nb + i, 0)),
                  pl.BlockSpec((1, tb, C), lambda k, i, c: (k, i, 0))],
        out_specs=[pl.BlockSpec((1, tb, C), lambda k, i, c: (k, i, 0))] * 2)
    return pl.pallas_call(
        body, name=name, grid_spec=grid_spec,
        out_shape=[jax.ShapeDtypeStruct((N_CHIPS, h, C), F32), jax.ShapeDtypeStruct((N_CHIPS, h, C), BF16)],
        compiler_params=_cp(("parallel", "parallel")),
    )(c_idx, g, r)


def scatter_chips(b_in, b_out):
    def body(bi, bo, ri, ro, send, recv):
        x, y, c = _mesh_pos()
        cps = []
        for j, (px, py) in enumerate(_other_chips(x, y)):
            kp = 2 * px + py
            to = (px, py, c)
            cps += [pltpu.make_async_remote_copy(src_ref=bi.at[kp], dst_ref=ri.at[j], send_sem=send.at[2 * j],
                                                 recv_sem=recv.at[2 * j], device_id=to, device_id_type=MESH),
                    pltpu.make_async_remote_copy(src_ref=bo.at[kp], dst_ref=ro.at[j], send_sem=send.at[2 * j + 1],
                                                 recv_sem=recv.at[2 * j + 1], device_id=to, device_id_type=MESH)]
        for cp in cps:
            cp.start()
        for cp in cps:
            cp.wait()

    return pl.pallas_call(
        body, name="scatter_chips", in_specs=[HBM, HBM], out_specs=[HBM, HBM],
        out_shape=[jax.ShapeDtypeStruct((3,) + b_in.shape[1:], BF16), jax.ShapeDtypeStruct((3,) + b_out.shape[1:], BF16)],
        scratch_shapes=[pltpu.SemaphoreType.DMA((6,)), pltpu.SemaphoreType.DMA((6,))],
        compiler_params=pltpu.CompilerParams(has_side_effects=True),
    )(b_in, b_out)


def chip_sum(name, f, r, me_idx, tb):
    _, h, C = f.shape
    nb = h // tb

    def body(me_ref, f_ref, r0, r1, r2, o_ref):
        o_ref[...] = ((f_ref[0] + r0[0].astype(F32)) + r1[0].astype(F32)) + r2[0].astype(F32)

    def slot(j):
        return pl.BlockSpec((1, tb, C), lambda i, me: (j, i, 0))

    grid_spec = pltpu.PrefetchScalarGridSpec(
        num_scalar_prefetch=1, grid=(nb,),
        in_specs=[pl.BlockSpec((1, tb, C), lambda i, me: (me[0], i, 0)), slot(0), slot(1), slot(2)],
        out_specs=pl.BlockSpec((tb, C), lambda i, me: (i, 0)))
    return pl.pallas_call(
        body, name=name, grid_spec=grid_spec, out_shape=jax.ShapeDtypeStruct((h, C), F32),
        compiler_params=_cp(("parallel",)),
    )(me_idx, f, r, r, r)


def share_halves(f_in, f_out):
    h_in, h_out = f_in.shape[0], f_out.shape[0]

    def body(fi, fo, wi, wo, send, recv, lsem):
        x, y, c = _mesh_pos()
        sib = (x, y, 1 - c)

        def rows(half, n):
            return pl.ds(pl.multiple_of(half * n, n), n)

        local = [pltpu.make_async_copy(fi, wi.at[rows(c, h_in)], lsem.at[0]),
                 pltpu.make_async_copy(fo, wo.at[rows(c, h_out)], lsem.at[1])]
        cps = [pltpu.make_async_remote_copy(src_ref=fi, dst_ref=wi.at[rows(c, h_in)], send_sem=send.at[0],
                                            recv_sem=recv.at[0], device_id=sib, device_id_type=MESH),
               pltpu.make_async_remote_copy(src_ref=fo, dst_ref=wo.at[rows(c, h_out)], send_sem=send.at[1],
                                            recv_sem=recv.at[1], device_id=sib, device_id_type=MESH)]
        for cp in local + cps:
            cp.start()
        for cp in cps:
            cp.wait_send()
        pltpu.make_async_remote_copy(src_ref=fi, dst_ref=wi.at[rows(1 - c, h_in)], send_sem=send.at[0],
                                     recv_sem=recv.at[0], device_id=sib, device_id_type=MESH).wait_recv()
        pltpu.make_async_remote_copy(src_ref=fo, dst_ref=wo.at[rows(1 - c, h_out)], send_sem=send.at[1],
                                     recv_sem=recv.at[1], device_id=sib, device_id_type=MESH).wait_recv()
        for cp in local:
            cp.wait()

    return pl.pallas_call(
        body, name="share_halves", in_specs=[HBM, HBM], out_specs=[HBM, HBM],
        out_shape=[jax.ShapeDtypeStruct((2 * h_in, f_in.shape[1]), F32), jax.ShapeDtypeStruct((2 * h_out, f_out.shape[1]), F32)],
        scratch_shapes=[pltpu.SemaphoreType.DMA((2,)), pltpu.SemaphoreType.DMA((2,)), pltpu.SemaphoreType.DMA((2,))],
        compiler_params=pltpu.CompilerParams(has_side_effects=True),
    )(f_in, f_out)


def gather_small(pack):
    def body(p, g, send, recv, lsem):
        x, y, c = _mesh_pos()
        me = 4 * x + 2 * y + c
        local = pltpu.make_async_copy(p, g.at[me], lsem)
        local.start()
        cps = []
        k = 0
        for fx in (0, 1):
            for fy in (0, 1):
                for fc in (0, 1):
                    if fx + fy + fc == 0:
                        continue
                    to = (x ^ fx, y ^ fy, c ^ fc)
                    cps.append((pltpu.make_async_remote_copy(src_ref=p, dst_ref=g.at[me], send_sem=send.at[k],
                                                             recv_sem=recv.at[k], device_id=to, device_id_type=MESH), to, k))
                    k += 1
        for cp, _, _ in cps:
            cp.start()
        for cp, to, k in cps:
            frm = 4 * to[0] + 2 * to[1] + to[2]
            pltpu.make_async_remote_copy(src_ref=p, dst_ref=g.at[frm], send_sem=send.at[k], recv_sem=recv.at[k],
                                         device_id=to, device_id_type=MESH).wait_recv()
        for cp, _, _ in cps:
            cp.wait_send()
        local.wait()

    return pl.pallas_call(
        body, name="gather_small", in_specs=[HBM], out_specs=HBM,
        out_shape=jax.ShapeDtypeStruct((N_DEV,) + pack.shape, pack.dtype),
        scratch_shapes=[pltpu.SemaphoreType.DMA((7,)), pltpu.SemaphoreType.DMA((7,)), pltpu.SemaphoreType.DMA],
        compiler_params=pltpu.CompilerParams(has_side_effects=True),
    )(pack)


def _adamw(w, g, m, v):
    m = ADAM_B1 * m + (1.0 - ADAM_B1) * g
    v = ADAM_B2 * v + (1.0 - ADAM_B2) * (g * g)
    m_hat = m / (1.0 - ADAM_B1 ** ADAM_STEP)
    v_hat = v / (1.0 - ADAM_B2 ** ADAM_STEP)
    delta = -ADAM_LR * (m_hat / (jnp.sqrt(v_hat) + ADAM_EPS) + ADAM_WD * w)
    return delta, m, v


def adamw_big(name, w, g, m, v):
    shp = w.shape
    flat = lambda a: a.reshape(shp[-2], shp[-1])
    (d, nm, nv), _ = rowcall(name, lambda w, g, m, v: (_adamw(w, g, m, v), ()), [flat(w), flat(g), flat(m), flat(v)], [],
                             [(shp[-1], F32)] * 3, tb=256)
    return d.reshape(shp), nm.reshape(shp), nv.reshape(shp)


def _pack_small(arrs):
    flat = jnp.concatenate([a.reshape(-1) for a in arrs])
    n = flat.shape[0]
    rows = -(-n // (SUBLANE * LANE)) * SUBLANE
    return jnp.pad(flat, (0, rows * LANE - n)).reshape(rows, LANE)


def _unpack_small(pack, shapes):
    flat = pack.reshape(-1)
    outs, off = [], 0
    for s in shapes:
        n = int(np.prod(s))
        outs.append(flat[off:off + n].reshape(s))
        off += n
    return outs


SMALL_NAMES = ("norm_g", "final_g", "rel_bias", "hgrn_lb", "ssd_conv_w", "ssd_conv_b", "ssd_dt_bias", "ssd_a_log",
               "ssd_d", "ssd_norm_g", "hg_norm_g", "at_q_norm_g", "at_k_norm_g")
ALL_NAMES = ("norm_g", "final_g", "rel_bias", "hgrn_lb", "ssd_w_in", "ssd_conv_w", "ssd_conv_b", "ssd_dt_bias",
             "ssd_a_log", "ssd_d", "ssd_norm_g", "ssd_w_out", "hg_w_in", "hg_norm_g", "hg_w_out", "at_w_in",
             "at_q_norm_g", "at_k_norm_g", "at_w_out", "dl_w_in", "dl_w_out")


def kernel(x, norm_g, final_g, rel_bias, hgrn_lb, ssd_w_in, ssd_conv_w, ssd_conv_b, ssd_dt_bias, ssd_a_log, ssd_d, ssd_norm_g, ssd_w_out, hg_w_in, hg_norm_g, hg_w_out, at_w_in, at_q_norm_g, at_k_norm_g, at_w_out, dl_w_in, dl_w_out, loss_target, m_norm_g, m_final_g, m_rel_bias, m_hgrn_lb, m_ssd_w_in, m_ssd_conv_w, m_ssd_conv_b, m_ssd_dt_bias, m_ssd_a_log, m_ssd_d, m_ssd_norm_g, m_ssd_w_out, m_hg_w_in, m_hg_norm_g, m_hg_w_out, m_at_w_in, m_at_q_norm_g, m_at_k_norm_g, m_at_w_out, m_dl_w_in, m_dl_w_out, v_norm_g, v_final_g, v_rel_bias, v_hgrn_lb, v_ssd_w_in, v_ssd_conv_w, v_ssd_conv_b, v_ssd_dt_bias, v_ssd_a_log, v_ssd_d, v_ssd_norm_g, v_ssd_w_out, v_hg_w_in, v_hg_norm_g, v_hg_w_out, v_at_w_in, v_at_q_norm_g, v_at_k_norm_g, v_at_w_out, v_dl_w_in, v_dl_w_out):
    args = locals()
    W = {n: args[n] for n in ALL_NAMES}
    M = {n: args["m_" + n] for n in ALL_NAMES}
    V = {n: args["v_" + n] for n in ALL_NAMES}
    xi, yi, ci = lax.axis_index("x"), lax.axis_index("y"), lax.axis_index("c")
    chip = 2 * xi + yi
    conv_shard = SSD_CONV_CH // N_CHIPS
    hgn_shard = HG_W // N_CHIPS

    p_in = jnp.concatenate([W[n][0].astype(BF16) for n in IN_NAMES], axis=1)
    p_out = jnp.concatenate([W[n][0].astype(BF16) for n in OUT_NAMES], axis=0)
    p_small = jnp.concatenate([
        jnp.pad(ssd_conv_w[0], ((0, 0), (0, D_MODEL - conv_shard))),
        jnp.pad(hg_norm_g, ((0, 0), (0, D_MODEL - hgn_shard)))], axis=0)
    g_in, g_out, g_small = gather_weights(p_in, p_out, p_small)
    full = {}
    off = 0
    for n, cols in zip(IN_NAMES, IN_COLS):
        full[n] = jnp.concatenate([g_in[k, :, off:off + cols] for k in range(N_CHIPS)], axis=1)
        off += cols
    off = 0
    for n, rows in zip(OUT_NAMES, OUT_ROWS):
        full[n] = jnp.concatenate([g_out[k, off:off + rows] for k in range(N_CHIPS)], axis=0)
        off += rows
    conv_full = jnp.concatenate([g_small[k, :SSD_CONV, :conv_shard] for k in range(N_CHIPS)], axis=1)
    hgn_full = jnp.concatenate([g_small[k, SSD_CONV:SSD_CONV + 1, :hgn_shard] for k in range(N_CHIPS)], axis=1)
    w = dict(
        norm_g=norm_g, final_g=final_g, rel_bias=rel_bias, hgrn_lb=hgrn_lb,
        ssd_w_main=full["ssd_w_in"][:, :SSD_MAIN], ssd_w_dt=full["ssd_w_in"][:, SSD_MAIN:],
        ssd_conv_w8=jnp.concatenate([conv_full, jnp.zeros((1, SSD_CONV_CH), F32)], axis=0),
        ssd_conv_b=ssd_conv_b, ssd_dt_bias=ssd_dt_bias, ssd_a_log=ssd_a_log, ssd_d=ssd_d, ssd_norm_g=ssd_norm_g,
        ssd_w_out=full["ssd_w_out"], hg_w_in=full["hg_w_in"], hg_norm_g=hgn_full, hg_w_out=full["hg_w_out"],
        at_w_in=full["at_w_in"], at_q_norm_g=at_q_norm_g, at_k_norm_g=at_k_norm_g, at_w_out=full["at_w_out"],
        dl_w_in=full["dl_w_in"], dl_w_out=full["dl_w_out"])

    loss_tile, grad_x, grads = local_step(x[0], loss_target[0], w)
    loss = lax.psum(loss_tile[0, 0], ("x", "y", "c"))

    gp_in = jnp.concatenate([grads[n][0].reshape(D_MODEL, N_CHIPS, cols).transpose(1, 0, 2)
                             for n, cols in zip(IN_NAMES, IN_COLS)], axis=2)
    gp_out = jnp.concatenate([grads[n][0].reshape(N_CHIPS, rows, D_MODEL)
                              for n, rows in zip(OUT_NAMES, OUT_ROWS)], axis=1)
    r_in, r_out = swap_halves(gp_in, gp_out)
    c_idx = ci.astype(jnp.int32).reshape(1)
    me_idx = chip.astype(jnp.int32).reshape(1)
    f_in, b_in = half_add("half_add_in", gp_in, r_in, c_idx, 128)
    f_out, b_out = half_add("half_add_out", gp_out, r_out, c_idx, 256)
    x_in, x_out = scatter_chips(b_in, b_out)
    s_in = chip_sum("chip_sum_in", f_in, x_in, me_idx, 128)
    s_out = chip_sum("chip_sum_out", f_out, x_out, me_idx, 256)
    red_in, red_out = share_halves(s_in, s_out)
    G = {}
    off = 0
    for n, cols in zip(IN_NAMES, IN_COLS):
        G[n] = red_in[:, off:off + cols][None]
        off += cols
    off = 0
    for n, rows in zip(OUT_NAMES, OUT_ROWS):
        G[n] = red_out[off:off + rows][None]
        off += rows

    small_full = [grads[n].reshape(-1) for n in SMALL_NAMES]
    shapes_full = [grads[n].shape for n in SMALL_NAMES]
    packs = gather_small(_pack_small(small_full))
    (red_small,) = smallcall("sum_small", lambda p: (functools.reduce(lambda a, b: a + b, [p[k] for k in range(N_DEV)]),),
                             [packs], [packs.shape[1:]])
    for n, g in zip(SMALL_NAMES, _unpack_small(red_small, shapes_full)):
        G[n] = g
    G["ssd_conv_w"] = lax.dynamic_slice_in_dim(G["ssd_conv_w"].reshape(1, SSD_CONV, SSD_CONV_CH), chip * conv_shard, conv_shard, axis=2)
    G["hg_norm_g"] = lax.dynamic_slice_in_dim(G["hg_norm_g"].reshape(1, HG_W), chip * hgn_shard, hgn_shard, axis=1)
    for n in SMALL_NAMES:
        G[n] = G[n].reshape(W[n].shape)

    D, NM, NV = {}, {}, {}
    for n in IN_NAMES + OUT_NAMES:
        D[n], NM[n], NV[n] = adamw_big("adamw_" + n, W[n], G[n], M[n], V[n])
    shapes = [W[n].shape for n in SMALL_NAMES]
    pk = [_pack_small([T[n] for n in SMALL_NAMES]) for T in (W, G, M, V)]
    outs = smallcall("adamw_small", lambda w, g, m, v: _adamw(w, g, m, v), pk, [pk[0].shape] * 3)
    for T, pack in zip((D, NM, NV), outs):
        for n, a in zip(SMALL_NAMES, _unpack_small(pack, shapes)):
            T[n] = a
    return (loss, grad_x[None], *[G[n] for n in ALL_NAMES], *[D[n] for n in ALL_NAMES],
            *[NM[n] for n in ALL_NAMES], *[NV[n] for n in ALL_NAMES])
```

```python
import functools
import math

import numpy as np
import jax
import jax.numpy as jnp
from jax import lax
from jax.experimental import pallas as pl
from jax.experimental.pallas import tpu as pltpu

F32 = jnp.float32
BF16 = jnp.bfloat16
MESH = pl.DeviceIdType.MESH

D_MODEL = 1024
DEPTH = 4
GRID_W = 64
EPS = 1e-6
NEG_BIG = -1e30

SSD_DI = 2048
SSD_HEADDIM = 64
SSD_HEADS = 32
SSD_GROUPS = 4
SSD_HPG = 8
SSD_STATE = 128
SSD_CONV = 7
SSD_CHUNK = 128
SSD_GPS = 4
SSD_CONV_CH = SSD_DI + 2 * SSD_GROUPS * SSD_STATE
SSD_MAIN = SSD_DI + SSD_CONV_CH
SSD_IN = SSD_MAIN + 2 * SSD_HEADS

HG_HEADS = 8
HG_EXPAND = 128
HG_W = 1024
HG_CHUNK = 32
HG_ROWS = 128
HG_HPS = 8
HG_IN = 5 * HG_W

AT_HEADS = 16
AT_KV = 8
AT_GRP = 2
AT_HD = 128
ROPE_THETA = 10000.0
ROPE_AXIS = 64
AT_QW = AT_HEADS * AT_HD
AT_KW = AT_KV * AT_HD
AT_IN = 2 * AT_QW + 2 * AT_KW

DL_PAIRS = ((128, 1), (512, 4), (2048, 16))
DL_HEADS = 16
DL_HD = 64
DL_W = 1024
DL_STEPS = 64
DL_IN = 10 * DL_W
REL_BUCKETS = 32
REL_MAX_DIST = 1024

ADAM_LR = 0.001
ADAM_B1 = 0.9
ADAM_B2 = 0.999
ADAM_EPS = 1e-08
ADAM_WD = 0.01
ADAM_STEP = 10

VMEM_LIMIT = 56 * 1024 * 1024
LANE = 128
SUBLANE = 8


def _cp(sem=None):
    return pltpu.CompilerParams(dimension_semantics=sem, vmem_limit_bytes=VMEM_LIMIT)


def _mm(a, b):
    return jnp.dot(a.astype(BF16), b.astype(BF16), preferred_element_type=F32)


def _mm_nt(a, b):
    return lax.dot_general(a.astype(BF16), b.astype(BF16), (((1,), (1,)), ((), ())), preferred_element_type=F32)


def _mm_tn(a, b):
    return lax.dot_general(a.astype(BF16), b.astype(BF16), (((0,), (0,)), ((), ())), preferred_element_type=F32)


def _mm_exact(a, b):
    return jnp.dot(a, b, preferred_element_type=F32, precision=lax.Precision.HIGHEST)


def _mm_nt_exact(a, b):
    return lax.dot_general(a, b, (((1,), (1,)), ((), ())), preferred_element_type=F32,
                           precision=lax.Precision.HIGHEST)


def _silu(x):
    return x * jax.nn.sigmoid(x)


def _softplus(z):
    return jnp.maximum(z, 0.0) + jnp.log(1.0 + jnp.exp(-jnp.abs(z)))


def _pick(dim, pref):
    best = None
    t = LANE
    while t <= min(dim, pref):
        if dim % t == 0:
            best = t
        t += LANE
    return best if best is not None else dim


def _const_map(n):
    return lambda *_: (0,) * n


def matmul(name, a, b, mode="nn", res=None, out_dtype=F32, tm=512, tn=1024, tk=1024):
    if mode == "tn":
        K, M = a.shape
    else:
        M, K = a.shape
    N = b.shape[0] if mode == "nt" else b.shape[1]
    tm, tn, tk = _pick(M, tm), _pick(N, tn), _pick(K, tk)
    nk = K // tk
    a_spec = (pl.BlockSpec((tk, tm), lambda i, j, k: (k, i)) if mode == "tn"
              else pl.BlockSpec((tm, tk), lambda i, j, k: (i, k)))
    b_spec = (pl.BlockSpec((tn, tk), lambda i, j, k: (j, k)) if mode == "nt"
              else pl.BlockSpec((tk, tn), lambda i, j, k: (k, j)))
    dot = {"nn": _mm, "nt": _mm_nt, "tn": _mm_tn}[mode]
    has_res = res is not None

    def body(*refs):
        if has_res:
            a_ref, b_ref, r_ref, o_ref, acc = refs
        else:
            a_ref, b_ref, o_ref, acc = refs
        k = pl.program_id(2)

        @pl.when(k == 0)
        def _():
            acc[...] = jnp.zeros_like(acc)

        acc[...] += dot(a_ref[...], b_ref[...])

        @pl.when(k == nk - 1)
        def _():
            out = acc[...]
            if has_res:
                out = out + r_ref[...].astype(F32)
            o_ref[...] = out.astype(o_ref.dtype)

    in_specs = [a_spec, b_spec]
    args = [a, b]
    if has_res:
        in_specs.append(pl.BlockSpec((tm, tn), lambda i, j, k: (i, j)))
        args.append(res)
    return pl.pallas_call(
        body, name=name, grid=(M // tm, N // tn, nk), in_specs=in_specs,
        out_specs=pl.BlockSpec((tm, tn), lambda i, j, k: (i, j)),
        out_shape=jax.ShapeDtypeStruct((M, N), out_dtype),
        scratch_shapes=[pltpu.VMEM((tm, tn), F32)],
        compiler_params=_cp(("parallel", "parallel", "arbitrary")),
    )(*args)


def _col(arr, width, idx):
    return (arr, width, idx)


def rowcall(name, fn, rows, bcs, row_outs, bc_outs=(), tb=256, halo=()):
    rows = [r if isinstance(r, tuple) else (r, r.shape[1], 0) for r in rows]
    S = rows[0][0].shape[0]
    tb = min(tb, S)
    nb = S // tb
    n_r, n_h, n_b, n_ro, n_bo = len(rows), len(halo), len(bcs), len(row_outs), len(bc_outs)
    hb = tb // SUBLANE
    last = S // SUBLANE - 1

    def body(*refs):
        i = pl.program_id(0)
        pos = 0
        r_in = [r[...] for r in refs[pos:pos + n_r]]
        pos += n_r
        h_in = []
        for _ in range(n_h):
            prev = refs[pos][...] * (i > 0).astype(F32)
            nxt = refs[pos + 1][...] * (i < nb - 1).astype(F32)
            h_in += [prev, nxt]
            pos += 2
        b_in = [r[...] for r in refs[pos:pos + n_b]]
        pos += n_b
        ro = refs[pos:pos + n_ro]
        bo = refs[pos + n_ro:]
        outs_r, outs_b = fn(*r_in, *h_in, *b_in)
        for ref, val in zip(ro, outs_r, strict=True):
            ref[...] = val.astype(ref.dtype)
        if n_bo:
            @pl.when(i == 0)
            def _():
                for ref in bo:
                    ref[...] = jnp.zeros_like(ref)

            for ref, val in zip(bo, outs_b, strict=True):
                ref[...] += val

    in_specs = [pl.BlockSpec((tb, w), functools.partial(lambda i, c: (i, c), c=c)) for (_, w, c) in rows]
    args = [a for (a, _, _) in rows]
    for h in halo:
        a, w, c = rows[h]
        in_specs.append(pl.BlockSpec((SUBLANE, w), functools.partial(
            lambda i, c: (jnp.maximum(i * hb - 1, 0), c), c=c)))
        in_specs.append(pl.BlockSpec((SUBLANE, w), functools.partial(
            lambda i, c: (jnp.minimum((i + 1) * hb, last), c), c=c)))
        args += [a, a]
    for b in bcs:
        in_specs.append(pl.BlockSpec(b.shape, _const_map(b.ndim)))
        args.append(b)
    out_specs = [pl.BlockSpec((tb, w), lambda i: (i, 0)) for (w, _) in row_outs]
    out_shape = [jax.ShapeDtypeStruct((S, w), dt) for (w, dt) in row_outs]
    for shp in bc_outs:
        out_specs.append(pl.BlockSpec(shp, _const_map(len(shp))))
        out_shape.append(jax.ShapeDtypeStruct(shp, F32))
    outs = pl.pallas_call(
        body, name=name, grid=(nb,), in_specs=in_specs, out_specs=out_specs, out_shape=out_shape,
        compiler_params=_cp(("arbitrary",) if n_bo else ("parallel",)),
    )(*args)
    return list(outs[:n_ro]), list(outs[n_ro:])


def smallcall(name, fn, ins, out_shapes):
    n_in = len(ins)

    def body(*refs):
        outs = fn(*[r[...] for r in refs[:n_in]])
        for ref, val in zip(refs[n_in:], outs, strict=True):
            ref[...] = val.astype(ref.dtype)

    return pl.pallas_call(
        body, name=name, out_shape=[jax.ShapeDtypeStruct(s, F32) for s in out_shapes],
        compiler_params=_cp(),
    )(*ins)


def _rms(x, g):
    return x * lax.rsqrt(jnp.mean(x * x, axis=-1, keepdims=True) + EPS) * g


def _rms_groups(y, g, width):
    outs = []
    for j in range(y.shape[1] // width):
        sl = slice(j * width, (j + 1) * width)
        outs.append(_rms(y[:, sl], g[:, sl]))
    return jnp.concatenate(outs, axis=1)


def norm_fwd(name, x, g):
    (hn,), _ = rowcall(name, lambda x, g: ((_rms(x, g),), ()), [x], [g], [(D_MODEL, BF16)], tb=512)
    return hn


def norm_bwd(name, x, g, dhn, dres):
    def fn(x, dhn, dres, g):
        _, vjp = jax.vjp(_rms, x, g)
        dx, dg = vjp(dhn)
        return (dx + dres,), (dg,)

    (dx,), (dg,) = rowcall(name, fn, [x, dhn, dres], [g], [(D_MODEL, F32)], [(1, D_MODEL)], tb=512)
    return dx, dg


def loss_head(x, tgt, g):
    def fn(x, tgt, g):
        y, vjp = jax.vjp(_rms, x, g)
        diff = y - tgt
        loss = 0.5 * jnp.sum(jnp.mean(diff * diff, axis=-1, keepdims=True), axis=0, keepdims=True)
        dx, dg = vjp(diff * (1.0 / D_MODEL))
        return (dx,), (jnp.broadcast_to(loss, (1, LANE)), dg)

    (dx,), (loss, dg) = rowcall("loss_head", fn, [x, tgt], [g], [(D_MODEL, F32)],
                                [(1, LANE), (1, D_MODEL)], tb=512)
    return loss, dx, dg


def _shift_rows(x, s):
    if s == 0:
        return x
    return pltpu.roll(x, (-s) % x.shape[0], 0)


def _conv_ext(x, prev, nxt, w):
    xe = jnp.concatenate([prev, x, nxt], axis=0)
    pad = SSD_CONV // 2
    c = jnp.zeros_like(xe)
    for k in range(SSD_CONV):
        c = c + w[k:k + 1, :] * _shift_rows(xe, k - pad)
    return xe, c


def ssd_conv_fwd(u, conv_w, conv_b):
    def fn(x0, x1, x2, p0, n0, p1, n1, p2, n2, w, b):
        tb = x0.shape[0]
        outs = []
        for j, (x, p, n) in enumerate(((x0, p0, n0), (x1, p1, n1), (x2, p2, n2))):
            sl = slice(j * 1024, (j + 1) * 1024)
            _, c = _conv_ext(x, p, n, w[:, sl])
            outs.append(_silu(c[SUBLANE:SUBLANE + tb] + b[:, sl]))
        return (jnp.concatenate(outs, axis=1),), ()

    (xbc,), _ = rowcall("ssd_conv_fwd", fn, [_col(u, 1024, 2), _col(u, 1024, 3), _col(u, 1024, 4)],
                        [conv_w, conv_b], [(SSD_CONV_CH, F32)], tb=256, halo=(0, 1, 2))
    return xbc


def ssd_conv_bwd(u, dxbc, dz, conv_w, conv_b):
    pad = SSD_CONV // 2

    def fn(x0, x1, x2, g0, g1, g2, dz, xp0, xn0, xp1, xn1, xp2, xn2, gp0, gn0, gp1, gn1, gp2, gn2, w, b):
        tb = x0.shape[0]
        blk = slice(SUBLANE, SUBLANE + tb)
        dpre, dws, dbs = [], [], []
        xs = ((x0, xp0, xn0), (x1, xp1, xn1), (x2, xp2, xn2))
        gs = ((g0, gp0, gn0), (g1, gp1, gn1), (g2, gp2, gn2))
        for j in range(3):
            sl = slice(j * 1024, (j + 1) * 1024)
            wj = w[:, sl]
            xe, c = _conv_ext(*xs[j], wj)
            ce = c + b[:, sl]
            sig = jax.nn.sigmoid(ce)
            ge = jnp.concatenate([gs[j][1], gs[j][0], gs[j][2]], axis=0)
            dce = ge * (sig * (1.0 + ce * (1.0 - sig)))
            dx = jnp.zeros_like(xe)
            dw_rows = []
            for k in range(SSD_CONV):
                dx = dx + wj[k:k + 1, :] * _shift_rows(dce, pad - k)
                dw_rows.append(jnp.sum(dce[blk] * _shift_rows(xe, k - pad)[blk], axis=0, keepdims=True))
            dw_rows.append(jnp.zeros_like(dw_rows[0]))
            dpre.append(dx[blk])
            dws.append(jnp.concatenate(dw_rows, axis=0))
            dbs.append(jnp.sum(dce[blk], axis=0, keepdims=True))
        du = jnp.concatenate([dz] + dpre, axis=1)
        return (du,), (jnp.concatenate(dws, axis=1), jnp.concatenate(dbs, axis=1))

    rows = [_col(u, 1024, 2), _col(u, 1024, 3), _col(u, 1024, 4),
            _col(dxbc, 1024, 0), _col(dxbc, 1024, 1), _col(dxbc, 1024, 2), dz]
    (du,), (dw, db) = rowcall("ssd_conv_bwd", fn, rows, [conv_w, conv_b], [(SSD_MAIN, BF16)],
                              [(SUBLANE, SSD_CONV_CH), (1, SSD_CONV_CH)], tb=128, halo=(0, 1, 2, 3, 4, 5))
    return du, dw, db


def _expand_heads(v):
    return jnp.concatenate([jnp.broadcast_to(v[:, j:j + 1], (v.shape[0], SSD_HEADDIM)) for j in range(SSD_HPG)], axis=1)


def _ssd_chunk(rev, st_in, xs, udt, dtb, alog, B, C):
    Q = B.shape[0]
    P = SSD_HEADDIM
    dt = _softplus(udt + dtb)
    a = dt * (-jnp.exp(alog))
    r = lax.broadcasted_iota(jnp.int32, (Q, Q), 0)
    c = lax.broadcasted_iota(jnp.int32, (Q, Q), 1)
    mask = (r <= c) if rev else (r >= c)
    p = _mm_exact(mask.astype(F32), a)
    pT = p.T
    p_e = _expand_heads(p)
    tot_e = p_e[0:1] if rev else p_e[Q - 1:Q]
    xdt = xs * _expand_heads(dt)
    CB = _mm_nt(C, B)
    ys = []
    for j in range(SSD_HPG):
        L = jnp.exp(jnp.where(mask, p[:, j:j + 1] - pT[j:j + 1, :], NEG_BIG))
        ys.append(_mm(CB * L, xdt[:, j * P:(j + 1) * P]))
    y = jnp.concatenate(ys, axis=1) + _mm(C, st_in) * jnp.exp(p_e)
    st_out = st_in * jnp.exp(tot_e) + _mm_tn(B, xdt * jnp.exp(tot_e - p_e))
    return y, st_out


def _ssd_specs(nc, rev_order):
    Q = SSD_CHUNK
    N, P, H, GS = SSD_STATE, SSD_HEADDIM, SSD_HPG, SSD_GPS
    gw = H * P
    nbc = SSD_GROUPS // GS

    def cidx(s):
        return nc - 1 - s if rev_order else s

    xs = pl.BlockSpec((Q, GS * gw), lambda g, s: (cidx(s), g))
    Bs = pl.BlockSpec((Q, GS * N), lambda g, s: (cidx(s), SSD_DI // (GS * N) + g))
    Cs = pl.BlockSpec((Q, GS * N), lambda g, s: (cidx(s), SSD_DI // (GS * N) + nbc + g))
    BC_out = pl.BlockSpec((Q, GS * N), lambda g, s: (cidx(s), g))
    udt = pl.BlockSpec((GS, Q, H), lambda g, s: (g, cidx(s), 0))
    small = pl.BlockSpec((GS, 1, H), lambda g, s: (g, 0, 0))
    st = pl.BlockSpec((GS, 1, N, gw), lambda g, s: (g, cidx(s), 0, 0))
    return xs, Bs, Cs, BC_out, udt, small, st


def ssd_scan_fwd(name, xbc, udt, dtb, alog, rev):
    S = xbc.shape[0]
    Q, N, P, H, GS = SSD_CHUNK, SSD_STATE, SSD_HEADDIM, SSD_HPG, SSD_GPS
    gw = H * P
    nc = S // Q
    xs_s, B_s, C_s, _, udt_s, small_s, st_s = _ssd_specs(nc, rev)

    def body(xs_ref, B_ref, C_ref, udt_ref, dtb_ref, alog_ref, y_ref, st_ref, state):
        @pl.when(pl.program_id(1) == 0)
        def _():
            state[...] = jnp.zeros_like(state)

        for g in range(GS):
            st_ref[g, 0] = state[g]
            y, st_out = _ssd_chunk(rev, state[g], xs_ref[:, g * gw:(g + 1) * gw], udt_ref[g], dtb_ref[g], alog_ref[g],
                                   B_ref[:, g * N:(g + 1) * N], C_ref[:, g * N:(g + 1) * N])
            y_ref[:, g * gw:(g + 1) * gw] = y
            state[g] = st_out

    return pl.pallas_call(
        body, name=name, grid=(SSD_GROUPS // GS, nc),
        in_specs=[xs_s, B_s, C_s, udt_s, small_s, small_s],
        out_specs=[xs_s, st_s],
        out_shape=[jax.ShapeDtypeStruct((S, SSD_DI), F32),
                   jax.ShapeDtypeStruct((SSD_GROUPS, nc, N, gw), F32)],
        scratch_shapes=[pltpu.VMEM((GS, N, gw), F32)],
        compiler_params=_cp(("parallel", "arbitrary")),
    )(xbc, xbc, xbc, udt, dtb, alog)


def ssd_scan_bwd(name, xbc, udt, dtb, alog, states, dy, rev):
    S = xbc.shape[0]
    Q, N, P, H, GS = SSD_CHUNK, SSD_STATE, SSD_HEADDIM, SSD_HPG, SSD_GPS
    gw = H * P
    nc = S // Q
    xs_s, B_s, C_s, BC_out, udt_s, small_s, st_s = _ssd_specs(nc, not rev)

    def body(xs_ref, B_ref, C_ref, udt_ref, dtb_ref, alog_ref, st_ref, dy_ref,
             dx_ref, dB_ref, dC_ref, dudt_ref, ddtb_ref, dalog_ref, dstate):
        @pl.when(pl.program_id(1) == 0)
        def _():
            dstate[...] = jnp.zeros_like(dstate)
            ddtb_ref[...] = jnp.zeros_like(ddtb_ref)
            dalog_ref[...] = jnp.zeros_like(dalog_ref)

        for g in range(GS):
            cols, bc = slice(g * gw, (g + 1) * gw), slice(g * N, (g + 1) * N)
            _, vjp = jax.vjp(functools.partial(_ssd_chunk, rev), st_ref[g, 0], xs_ref[:, cols], udt_ref[g], dtb_ref[g],
                             alog_ref[g], B_ref[:, bc], C_ref[:, bc])
            dst_in, dxs, dudt, ddtb, dalog, dB, dC = vjp((dy_ref[:, cols], dstate[g]))
            dx_ref[:, cols] = dxs
            dB_ref[:, bc] = dB
            dC_ref[:, bc] = dC
            dudt_ref[g] = dudt
            ddtb_ref[g] += ddtb
            dalog_ref[g] += dalog
            dstate[g] = dst_in

    return pl.pallas_call(
        body, name=name, grid=(SSD_GROUPS // GS, nc),
        in_specs=[xs_s, B_s, C_s, udt_s, small_s, small_s, st_s, xs_s],
        out_specs=[xs_s, BC_out, BC_out, udt_s, small_s, small_s],
        out_shape=[jax.ShapeDtypeStruct((S, SSD_DI), F32),
                   jax.ShapeDtypeStruct((S, SSD_GROUPS * N), F32),
                   jax.ShapeDtypeStruct((S, SSD_GROUPS * N), F32),
                   jax.ShapeDtypeStruct((SSD_GROUPS, S, H), F32),
                   jax.ShapeDtypeStruct((SSD_GROUPS, 1, H), F32),
                   jax.ShapeDtypeStruct((SSD_GROUPS, 1, H), F32)],
        scratch_shapes=[pltpu.VMEM((GS, N, gw), F32)],
        compiler_params=_cp(("parallel", "arbitrary")),
    )(xbc, xbc, xbc, udt, dtb, alog, states, dy)


def _ssd_combine(yf, yb, xs, z, dexp, ng):
    y = (yf + yb + xs * dexp) * _silu(z)
    return _rms_groups(y, ng, SSD_DI // SSD_GROUPS)


def ssd_forward(x, hn, w):
    S = x.shape[0]
    u = matmul("ssd_in", hn, w["ssd_w_main"])
    udt = matmul("ssd_in_dt", hn, w["ssd_w_dt"])
    xbc = ssd_conv_fwd(u, w["ssd_conv_w8"], w["ssd_conv_b"])
    udt_t = udt.reshape(S, 2, SSD_GROUPS, SSD_HPG).transpose(1, 2, 0, 3)
    dtb = w["ssd_dt_bias"].reshape(2, SSD_GROUPS, 1, SSD_HPG)
    alog = w["ssd_a_log"].reshape(2, SSD_GROUPS, 1, SSD_HPG)
    yf, stf = ssd_scan_fwd("ssd_scan_f", xbc, udt_t[0], dtb[0], alog[0], False)
    yb, stb = ssd_scan_fwd("ssd_scan_b", xbc, udt_t[1], dtb[1], alog[1], True)
    dexp = jnp.repeat(w["ssd_d"].reshape(1, SSD_HEADS), SSD_HEADDIM, axis=1)
    (yn,), _ = rowcall("ssd_combine", lambda yf, yb, xs, z, d, g: ((_ssd_combine(yf, yb, xs, z, d, g),), ()),
                       [yf, yb, _col(xbc, SSD_DI, 0), _col(u, SSD_DI, 0)], [dexp, w["ssd_norm_g"]],
                       [(SSD_DI, BF16)], tb=256)
    out = matmul("ssd_out", yn, w["ssd_w_out"], res=x)
    saved = dict(hn=hn, u=u, xbc=xbc, udt_t=udt_t, dtb=dtb, alog=alog, yf=yf, yb=yb, stf=stf, stb=stb,
                 dexp=dexp, yn=yn)
    return out, saved


def ssd_backward(dy, sv, w):
    S = dy.shape[0]
    u, xbc = sv["u"], sv["xbc"]
    dyn = matmul("ssd_out_dx", dy, w["ssd_w_out"], mode="nt")
    g_w_out = matmul("ssd_out_dw", sv["yn"], dy, mode="tn")

    def comb_bwd(yf, yb, xs, z, dyn, dexp, ng):
        _, vjp = jax.vjp(_ssd_combine, yf, yb, xs, z, dexp, ng)
        dyf, _, dxs, dz, ddexp, dng = vjp(dyn)
        return (dyf, dxs, dz), (ddexp, dng)

    (dyc, dskip, dz), (ddexp, g_norm) = rowcall(
        "ssd_combine_bwd", comb_bwd, [sv["yf"], sv["yb"], _col(xbc, SSD_DI, 0), _col(u, SSD_DI, 0), dyn],
        [sv["dexp"], w["ssd_norm_g"]], [(SSD_DI, F32)] * 3, [(1, SSD_DI), (1, SSD_DI)], tb=256)
    udt_t, dtb, alog = sv["udt_t"], sv["dtb"], sv["alog"]
    dxf, dBf, dCf, dudt_f, ddtb_f, dalog_f = ssd_scan_bwd("ssd_scan_f_bwd", xbc, udt_t[0], dtb[0], alog[0],
                                                          sv["stf"], dyc, False)
    dxb, dBb, dCb, dudt_b, ddtb_b, dalog_b = ssd_scan_bwd("ssd_scan_b_bwd", xbc, udt_t[1], dtb[1], alog[1],
                                                          sv["stb"], dyc, True)

    def gather(dxf, dxb, dskip, dBf, dBb, dCf, dCb):
        return (jnp.concatenate([dxf + dxb + dskip, dBf + dBb, dCf + dCb], axis=1),), ()

    (dxbc,), _ = rowcall("ssd_dxbc", gather, [dxf, dxb, dskip, dBf, dBb, dCf, dCb], [], [(SSD_CONV_CH, F32)], tb=256)
    du, g_conv_w8, g_conv_b = ssd_conv_bwd(u, dxbc, dz, w["ssd_conv_w8"], w["ssd_conv_b"])
    dudt = jnp.stack([dudt_f, dudt_b]).transpose(2, 0, 1, 3).reshape(S, 2 * SSD_HEADS)
    hn = sv["hn"]
    g_main = matmul("ssd_in_dw", hn, du, mode="tn")
    g_dt = matmul("ssd_in_dt_dw", hn, dudt, mode="tn")
    dhn = matmul("ssd_in_dt_dx", dudt, w["ssd_w_dt"], mode="nt")
    dhn = matmul("ssd_in_dx", du, w["ssd_w_main"], mode="nt", res=dhn)
    grads = dict(
        ssd_w_in=jnp.concatenate([g_main, g_dt], axis=1)[None],
        ssd_conv_w=g_conv_w8[None, :SSD_CONV],
        ssd_conv_b=g_conv_b,
        ssd_dt_bias=jnp.stack([ddtb_f, ddtb_b]).reshape(1, 2, SSD_HEADS),
        ssd_a_log=jnp.stack([dalog_f, dalog_b]).reshape(1, 2, SSD_HEADS),
        ssd_d_exp=ddexp,
        ssd_norm_g=g_norm,
        ssd_w_out=g_w_out[None],
    )
    return dhn, grads


def _hg_block(rev, stTs, uq, uf, ui, lb):
    C = HG_CHUNK
    n = uq.shape[0] // C
    nh = uq.shape[1] // HG_EXPAND
    stTs = list(stTs)
    q = _silu(uq)
    f = lb + (1.0 - lb) * jax.nn.sigmoid(uf)
    k = 1.0 - f
    g = jnp.log(f)
    r = lax.broadcasted_iota(jnp.int32, (C, C), 0)
    c = lax.broadcasted_iota(jnp.int32, (C, C), 1)
    mask = (r <= c) if rev else (r >= c)
    Tm = mask.astype(F32)
    outs = [[None] * n for _ in range(nh)]
    for i in (reversed(range(n)) if rev else range(n)):
        sl = slice(i * C, (i + 1) * C)
        qi, ki, vi = q[sl], k[sl], ui[sl]
        G = _mm_exact(Tm, g[sl])
        Gr = G[C // 2:C // 2 + 1]
        Gl = G[0:1] if rev else G[C - 1:C]
        q_in, k_in = qi * jnp.exp(G - Gr), ki * jnp.exp(Gr - G)
        q_st, k_st, e_l = qi * jnp.exp(G), ki * jnp.exp(Gl - G), jnp.exp(Gl)
        for h in range(nh):
            cs = slice(h * HG_EXPAND, (h + 1) * HG_EXPAND)
            att = jnp.where(mask, _mm_nt(q_in[:, cs], k_in[:, cs]), 0.0)
            outs[h][i] = _mm(att, vi[:, cs]) + _mm_nt(q_st[:, cs], stTs[h])
            stTs[h] = stTs[h] * e_l[:, cs] + _mm_tn(vi[:, cs], k_st[:, cs])
    o = jnp.concatenate([jnp.concatenate(outs[h], axis=0) for h in range(nh)], axis=1)
    return o, stTs


def _hg_specs(nb, rev_order, f_col):
    R = HG_ROWS
    gw = HG_HPS * HG_EXPAND
    ng = HG_HEADS // HG_HPS

    def bidx(s):
        return nb - 1 - s if rev_order else s

    def col(base):
        return pl.BlockSpec((R, gw), lambda h, s: (bidx(s), base * ng + h))

    out = pl.BlockSpec((R, gw), lambda h, s: (bidx(s), h))
    lb = pl.BlockSpec((1, gw), lambda h, s: (0, h))
    st = pl.BlockSpec((1, 1, HG_HPS, HG_EXPAND, HG_EXPAND), lambda h, s: (h, bidx(s), 0, 0, 0))
    return col(0), col(f_col), col(3), out, lb, st


def hg_scan_fwd(name, u, lb, rev):
    S = u.shape[0]
    nb = S // HG_ROWS
    ng = HG_HEADS // HG_HPS
    q_s, f_s, i_s, o_s, lb_s, st_s = _hg_specs(nb, rev, 2 if rev else 1)

    def body(uq, uf, ui, lb_ref, o_ref, st_ref, state):
        @pl.when(pl.program_id(1) == 0)
        def _():
            state[...] = jnp.zeros_like(state)

        st_ref[0, 0] = state[...]
        o, st = _hg_block(rev, [state[h] for h in range(HG_HPS)], uq[...], uf[...], ui[...], lb_ref[...])
        o_ref[...] = o
        for h in range(HG_HPS):
            state[h] = st[h]

    return pl.pallas_call(
        body, name=name, grid=(ng, nb), in_specs=[q_s, f_s, i_s, lb_s], out_specs=[o_s, st_s],
        out_shape=[jax.ShapeDtypeStruct((S, HG_W), F32),
                   jax.ShapeDtypeStruct((ng, nb, HG_HPS, HG_EXPAND, HG_EXPAND), F32)],
        scratch_shapes=[pltpu.VMEM((HG_HPS, HG_EXPAND, HG_EXPAND), F32)],
        compiler_params=_cp(("parallel", "arbitrary")),
    )(u, u, u, lb)


def hg_scan_bwd(name, u, lb, states, do, rev):
    S = u.shape[0]
    nb = S // HG_ROWS
    ng = HG_HEADS // HG_HPS
    q_s, f_s, i_s, o_s, lb_s, st_s = _hg_specs(nb, not rev, 2 if rev else 1)

    def body(uq, uf, ui, lb_ref, st_ref, do_ref, dq_ref, df_ref, di_ref, dlb_ref, dstate):
        @pl.when(pl.program_id(1) == 0)
        def _():
            dstate[...] = jnp.zeros_like(dstate)
            dlb_ref[...] = jnp.zeros_like(dlb_ref)

        _, vjp = jax.vjp(functools.partial(_hg_block, rev), [st_ref[0, 0, h] for h in range(HG_HPS)],
                         uq[...], uf[...], ui[...], lb_ref[...])
        dst, dq, df, di, dlb = vjp((do_ref[...], [dstate[h] for h in range(HG_HPS)]))
        dq_ref[...] = dq
        df_ref[...] = df
        di_ref[...] = di
        dlb_ref[...] += dlb
        for h in range(HG_HPS):
            dstate[h] = dst[h]

    return pl.pallas_call(
        body, name=name, grid=(ng, nb), in_specs=[q_s, f_s, i_s, lb_s, st_s, o_s],
        out_specs=[o_s, o_s, o_s, lb_s],
        out_shape=[jax.ShapeDtypeStruct((S, HG_W), F32)] * 3 + [jax.ShapeDtypeStruct((1, HG_W), F32)],
        scratch_shapes=[pltpu.VMEM((HG_HPS, HG_EXPAND, HG_EXPAND), F32)],
        compiler_params=_cp(("parallel", "arbitrary")),
    )(u, u, u, lb, states, do)


def _hg_lb(hgrn_lb, layer):
    m = jnp.max(hgrn_lb, axis=0, keepdims=True)
    e = jnp.exp(hgrn_lb - m)
    sm = e / jnp.sum(e, axis=0, keepdims=True)
    lb = jnp.zeros_like(sm[0:1])
    for i in range(1, layer + 1):
        lb = lb + sm[i:i + 1]
    return lb


def _hg_combine(of, ob, gate, ng):
    return _rms_groups(of + ob, ng, HG_EXPAND) * _silu(gate)


def hg_forward(x, hn, w, layer):
    u = matmul("hg_in", hn, w["hg_w_in"])
    (lb,) = smallcall("hg_lb", lambda t: (_hg_lb(t, layer),), [w["hgrn_lb"]], [(1, HG_W)])
    of, stf = hg_scan_fwd("hg_scan_f", u, lb, False)
    ob, stb = hg_scan_fwd("hg_scan_b", u, lb, True)
    (og,), _ = rowcall("hg_combine", lambda of, ob, gate, ng: ((_hg_combine(of, ob, gate, ng),), ()),
                       [of, ob, _col(u, HG_W, 4)], [w["hg_norm_g"]], [(HG_W, BF16)], tb=256)
    out = matmul("hg_out", og, w["hg_w_out"], res=x)
    return out, dict(hn=hn, u=u, lb=lb, of=of, ob=ob, stf=stf, stb=stb, og=og)


def hg_backward(dy, sv, w, layer):
    u, lb = sv["u"], sv["lb"]
    dog = matmul("hg_out_dx", dy, w["hg_w_out"], mode="nt")
    g_w_out = matmul("hg_out_dw", sv["og"], dy, mode="tn")

    def comb_bwd(of, ob, gate, dog, ng):
        _, vjp = jax.vjp(_hg_combine, of, ob, gate, ng)
        dof, _, dgate, dng = vjp(dog)
        return (dof, dgate), (dng,)

    (do, dgate), (g_norm,) = rowcall("hg_combine_bwd", comb_bwd, [sv["of"], sv["ob"], _col(u, HG_W, 4), dog],
                                     [w["hg_norm_g"]], [(HG_W, F32)] * 2, [(1, HG_W)], tb=256)
    dqf, dff, dif, dlbf = hg_scan_bwd("hg_scan_f_bwd", u, lb, sv["stf"], do, False)
    dqb, dfb, dib, dlbb = hg_scan_bwd("hg_scan_b_bwd", u, lb, sv["stb"], do, True)

    def gather(dqf, dqb, dff, dfb, dif, dib, dgate):
        return (jnp.concatenate([dqf + dqb, dff, dfb, dif + dib, dgate], axis=1),), ()

    (du,), _ = rowcall("hg_du", gather, [dqf, dqb, dff, dfb, dif, dib, dgate], [], [(HG_IN, BF16)], tb=256)

    def lb_bwd(t, dlbf, dlbb):
        _, vjp = jax.vjp(lambda t: _hg_lb(t, layer), t)
        return vjp(dlbf + dlbb)

    (g_lb,) = smallcall("hg_lb_bwd", lb_bwd, [w["hgrn_lb"], dlbf, dlbb], [(DEPTH, HG_W)])
    hn = sv["hn"]
    g_w_in = matmul("hg_in_dw", hn, du, mode="tn")
    dhn = matmul("hg_in_dx", du, w["hg_w_in"], mode="nt")
    return dhn, dict(hg_w_in=g_w_in[None], hg_norm_g=g_norm, hg_w_out=g_w_out[None], hgrn_lb=g_lb)


def _rope_tables(S):
    t = np.arange(S)
    row = (t // GRID_W).astype(np.float32)
    col = (t % GRID_W).astype(np.float32)
    inv = (ROPE_THETA ** (-np.arange(0, ROPE_AXIS, 2, dtype=np.float32) / ROPE_AXIS)).astype(np.float32)
    ar = jnp.asarray(row)[:, None] * jnp.asarray(inv)[None, :]
    ac = jnp.asarray(col)[:, None] * jnp.asarray(inv)[None, :]
    cos = jnp.concatenate([jnp.cos(ar), jnp.cos(ar), jnp.cos(ac), jnp.cos(ac)], axis=1)
    sin = jnp.concatenate([-jnp.sin(ar), jnp.sin(ar), -jnp.sin(ac), jnp.sin(ac)], axis=1)
    return cos.astype(F32), sin.astype(F32)


def _rope(x, cos, sin):
    h = ROPE_AXIS // 2
    sw = jnp.concatenate([x[:, h:2 * h], x[:, 0:h], x[:, 3 * h:4 * h], x[:, 2 * h:3 * h]], axis=1)
    return x * cos + sw * sin


def _at_pre(uq, uk, cos, sin, qg, kg):
    qs, ks = [], []
    for h in range(AT_HEADS):
        qs.append(_rope(_rms(uq[:, h * AT_HD:(h + 1) * AT_HD], qg), cos, sin) * (AT_HD ** -0.5))
    for h in range(AT_KV):
        ks.append(_rope(_rms(uk[:, h * AT_HD:(h + 1) * AT_HD], kg), cos, sin))
    return jnp.concatenate(qs, axis=1), jnp.concatenate(ks, axis=1)


def _stack_heads(x):
    return jnp.concatenate([x[:, :AT_HD], x[:, AT_HD:]], axis=0)


def _unstack_heads(x):
    t = x.shape[0] // 2
    return jnp.concatenate([x[:t], x[t:]], axis=1)


def at_flash_fwd(q, k, u):
    S = q.shape[0]
    tq, tk = _pick(S, 512), _pick(S, 1024)
    nq, nk = S // tq, S // tk
    gw = AT_GRP * AT_HD

    def body(q_ref, k_ref, v_ref, o_ref, lse_ref, m_s, l_s, acc):
        j = pl.program_id(2)

        @pl.when(j == 0)
        def _():
            m_s[...] = jnp.full_like(m_s, NEG_BIG)
            l_s[...] = jnp.zeros_like(l_s)
            acc[...] = jnp.zeros_like(acc)

        s = _mm_nt(_stack_heads(q_ref[...]), k_ref[...])
        m_new = jnp.maximum(m_s[...], jnp.max(s, axis=-1, keepdims=True))
        alpha = jnp.exp(m_s[...] - m_new)
        p = jnp.exp(s - m_new)
        l_s[...] = alpha * l_s[...] + jnp.sum(p, axis=-1, keepdims=True)
        acc[...] = alpha * acc[...] + _mm(p, v_ref[...])
        m_s[...] = m_new

        @pl.when(j == nk - 1)
        def _():
            o_ref[...] = _unstack_heads(acc[...] / l_s[...])
            lse_ref[0, 0] = m_s[...] + jnp.log(l_s[...])

    return pl.pallas_call(
        body, name="at_flash_fwd", grid=(AT_KV, nq, nk),
        in_specs=[pl.BlockSpec((tq, gw), lambda h, i, j: (i, h)),
                  pl.BlockSpec((tk, AT_HD), lambda h, i, j: (j, h)),
                  pl.BlockSpec((tk, AT_HD), lambda h, i, j: (j, (AT_QW + AT_KW) // AT_HD + h))],
        out_specs=[pl.BlockSpec((tq, gw), lambda h, i, j: (i, h)),
                   pl.BlockSpec((1, 1, 2 * tq, 1), lambda h, i, j: (h, i, 0, 0))],
        out_shape=[jax.ShapeDtypeStruct((S, AT_QW), F32), jax.ShapeDtypeStruct((AT_KV, nq, 2 * tq, 1), F32)],
        scratch_shapes=[pltpu.VMEM((2 * tq, 1), F32), pltpu.VMEM((2 * tq, 1), F32), pltpu.VMEM((2 * tq, AT_HD), F32)],
        compiler_params=_cp(("parallel", "parallel", "arbitrary")),
    )(q, k, u)


def at_flash_bwd(q, k, u, o, lse, do):
    S = q.shape[0]
    tq, tk = _pick(S, 512), _pick(S, 1024)
    nq, nk = S // tq, S // tk
    gw = AT_GRP * AT_HD

    def body(q_ref, k_ref, v_ref, o_ref, lse_ref, do_ref, dq_ref, dk_ref, dv_ref, dk_acc, dv_acc):
        j, i = pl.program_id(1), pl.program_id(2)

        @pl.when(i == 0)
        def _():
            dk_acc[...] = jnp.zeros_like(dk_acc)
            dv_acc[...] = jnp.zeros_like(dv_acc)

        q2 = _stack_heads(q_ref[...])
        do_blk = do_ref[...]
        do2 = _stack_heads(do_blk)
        delta = _stack_heads(do_blk * o_ref[...])
        delta = jnp.sum(delta, axis=-1, keepdims=True)
        kb, vb = k_ref[...], v_ref[...]
        p = jnp.exp(_mm_nt(q2, kb) - lse_ref[0, 0])
        dv_acc[...] += _mm_tn(p, do2)
        ds = p * (_mm_nt(do2, vb) - delta)
        dk_acc[...] += _mm_tn(ds, q2)
        dq = _unstack_heads(_mm(ds, kb))
        rows = pl.ds(pl.multiple_of(i * tq, tq), tq)

        @pl.when(j == 0)
        def _():
            dq_ref[rows, :] = dq

        @pl.when(j > 0)
        def _():
            dq_ref[rows, :] += dq

        @pl.when(i == nq - 1)
        def _():
            dk_ref[...] = dk_acc[...]
            dv_ref[...] = dv_acc[...]

    return pl.pallas_call(
        body, name="at_flash_bwd", grid=(AT_KV, nk, nq),
        in_specs=[pl.BlockSpec((tq, gw), lambda h, j, i: (i, h)),
                  pl.BlockSpec((tk, AT_HD), lambda h, j, i: (j, h)),
                  pl.BlockSpec((tk, AT_HD), lambda h, j, i: (j, (AT_QW + AT_KW) // AT_HD + h)),
                  pl.BlockSpec((tq, gw), lambda h, j, i: (i, h)),
                  pl.BlockSpec((1, 1, 2 * tq, 1), lambda h, j, i: (h, i, 0, 0)),
                  pl.BlockSpec((tq, gw), lambda h, j, i: (i, h))],
        out_specs=[pl.BlockSpec((S, gw), lambda h, j, i: (0, h)),
                   pl.BlockSpec((tk, AT_HD), lambda h, j, i: (j, h)),
                   pl.BlockSpec((tk, AT_HD), lambda h, j, i: (j, h))],
        out_shape=[jax.ShapeDtypeStruct((S, AT_QW), F32), jax.ShapeDtypeStruct((S, AT_KW), F32),
                   jax.ShapeDtypeStruct((S, AT_KW), F32)],
        scratch_shapes=[pltpu.VMEM((tk, AT_HD), F32), pltpu.VMEM((tk, AT_HD), F32)],
        compiler_params=_cp(("parallel", "arbitrary", "arbitrary")),
    )(q, k, u, o, lse, do)


def at_forward(x, hn, w):
    S = x.shape[0]
    u = matmul("at_in", hn, w["at_w_in"])
    cos, sin = _rope_tables(S)
    (q, k), _ = rowcall("at_pre", lambda uq, uk, c, s, qg, kg: (_at_pre(uq, uk, c, s, qg, kg), ()),
                        [_col(u, AT_QW, 0), _col(u, AT_KW, 2), cos, sin], [w["at_q_norm_g"], w["at_k_norm_g"]],
                        [(AT_QW, BF16), (AT_KW, BF16)], tb=256)
    o, lse = at_flash_fwd(q, k, u)
    (og,), _ = rowcall("at_gate", lambda o, gate: ((o * _silu(gate),), ()), [o, _col(u, AT_QW, 2)], [],
                       [(AT_QW, BF16)], tb=256)
    out = matmul("at_out", og, w["at_w_out"], res=x)
    return out, dict(hn=hn, u=u, cos=cos, sin=sin, q=q, k=k, o=o, lse=lse, og=og)


def at_backward(dy, sv, w):
    u = sv["u"]
    dog = matmul("at_out_dx", dy, w["at_w_out"], mode="nt")
    g_w_out = matmul("at_out_dw", sv["og"], dy, mode="tn")

    def gate_bwd(o, gate, dog):
        _, vjp = jax.vjp(lambda o, gate: o * _silu(gate), o, gate)
        return vjp(dog), ()

    (do, dgate), _ = rowcall("at_gate_bwd", gate_bwd, [sv["o"], _col(u, AT_QW, 2), dog], [],
                             [(AT_QW, F32)] * 2, tb=256)
    dq, dk, dv = at_flash_bwd(sv["q"], sv["k"], u, sv["o"], sv["lse"], do)

    def pre_bwd(uq, uk, cos, sin, dq, dk, dv, dgate, qg, kg):
        _, vjp = jax.vjp(lambda uq, uk, qg, kg: _at_pre(uq, uk, cos, sin, qg, kg), uq, uk, qg, kg)
        duq, duk, dqg, dkg = vjp((dq, dk))
        return (jnp.concatenate([duq, duk, dv, dgate], axis=1),), (dqg, dkg)

    (du,), (g_qg, g_kg) = rowcall(
        "at_pre_bwd", pre_bwd, [_col(u, AT_QW, 0), _col(u, AT_KW, 2), sv["cos"], sv["sin"], dq, dk, dv, dgate],
        [w["at_q_norm_g"], w["at_k_norm_g"]], [(AT_IN, BF16)], [(1, AT_HD), (1, AT_HD)], tb=128)
    hn = sv["hn"]
    g_w_in = matmul("at_in_dw", hn, du, mode="tn")
    dhn = matmul("at_in_dx", du, w["at_w_in"], mode="nt")
    return dhn, dict(at_w_in=g_w_in[None], at_q_norm_g=g_qg, at_k_norm_g=g_kg, at_w_out=g_w_out[None])


def _t5_bucket_np(rel):
    half = REL_BUCKETS // 2
    exact = half // 2
    n = np.abs(rel)
    large = exact + (np.log(np.maximum(n, 1).astype(np.float32) / exact)
                     / math.log(REL_MAX_DIST / exact) * (half - exact)).astype(np.int32)
    large = np.minimum(large, half - 1)
    return np.where(rel > 0, half, 0) + np.where(n < exact, n, large)


def _dl_tq(S, dil):
    return min(128, S // dil)


def _dl_bias_maps(tq, dil):
    W = tq + 2 * DL_STEPS
    i = np.arange(tq)[:, None]
    wdx = np.arange(W)[None, :]
    dm = wdx - DL_STEPS - i
    bucket = _t5_bucket_np(dm * dil).reshape(-1).astype(np.int32)
    band = np.where(np.abs(dm) <= DL_STEPS, 0.0, NEG_BIG).reshape(1, -1).astype(np.float32)
    onehot = (jnp.asarray(bucket)[None, :] == jnp.arange(REL_BUCKETS, dtype=jnp.int32)[:, None]).astype(F32)
    return onehot, jnp.asarray(band)


def _dl_attend(q, kwin, vwin, T, valid):
    tq = q.shape[0]
    os, ls = [], []
    for h in range(DL_HEADS):
        sl = slice(h * DL_HD, (h + 1) * DL_HD)
        s = _mm_nt(q[:, sl] * (DL_HD ** -0.5), kwin[:, sl]) + T[h]
        s = jnp.where(valid, s, NEG_BIG)
        m = lax.stop_gradient(jnp.max(s, axis=-1, keepdims=True))
        lse = m + jnp.log(jnp.sum(jnp.exp(s - m), axis=-1, keepdims=True))
        p = jnp.exp(s - lse)
        os.append(_mm(p, vwin[:, sl]))
        ls.append(jnp.broadcast_to(lse, (tq, DL_HD)))
    return jnp.concatenate(os, axis=1), jnp.concatenate(ls, axis=1)


def _dl_specs(gi, dil, tq, Ls):
    nb = Ls // tq
    hs = DL_STEPS
    per = tq // hs
    ncol = DL_IN // DL_W

    def main(c):
        return pl.BlockSpec((tq, DL_W), lambda r, i: (i, r * ncol + gi * 3 + c))

    def prev(c):
        return pl.BlockSpec((hs, DL_W), lambda r, i: (jnp.maximum(i * per - 1, 0), r * ncol + gi * 3 + c))

    def nxt(c):
        return pl.BlockSpec((hs, DL_W), lambda r, i: (jnp.minimum((i + 1) * per, Ls // hs - 1), r * ncol + gi * 3 + c))

    out = pl.BlockSpec((tq, DL_W), lambda r, i: (i, r))
    return nb, main, prev, nxt, out


def _dl_valid(i, tq, Ls):
    W = tq + 2 * DL_STEPS
    mk = i * tq - DL_STEPS + lax.broadcasted_iota(jnp.int32, (1, W), 1)
    return (mk >= 0) & (mk < Ls)


def dl_attn_fwd(gi, dil, u, T):
    S = u.shape[0]
    Ls = S // dil
    tq = _dl_tq(S, dil)
    nb, main, prev, nxt, out = _dl_specs(gi, dil, tq, Ls)
    uv = u.reshape(Ls, dil * DL_IN)

    def body(q_ref, kp, kc, kn, vp, vc, vn, T_ref, o_ref, l_ref):
        kwin = jnp.concatenate([kp[...], kc[...], kn[...]], axis=0)
        vwin = jnp.concatenate([vp[...], vc[...], vn[...]], axis=0)
        o, l = _dl_attend(q_ref[...], kwin, vwin, T_ref[...], _dl_valid(pl.program_id(1), tq, Ls))
        o_ref[...] = o
        l_ref[...] = l

    o, l = pl.pallas_call(
        body, name=f"dl_attn_fwd{gi}", grid=(dil, nb),
        in_specs=[main(0), prev(1), main(1), nxt(1), prev(2), main(2), nxt(2),
                  pl.BlockSpec(T.shape, _const_map(3))],
        out_specs=[out, out],
        out_shape=[jax.ShapeDtypeStruct((Ls, dil * DL_W), F32)] * 2,
        compiler_params=_cp(("parallel", "parallel")),
    )(uv, uv, uv, uv, uv, uv, uv, T)
    return o.reshape(S, DL_W), l.reshape(S, DL_W)


def dl_attn_bwd(gi, dil, u, T, do, dl):
    S = u.shape[0]
    Ls = S // dil
    tq = _dl_tq(S, dil)
    hs = DL_STEPS
    W = tq + 2 * hs
    nb, main, prev, nxt, out = _dl_specs(gi, dil, tq, Ls)
    uv = u.reshape(Ls, dil * DL_IN)
    win = pl.BlockSpec((1, W, DL_W), lambda r, i: (r * nb + i, 0, 0))

    def body(q_ref, kp, kc, kn, vp, vc, vn, T_ref, do_ref, dl_ref, dq_ref, dkw_ref, dvw_ref, dT_ref):
        first = (pl.program_id(0) == 0) & (pl.program_id(1) == 0)

        @pl.when(first)
        def _():
            dT_ref[...] = jnp.zeros_like(dT_ref)

        kwin = jnp.concatenate([kp[...], kc[...], kn[...]], axis=0)
        vwin = jnp.concatenate([vp[...], vc[...], vn[...]], axis=0)
        valid = _dl_valid(pl.program_id(1), tq, Ls)
        _, vjp = jax.vjp(lambda q, k, v, T: _dl_attend(q, k, v, T, valid), q_ref[...], kwin, vwin, T_ref[...])
        dq, dkw, dvw, dT = vjp((do_ref[...], dl_ref[...]))
        dq_ref[...] = dq
        dkw_ref[0] = dkw
        dvw_ref[0] = dvw
        dT_ref[...] += dT

    dq, dkw, dvw, dT = pl.pallas_call(
        body, name=f"dl_attn_bwd{gi}", grid=(dil, nb),
        in_specs=[main(0), prev(1), main(1), nxt(1), prev(2), main(2), nxt(2),
                  pl.BlockSpec(T.shape, _const_map(3)), out, out],
        out_specs=[out, win, win, pl.BlockSpec(T.shape, _const_map(3))],
        out_shape=[jax.ShapeDtypeStruct((Ls, dil * DL_W), F32),
                   jax.ShapeDtypeStruct((dil * nb, W, DL_W), F32),
                   jax.ShapeDtypeStruct((dil * nb, W, DL_W), F32),
                   jax.ShapeDtypeStruct(T.shape, F32)],
        compiler_params=_cp(("arbitrary", "arbitrary")),
    )(uv, uv, uv, uv, uv, uv, uv, T, do.reshape(Ls, dil * DL_W), dl.reshape(Ls, dil * DL_W))

    per = tq // hs

    def fold(kc, kp, kn, vc, vp, vn, dk_ref, dv_ref):
        i = pl.program_id(1)
        has_p = (i > 0).astype(F32)
        has_n = (i < nb - 1).astype(F32)
        for c_ref, p_ref, n_ref, o_ref in ((kc, kp, kn, dk_ref), (vc, vp, vn, dv_ref)):
            o_ref[...] = c_ref[0, hs:hs + tq, :]
            o_ref[0:hs, :] += p_ref[0] * has_p
            o_ref[tq - hs:tq, :] += n_ref[0] * has_n

    wfull = pl.BlockSpec((1, W, DL_W), lambda r, i: (r * nb + i, 0, 0))
    wprev = pl.BlockSpec((1, hs, DL_W), lambda r, i: (r * nb + jnp.maximum(i - 1, 0), per + 1, 0))
    wnext = pl.BlockSpec((1, hs, DL_W), lambda r, i: (r * nb + jnp.minimum(i + 1, nb - 1), 0, 0))
    dk, dv = pl.pallas_call(
        fold, name=f"dl_fold{gi}", grid=(dil, nb),
        in_specs=[wfull, wprev, wnext, wfull, wprev, wnext], out_specs=[out, out],
        out_shape=[jax.ShapeDtypeStruct((Ls, dil * DL_W), F32)] * 2,
        compiler_params=_cp(("parallel", "parallel")),
    )(dkw, dkw, dkw, dvw, dvw, dvw)
    return dq.reshape(S, DL_W), dk.reshape(S, DL_W), dv.reshape(S, DL_W), dT


def _dl_merge(o0, o1, o2, l0, l1, l2, gate):
    m = jnp.maximum(jnp.maximum(l0, l1), l2)
    e0, e1, e2 = jnp.exp(l0 - m), jnp.exp(l1 - m), jnp.exp(l2 - m)
    den = e0 + e1 + e2
    return ((e0 * o0 + e1 * o1 + e2 * o2) / den) * _silu(gate)


def dl_forward(x, hn, w):
    S = x.shape[0]
    u = matmul("dl_in", hn, w["dl_w_in"])
    rbT = w["rel_bias"].T
    os, ls, Ts, maps = [], [], [], []
    for gi, (_, dil) in enumerate(DL_PAIRS):
        tq = _dl_tq(S, dil)
        W = tq + 2 * DL_STEPS
        onehot, band = _dl_bias_maps(tq, dil)
        (T,) = smallcall(f"dl_bias{gi}", lambda rbT, oh, band: (_mm_exact(rbT, oh) + band,), [rbT, onehot, band],
                         [(DL_HEADS, tq * W)])
        T = T.reshape(DL_HEADS, tq, W)
        o, l = dl_attn_fwd(gi, dil, u, T)
        os.append(o)
        ls.append(l)
        Ts.append(T)
        maps.append(onehot)
    (og,), _ = rowcall("dl_merge", lambda *a: ((_dl_merge(*a),), ()), os + ls + [_col(u, DL_W, 9)], [],
                       [(DL_W, BF16)], tb=256)
    out = matmul("dl_out", og, w["dl_w_out"], res=x)
    return out, dict(hn=hn, u=u, os=os, ls=ls, Ts=Ts, maps=maps, og=og)


def dl_backward(dy, sv, w):
    u = sv["u"]
    dog = matmul("dl_out_dx", dy, w["dl_w_out"], mode="nt")
    g_w_out = matmul("dl_out_dw", sv["og"], dy, mode="tn")

    def merge_bwd(o0, o1, o2, l0, l1, l2, gate, dog):
        _, vjp = jax.vjp(_dl_merge, o0, o1, o2, l0, l1, l2, gate)
        return vjp(dog), ()

    grads7, _ = rowcall("dl_merge_bwd", merge_bwd, sv["os"] + sv["ls"] + [_col(u, DL_W, 9), dog], [],
                        [(DL_W, F32)] * 7, tb=256)
    dos, dls, dgate = grads7[0:3], grads7[3:6], grads7[6]
    pieces, g_rbT = [], None
    for gi, (_, dil) in enumerate(DL_PAIRS):
        dq, dk, dv, dT = dl_attn_bwd(gi, dil, u, sv["Ts"][gi], dos[gi], dls[gi])
        pieces += [dq, dk, dv]
        (g,) = smallcall(f"dl_bias_bwd{gi}", lambda dT, oh: (_mm_nt_exact(dT, oh),),
                         [dT.reshape(DL_HEADS, -1), sv["maps"][gi]], [(DL_HEADS, REL_BUCKETS)])
        g_rbT = g if g_rbT is None else g_rbT + g
    pieces.append(dgate)
    (du,), _ = rowcall("dl_du", lambda *a: ((jnp.concatenate(a, axis=1),), ()), pieces, [], [(DL_IN, BF16)], tb=128)
    hn = sv["hn"]
    g_w_in = matmul("dl_in_dw", hn, du, mode="tn")
    dhn = matmul("dl_in_dx", du, w["dl_w_in"], mode="nt")
    return dhn, dict(dl_w_in=g_w_in[None], dl_w_out=g_w_out[None], rel_bias=g_rbT.T)


_FWD = (ssd_forward, hg_forward, at_forward, dl_forward)
_BWD = (ssd_backward, hg_backward, at_backward, dl_backward)


def local_step(x, tgt, w):
    saved = []
    h = x
    for layer in range(DEPTH):
        hn = norm_fwd(f"norm{layer}", h, w["norm_g"][layer:layer + 1])
        extra = (layer,) if layer % 4 == 1 else ()
        h_next, sv = _FWD[layer % 4](h, hn, w, *extra)
        saved.append((h, sv))
        h = h_next
    loss, dh, g_final = loss_head(h, tgt, w["final_g"].reshape(1, D_MODEL))
    grads = {}
    g_norm = [None] * DEPTH
    for layer in reversed(range(DEPTH)):
        h_in, sv = saved[layer]
        extra = (layer,) if layer % 4 == 1 else ()
        dhn, g = _BWD[layer % 4](dh, sv, w, *extra)
        grads.update(g)
        dh, g_norm[layer] = norm_bwd(f"norm{layer}_bwd", h_in, w["norm_g"][layer:layer + 1], dhn, dh)
    grads["norm_g"] = jnp.concatenate(g_norm, axis=0)
    grads["final_g"] = g_final.reshape(D_MODEL)
    grads["ssd_d"] = jnp.sum(grads.pop("ssd_d_exp").reshape(SSD_HEADS, SSD_HEADDIM), axis=1)[None]
    return loss, dh, grads


IN_NAMES = ("ssd_w_in", "hg_w_in", "at_w_in", "dl_w_in")
OUT_NAMES = ("ssd_w_out", "hg_w_out", "at_w_out", "dl_w_out")
IN_COLS = (SSD_IN // 4, HG_IN // 4, AT_IN // 4, DL_IN // 4)
OUT_ROWS = (SSD_DI // 4, HG_W // 4, AT_QW // 4, DL_W // 4)
PACK_IN = sum(IN_COLS)
PACK_OUT = sum(OUT_ROWS)
N_CHIPS = 4
N_DEV = 8
HBM = pl.BlockSpec(memory_space=pl.ANY)


def _mesh_pos():
    return lax.axis_index("x"), lax.axis_index("y"), lax.axis_index("c")


def _other_chips(x, y):
    return [(1 - x, y), (x, 1 - y), (1 - x, 1 - y)]


def gather_weights(p_in, p_out, p_small):
    h_in, h_out = p_in.shape[0] // 2, p_out.shape[0] // 2

    def body(pin, pout, psm, gin, gout, gsm, send, recv):
        x, y, c = _mesh_pos()
        me = 2 * x + y
        sib = (x, y, 1 - c)
        chips = _other_chips(x, y)

        def rows(half, n):
            return pl.ds(pl.multiple_of(half * n, n), n)

        def rc(src, dst, k, to):
            return pltpu.make_async_remote_copy(src_ref=src, dst_ref=dst, send_sem=send.at[k], recv_sem=recv.at[k],
                                                device_id=to, device_id_type=MESH)

        started = []
        for j, (px, py) in enumerate(chips):
            to = (px, py, c)
            started += [rc(pin.at[rows(c, h_in)], gin.at[me, rows(c, h_in)], 3 * j, to),
                        rc(pout.at[rows(c, h_out)], gout.at[me, rows(c, h_out)], 3 * j + 1, to),
                        rc(psm, gsm.at[me], 3 * j + 2, to)]
        for cp in started:
            cp.start()
        for j, (px, py) in enumerate(chips):
            kp = 2 * px + py
            frm = (px, py, c)
            rc(pin.at[rows(c, h_in)], gin.at[kp, rows(c, h_in)], 3 * j, frm).wait_recv()
            f_in = rc(gin.at[kp, rows(c, h_in)], gin.at[kp, rows(c, h_in)], 9 + 2 * j, sib)
            f_in.start()
            rc(pout.at[rows(c, h_out)], gout.at[kp, rows(c, h_out)], 3 * j + 1, frm).wait_recv()
            f_out = rc(gout.at[kp, rows(c, h_out)], gout.at[kp, rows(c, h_out)], 10 + 2 * j, sib)
            f_out.start()
            rc(psm, gsm.at[kp], 3 * j + 2, frm).wait_recv()
            started += [f_in, f_out]
        for j, (px, py) in enumerate(chips):
            kp = 2 * px + py
            rc(gin.at[kp, rows(1 - c, h_in)], gin.at[kp, rows(1 - c, h_in)], 9 + 2 * j, sib).wait_recv()
            rc(gout.at[kp, rows(1 - c, h_out)], gout.at[kp, rows(1 - c, h_out)], 10 + 2 * j, sib).wait_recv()
        for cp in started:
            cp.wait_send()

    return pl.pallas_call(
        body, name="gather_weights", in_specs=[HBM, HBM, HBM], out_specs=[HBM, HBM, HBM],
        out_shape=[jax.ShapeDtypeStruct((N_CHIPS,) + p_in.shape, p_in.dtype),
                   jax.ShapeDtypeStruct((N_CHIPS,) + p_out.shape, p_out.dtype),
                   jax.ShapeDtypeStruct((N_CHIPS,) + p_small.shape, p_small.dtype)],
        scratch_shapes=[pltpu.SemaphoreType.DMA((15,)), pltpu.SemaphoreType.DMA((15,))],
        compiler_params=pltpu.CompilerParams(has_side_effects=True),
    )(p_in, p_out, p_small)


def swap_halves(g_in, g_out):
    h_in, h_out = g_in.shape[1] // 2, g_out.shape[1] // 2

    def body(gi, go, ri, ro, send, recv):
        x, y, c = _mesh_pos()
        sib = (x, y, 1 - c)

        def rows(half, n):
            return pl.ds(pl.multiple_of(half * n, n), n)

        cps = [pltpu.make_async_remote_copy(src_ref=gi.at[:, rows(1 - c, h_in)], dst_ref=ri, send_sem=send.at[0],
                                            recv_sem=recv.at[0], device_id=sib, device_id_type=MESH),
               pltpu.make_async_remote_copy(src_ref=go.at[:, rows(1 - c, h_out)], dst_ref=ro, send_sem=send.at[1],
                                            recv_sem=recv.at[1], device_id=sib, device_id_type=MESH)]
        for cp in cps:
            cp.start()
        for cp in cps:
            cp.wait()

    return pl.pallas_call(
        body, name="swap_halves", in_specs=[HBM, HBM], out_specs=[HBM, HBM],
        out_shape=[jax.ShapeDtypeStruct((N_CHIPS, h_in, g_in.shape[2]), g_in.dtype),
                   jax.ShapeDtypeStruct((N_CHIPS, h_out, g_out.shape[2]), g_out.dtype)],
        scratch_shapes=[pltpu.SemaphoreType.DMA((2,)), pltpu.SemaphoreType.DMA((2,))],
        compiler_params=pltpu.CompilerParams(has_side_effects=True),
    )(g_in, g_out)


def half_add(name, g, r, c_idx, tb):
    _, rows2, C = g.shape
    h = rows2 // 2
    nb = h // tb

    def body(c_ref, g_ref, r_ref, f_ref, b_ref):
        s = g_ref[...] + r_ref[...]
        f_ref[...] = s
        b_ref[...] = s.astype(BF16)

    grid_spec = pltpu.PrefetchScalarGridSpec(
        num_scalar_prefetch=1, grid=(N_CHIPS, nb),
        in_specs=[pl.BlockSpec((1, tb, C), lambda k, i, c: (k, c[0] * nb + i, 0)),
                  pl.BlockSpec((1, tb, C), lambda k, i, c: (k, i, 0))],
        out_specs=[pl.BlockSpec((1, tb, C), lambda k, i, c: (k, i, 0))] * 2)
    return pl.pallas_call(
        body, name=name, grid_spec=grid_spec,
        out_shape=[jax.ShapeDtypeStruct((N_CHIPS, h, C), F32), jax.ShapeDtypeStruct((N_CHIPS, h, C), BF16)],
        compiler_params=_cp(("parallel", "parallel")),
    )(c_idx, g, r)


def scatter_chips(b_in, b_out):
    def body(bi, bo, ri, ro, send, recv):
        x, y, c = _mesh_pos()
        cps = []
        for j, (px, py) in enumerate(_other_chips(x, y)):
            kp = 2 * px + py
            to = (px, py, c)
            cps += [pltpu.make_async_remote_copy(src_ref=bi.at[kp], dst_ref=ri.at[j], send_sem=send.at[2 * j],
                                                 recv_sem=recv.at[2 * j], device_id=to, device_id_type=MESH),
                    pltpu.make_async_remote_copy(src_ref=bo.at[kp], dst_ref=ro.at[j], send_sem=send.at[2 * j + 1],
                                                 recv_sem=recv.at[2 * j + 1], device_id=to, device_id_type=MESH)]
        for cp in cps:
            cp.start()
        for cp in cps:
            cp.wait()

    return pl.pallas_call(
        body, name="scatter_chips", in_specs=[HBM, HBM], out_specs=[HBM, HBM],
        out_shape=[jax.ShapeDtypeStruct((3,) + b_in.shape[1:], BF16), jax.ShapeDtypeStruct((3,) + b_out.shape[1:], BF16)],
        scratch_shapes=[pltpu.SemaphoreType.DMA((6,)), pltpu.SemaphoreType.DMA((6,))],
        compiler_params=pltpu.CompilerParams(has_side_effects=True),
    )(b_in, b_out)


def chip_sum(name, f, r, me_idx, tb):
    _, h, C = f.shape
    nb = h // tb

    def body(me_ref, f_ref, r0, r1, r2, o_ref):
        o_ref[...] = ((f_ref[0] + r0[0].astype(F32)) + r1[0].astype(F32)) + r2[0].astype(F32)

    def slot(j):
        return pl.BlockSpec((1, tb, C), lambda i, me: (j, i, 0))

    grid_spec = pltpu.PrefetchScalarGridSpec(
        num_scalar_prefetch=1, grid=(nb,),
        in_specs=[pl.BlockSpec((1, tb, C), lambda i, me: (me[0], i, 0)), slot(0), slot(1), slot(2)],
        out_specs=pl.BlockSpec((tb, C), lambda i, me: (i, 0)))
    return pl.pallas_call(
        body, name=name, grid_spec=grid_spec, out_shape=jax.ShapeDtypeStruct((h, C), F32),
        compiler_params=_cp(("parallel",)),
    )(me_idx, f, r, r, r)


def share_halves(f_in, f_out):
    def body(fi, fo, oi, oo, send, recv):
        x, y, c = _mesh_pos()
        sib = (x, y, 1 - c)
        cps = [pltpu.make_async_remote_copy(src_ref=fi, dst_ref=oi, send_sem=send.at[0], recv_sem=recv.at[0],
                                            device_id=sib, device_id_type=MESH),
               pltpu.make_async_remote_copy(src_ref=fo, dst_ref=oo, send_sem=send.at[1], recv_sem=recv.at[1],
                                            device_id=sib, device_id_type=MESH)]
        for cp in cps:
            cp.start()
        for cp in cps:
            cp.wait()

    return pl.pallas_call(
        body, name="share_halves", in_specs=[HBM, HBM], out_specs=[HBM, HBM],
        out_shape=[jax.ShapeDtypeStruct(f_in.shape, F32), jax.ShapeDtypeStruct(f_out.shape, F32)],
        scratch_shapes=[pltpu.SemaphoreType.DMA((2,)), pltpu.SemaphoreType.DMA((2,))],
        compiler_params=pltpu.CompilerParams(has_side_effects=True),
    )(f_in, f_out)


def gather_small(pack):
    def body(p, g, send, recv, lsem):
        x, y, c = _mesh_pos()
        me = 4 * x + 2 * y + c
        local = pltpu.make_async_copy(p, g.at[me], lsem)
        local.start()
        cps = []
        k = 0
        for fx in (0, 1):
            for fy in (0, 1):
                for fc in (0, 1):
                    if fx + fy + fc == 0:
                        continue
                    to = (x ^ fx, y ^ fy, c ^ fc)
                    cps.append((pltpu.make_async_remote_copy(src_ref=p, dst_ref=g.at[me], send_sem=send.at[k],
                                                             recv_sem=recv.at[k], device_id=to, device_id_type=MESH), to, k))
                    k += 1
        for cp, _, _ in cps:
            cp.start()
        for cp, to, k in cps:
            frm = 4 * to[0] + 2 * to[1] + to[2]
            pltpu.make_async_remote_copy(src_ref=p, dst_ref=g.at[frm], send_sem=send.at[k], recv_sem=recv.at[k],
                                         device_id=to, device_id_type=MESH).wait_recv()
        for cp, _, _ in cps:
            cp.wait_send()
        local.wait()

    return pl.pallas_call(
        body, name="gather_small", in_specs=[HBM], out_specs=HBM,
        out_shape=jax.ShapeDtypeStruct((N_DEV,) + pack.shape, pack.dtype),
        scratch_shapes=[pltpu.SemaphoreType.DMA((7,)), pltpu.SemaphoreType.DMA((7,)), pltpu.SemaphoreType.DMA],
        compiler_params=pltpu.CompilerParams(has_side_effects=True),
    )(pack)


def _adamw(w, g, m, v):
    m = ADAM_B1 * m + (1.0 - ADAM_B1) * g
    v = ADAM_B2 * v + (1.0 - ADAM_B2) * (g * g)
    m_hat = m / (1.0 - ADAM_B1 ** ADAM_STEP)
    v_hat = v / (1.0 - ADAM_B2 ** ADAM_STEP)
    delta = -ADAM_LR * (m_hat / (jnp.sqrt(v_hat) + ADAM_EPS) + ADAM_WD * w)
    return delta, m, v


def adamw_big(name, w, g, m, v):
    shp = w.shape
    flat = lambda a: a.reshape(shp[-2], shp[-1])
    (d, nm, nv), _ = rowcall(name, lambda w, g, m, v: (_adamw(w, g, m, v), ()), [flat(w), flat(g), flat(m), flat(v)], [],
                             [(shp[-1], F32)] * 3, tb=256)
    return d.reshape(shp), nm.reshape(shp), nv.reshape(shp)


def _pack_small(arrs):
    flat = jnp.concatenate([a.reshape(-1) for a in arrs])
    n = flat.shape[0]
    rows = -(-n // (SUBLANE * LANE)) * SUBLANE
    return jnp.pad(flat, (0, rows * LANE - n)).reshape(rows, LANE)


def _unpack_small(pack, shapes):
    flat = pack.reshape(-1)
    outs, off = [], 0
    for s in shapes:
        n = int(np.prod(s))
        outs.append(flat[off:off + n].reshape(s))
        off += n
    return outs


SMALL_NAMES = ("norm_g", "final_g", "rel_bias", "hgrn_lb", "ssd_conv_w", "ssd_conv_b", "ssd_dt_bias", "ssd_a_log",
               "ssd_d", "ssd_norm_g", "hg_norm_g", "at_q_norm_g", "at_k_norm_g")
ALL_NAMES = ("norm_g", "final_g", "rel_bias", "hgrn_lb", "ssd_w_in", "ssd_conv_w", "ssd_conv_b", "ssd_dt_bias",
             "ssd_a_log", "ssd_d", "ssd_norm_g", "ssd_w_out", "hg_w_in", "hg_norm_g", "hg_w_out", "at_w_in",
             "at_q_norm_g", "at_k_norm_g", "at_w_out", "dl_w_in", "dl_w_out")


def kernel(x, norm_g, final_g, rel_bias, hgrn_lb, ssd_w_in, ssd_conv_w, ssd_conv_b, ssd_dt_bias, ssd_a_log, ssd_d, ssd_norm_g, ssd_w_out, hg_w_in, hg_norm_g, hg_w_out, at_w_in, at_q_norm_g, at_k_norm_g, at_w_out, dl_w_in, dl_w_out, loss_target, m_norm_g, m_final_g, m_rel_bias, m_hgrn_lb, m_ssd_w_in, m_ssd_conv_w, m_ssd_conv_b, m_ssd_dt_bias, m_ssd_a_log, m_ssd_d, m_ssd_norm_g, m_ssd_w_out, m_hg_w_in, m_hg_norm_g, m_hg_w_out, m_at_w_in, m_at_q_norm_g, m_at_k_norm_g, m_at_w_out, m_dl_w_in, m_dl_w_out, v_norm_g, v_final_g, v_rel_bias, v_hgrn_lb, v_ssd_w_in, v_ssd_conv_w, v_ssd_conv_b, v_ssd_dt_bias, v_ssd_a_log, v_ssd_d, v_ssd_norm_g, v_ssd_w_out, v_hg_w_in, v_hg_norm_g, v_hg_w_out, v_at_w_in, v_at_q_norm_g, v_at_k_norm_g, v_at_w_out, v_dl_w_in, v_dl_w_out):
    args = locals()
    W = {n: args[n] for n in ALL_NAMES}
    M = {n: args["m_" + n] for n in ALL_NAMES}
    V = {n: args["v_" + n] for n in ALL_NAMES}
    xi, yi, ci = lax.axis_index("x"), lax.axis_index("y"), lax.axis_index("c")
    chip = 2 * xi + yi
    conv_shard = SSD_CONV_CH // N_CHIPS
    hgn_shard = HG_W // N_CHIPS

    p_in = jnp.concatenate([W[n][0].astype(BF16) for n in IN_NAMES], axis=1)
    p_out = jnp.concatenate([W[n][0].astype(BF16) for n in OUT_NAMES], axis=0)
    p_small = jnp.concatenate([
        jnp.pad(ssd_conv_w[0], ((0, 0), (0, D_MODEL - conv_shard))),
        jnp.pad(hg_norm_g, ((0, 0), (0, D_MODEL - hgn_shard)))], axis=0)
    g_in, g_out, g_small = gather_weights(p_in, p_out, p_small)

    def slot(stack, own, k):
        return jnp.where(chip == k, own, stack[k])

    full = {}
    off = 0
    for n, cols in zip(IN_NAMES, IN_COLS):
        full[n] = jnp.concatenate([slot(g_in, p_in, k)[:, off:off + cols] for k in range(N_CHIPS)], axis=1)
        off += cols
    off = 0
    for n, rows in zip(OUT_NAMES, OUT_ROWS):
        full[n] = jnp.concatenate([slot(g_out, p_out, k)[off:off + rows] for k in range(N_CHIPS)], axis=0)
        off += rows
    conv_full = jnp.concatenate([slot(g_small, p_small, k)[:SSD_CONV, :conv_shard] for k in range(N_CHIPS)], axis=1)
    hgn_full = jnp.concatenate([slot(g_small, p_small, k)[SSD_CONV:SSD_CONV + 1, :hgn_shard] for k in range(N_CHIPS)], axis=1)
    w = dict(
        norm_g=norm_g, final_g=final_g, rel_bias=rel_bias, hgrn_lb=hgrn_lb,
        ssd_w_main=full["ssd_w_in"][:, :SSD_MAIN], ssd_w_dt=full["ssd_w_in"][:, SSD_MAIN:],
        ssd_conv_w8=jnp.concatenate([conv_full, jnp.zeros((1, SSD_CONV_CH), F32)], axis=0),
        ssd_conv_b=ssd_conv_b, ssd_dt_bias=ssd_dt_bias, ssd_a_log=ssd_a_log, ssd_d=ssd_d, ssd_norm_g=ssd_norm_g,
        ssd_w_out=full["ssd_w_out"], hg_w_in=full["hg_w_in"], hg_norm_g=hgn_full, hg_w_out=full["hg_w_out"],
        at_w_in=full["at_w_in"], at_q_norm_g=at_q_norm_g, at_k_norm_g=at_k_norm_g, at_w_out=full["at_w_out"],
        dl_w_in=full["dl_w_in"], dl_w_out=full["dl_w_out"])

    loss_tile, grad_x, grads = local_step(x[0], loss_target[0], w)
    loss = lax.psum(loss_tile[0, 0], ("x", "y", "c"))

    gp_in = jnp.concatenate([grads[n][0].reshape(D_MODEL, N_CHIPS, cols).transpose(1, 0, 2)
                             for n, cols in zip(IN_NAMES, IN_COLS)], axis=2)
    gp_out = jnp.concatenate([grads[n][0].reshape(N_CHIPS, rows, D_MODEL)
                              for n, rows in zip(OUT_NAMES, OUT_ROWS)], axis=1)
    r_in, r_out = swap_halves(gp_in, gp_out)
    c_idx = ci.astype(jnp.int32).reshape(1)
    me_idx = chip.astype(jnp.int32).reshape(1)
    f_in, b_in = half_add("half_add_in", gp_in, r_in, c_idx, 128)
    f_out, b_out = half_add("half_add_out", gp_out, r_out, c_idx, 256)
    x_in, x_out = scatter_chips(b_in, b_out)
    s_in = chip_sum("chip_sum_in", f_in, x_in, me_idx, 128)
    s_out = chip_sum("chip_sum_out", f_out, x_out, me_idx, 256)
    o_in, o_out = share_halves(s_in, s_out)
    red_in = jnp.where(ci == 0, jnp.concatenate([s_in, o_in], axis=0), jnp.concatenate([o_in, s_in], axis=0))
    red_out = jnp.where(ci == 0, jnp.concatenate([s_out, o_out], axis=0), jnp.concatenate([o_out, s_out], axis=0))
    G = {}
    off = 0
    for n, cols in zip(IN_NAMES, IN_COLS):
        G[n] = red_in[:, off:off + cols][None]
        off += cols
    off = 0
    for n, rows in zip(OUT_NAMES, OUT_ROWS):
        G[n] = red_out[off:off + rows][None]
        off += rows

    small_full = [grads[n].reshape(-1) for n in SMALL_NAMES]
    shapes_full = [grads[n].shape for n in SMALL_NAMES]
    packs = gather_small(_pack_small(small_full))
    (red_small,) = smallcall("sum_small", lambda p: (functools.reduce(lambda a, b: a + b, [p[k] for k in range(N_DEV)]),),
                             [packs], [packs.shape[1:]])
    for n, g in zip(SMALL_NAMES, _unpack_small(red_small, shapes_full)):
        G[n] = g
    G["ssd_conv_w"] = lax.dynamic_slice_in_dim(G["ssd_conv_w"].reshape(1, SSD_CONV, SSD_CONV_CH), chip * conv_shard, conv_shard, axis=2)
    G["hg_norm_g"] = lax.dynamic_slice_in_dim(G["hg_norm_g"].reshape(1, HG_W), chip * hgn_shard, hgn_shard, axis=1)
    for n in SMALL_NAMES:
        G[n] = G[n].reshape(W[n].shape)

    D, NM, NV = {}, {}, {}
    for n in IN_NAMES + OUT_NAMES:
        D[n], NM[n], NV[n] = adamw_big("adamw_" + n, W[n], G[n], M[n], V[n])
    shapes = [W[n].shape for n in SMALL_NAMES]
    pk = [_pack_small([T[n] for n in SMALL_NAMES]) for T in (W, G, M, V)]
    outs = smallcall("adamw_small", lambda w, g, m, v: _adamw(w, g, m, v), pk, [pk[0].shape] * 3)
    for T, pack in zip((D, NM, NV), outs):
        for n, a in zip(SMALL_NAMES, _unpack_small(pack, shapes)):
            T[n] = a
    return (loss, grad_x[None], *[G[n] for n in ALL_NAMES], *[D[n] for n in ALL_NAMES],
            *[NM[n] for n in ALL_NAMES], *[NV[n] for n in ALL_NAMES])
```

```python
import functools
import math

import numpy as np
import jax
import jax.numpy as jnp
from jax import lax
from jax.experimental import pallas as pl
from jax.experimental.pallas import tpu as pltpu

F32 = jnp.float32
BF16 = jnp.bfloat16
MESH = pl.DeviceIdType.MESH

D_MODEL = 1024
DEPTH = 4
GRID_W = 64
EPS = 1e-6
NEG_BIG = -1e30

SSD_DI = 2048
SSD_HEADDIM = 64
SSD_HEADS = 32
SSD_GROUPS = 4
SSD_HPG = 8
SSD_STATE = 128
SSD_CONV = 7
SSD_CHUNK = 128
SSD_GPS = 4
SSD_CONV_CH = SSD_DI + 2 * SSD_GROUPS * SSD_STATE
SSD_MAIN = SSD_DI + SSD_CONV_CH
SSD_IN = SSD_MAIN + 2 * SSD_HEADS

HG_HEADS = 8
HG_EXPAND = 128
HG_W = 1024
HG_CHUNK = 32
HG_ROWS = 128
HG_HPS = 8
HG_IN = 5 * HG_W

AT_HEADS = 16
AT_KV = 8
AT_GRP = 2
AT_HD = 128
ROPE_THETA = 10000.0
ROPE_AXIS = 64
AT_QW = AT_HEADS * AT_HD
AT_KW = AT_KV * AT_HD
AT_IN = 2 * AT_QW + 2 * AT_KW

DL_PAIRS = ((128, 1), (512, 4), (2048, 16))
DL_HEADS = 16
DL_HD = 64
DL_W = 1024
DL_STEPS = 64
DL_IN = 10 * DL_W
REL_BUCKETS = 32
REL_MAX_DIST = 1024

ADAM_LR = 0.001
ADAM_B1 = 0.9
ADAM_B2 = 0.999
ADAM_EPS = 1e-08
ADAM_WD = 0.01
ADAM_STEP = 10

VMEM_LIMIT = 56 * 1024 * 1024
LANE = 128
SUBLANE = 8


def _cp(sem=None):
    return pltpu.CompilerParams(dimension_semantics=sem, vmem_limit_bytes=VMEM_LIMIT)


def _mm(a, b):
    return jnp.dot(a.astype(BF16), b.astype(BF16), preferred_element_type=F32)


def _mm_nt(a, b):
    return lax.dot_general(a.astype(BF16), b.astype(BF16), (((1,), (1,)), ((), ())), preferred_element_type=F32)


def _mm_tn(a, b):
    return lax.dot_general(a.astype(BF16), b.astype(BF16), (((0,), (0,)), ((), ())), preferred_element_type=F32)


def _mm_exact(a, b):
    return jnp.dot(a, b, preferred_element_type=F32, precision=lax.Precision.HIGHEST)


def _mm_nt_exact(a, b):
    return lax.dot_general(a, b, (((1,), (1,)), ((), ())), preferred_element_type=F32,
                           precision=lax.Precision.HIGHEST)


def _silu(x):
    return x * jax.nn.sigmoid(x)


def _softplus(z):
    return jnp.maximum(z, 0.0) + jnp.log(1.0 + jnp.exp(-jnp.abs(z)))


def _pick(dim, pref):
    best = None
    t = LANE
    while t <= min(dim, pref):
        if dim % t == 0:
            best = t
        t += LANE
    return best if best is not None else dim


def _const_map(n):
    return lambda *_: (0,) * n


def matmul(name, a, b, mode="nn", res=None, out_dtype=F32, tm=512, tn=1024, tk=1024):
    if mode == "tn":
        K, M = a.shape
    else:
        M, K = a.shape
    N = b.shape[0] if mode == "nt" else b.shape[1]
    tm, tn, tk = _pick(M, tm), _pick(N, tn), _pick(K, tk)
    nk = K // tk
    a_spec = (pl.BlockSpec((tk, tm), lambda i, j, k: (k, i)) if mode == "tn"
              else pl.BlockSpec((tm, tk), lambda i, j, k: (i, k)))
    b_spec = (pl.BlockSpec((tn, tk), lambda i, j, k: (j, k)) if mode == "nt"
              else pl.BlockSpec((tk, tn), lambda i, j, k: (k, j)))
    dot = {"nn": _mm, "nt": _mm_nt, "tn": _mm_tn}[mode]
    has_res = res is not None

    def body(*refs):
        if has_res:
            a_ref, b_ref, r_ref, o_ref, acc = refs
        else:
            a_ref, b_ref, o_ref, acc = refs
        k = pl.program_id(2)

        @pl.when(k == 0)
        def _():
            acc[...] = jnp.zeros_like(acc)

        acc[...] += dot(a_ref[...], b_ref[...])

        @pl.when(k == nk - 1)
        def _():
            out = acc[...]
            if has_res:
                out = out + r_ref[...].astype(F32)
            o_ref[...] = out.astype(o_ref.dtype)

    in_specs = [a_spec, b_spec]
    args = [a, b]
    if has_res:
        in_specs.append(pl.BlockSpec((tm, tn), lambda i, j, k: (i, j)))
        args.append(res)
    return pl.pallas_call(
        body, name=name, grid=(M // tm, N // tn, nk), in_specs=in_specs,
        out_specs=pl.BlockSpec((tm, tn), lambda i, j, k: (i, j)),
        out_shape=jax.ShapeDtypeStruct((M, N), out_dtype),
        scratch_shapes=[pltpu.VMEM((tm, tn), F32)],
        compiler_params=_cp(("parallel", "parallel", "arbitrary")),
    )(*args)


def _col(arr, width, idx):
    return (arr, width, idx)


def _perm(arr, dil, width=None, idx=0):
    return (arr, arr.shape[1] if width is None else width, idx, dil)


def _from_perm(ref, scr, dil):
    n, w = ref.shape[1], ref.shape[2]
    for r in range(dil):
        for j in range(w // LANE):
            scr[j, pl.ds(r, n, stride=dil), :] = ref[r, :, j * LANE:(j + 1) * LANE].astype(F32)
    return jnp.concatenate([scr[j] for j in range(w // LANE)], axis=1)


def _to_perm(val, ref, scr, dil):
    n, w = ref.shape[1], ref.shape[2]
    for j in range(w // LANE):
        scr[j] = val[:, j * LANE:(j + 1) * LANE].astype(F32)
    for r in range(dil):
        ref[r] = jnp.concatenate([scr[j, pl.ds(r, n, stride=dil), :] for j in range(w // LANE)], axis=1).astype(ref.dtype)


def rowcall(name, fn, rows, bcs, row_outs, bc_outs=(), tb=256, halo=()):
    rows = [r if isinstance(r, tuple) else (r, r.shape[1], 0) for r in rows]
    rows = [r if len(r) == 4 else r + (1,) for r in rows]
    row_outs = [o if len(o) == 3 else o + (1,) for o in row_outs]
    S = rows[0][0].shape[0]
    tb = min(tb, S)
    nb = S // tb
    n_r, n_h, n_b, n_ro, n_bo = len(rows), len(halo), len(bcs), len(row_outs), len(bc_outs)
    hb = tb // SUBLANE
    last = S // SUBLANE - 1
    perm_w = max([w for (_, w, _, d) in rows if d > 1] + [w for (w, _, d) in row_outs if d > 1] + [0])

    def body(*refs):
        i = pl.program_id(0)
        scr = refs[-1] if perm_w else None
        pos = 0
        r_in = [r[...] if d == 1 else _from_perm(r, scr, d) for r, (_, _, _, d) in zip(refs[pos:pos + n_r], rows)]
        pos += n_r
        h_in = []
        for _ in range(n_h):
            prev = refs[pos][...] * (i > 0).astype(F32)
            nxt = refs[pos + 1][...] * (i < nb - 1).astype(F32)
            h_in += [prev, nxt]
            pos += 2
        b_in = [r[...] for r in refs[pos:pos + n_b]]
        pos += n_b
        ro = refs[pos:pos + n_ro]
        bo = refs[pos + n_ro:pos + n_ro + n_bo]
        outs_r, outs_b = fn(*r_in, *h_in, *b_in)
        for ref, val, (_, _, d) in zip(ro, outs_r, row_outs, strict=True):
            if d == 1:
                ref[...] = val.astype(ref.dtype)
            else:
                _to_perm(val, ref, scr, d)
        if n_bo:
            @pl.when(i == 0)
            def _():
                for ref in bo:
                    ref[...] = jnp.zeros_like(ref)

            for ref, val in zip(bo, outs_b, strict=True):
                ref[...] += val

    in_specs, args = [], []
    for (a, w, c, d) in rows:
        if d == 1:
            in_specs.append(pl.BlockSpec((tb, w), functools.partial(lambda i, c: (i, c), c=c)))
            args.append(a)
        else:
            in_specs.append(pl.BlockSpec((d, tb // d, w), functools.partial(lambda i, c: (0, i, c), c=c)))
            args.append(a.reshape(d, S // d, a.shape[1]))
    for h in halo:
        a, w, c, _ = rows[h]
        in_specs.append(pl.BlockSpec((SUBLANE, w), functools.partial(
            lambda i, c: (jnp.maximum(i * hb - 1, 0), c), c=c)))
        in_specs.append(pl.BlockSpec((SUBLANE, w), functools.partial(
            lambda i, c: (jnp.minimum((i + 1) * hb, last), c), c=c)))
        args += [a, a]
    for b in bcs:
        in_specs.append(pl.BlockSpec(b.shape, _const_map(b.ndim)))
        args.append(b)
    out_specs, out_shape = [], []
    for (w, dt, d) in row_outs:
        if d == 1:
            out_specs.append(pl.BlockSpec((tb, w), lambda i: (i, 0)))
            out_shape.append(jax.ShapeDtypeStruct((S, w), dt))
        else:
            out_specs.append(pl.BlockSpec((d, tb // d, w), lambda i: (0, i, 0)))
            out_shape.append(jax.ShapeDtypeStruct((d, S // d, w), dt))
    for shp in bc_outs:
        out_specs.append(pl.BlockSpec(shp, _const_map(len(shp))))
        out_shape.append(jax.ShapeDtypeStruct(shp, F32))
    outs = pl.pallas_call(
        body, name=name, grid=(nb,), in_specs=in_specs, out_specs=out_specs, out_shape=out_shape,
        scratch_shapes=[pltpu.VMEM((perm_w // LANE, tb, LANE), F32)] if perm_w else [],
        compiler_params=_cp(("arbitrary",) if n_bo else ("parallel",)),
    )(*args)
    row_res = [o if d == 1 else o.reshape(S, w) for o, (w, _, d) in zip(outs[:n_ro], row_outs)]
    return row_res, list(outs[n_ro:])


def smallcall(name, fn, ins, out_shapes):
    n_in = len(ins)

    def body(*refs):
        outs = fn(*[r[...] for r in refs[:n_in]])
        for ref, val in zip(refs[n_in:], outs, strict=True):
            ref[...] = val.astype(ref.dtype)

    return pl.pallas_call(
        body, name=name, out_shape=[jax.ShapeDtypeStruct(s, F32) for s in out_shapes],
        compiler_params=_cp(),
    )(*ins)


def _rms(x, g):
    return x * lax.rsqrt(jnp.mean(x * x, axis=-1, keepdims=True) + EPS) * g


def _rms_groups(y, g, width):
    outs = []
    for j in range(y.shape[1] // width):
        sl = slice(j * width, (j + 1) * width)
        outs.append(_rms(y[:, sl], g[:, sl]))
    return jnp.concatenate(outs, axis=1)


def norm_fwd(name, x, g, dils=(1,)):
    outs, _ = rowcall(name, lambda x, g: ((_rms(x, g),) * len(dils), ()), [x], [g],
                      [(D_MODEL, BF16, d) for d in dils], tb=512)
    return outs[0] if len(dils) == 1 else tuple(outs)


def norm_bwd(name, x, g, dhn, dres, dils=(1,)):
    parts = dhn if isinstance(dhn, tuple) else (dhn,)
    n = len(parts)

    def fn(x, *rest):
        dh = functools.reduce(lambda a, b: a + b, rest[:n])
        _, vjp = jax.vjp(_rms, x, rest[n + 1])
        dx, dg = vjp(dh)
        return (dx + rest[n],), (dg,)

    rows = [x] + [a if d == 1 else _perm(a, d) for a, d in zip(parts, dils)] + [dres]
    (dx,), (dg,) = rowcall(name, fn, rows, [g], [(D_MODEL, F32)], [(1, D_MODEL)], tb=512)
    return dx, dg


def loss_head(x, tgt, g):
    def fn(x, tgt, g):
        y, vjp = jax.vjp(_rms, x, g)
        diff = y - tgt
        loss = 0.5 * jnp.sum(jnp.mean(diff * diff, axis=-1, keepdims=True), axis=0, keepdims=True)
        dx, dg = vjp(diff * (1.0 / D_MODEL))
        return (dx,), (jnp.broadcast_to(loss, (1, LANE)), dg)

    (dx,), (loss, dg) = rowcall("loss_head", fn, [x, tgt], [g], [(D_MODEL, F32)],
                                [(1, LANE), (1, D_MODEL)], tb=512)
    return loss, dx, dg


def _shift_rows(x, s):
    if s == 0:
        return x
    return pltpu.roll(x, (-s) % x.shape[0], 0)


def _conv_ext(x, prev, nxt, w):
    xe = jnp.concatenate([prev, x, nxt], axis=0)
    pad = SSD_CONV // 2
    c = jnp.zeros_like(xe)
    for k in range(SSD_CONV):
        c = c + w[k:k + 1, :] * _shift_rows(xe, k - pad)
    return xe, c


def ssd_conv_fwd(u, conv_w, conv_b):
    def fn(x0, x1, x2, p0, n0, p1, n1, p2, n2, w, b):
        tb = x0.shape[0]
        outs = []
        for j, (x, p, n) in enumerate(((x0, p0, n0), (x1, p1, n1), (x2, p2, n2))):
            sl = slice(j * 1024, (j + 1) * 1024)
            _, c = _conv_ext(x, p, n, w[:, sl])
            outs.append(_silu(c[SUBLANE:SUBLANE + tb] + b[:, sl]))
        return (jnp.concatenate(outs, axis=1),), ()

    (xbc,), _ = rowcall("ssd_conv_fwd", fn, [_col(u, 1024, 2), _col(u, 1024, 3), _col(u, 1024, 4)],
                        [conv_w, conv_b], [(SSD_CONV_CH, F32)], tb=256, halo=(0, 1, 2))
    return xbc


def ssd_conv_bwd(u, dxbc, dz, conv_w, conv_b):
    pad = SSD_CONV // 2

    def fn(x0, x1, x2, g0, g1, g2, dz, xp0, xn0, xp1, xn1, xp2, xn2, gp0, gn0, gp1, gn1, gp2, gn2, w, b):
        tb = x0.shape[0]
        blk = slice(SUBLANE, SUBLANE + tb)
        dpre, dws, dbs = [], [], []
        xs = ((x0, xp0, xn0), (x1, xp1, xn1), (x2, xp2, xn2))
        gs = ((g0, gp0, gn0), (g1, gp1, gn1), (g2, gp2, gn2))
        for j in range(3):
            sl = slice(j * 1024, (j + 1) * 1024)
            wj = w[:, sl]
            xe, c = _conv_ext(*xs[j], wj)
            ce = c + b[:, sl]
            sig = jax.nn.sigmoid(ce)
            ge = jnp.concatenate([gs[j][1], gs[j][0], gs[j][2]], axis=0)
            dce = ge * (sig * (1.0 + ce * (1.0 - sig)))
            dx = jnp.zeros_like(xe)
            dw_rows = []
            for k in range(SSD_CONV):
                dx = dx + wj[k:k + 1, :] * _shift_rows(dce, pad - k)
                dw_rows.append(jnp.sum(dce[blk] * _shift_rows(xe, k - pad)[blk], axis=0, keepdims=True))
            dw_rows.append(jnp.zeros_like(dw_rows[0]))
            dpre.append(dx[blk])
            dws.append(jnp.concatenate(dw_rows, axis=0))
            dbs.append(jnp.sum(dce[blk], axis=0, keepdims=True))
        du = jnp.concatenate([dz] + dpre, axis=1)
        return (du,), (jnp.concatenate(dws, axis=1), jnp.concatenate(dbs, axis=1))

    rows = [_col(u, 1024, 2), _col(u, 1024, 3), _col(u, 1024, 4),
            _col(dxbc, 1024, 0), _col(dxbc, 1024, 1), _col(dxbc, 1024, 2), dz]
    (du,), (dw, db) = rowcall("ssd_conv_bwd", fn, rows, [conv_w, conv_b], [(SSD_MAIN, BF16)],
                              [(SUBLANE, SSD_CONV_CH), (1, SSD_CONV_CH)], tb=128, halo=(0, 1, 2, 3, 4, 5))
    return du, dw, db


def _expand_heads(v):
    return jnp.concatenate([jnp.broadcast_to(v[:, j:j + 1], (v.shape[0], SSD_HEADDIM)) for j in range(SSD_HPG)], axis=1)


def _ssd_chunk(rev, st_in, xs, udt, dtb, alog, B, C):
    Q = B.shape[0]
    P = SSD_HEADDIM
    dt = _softplus(udt + dtb)
    a = dt * (-jnp.exp(alog))
    r = lax.broadcasted_iota(jnp.int32, (Q, Q), 0)
    c = lax.broadcasted_iota(jnp.int32, (Q, Q), 1)
    mask = (r <= c) if rev else (r >= c)
    p = _mm_exact(mask.astype(F32), a)
    pT = p.T
    p_e = _expand_heads(p)
    tot_e = p_e[0:1] if rev else p_e[Q - 1:Q]
    xdt = xs * _expand_heads(dt)
    CB = _mm_nt(C, B)
    ys = []
    for j in range(SSD_HPG):
        L = jnp.exp(jnp.where(mask, p[:, j:j + 1] - pT[j:j + 1, :], NEG_BIG))
        ys.append(_mm(CB * L, xdt[:, j * P:(j + 1) * P]))
    y = jnp.concatenate(ys, axis=1) + _mm(C, st_in) * jnp.exp(p_e)
    st_out = st_in * jnp.exp(tot_e) + _mm_tn(B, xdt * jnp.exp(tot_e - p_e))
    return y, st_out


def _ssd_specs(nc, rev_order):
    Q = SSD_CHUNK
    N, P, H, GS = SSD_STATE, SSD_HEADDIM, SSD_HPG, SSD_GPS
    gw = H * P
    nbc = SSD_GROUPS // GS

    def cidx(s):
        return nc - 1 - s if rev_order else s

    xs = pl.BlockSpec((Q, GS * gw), lambda g, s: (cidx(s), g))
    Bs = pl.BlockSpec((Q, GS * N), lambda g, s: (cidx(s), SSD_DI // (GS * N) + g))
    Cs = pl.BlockSpec((Q, GS * N), lambda g, s: (cidx(s), SSD_DI // (GS * N) + nbc + g))
    BC_out = pl.BlockSpec((Q, GS * N), lambda g, s: (cidx(s), g))
    udt = pl.BlockSpec((GS, Q, H), lambda g, s: (g, cidx(s), 0))
    small = pl.BlockSpec((GS, 1, H), lambda g, s: (g, 0, 0))
    st = pl.BlockSpec((GS, 1, N, gw), lambda g, s: (g, cidx(s), 0, 0))
    return xs, Bs, Cs, BC_out, udt, small, st


def ssd_scan_fwd(name, xbc, udt, dtb, alog, rev):
    S = xbc.shape[0]
    Q, N, P, H, GS = SSD_CHUNK, SSD_STATE, SSD_HEADDIM, SSD_HPG, SSD_GPS
    gw = H * P
    nc = S // Q
    xs_s, B_s, C_s, _, udt_s, small_s, st_s = _ssd_specs(nc, rev)

    def body(xs_ref, B_ref, C_ref, udt_ref, dtb_ref, alog_ref, y_ref, st_ref, state):
        @pl.when(pl.program_id(1) == 0)
        def _():
            state[...] = jnp.zeros_like(state)

        for g in range(GS):
            st_ref[g, 0] = state[g]
            y, st_out = _ssd_chunk(rev, state[g], xs_ref[:, g * gw:(g + 1) * gw], udt_ref[g], dtb_ref[g], alog_ref[g],
                                   B_ref[:, g * N:(g + 1) * N], C_ref[:, g * N:(g + 1) * N])
            y_ref[:, g * gw:(g + 1) * gw] = y
            state[g] = st_out

    return pl.pallas_call(
        body, name=name, grid=(SSD_GROUPS // GS, nc),
        in_specs=[xs_s, B_s, C_s, udt_s, small_s, small_s],
        out_specs=[xs_s, st_s],
        out_shape=[jax.ShapeDtypeStruct((S, SSD_DI), F32),
                   jax.ShapeDtypeStruct((SSD_GROUPS, nc, N, gw), F32)],
        scratch_shapes=[pltpu.VMEM((GS, N, gw), F32)],
        compiler_params=_cp(("parallel", "arbitrary")),
    )(xbc, xbc, xbc, udt, dtb, alog)


def ssd_scan_bwd(name, xbc, udt, dtb, alog, states, dy, rev):
    S = xbc.shape[0]
    Q, N, P, H, GS = SSD_CHUNK, SSD_STATE, SSD_HEADDIM, SSD_HPG, SSD_GPS
    gw = H * P
    nc = S // Q
    xs_s, B_s, C_s, BC_out, udt_s, small_s, st_s = _ssd_specs(nc, not rev)

    def body(xs_ref, B_ref, C_ref, udt_ref, dtb_ref, alog_ref, st_ref, dy_ref,
             dx_ref, dB_ref, dC_ref, dudt_ref, ddtb_ref, dalog_ref, dstate):
        @pl.when(pl.program_id(1) == 0)
        def _():
            dstate[...] = jnp.zeros_like(dstate)
            ddtb_ref[...] = jnp.zeros_like(ddtb_ref)
            dalog_ref[...] = jnp.zeros_like(dalog_ref)

        for g in range(GS):
            cols, bc = slice(g * gw, (g + 1) * gw), slice(g * N, (g + 1) * N)
            _, vjp = jax.vjp(functools.partial(_ssd_chunk, rev), st_ref[g, 0], xs_ref[:, cols], udt_ref[g], dtb_ref[g],
                             alog_ref[g], B_ref[:, bc], C_ref[:, bc])
            dst_in, dxs, dudt, ddtb, dalog, dB, dC = vjp((dy_ref[:, cols], dstate[g]))
            dx_ref[:, cols] = dxs
            dB_ref[:, bc] = dB
            dC_ref[:, bc] = dC
            dudt_ref[g] = dudt
            ddtb_ref[g] += ddtb
            dalog_ref[g] += dalog
            dstate[g] = dst_in

    return pl.pallas_call(
        body, name=name, grid=(SSD_GROUPS // GS, nc),
        in_specs=[xs_s, B_s, C_s, udt_s, small_s, small_s, st_s, xs_s],
        out_specs=[xs_s, BC_out, BC_out, udt_s, small_s, small_s],
        out_shape=[jax.ShapeDtypeStruct((S, SSD_DI), F32),
                   jax.ShapeDtypeStruct((S, SSD_GROUPS * N), F32),
                   jax.ShapeDtypeStruct((S, SSD_GROUPS * N), F32),
                   jax.ShapeDtypeStruct((SSD_GROUPS, S, H), F32),
                   jax.ShapeDtypeStruct((SSD_GROUPS, 1, H), F32),
                   jax.ShapeDtypeStruct((SSD_GROUPS, 1, H), F32)],
        scratch_shapes=[pltpu.VMEM((GS, N, gw), F32)],
        compiler_params=_cp(("parallel", "arbitrary")),
    )(xbc, xbc, xbc, udt, dtb, alog, states, dy)


def _ssd_combine(yf, yb, xs, z, dexp, ng):
    y = (yf + yb + xs * dexp) * _silu(z)
    return _rms_groups(y, ng, SSD_DI // SSD_GROUPS)


def ssd_forward(x, hn, w):
    S = x.shape[0]
    u = matmul("ssd_in", hn, w["ssd_w_main"])
    udt = matmul("ssd_in_dt", hn, w["ssd_w_dt"])
    xbc = ssd_conv_fwd(u, w["ssd_conv_w8"], w["ssd_conv_b"])
    udt_t = udt.reshape(S, 2, SSD_GROUPS, SSD_HPG).transpose(1, 2, 0, 3)
    dtb = w["ssd_dt_bias"].reshape(2, SSD_GROUPS, 1, SSD_HPG)
    alog = w["ssd_a_log"].reshape(2, SSD_GROUPS, 1, SSD_HPG)
    yf, stf = ssd_scan_fwd("ssd_scan_f", xbc, udt_t[0], dtb[0], alog[0], False)
    yb, stb = ssd_scan_fwd("ssd_scan_b", xbc, udt_t[1], dtb[1], alog[1], True)
    dexp = jnp.repeat(w["ssd_d"].reshape(1, SSD_HEADS), SSD_HEADDIM, axis=1)
    (yn,), _ = rowcall("ssd_combine", lambda yf, yb, xs, z, d, g: ((_ssd_combine(yf, yb, xs, z, d, g),), ()),
                       [yf, yb, _col(xbc, SSD_DI, 0), _col(u, SSD_DI, 0)], [dexp, w["ssd_norm_g"]],
                       [(SSD_DI, BF16)], tb=256)
    out = matmul("ssd_out", yn, w["ssd_w_out"], res=x)
    saved = dict(hn=hn, u=u, xbc=xbc, udt_t=udt_t, dtb=dtb, alog=alog, yf=yf, yb=yb, stf=stf, stb=stb,
                 dexp=dexp, yn=yn)
    return out, saved


def ssd_backward(dy, sv, w):
    S = dy.shape[0]
    u, xbc = sv["u"], sv["xbc"]
    dyn = matmul("ssd_out_dx", dy, w["ssd_w_out"], mode="nt")
    g_w_out = matmul("ssd_out_dw", sv["yn"], dy, mode="tn")

    def comb_bwd(yf, yb, xs, z, dyn, dexp, ng):
        _, vjp = jax.vjp(_ssd_combine, yf, yb, xs, z, dexp, ng)
        dyf, _, dxs, dz, ddexp, dng = vjp(dyn)
        return (dyf, dxs, dz), (ddexp, dng)

    (dyc, dskip, dz), (ddexp, g_norm) = rowcall(
        "ssd_combine_bwd", comb_bwd, [sv["yf"], sv["yb"], _col(xbc, SSD_DI, 0), _col(u, SSD_DI, 0), dyn],
        [sv["dexp"], w["ssd_norm_g"]], [(SSD_DI, F32)] * 3, [(1, SSD_DI), (1, SSD_DI)], tb=256)
    udt_t, dtb, alog = sv["udt_t"], sv["dtb"], sv["alog"]
    dxf, dBf, dCf, dudt_f, ddtb_f, dalog_f = ssd_scan_bwd("ssd_scan_f_bwd", xbc, udt_t[0], dtb[0], alog[0],
                                                          sv["stf"], dyc, False)
    dxb, dBb, dCb, dudt_b, ddtb_b, dalog_b = ssd_scan_bwd("ssd_scan_b_bwd", xbc, udt_t[1], dtb[1], alog[1],
                                                          sv["stb"], dyc, True)

    def gather(dxf, dxb, dskip, dBf, dBb, dCf, dCb):
        return (jnp.concatenate([dxf + dxb + dskip, dBf + dBb, dCf + dCb], axis=1),), ()

    (dxbc,), _ = rowcall("ssd_dxbc", gather, [dxf, dxb, dskip, dBf, dBb, dCf, dCb], [], [(SSD_CONV_CH, F32)], tb=256)
    du, g_conv_w8, g_conv_b = ssd_conv_bwd(u, dxbc, dz, w["ssd_conv_w8"], w["ssd_conv_b"])
    dudt = jnp.stack([dudt_f, dudt_b]).transpose(2, 0, 1, 3).reshape(S, 2 * SSD_HEADS)
    hn = sv["hn"]
    g_main = matmul("ssd_in_dw", hn, du, mode="tn")
    g_dt = matmul("ssd_in_dt_dw", hn, dudt, mode="tn")
    dhn = matmul("ssd_in_dt_dx", dudt, w["ssd_w_dt"], mode="nt")
    dhn = matmul("ssd_in_dx", du, w["ssd_w_main"], mode="nt", res=dhn)
    grads = dict(
        ssd_w_in=jnp.concatenate([g_main, g_dt], axis=1)[None],
        ssd_conv_w=g_conv_w8[None, :SSD_CONV],
        ssd_conv_b=g_conv_b,
        ssd_dt_bias=jnp.stack([ddtb_f, ddtb_b]).reshape(1, 2, SSD_HEADS),
        ssd_a_log=jnp.stack([dalog_f, dalog_b]).reshape(1, 2, SSD_HEADS),
        ssd_d_exp=ddexp,
        ssd_norm_g=g_norm,
        ssd_w_out=g_w_out[None],
    )
    return dhn, grads


def _hg_block(rev, stTs, uq, uf, ui, lb):
    C = HG_CHUNK
    n = uq.shape[0] // C
    nh = uq.shape[1] // HG_EXPAND
    stTs = list(stTs)
    q = _silu(uq)
    f = lb + (1.0 - lb) * jax.nn.sigmoid(uf)
    k = 1.0 - f
    g = jnp.log(f)
    r = lax.broadcasted_iota(jnp.int32, (C, C), 0)
    c = lax.broadcasted_iota(jnp.int32, (C, C), 1)
    mask = (r <= c) if rev else (r >= c)
    Tm = mask.astype(F32)
    outs = [[None] * n for _ in range(nh)]
    for i in (reversed(range(n)) if rev else range(n)):
        sl = slice(i * C, (i + 1) * C)
        qi, ki, vi = q[sl], k[sl], ui[sl]
        G = _mm_exact(Tm, g[sl])
        Gr = G[C // 2:C // 2 + 1]
        Gl = G[0:1] if rev else G[C - 1:C]
        q_in, k_in = qi * jnp.exp(G - Gr), ki * jnp.exp(Gr - G)
        q_st, k_st, e_l = qi * jnp.exp(G), ki * jnp.exp(Gl - G), jnp.exp(Gl)
        for h in range(nh):
            cs = slice(h * HG_EXPAND, (h + 1) * HG_EXPAND)
            att = jnp.where(mask, _mm_nt(q_in[:, cs], k_in[:, cs]), 0.0)
            outs[h][i] = _mm(att, vi[:, cs]) + _mm_nt(q_st[:, cs], stTs[h])
            stTs[h] = stTs[h] * e_l[:, cs] + _mm_tn(vi[:, cs], k_st[:, cs])
    o = jnp.concatenate([jnp.concatenate(outs[h], axis=0) for h in range(nh)], axis=1)
    return o, stTs


def _hg_specs(nb, rev_order, f_col):
    R = HG_ROWS
    gw = HG_HPS * HG_EXPAND
    ng = HG_HEADS // HG_HPS

    def bidx(s):
        return nb - 1 - s if rev_order else s

    def col(base):
        return pl.BlockSpec((R, gw), lambda h, s: (bidx(s), base * ng + h))

    out = pl.BlockSpec((R, gw), lambda h, s: (bidx(s), h))
    lb = pl.BlockSpec((1, gw), lambda h, s: (0, h))
    st = pl.BlockSpec((1, 1, HG_HPS, HG_EXPAND, HG_EXPAND), lambda h, s: (h, bidx(s), 0, 0, 0))
    return col(0), col(f_col), col(3), out, lb, st


def hg_scan_fwd(name, u, lb, rev):
    S = u.shape[0]
    nb = S // HG_ROWS
    ng = HG_HEADS // HG_HPS
    q_s, f_s, i_s, o_s, lb_s, st_s = _hg_specs(nb, rev, 2 if rev else 1)

    def body(uq, uf, ui, lb_ref, o_ref, st_ref, state):
        @pl.when(pl.program_id(1) == 0)
        def _():
            state[...] = jnp.zeros_like(state)

        st_ref[0, 0] = state[...]
        o, st = _hg_block(rev, [state[h] for h in range(HG_HPS)], uq[...], uf[...], ui[...], lb_ref[...])
        o_ref[...] = o
        for h in range(HG_HPS):
            state[h] = st[h]

    return pl.pallas_call(
        body, name=name, grid=(ng, nb), in_specs=[q_s, f_s, i_s, lb_s], out_specs=[o_s, st_s],
        out_shape=[jax.ShapeDtypeStruct((S, HG_W), F32),
                   jax.ShapeDtypeStruct((ng, nb, HG_HPS, HG_EXPAND, HG_EXPAND), F32)],
        scratch_shapes=[pltpu.VMEM((HG_HPS, HG_EXPAND, HG_EXPAND), F32)],
        compiler_params=_cp(("parallel", "arbitrary")),
    )(u, u, u, lb)


def hg_scan_bwd(name, u, lb, states, do, rev):
    S = u.shape[0]
    nb = S // HG_ROWS
    ng = HG_HEADS // HG_HPS
    q_s, f_s, i_s, o_s, lb_s, st_s = _hg_specs(nb, not rev, 2 if rev else 1)

    def body(uq, uf, ui, lb_ref, st_ref, do_ref, dq_ref, df_ref, di_ref, dlb_ref, dstate):
        @pl.when(pl.program_id(1) == 0)
        def _():
            dstate[...] = jnp.zeros_like(dstate)
            dlb_ref[...] = jnp.zeros_like(dlb_ref)

        _, vjp = jax.vjp(functools.partial(_hg_block, rev), [st_ref[0, 0, h] for h in range(HG_HPS)],
                         uq[...], uf[...], ui[...], lb_ref[...])
        dst, dq, df, di, dlb = vjp((do_ref[...], [dstate[h] for h in range(HG_HPS)]))
        dq_ref[...] = dq
        df_ref[...] = df
        di_ref[...] = di
        dlb_ref[...] += dlb
        for h in range(HG_HPS):
            dstate[h] = dst[h]

    return pl.pallas_call(
        body, name=name, grid=(ng, nb), in_specs=[q_s, f_s, i_s, lb_s, st_s, o_s],
        out_specs=[o_s, o_s, o_s, lb_s],
        out_shape=[jax.ShapeDtypeStruct((S, HG_W), F32)] * 3 + [jax.ShapeDtypeStruct((1, HG_W), F32)],
        scratch_shapes=[pltpu.VMEM((HG_HPS, HG_EXPAND, HG_EXPAND), F32)],
        compiler_params=_cp(("parallel", "arbitrary")),
    )(u, u, u, lb, states, do)


def _hg_lb(hgrn_lb, layer):
    m = jnp.max(hgrn_lb, axis=0, keepdims=True)
    e = jnp.exp(hgrn_lb - m)
    sm = e / jnp.sum(e, axis=0, keepdims=True)
    lb = jnp.zeros_like(sm[0:1])
    for i in range(1, layer + 1):
        lb = lb + sm[i:i + 1]
    return lb


def _hg_combine(of, ob, gate, ng):
    return _rms_groups(of + ob, ng, HG_EXPAND) * _silu(gate)


def hg_forward(x, hn, w, layer):
    u = matmul("hg_in", hn, w["hg_w_in"])
    (lb,) = smallcall("hg_lb", lambda t: (_hg_lb(t, layer),), [w["hgrn_lb"]], [(1, HG_W)])
    of, stf = hg_scan_fwd("hg_scan_f", u, lb, False)
    ob, stb = hg_scan_fwd("hg_scan_b", u, lb, True)
    (og,), _ = rowcall("hg_combine", lambda of, ob, gate, ng: ((_hg_combine(of, ob, gate, ng),), ()),
                       [of, ob, _col(u, HG_W, 4)], [w["hg_norm_g"]], [(HG_W, BF16)], tb=256)
    out = matmul("hg_out", og, w["hg_w_out"], res=x)
    return out, dict(hn=hn, u=u, lb=lb, of=of, ob=ob, stf=stf, stb=stb, og=og)


def hg_backward(dy, sv, w, layer):
    u, lb = sv["u"], sv["lb"]
    dog = matmul("hg_out_dx", dy, w["hg_w_out"], mode="nt")
    g_w_out = matmul("hg_out_dw", sv["og"], dy, mode="tn")

    def comb_bwd(of, ob, gate, dog, ng):
        _, vjp = jax.vjp(_hg_combine, of, ob, gate, ng)
        dof, _, dgate, dng = vjp(dog)
        return (dof, dgate), (dng,)

    (do, dgate), (g_norm,) = rowcall("hg_combine_bwd", comb_bwd, [sv["of"], sv["ob"], _col(u, HG_W, 4), dog],
                                     [w["hg_norm_g"]], [(HG_W, F32)] * 2, [(1, HG_W)], tb=256)
    dqf, dff, dif, dlbf = hg_scan_bwd("hg_scan_f_bwd", u, lb, sv["stf"], do, False)
    dqb, dfb, dib, dlbb = hg_scan_bwd("hg_scan_b_bwd", u, lb, sv["stb"], do, True)

    def gather(dqf, dqb, dff, dfb, dif, dib, dgate):
        return (jnp.concatenate([dqf + dqb, dff, dfb, dif + dib, dgate], axis=1),), ()

    (du,), _ = rowcall("hg_du", gather, [dqf, dqb, dff, dfb, dif, dib, dgate], [], [(HG_IN, BF16)], tb=256)

    def lb_bwd(t, dlbf, dlbb):
        _, vjp = jax.vjp(lambda t: _hg_lb(t, layer), t)
        return vjp(dlbf + dlbb)

    (g_lb,) = smallcall("hg_lb_bwd", lb_bwd, [w["hgrn_lb"], dlbf, dlbb], [(DEPTH, HG_W)])
    hn = sv["hn"]
    g_w_in = matmul("hg_in_dw", hn, du, mode="tn")
    dhn = matmul("hg_in_dx", du, w["hg_w_in"], mode="nt")
    return dhn, dict(hg_w_in=g_w_in[None], hg_norm_g=g_norm, hg_w_out=g_w_out[None], hgrn_lb=g_lb)


def _rope_tables(S):
    t = np.arange(S)
    row = (t // GRID_W).astype(np.float32)
    col = (t % GRID_W).astype(np.float32)
    inv = (ROPE_THETA ** (-np.arange(0, ROPE_AXIS, 2, dtype=np.float32) / ROPE_AXIS)).astype(np.float32)
    ar = jnp.asarray(row)[:, None] * jnp.asarray(inv)[None, :]
    ac = jnp.asarray(col)[:, None] * jnp.asarray(inv)[None, :]
    cos = jnp.concatenate([jnp.cos(ar), jnp.cos(ar), jnp.cos(ac), jnp.cos(ac)], axis=1)
    sin = jnp.concatenate([-jnp.sin(ar), jnp.sin(ar), -jnp.sin(ac), jnp.sin(ac)], axis=1)
    return cos.astype(F32), sin.astype(F32)


def _rope(x, cos, sin):
    h = ROPE_AXIS // 2
    sw = jnp.concatenate([x[:, h:2 * h], x[:, 0:h], x[:, 3 * h:4 * h], x[:, 2 * h:3 * h]], axis=1)
    return x * cos + sw * sin


def _at_pre(uq, uk, cos, sin, qg, kg):
    qs, ks = [], []
    for h in range(AT_HEADS):
        qs.append(_rope(_rms(uq[:, h * AT_HD:(h + 1) * AT_HD], qg), cos, sin) * (AT_HD ** -0.5))
    for h in range(AT_KV):
        ks.append(_rope(_rms(uk[:, h * AT_HD:(h + 1) * AT_HD], kg), cos, sin))
    return jnp.concatenate(qs, axis=1), jnp.concatenate(ks, axis=1)


def _stack_heads(x):
    return jnp.concatenate([x[:, :AT_HD], x[:, AT_HD:]], axis=0)


def _unstack_heads(x):
    t = x.shape[0] // 2
    return jnp.concatenate([x[:t], x[t:]], axis=1)


def at_flash_fwd(q, k, u):
    S = q.shape[0]
    tq, tk = _pick(S, 512), _pick(S, 1024)
    nq, nk = S // tq, S // tk
    gw = AT_GRP * AT_HD

    def body(q_ref, k_ref, v_ref, o_ref, lse_ref, m_s, l_s, acc):
        j = pl.program_id(2)

        @pl.when(j == 0)
        def _():
            m_s[...] = jnp.full_like(m_s, NEG_BIG)
            l_s[...] = jnp.zeros_like(l_s)
            acc[...] = jnp.zeros_like(acc)

        s = _mm_nt(_stack_heads(q_ref[...]), k_ref[...])
        m_new = jnp.maximum(m_s[...], jnp.max(s, axis=-1, keepdims=True))
        alpha = jnp.exp(m_s[...] - m_new)
        p = jnp.exp(s - m_new)
        l_s[...] = alpha * l_s[...] + jnp.sum(p, axis=-1, keepdims=True)
        acc[...] = alpha * acc[...] + _mm(p, v_ref[...])
        m_s[...] = m_new

        @pl.when(j == nk - 1)
        def _():
            o_ref[...] = _unstack_heads(acc[...] / l_s[...])
            lse_ref[0, 0] = m_s[...] + jnp.log(l_s[...])

    return pl.pallas_call(
        body, name="at_flash_fwd", grid=(AT_KV, nq, nk),
        in_specs=[pl.BlockSpec((tq, gw), lambda h, i, j: (i, h)),
                  pl.BlockSpec((tk, AT_HD), lambda h, i, j: (j, h)),
                  pl.BlockSpec((tk, AT_HD), lambda h, i, j: (j, (AT_QW + AT_KW) // AT_HD + h))],
        out_specs=[pl.BlockSpec((tq, gw), lambda h, i, j: (i, h)),
                   pl.BlockSpec((1, 1, 2 * tq, 1), lambda h, i, j: (h, i, 0, 0))],
        out_shape=[jax.ShapeDtypeStruct((S, AT_QW), F32), jax.ShapeDtypeStruct((AT_KV, nq, 2 * tq, 1), F32)],
        scratch_shapes=[pltpu.VMEM((2 * tq, 1), F32), pltpu.VMEM((2 * tq, 1), F32), pltpu.VMEM((2 * tq, AT_HD), F32)],
        compiler_params=_cp(("parallel", "parallel", "arbitrary")),
    )(q, k, u)


def at_flash_bwd(q, k, u, o, lse, do):
    S = q.shape[0]
    tq, tk = _pick(S, 512), _pick(S, 1024)
    nq, nk = S // tq, S // tk
    gw = AT_GRP * AT_HD

    def body(q_ref, k_ref, v_ref, o_ref, lse_ref, do_ref, dq_ref, dk_ref, dv_ref, dk_acc, dv_acc):
        j, i = pl.program_id(1), pl.program_id(2)

        @pl.when(i == 0)
        def _():
            dk_acc[...] = jnp.zeros_like(dk_acc)
            dv_acc[...] = jnp.zeros_like(dv_acc)

        q2 = _stack_heads(q_ref[...])
        do_blk = do_ref[...]
        do2 = _stack_heads(do_blk)
        delta = _stack_heads(do_blk * o_ref[...])
        delta = jnp.sum(delta, axis=-1, keepdims=True)
        kb, vb = k_ref[...], v_ref[...]
        p = jnp.exp(_mm_nt(q2, kb) - lse_ref[0, 0])
        dv_acc[...] += _mm_tn(p, do2)
        ds = p * (_mm_nt(do2, vb) - delta)
        dk_acc[...] += _mm_tn(ds, q2)
        dq = _unstack_heads(_mm(ds, kb))
        rows = pl.ds(pl.multiple_of(i * tq, tq), tq)

        @pl.when(j == 0)
        def _():
            dq_ref[rows, :] = dq

        @pl.when(j > 0)
        def _():
            dq_ref[rows, :] += dq

        @pl.when(i == nq - 1)
        def _():
            dk_ref[...] = dk_acc[...]
            dv_ref[...] = dv_acc[...]

    return pl.pallas_call(
        body, name="at_flash_bwd", grid=(AT_KV, nk, nq),
        in_specs=[pl.BlockSpec((tq, gw), lambda h, j, i: (i, h)),
                  pl.BlockSpec((tk, AT_HD), lambda h, j, i: (j, h)),
                  pl.BlockSpec((tk, AT_HD), lambda h, j, i: (j, (AT_QW + AT_KW) // AT_HD + h)),
                  pl.BlockSpec((tq, gw), lambda h, j, i: (i, h)),
                  pl.BlockSpec((1, 1, 2 * tq, 1), lambda h, j, i: (h, i, 0, 0)),
                  pl.BlockSpec((tq, gw), lambda h, j, i: (i, h))],
        out_specs=[pl.BlockSpec((S, gw), lambda h, j, i: (0, h)),
                   pl.BlockSpec((tk, AT_HD), lambda h, j, i: (j, h)),
                   pl.BlockSpec((tk, AT_HD), lambda h, j, i: (j, h))],
        out_shape=[jax.ShapeDtypeStruct((S, AT_QW), F32), jax.ShapeDtypeStruct((S, AT_KW), F32),
                   jax.ShapeDtypeStruct((S, AT_KW), F32)],
        scratch_shapes=[pltpu.VMEM((tk, AT_HD), F32), pltpu.VMEM((tk, AT_HD), F32)],
        compiler_params=_cp(("parallel", "arbitrary", "arbitrary")),
    )(q, k, u, o, lse, do)


def at_forward(x, hn, w):
    S = x.shape[0]
    u = matmul("at_in", hn, w["at_w_in"])
    cos, sin = _rope_tables(S)
    (q, k), _ = rowcall("at_pre", lambda uq, uk, c, s, qg, kg: (_at_pre(uq, uk, c, s, qg, kg), ()),
                        [_col(u, AT_QW, 0), _col(u, AT_KW, 2), cos, sin], [w["at_q_norm_g"], w["at_k_norm_g"]],
                        [(AT_QW, BF16), (AT_KW, BF16)], tb=256)
    o, lse = at_flash_fwd(q, k, u)
    (og,), _ = rowcall("at_gate", lambda o, gate: ((o * _silu(gate),), ()), [o, _col(u, AT_QW, 2)], [],
                       [(AT_QW, BF16)], tb=256)
    out = matmul("at_out", og, w["at_w_out"], res=x)
    return out, dict(hn=hn, u=u, cos=cos, sin=sin, q=q, k=k, o=o, lse=lse, og=og)


def at_backward(dy, sv, w):
    u = sv["u"]
    dog = matmul("at_out_dx", dy, w["at_w_out"], mode="nt")
    g_w_out = matmul("at_out_dw", sv["og"], dy, mode="tn")

    def gate_bwd(o, gate, dog):
        _, vjp = jax.vjp(lambda o, gate: o * _silu(gate), o, gate)
        return vjp(dog), ()

    (do, dgate), _ = rowcall("at_gate_bwd", gate_bwd, [sv["o"], _col(u, AT_QW, 2), dog], [],
                             [(AT_QW, F32)] * 2, tb=256)
    dq, dk, dv = at_flash_bwd(sv["q"], sv["k"], u, sv["o"], sv["lse"], do)

    def pre_bwd(uq, uk, cos, sin, dq, dk, dv, dgate, qg, kg):
        _, vjp = jax.vjp(lambda uq, uk, qg, kg: _at_pre(uq, uk, cos, sin, qg, kg), uq, uk, qg, kg)
        duq, duk, dqg, dkg = vjp((dq, dk))
        return (jnp.concatenate([duq, duk, dv, dgate], axis=1),), (dqg, dkg)

    (du,), (g_qg, g_kg) = rowcall(
        "at_pre_bwd", pre_bwd, [_col(u, AT_QW, 0), _col(u, AT_KW, 2), sv["cos"], sv["sin"], dq, dk, dv, dgate],
        [w["at_q_norm_g"], w["at_k_norm_g"]], [(AT_IN, BF16)], [(1, AT_HD), (1, AT_HD)], tb=128)
    hn = sv["hn"]
    g_w_in = matmul("at_in_dw", hn, du, mode="tn")
    dhn = matmul("at_in_dx", du, w["at_w_in"], mode="nt")
    return dhn, dict(at_w_in=g_w_in[None], at_q_norm_g=g_qg, at_k_norm_g=g_kg, at_w_out=g_w_out[None])


def _t5_bucket_np(rel):
    half = REL_BUCKETS // 2
    exact = half // 2
    n = np.abs(rel)
    large = exact + (np.log(np.maximum(n, 1).astype(np.float32) / exact)
                     / math.log(REL_MAX_DIST / exact) * (half - exact)).astype(np.int32)
    large = np.minimum(large, half - 1)
    return np.where(rel > 0, half, 0) + np.where(n < exact, n, large)


def _dl_tq(S, dil):
    return min(128, S // dil)


def _dl_bias_maps(tq, dil):
    W = tq + 2 * DL_STEPS
    i = np.arange(tq)[:, None]
    wdx = np.arange(W)[None, :]
    dm = wdx - DL_STEPS - i
    bucket = _t5_bucket_np(dm * dil).reshape(-1).astype(np.int32)
    band = np.where(np.abs(dm) <= DL_STEPS, 0.0, NEG_BIG).reshape(1, -1).astype(np.float32)
    onehot = (jnp.asarray(bucket)[None, :] == jnp.arange(REL_BUCKETS, dtype=jnp.int32)[:, None]).astype(F32)
    return onehot, jnp.asarray(band)


def _dl_attend(q, kwin, vwin, T, valid):
    tq = q.shape[0]
    os, ls = [], []
    for h in range(DL_HEADS):
        sl = slice(h * DL_HD, (h + 1) * DL_HD)
        s = _mm_nt(q[:, sl] * (DL_HD ** -0.5), kwin[:, sl]) + T[h]
        s = jnp.where(valid, s, NEG_BIG)
        m = lax.stop_gradient(jnp.max(s, axis=-1, keepdims=True))
        lse = m + jnp.log(jnp.sum(jnp.exp(s - m), axis=-1, keepdims=True))
        p = jnp.exp(s - lse)
        os.append(_mm(p, vwin[:, sl]))
        ls.append(jnp.broadcast_to(lse, (tq, DL_HD)))
    return jnp.concatenate(os, axis=1), jnp.concatenate(ls, axis=1)


def _dl_specs(tq, Ls):
    nb = Ls // tq
    hs = DL_STEPS
    per = tq // hs
    nh = Ls // hs

    def main(c):
        return pl.BlockSpec((tq, DL_W), lambda r, i: (r * nb + i, c))

    def prev(c):
        return pl.BlockSpec((hs, DL_W), lambda r, i: (r * nh + jnp.maximum(i * per - 1, 0), c))

    def nxt(c):
        return pl.BlockSpec((hs, DL_W), lambda r, i: (r * nh + jnp.minimum((i + 1) * per, nh - 1), c))

    return nb, main, prev, nxt


def _dl_valid(i, tq, Ls):
    W = tq + 2 * DL_STEPS
    mk = i * tq - DL_STEPS + lax.broadcasted_iota(jnp.int32, (1, W), 1)
    return (mk >= 0) & (mk < Ls)


def dl_attn_fwd(gi, dil, u, T):
    S = u.shape[0]
    Ls = S // dil
    tq = _dl_tq(S, dil)
    nb, main, prev, nxt = _dl_specs(tq, Ls)
    out = main(0)

    def body(q_ref, kp, kc, kn, vp, vc, vn, T_ref, o_ref, l_ref):
        kwin = jnp.concatenate([kp[...], kc[...], kn[...]], axis=0)
        vwin = jnp.concatenate([vp[...], vc[...], vn[...]], axis=0)
        o, l = _dl_attend(q_ref[...], kwin, vwin, T_ref[...], _dl_valid(pl.program_id(1), tq, Ls))
        o_ref[...] = o
        l_ref[...] = l

    o, l = pl.pallas_call(
        body, name=f"dl_attn_fwd{gi}", grid=(dil, nb),
        in_specs=[main(0), prev(1), main(1), nxt(1), prev(2), main(2), nxt(2),
                  pl.BlockSpec(T.shape, _const_map(3))],
        out_specs=[out, out],
        out_shape=[jax.ShapeDtypeStruct((S, DL_W), F32)] * 2,
        compiler_params=_cp(("parallel", "parallel")),
    )(u, u, u, u, u, u, u, T)
    return o, l


def dl_attn_bwd(gi, dil, u, T, do, dl, dgate=None):
    S = u.shape[0]
    Ls = S // dil
    tq = _dl_tq(S, dil)
    hs = DL_STEPS
    W = tq + 2 * hs
    nb, main, prev, nxt = _dl_specs(tq, Ls)
    out = main(0)
    win = pl.BlockSpec((1, W, DL_W), lambda r, i: (r * nb + i, 0, 0))

    def body(q_ref, kp, kc, kn, vp, vc, vn, T_ref, do_ref, dl_ref, dq_ref, dkw_ref, dvw_ref, dT_ref):
        first = (pl.program_id(0) == 0) & (pl.program_id(1) == 0)

        @pl.when(first)
        def _():
            dT_ref[...] = jnp.zeros_like(dT_ref)

        kwin = jnp.concatenate([kp[...], kc[...], kn[...]], axis=0)
        vwin = jnp.concatenate([vp[...], vc[...], vn[...]], axis=0)
        valid = _dl_valid(pl.program_id(1), tq, Ls)
        _, vjp = jax.vjp(lambda q, k, v, T: _dl_attend(q, k, v, T, valid), q_ref[...], kwin, vwin, T_ref[...])
        dq, dkw, dvw, dT = vjp((do_ref[...], dl_ref[...]))
        dq_ref[...] = dq
        dkw_ref[0] = dkw
        dvw_ref[0] = dvw
        dT_ref[...] += dT

    dq, dkw, dvw, dT = pl.pallas_call(
        body, name=f"dl_attn_bwd{gi}", grid=(dil, nb),
        in_specs=[main(0), prev(1), main(1), nxt(1), prev(2), main(2), nxt(2),
                  pl.BlockSpec(T.shape, _const_map(3)), out, out],
        out_specs=[out, win, win, pl.BlockSpec(T.shape, _const_map(3))],
        out_shape=[jax.ShapeDtypeStruct((S, DL_W), F32),
                   jax.ShapeDtypeStruct((dil * nb, W, DL_W), F32),
                   jax.ShapeDtypeStruct((dil * nb, W, DL_W), F32),
                   jax.ShapeDtypeStruct(T.shape, F32)],
        compiler_params=_cp(("arbitrary", "arbitrary")),
    )(u, u, u, u, u, u, u, T, do, dl)

    per = tq // hs
    n_out = 3 if dgate is None else 4

    def fold(*refs):
        dq_ref, kc, kp, kn, vc, vp, vn = refs[:7]
        du_ref = refs[-1]
        i = pl.program_id(1)
        has_p = (i > 0).astype(F32)
        has_n = (i < nb - 1).astype(F32)
        du_ref[:, 0:DL_W] = dq_ref[...].astype(BF16)
        for c, (c_ref, p_ref, n_ref) in enumerate(((kc, kp, kn), (vc, vp, vn)), start=1):
            mid = c_ref[0, hs:hs + tq, :]
            top = mid[0:hs] + p_ref[0] * has_p
            bot = mid[tq - hs:tq] + n_ref[0] * has_n
            parts = [top, bot] if tq == 2 * hs else ([top, mid[hs:tq - hs], bot] if tq > 2 * hs else [top + n_ref[0] * has_n])
            du_ref[:, c * DL_W:(c + 1) * DL_W] = jnp.concatenate(parts, axis=0).astype(BF16)
        if dgate is not None:
            du_ref[:, 3 * DL_W:4 * DL_W] = refs[7][...].astype(BF16)

    wfull = pl.BlockSpec((1, W, DL_W), lambda r, i: (r * nb + i, 0, 0))
    wprev = pl.BlockSpec((1, hs, DL_W), lambda r, i: (r * nb + jnp.maximum(i - 1, 0), per + 1, 0))
    wnext = pl.BlockSpec((1, hs, DL_W), lambda r, i: (r * nb + jnp.minimum(i + 1, nb - 1), 0, 0))
    extra_specs, extra_args = ([], []) if dgate is None else ([out], [dgate])
    du = pl.pallas_call(
        fold, name=f"dl_fold{gi}", grid=(dil, nb),
        in_specs=[out, wfull, wprev, wnext, wfull, wprev, wnext] + extra_specs,
        out_specs=pl.BlockSpec((tq, n_out * DL_W), lambda r, i: (r * nb + i, 0)),
        out_shape=jax.ShapeDtypeStruct((S, n_out * DL_W), BF16),
        compiler_params=_cp(("parallel", "parallel")),
    )(dq, dkw, dkw, dkw, dvw, dvw, dvw, *extra_args)
    return du, dT


def _dl_merge(o0, o1, o2, l0, l1, l2, gate):
    m = jnp.maximum(jnp.maximum(l0, l1), l2)
    e0, e1, e2 = jnp.exp(l0 - m), jnp.exp(l1 - m), jnp.exp(l2 - m)
    den = e0 + e1 + e2
    return ((e0 * o0 + e1 * o1 + e2 * o2) / den) * _silu(gate)


DL_DILS = tuple(d for _, d in DL_PAIRS)


def _dl_group_weights(w_in):
    g3 = 3 * DL_W
    return [jnp.concatenate([w_in[:, :g3], w_in[:, 3 * g3:]], axis=1), w_in[:, g3:2 * g3], w_in[:, 2 * g3:3 * g3]]


def dl_forward(x, hns, w):
    S = x.shape[0]
    wg = _dl_group_weights(w["dl_w_in"])
    rbT = w["rel_bias"].T
    us, os, ls, Ts, maps = [], [], [], [], []
    for gi, dil in enumerate(DL_DILS):
        u = matmul(f"dl_in{gi}", hns[gi], wg[gi])
        tq = _dl_tq(S, dil)
        W = tq + 2 * DL_STEPS
        onehot, band = _dl_bias_maps(tq, dil)
        (T,) = smallcall(f"dl_bias{gi}", lambda rbT, oh, band: (_mm_exact(rbT, oh) + band,), [rbT, onehot, band],
                         [(DL_HEADS, tq * W)])
        T = T.reshape(DL_HEADS, tq, W)
        o, l = dl_attn_fwd(gi, dil, u, T)
        us.append(u)
        os.append(o)
        ls.append(l)
        Ts.append(T)
        maps.append(onehot)
    rows = [a if d == 1 else _perm(a, d) for a, d in zip(os + ls, DL_DILS * 2)] + [_col(us[0], DL_W, 3)]
    (og,), _ = rowcall("dl_merge", lambda *a: ((_dl_merge(*a),), ()), rows, [], [(DL_W, BF16)], tb=256)
    out = matmul("dl_out", og, w["dl_w_out"], res=x)
    return out, dict(hns=hns, us=us, os=os, ls=ls, Ts=Ts, maps=maps, og=og, wg=wg)


def dl_backward(dy, sv, w):
    us = sv["us"]
    dog = matmul("dl_out_dx", dy, w["dl_w_out"], mode="nt")
    g_w_out = matmul("dl_out_dw", sv["og"], dy, mode="tn")

    def merge_bwd(o0, o1, o2, l0, l1, l2, gate, dog):
        _, vjp = jax.vjp(_dl_merge, o0, o1, o2, l0, l1, l2, gate)
        return vjp(dog), ()

    rows = [a if d == 1 else _perm(a, d) for a, d in zip(sv["os"] + sv["ls"], DL_DILS * 2)] + [_col(us[0], DL_W, 3), dog]
    grads7, _ = rowcall("dl_merge_bwd", merge_bwd, rows, [], [(DL_W, F32, d) for d in DL_DILS * 2] + [(DL_W, F32)], tb=256)
    dos, dls, dgate = grads7[0:3], grads7[3:6], grads7[6]
    g_rbT, g_ws, dhns = None, [], []
    for gi, dil in enumerate(DL_DILS):
        du, dT = dl_attn_bwd(gi, dil, us[gi], sv["Ts"][gi], dos[gi], dls[gi], dgate if gi == 0 else None)
        (g,) = smallcall(f"dl_bias_bwd{gi}", lambda dT, oh: (_mm_nt_exact(dT, oh),),
                         [dT.reshape(DL_HEADS, -1), sv["maps"][gi]], [(DL_HEADS, REL_BUCKETS)])
        g_rbT = g if g_rbT is None else g_rbT + g
        g_ws.append(matmul(f"dl_in_dw{gi}", sv["hns"][gi], du, mode="tn"))
        dhns.append(matmul(f"dl_in_dx{gi}", du, sv["wg"][gi], mode="nt"))
    g3 = 3 * DL_W
    g_w_in = jnp.concatenate([g_ws[0][:, :g3], g_ws[1], g_ws[2], g_ws[0][:, g3:]], axis=1)
    return tuple(dhns), dict(dl_w_in=g_w_in[None], dl_w_out=g_w_out[None], rel_bias=g_rbT.T)


_FWD = (ssd_forward, hg_forward, at_forward, dl_forward)
_BWD = (ssd_backward, hg_backward, at_backward, dl_backward)


def _norm_dils(layer):
    return DL_DILS if layer % 4 == 3 else (1,)


def local_step(x, tgt, w):
    saved = []
    h = x
    for layer in range(DEPTH):
        hn = norm_fwd(f"norm{layer}", h, w["norm_g"][layer:layer + 1], _norm_dils(layer))
        extra = (layer,) if layer % 4 == 1 else ()
        h_next, sv = _FWD[layer % 4](h, hn, w, *extra)
        saved.append((h, sv))
        h = h_next
    loss, dh, g_final = loss_head(h, tgt, w["final_g"].reshape(1, D_MODEL))
    grads = {}
    g_norm = [None] * DEPTH
    for layer in reversed(range(DEPTH)):
        h_in, sv = saved[layer]
        extra = (layer,) if layer % 4 == 1 else ()
        dhn, g = _BWD[layer % 4](dh, sv, w, *extra)
        grads.update(g)
        dh, g_norm[layer] = norm_bwd(f"norm{layer}_bwd", h_in, w["norm_g"][layer:layer + 1], dhn, dh, _norm_dils(layer))
    grads["norm_g"] = jnp.concatenate(g_norm, axis=0)
    grads["final_g"] = g_final.reshape(D_MODEL)
    grads["ssd_d"] = jnp.sum(grads.pop("ssd_d_exp").reshape(SSD_HEADS, SSD_HEADDIM), axis=1)[None]
    return loss, dh, grads


IN_NAMES = ("ssd_w_in", "hg_w_in", "at_w_in", "dl_w_in")
OUT_NAMES = ("ssd_w_out", "hg_w_out", "at_w_out", "dl_w_out")
IN_COLS = (SSD_IN // 4, HG_IN // 4, AT_IN // 4, DL_IN // 4)
OUT_ROWS = (SSD_DI // 4, HG_W // 4, AT_QW // 4, DL_W // 4)
PACK_IN = sum(IN_COLS)
PACK_OUT = sum(OUT_ROWS)
N_CHIPS = 4
N_DEV = 8
HBM = pl.BlockSpec(memory_space=pl.ANY)


def _mesh_pos():
    return lax.axis_index("x"), lax.axis_index("y"), lax.axis_index("c")


def _other_chips(x, y):
    return [(1 - x, y), (x, 1 - y), (1 - x, 1 - y)]


def gather_weights(p_in, p_out, p_small):
    h_in, h_out = p_in.shape[0] // 2, p_out.shape[0] // 2

    def body(pin, pout, psm, gin, gout, gsm, send, recv):
        x, y, c = _mesh_pos()
        me = 2 * x + y
        sib = (x, y, 1 - c)
        chips = _other_chips(x, y)

        def rows(half, n):
            return pl.ds(pl.multiple_of(half * n, n), n)

        def rc(src, dst, k, to):
            return pltpu.make_async_remote_copy(src_ref=src, dst_ref=dst, send_sem=send.at[k], recv_sem=recv.at[k],
                                                device_id=to, device_id_type=MESH)

        started = []
        for j, (px, py) in enumerate(chips):
            to = (px, py, c)
            started += [rc(pin.at[rows(c, h_in)], gin.at[me, rows(c, h_in)], 3 * j, to),
                        rc(pout.at[rows(c, h_out)], gout.at[me, rows(c, h_out)], 3 * j + 1, to),
                        rc(psm, gsm.at[me], 3 * j + 2, to)]
        for cp in started:
            cp.start()
        for j, (px, py) in enumerate(chips):
            kp = 2 * px + py
            frm = (px, py, c)
            rc(pin.at[rows(c, h_in)], gin.at[kp, rows(c, h_in)], 3 * j, frm).wait_recv()
            f_in = rc(gin.at[kp, rows(c, h_in)], gin.at[kp, rows(c, h_in)], 9 + 2 * j, sib)
            f_in.start()
            rc(pout.at[rows(c, h_out)], gout.at[kp, rows(c, h_out)], 3 * j + 1, frm).wait_recv()
            f_out = rc(gout.at[kp, rows(c, h_out)], gout.at[kp, rows(c, h_out)], 10 + 2 * j, sib)
            f_out.start()
            rc(psm, gsm.at[kp], 3 * j + 2, frm).wait_recv()
            started += [f_in, f_out]
        for j, (px, py) in enumerate(chips):
            kp = 2 * px + py
            rc(gin.at[kp, rows(1 - c, h_in)], gin.at[kp, rows(1 - c, h_in)], 9 + 2 * j, sib).wait_recv()
            rc(gout.at[kp, rows(1 - c, h_out)], gout.at[kp, rows(1 - c, h_out)], 10 + 2 * j, sib).wait_recv()
        for cp in started:
            cp.wait_send()

    return pl.pallas_call(
        body, name="gather_weights", in_specs=[HBM, HBM, HBM], out_specs=[HBM, HBM, HBM],
        out_shape=[jax.ShapeDtypeStruct((N_CHIPS,) + p_in.shape, p_in.dtype),
                   jax.ShapeDtypeStruct((N_CHIPS,) + p_out.shape, p_out.dtype),
                   jax.ShapeDtypeStruct((N_CHIPS,) + p_small.shape, p_small.dtype)],
        scratch_shapes=[pltpu.SemaphoreType.DMA((15,)), pltpu.SemaphoreType.DMA((15,))],
        compiler_params=pltpu.CompilerParams(has_side_effects=True),
    )(p_in, p_out, p_small)


def swap_halves(g_in, g_out):
    h_in, h_out = g_in.shape[1] // 2, g_out.shape[1] // 2

    def body(gi, go, ri, ro, send, recv):
        x, y, c = _mesh_pos()
        sib = (x, y, 1 - c)

        def rows(half, n):
            return pl.ds(pl.multiple_of(half * n, n), n)

        cps = [pltpu.make_async_remote_copy(src_ref=gi.at[:, rows(1 - c, h_in)], dst_ref=ri, send_sem=send.at[0],
                                            recv_sem=recv.at[0], device_id=sib, device_id_type=MESH),
               pltpu.make_async_remote_copy(src_ref=go.at[:, rows(1 - c, h_out)], dst_ref=ro, send_sem=send.at[1],
                                            recv_sem=recv.at[1], device_id=sib, device_id_type=MESH)]
        for cp in cps:
            cp.start()
        for cp in cps:
            cp.wait()

    return pl.pallas_call(
        body, name="swap_halves", in_specs=[HBM, HBM], out_specs=[HBM, HBM],
        out_shape=[jax.ShapeDtypeStruct((N_CHIPS, h_in, g_in.shape[2]), g_in.dtype),
                   jax.ShapeDtypeStruct((N_CHIPS, h_out, g_out.shape[2]), g_out.dtype)],
        scratch_shapes=[pltpu.SemaphoreType.DMA((2,)), pltpu.SemaphoreType.DMA((2,))],
        compiler_params=pltpu.CompilerParams(has_side_effects=True),
    )(g_in, g_out)


def half_add(name, g, r, c_idx, tb):
    _, rows2, C = g.shape
    h = rows2 // 2
    nb = h // tb

    def body(c_ref, g_ref, r_ref, f_ref, b_ref):
        s = g_ref[...] + r_ref[...]
        f_ref[...] = s
        b_ref[...] = s.astype(BF16)

    grid_spec = pltpu.PrefetchScalarGridSpec(
        num_scalar_prefetch=1, grid=(N_CHIPS, nb),
        in_specs=[pl.BlockSpec((1, tb, C), lambda k, i, c: (k, c[0] * nb + i, 0)),
                  pl.BlockSpec((1, tb, C), lambda k, i, c: (k, i, 0))],
        out_specs=[pl.BlockSpec((1, tb, C), lambda k, i, c: (k, i, 0))] * 2)
    return pl.pallas_call(
        body, name=name, grid_spec=grid_spec,
        out_shape=[jax.ShapeDtypeStruct((N_CHIPS, h, C), F32), jax.ShapeDtypeStruct((N_CHIPS, h, C), BF16)],
        compiler_params=_cp(("parallel", "parallel")),
    )(c_idx, g, r)


def scatter_chips(b_in, b_out):
    def body(bi, bo, ri, ro, send, recv):
        x, y, c = _mesh_pos()
        cps = []
        for j, (px, py) in enumerate(_other_chips(x, y)):
            kp = 2 * px + py
            to = (px, py, c)
            cps += [pltpu.make_async_remote_copy(src_ref=bi.at[kp], dst_ref=ri.at[j], send_sem=send.at[2 * j],
                                                 recv_sem=recv.at[2 * j], device_id=to, device_id_type=MESH),
                    pltpu.make_async_remote_copy(src_ref=bo.at[kp], dst_ref=ro.at[j], send_sem=send.at[2 * j + 1],
                                                 recv_sem=recv.at[2 * j + 1], device_id=to, device_id_type=MESH)]
        for cp in cps:
            cp.start()
        for cp in cps:
            cp.wait()

    return pl.pallas_call(
        body, name="scatter_chips", in_specs=[HBM, HBM], out_specs=[HBM, HBM],
        out_shape=[jax.ShapeDtypeStruct((3,) + b_in.shape[1:], BF16), jax.ShapeDtypeStruct((3,) + b_out.shape[1:], BF16)],
        scratch_shapes=[pltpu.SemaphoreType.DMA((6,)), pltpu.SemaphoreType.DMA((6,))],
        compiler_params=pltpu.CompilerParams(has_side_effects=True),
    )(b_in, b_out)


def chip_sum(name, f, r, me_idx, tb):
    _, h, C = f.shape
    nb = h // tb

    def body(me_ref, f_ref, r0, r1, r2, o_ref):
        o_ref[...] = ((f_ref[0] + r0[0].astype(F32)) + r1[0].astype(F32)) + r2[0].astype(F32)

    def slot(j):
        return pl.BlockSpec((1, tb, C), lambda i, me: (j, i, 0))

    grid_spec = pltpu.PrefetchScalarGridSpec(
        num_scalar_prefetch=1, grid=(nb,),
        in_specs=[pl.BlockSpec((1, tb, C), lambda i, me: (me[0], i, 0)), slot(0), slot(1), slot(2)],
        out_specs=pl.BlockSpec((tb, C), lambda i, me: (i, 0)))
    return pl.pallas_call(
        body, name=name, grid_spec=grid_spec, out_shape=jax.ShapeDtypeStruct((h, C), F32),
        compiler_params=_cp(("parallel",)),
    )(me_idx, f, r, r, r)


def share_halves(f_in, f_out):
    def body(fi, fo, oi, oo, send, recv):
        x, y, c = _mesh_pos()
        sib = (x, y, 1 - c)
        cps = [pltpu.make_async_remote_copy(src_ref=fi, dst_ref=oi, send_sem=send.at[0], recv_sem=recv.at[0],
                                            device_id=sib, device_id_type=MESH),
               pltpu.make_async_remote_copy(src_ref=fo, dst_ref=oo, send_sem=send.at[1], recv_sem=recv.at[1],
                                            device_id=sib, device_id_type=MESH)]
        for cp in cps:
            cp.start()
        for cp in cps:
            cp.wait()

    return pl.pallas_call(
        body, name="share_halves", in_specs=[HBM, HBM], out_specs=[HBM, HBM],
        out_shape=[jax.ShapeDtypeStruct(f_in.shape, F32), jax.ShapeDtypeStruct(f_out.shape, F32)],
        scratch_shapes=[pltpu.SemaphoreType.DMA((2,)), pltpu.SemaphoreType.DMA((2,))],
        compiler_params=pltpu.CompilerParams(has_side_effects=True),
    )(f_in, f_out)


def gather_small(pack):
    def body(p, g, send, recv, lsem):
        x, y, c = _mesh_pos()
        me = 4 * x + 2 * y + c
        local = pltpu.make_async_copy(p, g.at[me], lsem)
        local.start()
        cps = []
        k = 0
        for fx in (0, 1):
            for fy in (0, 1):
                for fc in (0, 1):
                    if fx + fy + fc == 0:
                        continue
                    to = (x ^ fx, y ^ fy, c ^ fc)
                    cps.append((pltpu.make_async_remote_copy(src_ref=p, dst_ref=g.at[me], send_sem=send.at[k],
                                                             recv_sem=recv.at[k], device_id=to, device_id_type=MESH), to, k))
                    k += 1
        for cp, _, _ in cps:
            cp.start()
        for cp, to, k in cps:
            frm = 4 * to[0] + 2 * to[1] + to[2]
            pltpu.make_async_remote_copy(src_ref=p, dst_ref=g.at[frm], send_sem=send.at[k], recv_sem=recv.at[k],
                                         device_id=to, device_id_type=MESH).wait_recv()
        for cp, _, _ in cps:
            cp.wait_send()
        local.wait()

    return pl.pallas_call(
        body, name="gather_small", in_specs=[HBM], out_specs=HBM,
        out_shape=jax.ShapeDtypeStruct((N_DEV,) + pack.shape, pack.dtype),
        scratch_shapes=[pltpu.SemaphoreType.DMA((7,)), pltpu.SemaphoreType.DMA((7,)), pltpu.SemaphoreType.DMA],
        compiler_params=pltpu.CompilerParams(has_side_effects=True),
    )(pack)


def _adamw(w, g, m, v):
    m = ADAM_B1 * m + (1.0 - ADAM_B1) * g
    v = ADAM_B2 * v + (1.0 - ADAM_B2) * (g * g)
    m_hat = m / (1.0 - ADAM_B1 ** ADAM_STEP)
    v_hat = v / (1.0 - ADAM_B2 ** ADAM_STEP)
    delta = -ADAM_LR * (m_hat / (jnp.sqrt(v_hat) + ADAM_EPS) + ADAM_WD * w)
    return delta, m, v


def adamw_big(name, w, g, m, v):
    shp = w.shape
    flat = lambda a: a.reshape(shp[-2], shp[-1])
    (d, nm, nv), _ = rowcall(name, lambda w, g, m, v: (_adamw(w, g, m, v), ()), [flat(w), flat(g), flat(m), flat(v)], [],
                             [(shp[-1], F32)] * 3, tb=256)
    return d.reshape(shp), nm.reshape(shp), nv.reshape(shp)


def _pack_small(arrs):
    flat = jnp.concatenate([a.reshape(-1) for a in arrs])
    n = flat.shape[0]
    rows = -(-n // (SUBLANE * LANE)) * SUBLANE
    return jnp.pad(flat, (0, rows * LANE - n)).reshape(rows, LANE)


def _unpack_small(pack, shapes):
    flat = pack.reshape(-1)
    outs, off = [], 0
    for s in shapes:
        n = int(np.prod(s))
        outs.append(flat[off:off + n].reshape(s))
        off += n
    return outs


SMALL_NAMES = ("norm_g", "final_g", "rel_bias", "hgrn_lb", "ssd_conv_w", "ssd_conv_b", "ssd_dt_bias", "ssd_a_log",
               "ssd_d", "ssd_norm_g", "hg_norm_g", "at_q_norm_g", "at_k_norm_g")
ALL_NAMES = ("norm_g", "final_g", "rel_bias", "hgrn_lb", "ssd_w_in", "ssd_conv_w", "ssd_conv_b", "ssd_dt_bias",
             "ssd_a_log", "ssd_d", "ssd_norm_g", "ssd_w_out", "hg_w_in", "hg_norm_g", "hg_w_out", "at_w_in",
             "at_q_norm_g", "at_k_norm_g", "at_w_out", "dl_w_in", "dl_w_out")


def kernel(x, norm_g, final_g, rel_bias, hgrn_lb, ssd_w_in, ssd_conv_w, ssd_conv_b, ssd_dt_bias, ssd_a_log, ssd_d, ssd_norm_g, ssd_w_out, hg_w_in, hg_norm_g, hg_w_out, at_w_in, at_q_norm_g, at_k_norm_g, at_w_out, dl_w_in, dl_w_out, loss_target, m_norm_g, m_final_g, m_rel_bias, m_hgrn_lb, m_ssd_w_in, m_ssd_conv_w, m_ssd_conv_b, m_ssd_dt_bias, m_ssd_a_log, m_ssd_d, m_ssd_norm_g, m_ssd_w_out, m_hg_w_in, m_hg_norm_g, m_hg_w_out, m_at_w_in, m_at_q_norm_g, m_at_k_norm_g, m_at_w_out, m_dl_w_in, m_dl_w_out, v_norm_g, v_final_g, v_rel_bias, v_hgrn_lb, v_ssd_w_in, v_ssd_conv_w, v_ssd_conv_b, v_ssd_dt_bias, v_ssd_a_log, v_ssd_d, v_ssd_norm_g, v_ssd_w_out, v_hg_w_in, v_hg_norm_g, v_hg_w_out, v_at_w_in, v_at_q_norm_g, v_at_k_norm_g, v_at_w_out, v_dl_w_in, v_dl_w_out):
    args = locals()
    W = {n: args[n] for n in ALL_NAMES}
    M = {n: args["m_" + n] for n in ALL_NAMES}
    V = {n: args["v_" + n] for n in ALL_NAMES}
    xi, yi, ci = lax.axis_index("x"), lax.axis_index("y"), lax.axis_index("c")
    chip = 2 * xi + yi
    conv_shard = SSD_CONV_CH // N_CHIPS
    hgn_shard = HG_W // N_CHIPS

    p_in = jnp.concatenate([W[n][0].astype(BF16) for n in IN_NAMES], axis=1)
    p_out = jnp.concatenate([W[n][0].astype(BF16) for n in OUT_NAMES], axis=0)
    p_small = jnp.concatenate([
        jnp.pad(ssd_conv_w[0], ((0, 0), (0, D_MODEL - conv_shard))),
        jnp.pad(hg_norm_g, ((0, 0), (0, D_MODEL - hgn_shard)))], axis=0)
    g_in, g_out, g_small = gather_weights(p_in, p_out, p_small)

    def slot(stack, own, k):
        return jnp.where(chip == k, own, stack[k])

    full = {}
    off = 0
    for n, cols in zip(IN_NAMES, IN_COLS):
        full[n] = jnp.concatenate([slot(g_in, p_in, k)[:, off:off + cols] for k in range(N_CHIPS)], axis=1)
        off += cols
    off = 0
    for n, rows in zip(OUT_NAMES, OUT_ROWS):
        full[n] = jnp.concatenate([slot(g_out, p_out, k)[off:off + rows] for k in range(N_CHIPS)], axis=0)
        off += rows
    conv_full = jnp.concatenate([slot(g_small, p_small, k)[:SSD_CONV, :conv_shard] for k in range(N_CHIPS)], axis=1)
    hgn_full = jnp.concatenate([slot(g_small, p_small, k)[SSD_CONV:SSD_CONV + 1, :hgn_shard] for k in range(N_CHIPS)], axis=1)
    w = dict(
        norm_g=norm_g, final_g=final_g, rel_bias=rel_bias, hgrn_lb=hgrn_lb,
        ssd_w_main=full["ssd_w_in"][:, :SSD_MAIN], ssd_w_dt=full["ssd_w_in"][:, SSD_MAIN:],
        ssd_conv_w8=jnp.concatenate([conv_full, jnp.zeros((1, SSD_CONV_CH), F32)], axis=0),
        ssd_conv_b=ssd_conv_b, ssd_dt_bias=ssd_dt_bias, ssd_a_log=ssd_a_log, ssd_d=ssd_d, ssd_norm_g=ssd_norm_g,
        ssd_w_out=full["ssd_w_out"], hg_w_in=full["hg_w_in"], hg_norm_g=hgn_full, hg_w_out=full["hg_w_out"],
        at_w_in=full["at_w_in"], at_q_norm_g=at_q_norm_g, at_k_norm_g=at_k_norm_g, at_w_out=full["at_w_out"],
        dl_w_in=full["dl_w_in"], dl_w_out=full["dl_w_out"])

    loss_tile, grad_x, grads = local_step(x[0], loss_target[0], w)
    loss = lax.psum(loss_tile[0, 0], ("x", "y", "c"))

    gp_in = jnp.concatenate([grads[n][0].reshape(D_MODEL, N_CHIPS, cols).transpose(1, 0, 2)
                             for n, cols in zip(IN_NAMES, IN_COLS)], axis=2)
    gp_out = jnp.concatenate([grads[n][0].reshape(N_CHIPS, rows, D_MODEL)
                              for n, rows in zip(OUT_NAMES, OUT_ROWS)], axis=1)
    r_in, r_out = swap_halves(gp_in, gp_out)
    c_idx = ci.astype(jnp.int32).reshape(1)
    me_idx = chip.astype(jnp.int32).reshape(1)
    f_in, b_in = half_add("half_add_in", gp_in, r_in, c_idx, 128)
    f_out, b_out = half_add("half_add_out", gp_out, r_out, c_idx, 256)
    x_in, x_out = scatter_chips(b_in, b_out)
    s_in = chip_sum("chip_sum_in", f_in, x_in, me_idx, 128)
    s_out = chip_sum("chip_sum_out", f_out, x_out, me_idx, 256)
    o_in, o_out = share_halves(s_in, s_out)
    red_in = jnp.where(ci == 0, jnp.concatenate([s_in, o_in], axis=0), jnp.concatenate([o_in, s_in], axis=0))
    red_out = jnp.where(ci == 0, jnp.concatenate([s_out, o_out], axis=0), jnp.concatenate([o_out, s_out], axis=0))
    G = {}
    off = 0
    for n, cols in zip(IN_NAMES, IN_COLS):
        G[n] = red_in[:, off:off + cols][None]
        off += cols
    off = 0
    for n, rows in zip(OUT_NAMES, OUT_ROWS):
        G[n] = red_out[off:off + rows][None]
        off += rows

    small_full = [grads[n].reshape(-1) for n in SMALL_NAMES]
    shapes_full = [grads[n].shape for n in SMALL_NAMES]
    packs = gather_small(_pack_small(small_full))
    (red_small,) = smallcall("sum_small", lambda p: (functools.reduce(lambda a, b: a + b, [p[k] for k in range(N_DEV)]),),
                             [packs], [packs.shape[1:]])
    for n, g in zip(SMALL_NAMES, _unpack_small(red_small, shapes_full)):
        G[n] = g
    G["ssd_conv_w"] = lax.dynamic_slice_in_dim(G["ssd_conv_w"].reshape(1, SSD_CONV, SSD_CONV_CH), chip * conv_shard, conv_shard, axis=2)
    G["hg_norm_g"] = lax.dynamic_slice_in_dim(G["hg_norm_g"].reshape(1, HG_W), chip * hgn_shard, hgn_shard, axis=1)
    for n in SMALL_NAMES:
        G[n] = G[n].reshape(W[n].shape)

    D, NM, NV = {}, {}, {}
    for n in IN_NAMES + OUT_NAMES:
        D[n], NM[n], NV[n] = adamw_big("adamw_" + n, W[n], G[n], M[n], V[n])
    shapes = [W[n].shape for n in SMALL_NAMES]
    pk = [_pack_small([T[n] for n in SMALL_NAMES]) for T in (W, G, M, V)]
    outs = smallcall("adamw_small", lambda w, g, m, v: _adamw(w, g, m, v), pk, [pk[0].shape] * 3)
    for T, pack in zip((D, NM, NV), outs):
        for n, a in zip(SMALL_NAMES, _unpack_small(pack, shapes)):
            T[n] = a
    return (loss, grad_x[None], *[G[n] for n in ALL_NAMES], *[D[n] for n in ALL_NAMES],
            *[NM[n] for n in ALL_NAMES], *[NV[n] for n in ALL_NAMES])
```

```python
import functools
import math

import numpy as np
import jax
import jax.numpy as jnp
from jax import lax
from jax.experimental import pallas as pl
from jax.experimental.pallas import tpu as pltpu

F32 = jnp.float32
BF16 = jnp.bfloat16
MESH = pl.DeviceIdType.MESH

D_MODEL = 1024
DEPTH = 4
GRID_W = 64
EPS = 1e-6
NEG_BIG = -1e30

SSD_DI = 2048
SSD_HEADDIM = 64
SSD_HEADS = 32
SSD_GROUPS = 4
SSD_HPG = 8
SSD_STATE = 128
SSD_CONV = 7
SSD_CHUNK = 128
SSD_GPS = 4
SSD_CONV_CH = SSD_DI + 2 * SSD_GROUPS * SSD_STATE
SSD_MAIN = SSD_DI + SSD_CONV_CH
SSD_IN = SSD_MAIN + 2 * SSD_HEADS

HG_HEADS = 8
HG_EXPAND = 128
HG_W = 1024
HG_CHUNK = 32
HG_ROWS = 128
HG_HPS = 8
HG_IN = 5 * HG_W

AT_HEADS = 16
AT_KV = 8
AT_GRP = 2
AT_HD = 128
ROPE_THETA = 10000.0
ROPE_AXIS = 64
AT_QW = AT_HEADS * AT_HD
AT_KW = AT_KV * AT_HD
AT_IN = 2 * AT_QW + 2 * AT_KW

DL_PAIRS = ((128, 1), (512, 4), (2048, 16))
DL_HEADS = 16
DL_HD = 64
DL_W = 1024
DL_STEPS = 64
DL_IN = 10 * DL_W
REL_BUCKETS = 32
REL_MAX_DIST = 1024

ADAM_LR = 0.001
ADAM_B1 = 0.9
ADAM_B2 = 0.999
ADAM_EPS = 1e-08
ADAM_WD = 0.01
ADAM_STEP = 10

VMEM_LIMIT = 56 * 1024 * 1024
LANE = 128
SUBLANE = 8


def _cp(sem=None):
    return pltpu.CompilerParams(dimension_semantics=sem, vmem_limit_bytes=VMEM_LIMIT)


_NN, _NT, _TN = ((1,), (0,)), ((1,), (1,)), ((0,), (0,))


def _dot(a, b, dims):
    return lax.dot_general(a.astype(BF16), b.astype(BF16), (dims, ((), ())), preferred_element_type=F32)


def _dot_rule(dims, da_rule, db_rule):
    @jax.custom_vjp
    def f(a, b):
        return _dot(a, b, dims)

    def fwd(a, b):
        return _dot(a, b, dims), (a, b)

    def bwd(res, g):
        a, b = res
        return da_rule(a, b, g).astype(a.dtype), db_rule(a, b, g).astype(b.dtype)

    f.defvjp(fwd, bwd)
    return f


_mm = _dot_rule(_NN, lambda a, b, g: _dot(g, b, _NT), lambda a, b, g: _dot(a, g, _TN))
_mm_nt = _dot_rule(_NT, lambda a, b, g: _dot(g, b, _NN), lambda a, b, g: _dot(g, a, _TN))
_mm_tn = _dot_rule(_TN, lambda a, b, g: _dot(b, g, _NT), lambda a, b, g: _dot(a, g, _NN))


def _mm_exact(a, b):
    return jnp.dot(a, b, preferred_element_type=F32, precision=lax.Precision.HIGHEST)


def _dot3(t, a, dims):
    hi = a.astype(BF16)
    r1 = a - hi.astype(F32)
    mid = r1.astype(BF16)
    lo = r1 - mid.astype(F32)
    return _dot(t, hi, dims) + (_dot(t, mid, dims) + _dot(t, lo, dims))


@jax.custom_vjp
def _mm_tri(t, a):
    return _dot3(t, a, _NN)


def _mm_tri_fwd(t, a):
    return _dot3(t, a, _NN), t


def _mm_tri_bwd(t, g):
    return None, _dot3(t, g, _TN)


_mm_tri.defvjp(_mm_tri_fwd, _mm_tri_bwd)


def _mm_nt_exact(a, b):
    return lax.dot_general(a, b, (((1,), (1,)), ((), ())), preferred_element_type=F32,
                           precision=lax.Precision.HIGHEST)


def _silu(x):
    return x * jax.nn.sigmoid(x)


def _softplus(z):
    return jnp.maximum(z, 0.0) + jnp.log(1.0 + jnp.exp(-jnp.abs(z)))


def _pick(dim, pref):
    best = None
    t = LANE
    while t <= min(dim, pref):
        if dim % t == 0:
            best = t
        t += LANE
    return best if best is not None else dim


def _const_map(n):
    return lambda *_: (0,) * n


def matmul(name, a, b, mode="nn", res=None, out_dtype=F32, tm=512, tn=1024, tk=1024):
    if mode == "tn":
        K, M = a.shape
    else:
        M, K = a.shape
    N = b.shape[0] if mode == "nt" else b.shape[1]
    tm, tn, tk = _pick(M, tm), _pick(N, tn), _pick(K, tk)
    nk = K // tk
    a_spec = (pl.BlockSpec((tk, tm), lambda i, j, k: (k, i)) if mode == "tn"
              else pl.BlockSpec((tm, tk), lambda i, j, k: (i, k)))
    b_spec = (pl.BlockSpec((tn, tk), lambda i, j, k: (j, k)) if mode == "nt"
              else pl.BlockSpec((tk, tn), lambda i, j, k: (k, j)))
    dot = {"nn": _mm, "nt": _mm_nt, "tn": _mm_tn}[mode]
    has_res = res is not None

    def body(*refs):
        if has_res:
            a_ref, b_ref, r_ref, o_ref, acc = refs
        else:
            a_ref, b_ref, o_ref, acc = refs
        k = pl.program_id(2)

        @pl.when(k == 0)
        def _():
            acc[...] = jnp.zeros_like(acc)

        acc[...] += dot(a_ref[...], b_ref[...])

        @pl.when(k == nk - 1)
        def _():
            out = acc[...]
            if has_res:
                out = out + r_ref[...].astype(F32)
            o_ref[...] = out.astype(o_ref.dtype)

    in_specs = [a_spec, b_spec]
    args = [a, b]
    if has_res:
        in_specs.append(pl.BlockSpec((tm, tn), lambda i, j, k: (i, j)))
        args.append(res)
    return pl.pallas_call(
        body, name=name, grid=(M // tm, N // tn, nk), in_specs=in_specs,
        out_specs=pl.BlockSpec((tm, tn), lambda i, j, k: (i, j)),
        out_shape=jax.ShapeDtypeStruct((M, N), out_dtype),
        scratch_shapes=[pltpu.VMEM((tm, tn), F32)],
        compiler_params=_cp(("parallel", "parallel", "arbitrary")),
    )(*args)


def _col(arr, width, idx):
    return (arr, width, idx)


def _perm(arr, dil, width=None, idx=0):
    return (arr, arr.shape[1] if width is None else width, idx, dil)


def _from_perm(ref, scr, dil):
    n, w = ref.shape[1], ref.shape[2]
    for r in range(dil):
        for j in range(w // LANE):
            scr[j, pl.ds(r, n, stride=dil), :] = ref[r, :, j * LANE:(j + 1) * LANE].astype(F32)
    return jnp.concatenate([scr[j] for j in range(w // LANE)], axis=1)


def _to_perm(val, ref, scr, dil):
    n, w = ref.shape[1], ref.shape[2]
    for j in range(w // LANE):
        scr[j] = val[:, j * LANE:(j + 1) * LANE].astype(F32)
    for r in range(dil):
        ref[r] = jnp.concatenate([scr[j, pl.ds(r, n, stride=dil), :] for j in range(w // LANE)], axis=1).astype(ref.dtype)


def rowcall(name, fn, rows, bcs, row_outs, bc_outs=(), tb=256, halo=()):
    rows = [r if isinstance(r, tuple) else (r, r.shape[1], 0) for r in rows]
    rows = [r if len(r) == 4 else r + (1,) for r in rows]
    row_outs = [o if len(o) == 3 else o + (1,) for o in row_outs]
    S = rows[0][0].shape[0]
    tb = min(tb, S)
    nb = S // tb
    n_r, n_h, n_b, n_ro, n_bo = len(rows), len(halo), len(bcs), len(row_outs), len(bc_outs)
    hb = tb // SUBLANE
    last = S // SUBLANE - 1
    perm_w = max([w for (_, w, _, d) in rows if d > 1] + [w for (w, _, d) in row_outs if d > 1] + [0])

    def body(*refs):
        i = pl.program_id(0)
        scr = refs[-1] if perm_w else None
        pos = 0
        r_in = [r[...] if d == 1 else _from_perm(r, scr, d) for r, (_, _, _, d) in zip(refs[pos:pos + n_r], rows)]
        pos += n_r
        h_in = []
        for _ in range(n_h):
            prev = refs[pos][...] * (i > 0).astype(F32)
            nxt = refs[pos + 1][...] * (i < nb - 1).astype(F32)
            h_in += [prev, nxt]
            pos += 2
        b_in = [r[...] for r in refs[pos:pos + n_b]]
        pos += n_b
        ro = refs[pos:pos + n_ro]
        bo = refs[pos + n_ro:pos + n_ro + n_bo]
        outs_r, outs_b = fn(*r_in, *h_in, *b_in)
        for ref, val, (_, _, d) in zip(ro, outs_r, row_outs, strict=True):
            if d == 1:
                ref[...] = val.astype(ref.dtype)
            else:
                _to_perm(val, ref, scr, d)
        if n_bo:
            @pl.when(i == 0)
            def _():
                for ref in bo:
                    ref[...] = jnp.zeros_like(ref)

            for ref, val in zip(bo, outs_b, strict=True):
                ref[...] += val

    in_specs, args = [], []
    for (a, w, c, d) in rows:
        if d == 1:
            in_specs.append(pl.BlockSpec((tb, w), functools.partial(lambda i, c: (i, c), c=c)))
            args.append(a)
        else:
            in_specs.append(pl.BlockSpec((d, tb // d, w), functools.partial(lambda i, c: (0, i, c), c=c)))
            args.append(a.reshape(d, S // d, a.shape[1]))
    for h in halo:
        a, w, c, _ = rows[h]
        in_specs.append(pl.BlockSpec((SUBLANE, w), functools.partial(
            lambda i, c: (jnp.maximum(i * hb - 1, 0), c), c=c)))
        in_specs.append(pl.BlockSpec((SUBLANE, w), functools.partial(
            lambda i, c: (jnp.minimum((i + 1) * hb, last), c), c=c)))
        args += [a, a]
    for b in bcs:
        in_specs.append(pl.BlockSpec(b.shape, _const_map(b.ndim)))
        args.append(b)
    out_specs, out_shape = [], []
    for (w, dt, d) in row_outs:
        if d == 1:
            out_specs.append(pl.BlockSpec((tb, w), lambda i: (i, 0)))
            out_shape.append(jax.ShapeDtypeStruct((S, w), dt))
        else:
            out_specs.append(pl.BlockSpec((d, tb // d, w), lambda i: (0, i, 0)))
            out_shape.append(jax.ShapeDtypeStruct((d, S // d, w), dt))
    for shp in bc_outs:
        out_specs.append(pl.BlockSpec(shp, _const_map(len(shp))))
        out_shape.append(jax.ShapeDtypeStruct(shp, F32))
    outs = pl.pallas_call(
        body, name=name, grid=(nb,), in_specs=in_specs, out_specs=out_specs, out_shape=out_shape,
        scratch_shapes=[pltpu.VMEM((perm_w // LANE, tb, LANE), F32)] if perm_w else [],
        compiler_params=_cp(("arbitrary",) if n_bo else ("parallel",)),
    )(*args)
    row_res = [o if d == 1 else o.reshape(S, w) for o, (w, _, d) in zip(outs[:n_ro], row_outs)]
    return row_res, list(outs[n_ro:])


def smallcall(name, fn, ins, out_shapes):
    n_in = len(ins)

    def body(*refs):
        outs = fn(*[r[...] for r in refs[:n_in]])
        for ref, val in zip(refs[n_in:], outs, strict=True):
            ref[...] = val.astype(ref.dtype)

    return pl.pallas_call(
        body, name=name, out_shape=[jax.ShapeDtypeStruct(s, F32) for s in out_shapes],
        compiler_params=_cp(),
    )(*ins)


def _rms(x, g):
    return x * lax.rsqrt(jnp.mean(x * x, axis=-1, keepdims=True) + EPS) * g


def _rms_groups(y, g, width):
    outs = []
    for j in range(y.shape[1] // width):
        sl = slice(j * width, (j + 1) * width)
        outs.append(_rms(y[:, sl], g[:, sl]))
    return jnp.concatenate(outs, axis=1)


def norm_fwd(name, x, g, dils=(1,)):
    outs, _ = rowcall(name, lambda x, g: ((_rms(x, g),) * len(dils), ()), [x], [g],
                      [(D_MODEL, BF16, d) for d in dils], tb=512)
    return outs[0] if len(dils) == 1 else tuple(outs)


def norm_bwd(name, x, g, dhn, dres, dils=(1,)):
    parts = dhn if isinstance(dhn, tuple) else (dhn,)
    n = len(parts)

    def fn(x, *rest):
        dh = functools.reduce(lambda a, b: a + b, rest[:n])
        _, vjp = jax.vjp(_rms, x, rest[n + 1])
        dx, dg = vjp(dh)
        return (dx + rest[n],), (dg,)

    rows = [x] + [a if d == 1 else _perm(a, d) for a, d in zip(parts, dils)] + [dres]
    (dx,), (dg,) = rowcall(name, fn, rows, [g], [(D_MODEL, F32)], [(1, D_MODEL)], tb=512)
    return dx, dg


def loss_head(x, tgt, g):
    def fn(x, tgt, g):
        y, vjp = jax.vjp(_rms, x, g)
        diff = y - tgt
        loss = 0.5 * jnp.sum(jnp.mean(diff * diff, axis=-1, keepdims=True), axis=0, keepdims=True)
        dx, dg = vjp(diff * (1.0 / D_MODEL))
        return (dx,), (jnp.broadcast_to(loss, (1, LANE)), dg)

    (dx,), (loss, dg) = rowcall("loss_head", fn, [x, tgt], [g], [(D_MODEL, F32)],
                                [(1, LANE), (1, D_MODEL)], tb=512)
    return loss, dx, dg


def _shift_rows(x, s):
    if s == 0:
        return x
    return pltpu.roll(x, (-s) % x.shape[0], 0)


def _conv_ext(x, prev, nxt, w):
    xe = jnp.concatenate([prev, x, nxt], axis=0)
    pad = SSD_CONV // 2
    c = jnp.zeros_like(xe)
    for k in range(SSD_CONV):
        c = c + w[k:k + 1, :] * _shift_rows(xe, k - pad)
    return xe, c


def ssd_conv_fwd(u, conv_w, conv_b):
    def fn(x0, x1, x2, p0, n0, p1, n1, p2, n2, w, b):
        tb = x0.shape[0]
        outs = []
        for j, (x, p, n) in enumerate(((x0, p0, n0), (x1, p1, n1), (x2, p2, n2))):
            sl = slice(j * 1024, (j + 1) * 1024)
            _, c = _conv_ext(x, p, n, w[:, sl])
            outs.append(_silu(c[SUBLANE:SUBLANE + tb] + b[:, sl]))
        return (jnp.concatenate(outs, axis=1),), ()

    (xbc,), _ = rowcall("ssd_conv_fwd", fn, [_col(u, 1024, 2), _col(u, 1024, 3), _col(u, 1024, 4)],
                        [conv_w, conv_b], [(SSD_CONV_CH, F32)], tb=256, halo=(0, 1, 2))
    return xbc


def ssd_conv_bwd(u, dxbc, dz, conv_w, conv_b):
    pad = SSD_CONV // 2

    def fn(x0, x1, x2, g0, g1, g2, dz, xp0, xn0, xp1, xn1, xp2, xn2, gp0, gn0, gp1, gn1, gp2, gn2, w, b):
        tb = x0.shape[0]
        blk = slice(SUBLANE, SUBLANE + tb)
        dpre, dws, dbs = [], [], []
        xs = ((x0, xp0, xn0), (x1, xp1, xn1), (x2, xp2, xn2))
        gs = ((g0, gp0, gn0), (g1, gp1, gn1), (g2, gp2, gn2))
        for j in range(3):
            sl = slice(j * 1024, (j + 1) * 1024)
            wj = w[:, sl]
            xe, c = _conv_ext(*xs[j], wj)
            ce = c + b[:, sl]
            sig = jax.nn.sigmoid(ce)
            ge = jnp.concatenate([gs[j][1], gs[j][0], gs[j][2]], axis=0)
            dce = ge * (sig * (1.0 + ce * (1.0 - sig)))
            dx = jnp.zeros_like(xe)
            dw_rows = []
            for k in range(SSD_CONV):
                dx = dx + wj[k:k + 1, :] * _shift_rows(dce, pad - k)
                dw_rows.append(jnp.sum(dce[blk] * _shift_rows(xe, k - pad)[blk], axis=0, keepdims=True))
            dw_rows.append(jnp.zeros_like(dw_rows[0]))
            dpre.append(dx[blk])
            dws.append(jnp.concatenate(dw_rows, axis=0))
            dbs.append(jnp.sum(dce[blk], axis=0, keepdims=True))
        du = jnp.concatenate([dz] + dpre, axis=1)
        return (du,), (jnp.concatenate(dws, axis=1), jnp.concatenate(dbs, axis=1))

    rows = [_col(u, 1024, 2), _col(u, 1024, 3), _col(u, 1024, 4),
            _col(dxbc, 1024, 0), _col(dxbc, 1024, 1), _col(dxbc, 1024, 2), dz]
    (du,), (dw, db) = rowcall("ssd_conv_bwd", fn, rows, [conv_w, conv_b], [(SSD_MAIN, BF16)],
                              [(SUBLANE, SSD_CONV_CH), (1, SSD_CONV_CH)], tb=128, halo=(0, 1, 2, 3, 4, 5))
    return du, dw, db


def _expand_heads(v):
    return jnp.concatenate([jnp.broadcast_to(v[:, j:j + 1], (v.shape[0], SSD_HEADDIM)) for j in range(SSD_HPG)], axis=1)


def _ssd_chunk(rev, st_in, xs, udt, dtb, alog, B, C):
    Q = B.shape[0]
    P = SSD_HEADDIM
    dt = _softplus(udt + dtb)
    a = dt * (-jnp.exp(alog))
    r = lax.broadcasted_iota(jnp.int32, (Q, Q), 0)
    c = lax.broadcasted_iota(jnp.int32, (Q, Q), 1)
    mask = (r <= c) if rev else (r >= c)
    p = _mm_tri(mask, a)
    pT = p.T
    p_e = _expand_heads(p)
    tot_e = p_e[0:1] if rev else p_e[Q - 1:Q]
    xdt = xs * _expand_heads(dt)
    CB = _mm_nt(C, B)
    ys = []
    for j in range(SSD_HPG):
        L = jnp.exp(jnp.where(mask, p[:, j:j + 1] - pT[j:j + 1, :], NEG_BIG))
        ys.append(_mm(CB * L, xdt[:, j * P:(j + 1) * P]))
    y = jnp.concatenate(ys, axis=1) + _mm(C, st_in) * jnp.exp(p_e)
    st_out = st_in * jnp.exp(tot_e) + _mm_tn(B, xdt * jnp.exp(tot_e - p_e))
    return y, st_out


def _ssd_specs(nc, rev_order):
    Q = SSD_CHUNK
    N, P, H, GS = SSD_STATE, SSD_HEADDIM, SSD_HPG, SSD_GPS
    gw = H * P
    nbc = SSD_GROUPS // GS

    def cidx(s):
        return nc - 1 - s if rev_order else s

    xs = pl.BlockSpec((Q, GS * gw), lambda g, s: (cidx(s), g))
    Bs = pl.BlockSpec((Q, GS * N), lambda g, s: (cidx(s), SSD_DI // (GS * N) + g))
    Cs = pl.BlockSpec((Q, GS * N), lambda g, s: (cidx(s), SSD_DI // (GS * N) + nbc + g))
    BC_out = pl.BlockSpec((Q, GS * N), lambda g, s: (cidx(s), g))
    udt = pl.BlockSpec((GS, Q, H), lambda g, s: (g, cidx(s), 0))
    small = pl.BlockSpec((GS, 1, H), lambda g, s: (g, 0, 0))
    st = pl.BlockSpec((GS, 1, N, gw), lambda g, s: (g, cidx(s), 0, 0))
    return xs, Bs, Cs, BC_out, udt, small, st


def ssd_scan_fwd(name, xbc, udt, dtb, alog, rev):
    S = xbc.shape[0]
    Q, N, P, H, GS = SSD_CHUNK, SSD_STATE, SSD_HEADDIM, SSD_HPG, SSD_GPS
    gw = H * P
    nc = S // Q
    xs_s, B_s, C_s, _, udt_s, small_s, st_s = _ssd_specs(nc, rev)

    def body(xs_ref, B_ref, C_ref, udt_ref, dtb_ref, alog_ref, y_ref, st_ref, state):
        @pl.when(pl.program_id(1) == 0)
        def _():
            state[...] = jnp.zeros_like(state)

        for g in range(GS):
            st_ref[g, 0] = state[g]
            y, st_out = _ssd_chunk(rev, state[g], xs_ref[:, g * gw:(g + 1) * gw], udt_ref[g], dtb_ref[g], alog_ref[g],
                                   B_ref[:, g * N:(g + 1) * N], C_ref[:, g * N:(g + 1) * N])
            y_ref[:, g * gw:(g + 1) * gw] = y
            state[g] = st_out

    return pl.pallas_call(
        body, name=name, grid=(SSD_GROUPS // GS, nc),
        in_specs=[xs_s, B_s, C_s, udt_s, small_s, small_s],
        out_specs=[xs_s, st_s],
        out_shape=[jax.ShapeDtypeStruct((S, SSD_DI), F32),
                   jax.ShapeDtypeStruct((SSD_GROUPS, nc, N, gw), F32)],
        scratch_shapes=[pltpu.VMEM((GS, N, gw), F32)],
        compiler_params=_cp(("parallel", "arbitrary")),
    )(xbc, xbc, xbc, udt, dtb, alog)


def ssd_scan_bwd(name, xbc, udt, dtb, alog, states, dy, rev):
    S = xbc.shape[0]
    Q, N, P, H, GS = SSD_CHUNK, SSD_STATE, SSD_HEADDIM, SSD_HPG, SSD_GPS
    gw = H * P
    nc = S // Q
    xs_s, B_s, C_s, BC_out, udt_s, small_s, st_s = _ssd_specs(nc, not rev)

    def body(xs_ref, B_ref, C_ref, udt_ref, dtb_ref, alog_ref, st_ref, dy_ref,
             dx_ref, dB_ref, dC_ref, dudt_ref, ddtb_ref, dalog_ref, dstate):
        @pl.when(pl.program_id(1) == 0)
        def _():
            dstate[...] = jnp.zeros_like(dstate)
            ddtb_ref[...] = jnp.zeros_like(ddtb_ref)
            dalog_ref[...] = jnp.zeros_like(dalog_ref)

        for g in range(GS):
            cols, bc = slice(g * gw, (g + 1) * gw), slice(g * N, (g + 1) * N)
            _, vjp = jax.vjp(functools.partial(_ssd_chunk, rev), st_ref[g, 0], xs_ref[:, cols], udt_ref[g], dtb_ref[g],
                             alog_ref[g], B_ref[:, bc], C_ref[:, bc])
            dst_in, dxs, dudt, ddtb, dalog, dB, dC = vjp((dy_ref[:, cols], dstate[g]))
            dx_ref[:, cols] = dxs
            dB_ref[:, bc] = dB
            dC_ref[:, bc] = dC
            dudt_ref[g] = dudt
            ddtb_ref[g] += ddtb
            dalog_ref[g] += dalog
            dstate[g] = dst_in

    return pl.pallas_call(
        body, name=name, grid=(SSD_GROUPS // GS, nc),
        in_specs=[xs_s, B_s, C_s, udt_s, small_s, small_s, st_s, xs_s],
        out_specs=[xs_s, BC_out, BC_out, udt_s, small_s, small_s],
        out_shape=[jax.ShapeDtypeStruct((S, SSD_DI), F32),
                   jax.ShapeDtypeStruct((S, SSD_GROUPS * N), F32),
                   jax.ShapeDtypeStruct((S, SSD_GROUPS * N), F32),
                   jax.ShapeDtypeStruct((SSD_GROUPS, S, H), F32),
                   jax.ShapeDtypeStruct((SSD_GROUPS, 1, H), F32),
                   jax.ShapeDtypeStruct((SSD_GROUPS, 1, H), F32)],
        scratch_shapes=[pltpu.VMEM((GS, N, gw), F32)],
        compiler_params=_cp(("parallel", "arbitrary")),
    )(xbc, xbc, xbc, udt, dtb, alog, states, dy)


def _ssd_combine(yf, yb, xs, z, dexp, ng):
    y = (yf + yb + xs * dexp) * _silu(z)
    return _rms_groups(y, ng, SSD_DI // SSD_GROUPS)


def ssd_forward(x, hn, w):
    S = x.shape[0]
    u = matmul("ssd_in", hn, w["ssd_w_main"])
    udt = matmul("ssd_in_dt", hn, w["ssd_w_dt"])
    xbc = ssd_conv_fwd(u, w["ssd_conv_w8"], w["ssd_conv_b"])
    udt_t = udt.reshape(S, 2, SSD_GROUPS, SSD_HPG).transpose(1, 2, 0, 3)
    dtb = w["ssd_dt_bias"].reshape(2, SSD_GROUPS, 1, SSD_HPG)
    alog = w["ssd_a_log"].reshape(2, SSD_GROUPS, 1, SSD_HPG)
    yf, stf = ssd_scan_fwd("ssd_scan_f", xbc, udt_t[0], dtb[0], alog[0], False)
    yb, stb = ssd_scan_fwd("ssd_scan_b", xbc, udt_t[1], dtb[1], alog[1], True)
    dexp = jnp.repeat(w["ssd_d"].reshape(1, SSD_HEADS), SSD_HEADDIM, axis=1)
    (yn,), _ = rowcall("ssd_combine", lambda yf, yb, xs, z, d, g: ((_ssd_combine(yf, yb, xs, z, d, g),), ()),
                       [yf, yb, _col(xbc, SSD_DI, 0), _col(u, SSD_DI, 0)], [dexp, w["ssd_norm_g"]],
                       [(SSD_DI, BF16)], tb=256)
    out = matmul("ssd_out", yn, w["ssd_w_out"], res=x)
    saved = dict(hn=hn, u=u, xbc=xbc, udt_t=udt_t, dtb=dtb, alog=alog, yf=yf, yb=yb, stf=stf, stb=stb,
                 dexp=dexp, yn=yn)
    return out, saved


def ssd_backward(dy, sv, w):
    S = dy.shape[0]
    u, xbc = sv["u"], sv["xbc"]
    dyn = matmul("ssd_out_dx", dy, w["ssd_w_out"], mode="nt")
    g_w_out = matmul("ssd_out_dw", sv["yn"], dy, mode="tn")

    def comb_bwd(yf, yb, xs, z, dyn, dexp, ng):
        _, vjp = jax.vjp(_ssd_combine, yf, yb, xs, z, dexp, ng)
        dyf, _, dxs, dz, ddexp, dng = vjp(dyn)
        return (dyf, dxs, dz), (ddexp, dng)

    (dyc, dskip, dz), (ddexp, g_norm) = rowcall(
        "ssd_combine_bwd", comb_bwd, [sv["yf"], sv["yb"], _col(xbc, SSD_DI, 0), _col(u, SSD_DI, 0), dyn],
        [sv["dexp"], w["ssd_norm_g"]], [(SSD_DI, F32)] * 3, [(1, SSD_DI), (1, SSD_DI)], tb=256)
    udt_t, dtb, alog = sv["udt_t"], sv["dtb"], sv["alog"]
    dxf, dBf, dCf, dudt_f, ddtb_f, dalog_f = ssd_scan_bwd("ssd_scan_f_bwd", xbc, udt_t[0], dtb[0], alog[0],
                                                          sv["stf"], dyc, False)
    dxb, dBb, dCb, dudt_b, ddtb_b, dalog_b = ssd_scan_bwd("ssd_scan_b_bwd", xbc, udt_t[1], dtb[1], alog[1],
                                                          sv["stb"], dyc, True)

    def gather(dxf, dxb, dskip, dBf, dBb, dCf, dCb):
        return (jnp.concatenate([dxf + dxb + dskip, dBf + dBb, dCf + dCb], axis=1),), ()

    (dxbc,), _ = rowcall("ssd_dxbc", gather, [dxf, dxb, dskip, dBf, dBb, dCf, dCb], [], [(SSD_CONV_CH, F32)], tb=256)
    du, g_conv_w8, g_conv_b = ssd_conv_bwd(u, dxbc, dz, w["ssd_conv_w8"], w["ssd_conv_b"])
    dudt = jnp.stack([dudt_f, dudt_b]).transpose(2, 0, 1, 3).reshape(S, 2 * SSD_HEADS)
    hn = sv["hn"]
    g_main = matmul("ssd_in_dw", hn, du, mode="tn")
    g_dt = matmul("ssd_in_dt_dw", hn, dudt, mode="tn")
    dhn = matmul("ssd_in_dt_dx", dudt, w["ssd_w_dt"], mode="nt")
    dhn = matmul("ssd_in_dx", du, w["ssd_w_main"], mode="nt", res=dhn)
    grads = dict(
        ssd_w_in=jnp.concatenate([g_main, g_dt], axis=1)[None],
        ssd_conv_w=g_conv_w8[None, :SSD_CONV],
        ssd_conv_b=g_conv_b,
        ssd_dt_bias=jnp.stack([ddtb_f, ddtb_b]).reshape(1, 2, SSD_HEADS),
        ssd_a_log=jnp.stack([dalog_f, dalog_b]).reshape(1, 2, SSD_HEADS),
        ssd_d_exp=ddexp,
        ssd_norm_g=g_norm,
        ssd_w_out=g_w_out[None],
    )
    return dhn, grads


def _hg_block(rev, stTs, uq, uf, ui, lb):
    C = HG_CHUNK
    n = uq.shape[0] // C
    nh = uq.shape[1] // HG_EXPAND
    stTs = list(stTs)
    q = _silu(uq)
    f = lb + (1.0 - lb) * jax.nn.sigmoid(uf)
    k = 1.0 - f
    g = jnp.log(f)
    r = lax.broadcasted_iota(jnp.int32, (C, C), 0)
    c = lax.broadcasted_iota(jnp.int32, (C, C), 1)
    mask = (r <= c) if rev else (r >= c)
    Tm = mask.astype(F32)
    outs = [[None] * n for _ in range(nh)]
    for i in (reversed(range(n)) if rev else range(n)):
        sl = slice(i * C, (i + 1) * C)
        qi, ki, vi = q[sl], k[sl], ui[sl]
        G = _mm_tri(mask, g[sl])
        Gr = G[C // 2:C // 2 + 1]
        Gl = G[0:1] if rev else G[C - 1:C]
        q_in, k_in = qi * jnp.exp(G - Gr), ki * jnp.exp(Gr - G)
        q_st, k_st, e_l = qi * jnp.exp(G), ki * jnp.exp(Gl - G), jnp.exp(Gl)
        for h in range(nh):
            cs = slice(h * HG_EXPAND, (h + 1) * HG_EXPAND)
            att = jnp.where(mask, _mm_nt(q_in[:, cs], k_in[:, cs]), 0.0)
            outs[h][i] = _mm(att, vi[:, cs]) + _mm_nt(q_st[:, cs], stTs[h])
            stTs[h] = stTs[h] * e_l[:, cs] + _mm_tn(vi[:, cs], k_st[:, cs])
    o = jnp.concatenate([jnp.concatenate(outs[h], axis=0) for h in range(nh)], axis=1)
    return o, stTs


def _hg_specs(nb, rev_order, f_col):
    R = HG_ROWS
    gw = HG_HPS * HG_EXPAND
    ng = HG_HEADS // HG_HPS

    def bidx(s):
        return nb - 1 - s if rev_order else s

    def col(base):
        return pl.BlockSpec((R, gw), lambda h, s: (bidx(s), base * ng + h))

    out = pl.BlockSpec((R, gw), lambda h, s: (bidx(s), h))
    lb = pl.BlockSpec((1, gw), lambda h, s: (0, h))
    st = pl.BlockSpec((1, 1, HG_HPS, HG_EXPAND, HG_EXPAND), lambda h, s: (h, bidx(s), 0, 0, 0))
    return col(0), col(f_col), col(3), out, lb, st


def hg_scan_fwd(name, u, lb, rev):
    S = u.shape[0]
    nb = S // HG_ROWS
    ng = HG_HEADS // HG_HPS
    q_s, f_s, i_s, o_s, lb_s, st_s = _hg_specs(nb, rev, 2 if rev else 1)

    def body(uq, uf, ui, lb_ref, o_ref, st_ref, state):
        @pl.when(pl.program_id(1) == 0)
        def _():
            state[...] = jnp.zeros_like(state)

        st_ref[0, 0] = state[...]
        o, st = _hg_block(rev, [state[h] for h in range(HG_HPS)], uq[...], uf[...], ui[...], lb_ref[...])
        o_ref[...] = o
        for h in range(HG_HPS):
            state[h] = st[h]

    return pl.pallas_call(
        body, name=name, grid=(ng, nb), in_specs=[q_s, f_s, i_s, lb_s], out_specs=[o_s, st_s],
        out_shape=[jax.ShapeDtypeStruct((S, HG_W), F32),
                   jax.ShapeDtypeStruct((ng, nb, HG_HPS, HG_EXPAND, HG_EXPAND), F32)],
        scratch_shapes=[pltpu.VMEM((HG_HPS, HG_EXPAND, HG_EXPAND), F32)],
        compiler_params=_cp(("parallel", "arbitrary")),
    )(u, u, u, lb)


def hg_scan_bwd(name, u, lb, states, do, rev):
    S = u.shape[0]
    nb = S // HG_ROWS
    ng = HG_HEADS // HG_HPS
    q_s, f_s, i_s, o_s, lb_s, st_s = _hg_specs(nb, not rev, 2 if rev else 1)

    def body(uq, uf, ui, lb_ref, st_ref, do_ref, dq_ref, df_ref, di_ref, dlb_ref, dstate):
        @pl.when(pl.program_id(1) == 0)
        def _():
            dstate[...] = jnp.zeros_like(dstate)
            dlb_ref[...] = jnp.zeros_like(dlb_ref)

        _, vjp = jax.vjp(functools.partial(_hg_block, rev), [st_ref[0, 0, h] for h in range(HG_HPS)],
                         uq[...], uf[...], ui[...], lb_ref[...])
        dst, dq, df, di, dlb = vjp((do_ref[...], [dstate[h] for h in range(HG_HPS)]))
        dq_ref[...] = dq
        df_ref[...] = df
        di_ref[...] = di
        dlb_ref[...] += dlb
        for h in range(HG_HPS):
            dstate[h] = dst[h]

    return pl.pallas_call(
        body, name=name, grid=(ng, nb), in_specs=[q_s, f_s, i_s, lb_s, st_s, o_s],
        out_specs=[o_s, o_s, o_s, lb_s],
        out_shape=[jax.ShapeDtypeStruct((S, HG_W), F32)] * 3 + [jax.ShapeDtypeStruct((1, HG_W), F32)],
        scratch_shapes=[pltpu.VMEM((HG_HPS, HG_EXPAND, HG_EXPAND), F32)],
        compiler_params=_cp(("parallel", "arbitrary")),
    )(u, u, u, lb, states, do)


def _hg_lb(hgrn_lb, layer):
    m = jnp.max(hgrn_lb, axis=0, keepdims=True)
    e = jnp.exp(hgrn_lb - m)
    sm = e / jnp.sum(e, axis=0, keepdims=True)
    lb = jnp.zeros_like(sm[0:1])
    for i in range(1, layer + 1):
        lb = lb + sm[i:i + 1]
    return lb


def _hg_combine(of, ob, gate, ng):
    return _rms_groups(of + ob, ng, HG_EXPAND) * _silu(gate)


def hg_forward(x, hn, w, layer):
    u = matmul("hg_in", hn, w["hg_w_in"])
    (lb,) = smallcall("hg_lb", lambda t: (_hg_lb(t, layer),), [w["hgrn_lb"]], [(1, HG_W)])
    of, stf = hg_scan_fwd("hg_scan_f", u, lb, False)
    ob, stb = hg_scan_fwd("hg_scan_b", u, lb, True)
    (og,), _ = rowcall("hg_combine", lambda of, ob, gate, ng: ((_hg_combine(of, ob, gate, ng),), ()),
                       [of, ob, _col(u, HG_W, 4)], [w["hg_norm_g"]], [(HG_W, BF16)], tb=256)
    out = matmul("hg_out", og, w["hg_w_out"], res=x)
    return out, dict(hn=hn, u=u, lb=lb, of=of, ob=ob, stf=stf, stb=stb, og=og)


def hg_backward(dy, sv, w, layer):
    u, lb = sv["u"], sv["lb"]
    dog = matmul("hg_out_dx", dy, w["hg_w_out"], mode="nt")
    g_w_out = matmul("hg_out_dw", sv["og"], dy, mode="tn")

    def comb_bwd(of, ob, gate, dog, ng):
        _, vjp = jax.vjp(_hg_combine, of, ob, gate, ng)
        dof, _, dgate, dng = vjp(dog)
        return (dof, dgate), (dng,)

    (do, dgate), (g_norm,) = rowcall("hg_combine_bwd", comb_bwd, [sv["of"], sv["ob"], _col(u, HG_W, 4), dog],
                                     [w["hg_norm_g"]], [(HG_W, F32)] * 2, [(1, HG_W)], tb=256)
    dqf, dff, dif, dlbf = hg_scan_bwd("hg_scan_f_bwd", u, lb, sv["stf"], do, False)
    dqb, dfb, dib, dlbb = hg_scan_bwd("hg_scan_b_bwd", u, lb, sv["stb"], do, True)

    def gather(dqf, dqb, dff, dfb, dif, dib, dgate):
        return (jnp.concatenate([dqf + dqb, dff, dfb, dif + dib, dgate], axis=1),), ()

    (du,), _ = rowcall("hg_du", gather, [dqf, dqb, dff, dfb, dif, dib, dgate], [], [(HG_IN, BF16)], tb=256)

    def lb_bwd(t, dlbf, dlbb):
        _, vjp = jax.vjp(lambda t: _hg_lb(t, layer), t)
        return vjp(dlbf + dlbb)

    (g_lb,) = smallcall("hg_lb_bwd", lb_bwd, [w["hgrn_lb"], dlbf, dlbb], [(DEPTH, HG_W)])
    hn = sv["hn"]
    g_w_in = matmul("hg_in_dw", hn, du, mode="tn")
    dhn = matmul("hg_in_dx", du, w["hg_w_in"], mode="nt")
    return dhn, dict(hg_w_in=g_w_in[None], hg_norm_g=g_norm, hg_w_out=g_w_out[None], hgrn_lb=g_lb)


def _rope_tables(S):
    t = np.arange(S)
    row = (t // GRID_W).astype(np.float32)
    col = (t % GRID_W).astype(np.float32)
    inv = (ROPE_THETA ** (-np.arange(0, ROPE_AXIS, 2, dtype=np.float32) / ROPE_AXIS)).astype(np.float32)
    ar = jnp.asarray(row)[:, None] * jnp.asarray(inv)[None, :]
    ac = jnp.asarray(col)[:, None] * jnp.asarray(inv)[None, :]
    cos = jnp.concatenate([jnp.cos(ar), jnp.cos(ar), jnp.cos(ac), jnp.cos(ac)], axis=1)
    sin = jnp.concatenate([-jnp.sin(ar), jnp.sin(ar), -jnp.sin(ac), jnp.sin(ac)], axis=1)
    return cos.astype(F32), sin.astype(F32)


def _rope(x, cos, sin):
    h = ROPE_AXIS // 2
    sw = jnp.concatenate([x[:, h:2 * h], x[:, 0:h], x[:, 3 * h:4 * h], x[:, 2 * h:3 * h]], axis=1)
    return x * cos + sw * sin


def _at_pre(uq, uk, cos, sin, qg, kg):
    qs, ks = [], []
    for h in range(AT_HEADS):
        qs.append(_rope(_rms(uq[:, h * AT_HD:(h + 1) * AT_HD], qg), cos, sin) * (AT_HD ** -0.5))
    for h in range(AT_KV):
        ks.append(_rope(_rms(uk[:, h * AT_HD:(h + 1) * AT_HD], kg), cos, sin))
    return jnp.concatenate(qs, axis=1), jnp.concatenate(ks, axis=1)


def _stack_heads(x):
    return jnp.concatenate([x[:, :AT_HD], x[:, AT_HD:]], axis=0)


def _unstack_heads(x):
    t = x.shape[0] // 2
    return jnp.concatenate([x[:t], x[t:]], axis=1)


def at_flash_fwd(q, k, u):
    S = q.shape[0]
    tq, tk = _pick(S, 512), _pick(S, 4096)
    nq, nk = S // tq, S // tk
    gw = AT_GRP * AT_HD

    def body(q_ref, k_ref, v_ref, o_ref, lse_ref, m_s, l_s, acc):
        j = pl.program_id(2)

        @pl.when(j == 0)
        def _():
            m_s[...] = jnp.full_like(m_s, NEG_BIG)
            l_s[...] = jnp.zeros_like(l_s)
            acc[...] = jnp.zeros_like(acc)

        s = _mm_nt(_stack_heads(q_ref[...]), k_ref[...])
        m_new = jnp.maximum(m_s[...], jnp.max(s, axis=-1, keepdims=True))
        alpha = jnp.exp(m_s[...] - m_new)
        p = jnp.exp(s - m_new)
        l_s[...] = alpha * l_s[...] + jnp.sum(p, axis=-1, keepdims=True)
        acc[...] = alpha * acc[...] + _mm(p, v_ref[...])
        m_s[...] = m_new

        @pl.when(j == nk - 1)
        def _():
            o_ref[...] = _unstack_heads(acc[...] / l_s[...])
            lse_ref[0, 0] = m_s[...] + jnp.log(l_s[...])

    return pl.pallas_call(
        body, name="at_flash_fwd", grid=(AT_KV, nq, nk),
        in_specs=[pl.BlockSpec((tq, gw), lambda h, i, j: (i, h)),
                  pl.BlockSpec((tk, AT_HD), lambda h, i, j: (j, h)),
                  pl.BlockSpec((tk, AT_HD), lambda h, i, j: (j, (AT_QW + AT_KW) // AT_HD + h))],
        out_specs=[pl.BlockSpec((tq, gw), lambda h, i, j: (i, h)),
                   pl.BlockSpec((1, 1, 2 * tq, 1), lambda h, i, j: (h, i, 0, 0))],
        out_shape=[jax.ShapeDtypeStruct((S, AT_QW), F32), jax.ShapeDtypeStruct((AT_KV, nq, 2 * tq, 1), F32)],
        scratch_shapes=[pltpu.VMEM((2 * tq, 1), F32), pltpu.VMEM((2 * tq, 1), F32), pltpu.VMEM((2 * tq, AT_HD), F32)],
        compiler_params=_cp(("parallel", "parallel", "arbitrary")),
    )(q, k, u)


def at_flash_bwd(q, k, u, o, lse, do):
    S = q.shape[0]
    tq, tk = _pick(S, 512), _pick(S, 1024)
    nq, nk = S // tq, S // tk
    gw = AT_GRP * AT_HD

    def body(q_ref, k_ref, v_ref, o_ref, lse_ref, do_ref, dq_ref, dk_ref, dv_ref, dk_acc, dv_acc):
        j, i = pl.program_id(1), pl.program_id(2)

        @pl.when(i == 0)
        def _():
            dk_acc[...] = jnp.zeros_like(dk_acc)
            dv_acc[...] = jnp.zeros_like(dv_acc)

        q2 = _stack_heads(q_ref[...])
        do_blk = do_ref[...]
        do2 = _stack_heads(do_blk)
        delta = _stack_heads(do_blk * o_ref[...])
        delta = jnp.sum(delta, axis=-1, keepdims=True)
        kb, vb = k_ref[...], v_ref[...]
        p = jnp.exp(_mm_nt(q2, kb) - lse_ref[0, 0])
        dv_acc[...] += _mm_tn(p, do2)
        ds = p * (_mm_nt(do2, vb) - delta)
        dk_acc[...] += _mm_tn(ds, q2)
        dq = _unstack_heads(_mm(ds, kb))
        rows = pl.ds(pl.multiple_of(i * tq, tq), tq)

        @pl.when(j == 0)
        def _():
            dq_ref[rows, :] = dq

        @pl.when(j > 0)
        def _():
            dq_ref[rows, :] += dq

        @pl.when(i == nq - 1)
        def _():
            dk_ref[...] = dk_acc[...]
            dv_ref[...] = dv_acc[...]

    return pl.pallas_call(
        body, name="at_flash_bwd", grid=(AT_KV, nk, nq),
        in_specs=[pl.BlockSpec((tq, gw), lambda h, j, i: (i, h)),
                  pl.BlockSpec((tk, AT_HD), lambda h, j, i: (j, h)),
                  pl.BlockSpec((tk, AT_HD), lambda h, j, i: (j, (AT_QW + AT_KW) // AT_HD + h)),
                  pl.BlockSpec((tq, gw), lambda h, j, i: (i, h)),
                  pl.BlockSpec((1, 1, 2 * tq, 1), lambda h, j, i: (h, i, 0, 0)),
                  pl.BlockSpec((tq, gw), lambda h, j, i: (i, h))],
        out_specs=[pl.BlockSpec((S, gw), lambda h, j, i: (0, h)),
                   pl.BlockSpec((tk, AT_HD), lambda h, j, i: (j, h)),
                   pl.BlockSpec((tk, AT_HD), lambda h, j, i: (j, h))],
        out_shape=[jax.ShapeDtypeStruct((S, AT_QW), F32), jax.ShapeDtypeStruct((S, AT_KW), F32),
                   jax.ShapeDtypeStruct((S, AT_KW), F32)],
        scratch_shapes=[pltpu.VMEM((tk, AT_HD), F32), pltpu.VMEM((tk, AT_HD), F32)],
        compiler_params=_cp(("parallel", "arbitrary", "arbitrary")),
    )(q, k, u, o, lse, do)


def at_forward(x, hn, w):
    S = x.shape[0]
    u = matmul("at_in", hn, w["at_w_in"])
    cos, sin = _rope_tables(S)
    (q, k), _ = rowcall("at_pre", lambda uq, uk, c, s, qg, kg: (_at_pre(uq, uk, c, s, qg, kg), ()),
                        [_col(u, AT_QW, 0), _col(u, AT_KW, 2), cos, sin], [w["at_q_norm_g"], w["at_k_norm_g"]],
                        [(AT_QW, BF16), (AT_KW, BF16)], tb=256)
    o, lse = at_flash_fwd(q, k, u)
    (og,), _ = rowcall("at_gate", lambda o, gate: ((o * _silu(gate),), ()), [o, _col(u, AT_QW, 2)], [],
                       [(AT_QW, BF16)], tb=256)
    out = matmul("at_out", og, w["at_w_out"], res=x)
    return out, dict(hn=hn, u=u, cos=cos, sin=sin, q=q, k=k, o=o, lse=lse, og=og)


def at_backward(dy, sv, w):
    u = sv["u"]
    dog = matmul("at_out_dx", dy, w["at_w_out"], mode="nt")
    g_w_out = matmul("at_out_dw", sv["og"], dy, mode="tn")

    def gate_bwd(o, gate, dog):
        _, vjp = jax.vjp(lambda o, gate: o * _silu(gate), o, gate)
        return vjp(dog), ()

    (do, dgate), _ = rowcall("at_gate_bwd", gate_bwd, [sv["o"], _col(u, AT_QW, 2), dog], [],
                             [(AT_QW, F32)] * 2, tb=256)
    dq, dk, dv = at_flash_bwd(sv["q"], sv["k"], u, sv["o"], sv["lse"], do)

    def pre_bwd(uq, uk, cos, sin, dq, dk, dv, dgate, qg, kg):
        _, vjp = jax.vjp(lambda uq, uk, qg, kg: _at_pre(uq, uk, cos, sin, qg, kg), uq, uk, qg, kg)
        duq, duk, dqg, dkg = vjp((dq, dk))
        return (jnp.concatenate([duq, duk, dv, dgate], axis=1),), (dqg, dkg)

    (du,), (g_qg, g_kg) = rowcall(
        "at_pre_bwd", pre_bwd, [_col(u, AT_QW, 0), _col(u, AT_KW, 2), sv["cos"], sv["sin"], dq, dk, dv, dgate],
        [w["at_q_norm_g"], w["at_k_norm_g"]], [(AT_IN, BF16)], [(1, AT_HD), (1, AT_HD)], tb=128)
    hn = sv["hn"]
    g_w_in = matmul("at_in_dw", hn, du, mode="tn")
    dhn = matmul("at_in_dx", du, w["at_w_in"], mode="nt")
    return dhn, dict(at_w_in=g_w_in[None], at_q_norm_g=g_qg, at_k_norm_g=g_kg, at_w_out=g_w_out[None])


def _t5_bucket_np(rel):
    half = REL_BUCKETS // 2
    exact = half // 2
    n = np.abs(rel)
    large = exact + (np.log(np.maximum(n, 1).astype(np.float32) / exact)
                     / math.log(REL_MAX_DIST / exact) * (half - exact)).astype(np.int32)
    large = np.minimum(large, half - 1)
    return np.where(rel > 0, half, 0) + np.where(n < exact, n, large)


def _dl_tq(S, dil):
    return min(128, S // dil)


def _dl_bias_maps(tq, dil):
    W = tq + 2 * DL_STEPS
    i = np.arange(tq)[:, None]
    wdx = np.arange(W)[None, :]
    dm = wdx - DL_STEPS - i
    bucket = _t5_bucket_np(dm * dil).reshape(-1).astype(np.int32)
    band = np.where(np.abs(dm) <= DL_STEPS, 0.0, NEG_BIG).reshape(1, -1).astype(np.float32)
    onehot = (jnp.asarray(bucket)[None, :] == jnp.arange(REL_BUCKETS, dtype=jnp.int32)[:, None]).astype(F32)
    return onehot, jnp.asarray(band)


def _dl_attend(q, kwin, vwin, T, valid):
    tq = q.shape[0]
    os, ls = [], []
    for h in range(DL_HEADS):
        sl = slice(h * DL_HD, (h + 1) * DL_HD)
        s = _mm_nt(q[:, sl] * (DL_HD ** -0.5), kwin[:, sl]) + T[h]
        s = jnp.where(valid, s, NEG_BIG)
        m = lax.stop_gradient(jnp.max(s, axis=-1, keepdims=True))
        lse = m + jnp.log(jnp.sum(jnp.exp(s - m), axis=-1, keepdims=True))
        p = jnp.exp(s - lse)
        os.append(_mm(p, vwin[:, sl]))
        ls.append(jnp.broadcast_to(lse, (tq, DL_HD)))
    return jnp.concatenate(os, axis=1), jnp.concatenate(ls, axis=1)


def _dl_specs(tq, Ls):
    nb = Ls // tq
    hs = DL_STEPS
    per = tq // hs
    nh = Ls // hs

    def main(c):
        return pl.BlockSpec((tq, DL_W), lambda r, i: (r * nb + i, c))

    def prev(c):
        return pl.BlockSpec((hs, DL_W), lambda r, i: (r * nh + jnp.maximum(i * per - 1, 0), c))

    def nxt(c):
        return pl.BlockSpec((hs, DL_W), lambda r, i: (r * nh + jnp.minimum((i + 1) * per, nh - 1), c))

    return nb, main, prev, nxt


def _dl_valid(i, tq, Ls):
    W = tq + 2 * DL_STEPS
    mk = i * tq - DL_STEPS + lax.broadcasted_iota(jnp.int32, (1, W), 1)
    return (mk >= 0) & (mk < Ls)


def dl_attn_fwd(gi, dil, u, T):
    S = u.shape[0]
    Ls = S // dil
    tq = _dl_tq(S, dil)
    nb, main, prev, nxt = _dl_specs(tq, Ls)
    out = main(0)

    def body(q_ref, kp, kc, kn, vp, vc, vn, T_ref, o_ref, l_ref):
        kwin = jnp.concatenate([kp[...], kc[...], kn[...]], axis=0)
        vwin = jnp.concatenate([vp[...], vc[...], vn[...]], axis=0)
        o, l = _dl_attend(q_ref[...], kwin, vwin, T_ref[...], _dl_valid(pl.program_id(1), tq, Ls))
        o_ref[...] = o
        l_ref[...] = l

    o, l = pl.pallas_call(
        body, name=f"dl_attn_fwd{gi}", grid=(dil, nb),
        in_specs=[main(0), prev(1), main(1), nxt(1), prev(2), main(2), nxt(2),
                  pl.BlockSpec(T.shape, _const_map(3))],
        out_specs=[out, out],
        out_shape=[jax.ShapeDtypeStruct((S, DL_W), F32)] * 2,
        compiler_params=_cp(("parallel", "parallel")),
    )(u, u, u, u, u, u, u, T)
    return o, l


def dl_attn_bwd(gi, dil, u, T, do, dl, dgate=None):
    S = u.shape[0]
    Ls = S // dil
    tq = _dl_tq(S, dil)
    hs = DL_STEPS
    W = tq + 2 * hs
    nb, main, prev, nxt = _dl_specs(tq, Ls)
    out = main(0)
    win = pl.BlockSpec((1, W, DL_W), lambda r, i: (r * nb + i, 0, 0))

    def body(q_ref, kp, kc, kn, vp, vc, vn, T_ref, do_ref, dl_ref, dq_ref, dkw_ref, dvw_ref, dT_ref):
        first = (pl.program_id(0) == 0) & (pl.program_id(1) == 0)

        @pl.when(first)
        def _():
            dT_ref[...] = jnp.zeros_like(dT_ref)

        kwin = jnp.concatenate([kp[...], kc[...], kn[...]], axis=0)
        vwin = jnp.concatenate([vp[...], vc[...], vn[...]], axis=0)
        valid = _dl_valid(pl.program_id(1), tq, Ls)
        _, vjp = jax.vjp(lambda q, k, v, T: _dl_attend(q, k, v, T, valid), q_ref[...], kwin, vwin, T_ref[...])
        dq, dkw, dvw, dT = vjp((do_ref[...], dl_ref[...]))
        dq_ref[...] = dq
        dkw_ref[0] = dkw
        dvw_ref[0] = dvw
        dT_ref[...] += dT

    dq, dkw, dvw, dT = pl.pallas_call(
        body, name=f"dl_attn_bwd{gi}", grid=(dil, nb),
        in_specs=[main(0), prev(1), main(1), nxt(1), prev(2), main(2), nxt(2),
                  pl.BlockSpec(T.shape, _const_map(3)), out, out],
        out_specs=[out, win, win, pl.BlockSpec(T.shape, _const_map(3))],
        out_shape=[jax.ShapeDtypeStruct((S, DL_W), F32),
                   jax.ShapeDtypeStruct((dil * nb, W, DL_W), F32),
                   jax.ShapeDtypeStruct((dil * nb, W, DL_W), F32),
                   jax.ShapeDtypeStruct(T.shape, F32)],
        compiler_params=_cp(("arbitrary", "arbitrary")),
    )(u, u, u, u, u, u, u, T, do, dl)

    per = tq // hs
    n_out = 3 if dgate is None else 4

    def fold(*refs):
        dq_ref, kc, kp, kn, vc, vp, vn = refs[:7]
        du_ref = refs[-1]
        i = pl.program_id(1)
        has_p = (i > 0).astype(F32)
        has_n = (i < nb - 1).astype(F32)
        du_ref[:, 0:DL_W] = dq_ref[...].astype(BF16)
        for c, (c_ref, p_ref, n_ref) in enumerate(((kc, kp, kn), (vc, vp, vn)), start=1):
            mid = c_ref[0, hs:hs + tq, :]
            top = mid[0:hs] + p_ref[0] * has_p
            bot = mid[tq - hs:tq] + n_ref[0] * has_n
            parts = [top, bot] if tq == 2 * hs else ([top, mid[hs:tq - hs], bot] if tq > 2 * hs else [top + n_ref[0] * has_n])
            du_ref[:, c * DL_W:(c + 1) * DL_W] = jnp.concatenate(parts, axis=0).astype(BF16)
        if dgate is not None:
            du_ref[:, 3 * DL_W:4 * DL_W] = refs[7][...].astype(BF16)

    wfull = pl.BlockSpec((1, W, DL_W), lambda r, i: (r * nb + i, 0, 0))
    wprev = pl.BlockSpec((1, hs, DL_W), lambda r, i: (r * nb + jnp.maximum(i - 1, 0), per + 1, 0))
    wnext = pl.BlockSpec((1, hs, DL_W), lambda r, i: (r * nb + jnp.minimum(i + 1, nb - 1), 0, 0))
    extra_specs, extra_args = ([], []) if dgate is None else ([out], [dgate])
    du = pl.pallas_call(
        fold, name=f"dl_fold{gi}", grid=(dil, nb),
        in_specs=[out, wfull, wprev, wnext, wfull, wprev, wnext] + extra_specs,
        out_specs=pl.BlockSpec((tq, n_out * DL_W), lambda r, i: (r * nb + i, 0)),
        out_shape=jax.ShapeDtypeStruct((S, n_out * DL_W), BF16),
        compiler_params=_cp(("parallel", "parallel")),
    )(dq, dkw, dkw, dkw, dvw, dvw, dvw, *extra_args)
    return du, dT


def _dl_merge(o0, o1, o2, l0, l1, l2, gate):
    m = jnp.maximum(jnp.maximum(l0, l1), l2)
    e0, e1, e2 = jnp.exp(l0 - m), jnp.exp(l1 - m), jnp.exp(l2 - m)
    den = e0 + e1 + e2
    return ((e0 * o0 + e1 * o1 + e2 * o2) / den) * _silu(gate)


DL_DILS = tuple(d for _, d in DL_PAIRS)


def _dl_group_weights(w_in):
    g3 = 3 * DL_W
    return [jnp.concatenate([w_in[:, :g3], w_in[:, 3 * g3:]], axis=1), w_in[:, g3:2 * g3], w_in[:, 2 * g3:3 * g3]]


def dl_forward(x, hns, w):
    S = x.shape[0]
    wg = _dl_group_weights(w["dl_w_in"])
    rbT = w["rel_bias"].T
    us, os, ls, Ts, maps = [], [], [], [], []
    for gi, dil in enumerate(DL_DILS):
        u = matmul(f"dl_in{gi}", hns[gi], wg[gi])
        tq = _dl_tq(S, dil)
        W = tq + 2 * DL_STEPS
        onehot, band = _dl_bias_maps(tq, dil)
        (T,) = smallcall(f"dl_bias{gi}", lambda rbT, oh, band: (_mm_exact(rbT, oh) + band,), [rbT, onehot, band],
                         [(DL_HEADS, tq * W)])
        T = T.reshape(DL_HEADS, tq, W)
        o, l = dl_attn_fwd(gi, dil, u, T)
        us.append(u)
        os.append(o)
        ls.append(l)
        Ts.append(T)
        maps.append(onehot)
    rows = [a if d == 1 else _perm(a, d) for a, d in zip(os + ls, DL_DILS * 2)] + [_col(us[0], DL_W, 3)]
    (og,), _ = rowcall("dl_merge", lambda *a: ((_dl_merge(*a),), ()), rows, [], [(DL_W, BF16)], tb=256)
    out = matmul("dl_out", og, w["dl_w_out"], res=x)
    return out, dict(hns=hns, us=us, os=os, ls=ls, Ts=Ts, maps=maps, og=og, wg=wg)


def dl_backward(dy, sv, w):
    us = sv["us"]
    dog = matmul("dl_out_dx", dy, w["dl_w_out"], mode="nt")
    g_w_out = matmul("dl_out_dw", sv["og"], dy, mode="tn")

    def merge_bwd(o0, o1, o2, l0, l1, l2, gate, dog):
        _, vjp = jax.vjp(_dl_merge, o0, o1, o2, l0, l1, l2, gate)
        return vjp(dog), ()

    rows = [a if d == 1 else _perm(a, d) for a, d in zip(sv["os"] + sv["ls"], DL_DILS * 2)] + [_col(us[0], DL_W, 3), dog]
    grads7, _ = rowcall("dl_merge_bwd", merge_bwd, rows, [], [(DL_W, F32, d) for d in DL_DILS * 2] + [(DL_W, F32)], tb=256)
    dos, dls, dgate = grads7[0:3], grads7[3:6], grads7[6]
    g_rbT, g_ws, dhns = None, [], []
    for gi, dil in enumerate(DL_DILS):
        du, dT = dl_attn_bwd(gi, dil, us[gi], sv["Ts"][gi], dos[gi], dls[gi], dgate if gi == 0 else None)
        (g,) = smallcall(f"dl_bias_bwd{gi}", lambda dT, oh: (_mm_nt_exact(dT, oh),),
                         [dT.reshape(DL_HEADS, -1), sv["maps"][gi]], [(DL_HEADS, REL_BUCKETS)])
        g_rbT = g if g_rbT is None else g_rbT + g
        g_ws.append(matmul(f"dl_in_dw{gi}", sv["hns"][gi], du, mode="tn"))
        dhns.append(matmul(f"dl_in_dx{gi}", du, sv["wg"][gi], mode="nt"))
    g3 = 3 * DL_W
    g_w_in = jnp.concatenate([g_ws[0][:, :g3], g_ws[1], g_ws[2], g_ws[0][:, g3:]], axis=1)
    return tuple(dhns), dict(dl_w_in=g_w_in[None], dl_w_out=g_w_out[None], rel_bias=g_rbT.T)


_FWD = (ssd_forward, hg_forward, at_forward, dl_forward)
_BWD = (ssd_backward, hg_backward, at_backward, dl_backward)


def _norm_dils(layer):
    return DL_DILS if layer % 4 == 3 else (1,)


def local_step(x, tgt, w):
    saved = []
    h = x
    for layer in range(DEPTH):
        hn = norm_fwd(f"norm{layer}", h, w["norm_g"][layer:layer + 1], _norm_dils(layer))
        extra = (layer,) if layer % 4 == 1 else ()
        h_next, sv = _FWD[layer % 4](h, hn, w, *extra)
        saved.append((h, sv))
        h = h_next
    loss, dh, g_final = loss_head(h, tgt, w["final_g"].reshape(1, D_MODEL))
    grads = {}
    g_norm = [None] * DEPTH
    for layer in reversed(range(DEPTH)):
        h_in, sv = saved[layer]
        extra = (layer,) if layer % 4 == 1 else ()
        dhn, g = _BWD[layer % 4](dh, sv, w, *extra)
        grads.update(g)
        dh, g_norm[layer] = norm_bwd(f"norm{layer}_bwd", h_in, w["norm_g"][layer:layer + 1], dhn, dh, _norm_dils(layer))
    grads["norm_g"] = jnp.concatenate(g_norm, axis=0)
    grads["final_g"] = g_final.reshape(D_MODEL)
    grads["ssd_d"] = jnp.sum(grads.pop("ssd_d_exp").reshape(SSD_HEADS, SSD_HEADDIM), axis=1)[None]
    return loss, dh, grads


IN_NAMES = ("ssd_w_in", "hg_w_in", "at_w_in", "dl_w_in")
OUT_NAMES = ("ssd_w_out", "hg_w_out", "at_w_out", "dl_w_out")
IN_COLS = (SSD_IN // 4, HG_IN // 4, AT_IN // 4, DL_IN // 4)
OUT_ROWS = (SSD_DI // 4, HG_W // 4, AT_QW // 4, DL_W // 4)
PACK_IN = sum(IN_COLS)
PACK_OUT = sum(OUT_ROWS)
N_CHIPS = 4
N_DEV = 8
HBM = pl.BlockSpec(memory_space=pl.ANY)


def _mesh_pos():
    return lax.axis_index("x"), lax.axis_index("y"), lax.axis_index("c")


def _other_chips(x, y):
    return [(1 - x, y), (x, 1 - y), (1 - x, 1 - y)]


def gather_weights(p_in, p_out, p_small):
    h_in, h_out = p_in.shape[0] // 2, p_out.shape[0] // 2

    def body(pin, pout, psm, gin, gout, gsm, send, recv):
        x, y, c = _mesh_pos()
        me = 2 * x + y
        sib = (x, y, 1 - c)
        chips = _other_chips(x, y)

        def rows(half, n):
            return pl.ds(pl.multiple_of(half * n, n), n)

        def rc(src, dst, k, to):
            return pltpu.make_async_remote_copy(src_ref=src, dst_ref=dst, send_sem=send.at[k], recv_sem=recv.at[k],
                                                device_id=to, device_id_type=MESH)

        started = []
        for j, (px, py) in enumerate(chips):
            to = (px, py, c)
            started += [rc(pin.at[rows(c, h_in)], gin.at[me, rows(c, h_in)], 3 * j, to),
                        rc(pout.at[rows(c, h_out)], gout.at[me, rows(c, h_out)], 3 * j + 1, to),
                        rc(psm, gsm.at[me], 3 * j + 2, to)]
        for cp in started:
            cp.start()
        for j, (px, py) in enumerate(chips):
            kp = 2 * px + py
            frm = (px, py, c)
            rc(pin.at[rows(c, h_in)], gin.at[kp, rows(c, h_in)], 3 * j, frm).wait_recv()
            f_in = rc(gin.at[kp, rows(c, h_in)], gin.at[kp, rows(c, h_in)], 9 + 2 * j, sib)
            f_in.start()
            rc(pout.at[rows(c, h_out)], gout.at[kp, rows(c, h_out)], 3 * j + 1, frm).wait_recv()
            f_out = rc(gout.at[kp, rows(c, h_out)], gout.at[kp, rows(c, h_out)], 10 + 2 * j, sib)
            f_out.start()
            rc(psm, gsm.at[kp], 3 * j + 2, frm).wait_recv()
            started += [f_in, f_out]
        for j, (px, py) in enumerate(chips):
            kp = 2 * px + py
            rc(gin.at[kp, rows(1 - c, h_in)], gin.at[kp, rows(1 - c, h_in)], 9 + 2 * j, sib).wait_recv()
            rc(gout.at[kp, rows(1 - c, h_out)], gout.at[kp, rows(1 - c, h_out)], 10 + 2 * j, sib).wait_recv()
        for cp in started:
            cp.wait_send()

    return pl.pallas_call(
        body, name="gather_weights", in_specs=[HBM, HBM, HBM], out_specs=[HBM, HBM, HBM],
        out_shape=[jax.ShapeDtypeStruct((N_CHIPS,) + p_in.shape, p_in.dtype),
                   jax.ShapeDtypeStruct((N_CHIPS,) + p_out.shape, p_out.dtype),
                   jax.ShapeDtypeStruct((N_CHIPS,) + p_small.shape, p_small.dtype)],
        scratch_shapes=[pltpu.SemaphoreType.DMA((15,)), pltpu.SemaphoreType.DMA((15,))],
        compiler_params=pltpu.CompilerParams(has_side_effects=True),
    )(p_in, p_out, p_small)


def swap_halves(g_in, g_out):
    h_in, h_out = g_in.shape[1] // 2, g_out.shape[1] // 2

    def body(gi, go, ri, ro, send, recv):
        x, y, c = _mesh_pos()
        sib = (x, y, 1 - c)

        def rows(half, n):
            return pl.ds(pl.multiple_of(half * n, n), n)

        cps = [pltpu.make_async_remote_copy(src_ref=gi.at[:, rows(1 - c, h_in)], dst_ref=ri, send_sem=send.at[0],
                                            recv_sem=recv.at[0], device_id=sib, device_id_type=MESH),
               pltpu.make_async_remote_copy(src_ref=go.at[:, rows(1 - c, h_out)], dst_ref=ro, send_sem=send.at[1],
                                            recv_sem=recv.at[1], device_id=sib, device_id_type=MESH)]
        for cp in cps:
            cp.start()
        for cp in cps:
            cp.wait()

    return pl.pallas_call(
        body, name="swap_halves", in_specs=[HBM, HBM], out_specs=[HBM, HBM],
        out_shape=[jax.ShapeDtypeStruct((N_CHIPS, h_in, g_in.shape[2]), g_in.dtype),
                   jax.ShapeDtypeStruct((N_CHIPS, h_out, g_out.shape[2]), g_out.dtype)],
        scratch_shapes=[pltpu.SemaphoreType.DMA((2,)), pltpu.SemaphoreType.DMA((2,))],
        compiler_params=pltpu.CompilerParams(has_side_effects=True),
    )(g_in, g_out)


def half_add(name, g, r, c_idx, tb):
    _, rows2, C = g.shape
    h = rows2 // 2
    nb = h // tb

    def body(c_ref, g_ref, r_ref, f_ref, b_ref):
        s = g_ref[...] + r_ref[...]
        f_ref[...] = s
        b_ref[...] = s.astype(BF16)

    grid_spec = pltpu.PrefetchScalarGridSpec(
        num_scalar_prefetch=1, grid=(N_CHIPS, nb),
        in_specs=[pl.BlockSpec((1, tb, C), lambda k, i, c: (k, c[0] * nb + i, 0)),
                  pl.BlockSpec((1, tb, C), lambda k, i, c: (k, i, 0))],
        out_specs=[pl.BlockSpec((1, tb, C), lambda k, i, c: (k, i, 0))] * 2)
    return pl.pallas_call(
        body, name=name, grid_spec=grid_spec,
        out_shape=[jax.ShapeDtypeStruct((N_CHIPS, h, C), F32), jax.ShapeDtypeStruct((N_CHIPS, h, C), BF16)],
        compiler_params=_cp(("parallel", "parallel")),
    )(c_idx, g, r)


def scatter_chips(b_in, b_out):
    def body(bi, bo, ri, ro, send, recv):
        x, y, c = _mesh_pos()
        cps = []
        for j, (px, py) in enumerate(_other_chips(x, y)):
            kp = 2 * px + py
            to = (px, py, c)
            cps += [pltpu.make_async_remote_copy(src_ref=bi.at[kp], dst_ref=ri.at[j], send_sem=send.at[2 * j],
                                                 recv_sem=recv.at[2 * j], device_id=to, device_id_type=MESH),
                    pltpu.make_async_remote_copy(src_ref=bo.at[kp], dst_ref=ro.at[j], send_sem=send.at[2 * j + 1],
                                                 recv_sem=recv.at[2 * j + 1], device_id=to, device_id_type=MESH)]
        for cp in cps:
            cp.start()
        for cp in cps:
            cp.wait()

    return pl.pallas_call(
        body, name="scatter_chips", in_specs=[HBM, HBM], out_specs=[HBM, HBM],
        out_shape=[jax.ShapeDtypeStruct((3,) + b_in.shape[1:], BF16), jax.ShapeDtypeStruct((3,) + b_out.shape[1:], BF16)],
        scratch_shapes=[pltpu.SemaphoreType.DMA((6,)), pltpu.SemaphoreType.DMA((6,))],
        compiler_params=pltpu.CompilerParams(has_side_effects=True),
    )(b_in, b_out)


def chip_sum(name, f, r, me_idx, tb):
    _, h, C = f.shape
    nb = h // tb

    def body(me_ref, f_ref, r0, r1, r2, o_ref):
        o_ref[...] = ((f_ref[0] + r0[0].astype(F32)) + r1[0].astype(F32)) + r2[0].astype(F32)

    def slot(j):
        return pl.BlockSpec((1, tb, C), lambda i, me: (j, i, 0))

    grid_spec = pltpu.PrefetchScalarGridSpec(
        num_scalar_prefetch=1, grid=(nb,),
        in_specs=[pl.BlockSpec((1, tb, C), lambda i, me: (me[0], i, 0)), slot(0), slot(1), slot(2)],
        out_specs=pl.BlockSpec((tb, C), lambda i, me: (i, 0)))
    return pl.pallas_call(
        body, name=name, grid_spec=grid_spec, out_shape=jax.ShapeDtypeStruct((h, C), F32),
        compiler_params=_cp(("parallel",)),
    )(me_idx, f, r, r, r)


def share_halves(f_in, f_out):
    def body(fi, fo, oi, oo, send, recv):
        x, y, c = _mesh_pos()
        sib = (x, y, 1 - c)
        cps = [pltpu.make_async_remote_copy(src_ref=fi, dst_ref=oi, send_sem=send.at[0], recv_sem=recv.at[0],
                                            device_id=sib, device_id_type=MESH),
               pltpu.make_async_remote_copy(src_ref=fo, dst_ref=oo, send_sem=send.at[1], recv_sem=recv.at[1],
                                            device_id=sib, device_id_type=MESH)]
        for cp in cps:
            cp.start()
        for cp in cps:
            cp.wait()

    return pl.pallas_call(
        body, name="share_halves", in_specs=[HBM, HBM], out_specs=[HBM, HBM],
        out_shape=[jax.ShapeDtypeStruct(f_in.shape, F32), jax.ShapeDtypeStruct(f_out.shape, F32)],
        scratch_shapes=[pltpu.SemaphoreType.DMA((2,)), pltpu.SemaphoreType.DMA((2,))],
        compiler_params=pltpu.CompilerParams(has_side_effects=True),
    )(f_in, f_out)


def gather_small(pack):
    def body(p, g, send, recv, lsem):
        x, y, c = _mesh_pos()
        me = 4 * x + 2 * y + c
        local = pltpu.make_async_copy(p, g.at[me], lsem)
        local.start()
        cps = []
        k = 0
        for fx in (0, 1):
            for fy in (0, 1):
                for fc in (0, 1):
                    if fx + fy + fc == 0:
                        continue
                    to = (x ^ fx, y ^ fy, c ^ fc)
                    cps.append((pltpu.make_async_remote_copy(src_ref=p, dst_ref=g.at[me], send_sem=send.at[k],
                                                             recv_sem=recv.at[k], device_id=to, device_id_type=MESH), to, k))
                    k += 1
        for cp, _, _ in cps:
            cp.start()
        for cp, to, k in cps:
            frm = 4 * to[0] + 2 * to[1] + to[2]
            pltpu.make_async_remote_copy(src_ref=p, dst_ref=g.at[frm], send_sem=send.at[k], recv_sem=recv.at[k],
                                         device_id=to, device_id_type=MESH).wait_recv()
        for cp, _, _ in cps:
            cp.wait_send()
        local.wait()

    return pl.pallas_call(
        body, name="gather_small", in_specs=[HBM], out_specs=HBM,
        out_shape=jax.ShapeDtypeStruct((N_DEV,) + pack.shape, pack.dtype),
        scratch_shapes=[pltpu.SemaphoreType.DMA((7,)), pltpu.SemaphoreType.DMA((7,)), pltpu.SemaphoreType.DMA],
        compiler_params=pltpu.CompilerParams(has_side_effects=True),
    )(pack)


def _adamw(w, g, m, v):
    m = ADAM_B1 * m + (1.0 - ADAM_B1) * g
    v = ADAM_B2 * v + (1.0 - ADAM_B2) * (g * g)
    m_hat = m / (1.0 - ADAM_B1 ** ADAM_STEP)
    v_hat = v / (1.0 - ADAM_B2 ** ADAM_STEP)
    delta = -ADAM_LR * (m_hat / (jnp.sqrt(v_hat) + ADAM_EPS) + ADAM_WD * w)
    return delta, m, v


def adamw_big(name, w, g, m, v):
    shp = w.shape
    flat = lambda a: a.reshape(shp[-2], shp[-1])
    (d, nm, nv), _ = rowcall(name, lambda w, g, m, v: (_adamw(w, g, m, v), ()), [flat(w), flat(g), flat(m), flat(v)], [],
                             [(shp[-1], F32)] * 3, tb=256)
    return d.reshape(shp), nm.reshape(shp), nv.reshape(shp)


def _pack_small(arrs):
    flat = jnp.concatenate([a.reshape(-1) for a in arrs])
    n = flat.shape[0]
    rows = -(-n // (SUBLANE * LANE)) * SUBLANE
    return jnp.pad(flat, (0, rows * LANE - n)).reshape(rows, LANE)


def _unpack_small(pack, shapes):
    flat = pack.reshape(-1)
    outs, off = [], 0
    for s in shapes:
        n = int(np.prod(s))
        outs.append(flat[off:off + n].reshape(s))
        off += n
    return outs


SMALL_NAMES = ("norm_g", "final_g", "rel_bias", "hgrn_lb", "ssd_conv_w", "ssd_conv_b", "ssd_dt_bias", "ssd_a_log",
               "ssd_d", "ssd_norm_g", "hg_norm_g", "at_q_norm_g", "at_k_norm_g")
ALL_NAMES = ("norm_g", "final_g", "rel_bias", "hgrn_lb", "ssd_w_in", "ssd_conv_w", "ssd_conv_b", "ssd_dt_bias",
             "ssd_a_log", "ssd_d", "ssd_norm_g", "ssd_w_out", "hg_w_in", "hg_norm_g", "hg_w_out", "at_w_in",
             "at_q_norm_g", "at_k_norm_g", "at_w_out", "dl_w_in", "dl_w_out")


def kernel(x, norm_g, final_g, rel_bias, hgrn_lb, ssd_w_in, ssd_conv_w, ssd_conv_b, ssd_dt_bias, ssd_a_log, ssd_d, ssd_norm_g, ssd_w_out, hg_w_in, hg_norm_g, hg_w_out, at_w_in, at_q_norm_g, at_k_norm_g, at_w_out, dl_w_in, dl_w_out, loss_target, m_norm_g, m_final_g, m_rel_bias, m_hgrn_lb, m_ssd_w_in, m_ssd_conv_w, m_ssd_conv_b, m_ssd_dt_bias, m_ssd_a_log, m_ssd_d, m_ssd_norm_g, m_ssd_w_out, m_hg_w_in, m_hg_norm_g, m_hg_w_out, m_at_w_in, m_at_q_norm_g, m_at_k_norm_g, m_at_w_out, m_dl_w_in, m_dl_w_out, v_norm_g, v_final_g, v_rel_bias, v_hgrn_lb, v_ssd_w_in, v_ssd_conv_w, v_ssd_conv_b, v_ssd_dt_bias, v_ssd_a_log, v_ssd_d, v_ssd_norm_g, v_ssd_w_out, v_hg_w_in, v_hg_norm_g, v_hg_w_out, v_at_w_in, v_at_q_norm_g, v_at_k_norm_g, v_at_w_out, v_dl_w_in, v_dl_w_out):
    args = locals()
    W = {n: args[n] for n in ALL_NAMES}
    M = {n: args["m_" + n] for n in ALL_NAMES}
    V = {n: args["v_" + n] for n in ALL_NAMES}
    xi, yi, ci = lax.axis_index("x"), lax.axis_index("y"), lax.axis_index("c")
    chip = 2 * xi + yi
    conv_shard = SSD_CONV_CH // N_CHIPS
    hgn_shard = HG_W // N_CHIPS

    p_in = jnp.concatenate([W[n][0].astype(BF16) for n in IN_NAMES], axis=1)
    p_out = jnp.concatenate([W[n][0].astype(BF16) for n in OUT_NAMES], axis=0)
    p_small = jnp.concatenate([
        jnp.pad(ssd_conv_w[0], ((0, 0), (0, D_MODEL - conv_shard))),
        jnp.pad(hg_norm_g, ((0, 0), (0, D_MODEL - hgn_shard)))], axis=0)
    g_in, g_out, g_small = gather_weights(p_in, p_out, p_small)

    def slot(stack, own, k):
        return jnp.where(chip == k, own, stack[k])

    full = {}
    off = 0
    for n, cols in zip(IN_NAMES, IN_COLS):
        full[n] = jnp.concatenate([slot(g_in, p_in, k)[:, off:off + cols] for k in range(N_CHIPS)], axis=1)
        off += cols
    off = 0
    for n, rows in zip(OUT_NAMES, OUT_ROWS):
        full[n] = jnp.concatenate([slot(g_out, p_out, k)[off:off + rows] for k in range(N_CHIPS)], axis=0)
        off += rows
    conv_full = jnp.concatenate([slot(g_small, p_small, k)[:SSD_CONV, :conv_shard] for k in range(N_CHIPS)], axis=1)
    hgn_full = jnp.concatenate([slot(g_small, p_small, k)[SSD_CONV:SSD_CONV + 1, :hgn_shard] for k in range(N_CHIPS)], axis=1)
    w = dict(
        norm_g=norm_g, final_g=final_g, rel_bias=rel_bias, hgrn_lb=hgrn_lb,
        ssd_w_main=full["ssd_w_in"][:, :SSD_MAIN], ssd_w_dt=full["ssd_w_in"][:, SSD_MAIN:],
        ssd_conv_w8=jnp.concatenate([conv_full, jnp.zeros((1, SSD_CONV_CH), F32)], axis=0),
        ssd_conv_b=ssd_conv_b, ssd_dt_bias=ssd_dt_bias, ssd_a_log=ssd_a_log, ssd_d=ssd_d, ssd_norm_g=ssd_norm_g,
        ssd_w_out=full["ssd_w_out"], hg_w_in=full["hg_w_in"], hg_norm_g=hgn_full, hg_w_out=full["hg_w_out"],
        at_w_in=full["at_w_in"], at_q_norm_g=at_q_norm_g, at_k_norm_g=at_k_norm_g, at_w_out=full["at_w_out"],
        dl_w_in=full["dl_w_in"], dl_w_out=full["dl_w_out"])

    loss_tile, grad_x, grads = local_step(x[0], loss_target[0], w)
    loss = lax.psum(loss_tile[0, 0], ("x", "y", "c"))

    gp_in = jnp.concatenate([grads[n][0].reshape(D_MODEL, N_CHIPS, cols).transpose(1, 0, 2)
                             for n, cols in zip(IN_NAMES, IN_COLS)], axis=2)
    gp_out = jnp.concatenate([grads[n][0].reshape(N_CHIPS, rows, D_MODEL)
                              for n, rows in zip(OUT_NAMES, OUT_ROWS)], axis=1)
    r_in, r_out = swap_halves(gp_in, gp_out)
    c_idx = ci.astype(jnp.int32).reshape(1)
    me_idx = chip.astype(jnp.int32).reshape(1)
    f_in, b_in = half_add("half_add_in", gp_in, r_in, c_idx, 128)
    f_out, b_out = half_add("half_add_out", gp_out, r_out, c_idx, 256)
    x_in, x_out = scatter_chips(b_in, b_out)
    s_in = chip_sum("chip_sum_in", f_in, x_in, me_idx, 128)
    s_out = chip_sum("chip_sum_out", f_out, x_out, me_idx, 256)
    o_in, o_out = share_halves(s_in, s_out)
    red_in = jnp.where(ci == 0, jnp.concatenate([s_in, o_in], axis=0), jnp.concatenate([o_in, s_in], axis=0))
    red_out = jnp.where(ci == 0, jnp.concatenate([s_out, o_out], axis=0), jnp.concatenate([o_out, s_out], axis=0))
    G = {}
    off = 0
    for n, cols in zip(IN_NAMES, IN_COLS):
        G[n] = red_in[:, off:off + cols][None]
        off += cols
    off = 0
    for n, rows in zip(OUT_NAMES, OUT_ROWS):
        G[n] = red_out[off:off + rows][None]
        off += rows

    small_full = [grads[n].reshape(-1) for n in SMALL_NAMES]
    shapes_full = [grads[n].shape for n in SMALL_NAMES]
    packs = gather_small(_pack_small(small_full))
    (red_small,) = smallcall("sum_small", lambda p: (functools.reduce(lambda a, b: a + b, [p[k] for k in range(N_DEV)]),),
                             [packs], [packs.shape[1:]])
    for n, g in zip(SMALL_NAMES, _unpack_small(red_small, shapes_full)):
        G[n] = g
    G["ssd_conv_w"] = lax.dynamic_slice_in_dim(G["ssd_conv_w"].reshape(1, SSD_CONV, SSD_CONV_CH), chip * conv_shard, conv_shard, axis=2)
    G["hg_norm_g"] = lax.dynamic_slice_in_dim(G["hg_norm_g"].reshape(1, HG_W), chip * hgn_shard, hgn_shard, axis=1)
    for n in SMALL_NAMES:
        G[n] = G[n].reshape(W[n].shape)

    D, NM, NV = {}, {}, {}
    for n in IN_NAMES + OUT_NAMES:
        D[n], NM[n], NV[n] = adamw_big("adamw_" + n, W[n], G[n], M[n], V[n])
    shapes = [W[n].shape for n in SMALL_NAMES]
    pk = [_pack_small([T[n] for n in SMALL_NAMES]) for T in (W, G, M, V)]
    outs = smallcall("adamw_small", lambda w, g, m, v: _adamw(w, g, m, v), pk, [pk[0].shape] * 3)
    for T, pack in zip((D, NM, NV), outs):
        for n, a in zip(SMALL_NAMES, _unpack_small(pack, shapes)):
            T[n] = a
    return (loss, grad_x[None], *[G[n] for n in ALL_NAMES], *[D[n] for n in ALL_NAMES],
            *[NM[n] for n in ALL_NAMES], *[NV[n] for n in ALL_NAMES])
```

```python
import functools
import math

import numpy as np
import jax
import jax.numpy as jnp
from jax import lax
from jax.experimental import pallas as pl
from jax.experimental.pallas import tpu as pltpu

F32 = jnp.float32
BF16 = jnp.bfloat16
MESH = pl.DeviceIdType.MESH

D_MODEL = 1024
DEPTH = 4
GRID_W = 64
EPS = 1e-6
NEG_BIG = -1e30

SSD_DI = 2048
SSD_HEADDIM = 64
SSD_HEADS = 32
SSD_GROUPS = 4
SSD_HPG = 8
SSD_STATE = 128
SSD_CONV = 7
SSD_CHUNK = 128
SSD_GPS = 4
SSD_CONV_CH = SSD_DI + 2 * SSD_GROUPS * SSD_STATE
SSD_MAIN = SSD_DI + SSD_CONV_CH
SSD_IN = SSD_MAIN + 2 * SSD_HEADS

HG_HEADS = 8
HG_EXPAND = 128
HG_W = 1024
HG_CHUNK = 32
HG_ROWS = 128
HG_HPS = 8
HG_IN = 5 * HG_W

AT_HEADS = 16
AT_KV = 8
AT_GRP = 2
AT_HD = 128
ROPE_THETA = 10000.0
ROPE_AXIS = 64
AT_QW = AT_HEADS * AT_HD
AT_KW = AT_KV * AT_HD
AT_IN = 2 * AT_QW + 2 * AT_KW

DL_PAIRS = ((128, 1), (512, 4), (2048, 16))
DL_HEADS = 16
DL_HD = 64
DL_W = 1024
DL_STEPS = 64
DL_IN = 10 * DL_W
REL_BUCKETS = 32
REL_MAX_DIST = 1024

ADAM_LR = 0.001
ADAM_B1 = 0.9
ADAM_B2 = 0.999
ADAM_EPS = 1e-08
ADAM_WD = 0.01
ADAM_STEP = 10

VMEM_LIMIT = 56 * 1024 * 1024
LANE = 128
SUBLANE = 8


def _cp(sem=None):
    return pltpu.CompilerParams(dimension_semantics=sem, vmem_limit_bytes=VMEM_LIMIT)


_NN, _NT, _TN = ((1,), (0,)), ((1,), (1,)), ((0,), (0,))


def _dot(a, b, dims):
    return lax.dot_general(a.astype(BF16), b.astype(BF16), (dims, ((), ())), preferred_element_type=F32)


def _dot_rule(dims, da_rule, db_rule):
    @jax.custom_vjp
    def f(a, b):
        return _dot(a, b, dims)

    def fwd(a, b):
        return _dot(a, b, dims), (a, b)

    def bwd(res, g):
        a, b = res
        return da_rule(a, b, g).astype(a.dtype), db_rule(a, b, g).astype(b.dtype)

    f.defvjp(fwd, bwd)
    return f


_mm = _dot_rule(_NN, lambda a, b, g: _dot(g, b, _NT), lambda a, b, g: _dot(a, g, _TN))
_mm_nt = _dot_rule(_NT, lambda a, b, g: _dot(g, b, _NN), lambda a, b, g: _dot(g, a, _TN))
_mm_tn = _dot_rule(_TN, lambda a, b, g: _dot(b, g, _NT), lambda a, b, g: _dot(a, g, _NN))


def _mm_exact(a, b):
    return jnp.dot(a, b, preferred_element_type=F32, precision=lax.Precision.HIGHEST)


def _dot3(t, a, dims):
    hi = a.astype(BF16)
    r1 = a - hi.astype(F32)
    mid = r1.astype(BF16)
    lo = r1 - mid.astype(F32)
    return _dot(t, hi, dims) + (_dot(t, mid, dims) + _dot(t, lo, dims))


@jax.custom_vjp
def _mm_tri(t, a):
    return _dot3(t, a, _NN)


def _mm_tri_fwd(t, a):
    return _dot3(t, a, _NN), t


def _mm_tri_bwd(t, g):
    return None, _dot3(t, g, _TN)


_mm_tri.defvjp(_mm_tri_fwd, _mm_tri_bwd)


def _mm_nt_exact(a, b):
    return lax.dot_general(a, b, (((1,), (1,)), ((), ())), preferred_element_type=F32,
                           precision=lax.Precision.HIGHEST)


def _silu(x):
    return x * jax.nn.sigmoid(x)


def _softplus(z):
    return jnp.maximum(z, 0.0) + jnp.log(1.0 + jnp.exp(-jnp.abs(z)))


def _pick(dim, pref):
    best = None
    t = LANE
    while t <= min(dim, pref):
        if dim % t == 0:
            best = t
        t += LANE
    return best if best is not None else dim


def _const_map(n):
    return lambda *_: (0,) * n


MM_BLOCK_BYTES = 8 * 1024 * 1024


def _mm_tiles(mode, M, N, K, a_bytes, b_bytes):
    if mode == "nn":
        tk = K if K <= 2048 else _pick(K, 1024)
        tm = _pick(M, max(512, MM_BLOCK_BYTES // (tk * a_bytes)))
        tn = _pick(N, 512)
    elif mode == "tn":
        tk = K if K <= 4096 else _pick(K, 1024)
        tm = _pick(M, MM_BLOCK_BYTES // (tk * a_bytes))
        tn = _pick(N, MM_BLOCK_BYTES // (tk * b_bytes))
    else:
        tk = _pick(K, 1024)
        tn = _pick(N, 1024)
        tm = _pick(M, MM_BLOCK_BYTES // (8 * tn))
    return tm, tn, tk


def matmul(name, a, b, mode="nn", res=None, out_dtype=F32):
    if mode == "tn":
        K, M = a.shape
    else:
        M, K = a.shape
    N = b.shape[0] if mode == "nt" else b.shape[1]
    tm, tn, tk = _mm_tiles(mode, M, N, K, a.dtype.itemsize, b.dtype.itemsize)
    nk = K // tk
    a_spec = (pl.BlockSpec((tk, tm), lambda i, j, k: (k, i)) if mode == "tn"
              else pl.BlockSpec((tm, tk), lambda i, j, k: (i, k)))
    b_spec = (pl.BlockSpec((tn, tk), lambda i, j, k: (j, k)) if mode == "nt"
              else pl.BlockSpec((tk, tn), lambda i, j, k: (k, j)))
    dot = {"nn": _mm, "nt": _mm_nt, "tn": _mm_tn}[mode]
    has_res = res is not None

    def body(*refs):
        a_ref, b_ref = refs[0], refs[1]
        r_ref = refs[2] if has_res else None
        o_ref = refs[3] if has_res else refs[2]

        def finish(out):
            if has_res:
                out = out + r_ref[...].astype(F32)
            o_ref[...] = out.astype(o_ref.dtype)

        if nk == 1:
            finish(dot(a_ref[...], b_ref[...]))
            return
        acc = refs[-1]
        k = pl.program_id(2)

        @pl.when(k == 0)
        def _():
            acc[...] = jnp.zeros_like(acc)

        acc[...] += dot(a_ref[...], b_ref[...])

        @pl.when(k == nk - 1)
        def _():
            finish(acc[...])

    in_specs = [a_spec, b_spec]
    args = [a, b]
    if has_res:
        in_specs.append(pl.BlockSpec((tm, tn), lambda i, j, k: (i, j)))
        args.append(res)
    return pl.pallas_call(
        body, name=name, grid=(M // tm, N // tn, nk), in_specs=in_specs,
        out_specs=pl.BlockSpec((tm, tn), lambda i, j, k: (i, j)),
        out_shape=jax.ShapeDtypeStruct((M, N), out_dtype),
        scratch_shapes=[pltpu.VMEM((tm, tn), F32)] if nk > 1 else [],
        compiler_params=_cp(("parallel", "parallel", "arbitrary")),
    )(*args)


def _col(arr, width, idx):
    return (arr, width, idx)


def _perm(arr, dil, width=None, idx=0):
    return (arr, arr.shape[1] if width is None else width, idx, dil)


def _from_perm(ref, scr, dil):
    n, w = ref.shape[1], ref.shape[2]
    for r in range(dil):
        for j in range(w // LANE):
            scr[j, pl.ds(r, n, stride=dil), :] = ref[r, :, j * LANE:(j + 1) * LANE].astype(F32)
    return jnp.concatenate([scr[j] for j in range(w // LANE)], axis=1)


def _to_perm(val, ref, scr, dil):
    n, w = ref.shape[1], ref.shape[2]
    for j in range(w // LANE):
        scr[j] = val[:, j * LANE:(j + 1) * LANE].astype(F32)
    for r in range(dil):
        ref[r] = jnp.concatenate([scr[j, pl.ds(r, n, stride=dil), :] for j in range(w // LANE)], axis=1).astype(ref.dtype)


def rowcall(name, fn, rows, bcs, row_outs, bc_outs=(), tb=256, halo=()):
    rows = [r if isinstance(r, tuple) else (r, r.shape[1], 0) for r in rows]
    rows = [r if len(r) == 4 else r + (1,) for r in rows]
    row_outs = [o if len(o) == 3 else o + (1,) for o in row_outs]
    S = rows[0][0].shape[0]
    tb = min(tb, S)
    nb = S // tb
    n_r, n_h, n_b, n_ro, n_bo = len(rows), len(halo), len(bcs), len(row_outs), len(bc_outs)
    hb = tb // SUBLANE
    last = S // SUBLANE - 1
    perm_w = max([w for (_, w, _, d) in rows if d > 1] + [w for (w, _, d) in row_outs if d > 1] + [0])

    def body(*refs):
        i = pl.program_id(0)
        scr = refs[-1] if perm_w else None
        pos = 0
        r_in = [r[...] if d == 1 else _from_perm(r, scr, d) for r, (_, _, _, d) in zip(refs[pos:pos + n_r], rows)]
        pos += n_r
        h_in = []
        for _ in range(n_h):
            prev = refs[pos][...] * (i > 0).astype(F32)
            nxt = refs[pos + 1][...] * (i < nb - 1).astype(F32)
            h_in += [prev, nxt]
            pos += 2
        b_in = [r[...] for r in refs[pos:pos + n_b]]
        pos += n_b
        ro = refs[pos:pos + n_ro]
        bo = refs[pos + n_ro:pos + n_ro + n_bo]
        outs_r, outs_b = fn(*r_in, *h_in, *b_in)
        for ref, val, (_, _, d) in zip(ro, outs_r, row_outs, strict=True):
            if d == 1:
                ref[...] = val.astype(ref.dtype)
            else:
                _to_perm(val, ref, scr, d)
        if n_bo:
            @pl.when(i == 0)
            def _():
                for ref in bo:
                    ref[...] = jnp.zeros_like(ref)

            for ref, val in zip(bo, outs_b, strict=True):
                ref[...] += val

    in_specs, args = [], []
    for (a, w, c, d) in rows:
        if d == 1:
            in_specs.append(pl.BlockSpec((tb, w), functools.partial(lambda i, c: (i, c), c=c)))
            args.append(a)
        else:
            in_specs.append(pl.BlockSpec((d, tb // d, w), functools.partial(lambda i, c: (0, i, c), c=c)))
            args.append(a.reshape(d, S // d, a.shape[1]))
    for h in halo:
        a, w, c, _ = rows[h]
        in_specs.append(pl.BlockSpec((SUBLANE, w), functools.partial(
            lambda i, c: (jnp.maximum(i * hb - 1, 0), c), c=c)))
        in_specs.append(pl.BlockSpec((SUBLANE, w), functools.partial(
            lambda i, c: (jnp.minimum((i + 1) * hb, last), c), c=c)))
        args += [a, a]
    for b in bcs:
        in_specs.append(pl.BlockSpec(b.shape, _const_map(b.ndim)))
        args.append(b)
    out_specs, out_shape = [], []
    for (w, dt, d) in row_outs:
        if d == 1:
            out_specs.append(pl.BlockSpec((tb, w), lambda i: (i, 0)))
            out_shape.append(jax.ShapeDtypeStruct((S, w), dt))
        else:
            out_specs.append(pl.BlockSpec((d, tb // d, w), lambda i: (0, i, 0)))
            out_shape.append(jax.ShapeDtypeStruct((d, S // d, w), dt))
    for shp in bc_outs:
        out_specs.append(pl.BlockSpec(shp, _const_map(len(shp))))
        out_shape.append(jax.ShapeDtypeStruct(shp, F32))
    outs = pl.pallas_call(
        body, name=name, grid=(nb,), in_specs=in_specs, out_specs=out_specs, out_shape=out_shape,
        scratch_shapes=[pltpu.VMEM((perm_w // LANE, tb, LANE), F32)] if perm_w else [],
        compiler_params=_cp(("arbitrary",) if n_bo else ("parallel",)),
    )(*args)
    row_res = [o if d == 1 else o.reshape(S, w) for o, (w, _, d) in zip(outs[:n_ro], row_outs)]
    return row_res, list(outs[n_ro:])


def smallcall(name, fn, ins, out_shapes):
    n_in = len(ins)

    def body(*refs):
        outs = fn(*[r[...] for r in refs[:n_in]])
        for ref, val in zip(refs[n_in:], outs, strict=True):
            ref[...] = val.astype(ref.dtype)

    return pl.pallas_call(
        body, name=name, out_shape=[jax.ShapeDtypeStruct(s, F32) for s in out_shapes],
        compiler_params=_cp(),
    )(*ins)


def _rms(x, g):
    return x * lax.rsqrt(jnp.mean(x * x, axis=-1, keepdims=True) + EPS) * g


def _rms_groups(y, g, width):
    outs = []
    for j in range(y.shape[1] // width):
        sl = slice(j * width, (j + 1) * width)
        outs.append(_rms(y[:, sl], g[:, sl]))
    return jnp.concatenate(outs, axis=1)


def norm_fwd(name, x, g, dils=(1,)):
    outs, _ = rowcall(name, lambda x, g: ((_rms(x, g),) * len(dils), ()), [x], [g],
                      [(D_MODEL, BF16, d) for d in dils], tb=512)
    return outs[0] if len(dils) == 1 else tuple(outs)


def norm_bwd(name, x, g, dhn, dres, dils=(1,)):
    parts = dhn if isinstance(dhn, tuple) else (dhn,)
    n = len(parts)

    def fn(x, *rest):
        dh = functools.reduce(lambda a, b: a + b, rest[:n])
        _, vjp = jax.vjp(_rms, x, rest[n + 1])
        dx, dg = vjp(dh)
        return (dx + rest[n],), (dg,)

    rows = [x] + [a if d == 1 else _perm(a, d) for a, d in zip(parts, dils)] + [dres]
    (dx,), (dg,) = rowcall(name, fn, rows, [g], [(D_MODEL, F32)], [(1, D_MODEL)], tb=512)
    return dx, dg


def loss_head(x, tgt, g):
    def fn(x, tgt, g):
        y, vjp = jax.vjp(_rms, x, g)
        diff = y - tgt
        loss = 0.5 * jnp.sum(jnp.mean(diff * diff, axis=-1, keepdims=True), axis=0, keepdims=True)
        dx, dg = vjp(diff * (1.0 / D_MODEL))
        return (dx,), (jnp.broadcast_to(loss, (1, LANE)), dg)

    (dx,), (loss, dg) = rowcall("loss_head", fn, [x, tgt], [g], [(D_MODEL, F32)],
                                [(1, LANE), (1, D_MODEL)], tb=512)
    return loss, dx, dg


def _shift_rows(x, s):
    if s == 0:
        return x
    return pltpu.roll(x, (-s) % x.shape[0], 0)


def _conv_ext(x, prev, nxt, w):
    xe = jnp.concatenate([prev, x, nxt], axis=0)
    pad = SSD_CONV // 2
    c = jnp.zeros_like(xe)
    for k in range(SSD_CONV):
        c = c + w[k:k + 1, :] * _shift_rows(xe, k - pad)
    return xe, c


def ssd_conv_fwd(u, conv_w, conv_b):
    def fn(x0, x1, x2, p0, n0, p1, n1, p2, n2, w, b):
        tb = x0.shape[0]
        outs = []
        for j, (x, p, n) in enumerate(((x0, p0, n0), (x1, p1, n1), (x2, p2, n2))):
            sl = slice(j * 1024, (j + 1) * 1024)
            _, c = _conv_ext(x, p, n, w[:, sl])
            outs.append(_silu(c[SUBLANE:SUBLANE + tb] + b[:, sl]))
        return (jnp.concatenate(outs, axis=1),), ()

    (xbc,), _ = rowcall("ssd_conv_fwd", fn, [_col(u, 1024, 2), _col(u, 1024, 3), _col(u, 1024, 4)],
                        [conv_w, conv_b], [(SSD_CONV_CH, F32)], tb=256, halo=(0, 1, 2))
    return xbc


def ssd_conv_bwd(u, dxbc, dz, conv_w, conv_b):
    pad = SSD_CONV // 2

    def fn(x0, x1, x2, g0, g1, g2, dz, xp0, xn0, xp1, xn1, xp2, xn2, gp0, gn0, gp1, gn1, gp2, gn2, w, b):
        tb = x0.shape[0]
        blk = slice(SUBLANE, SUBLANE + tb)
        dpre, dws, dbs = [], [], []
        xs = ((x0, xp0, xn0), (x1, xp1, xn1), (x2, xp2, xn2))
        gs = ((g0, gp0, gn0), (g1, gp1, gn1), (g2, gp2, gn2))
        for j in range(3):
            sl = slice(j * 1024, (j + 1) * 1024)
            wj = w[:, sl]
            xe, c = _conv_ext(*xs[j], wj)
            ce = c + b[:, sl]
            sig = jax.nn.sigmoid(ce)
            ge = jnp.concatenate([gs[j][1], gs[j][0], gs[j][2]], axis=0)
            dce = ge * (sig * (1.0 + ce * (1.0 - sig)))
            dx = jnp.zeros_like(xe)
            dw_rows = []
            for k in range(SSD_CONV):
                dx = dx + wj[k:k + 1, :] * _shift_rows(dce, pad - k)
                dw_rows.append(jnp.sum(dce[blk] * _shift_rows(xe, k - pad)[blk], axis=0, keepdims=True))
            dw_rows.append(jnp.zeros_like(dw_rows[0]))
            dpre.append(dx[blk])
            dws.append(jnp.concatenate(dw_rows, axis=0))
            dbs.append(jnp.sum(dce[blk], axis=0, keepdims=True))
        du = jnp.concatenate([dz] + dpre, axis=1)
        return (du,), (jnp.concatenate(dws, axis=1), jnp.concatenate(dbs, axis=1))

    rows = [_col(u, 1024, 2), _col(u, 1024, 3), _col(u, 1024, 4),
            _col(dxbc, 1024, 0), _col(dxbc, 1024, 1), _col(dxbc, 1024, 2), dz]
    (du,), (dw, db) = rowcall("ssd_conv_bwd", fn, rows, [conv_w, conv_b], [(SSD_MAIN, BF16)],
                              [(SUBLANE, SSD_CONV_CH), (1, SSD_CONV_CH)], tb=128, halo=(0, 1, 2, 3, 4, 5))
    return du, dw, db


def _expand_heads(v):
    return jnp.concatenate([jnp.broadcast_to(v[:, j:j + 1], (v.shape[0], SSD_HEADDIM)) for j in range(SSD_HPG)], axis=1)


def _ssd_chunk(rev, st_in, xs, udt, dtb, alog, B, C):
    Q = B.shape[0]
    P = SSD_HEADDIM
    dt = _softplus(udt + dtb)
    a = dt * (-jnp.exp(alog))
    r = lax.broadcasted_iota(jnp.int32, (Q, Q), 0)
    c = lax.broadcasted_iota(jnp.int32, (Q, Q), 1)
    mask = (r <= c) if rev else (r >= c)
    p = _mm_tri(mask, a)
    pT = p.T
    p_e = _expand_heads(p)
    tot_e = p_e[0:1] if rev else p_e[Q - 1:Q]
    xdt = xs * _expand_heads(dt)
    CB = _mm_nt(C, B)
    ys = []
    for j in range(SSD_HPG):
        L = jnp.exp(jnp.where(mask, p[:, j:j + 1] - pT[j:j + 1, :], NEG_BIG))
        ys.append(_mm(CB * L, xdt[:, j * P:(j + 1) * P]))
    y = jnp.concatenate(ys, axis=1) + _mm(C, st_in) * jnp.exp(p_e)
    st_out = st_in * jnp.exp(tot_e) + _mm_tn(B, xdt * jnp.exp(tot_e - p_e))
    return y, st_out


def _ssd_specs(nc, rev_order):
    Q = SSD_CHUNK
    N, P, H, GS = SSD_STATE, SSD_HEADDIM, SSD_HPG, SSD_GPS
    gw = H * P
    nbc = SSD_GROUPS // GS

    def cidx(s):
        return nc - 1 - s if rev_order else s

    xs = pl.BlockSpec((Q, GS * gw), lambda g, s: (cidx(s), g))
    Bs = pl.BlockSpec((Q, GS * N), lambda g, s: (cidx(s), SSD_DI // (GS * N) + g))
    Cs = pl.BlockSpec((Q, GS * N), lambda g, s: (cidx(s), SSD_DI // (GS * N) + nbc + g))
    BC_out = pl.BlockSpec((Q, GS * N), lambda g, s: (cidx(s), g))
    udt = pl.BlockSpec((GS, Q, H), lambda g, s: (g, cidx(s), 0))
    small = pl.BlockSpec((GS, 1, H), lambda g, s: (g, 0, 0))
    st = pl.BlockSpec((GS, 1, N, gw), lambda g, s: (g, cidx(s), 0, 0))
    return xs, Bs, Cs, BC_out, udt, small, st


def ssd_scan_fwd(name, xbc, udt, dtb, alog, rev):
    S = xbc.shape[0]
    Q, N, P, H, GS = SSD_CHUNK, SSD_STATE, SSD_HEADDIM, SSD_HPG, SSD_GPS
    gw = H * P
    nc = S // Q
    xs_s, B_s, C_s, _, udt_s, small_s, st_s = _ssd_specs(nc, rev)

    def body(xs_ref, B_ref, C_ref, udt_ref, dtb_ref, alog_ref, y_ref, st_ref, state):
        @pl.when(pl.program_id(1) == 0)
        def _():
            state[...] = jnp.zeros_like(state)

        for g in range(GS):
            st_ref[g, 0] = state[g]
            y, st_out = _ssd_chunk(rev, state[g], xs_ref[:, g * gw:(g + 1) * gw], udt_ref[g], dtb_ref[g], alog_ref[g],
                                   B_ref[:, g * N:(g + 1) * N], C_ref[:, g * N:(g + 1) * N])
            y_ref[:, g * gw:(g + 1) * gw] = y
            state[g] = st_out

    return pl.pallas_call(
        body, name=name, grid=(SSD_GROUPS // GS, nc),
        in_specs=[xs_s, B_s, C_s, udt_s, small_s, small_s],
        out_specs=[xs_s, st_s],
        out_shape=[jax.ShapeDtypeStruct((S, SSD_DI), F32),
                   jax.ShapeDtypeStruct((SSD_GROUPS, nc, N, gw), F32)],
        scratch_shapes=[pltpu.VMEM((GS, N, gw), F32)],
        compiler_params=_cp(("parallel", "arbitrary")),
    )(xbc, xbc, xbc, udt, dtb, alog)


def ssd_scan_bwd(name, xbc, udt, dtb, alog, states, dy, rev):
    S = xbc.shape[0]
    Q, N, P, H, GS = SSD_CHUNK, SSD_STATE, SSD_HEADDIM, SSD_HPG, SSD_GPS
    gw = H * P
    nc = S // Q
    xs_s, B_s, C_s, BC_out, udt_s, small_s, st_s = _ssd_specs(nc, not rev)

    def body(xs_ref, B_ref, C_ref, udt_ref, dtb_ref, alog_ref, st_ref, dy_ref,
             dx_ref, dB_ref, dC_ref, dudt_ref, ddtb_ref, dalog_ref, dstate):
        @pl.when(pl.program_id(1) == 0)
        def _():
            dstate[...] = jnp.zeros_like(dstate)
            ddtb_ref[...] = jnp.zeros_like(ddtb_ref)
            dalog_ref[...] = jnp.zeros_like(dalog_ref)

        for g in range(GS):
            cols, bc = slice(g * gw, (g + 1) * gw), slice(g * N, (g + 1) * N)
            _, vjp = jax.vjp(functools.partial(_ssd_chunk, rev), st_ref[g, 0], xs_ref[:, cols], udt_ref[g], dtb_ref[g],
                             alog_ref[g], B_ref[:, bc], C_ref[:, bc])
            dst_in, dxs, dudt, ddtb, dalog, dB, dC = vjp((dy_ref[:, cols], dstate[g]))
            dx_ref[:, cols] = dxs
            dB_ref[:, bc] = dB
            dC_ref[:, bc] = dC
            dudt_ref[g] = dudt
            ddtb_ref[g] += ddtb
            dalog_ref[g] += dalog
            dstate[g] = dst_in

    return pl.pallas_call(
        body, name=name, grid=(SSD_GROUPS // GS, nc),
        in_specs=[xs_s, B_s, C_s, udt_s, small_s, small_s, st_s, xs_s],
        out_specs=[xs_s, BC_out, BC_out, udt_s, small_s, small_s],
        out_shape=[jax.ShapeDtypeStruct((S, SSD_DI), F32),
                   jax.ShapeDtypeStruct((S, SSD_GROUPS * N), F32),
                   jax.ShapeDtypeStruct((S, SSD_GROUPS * N), F32),
                   jax.ShapeDtypeStruct((SSD_GROUPS, S, H), F32),
                   jax.ShapeDtypeStruct((SSD_GROUPS, 1, H), F32),
                   jax.ShapeDtypeStruct((SSD_GROUPS, 1, H), F32)],
        scratch_shapes=[pltpu.VMEM((GS, N, gw), F32)],
        compiler_params=_cp(("parallel", "arbitrary")),
    )(xbc, xbc, xbc, udt, dtb, alog, states, dy)


def _ssd_combine(yf, yb, xs, z, dexp, ng):
    y = (yf + yb + xs * dexp) * _silu(z)
    return _rms_groups(y, ng, SSD_DI // SSD_GROUPS)


def ssd_forward(x, hn, w):
    S = x.shape[0]
    u = matmul("ssd_in", hn, w["ssd_w_main"])
    udt = matmul("ssd_in_dt", hn, w["ssd_w_dt"])
    xbc = ssd_conv_fwd(u, w["ssd_conv_w8"], w["ssd_conv_b"])
    udt_t = udt.reshape(S, 2, SSD_GROUPS, SSD_HPG).transpose(1, 2, 0, 3)
    dtb = w["ssd_dt_bias"].reshape(2, SSD_GROUPS, 1, SSD_HPG)
    alog = w["ssd_a_log"].reshape(2, SSD_GROUPS, 1, SSD_HPG)
    yf, stf = ssd_scan_fwd("ssd_scan_f", xbc, udt_t[0], dtb[0], alog[0], False)
    yb, stb = ssd_scan_fwd("ssd_scan_b", xbc, udt_t[1], dtb[1], alog[1], True)
    dexp = jnp.repeat(w["ssd_d"].reshape(1, SSD_HEADS), SSD_HEADDIM, axis=1)
    (yn,), _ = rowcall("ssd_combine", lambda yf, yb, xs, z, d, g: ((_ssd_combine(yf, yb, xs, z, d, g),), ()),
                       [yf, yb, _col(xbc, SSD_DI, 0), _col(u, SSD_DI, 0)], [dexp, w["ssd_norm_g"]],
                       [(SSD_DI, BF16)], tb=256)
    out = matmul("ssd_out", yn, w["ssd_w_out"], res=x)
    saved = dict(hn=hn, u=u, xbc=xbc, udt_t=udt_t, dtb=dtb, alog=alog, yf=yf, yb=yb, stf=stf, stb=stb,
                 dexp=dexp, yn=yn)
    return out, saved


def ssd_backward(dy, sv, w):
    S = dy.shape[0]
    u, xbc = sv["u"], sv["xbc"]
    dyn = matmul("ssd_out_dx", dy, w["ssd_w_out"], mode="nt")
    g_w_out = matmul("ssd_out_dw", sv["yn"], dy, mode="tn")

    def comb_bwd(yf, yb, xs, z, dyn, dexp, ng):
        _, vjp = jax.vjp(_ssd_combine, yf, yb, xs, z, dexp, ng)
        dyf, _, dxs, dz, ddexp, dng = vjp(dyn)
        return (dyf, dxs, dz), (ddexp, dng)

    (dyc, dskip, dz), (ddexp, g_norm) = rowcall(
        "ssd_combine_bwd", comb_bwd, [sv["yf"], sv["yb"], _col(xbc, SSD_DI, 0), _col(u, SSD_DI, 0), dyn],
        [sv["dexp"], w["ssd_norm_g"]], [(SSD_DI, F32)] * 3, [(1, SSD_DI), (1, SSD_DI)], tb=256)
    udt_t, dtb, alog = sv["udt_t"], sv["dtb"], sv["alog"]
    dxf, dBf, dCf, dudt_f, ddtb_f, dalog_f = ssd_scan_bwd("ssd_scan_f_bwd", xbc, udt_t[0], dtb[0], alog[0],
                                                          sv["stf"], dyc, False)
    dxb, dBb, dCb, dudt_b, ddtb_b, dalog_b = ssd_scan_bwd("ssd_scan_b_bwd", xbc, udt_t[1], dtb[1], alog[1],
                                                          sv["stb"], dyc, True)

    def gather(dxf, dxb, dskip, dBf, dBb, dCf, dCb):
        return (jnp.concatenate([dxf + dxb + dskip, dBf + dBb, dCf + dCb], axis=1),), ()

    (dxbc,), _ = rowcall("ssd_dxbc", gather, [dxf, dxb, dskip, dBf, dBb, dCf, dCb], [], [(SSD_CONV_CH, F32)], tb=256)
    du, g_conv_w8, g_conv_b = ssd_conv_bwd(u, dxbc, dz, w["ssd_conv_w8"], w["ssd_conv_b"])
    dudt = jnp.stack([dudt_f, dudt_b]).transpose(2, 0, 1, 3).reshape(S, 2 * SSD_HEADS)
    hn = sv["hn"]
    g_main = matmul("ssd_in_dw", hn, du, mode="tn")
    g_dt = matmul("ssd_in_dt_dw", hn, dudt, mode="tn")
    dhn = matmul("ssd_in_dt_dx", dudt, w["ssd_w_dt"], mode="nt")
    dhn = matmul("ssd_in_dx", du, w["ssd_w_main"], mode="nt", res=dhn)
    grads = dict(
        ssd_w_in=jnp.concatenate([g_main, g_dt], axis=1)[None],
        ssd_conv_w=g_conv_w8[None, :SSD_CONV],
        ssd_conv_b=g_conv_b,
        ssd_dt_bias=jnp.stack([ddtb_f, ddtb_b]).reshape(1, 2, SSD_HEADS),
        ssd_a_log=jnp.stack([dalog_f, dalog_b]).reshape(1, 2, SSD_HEADS),
        ssd_d_exp=ddexp,
        ssd_norm_g=g_norm,
        ssd_w_out=g_w_out[None],
    )
    return dhn, grads


def _hg_block(rev, stTs, uq, uf, ui, lb):
    C = HG_CHUNK
    n = uq.shape[0] // C
    nh = uq.shape[1] // HG_EXPAND
    stTs = list(stTs)
    q = _silu(uq)
    f = lb + (1.0 - lb) * jax.nn.sigmoid(uf)
    k = 1.0 - f
    g = jnp.log(f)
    r = lax.broadcasted_iota(jnp.int32, (C, C), 0)
    c = lax.broadcasted_iota(jnp.int32, (C, C), 1)
    mask = (r <= c) if rev else (r >= c)
    Tm = mask.astype(F32)
    outs = [[None] * n for _ in range(nh)]
    for i in (reversed(range(n)) if rev else range(n)):
        sl = slice(i * C, (i + 1) * C)
        qi, ki, vi = q[sl], k[sl], ui[sl]
        G = _mm_tri(mask, g[sl])
        Gr = G[C // 2:C // 2 + 1]
        Gl = G[0:1] if rev else G[C - 1:C]
        q_in, k_in = qi * jnp.exp(G - Gr), ki * jnp.exp(Gr - G)
        q_st, k_st, e_l = qi * jnp.exp(G), ki * jnp.exp(Gl - G), jnp.exp(Gl)
        for h in range(nh):
            cs = slice(h * HG_EXPAND, (h + 1) * HG_EXPAND)
            att = jnp.where(mask, _mm_nt(q_in[:, cs], k_in[:, cs]), 0.0)
            outs[h][i] = _mm(att, vi[:, cs]) + _mm_nt(q_st[:, cs], stTs[h])
            stTs[h] = stTs[h] * e_l[:, cs] + _mm_tn(vi[:, cs], k_st[:, cs])
    o = jnp.concatenate([jnp.concatenate(outs[h], axis=0) for h in range(nh)], axis=1)
    return o, stTs


def _hg_specs(nb, rev_order, f_col):
    R = HG_ROWS
    gw = HG_HPS * HG_EXPAND
    ng = HG_HEADS // HG_HPS

    def bidx(s):
        return nb - 1 - s if rev_order else s

    def col(base):
        return pl.BlockSpec((R, gw), lambda h, s: (bidx(s), base * ng + h))

    out = pl.BlockSpec((R, gw), lambda h, s: (bidx(s), h))
    lb = pl.BlockSpec((1, gw), lambda h, s: (0, h))
    st = pl.BlockSpec((1, 1, HG_HPS, HG_EXPAND, HG_EXPAND), lambda h, s: (h, bidx(s), 0, 0, 0))
    return col(0), col(f_col), col(3), out, lb, st


def hg_scan_fwd(name, u, lb, rev):
    S = u.shape[0]
    nb = S // HG_ROWS
    ng = HG_HEADS // HG_HPS
    q_s, f_s, i_s, o_s, lb_s, st_s = _hg_specs(nb, rev, 2 if rev else 1)

    def body(uq, uf, ui, lb_ref, o_ref, st_ref, state):
        @pl.when(pl.program_id(1) == 0)
        def _():
            state[...] = jnp.zeros_like(state)

        st_ref[0, 0] = state[...]
        o, st = _hg_block(rev, [state[h] for h in range(HG_HPS)], uq[...], uf[...], ui[...], lb_ref[...])
        o_ref[...] = o
        for h in range(HG_HPS):
            state[h] = st[h]

    return pl.pallas_call(
        body, name=name, grid=(ng, nb), in_specs=[q_s, f_s, i_s, lb_s], out_specs=[o_s, st_s],
        out_shape=[jax.ShapeDtypeStruct((S, HG_W), F32),
                   jax.ShapeDtypeStruct((ng, nb, HG_HPS, HG_EXPAND, HG_EXPAND), F32)],
        scratch_shapes=[pltpu.VMEM((HG_HPS, HG_EXPAND, HG_EXPAND), F32)],
        compiler_params=_cp(("parallel", "arbitrary")),
    )(u, u, u, lb)


def hg_scan_bwd(name, u, lb, states, do, rev):
    S = u.shape[0]
    nb = S // HG_ROWS
    ng = HG_HEADS // HG_HPS
    q_s, f_s, i_s, o_s, lb_s, st_s = _hg_specs(nb, not rev, 2 if rev else 1)

    def body(uq, uf, ui, lb_ref, st_ref, do_ref, dq_ref, df_ref, di_ref, dlb_ref, dstate):
        @pl.when(pl.program_id(1) == 0)
        def _():
            dstate[...] = jnp.zeros_like(dstate)
            dlb_ref[...] = jnp.zeros_like(dlb_ref)

        _, vjp = jax.vjp(functools.partial(_hg_block, rev), [st_ref[0, 0, h] for h in range(HG_HPS)],
                         uq[...], uf[...], ui[...], lb_ref[...])
        dst, dq, df, di, dlb = vjp((do_ref[...], [dstate[h] for h in range(HG_HPS)]))
        dq_ref[...] = dq
        df_ref[...] = df
        di_ref[...] = di
        dlb_ref[...] += dlb
        for h in range(HG_HPS):
            dstate[h] = dst[h]

    return pl.pallas_call(
        body, name=name, grid=(ng, nb), in_specs=[q_s, f_s, i_s, lb_s, st_s, o_s],
        out_specs=[o_s, o_s, o_s, lb_s],
        out_shape=[jax.ShapeDtypeStruct((S, HG_W), F32)] * 3 + [jax.ShapeDtypeStruct((1, HG_W), F32)],
        scratch_shapes=[pltpu.VMEM((HG_HPS, HG_EXPAND, HG_EXPAND), F32)],
        compiler_params=_cp(("parallel", "arbitrary")),
    )(u, u, u, lb, states, do)


def _hg_lb(hgrn_lb, layer):
    m = jnp.max(hgrn_lb, axis=0, keepdims=True)
    e = jnp.exp(hgrn_lb - m)
    sm = e / jnp.sum(e, axis=0, keepdims=True)
    lb = jnp.zeros_like(sm[0:1])
    for i in range(1, layer + 1):
        lb = lb + sm[i:i + 1]
    return lb


def _hg_combine(of, ob, gate, ng):
    return _rms_groups(of + ob, ng, HG_EXPAND) * _silu(gate)


def hg_forward(x, hn, w, layer):
    u = matmul("hg_in", hn, w["hg_w_in"])
    (lb,) = smallcall("hg_lb", lambda t: (_hg_lb(t, layer),), [w["hgrn_lb"]], [(1, HG_W)])
    of, stf = hg_scan_fwd("hg_scan_f", u, lb, False)
    ob, stb = hg_scan_fwd("hg_scan_b", u, lb, True)
    (og,), _ = rowcall("hg_combine", lambda of, ob, gate, ng: ((_hg_combine(of, ob, gate, ng),), ()),
                       [of, ob, _col(u, HG_W, 4)], [w["hg_norm_g"]], [(HG_W, BF16)], tb=256)
    out = matmul("hg_out", og, w["hg_w_out"], res=x)
    return out, dict(hn=hn, u=u, lb=lb, of=of, ob=ob, stf=stf, stb=stb, og=og)


def hg_backward(dy, sv, w, layer):
    u, lb = sv["u"], sv["lb"]
    dog = matmul("hg_out_dx", dy, w["hg_w_out"], mode="nt")
    g_w_out = matmul("hg_out_dw", sv["og"], dy, mode="tn")

    def comb_bwd(of, ob, gate, dog, ng):
        _, vjp = jax.vjp(_hg_combine, of, ob, gate, ng)
        dof, _, dgate, dng = vjp(dog)
        return (dof, dgate), (dng,)

    (do, dgate), (g_norm,) = rowcall("hg_combine_bwd", comb_bwd, [sv["of"], sv["ob"], _col(u, HG_W, 4), dog],
                                     [w["hg_norm_g"]], [(HG_W, F32)] * 2, [(1, HG_W)], tb=256)
    dqf, dff, dif, dlbf = hg_scan_bwd("hg_scan_f_bwd", u, lb, sv["stf"], do, False)
    dqb, dfb, dib, dlbb = hg_scan_bwd("hg_scan_b_bwd", u, lb, sv["stb"], do, True)

    def gather(dqf, dqb, dff, dfb, dif, dib, dgate):
        return (jnp.concatenate([dqf + dqb, dff, dfb, dif + dib, dgate], axis=1),), ()

    (du,), _ = rowcall("hg_du", gather, [dqf, dqb, dff, dfb, dif, dib, dgate], [], [(HG_IN, BF16)], tb=256)

    def lb_bwd(t, dlbf, dlbb):
        _, vjp = jax.vjp(lambda t: _hg_lb(t, layer), t)
        return vjp(dlbf + dlbb)

    (g_lb,) = smallcall("hg_lb_bwd", lb_bwd, [w["hgrn_lb"], dlbf, dlbb], [(DEPTH, HG_W)])
    hn = sv["hn"]
    g_w_in = matmul("hg_in_dw", hn, du, mode="tn")
    dhn = matmul("hg_in_dx", du, w["hg_w_in"], mode="nt")
    return dhn, dict(hg_w_in=g_w_in[None], hg_norm_g=g_norm, hg_w_out=g_w_out[None], hgrn_lb=g_lb)


def _rope_tables(S):
    t = np.arange(S)
    row = (t // GRID_W).astype(np.float32)
    col = (t % GRID_W).astype(np.float32)
    inv = (ROPE_THETA ** (-np.arange(0, ROPE_AXIS, 2, dtype=np.float32) / ROPE_AXIS)).astype(np.float32)
    ar = jnp.asarray(row)[:, None] * jnp.asarray(inv)[None, :]
    ac = jnp.asarray(col)[:, None] * jnp.asarray(inv)[None, :]
    cos = jnp.concatenate([jnp.cos(ar), jnp.cos(ar), jnp.cos(ac), jnp.cos(ac)], axis=1)
    sin = jnp.concatenate([-jnp.sin(ar), jnp.sin(ar), -jnp.sin(ac), jnp.sin(ac)], axis=1)
    return cos.astype(F32), sin.astype(F32)


def _rope(x, cos, sin):
    h = ROPE_AXIS // 2
    sw = jnp.concatenate([x[:, h:2 * h], x[:, 0:h], x[:, 3 * h:4 * h], x[:, 2 * h:3 * h]], axis=1)
    return x * cos + sw * sin


def _at_pre(uq, uk, cos, sin, qg, kg):
    qs, ks = [], []
    for h in range(AT_HEADS):
        qs.append(_rope(_rms(uq[:, h * AT_HD:(h + 1) * AT_HD], qg), cos, sin) * (AT_HD ** -0.5))
    for h in range(AT_KV):
        ks.append(_rope(_rms(uk[:, h * AT_HD:(h + 1) * AT_HD], kg), cos, sin))
    return jnp.concatenate(qs, axis=1), jnp.concatenate(ks, axis=1)


def _stack_heads(x):
    return jnp.concatenate([x[:, :AT_HD], x[:, AT_HD:]], axis=0)


def _unstack_heads(x):
    t = x.shape[0] // 2
    return jnp.concatenate([x[:t], x[t:]], axis=1)


def at_flash_fwd(q, k, u):
    S = q.shape[0]
    tq, tk = _pick(S, 512), _pick(S, 4096)
    nq, nk = S // tq, S // tk
    gw = AT_GRP * AT_HD

    def body(q_ref, k_ref, v_ref, o_ref, lse_ref, m_s, l_s, acc):
        j = pl.program_id(2)

        @pl.when(j == 0)
        def _():
            m_s[...] = jnp.full_like(m_s, NEG_BIG)
            l_s[...] = jnp.zeros_like(l_s)
            acc[...] = jnp.zeros_like(acc)

        s = _mm_nt(_stack_heads(q_ref[...]), k_ref[...])
        m_new = jnp.maximum(m_s[...], jnp.max(s, axis=-1, keepdims=True))
        alpha = jnp.exp(m_s[...] - m_new)
        p = jnp.exp(s - m_new)
        l_s[...] = alpha * l_s[...] + jnp.sum(p, axis=-1, keepdims=True)
        acc[...] = alpha * acc[...] + _mm(p, v_ref[...])
        m_s[...] = m_new

        @pl.when(j == nk - 1)
        def _():
            o_ref[...] = _unstack_heads(acc[...] / l_s[...])
            lse_ref[0, 0] = m_s[...] + jnp.log(l_s[...])

    return pl.pallas_call(
        body, name="at_flash_fwd", grid=(AT_KV, nq, nk),
        in_specs=[pl.BlockSpec((tq, gw), lambda h, i, j: (i, h)),
                  pl.BlockSpec((tk, AT_HD), lambda h, i, j: (j, h)),
                  pl.BlockSpec((tk, AT_HD), lambda h, i, j: (j, (AT_QW + AT_KW) // AT_HD + h))],
        out_specs=[pl.BlockSpec((tq, gw), lambda h, i, j: (i, h)),
                   pl.BlockSpec((1, 1, 2 * tq, 1), lambda h, i, j: (h, i, 0, 0))],
        out_shape=[jax.ShapeDtypeStruct((S, AT_QW), F32), jax.ShapeDtypeStruct((AT_KV, nq, 2 * tq, 1), F32)],
        scratch_shapes=[pltpu.VMEM((2 * tq, 1), F32), pltpu.VMEM((2 * tq, 1), F32), pltpu.VMEM((2 * tq, AT_HD), F32)],
        compiler_params=_cp(("parallel", "parallel", "arbitrary")),
    )(q, k, u)


def at_flash_bwd(q, k, u, o, lse, do):
    S = q.shape[0]
    tq, tk = _pick(S, 512), _pick(S, 1024)
    nq, nk = S // tq, S // tk
    gw = AT_GRP * AT_HD

    def body(q_ref, k_ref, v_ref, o_ref, lse_ref, do_ref, dq_ref, dk_ref, dv_ref, dk_acc, dv_acc):
        j, i = pl.program_id(1), pl.program_id(2)

        @pl.when(i == 0)
        def _():
            dk_acc[...] = jnp.zeros_like(dk_acc)
            dv_acc[...] = jnp.zeros_like(dv_acc)

        q2 = _stack_heads(q_ref[...])
        do_blk = do_ref[...]
        do2 = _stack_heads(do_blk)
        delta = _stack_heads(do_blk * o_ref[...])
        delta = jnp.sum(delta, axis=-1, keepdims=True)
        kb, vb = k_ref[...], v_ref[...]
        p = jnp.exp(_mm_nt(q2, kb) - lse_ref[0, 0])
        dv_acc[...] += _mm_tn(p, do2)
        ds = p * (_mm_nt(do2, vb) - delta)
        dk_acc[...] += _mm_tn(ds, q2)
        dq = _unstack_heads(_mm(ds, kb))
        rows = pl.ds(pl.multiple_of(i * tq, tq), tq)

        @pl.when(j == 0)
        def _():
            dq_ref[rows, :] = dq

        @pl.when(j > 0)
        def _():
            dq_ref[rows, :] += dq

        @pl.when(i == nq - 1)
        def _():
            dk_ref[...] = dk_acc[...]
            dv_ref[...] = dv_acc[...]

    return pl.pallas_call(
        body, name="at_flash_bwd", grid=(AT_KV, nk, nq),
        in_specs=[pl.BlockSpec((tq, gw), lambda h, j, i: (i, h)),
                  pl.BlockSpec((tk, AT_HD), lambda h, j, i: (j, h)),
                  pl.BlockSpec((tk, AT_HD), lambda h, j, i: (j, (AT_QW + AT_KW) // AT_HD + h)),
                  pl.BlockSpec((tq, gw), lambda h, j, i: (i, h)),
                  pl.BlockSpec((1, 1, 2 * tq, 1), lambda h, j, i: (h, i, 0, 0)),
                  pl.BlockSpec((tq, gw), lambda h, j, i: (i, h))],
        out_specs=[pl.BlockSpec((S, gw), lambda h, j, i: (0, h)),
                   pl.BlockSpec((tk, AT_HD), lambda h, j, i: (j, h)),
                   pl.BlockSpec((tk, AT_HD), lambda h, j, i: (j, h))],
        out_shape=[jax.ShapeDtypeStruct((S, AT_QW), F32), jax.ShapeDtypeStruct((S, AT_KW), F32),
                   jax.ShapeDtypeStruct((S, AT_KW), F32)],
        scratch_shapes=[pltpu.VMEM((tk, AT_HD), F32), pltpu.VMEM((tk, AT_HD), F32)],
        compiler_params=_cp(("parallel", "arbitrary", "arbitrary")),
    )(q, k, u, o, lse, do)


def at_forward(x, hn, w):
    S = x.shape[0]
    u = matmul("at_in", hn, w["at_w_in"])
    cos, sin = _rope_tables(S)
    (q, k), _ = rowcall("at_pre", lambda uq, uk, c, s, qg, kg: (_at_pre(uq, uk, c, s, qg, kg), ()),
                        [_col(u, AT_QW, 0), _col(u, AT_KW, 2), cos, sin], [w["at_q_norm_g"], w["at_k_norm_g"]],
                        [(AT_QW, BF16), (AT_KW, BF16)], tb=256)
    o, lse = at_flash_fwd(q, k, u)
    (og,), _ = rowcall("at_gate", lambda o, gate: ((o * _silu(gate),), ()), [o, _col(u, AT_QW, 2)], [],
                       [(AT_QW, BF16)], tb=256)
    out = matmul("at_out", og, w["at_w_out"], res=x)
    return out, dict(hn=hn, u=u, cos=cos, sin=sin, q=q, k=k, o=o, lse=lse, og=og)


def at_backward(dy, sv, w):
    u = sv["u"]
    dog = matmul("at_out_dx", dy, w["at_w_out"], mode="nt")
    g_w_out = matmul("at_out_dw", sv["og"], dy, mode="tn")

    def gate_bwd(o, gate, dog):
        _, vjp = jax.vjp(lambda o, gate: o * _silu(gate), o, gate)
        return vjp(dog), ()

    (do, dgate), _ = rowcall("at_gate_bwd", gate_bwd, [sv["o"], _col(u, AT_QW, 2), dog], [],
                             [(AT_QW, F32)] * 2, tb=256)
    dq, dk, dv = at_flash_bwd(sv["q"], sv["k"], u, sv["o"], sv["lse"], do)

    def pre_bwd(uq, uk, cos, sin, dq, dk, dv, dgate, qg, kg):
        _, vjp = jax.vjp(lambda uq, uk, qg, kg: _at_pre(uq, uk, cos, sin, qg, kg), uq, uk, qg, kg)
        duq, duk, dqg, dkg = vjp((dq, dk))
        return (jnp.concatenate([duq, duk, dv, dgate], axis=1),), (dqg, dkg)

    (du,), (g_qg, g_kg) = rowcall(
        "at_pre_bwd", pre_bwd, [_col(u, AT_QW, 0), _col(u, AT_KW, 2), sv["cos"], sv["sin"], dq, dk, dv, dgate],
        [w["at_q_norm_g"], w["at_k_norm_g"]], [(AT_IN, BF16)], [(1, AT_HD), (1, AT_HD)], tb=128)
    hn = sv["hn"]
    g_w_in = matmul("at_in_dw", hn, du, mode="tn")
    dhn = matmul("at_in_dx", du, w["at_w_in"], mode="nt")
    return dhn, dict(at_w_in=g_w_in[None], at_q_norm_g=g_qg, at_k_norm_g=g_kg, at_w_out=g_w_out[None])


def _t5_bucket_np(rel):
    half = REL_BUCKETS // 2
    exact = half // 2
    n = np.abs(rel)
    large = exact + (np.log(np.maximum(n, 1).astype(np.float32) / exact)
                     / math.log(REL_MAX_DIST / exact) * (half - exact)).astype(np.int32)
    large = np.minimum(large, half - 1)
    return np.where(rel > 0, half, 0) + np.where(n < exact, n, large)


def _dl_tq(S, dil):
    return min(128, S // dil)


def _dl_bias_maps(tq, dil):
    W = tq + 2 * DL_STEPS
    i = np.arange(tq)[:, None]
    wdx = np.arange(W)[None, :]
    dm = wdx - DL_STEPS - i
    bucket = _t5_bucket_np(dm * dil).reshape(-1).astype(np.int32)
    band = np.where(np.abs(dm) <= DL_STEPS, 0.0, NEG_BIG).reshape(1, -1).astype(np.float32)
    onehot = (jnp.asarray(bucket)[None, :] == jnp.arange(REL_BUCKETS, dtype=jnp.int32)[:, None]).astype(F32)
    return onehot, jnp.asarray(band)


def _dl_attend(q, kwin, vwin, T, valid):
    tq = q.shape[0]
    os, ls = [], []
    for h in range(DL_HEADS):
        sl = slice(h * DL_HD, (h + 1) * DL_HD)
        s = _mm_nt(q[:, sl] * (DL_HD ** -0.5), kwin[:, sl]) + T[h]
        s = jnp.where(valid, s, NEG_BIG)
        m = lax.stop_gradient(jnp.max(s, axis=-1, keepdims=True))
        lse = m + jnp.log(jnp.sum(jnp.exp(s - m), axis=-1, keepdims=True))
        p = jnp.exp(s - lse)
        os.append(_mm(p, vwin[:, sl]))
        ls.append(jnp.broadcast_to(lse, (tq, DL_HD)))
    return jnp.concatenate(os, axis=1), jnp.concatenate(ls, axis=1)


def _dl_specs(tq, Ls):
    nb = Ls // tq
    hs = DL_STEPS
    per = tq // hs
    nh = Ls // hs

    def main(c):
        return pl.BlockSpec((tq, DL_W), lambda r, i: (r * nb + i, c))

    def prev(c):
        return pl.BlockSpec((hs, DL_W), lambda r, i: (r * nh + jnp.maximum(i * per - 1, 0), c))

    def nxt(c):
        return pl.BlockSpec((hs, DL_W), lambda r, i: (r * nh + jnp.minimum((i + 1) * per, nh - 1), c))

    return nb, main, prev, nxt


def _dl_valid(i, tq, Ls):
    W = tq + 2 * DL_STEPS
    mk = i * tq - DL_STEPS + lax.broadcasted_iota(jnp.int32, (1, W), 1)
    return (mk >= 0) & (mk < Ls)


def dl_attn_fwd(gi, dil, u, T):
    S = u.shape[0]
    Ls = S // dil
    tq = _dl_tq(S, dil)
    nb, main, prev, nxt = _dl_specs(tq, Ls)
    out = main(0)

    def body(q_ref, kp, kc, kn, vp, vc, vn, T_ref, o_ref, l_ref):
        kwin = jnp.concatenate([kp[...], kc[...], kn[...]], axis=0)
        vwin = jnp.concatenate([vp[...], vc[...], vn[...]], axis=0)
        o, l = _dl_attend(q_ref[...], kwin, vwin, T_ref[...], _dl_valid(pl.program_id(1), tq, Ls))
        o_ref[...] = o
        l_ref[...] = l

    o, l = pl.pallas_call(
        body, name=f"dl_attn_fwd{gi}", grid=(dil, nb),
        in_specs=[main(0), prev(1), main(1), nxt(1), prev(2), main(2), nxt(2),
                  pl.BlockSpec(T.shape, _const_map(3))],
        out_specs=[out, out],
        out_shape=[jax.ShapeDtypeStruct((S, DL_W), F32)] * 2,
        compiler_params=_cp(("parallel", "parallel")),
    )(u, u, u, u, u, u, u, T)
    return o, l


def dl_attn_bwd(gi, dil, u, T, do, dl, dgate=None):
    S = u.shape[0]
    Ls = S // dil
    tq = _dl_tq(S, dil)
    hs = DL_STEPS
    W = tq + 2 * hs
    nb, main, prev, nxt = _dl_specs(tq, Ls)
    out = main(0)
    win = pl.BlockSpec((1, W, DL_W), lambda r, i: (r * nb + i, 0, 0))

    def body(q_ref, kp, kc, kn, vp, vc, vn, T_ref, do_ref, dl_ref, dq_ref, dkw_ref, dvw_ref, dT_ref):
        first = (pl.program_id(0) == 0) & (pl.program_id(1) == 0)

        @pl.when(first)
        def _():
            dT_ref[...] = jnp.zeros_like(dT_ref)

        kwin = jnp.concatenate([kp[...], kc[...], kn[...]], axis=0)
        vwin = jnp.concatenate([vp[...], vc[...], vn[...]], axis=0)
        valid = _dl_valid(pl.program_id(1), tq, Ls)
        _, vjp = jax.vjp(lambda q, k, v, T: _dl_attend(q, k, v, T, valid), q_ref[...], kwin, vwin, T_ref[...])
        dq, dkw, dvw, dT = vjp((do_ref[...], dl_ref[...]))
        dq_ref[...] = dq
        dkw_ref[0] = dkw
        dvw_ref[0] = dvw
        dT_ref[...] += dT

    dq, dkw, dvw, dT = pl.pallas_call(
        body, name=f"dl_attn_bwd{gi}", grid=(dil, nb),
        in_specs=[main(0), prev(1), main(1), nxt(1), prev(2), main(2), nxt(2),
                  pl.BlockSpec(T.shape, _const_map(3)), out, out],
        out_specs=[out, win, win, pl.BlockSpec(T.shape, _const_map(3))],
        out_shape=[jax.ShapeDtypeStruct((S, DL_W), F32),
                   jax.ShapeDtypeStruct((dil * nb, W, DL_W), F32),
                   jax.ShapeDtypeStruct((dil * nb, W, DL_W), F32),
                   jax.ShapeDtypeStruct(T.shape, F32)],
        compiler_params=_cp(("arbitrary", "arbitrary")),
    )(u, u, u, u, u, u, u, T, do, dl)

    per = tq // hs
    n_out = 3 if dgate is None else 4

    def fold(*refs):
        dq_ref, kc, kp, kn, vc, vp, vn = refs[:7]
        du_ref = refs[-1]
        i = pl.program_id(1)
        has_p = (i > 0).astype(F32)
        has_n = (i < nb - 1).astype(F32)
        du_ref[:, 0:DL_W] = dq_ref[...].astype(BF16)
        for c, (c_ref, p_ref, n_ref) in enumerate(((kc, kp, kn), (vc, vp, vn)), start=1):
            mid = c_ref[0, hs:hs + tq, :]
            top = mid[0:hs] + p_ref[0] * has_p
            bot = mid[tq - hs:tq] + n_ref[0] * has_n
            parts = [top, bot] if tq == 2 * hs else ([top, mid[hs:tq - hs], bot] if tq > 2 * hs else [top + n_ref[0] * has_n])
            du_ref[:, c * DL_W:(c + 1) * DL_W] = jnp.concatenate(parts, axis=0).astype(BF16)
        if dgate is not None:
            du_ref[:, 3 * DL_W:4 * DL_W] = refs[7][...].astype(BF16)

    wfull = pl.BlockSpec((1, W, DL_W), lambda r, i: (r * nb + i, 0, 0))
    wprev = pl.BlockSpec((1, hs, DL_W), lambda r, i: (r * nb + jnp.maximum(i - 1, 0), per + 1, 0))
    wnext = pl.BlockSpec((1, hs, DL_W), lambda r, i: (r * nb + jnp.minimum(i + 1, nb - 1), 0, 0))
    extra_specs, extra_args = ([], []) if dgate is None else ([out], [dgate])
    du = pl.pallas_call(
        fold, name=f"dl_fold{gi}", grid=(dil, nb),
        in_specs=[out, wfull, wprev, wnext, wfull, wprev, wnext] + extra_specs,
        out_specs=pl.BlockSpec((tq, n_out * DL_W), lambda r, i: (r * nb + i, 0)),
        out_shape=jax.ShapeDtypeStruct((S, n_out * DL_W), BF16),
        compiler_params=_cp(("parallel", "parallel")),
    )(dq, dkw, dkw, dkw, dvw, dvw, dvw, *extra_args)
    return du, dT


def _dl_merge(o0, o1, o2, l0, l1, l2, gate):
    m = jnp.maximum(jnp.maximum(l0, l1), l2)
    e0, e1, e2 = jnp.exp(l0 - m), jnp.exp(l1 - m), jnp.exp(l2 - m)
    den = e0 + e1 + e2
    return ((e0 * o0 + e1 * o1 + e2 * o2) / den) * _silu(gate)


DL_DILS = tuple(d for _, d in DL_PAIRS)


def _dl_group_weights(w_in):
    g3 = 3 * DL_W
    return [jnp.concatenate([w_in[:, :g3], w_in[:, 3 * g3:]], axis=1), w_in[:, g3:2 * g3], w_in[:, 2 * g3:3 * g3]]


def dl_forward(x, hns, w):
    S = x.shape[0]
    wg = _dl_group_weights(w["dl_w_in"])
    rbT = w["rel_bias"].T
    us, os, ls, Ts, maps = [], [], [], [], []
    for gi, dil in enumerate(DL_DILS):
        u = matmul(f"dl_in{gi}", hns[gi], wg[gi])
        tq = _dl_tq(S, dil)
        W = tq + 2 * DL_STEPS
        onehot, band = _dl_bias_maps(tq, dil)
        (T,) = smallcall(f"dl_bias{gi}", lambda rbT, oh, band: (_mm_exact(rbT, oh) + band,), [rbT, onehot, band],
                         [(DL_HEADS, tq * W)])
        T = T.reshape(DL_HEADS, tq, W)
        o, l = dl_attn_fwd(gi, dil, u, T)
        us.append(u)
        os.append(o)
        ls.append(l)
        Ts.append(T)
        maps.append(onehot)
    rows = [a if d == 1 else _perm(a, d) for a, d in zip(os + ls, DL_DILS * 2)] + [_col(us[0], DL_W, 3)]
    (og,), _ = rowcall("dl_merge", lambda *a: ((_dl_merge(*a),), ()), rows, [], [(DL_W, BF16)], tb=256)
    out = matmul("dl_out", og, w["dl_w_out"], res=x)
    return out, dict(hns=hns, us=us, os=os, ls=ls, Ts=Ts, maps=maps, og=og, wg=wg)


def dl_backward(dy, sv, w):
    us = sv["us"]
    dog = matmul("dl_out_dx", dy, w["dl_w_out"], mode="nt")
    g_w_out = matmul("dl_out_dw", sv["og"], dy, mode="tn")

    def merge_bwd(o0, o1, o2, l0, l1, l2, gate, dog):
        _, vjp = jax.vjp(_dl_merge, o0, o1, o2, l0, l1, l2, gate)
        return vjp(dog), ()

    rows = [a if d == 1 else _perm(a, d) for a, d in zip(sv["os"] + sv["ls"], DL_DILS * 2)] + [_col(us[0], DL_W, 3), dog]
    grads7, _ = rowcall("dl_merge_bwd", merge_bwd, rows, [], [(DL_W, F32, d) for d in DL_DILS * 2] + [(DL_W, F32)], tb=256)
    dos, dls, dgate = grads7[0:3], grads7[3:6], grads7[6]
    g_rbT, g_ws, dhns = None, [], []
    for gi, dil in enumerate(DL_DILS):
        du, dT = dl_attn_bwd(gi, dil, us[gi], sv["Ts"][gi], dos[gi], dls[gi], dgate if gi == 0 else None)
        (g,) = smallcall(f"dl_bias_bwd{gi}", lambda dT, oh: (_mm_nt_exact(dT, oh),),
                         [dT.reshape(DL_HEADS, -1), sv["maps"][gi]], [(DL_HEADS, REL_BUCKETS)])
        g_rbT = g if g_rbT is None else g_rbT + g
        g_ws.append(matmul(f"dl_in_dw{gi}", sv["hns"][gi], du, mode="tn"))
        dhns.append(matmul(f"dl_in_dx{gi}", du, sv["wg"][gi], mode="nt"))
    g3 = 3 * DL_W
    g_w_in = jnp.concatenate([g_ws[0][:, :g3], g_ws[1], g_ws[2], g_ws[0][:, g3:]], axis=1)
    return tuple(dhns), dict(dl_w_in=g_w_in[None], dl_w_out=g_w_out[None], rel_bias=g_rbT.T)


_FWD = (ssd_forward, hg_forward, at_forward, dl_forward)
_BWD = (ssd_backward, hg_backward, at_backward, dl_backward)


def _norm_dils(layer):
    return DL_DILS if layer % 4 == 3 else (1,)


def local_step(x, tgt, w):
    saved = []
    h = x
    for layer in range(DEPTH):
        hn = norm_fwd(f"norm{layer}", h, w["norm_g"][layer:layer + 1], _norm_dils(layer))
        extra = (layer,) if layer % 4 == 1 else ()
        h_next, sv = _FWD[layer % 4](h, hn, w, *extra)
        saved.append((h, sv))
        h = h_next
    loss, dh, g_final = loss_head(h, tgt, w["final_g"].reshape(1, D_MODEL))
    grads = {}
    g_norm = [None] * DEPTH
    for layer in reversed(range(DEPTH)):
        h_in, sv = saved[layer]
        extra = (layer,) if layer % 4 == 1 else ()
        dhn, g = _BWD[layer % 4](dh, sv, w, *extra)
        grads.update(g)
        dh, g_norm[layer] = norm_bwd(f"norm{layer}_bwd", h_in, w["norm_g"][layer:layer + 1], dhn, dh, _norm_dils(layer))
    grads["norm_g"] = jnp.concatenate(g_norm, axis=0)
    grads["final_g"] = g_final.reshape(D_MODEL)
    grads["ssd_d"] = jnp.sum(grads.pop("ssd_d_exp").reshape(SSD_HEADS, SSD_HEADDIM), axis=1)[None]
    return loss, dh, grads


IN_NAMES = ("ssd_w_in", "hg_w_in", "at_w_in", "dl_w_in")
OUT_NAMES = ("ssd_w_out", "hg_w_out", "at_w_out", "dl_w_out")
IN_COLS = (SSD_IN // 4, HG_IN // 4, AT_IN // 4, DL_IN // 4)
OUT_ROWS = (SSD_DI // 4, HG_W // 4, AT_QW // 4, DL_W // 4)
PACK_IN = sum(IN_COLS)
PACK_OUT = sum(OUT_ROWS)
N_CHIPS = 4
N_DEV = 8
HBM = pl.BlockSpec(memory_space=pl.ANY)


def _mesh_pos():
    return lax.axis_index("x"), lax.axis_index("y"), lax.axis_index("c")


def _other_chips(x, y):
    return [(1 - x, y), (x, 1 - y), (1 - x, 1 - y)]


def gather_weights(p_in, p_out, p_small):
    h_in, h_out = p_in.shape[0] // 2, p_out.shape[0] // 2

    def body(pin, pout, psm, gin, gout, gsm, send, recv):
        x, y, c = _mesh_pos()
        me = 2 * x + y
        sib = (x, y, 1 - c)
        chips = _other_chips(x, y)

        def rows(half, n):
            return pl.ds(pl.multiple_of(half * n, n), n)

        def rc(src, dst, k, to):
            return pltpu.make_async_remote_copy(src_ref=src, dst_ref=dst, send_sem=send.at[k], recv_sem=recv.at[k],
                                                device_id=to, device_id_type=MESH)

        started = []
        for j, (px, py) in enumerate(chips):
            to = (px, py, c)
            started += [rc(pin.at[rows(c, h_in)], gin.at[me, rows(c, h_in)], 3 * j, to),
                        rc(pout.at[rows(c, h_out)], gout.at[me, rows(c, h_out)], 3 * j + 1, to),
                        rc(psm, gsm.at[me], 3 * j + 2, to)]
        for cp in started:
            cp.start()
        for j, (px, py) in enumerate(chips):
            kp = 2 * px + py
            frm = (px, py, c)
            rc(pin.at[rows(c, h_in)], gin.at[kp, rows(c, h_in)], 3 * j, frm).wait_recv()
            f_in = rc(gin.at[kp, rows(c, h_in)], gin.at[kp, rows(c, h_in)], 9 + 2 * j, sib)
            f_in.start()
            rc(pout.at[rows(c, h_out)], gout.at[kp, rows(c, h_out)], 3 * j + 1, frm).wait_recv()
            f_out = rc(gout.at[kp, rows(c, h_out)], gout.at[kp, rows(c, h_out)], 10 + 2 * j, sib)
            f_out.start()
            rc(psm, gsm.at[kp], 3 * j + 2, frm).wait_recv()
            started += [f_in, f_out]
        for j, (px, py) in enumerate(chips):
            kp = 2 * px + py
            rc(gin.at[kp, rows(1 - c, h_in)], gin.at[kp, rows(1 - c, h_in)], 9 + 2 * j, sib).wait_recv()
            rc(gout.at[kp, rows(1 - c, h_out)], gout.at[kp, rows(1 - c, h_out)], 10 + 2 * j, sib).wait_recv()
        for cp in started:
            cp.wait_send()

    return pl.pallas_call(
        body, name="gather_weights", in_specs=[HBM, HBM, HBM], out_specs=[HBM, HBM, HBM],
        out_shape=[jax.ShapeDtypeStruct((N_CHIPS,) + p_in.shape, p_in.dtype),
                   jax.ShapeDtypeStruct((N_CHIPS,) + p_out.shape, p_out.dtype),
                   jax.ShapeDtypeStruct((N_CHIPS,) + p_small.shape, p_small.dtype)],
        scratch_shapes=[pltpu.SemaphoreType.DMA((15,)), pltpu.SemaphoreType.DMA((15,))],
        compiler_params=pltpu.CompilerParams(has_side_effects=True),
    )(p_in, p_out, p_small)


def swap_halves(g_in, g_out):
    h_in, h_out = g_in.shape[1] // 2, g_out.shape[1] // 2

    def body(gi, go, ri, ro, send, recv):
        x, y, c = _mesh_pos()
        sib = (x, y, 1 - c)

        def rows(half, n):
            return pl.ds(pl.multiple_of(half * n, n), n)

        cps = [pltpu.make_async_remote_copy(src_ref=gi.at[:, rows(1 - c, h_in)], dst_ref=ri, send_sem=send.at[0],
                                            recv_sem=recv.at[0], device_id=sib, device_id_type=MESH),
               pltpu.make_async_remote_copy(src_ref=go.at[:, rows(1 - c, h_out)], dst_ref=ro, send_sem=send.at[1],
                                            recv_sem=recv.at[1], device_id=sib, device_id_type=MESH)]
        for cp in cps:
            cp.start()
        for cp in cps:
            cp.wait()

    return pl.pallas_call(
        body, name="swap_halves", in_specs=[HBM, HBM], out_specs=[HBM, HBM],
        out_shape=[jax.ShapeDtypeStruct((N_CHIPS, h_in, g_in.shape[2]), g_in.dtype),
                   jax.ShapeDtypeStruct((N_CHIPS, h_out, g_out.shape[2]), g_out.dtype)],
        scratch_shapes=[pltpu.SemaphoreType.DMA((2,)), pltpu.SemaphoreType.DMA((2,))],
        compiler_params=pltpu.CompilerParams(has_side_effects=True),
    )(g_in, g_out)


def half_add(name, g, r, c_idx, tb):
    _, rows2, C = g.shape
    h = rows2 // 2
    nb = h // tb

    def body(c_ref, g_ref, r_ref, f_ref, b_ref):
        s = g_ref[...] + r_ref[...]
        f_ref[...] = s
        b_ref[...] = s.astype(BF16)

    grid_spec = pltpu.PrefetchScalarGridSpec(
        num_scalar_prefetch=1, grid=(N_CHIPS, nb),
        in_specs=[pl.BlockSpec((1, tb, C), lambda k, i, c: (k, c[0] * nb + i, 0)),
                  pl.BlockSpec((1, tb, C), lambda k, i, c: (k, i, 0))],
        out_specs=[pl.BlockSpec((1, tb, C), lambda k, i, c: (k, i, 0))] * 2)
    return pl.pallas_call(
        body, name=name, grid_spec=grid_spec,
        out_shape=[jax.ShapeDtypeStruct((N_CHIPS, h, C), F32), jax.ShapeDtypeStruct((N_CHIPS, h, C), BF16)],
        compiler_params=_cp(("parallel", "parallel")),
    )(c_idx, g, r)


def scatter_chips(b_in, b_out):
    def body(bi, bo, ri, ro, send, recv):
        x, y, c = _mesh_pos()
        cps = []
        for j, (px, py) in enumerate(_other_chips(x, y)):
            kp = 2 * px + py
            to = (px, py, c)
            cps += [pltpu.make_async_remote_copy(src_ref=bi.at[kp], dst_ref=ri.at[j], send_sem=send.at[2 * j],
                                                 recv_sem=recv.at[2 * j], device_id=to, device_id_type=MESH),
                    pltpu.make_async_remote_copy(src_ref=bo.at[kp], dst_ref=ro.at[j], send_sem=send.at[2 * j + 1],
                                                 recv_sem=recv.at[2 * j + 1], device_id=to, device_id_type=MESH)]
        for cp in cps:
            cp.start()
        for cp in cps:
            cp.wait()

    return pl.pallas_call(
        body, name="scatter_chips", in_specs=[HBM, HBM], out_specs=[HBM, HBM],
        out_shape=[jax.ShapeDtypeStruct((3,) + b_in.shape[1:], BF16), jax.ShapeDtypeStruct((3,) + b_out.shape[1:], BF16)],
        scratch_shapes=[pltpu.SemaphoreType.DMA((6,)), pltpu.SemaphoreType.DMA((6,))],
        compiler_params=pltpu.CompilerParams(has_side_effects=True),
    )(b_in, b_out)


def chip_sum(name, f, r, me_idx, tb):
    _, h, C = f.shape
    nb = h // tb

    def body(me_ref, f_ref, r0, r1, r2, o_ref):
        o_ref[...] = ((f_ref[0] + r0[0].astype(F32)) + r1[0].astype(F32)) + r2[0].astype(F32)

    def slot(j):
        return pl.BlockSpec((1, tb, C), lambda i, me: (j, i, 0))

    grid_spec = pltpu.PrefetchScalarGridSpec(
        num_scalar_prefetch=1, grid=(nb,),
        in_specs=[pl.BlockSpec((1, tb, C), lambda i, me: (me[0], i, 0)), slot(0), slot(1), slot(2)],
        out_specs=pl.BlockSpec((tb, C), lambda i, me: (i, 0)))
    return pl.pallas_call(
        body, name=name, grid_spec=grid_spec, out_shape=jax.ShapeDtypeStruct((h, C), F32),
        compiler_params=_cp(("parallel",)),
    )(me_idx, f, r, r, r)


def share_halves(f_in, f_out):
    def body(fi, fo, oi, oo, send, recv):
        x, y, c = _mesh_pos()
        sib = (x, y, 1 - c)
        cps = [pltpu.make_async_remote_copy(src_ref=fi, dst_ref=oi, send_sem=send.at[0], recv_sem=recv.at[0],
                                            device_id=sib, device_id_type=MESH),
               pltpu.make_async_remote_copy(src_ref=fo, dst_ref=oo, send_sem=send.at[1], recv_sem=recv.at[1],
                                            device_id=sib, device_id_type=MESH)]
        for cp in cps:
            cp.start()
        for cp in cps:
            cp.wait()

    return pl.pallas_call(
        body, name="share_halves", in_specs=[HBM, HBM], out_specs=[HBM, HBM],
        out_shape=[jax.ShapeDtypeStruct(f_in.shape, F32), jax.ShapeDtypeStruct(f_out.shape, F32)],
        scratch_shapes=[pltpu.SemaphoreType.DMA((2,)), pltpu.SemaphoreType.DMA((2,))],
        compiler_params=pltpu.CompilerParams(has_side_effects=True),
    )(f_in, f_out)


def gather_small(pack):
    def body(p, g, send, recv, lsem):
        x, y, c = _mesh_pos()
        me = 4 * x + 2 * y + c
        local = pltpu.make_async_copy(p, g.at[me], lsem)
        local.start()
        cps = []
        k = 0
        for fx in (0, 1):
            for fy in (0, 1):
                for fc in (0, 1):
                    if fx + fy + fc == 0:
                        continue
                    to = (x ^ fx, y ^ fy, c ^ fc)
                    cps.append((pltpu.make_async_remote_copy(src_ref=p, dst_ref=g.at[me], send_sem=send.at[k],
                                                             recv_sem=recv.at[k], device_id=to, device_id_type=MESH), to, k))
                    k += 1
        for cp, _, _ in cps:
            cp.start()
        for cp, to, k in cps:
            frm = 4 * to[0] + 2 * to[1] + to[2]
            pltpu.make_async_remote_copy(src_ref=p, dst_ref=g.at[frm], send_sem=send.at[k], recv_sem=recv.at[k],
                                         device_id=to, device_id_type=MESH).wait_recv()
        for cp, _, _ in cps:
            cp.wait_send()
        local.wait()

    return pl.pallas_call(
        body, name="gather_small", in_specs=[HBM], out_specs=HBM,
        out_shape=jax.ShapeDtypeStruct((N_DEV,) + pack.shape, pack.dtype),
        scratch_shapes=[pltpu.SemaphoreType.DMA((7,)), pltpu.SemaphoreType.DMA((7,)), pltpu.SemaphoreType.DMA],
        compiler_params=pltpu.CompilerParams(has_side_effects=True),
    )(pack)


def _adamw(w, g, m, v):
    m = ADAM_B1 * m + (1.0 - ADAM_B1) * g
    v = ADAM_B2 * v + (1.0 - ADAM_B2) * (g * g)
    m_hat = m / (1.0 - ADAM_B1 ** ADAM_STEP)
    v_hat = v / (1.0 - ADAM_B2 ** ADAM_STEP)
    delta = -ADAM_LR * (m_hat / (jnp.sqrt(v_hat) + ADAM_EPS) + ADAM_WD * w)
    return delta, m, v


def adamw_big(name, w, g, m, v):
    shp = w.shape
    flat = lambda a: a.reshape(shp[-2], shp[-1])
    (d, nm, nv), _ = rowcall(name, lambda w, g, m, v: (_adamw(w, g, m, v), ()), [flat(w), flat(g), flat(m), flat(v)], [],
                             [(shp[-1], F32)] * 3, tb=256)
    return d.reshape(shp), nm.reshape(shp), nv.reshape(shp)


def _pack_small(arrs):
    flat = jnp.concatenate([a.reshape(-1) for a in arrs])
    n = flat.shape[0]
    rows = -(-n // (SUBLANE * LANE)) * SUBLANE
    return jnp.pad(flat, (0, rows * LANE - n)).reshape(rows, LANE)


def _unpack_small(pack, shapes):
    flat = pack.reshape(-1)
    outs, off = [], 0
    for s in shapes:
        n = int(np.prod(s))
        outs.append(flat[off:off + n].reshape(s))
        off += n
    return outs


SMALL_NAMES = ("norm_g", "final_g", "rel_bias", "hgrn_lb", "ssd_conv_w", "ssd_conv_b", "ssd_dt_bias", "ssd_a_log",
               "ssd_d", "ssd_norm_g", "hg_norm_g", "at_q_norm_g", "at_k_norm_g")
ALL_NAMES = ("norm_g", "final_g", "rel_bias", "hgrn_lb", "ssd_w_in", "ssd_conv_w", "ssd_conv_b", "ssd_dt_bias",
             "ssd_a_log", "ssd_d", "ssd_norm_g", "ssd_w_out", "hg_w_in", "hg_norm_g", "hg_w_out", "at_w_in",
             "at_q_norm_g", "at_k_norm_g", "at_w_out", "dl_w_in", "dl_w_out")


def kernel(x, norm_g, final_g, rel_bias, hgrn_lb, ssd_w_in, ssd_conv_w, ssd_conv_b, ssd_dt_bias, ssd_a_log, ssd_d, ssd_norm_g, ssd_w_out, hg_w_in, hg_norm_g, hg_w_out, at_w_in, at_q_norm_g, at_k_norm_g, at_w_out, dl_w_in, dl_w_out, loss_target, m_norm_g, m_final_g, m_rel_bias, m_hgrn_lb, m_ssd_w_in, m_ssd_conv_w, m_ssd_conv_b, m_ssd_dt_bias, m_ssd_a_log, m_ssd_d, m_ssd_norm_g, m_ssd_w_out, m_hg_w_in, m_hg_norm_g, m_hg_w_out, m_at_w_in, m_at_q_norm_g, m_at_k_norm_g, m_at_w_out, m_dl_w_in, m_dl_w_out, v_norm_g, v_final_g, v_rel_bias, v_hgrn_lb, v_ssd_w_in, v_ssd_conv_w, v_ssd_conv_b, v_ssd_dt_bias, v_ssd_a_log, v_ssd_d, v_ssd_norm_g, v_ssd_w_out, v_hg_w_in, v_hg_norm_g, v_hg_w_out, v_at_w_in, v_at_q_norm_g, v_at_k_norm_g, v_at_w_out, v_dl_w_in, v_dl_w_out):
    args = locals()
    W = {n: args[n] for n in ALL_NAMES}
    M = {n: args["m_" + n] for n in ALL_NAMES}
    V = {n: args["v_" + n] for n in ALL_NAMES}
    xi, yi, ci = lax.axis_index("x"), lax.axis_index("y"), lax.axis_index("c")
    chip = 2 * xi + yi
    conv_shard = SSD_CONV_CH // N_CHIPS
    hgn_shard = HG_W // N_CHIPS

    p_in = jnp.concatenate([W[n][0].astype(BF16) for n in IN_NAMES], axis=1)
    p_out = jnp.concatenate([W[n][0].astype(BF16) for n in OUT_NAMES], axis=0)
    p_small = jnp.concatenate([
        jnp.pad(ssd_conv_w[0], ((0, 0), (0, D_MODEL - conv_shard))),
        jnp.pad(hg_norm_g, ((0, 0), (0, D_MODEL - hgn_shard)))], axis=0)
    g_in, g_out, g_small = gather_weights(p_in, p_out, p_small)

    def slot(stack, own, k):
        return jnp.where(chip == k, own, stack[k])

    full = {}
    off = 0
    for n, cols in zip(IN_NAMES, IN_COLS):
        full[n] = jnp.concatenate([slot(g_in, p_in, k)[:, off:off + cols] for k in range(N_CHIPS)], axis=1)
        off += cols
    off = 0
    for n, rows in zip(OUT_NAMES, OUT_ROWS):
        full[n] = jnp.concatenate([slot(g_out, p_out, k)[off:off + rows] for k in range(N_CHIPS)], axis=0)
        off += rows
    conv_full = jnp.concatenate([slot(g_small, p_small, k)[:SSD_CONV, :conv_shard] for k in range(N_CHIPS)], axis=1)
    hgn_full = jnp.concatenate([slot(g_small, p_small, k)[SSD_CONV:SSD_CONV + 1, :hgn_shard] for k in range(N_CHIPS)], axis=1)
    w = dict(
        norm_g=norm_g, final_g=final_g, rel_bias=rel_bias, hgrn_lb=hgrn_lb,
        ssd_w_main=full["ssd_w_in"][:, :SSD_MAIN], ssd_w_dt=full["ssd_w_in"][:, SSD_MAIN:],
        ssd_conv_w8=jnp.concatenate([conv_full, jnp.zeros((1, SSD_CONV_CH), F32)], axis=0),
        ssd_conv_b=ssd_conv_b, ssd_dt_bias=ssd_dt_bias, ssd_a_log=ssd_a_log, ssd_d=ssd_d, ssd_norm_g=ssd_norm_g,
        ssd_w_out=full["ssd_w_out"], hg_w_in=full["hg_w_in"], hg_norm_g=hgn_full, hg_w_out=full["hg_w_out"],
        at_w_in=full["at_w_in"], at_q_norm_g=at_q_norm_g, at_k_norm_g=at_k_norm_g, at_w_out=full["at_w_out"],
        dl_w_in=full["dl_w_in"], dl_w_out=full["dl_w_out"])

    loss_tile, grad_x, grads = local_step(x[0], loss_target[0], w)
    loss = lax.psum(loss_tile[0, 0], ("x", "y", "c"))

    gp_in = jnp.concatenate([grads[n][0].reshape(D_MODEL, N_CHIPS, cols).transpose(1, 0, 2)
                             for n, cols in zip(IN_NAMES, IN_COLS)], axis=2)
    gp_out = jnp.concatenate([grads[n][0].reshape(N_CHIPS, rows, D_MODEL)
                              for n, rows in zip(OUT_NAMES, OUT_ROWS)], axis=1)
    r_in, r_out = swap_halves(gp_in, gp_out)
    c_idx = ci.astype(jnp.int32).reshape(1)
    me_idx = chip.astype(jnp.int32).reshape(1)
    f_in, b_in = half_add("half_add_in", gp_in, r_in, c_idx, 128)
    f_out, b_out = half_add("half_add_out", gp_out, r_out, c_idx, 256)
    x_in, x_out = scatter_chips(b_in, b_out)
    s_in = chip_sum("chip_sum_in", f_in, x_in, me_idx, 128)
    s_out = chip_sum("chip_sum_out", f_out, x_out, me_idx, 256)
    o_in, o_out = share_halves(s_in, s_out)
    red_in = jnp.where(ci == 0, jnp.concatenate([s_in, o_in], axis=0), jnp.concatenate([o_in, s_in], axis=0))
    red_out = jnp.where(ci == 0, jnp.concatenate([s_out, o_out], axis=0), jnp.concatenate([o_out, s_out], axis=0))
    G = {}
    off = 0
    for n, cols in zip(IN_NAMES, IN_COLS):
        G[n] = red_in[:, off:off + cols][None]
        off += cols
    off = 0
    for n, rows in zip(OUT_NAMES, OUT_ROWS):
        G[n] = red_out[off:off + rows][None]
        off += rows

    small_full = [grads[n].reshape(-1) for n in SMALL_NAMES]
    shapes_full = [grads[n].shape for n in SMALL_NAMES]
    packs = gather_small(_pack_small(small_full))
    (red_small,) = smallcall("sum_small", lambda p: (functools.reduce(lambda a, b: a + b, [p[k] for k in range(N_DEV)]),),
                             [packs], [packs.shape[1:]])
    for n, g in zip(SMALL_NAMES, _unpack_small(red_small, shapes_full)):
        G[n] = g
    G["ssd_conv_w"] = lax.dynamic_slice_in_dim(G["ssd_conv_w"].reshape(1, SSD_CONV, SSD_CONV_CH), chip * conv_shard, conv_shard, axis=2)
    G["hg_norm_g"] = lax.dynamic_slice_in_dim(G["hg_norm_g"].reshape(1, HG_W), chip * hgn_shard, hgn_shard, axis=1)
    for n in SMALL_NAMES:
        G[n] = G[n].reshape(W[n].shape)

    D, NM, NV = {}, {}, {}
    for n in IN_NAMES + OUT_NAMES:
        D[n], NM[n], NV[n] = adamw_big("adamw_" + n, W[n], G[n], M[n], V[n])
    shapes = [W[n].shape for n in SMALL_NAMES]
    pk = [_pack_small([T[n] for n in SMALL_NAMES]) for T in (W, G, M, V)]
    outs = smallcall("adamw_small", lambda w, g, m, v: _adamw(w, g, m, v), pk, [pk[0].shape] * 3)
    for T, pack in zip((D, NM, NV), outs):
        for n, a in zip(SMALL_NAMES, _unpack_small(pack, shapes)):
            T[n] = a
    return (loss, grad_x[None], *[G[n] for n in ALL_NAMES], *[D[n] for n in ALL_NAMES],
            *[NM[n] for n in ALL_NAMES], *[NV[n] for n in ALL_NAMES])
```

```python
import functools
import math

import numpy as np
import jax
import jax.numpy as jnp
from jax import lax
from jax.experimental import pallas as pl
from jax.experimental.pallas import tpu as pltpu

F32 = jnp.float32
BF16 = jnp.bfloat16
MESH = pl.DeviceIdType.MESH

D_MODEL = 1024
DEPTH = 4
GRID_W = 64
EPS = 1e-6
NEG_BIG = -1e30

SSD_DI = 2048
SSD_HEADDIM = 64
SSD_HEADS = 32
SSD_GROUPS = 4
SSD_HPG = 8
SSD_STATE = 128
SSD_CONV = 7
SSD_CHUNK = 128
SSD_GPS = 4
SSD_CONV_CH = SSD_DI + 2 * SSD_GROUPS * SSD_STATE
SSD_MAIN = SSD_DI + SSD_CONV_CH
SSD_IN = SSD_MAIN + 2 * SSD_HEADS

HG_HEADS = 8
HG_EXPAND = 128
HG_W = 1024
HG_CHUNK = 32
HG_ROWS = 128
HG_HPS = 8
HG_IN = 5 * HG_W

AT_HEADS = 16
AT_KV = 8
AT_GRP = 2
AT_HD = 128
ROPE_THETA = 10000.0
ROPE_AXIS = 64
AT_QW = AT_HEADS * AT_HD
AT_KW = AT_KV * AT_HD
AT_IN = 2 * AT_QW + 2 * AT_KW

DL_PAIRS = ((128, 1), (512, 4), (2048, 16))
DL_HEADS = 16
DL_HD = 64
DL_W = 1024
DL_STEPS = 64
DL_IN = 10 * DL_W
REL_BUCKETS = 32
REL_MAX_DIST = 1024

ADAM_LR = 0.001
ADAM_B1 = 0.9
ADAM_B2 = 0.999
ADAM_EPS = 1e-08
ADAM_WD = 0.01
ADAM_STEP = 10

VMEM_LIMIT = 56 * 1024 * 1024
LANE = 128
SUBLANE = 8


def _cp(sem=None):
    return pltpu.CompilerParams(dimension_semantics=sem, vmem_limit_bytes=VMEM_LIMIT)


_NN, _NT, _TN = ((1,), (0,)), ((1,), (1,)), ((0,), (0,))


def _dot(a, b, dims):
    return lax.dot_general(a.astype(BF16), b.astype(BF16), (dims, ((), ())), preferred_element_type=F32)


def _dot_rule(dims, da_rule, db_rule):
    @jax.custom_vjp
    def f(a, b):
        return _dot(a, b, dims)

    def fwd(a, b):
        return _dot(a, b, dims), (a, b)

    def bwd(res, g):
        a, b = res
        return da_rule(a, b, g).astype(a.dtype), db_rule(a, b, g).astype(b.dtype)

    f.defvjp(fwd, bwd)
    return f


_mm = _dot_rule(_NN, lambda a, b, g: _dot(g, b, _NT), lambda a, b, g: _dot(a, g, _TN))
_mm_nt = _dot_rule(_NT, lambda a, b, g: _dot(g, b, _NN), lambda a, b, g: _dot(g, a, _TN))
_mm_tn = _dot_rule(_TN, lambda a, b, g: _dot(b, g, _NT), lambda a, b, g: _dot(a, g, _NN))


def _mm_exact(a, b):
    return jnp.dot(a, b, preferred_element_type=F32, precision=lax.Precision.HIGHEST)


def _dot3(t, a, dims):
    hi = a.astype(BF16)
    r1 = a - hi.astype(F32)
    mid = r1.astype(BF16)
    lo = r1 - mid.astype(F32)
    return _dot(t, hi, dims) + (_dot(t, mid, dims) + _dot(t, lo, dims))


@jax.custom_vjp
def _mm_tri(t, a):
    return _dot3(t, a, _NN)


def _mm_tri_fwd(t, a):
    return _dot3(t, a, _NN), t


def _mm_tri_bwd(t, g):
    return None, _dot3(t, g, _TN)


_mm_tri.defvjp(_mm_tri_fwd, _mm_tri_bwd)


def _mm_nt_exact(a, b):
    return lax.dot_general(a, b, (((1,), (1,)), ((), ())), preferred_element_type=F32,
                           precision=lax.Precision.HIGHEST)


def _silu(x):
    return x * jax.nn.sigmoid(x)


def _softplus(z):
    return jnp.maximum(z, 0.0) + jnp.log(1.0 + jnp.exp(-jnp.abs(z)))


def _pick(dim, pref):
    best = None
    t = LANE
    while t <= min(dim, pref):
        if dim % t == 0:
            best = t
        t += LANE
    return best if best is not None else dim


def _const_map(n):
    return lambda *_: (0,) * n


MM_BLOCK_BYTES = 8 * 1024 * 1024


def _mm_tiles(mode, M, N, K, a_bytes, b_bytes):
    if mode == "nn":
        tk = K if K <= 2048 else _pick(K, 1024)
        tm = _pick(M, max(512, MM_BLOCK_BYTES // (tk * a_bytes)))
        tn = _pick(N, 512)
    elif mode == "tn":
        tk = K if K <= 4096 else _pick(K, 1024)
        tm = _pick(M, MM_BLOCK_BYTES // (tk * a_bytes))
        tn = _pick(N, MM_BLOCK_BYTES // (tk * b_bytes))
    else:
        tk = _pick(K, 1024)
        tn = _pick(N, 1024)
        tm = _pick(M, MM_BLOCK_BYTES // (8 * tn))
    return tm, tn, tk


def matmul(name, a, b, mode="nn", res=None, out_dtype=F32):
    if mode == "tn":
        K, M = a.shape
    else:
        M, K = a.shape
    N = b.shape[0] if mode == "nt" else b.shape[1]
    tm, tn, tk = _mm_tiles(mode, M, N, K, a.dtype.itemsize, b.dtype.itemsize)
    nk = K // tk
    a_spec = (pl.BlockSpec((tk, tm), lambda i, j, k: (k, i)) if mode == "tn"
              else pl.BlockSpec((tm, tk), lambda i, j, k: (i, k)))
    b_spec = (pl.BlockSpec((tn, tk), lambda i, j, k: (j, k)) if mode == "nt"
              else pl.BlockSpec((tk, tn), lambda i, j, k: (k, j)))
    dot = {"nn": _mm, "nt": _mm_nt, "tn": _mm_tn}[mode]
    has_res = res is not None

    def body(*refs):
        a_ref, b_ref = refs[0], refs[1]
        r_ref = refs[2] if has_res else None
        o_ref = refs[3] if has_res else refs[2]

        def finish(out):
            if has_res:
                out = out + r_ref[...].astype(F32)
            o_ref[...] = out.astype(o_ref.dtype)

        if nk == 1:
            finish(dot(a_ref[...], b_ref[...]))
            return
        acc = refs[-1]
        k = pl.program_id(2)

        @pl.when(k == 0)
        def _():
            acc[...] = jnp.zeros_like(acc)

        acc[...] += dot(a_ref[...], b_ref[...])

        @pl.when(k == nk - 1)
        def _():
            finish(acc[...])

    in_specs = [a_spec, b_spec]
    args = [a, b]
    if has_res:
        in_specs.append(pl.BlockSpec((tm, tn), lambda i, j, k: (i, j)))
        args.append(res)
    return pl.pallas_call(
        body, name=name, grid=(M // tm, N // tn, nk), in_specs=in_specs,
        out_specs=pl.BlockSpec((tm, tn), lambda i, j, k: (i, j)),
        out_shape=jax.ShapeDtypeStruct((M, N), out_dtype),
        scratch_shapes=[pltpu.VMEM((tm, tn), F32)] if nk > 1 else [],
        compiler_params=_cp(("parallel", "parallel", "arbitrary")),
    )(*args)


def call_with_comm(body, comm, *, name, grid, in_specs, out_specs, out_shape, scratch_shapes, semantics, args):
    if comm is None:
        outs = pl.pallas_call(body, name=name, grid=grid, in_specs=in_specs, out_specs=out_specs, out_shape=out_shape,
                              scratch_shapes=scratch_shapes, compiler_params=_cp(semantics))(*args)
        return list(outs), []
    n_in, n_out, n_scr = len(in_specs), len(out_specs), len(scratch_shapes)
    c_in, c_out = len(comm["ins"]), len(comm["out_shape"])
    total = int(np.prod(grid))
    mid_step = (2 * total) // 3

    def wrapped(*refs):
        p = 0
        ins = refs[p:p + n_in]
        p += n_in
        cins = refs[p:p + c_in]
        p += c_in
        outs = refs[p:p + n_out]
        p += n_out
        couts = refs[p:p + c_out]
        p += c_out
        scr = refs[p:p + n_scr]
        send, recv = refs[p + n_scr], refs[p + n_scr + 1]
        step = pl.program_id(0)
        for ax in range(1, len(grid)):
            step = step * grid[ax] + pl.program_id(ax)

        @pl.when(step == 0)
        def _():
            comm["start"](cins, couts, send, recv)

        body(*ins, *outs, *scr)
        if comm["mid"] is not None:
            @pl.when(step == mid_step)
            def _():
                comm["mid"](cins, couts, send, recv)

        @pl.when(step == total - 1)
        def _():
            comm["finish"](cins, couts, send, recv)

    outs = pl.pallas_call(
        wrapped, name=name, grid=grid, in_specs=list(in_specs) + [HBM] * c_in,
        out_specs=list(out_specs) + [HBM] * c_out, out_shape=list(out_shape) + list(comm["out_shape"]),
        scratch_shapes=list(scratch_shapes) + [pltpu.SemaphoreType.DMA((comm["n_sems"],))] * 2,
        compiler_params=_cp(("arbitrary",) * len(grid)),
    )(*args, *comm["ins"])
    return list(outs[:n_out]), list(outs[n_out:])


def _col(arr, width, idx):
    return (arr, width, idx)


def _perm(arr, dil, width=None, idx=0):
    return (arr, arr.shape[1] if width is None else width, idx, dil)


def _from_perm(ref, scr, dil):
    n, w = ref.shape[1], ref.shape[2]
    for r in range(dil):
        for j in range(w // LANE):
            scr[j, pl.ds(r, n, stride=dil), :] = ref[r, :, j * LANE:(j + 1) * LANE].astype(F32)
    return jnp.concatenate([scr[j] for j in range(w // LANE)], axis=1)


def _to_perm(val, ref, scr, dil):
    n, w = ref.shape[1], ref.shape[2]
    for j in range(w // LANE):
        scr[j] = val[:, j * LANE:(j + 1) * LANE].astype(F32)
    for r in range(dil):
        ref[r] = jnp.concatenate([scr[j, pl.ds(r, n, stride=dil), :] for j in range(w // LANE)], axis=1).astype(ref.dtype)


def rowcall(name, fn, rows, bcs, row_outs, bc_outs=(), tb=256, halo=()):
    rows = [r if isinstance(r, tuple) else (r, r.shape[1], 0) for r in rows]
    rows = [r if len(r) == 4 else r + (1,) for r in rows]
    row_outs = [o if len(o) == 3 else o + (1,) for o in row_outs]
    S = rows[0][0].shape[0]
    tb = min(tb, S)
    nb = S // tb
    n_r, n_h, n_b, n_ro, n_bo = len(rows), len(halo), len(bcs), len(row_outs), len(bc_outs)
    hb = tb // SUBLANE
    last = S // SUBLANE - 1
    perm_w = max([w for (_, w, _, d) in rows if d > 1] + [w for (w, _, d) in row_outs if d > 1] + [0])

    def body(*refs):
        i = pl.program_id(0)
        scr = refs[-1] if perm_w else None
        pos = 0
        r_in = [r[...] if d == 1 else _from_perm(r, scr, d) for r, (_, _, _, d) in zip(refs[pos:pos + n_r], rows)]
        pos += n_r
        h_in = []
        for _ in range(n_h):
            prev = refs[pos][...] * (i > 0).astype(F32)
            nxt = refs[pos + 1][...] * (i < nb - 1).astype(F32)
            h_in += [prev, nxt]
            pos += 2
        b_in = [r[...] for r in refs[pos:pos + n_b]]
        pos += n_b
        ro = refs[pos:pos + n_ro]
        bo = refs[pos + n_ro:pos + n_ro + n_bo]
        outs_r, outs_b = fn(*r_in, *h_in, *b_in)
        for ref, val, (_, _, d) in zip(ro, outs_r, row_outs, strict=True):
            if d == 1:
                ref[...] = val.astype(ref.dtype)
            else:
                _to_perm(val, ref, scr, d)
        if n_bo:
            @pl.when(i == 0)
            def _():
                for ref in bo:
                    ref[...] = jnp.zeros_like(ref)

            for ref, val in zip(bo, outs_b, strict=True):
                ref[...] += val

    in_specs, args = [], []
    for (a, w, c, d) in rows:
        if d == 1:
            in_specs.append(pl.BlockSpec((tb, w), functools.partial(lambda i, c: (i, c), c=c)))
            args.append(a)
        else:
            in_specs.append(pl.BlockSpec((d, tb // d, w), functools.partial(lambda i, c: (0, i, c), c=c)))
            args.append(a.reshape(d, S // d, a.shape[1]))
    for h in halo:
        a, w, c, _ = rows[h]
        in_specs.append(pl.BlockSpec((SUBLANE, w), functools.partial(
            lambda i, c: (jnp.maximum(i * hb - 1, 0), c), c=c)))
        in_specs.append(pl.BlockSpec((SUBLANE, w), functools.partial(
            lambda i, c: (jnp.minimum((i + 1) * hb, last), c), c=c)))
        args += [a, a]
    for b in bcs:
        in_specs.append(pl.BlockSpec(b.shape, _const_map(b.ndim)))
        args.append(b)
    out_specs, out_shape = [], []
    for (w, dt, d) in row_outs:
        if d == 1:
            out_specs.append(pl.BlockSpec((tb, w), lambda i: (i, 0)))
            out_shape.append(jax.ShapeDtypeStruct((S, w), dt))
        else:
            out_specs.append(pl.BlockSpec((d, tb // d, w), lambda i: (0, i, 0)))
            out_shape.append(jax.ShapeDtypeStruct((d, S // d, w), dt))
    for shp in bc_outs:
        out_specs.append(pl.BlockSpec(shp, _const_map(len(shp))))
        out_shape.append(jax.ShapeDtypeStruct(shp, F32))
    outs = pl.pallas_call(
        body, name=name, grid=(nb,), in_specs=in_specs, out_specs=out_specs, out_shape=out_shape,
        scratch_shapes=[pltpu.VMEM((perm_w // LANE, tb, LANE), F32)] if perm_w else [],
        compiler_params=_cp(("arbitrary",) if n_bo else ("parallel",)),
    )(*args)
    row_res = [o if d == 1 else o.reshape(S, w) for o, (w, _, d) in zip(outs[:n_ro], row_outs)]
    return row_res, list(outs[n_ro:])


def smallcall(name, fn, ins, out_shapes):
    n_in = len(ins)

    def body(*refs):
        outs = fn(*[r[...] for r in refs[:n_in]])
        for ref, val in zip(refs[n_in:], outs, strict=True):
            ref[...] = val.astype(ref.dtype)

    return pl.pallas_call(
        body, name=name, out_shape=[jax.ShapeDtypeStruct(s, F32) for s in out_shapes],
        compiler_params=_cp(),
    )(*ins)


def _rms(x, g):
    return x * lax.rsqrt(jnp.mean(x * x, axis=-1, keepdims=True) + EPS) * g


def _rms_groups(y, g, width):
    outs = []
    for j in range(y.shape[1] // width):
        sl = slice(j * width, (j + 1) * width)
        outs.append(_rms(y[:, sl], g[:, sl]))
    return jnp.concatenate(outs, axis=1)


def norm_fwd(name, x, g, dils=(1,)):
    outs, _ = rowcall(name, lambda x, g: ((_rms(x, g),) * len(dils), ()), [x], [g],
                      [(D_MODEL, BF16, d) for d in dils], tb=512)
    return outs[0] if len(dils) == 1 else tuple(outs)


def norm_bwd(name, x, g, dhn, dres, dils=(1,)):
    parts = dhn if isinstance(dhn, tuple) else (dhn,)
    n = len(parts)

    def fn(x, *rest):
        dh = functools.reduce(lambda a, b: a + b, rest[:n])
        _, vjp = jax.vjp(_rms, x, rest[n + 1])
        dx, dg = vjp(dh)
        return (dx + rest[n],), (dg,)

    rows = [x] + [a if d == 1 else _perm(a, d) for a, d in zip(parts, dils)] + [dres]
    (dx,), (dg,) = rowcall(name, fn, rows, [g], [(D_MODEL, F32)], [(1, D_MODEL)], tb=512)
    return dx, dg


def loss_head(x, tgt, g):
    def fn(x, tgt, g):
        y, vjp = jax.vjp(_rms, x, g)
        diff = y - tgt
        loss = 0.5 * jnp.sum(jnp.mean(diff * diff, axis=-1, keepdims=True), axis=0, keepdims=True)
        dx, dg = vjp(diff * (1.0 / D_MODEL))
        return (dx,), (jnp.broadcast_to(loss, (1, LANE)), dg)

    (dx,), (loss, dg) = rowcall("loss_head", fn, [x, tgt], [g], [(D_MODEL, F32)],
                                [(1, LANE), (1, D_MODEL)], tb=512)
    return loss, dx, dg


def _shift_rows(x, s):
    if s == 0:
        return x
    return pltpu.roll(x, (-s) % x.shape[0], 0)


def _conv_ext(x, prev, nxt, w):
    xe = jnp.concatenate([prev, x, nxt], axis=0)
    pad = SSD_CONV // 2
    c = jnp.zeros_like(xe)
    for k in range(SSD_CONV):
        c = c + w[k:k + 1, :] * _shift_rows(xe, k - pad)
    return xe, c


def ssd_conv_fwd(u, conv_w, conv_b):
    def fn(x0, x1, x2, p0, n0, p1, n1, p2, n2, w, b):
        tb = x0.shape[0]
        outs = []
        for j, (x, p, n) in enumerate(((x0, p0, n0), (x1, p1, n1), (x2, p2, n2))):
            sl = slice(j * 1024, (j + 1) * 1024)
            _, c = _conv_ext(x, p, n, w[:, sl])
            outs.append(_silu(c[SUBLANE:SUBLANE + tb] + b[:, sl]))
        return (jnp.concatenate(outs, axis=1),), ()

    (xbc,), _ = rowcall("ssd_conv_fwd", fn, [_col(u, 1024, 2), _col(u, 1024, 3), _col(u, 1024, 4)],
                        [conv_w, conv_b], [(SSD_CONV_CH, F32)], tb=256, halo=(0, 1, 2))
    return xbc


def ssd_conv_bwd(u, dxbc, dz, conv_w, conv_b):
    pad = SSD_CONV // 2

    def fn(x0, x1, x2, g0, g1, g2, dz, xp0, xn0, xp1, xn1, xp2, xn2, gp0, gn0, gp1, gn1, gp2, gn2, w, b):
        tb = x0.shape[0]
        blk = slice(SUBLANE, SUBLANE + tb)
        dpre, dws, dbs = [], [], []
        xs = ((x0, xp0, xn0), (x1, xp1, xn1), (x2, xp2, xn2))
        gs = ((g0, gp0, gn0), (g1, gp1, gn1), (g2, gp2, gn2))
        for j in range(3):
            sl = slice(j * 1024, (j + 1) * 1024)
            wj = w[:, sl]
            xe, c = _conv_ext(*xs[j], wj)
            ce = c + b[:, sl]
            sig = jax.nn.sigmoid(ce)
            ge = jnp.concatenate([gs[j][1], gs[j][0], gs[j][2]], axis=0)
            dce = ge * (sig * (1.0 + ce * (1.0 - sig)))
            dx = jnp.zeros_like(xe)
            dw_rows = []
            for k in range(SSD_CONV):
                dx = dx + wj[k:k + 1, :] * _shift_rows(dce, pad - k)
                dw_rows.append(jnp.sum(dce[blk] * _shift_rows(xe, k - pad)[blk], axis=0, keepdims=True))
            dw_rows.append(jnp.zeros_like(dw_rows[0]))
            dpre.append(dx[blk])
            dws.append(jnp.concatenate(dw_rows, axis=0))
            dbs.append(jnp.sum(dce[blk], axis=0, keepdims=True))
        du = jnp.concatenate([dz] + dpre, axis=1)
        return (du,), (jnp.concatenate(dws, axis=1), jnp.concatenate(dbs, axis=1))

    rows = [_col(u, 1024, 2), _col(u, 1024, 3), _col(u, 1024, 4),
            _col(dxbc, 1024, 0), _col(dxbc, 1024, 1), _col(dxbc, 1024, 2), dz]
    (du,), (dw, db) = rowcall("ssd_conv_bwd", fn, rows, [conv_w, conv_b], [(SSD_MAIN, BF16)],
                              [(SUBLANE, SSD_CONV_CH), (1, SSD_CONV_CH)], tb=128, halo=(0, 1, 2, 3, 4, 5))
    return du, dw, db


def _expand_heads(v):
    return jnp.concatenate([jnp.broadcast_to(v[:, j:j + 1], (v.shape[0], SSD_HEADDIM)) for j in range(SSD_HPG)], axis=1)


def _ssd_chunk(rev, st_in, xs, udt, dtb, alog, B, C):
    Q = B.shape[0]
    P = SSD_HEADDIM
    dt = _softplus(udt + dtb)
    a = dt * (-jnp.exp(alog))
    r = lax.broadcasted_iota(jnp.int32, (Q, Q), 0)
    c = lax.broadcasted_iota(jnp.int32, (Q, Q), 1)
    mask = (r <= c) if rev else (r >= c)
    p = _mm_tri(mask, a)
    pT = p.T
    p_e = _expand_heads(p)
    tot_e = p_e[0:1] if rev else p_e[Q - 1:Q]
    xdt = xs * _expand_heads(dt)
    CB = _mm_nt(C, B)
    ys = []
    for j in range(SSD_HPG):
        L = jnp.exp(jnp.where(mask, p[:, j:j + 1] - pT[j:j + 1, :], NEG_BIG))
        ys.append(_mm(CB * L, xdt[:, j * P:(j + 1) * P]))
    y = jnp.concatenate(ys, axis=1) + _mm(C, st_in) * jnp.exp(p_e)
    st_out = st_in * jnp.exp(tot_e) + _mm_tn(B, xdt * jnp.exp(tot_e - p_e))
    return y, st_out


def _ssd_specs(nc, rev_order):
    Q = SSD_CHUNK
    N, P, H, GS = SSD_STATE, SSD_HEADDIM, SSD_HPG, SSD_GPS
    gw = H * P
    nbc = SSD_GROUPS // GS

    def cidx(s):
        return nc - 1 - s if rev_order else s

    xs = pl.BlockSpec((Q, GS * gw), lambda g, s: (cidx(s), g))
    Bs = pl.BlockSpec((Q, GS * N), lambda g, s: (cidx(s), SSD_DI // (GS * N) + g))
    Cs = pl.BlockSpec((Q, GS * N), lambda g, s: (cidx(s), SSD_DI // (GS * N) + nbc + g))
    BC_out = pl.BlockSpec((Q, GS * N), lambda g, s: (cidx(s), g))
    udt = pl.BlockSpec((GS, Q, H), lambda g, s: (g, cidx(s), 0))
    small = pl.BlockSpec((GS, 1, H), lambda g, s: (g, 0, 0))
    st = pl.BlockSpec((GS, 1, N, gw), lambda g, s: (g, cidx(s), 0, 0))
    return xs, Bs, Cs, BC_out, udt, small, st


def ssd_scan_fwd(name, xbc, udt, dtb, alog, rev, comm=None):
    S = xbc.shape[0]
    Q, N, P, H, GS = SSD_CHUNK, SSD_STATE, SSD_HEADDIM, SSD_HPG, SSD_GPS
    gw = H * P
    nc = S // Q
    xs_s, B_s, C_s, _, udt_s, small_s, st_s = _ssd_specs(nc, rev)

    def body(xs_ref, B_ref, C_ref, udt_ref, dtb_ref, alog_ref, y_ref, st_ref, state):
        @pl.when(pl.program_id(1) == 0)
        def _():
            state[...] = jnp.zeros_like(state)

        for g in range(GS):
            st_ref[g, 0] = state[g]
            y, st_out = _ssd_chunk(rev, state[g], xs_ref[:, g * gw:(g + 1) * gw], udt_ref[g], dtb_ref[g], alog_ref[g],
                                   B_ref[:, g * N:(g + 1) * N], C_ref[:, g * N:(g + 1) * N])
            y_ref[:, g * gw:(g + 1) * gw] = y
            state[g] = st_out

    (y, st), got = call_with_comm(
        body, comm, name=name, grid=(SSD_GROUPS // GS, nc),
        in_specs=[xs_s, B_s, C_s, udt_s, small_s, small_s],
        out_specs=[xs_s, st_s],
        out_shape=[jax.ShapeDtypeStruct((S, SSD_DI), F32),
                   jax.ShapeDtypeStruct((SSD_GROUPS, nc, N, gw), F32)],
        scratch_shapes=[pltpu.VMEM((GS, N, gw), F32)],
        semantics=("parallel", "arbitrary"), args=(xbc, xbc, xbc, udt, dtb, alog))
    return y, st, got


def ssd_scan_bwd(name, xbc, udt, dtb, alog, states, dy, rev):
    S = xbc.shape[0]
    Q, N, P, H, GS = SSD_CHUNK, SSD_STATE, SSD_HEADDIM, SSD_HPG, SSD_GPS
    gw = H * P
    nc = S // Q
    xs_s, B_s, C_s, BC_out, udt_s, small_s, st_s = _ssd_specs(nc, not rev)

    def body(xs_ref, B_ref, C_ref, udt_ref, dtb_ref, alog_ref, st_ref, dy_ref,
             dx_ref, dB_ref, dC_ref, dudt_ref, ddtb_ref, dalog_ref, dstate):
        @pl.when(pl.program_id(1) == 0)
        def _():
            dstate[...] = jnp.zeros_like(dstate)
            ddtb_ref[...] = jnp.zeros_like(ddtb_ref)
            dalog_ref[...] = jnp.zeros_like(dalog_ref)

        for g in range(GS):
            cols, bc = slice(g * gw, (g + 1) * gw), slice(g * N, (g + 1) * N)
            _, vjp = jax.vjp(functools.partial(_ssd_chunk, rev), st_ref[g, 0], xs_ref[:, cols], udt_ref[g], dtb_ref[g],
                             alog_ref[g], B_ref[:, bc], C_ref[:, bc])
            dst_in, dxs, dudt, ddtb, dalog, dB, dC = vjp((dy_ref[:, cols], dstate[g]))
            dx_ref[:, cols] = dxs
            dB_ref[:, bc] = dB
            dC_ref[:, bc] = dC
            dudt_ref[g] = dudt
            ddtb_ref[g] += ddtb
            dalog_ref[g] += dalog
            dstate[g] = dst_in

    return pl.pallas_call(
        body, name=name, grid=(SSD_GROUPS // GS, nc),
        in_specs=[xs_s, B_s, C_s, udt_s, small_s, small_s, st_s, xs_s],
        out_specs=[xs_s, BC_out, BC_out, udt_s, small_s, small_s],
        out_shape=[jax.ShapeDtypeStruct((S, SSD_DI), F32),
                   jax.ShapeDtypeStruct((S, SSD_GROUPS * N), F32),
                   jax.ShapeDtypeStruct((S, SSD_GROUPS * N), F32),
                   jax.ShapeDtypeStruct((SSD_GROUPS, S, H), F32),
                   jax.ShapeDtypeStruct((SSD_GROUPS, 1, H), F32),
                   jax.ShapeDtypeStruct((SSD_GROUPS, 1, H), F32)],
        scratch_shapes=[pltpu.VMEM((GS, N, gw), F32)],
        compiler_params=_cp(("parallel", "arbitrary")),
    )(xbc, xbc, xbc, udt, dtb, alog, states, dy)


def _ssd_combine(yf, yb, xs, z, dexp, ng):
    y = (yf + yb + xs * dexp) * _silu(z)
    return _rms_groups(y, ng, SSD_DI // SSD_GROUPS)


def ssd_forward(x, hn, w, comm=None):
    comm = comm or {}
    S = x.shape[0]
    u = matmul("ssd_in", hn, w["ssd_w_main"])
    udt = matmul("ssd_in_dt", hn, w["ssd_w_dt"])
    xbc = ssd_conv_fwd(u, w["ssd_conv_w8"], w["ssd_conv_b"])
    udt_t = udt.reshape(S, 2, SSD_GROUPS, SSD_HPG).transpose(1, 2, 0, 3)
    dtb = w["ssd_dt_bias"].reshape(2, SSD_GROUPS, 1, SSD_HPG)
    alog = w["ssd_a_log"].reshape(2, SSD_GROUPS, 1, SSD_HPG)
    yf, stf, got_f = ssd_scan_fwd("ssd_scan_f", xbc, udt_t[0], dtb[0], alog[0], False, comm.get("ssd_scan_f"))
    yb, stb, got_b = ssd_scan_fwd("ssd_scan_b", xbc, udt_t[1], dtb[1], alog[1], True, comm.get("ssd_scan_b"))
    dexp = jnp.repeat(w["ssd_d"].reshape(1, SSD_HEADS), SSD_HEADDIM, axis=1)
    (yn,), _ = rowcall("ssd_combine", lambda yf, yb, xs, z, d, g: ((_ssd_combine(yf, yb, xs, z, d, g),), ()),
                       [yf, yb, _col(xbc, SSD_DI, 0), _col(u, SSD_DI, 0)], [dexp, w["ssd_norm_g"]],
                       [(SSD_DI, BF16)], tb=256)
    out = matmul("ssd_out", yn, w["ssd_w_out"], res=x)
    saved = dict(hn=hn, u=u, xbc=xbc, udt_t=udt_t, dtb=dtb, alog=alog, yf=yf, yb=yb, stf=stf, stb=stb,
                 dexp=dexp, yn=yn, got=dict(ssd_scan_f=got_f, ssd_scan_b=got_b))
    return out, saved


def ssd_backward(dy, sv, w, comm=None):
    S = dy.shape[0]
    u, xbc = sv["u"], sv["xbc"]
    dyn = matmul("ssd_out_dx", dy, w["ssd_w_out"], mode="nt")
    g_w_out = matmul("ssd_out_dw", sv["yn"], dy, mode="tn")

    def comb_bwd(yf, yb, xs, z, dyn, dexp, ng):
        _, vjp = jax.vjp(_ssd_combine, yf, yb, xs, z, dexp, ng)
        dyf, _, dxs, dz, ddexp, dng = vjp(dyn)
        return (dyf, dxs, dz), (ddexp, dng)

    (dyc, dskip, dz), (ddexp, g_norm) = rowcall(
        "ssd_combine_bwd", comb_bwd, [sv["yf"], sv["yb"], _col(xbc, SSD_DI, 0), _col(u, SSD_DI, 0), dyn],
        [sv["dexp"], w["ssd_norm_g"]], [(SSD_DI, F32)] * 3, [(1, SSD_DI), (1, SSD_DI)], tb=256)
    udt_t, dtb, alog = sv["udt_t"], sv["dtb"], sv["alog"]
    dxf, dBf, dCf, dudt_f, ddtb_f, dalog_f = ssd_scan_bwd("ssd_scan_f_bwd", xbc, udt_t[0], dtb[0], alog[0],
                                                          sv["stf"], dyc, False)
    dxb, dBb, dCb, dudt_b, ddtb_b, dalog_b = ssd_scan_bwd("ssd_scan_b_bwd", xbc, udt_t[1], dtb[1], alog[1],
                                                          sv["stb"], dyc, True)

    def gather(dxf, dxb, dskip, dBf, dBb, dCf, dCb):
        return (jnp.concatenate([dxf + dxb + dskip, dBf + dBb, dCf + dCb], axis=1),), ()

    (dxbc,), _ = rowcall("ssd_dxbc", gather, [dxf, dxb, dskip, dBf, dBb, dCf, dCb], [], [(SSD_CONV_CH, F32)], tb=256)
    du, g_conv_w8, g_conv_b = ssd_conv_bwd(u, dxbc, dz, w["ssd_conv_w8"], w["ssd_conv_b"])
    dudt = jnp.stack([dudt_f, dudt_b]).transpose(2, 0, 1, 3).reshape(S, 2 * SSD_HEADS)
    hn = sv["hn"]
    g_main = matmul("ssd_in_dw", hn, du, mode="tn")
    g_dt = matmul("ssd_in_dt_dw", hn, dudt, mode="tn")
    dhn = matmul("ssd_in_dt_dx", dudt, w["ssd_w_dt"], mode="nt")
    dhn = matmul("ssd_in_dx", du, w["ssd_w_main"], mode="nt", res=dhn)
    grads = dict(
        ssd_w_in=jnp.concatenate([g_main, g_dt], axis=1)[None],
        ssd_conv_w=g_conv_w8[None, :SSD_CONV],
        ssd_conv_b=g_conv_b,
        ssd_dt_bias=jnp.stack([ddtb_f, ddtb_b]).reshape(1, 2, SSD_HEADS),
        ssd_a_log=jnp.stack([dalog_f, dalog_b]).reshape(1, 2, SSD_HEADS),
        ssd_d_exp=ddexp,
        ssd_norm_g=g_norm,
        ssd_w_out=g_w_out[None],
    )
    return dhn, grads


def _hg_block(rev, stTs, uq, uf, ui, lb):
    C = HG_CHUNK
    n = uq.shape[0] // C
    nh = uq.shape[1] // HG_EXPAND
    stTs = list(stTs)
    q = _silu(uq)
    f = lb + (1.0 - lb) * jax.nn.sigmoid(uf)
    k = 1.0 - f
    g = jnp.log(f)
    r = lax.broadcasted_iota(jnp.int32, (C, C), 0)
    c = lax.broadcasted_iota(jnp.int32, (C, C), 1)
    mask = (r <= c) if rev else (r >= c)
    Tm = mask.astype(F32)
    outs = [[None] * n for _ in range(nh)]
    for i in (reversed(range(n)) if rev else range(n)):
        sl = slice(i * C, (i + 1) * C)
        qi, ki, vi = q[sl], k[sl], ui[sl]
        G = _mm_tri(mask, g[sl])
        Gr = G[C // 2:C // 2 + 1]
        Gl = G[0:1] if rev else G[C - 1:C]
        q_in, k_in = qi * jnp.exp(G - Gr), ki * jnp.exp(Gr - G)
        q_st, k_st, e_l = qi * jnp.exp(G), ki * jnp.exp(Gl - G), jnp.exp(Gl)
        for h in range(nh):
            cs = slice(h * HG_EXPAND, (h + 1) * HG_EXPAND)
            att = jnp.where(mask, _mm_nt(q_in[:, cs], k_in[:, cs]), 0.0)
            outs[h][i] = _mm(att, vi[:, cs]) + _mm_nt(q_st[:, cs], stTs[h])
            stTs[h] = stTs[h] * e_l[:, cs] + _mm_tn(vi[:, cs], k_st[:, cs])
    o = jnp.concatenate([jnp.concatenate(outs[h], axis=0) for h in range(nh)], axis=1)
    return o, stTs


def _hg_specs(nb, rev_order, f_col):
    R = HG_ROWS
    gw = HG_HPS * HG_EXPAND
    ng = HG_HEADS // HG_HPS

    def bidx(s):
        return nb - 1 - s if rev_order else s

    def col(base):
        return pl.BlockSpec((R, gw), lambda h, s: (bidx(s), base * ng + h))

    out = pl.BlockSpec((R, gw), lambda h, s: (bidx(s), h))
    lb = pl.BlockSpec((1, gw), lambda h, s: (0, h))
    st = pl.BlockSpec((1, 1, HG_HPS, HG_EXPAND, HG_EXPAND), lambda h, s: (h, bidx(s), 0, 0, 0))
    return col(0), col(f_col), col(3), out, lb, st


def hg_scan_fwd(name, u, lb, rev, comm=None):
    S = u.shape[0]
    nb = S // HG_ROWS
    ng = HG_HEADS // HG_HPS
    q_s, f_s, i_s, o_s, lb_s, st_s = _hg_specs(nb, rev, 2 if rev else 1)

    def body(uq, uf, ui, lb_ref, o_ref, st_ref, state):
        @pl.when(pl.program_id(1) == 0)
        def _():
            state[...] = jnp.zeros_like(state)

        st_ref[0, 0] = state[...]
        o, st = _hg_block(rev, [state[h] for h in range(HG_HPS)], uq[...], uf[...], ui[...], lb_ref[...])
        o_ref[...] = o
        for h in range(HG_HPS):
            state[h] = st[h]

    (o, st), got = call_with_comm(
        body, comm, name=name, grid=(ng, nb), in_specs=[q_s, f_s, i_s, lb_s], out_specs=[o_s, st_s],
        out_shape=[jax.ShapeDtypeStruct((S, HG_W), F32),
                   jax.ShapeDtypeStruct((ng, nb, HG_HPS, HG_EXPAND, HG_EXPAND), F32)],
        scratch_shapes=[pltpu.VMEM((HG_HPS, HG_EXPAND, HG_EXPAND), F32)],
        semantics=("parallel", "arbitrary"), args=(u, u, u, lb))
    return o, st, got


def hg_scan_bwd(name, u, lb, states, do, rev, comm=None):
    S = u.shape[0]
    nb = S // HG_ROWS
    ng = HG_HEADS // HG_HPS
    q_s, f_s, i_s, o_s, lb_s, st_s = _hg_specs(nb, not rev, 2 if rev else 1)

    def body(uq, uf, ui, lb_ref, st_ref, do_ref, dq_ref, df_ref, di_ref, dlb_ref, dstate):
        @pl.when(pl.program_id(1) == 0)
        def _():
            dstate[...] = jnp.zeros_like(dstate)
            dlb_ref[...] = jnp.zeros_like(dlb_ref)

        _, vjp = jax.vjp(functools.partial(_hg_block, rev), [st_ref[0, 0, h] for h in range(HG_HPS)],
                         uq[...], uf[...], ui[...], lb_ref[...])
        dst, dq, df, di, dlb = vjp((do_ref[...], [dstate[h] for h in range(HG_HPS)]))
        dq_ref[...] = dq
        df_ref[...] = df
        di_ref[...] = di
        dlb_ref[...] += dlb
        for h in range(HG_HPS):
            dstate[h] = dst[h]

    outs, got = call_with_comm(
        body, comm, name=name, grid=(ng, nb), in_specs=[q_s, f_s, i_s, lb_s, st_s, o_s],
        out_specs=[o_s, o_s, o_s, lb_s],
        out_shape=[jax.ShapeDtypeStruct((S, HG_W), F32)] * 3 + [jax.ShapeDtypeStruct((1, HG_W), F32)],
        scratch_shapes=[pltpu.VMEM((HG_HPS, HG_EXPAND, HG_EXPAND), F32)],
        semantics=("parallel", "arbitrary"), args=(u, u, u, lb, states, do))
    return (*outs, got)


def _hg_lb(hgrn_lb, layer):
    m = jnp.max(hgrn_lb, axis=0, keepdims=True)
    e = jnp.exp(hgrn_lb - m)
    sm = e / jnp.sum(e, axis=0, keepdims=True)
    lb = jnp.zeros_like(sm[0:1])
    for i in range(1, layer + 1):
        lb = lb + sm[i:i + 1]
    return lb


def _hg_combine(of, ob, gate, ng):
    return _rms_groups(of + ob, ng, HG_EXPAND) * _silu(gate)


def hg_forward(x, hn, w, layer, comm=None):
    comm = comm or {}
    u = matmul("hg_in", hn, w["hg_w_in"])
    (lb,) = smallcall("hg_lb", lambda t: (_hg_lb(t, layer),), [w["hgrn_lb"]], [(1, HG_W)])
    of, stf, got_f = hg_scan_fwd("hg_scan_f", u, lb, False, comm.get("hg_scan_f"))
    ob, stb, _ = hg_scan_fwd("hg_scan_b", u, lb, True)
    (og,), _ = rowcall("hg_combine", lambda of, ob, gate, ng: ((_hg_combine(of, ob, gate, ng),), ()),
                       [of, ob, _col(u, HG_W, 4)], [w["hg_norm_g"]], [(HG_W, BF16)], tb=256)
    out = matmul("hg_out", og, w["hg_w_out"], res=x)
    return out, dict(hn=hn, u=u, lb=lb, of=of, ob=ob, stf=stf, stb=stb, og=og, got=dict(hg_scan_f=got_f))


def hg_backward(dy, sv, w, layer, comm=None):
    comm = comm or {}
    u, lb = sv["u"], sv["lb"]
    dog = matmul("hg_out_dx", dy, w["hg_w_out"], mode="nt")
    g_w_out = matmul("hg_out_dw", sv["og"], dy, mode="tn")

    def comb_bwd(of, ob, gate, dog, ng):
        _, vjp = jax.vjp(_hg_combine, of, ob, gate, ng)
        dof, _, dgate, dng = vjp(dog)
        return (dof, dgate), (dng,)

    (do, dgate), (g_norm,) = rowcall("hg_combine_bwd", comb_bwd, [sv["of"], sv["ob"], _col(u, HG_W, 4), dog],
                                     [w["hg_norm_g"]], [(HG_W, F32)] * 2, [(1, HG_W)], tb=256)
    dqf, dff, dif, dlbf, got_f = hg_scan_bwd("hg_scan_f_bwd", u, lb, sv["stf"], do, False, comm.get("hg_scan_f_bwd"))
    dqb, dfb, dib, dlbb, _ = hg_scan_bwd("hg_scan_b_bwd", u, lb, sv["stb"], do, True)

    def gather(dqf, dqb, dff, dfb, dif, dib, dgate):
        return (jnp.concatenate([dqf + dqb, dff, dfb, dif + dib, dgate], axis=1),), ()

    (du,), _ = rowcall("hg_du", gather, [dqf, dqb, dff, dfb, dif, dib, dgate], [], [(HG_IN, BF16)], tb=256)

    def lb_bwd(t, dlbf, dlbb):
        _, vjp = jax.vjp(lambda t: _hg_lb(t, layer), t)
        return vjp(dlbf + dlbb)

    (g_lb,) = smallcall("hg_lb_bwd", lb_bwd, [w["hgrn_lb"], dlbf, dlbb], [(DEPTH, HG_W)])
    hn = sv["hn"]
    g_w_in = matmul("hg_in_dw", hn, du, mode="tn")
    dhn = matmul("hg_in_dx", du, w["hg_w_in"], mode="nt")
    return dhn, dict(hg_w_in=g_w_in[None], hg_norm_g=g_norm, hg_w_out=g_w_out[None], hgrn_lb=g_lb,
                     got=dict(hg_scan_f_bwd=got_f))


def _rope_tables(S):
    t = np.arange(S)
    row = (t // GRID_W).astype(np.float32)
    col = (t % GRID_W).astype(np.float32)
    inv = (ROPE_THETA ** (-np.arange(0, ROPE_AXIS, 2, dtype=np.float32) / ROPE_AXIS)).astype(np.float32)
    ar = jnp.asarray(row)[:, None] * jnp.asarray(inv)[None, :]
    ac = jnp.asarray(col)[:, None] * jnp.asarray(inv)[None, :]
    cos = jnp.concatenate([jnp.cos(ar), jnp.cos(ar), jnp.cos(ac), jnp.cos(ac)], axis=1)
    sin = jnp.concatenate([-jnp.sin(ar), jnp.sin(ar), -jnp.sin(ac), jnp.sin(ac)], axis=1)
    return cos.astype(F32), sin.astype(F32)


def _rope(x, cos, sin):
    h = ROPE_AXIS // 2
    sw = jnp.concatenate([x[:, h:2 * h], x[:, 0:h], x[:, 3 * h:4 * h], x[:, 2 * h:3 * h]], axis=1)
    return x * cos + sw * sin


def _at_pre(uq, uk, cos, sin, qg, kg):
    qs, ks = [], []
    for h in range(AT_HEADS):
        qs.append(_rope(_rms(uq[:, h * AT_HD:(h + 1) * AT_HD], qg), cos, sin) * (AT_HD ** -0.5))
    for h in range(AT_KV):
        ks.append(_rope(_rms(uk[:, h * AT_HD:(h + 1) * AT_HD], kg), cos, sin))
    return jnp.concatenate(qs, axis=1), jnp.concatenate(ks, axis=1)


def _stack_heads(x):
    return jnp.concatenate([x[:, :AT_HD], x[:, AT_HD:]], axis=0)


def _unstack_heads(x):
    t = x.shape[0] // 2
    return jnp.concatenate([x[:t], x[t:]], axis=1)


def at_flash_fwd(q, k, u):
    S = q.shape[0]
    tq, tk = _pick(S, 512), _pick(S, 4096)
    nq, nk = S // tq, S // tk
    gw = AT_GRP * AT_HD

    def body(q_ref, k_ref, v_ref, o_ref, lse_ref, m_s, l_s, acc):
        j = pl.program_id(2)

        @pl.when(j == 0)
        def _():
            m_s[...] = jnp.full_like(m_s, NEG_BIG)
            l_s[...] = jnp.zeros_like(l_s)
            acc[...] = jnp.zeros_like(acc)

        s = _mm_nt(_stack_heads(q_ref[...]), k_ref[...])
        m_new = jnp.maximum(m_s[...], jnp.max(s, axis=-1, keepdims=True))
        alpha = jnp.exp(m_s[...] - m_new)
        p = jnp.exp(s - m_new)
        l_s[...] = alpha * l_s[...] + jnp.sum(p, axis=-1, keepdims=True)
        acc[...] = alpha * acc[...] + _mm(p, v_ref[...])
        m_s[...] = m_new

        @pl.when(j == nk - 1)
        def _():
            o_ref[...] = _unstack_heads(acc[...] / l_s[...])
            lse_ref[0, 0] = m_s[...] + jnp.log(l_s[...])

    return pl.pallas_call(
        body, name="at_flash_fwd", grid=(AT_KV, nq, nk),
        in_specs=[pl.BlockSpec((tq, gw), lambda h, i, j: (i, h)),
                  pl.BlockSpec((tk, AT_HD), lambda h, i, j: (j, h)),
                  pl.BlockSpec((tk, AT_HD), lambda h, i, j: (j, (AT_QW + AT_KW) // AT_HD + h))],
        out_specs=[pl.BlockSpec((tq, gw), lambda h, i, j: (i, h)),
                   pl.BlockSpec((1, 1, 2 * tq, 1), lambda h, i, j: (h, i, 0, 0))],
        out_shape=[jax.ShapeDtypeStruct((S, AT_QW), F32), jax.ShapeDtypeStruct((AT_KV, nq, 2 * tq, 1), F32)],
        scratch_shapes=[pltpu.VMEM((2 * tq, 1), F32), pltpu.VMEM((2 * tq, 1), F32), pltpu.VMEM((2 * tq, AT_HD), F32)],
        compiler_params=_cp(("parallel", "parallel", "arbitrary")),
    )(q, k, u)


def at_flash_bwd(q, k, u, o, lse, do):
    S = q.shape[0]
    tq, tk = _pick(S, 512), _pick(S, 1024)
    nq, nk = S // tq, S // tk
    gw = AT_GRP * AT_HD

    def body(q_ref, k_ref, v_ref, o_ref, lse_ref, do_ref, dq_ref, dk_ref, dv_ref, dk_acc, dv_acc):
        j, i = pl.program_id(1), pl.program_id(2)

        @pl.when(i == 0)
        def _():
            dk_acc[...] = jnp.zeros_like(dk_acc)
            dv_acc[...] = jnp.zeros_like(dv_acc)

        q2 = _stack_heads(q_ref[...])
        do_blk = do_ref[...]
        do2 = _stack_heads(do_blk)
        delta = _stack_heads(do_blk * o_ref[...])
        delta = jnp.sum(delta, axis=-1, keepdims=True)
        kb, vb = k_ref[...], v_ref[...]
        p = jnp.exp(_mm_nt(q2, kb) - lse_ref[0, 0])
        dv_acc[...] += _mm_tn(p, do2)
        ds = p * (_mm_nt(do2, vb) - delta)
        dk_acc[...] += _mm_tn(ds, q2)
        dq = _unstack_heads(_mm(ds, kb))
        rows = pl.ds(pl.multiple_of(i * tq, tq), tq)

        @pl.when(j == 0)
        def _():
            dq_ref[rows, :] = dq

        @pl.when(j > 0)
        def _():
            dq_ref[rows, :] += dq

        @pl.when(i == nq - 1)
        def _():
            dk_ref[...] = dk_acc[...]
            dv_ref[...] = dv_acc[...]

    return pl.pallas_call(
        body, name="at_flash_bwd", grid=(AT_KV, nk, nq),
        in_specs=[pl.BlockSpec((tq, gw), lambda h, j, i: (i, h)),
                  pl.BlockSpec((tk, AT_HD), lambda h, j, i: (j, h)),
                  pl.BlockSpec((tk, AT_HD), lambda h, j, i: (j, (AT_QW + AT_KW) // AT_HD + h)),
                  pl.BlockSpec((tq, gw), lambda h, j, i: (i, h)),
                  pl.BlockSpec((1, 1, 2 * tq, 1), lambda h, j, i: (h, i, 0, 0)),
                  pl.BlockSpec((tq, gw), lambda h, j, i: (i, h))],
        out_specs=[pl.BlockSpec((S, gw), lambda h, j, i: (0, h)),
                   pl.BlockSpec((tk, AT_HD), lambda h, j, i: (j, h)),
                   pl.BlockSpec((tk, AT_HD), lambda h, j, i: (j, h))],
        out_shape=[jax.ShapeDtypeStruct((S, AT_QW), F32), jax.ShapeDtypeStruct((S, AT_KW), F32),
                   jax.ShapeDtypeStruct((S, AT_KW), F32)],
        scratch_shapes=[pltpu.VMEM((tk, AT_HD), F32), pltpu.VMEM((tk, AT_HD), F32)],
        compiler_params=_cp(("parallel", "arbitrary", "arbitrary")),
    )(q, k, u, o, lse, do)


def at_forward(x, hn, w, comm=None):
    S = x.shape[0]
    u = matmul("at_in", hn, w["at_w_in"])
    cos, sin = _rope_tables(S)
    (q, k), _ = rowcall("at_pre", lambda uq, uk, c, s, qg, kg: (_at_pre(uq, uk, c, s, qg, kg), ()),
                        [_col(u, AT_QW, 0), _col(u, AT_KW, 2), cos, sin], [w["at_q_norm_g"], w["at_k_norm_g"]],
                        [(AT_QW, BF16), (AT_KW, BF16)], tb=256)
    o, lse = at_flash_fwd(q, k, u)
    (og,), _ = rowcall("at_gate", lambda o, gate: ((o * _silu(gate),), ()), [o, _col(u, AT_QW, 2)], [],
                       [(AT_QW, BF16)], tb=256)
    out = matmul("at_out", og, w["at_w_out"], res=x)
    return out, dict(hn=hn, u=u, cos=cos, sin=sin, q=q, k=k, o=o, lse=lse, og=og)


def at_backward(dy, sv, w, comm=None):
    u = sv["u"]
    dog = matmul("at_out_dx", dy, w["at_w_out"], mode="nt")
    g_w_out = matmul("at_out_dw", sv["og"], dy, mode="tn")

    def gate_bwd(o, gate, dog):
        _, vjp = jax.vjp(lambda o, gate: o * _silu(gate), o, gate)
        return vjp(dog), ()

    (do, dgate), _ = rowcall("at_gate_bwd", gate_bwd, [sv["o"], _col(u, AT_QW, 2), dog], [],
                             [(AT_QW, F32)] * 2, tb=256)
    dq, dk, dv = at_flash_bwd(sv["q"], sv["k"], u, sv["o"], sv["lse"], do)

    def pre_bwd(uq, uk, cos, sin, dq, dk, dv, dgate, qg, kg):
        _, vjp = jax.vjp(lambda uq, uk, qg, kg: _at_pre(uq, uk, cos, sin, qg, kg), uq, uk, qg, kg)
        duq, duk, dqg, dkg = vjp((dq, dk))
        return (jnp.concatenate([duq, duk, dv, dgate], axis=1),), (dqg, dkg)

    (du,), (g_qg, g_kg) = rowcall(
        "at_pre_bwd", pre_bwd, [_col(u, AT_QW, 0), _col(u, AT_KW, 2), sv["cos"], sv["sin"], dq, dk, dv, dgate],
        [w["at_q_norm_g"], w["at_k_norm_g"]], [(AT_IN, BF16)], [(1, AT_HD), (1, AT_HD)], tb=128)
    hn = sv["hn"]
    g_w_in = matmul("at_in_dw", hn, du, mode="tn")
    dhn = matmul("at_in_dx", du, w["at_w_in"], mode="nt")
    return dhn, dict(at_w_in=g_w_in[None], at_q_norm_g=g_qg, at_k_norm_g=g_kg, at_w_out=g_w_out[None])


def _t5_bucket_np(rel):
    half = REL_BUCKETS // 2
    exact = half // 2
    n = np.abs(rel)
    large = exact + (np.log(np.maximum(n, 1).astype(np.float32) / exact)
                     / math.log(REL_MAX_DIST / exact) * (half - exact)).astype(np.int32)
    large = np.minimum(large, half - 1)
    return np.where(rel > 0, half, 0) + np.where(n < exact, n, large)


def _dl_tq(S, dil):
    return min(128, S // dil)


def _dl_bias_maps(tq, dil):
    W = tq + 2 * DL_STEPS
    i = np.arange(tq)[:, None]
    wdx = np.arange(W)[None, :]
    dm = wdx - DL_STEPS - i
    bucket = _t5_bucket_np(dm * dil).reshape(-1).astype(np.int32)
    band = np.where(np.abs(dm) <= DL_STEPS, 0.0, NEG_BIG).reshape(1, -1).astype(np.float32)
    onehot = (jnp.asarray(bucket)[None, :] == jnp.arange(REL_BUCKETS, dtype=jnp.int32)[:, None]).astype(F32)
    return onehot, jnp.asarray(band)


def _dl_attend(q, kwin, vwin, T, valid):
    tq = q.shape[0]
    os, ls = [], []
    for h in range(DL_HEADS):
        sl = slice(h * DL_HD, (h + 1) * DL_HD)
        s = _mm_nt(q[:, sl] * (DL_HD ** -0.5), kwin[:, sl]) + T[h]
        s = jnp.where(valid, s, NEG_BIG)
        m = lax.stop_gradient(jnp.max(s, axis=-1, keepdims=True))
        lse = m + jnp.log(jnp.sum(jnp.exp(s - m), axis=-1, keepdims=True))
        p = jnp.exp(s - lse)
        os.append(_mm(p, vwin[:, sl]))
        ls.append(jnp.broadcast_to(lse, (tq, DL_HD)))
    return jnp.concatenate(os, axis=1), jnp.concatenate(ls, axis=1)


def _dl_specs(tq, Ls):
    nb = Ls // tq
    hs = DL_STEPS
    per = tq // hs
    nh = Ls // hs

    def main(c):
        return pl.BlockSpec((tq, DL_W), lambda r, i: (r * nb + i, c))

    def prev(c):
        return pl.BlockSpec((hs, DL_W), lambda r, i: (r * nh + jnp.maximum(i * per - 1, 0), c))

    def nxt(c):
        return pl.BlockSpec((hs, DL_W), lambda r, i: (r * nh + jnp.minimum((i + 1) * per, nh - 1), c))

    return nb, main, prev, nxt


def _dl_valid(i, tq, Ls):
    W = tq + 2 * DL_STEPS
    mk = i * tq - DL_STEPS + lax.broadcasted_iota(jnp.int32, (1, W), 1)
    return (mk >= 0) & (mk < Ls)


def dl_attn_fwd(gi, dil, u, T):
    S = u.shape[0]
    Ls = S // dil
    tq = _dl_tq(S, dil)
    nb, main, prev, nxt = _dl_specs(tq, Ls)
    out = main(0)

    def body(q_ref, kp, kc, kn, vp, vc, vn, T_ref, o_ref, l_ref):
        kwin = jnp.concatenate([kp[...], kc[...], kn[...]], axis=0)
        vwin = jnp.concatenate([vp[...], vc[...], vn[...]], axis=0)
        o, l = _dl_attend(q_ref[...], kwin, vwin, T_ref[...], _dl_valid(pl.program_id(1), tq, Ls))
        o_ref[...] = o
        l_ref[...] = l

    o, l = pl.pallas_call(
        body, name=f"dl_attn_fwd{gi}", grid=(dil, nb),
        in_specs=[main(0), prev(1), main(1), nxt(1), prev(2), main(2), nxt(2),
                  pl.BlockSpec(T.shape, _const_map(3))],
        out_specs=[out, out],
        out_shape=[jax.ShapeDtypeStruct((S, DL_W), F32)] * 2,
        compiler_params=_cp(("parallel", "parallel")),
    )(u, u, u, u, u, u, u, T)
    return o, l


def dl_attn_bwd(gi, dil, u, T, do, dl, dgate=None):
    S = u.shape[0]
    Ls = S // dil
    tq = _dl_tq(S, dil)
    hs = DL_STEPS
    W = tq + 2 * hs
    nb, main, prev, nxt = _dl_specs(tq, Ls)
    out = main(0)
    win = pl.BlockSpec((1, W, DL_W), lambda r, i: (r * nb + i, 0, 0))

    def body(q_ref, kp, kc, kn, vp, vc, vn, T_ref, do_ref, dl_ref, dq_ref, dkw_ref, dvw_ref, dT_ref):
        first = (pl.program_id(0) == 0) & (pl.program_id(1) == 0)

        @pl.when(first)
        def _():
            dT_ref[...] = jnp.zeros_like(dT_ref)

        kwin = jnp.concatenate([kp[...], kc[...], kn[...]], axis=0)
        vwin = jnp.concatenate([vp[...], vc[...], vn[...]], axis=0)
        valid = _dl_valid(pl.program_id(1), tq, Ls)
        _, vjp = jax.vjp(lambda q, k, v, T: _dl_attend(q, k, v, T, valid), q_ref[...], kwin, vwin, T_ref[...])
        dq, dkw, dvw, dT = vjp((do_ref[...], dl_ref[...]))
        dq_ref[...] = dq
        dkw_ref[0] = dkw
        dvw_ref[0] = dvw
        dT_ref[...] += dT

    dq, dkw, dvw, dT = pl.pallas_call(
        body, name=f"dl_attn_bwd{gi}", grid=(dil, nb),
        in_specs=[main(0), prev(1), main(1), nxt(1), prev(2), main(2), nxt(2),
                  pl.BlockSpec(T.shape, _const_map(3)), out, out],
        out_specs=[out, win, win, pl.BlockSpec(T.shape, _const_map(3))],
        out_shape=[jax.ShapeDtypeStruct((S, DL_W), F32),
                   jax.ShapeDtypeStruct((dil * nb, W, DL_W), F32),
                   jax.ShapeDtypeStruct((dil * nb, W, DL_W), F32),
                   jax.ShapeDtypeStruct(T.shape, F32)],
        compiler_params=_cp(("arbitrary", "arbitrary")),
    )(u, u, u, u, u, u, u, T, do, dl)

    per = tq // hs
    n_out = 3 if dgate is None else 4

    def fold(*refs):
        dq_ref, kc, kp, kn, vc, vp, vn = refs[:7]
        du_ref = refs[-1]
        i = pl.program_id(1)
        has_p = (i > 0).astype(F32)
        has_n = (i < nb - 1).astype(F32)
        du_ref[:, 0:DL_W] = dq_ref[...].astype(BF16)
        for c, (c_ref, p_ref, n_ref) in enumerate(((kc, kp, kn), (vc, vp, vn)), start=1):
            mid = c_ref[0, hs:hs + tq, :]
            top = mid[0:hs] + p_ref[0] * has_p
            bot = mid[tq - hs:tq] + n_ref[0] * has_n
            parts = [top, bot] if tq == 2 * hs else ([top, mid[hs:tq - hs], bot] if tq > 2 * hs else [top + n_ref[0] * has_n])
            du_ref[:, c * DL_W:(c + 1) * DL_W] = jnp.concatenate(parts, axis=0).astype(BF16)
        if dgate is not None:
            du_ref[:, 3 * DL_W:4 * DL_W] = refs[7][...].astype(BF16)

    wfull = pl.BlockSpec((1, W, DL_W), lambda r, i: (r * nb + i, 0, 0))
    wprev = pl.BlockSpec((1, hs, DL_W), lambda r, i: (r * nb + jnp.maximum(i - 1, 0), per + 1, 0))
    wnext = pl.BlockSpec((1, hs, DL_W), lambda r, i: (r * nb + jnp.minimum(i + 1, nb - 1), 0, 0))
    extra_specs, extra_args = ([], []) if dgate is None else ([out], [dgate])
    du = pl.pallas_call(
        fold, name=f"dl_fold{gi}", grid=(dil, nb),
        in_specs=[out, wfull, wprev, wnext, wfull, wprev, wnext] + extra_specs,
        out_specs=pl.BlockSpec((tq, n_out * DL_W), lambda r, i: (r * nb + i, 0)),
        out_shape=jax.ShapeDtypeStruct((S, n_out * DL_W), BF16),
        compiler_params=_cp(("parallel", "parallel")),
    )(dq, dkw, dkw, dkw, dvw, dvw, dvw, *extra_args)
    return du, dT


def _dl_merge(o0, o1, o2, l0, l1, l2, gate):
    m = jnp.maximum(jnp.maximum(l0, l1), l2)
    e0, e1, e2 = jnp.exp(l0 - m), jnp.exp(l1 - m), jnp.exp(l2 - m)
    den = e0 + e1 + e2
    return ((e0 * o0 + e1 * o1 + e2 * o2) / den) * _silu(gate)


DL_DILS = tuple(d for _, d in DL_PAIRS)


def _dl_group_weights(w_in):
    g3 = 3 * DL_W
    return [jnp.concatenate([w_in[:, :g3], w_in[:, 3 * g3:]], axis=1), w_in[:, g3:2 * g3], w_in[:, 2 * g3:3 * g3]]


def dl_forward(x, hns, w, comm=None):
    S = x.shape[0]
    wg = _dl_group_weights(w["dl_w_in"])
    rbT = w["rel_bias"].T
    us, os, ls, Ts, maps = [], [], [], [], []
    for gi, dil in enumerate(DL_DILS):
        u = matmul(f"dl_in{gi}", hns[gi], wg[gi])
        tq = _dl_tq(S, dil)
        W = tq + 2 * DL_STEPS
        onehot, band = _dl_bias_maps(tq, dil)
        (T,) = smallcall(f"dl_bias{gi}", lambda rbT, oh, band: (_mm_exact(rbT, oh) + band,), [rbT, onehot, band],
                         [(DL_HEADS, tq * W)])
        T = T.reshape(DL_HEADS, tq, W)
        o, l = dl_attn_fwd(gi, dil, u, T)
        us.append(u)
        os.append(o)
        ls.append(l)
        Ts.append(T)
        maps.append(onehot)
    rows = [a if d == 1 else _perm(a, d) for a, d in zip(os + ls, DL_DILS * 2)] + [_col(us[0], DL_W, 3)]
    (og,), _ = rowcall("dl_merge", lambda *a: ((_dl_merge(*a),), ()), rows, [], [(DL_W, BF16)], tb=256)
    out = matmul("dl_out", og, w["dl_w_out"], res=x)
    return out, dict(hns=hns, us=us, os=os, ls=ls, Ts=Ts, maps=maps, og=og, wg=wg)


def dl_backward(dy, sv, w, comm=None):
    us = sv["us"]
    dog = matmul("dl_out_dx", dy, w["dl_w_out"], mode="nt")
    g_w_out = matmul("dl_out_dw", sv["og"], dy, mode="tn")

    def merge_bwd(o0, o1, o2, l0, l1, l2, gate, dog):
        _, vjp = jax.vjp(_dl_merge, o0, o1, o2, l0, l1, l2, gate)
        return vjp(dog), ()

    rows = [a if d == 1 else _perm(a, d) for a, d in zip(sv["os"] + sv["ls"], DL_DILS * 2)] + [_col(us[0], DL_W, 3), dog]
    grads7, _ = rowcall("dl_merge_bwd", merge_bwd, rows, [], [(DL_W, F32, d) for d in DL_DILS * 2] + [(DL_W, F32)], tb=256)
    dos, dls, dgate = grads7[0:3], grads7[3:6], grads7[6]
    g_rbT, g_ws, dhns = None, [], []
    for gi, dil in enumerate(DL_DILS):
        du, dT = dl_attn_bwd(gi, dil, us[gi], sv["Ts"][gi], dos[gi], dls[gi], dgate if gi == 0 else None)
        (g,) = smallcall(f"dl_bias_bwd{gi}", lambda dT, oh: (_mm_nt_exact(dT, oh),),
                         [dT.reshape(DL_HEADS, -1), sv["maps"][gi]], [(DL_HEADS, REL_BUCKETS)])
        g_rbT = g if g_rbT is None else g_rbT + g
        g_ws.append(matmul(f"dl_in_dw{gi}", sv["hns"][gi], du, mode="tn"))
        dhns.append(matmul(f"dl_in_dx{gi}", du, sv["wg"][gi], mode="nt"))
    g3 = 3 * DL_W
    g_w_in = jnp.concatenate([g_ws[0][:, :g3], g_ws[1], g_ws[2], g_ws[0][:, g3:]], axis=1)
    return tuple(dhns), dict(dl_w_in=g_w_in[None], dl_w_out=g_w_out[None], rel_bias=g_rbT.T)


_FWD = (ssd_forward, hg_forward, at_forward, dl_forward)
_BWD = (ssd_backward, hg_backward, at_backward, dl_backward)


def _norm_dils(layer):
    return DL_DILS if layer % 4 == 3 else (1,)


class NoExchange:
    def fwd_plans(self, layer, w):
        return None

    def fwd_done(self, layer, got, w):
        pass

    def bwd_plans(self, layer, grads):
        return None

    def bwd_done(self, layer, got):
        pass


def local_step(x, tgt, w, sched=None):
    sched = sched or NoExchange()
    saved = []
    h = x
    for layer in range(DEPTH):
        hn = norm_fwd(f"norm{layer}", h, w["norm_g"][layer:layer + 1], _norm_dils(layer))
        extra = (layer,) if layer % 4 == 1 else ()
        h_next, sv = _FWD[layer % 4](h, hn, w, *extra, comm=sched.fwd_plans(layer, w))
        sched.fwd_done(layer, sv.get("got", {}), w)
        saved.append((h, sv))
        h = h_next
    loss, dh, g_final = loss_head(h, tgt, w["final_g"].reshape(1, D_MODEL))
    grads = {}
    g_norm = [None] * DEPTH
    for layer in reversed(range(DEPTH)):
        h_in, sv = saved[layer]
        extra = (layer,) if layer % 4 == 1 else ()
        dhn, g = _BWD[layer % 4](dh, sv, w, *extra, comm=sched.bwd_plans(layer, grads))
        sched.bwd_done(layer, g.pop("got", {}))
        grads.update(g)
        dh, g_norm[layer] = norm_bwd(f"norm{layer}_bwd", h_in, w["norm_g"][layer:layer + 1], dhn, dh, _norm_dils(layer))
    grads["norm_g"] = jnp.concatenate(g_norm, axis=0)
    grads["final_g"] = g_final.reshape(D_MODEL)
    grads["ssd_d"] = jnp.sum(grads.pop("ssd_d_exp").reshape(SSD_HEADS, SSD_HEADDIM), axis=1)[None]
    return loss, dh, grads


IN_NAMES = ("ssd_w_in", "hg_w_in", "at_w_in", "dl_w_in")
OUT_NAMES = ("ssd_w_out", "hg_w_out", "at_w_out", "dl_w_out")
IN_COLS = (SSD_IN // 4, HG_IN // 4, AT_IN // 4, DL_IN // 4)
OUT_ROWS = (SSD_DI // 4, HG_W // 4, AT_QW // 4, DL_W // 4)
PACK_IN = sum(IN_COLS)
PACK_OUT = sum(OUT_ROWS)
N_CHIPS = 4
N_DEV = 8
HBM = pl.BlockSpec(memory_space=pl.ANY)


def _mesh_pos():
    return lax.axis_index("x"), lax.axis_index("y"), lax.axis_index("c")


def _other_chips(x, y):
    return [(1 - x, y), (x, 1 - y), (1 - x, 1 - y)]


def _half_rows(half, n):
    return pl.ds(pl.multiple_of(half * n, n), n)


def _remote(src, dst, send, recv, k, to):
    return pltpu.make_async_remote_copy(src_ref=src, dst_ref=dst, send_sem=send.at[k], recv_sem=recv.at[k],
                                        device_id=to, device_id_type=MESH)


def gather_plan(packs, whole=()):
    arrs = list(packs) + list(whole)
    n_half = len(packs)

    def pieces(ins, outs):
        x, y, c = _mesh_pos()
        for a, (src, dst) in enumerate(zip(ins, outs)):
            h = src.shape[0] // 2 if a < n_half else None
            for j, (px, py) in enumerate(_other_chips(x, y)):
                yield a, j, src, dst, h, (x, y, c), (px, py)

    def start(ins, outs, send, recv):
        for a, j, src, dst, h, (x, y, c), (px, py) in pieces(ins, outs):
            me = 2 * x + y
            if h is None:
                _remote(src, dst.at[me], send, recv, 6 * a + j, (px, py, c)).start()
            else:
                _remote(src.at[_half_rows(c, h)], dst.at[me, _half_rows(c, h)], send, recv, 6 * a + j, (px, py, c)).start()

    def mid(ins, outs, send, recv):
        for a, j, src, dst, h, (x, y, c), (px, py) in pieces(ins, outs):
            kp = 2 * px + py
            if h is None:
                _remote(src, dst.at[kp], send, recv, 6 * a + j, (px, py, c)).wait_recv()
            else:
                got = dst.at[kp, _half_rows(c, h)]
                _remote(src.at[_half_rows(c, h)], got, send, recv, 6 * a + j, (px, py, c)).wait_recv()
                _remote(got, got, send, recv, 6 * a + 3 + j, (x, y, 1 - c)).start()

    def finish(ins, outs, send, recv):
        for a, j, src, dst, h, (x, y, c), (px, py) in pieces(ins, outs):
            me, kp = 2 * x + y, 2 * px + py
            if h is None:
                _remote(src, dst.at[me], send, recv, 6 * a + j, (px, py, c)).wait_send()
            else:
                theirs = dst.at[kp, _half_rows(1 - c, h)]
                _remote(theirs, theirs, send, recv, 6 * a + 3 + j, (x, y, 1 - c)).wait_recv()
                _remote(src.at[_half_rows(c, h)], dst.at[me, _half_rows(c, h)], send, recv, 6 * a + j, (px, py, c)).wait_send()
                mine = dst.at[kp, _half_rows(c, h)]
                _remote(mine, mine, send, recv, 6 * a + 3 + j, (x, y, 1 - c)).wait_send()

    return dict(ins=arrs, out_shape=[jax.ShapeDtypeStruct((N_CHIPS,) + a.shape, a.dtype) for a in arrs],
                n_sems=6 * len(arrs), start=start, mid=mid, finish=finish)


def scatter_plan(halves):
    def copies(ins, outs, send, recv):
        x, y, c = _mesh_pos()
        for a, (src, dst) in enumerate(zip(ins, outs)):
            for j, (px, py) in enumerate(_other_chips(x, y)):
                yield _remote(src.at[2 * px + py], dst.at[j], send, recv, 3 * a + j, (px, py, c))

    def start(ins, outs, send, recv):
        for cp in copies(ins, outs, send, recv):
            cp.start()

    def finish(ins, outs, send, recv):
        for cp in copies(ins, outs, send, recv):
            cp.wait()

    return dict(ins=list(halves), out_shape=[jax.ShapeDtypeStruct((3,) + a.shape[1:], a.dtype) for a in halves],
                n_sems=3 * len(halves), start=start, mid=None, finish=finish)


def run_exchange(name, plan):
    n_in = len(plan["ins"])

    def body(*refs):
        ins, outs = refs[:n_in], refs[n_in:-2]
        send, recv = refs[-2], refs[-1]
        plan["start"](ins, outs, send, recv)
        if plan["mid"] is not None:
            plan["mid"](ins, outs, send, recv)
        plan["finish"](ins, outs, send, recv)

    return pl.pallas_call(
        body, name=name, in_specs=[HBM] * n_in, out_specs=[HBM] * len(plan["out_shape"]), out_shape=plan["out_shape"],
        scratch_shapes=[pltpu.SemaphoreType.DMA((plan["n_sems"],))] * 2,
        compiler_params=pltpu.CompilerParams(has_side_effects=True),
    )(*plan["ins"])


def swap_halves(name, g_in, g_out):
    h_in, h_out = g_in.shape[1] // 2, g_out.shape[1] // 2

    def body(gi, go, ri, ro, send, recv):
        x, y, c = _mesh_pos()
        sib = (x, y, 1 - c)

        def rows(half, n):
            return pl.ds(pl.multiple_of(half * n, n), n)

        cps = [pltpu.make_async_remote_copy(src_ref=gi.at[:, rows(1 - c, h_in)], dst_ref=ri, send_sem=send.at[0],
                                            recv_sem=recv.at[0], device_id=sib, device_id_type=MESH),
               pltpu.make_async_remote_copy(src_ref=go.at[:, rows(1 - c, h_out)], dst_ref=ro, send_sem=send.at[1],
                                            recv_sem=recv.at[1], device_id=sib, device_id_type=MESH)]
        for cp in cps:
            cp.start()
        for cp in cps:
            cp.wait()

    return pl.pallas_call(
        body, name=name, in_specs=[HBM, HBM], out_specs=[HBM, HBM],
        out_shape=[jax.ShapeDtypeStruct((N_CHIPS, h_in, g_in.shape[2]), g_in.dtype),
                   jax.ShapeDtypeStruct((N_CHIPS, h_out, g_out.shape[2]), g_out.dtype)],
        scratch_shapes=[pltpu.SemaphoreType.DMA((2,)), pltpu.SemaphoreType.DMA((2,))],
        compiler_params=pltpu.CompilerParams(has_side_effects=True),
    )(g_in, g_out)


def half_add(name, g, r, c_idx, tb):
    _, rows2, C = g.shape
    h = rows2 // 2
    nb = h // tb

    def body(c_ref, g_ref, r_ref, f_ref, b_ref):
        s = g_ref[...] + r_ref[...]
        f_ref[...] = s
        b_ref[...] = s.astype(BF16)

    grid_spec = pltpu.PrefetchScalarGridSpec(
        num_scalar_prefetch=1, grid=(N_CHIPS, nb),
        in_specs=[pl.BlockSpec((1, tb, C), lambda k, i, c: (k, c[0] * nb + i, 0)),
                  pl.BlockSpec((1, tb, C), lambda k, i, c: (k, i, 0))],
        out_specs=[pl.BlockSpec((1, tb, C), lambda k, i, c: (k, i, 0))] * 2)
    return pl.pallas_call(
        body, name=name, grid_spec=grid_spec,
        out_shape=[jax.ShapeDtypeStruct((N_CHIPS, h, C), F32), jax.ShapeDtypeStruct((N_CHIPS, h, C), BF16)],
        compiler_params=_cp(("parallel", "parallel")),
    )(c_idx, g, r)


def chip_sum(name, f, r, me_idx, tb):
    _, h, C = f.shape
    nb = h // tb

    def body(me_ref, f_ref, r0, r1, r2, o_ref):
        o_ref[...] = ((f_ref[0] + r0[0].astype(F32)) + r1[0].astype(F32)) + r2[0].astype(F32)

    def slot(j):
        return pl.BlockSpec((1, tb, C), lambda i, me: (j, i, 0))

    grid_spec = pltpu.PrefetchScalarGridSpec(
        num_scalar_prefetch=1, grid=(nb,),
        in_specs=[pl.BlockSpec((1, tb, C), lambda i, me: (me[0], i, 0)), slot(0), slot(1), slot(2)],
        out_specs=pl.BlockSpec((tb, C), lambda i, me: (i, 0)))
    return pl.pallas_call(
        body, name=name, grid_spec=grid_spec, out_shape=jax.ShapeDtypeStruct((h, C), F32),
        compiler_params=_cp(("parallel",)),
    )(me_idx, f, r, r, r)


def share_halves(name, f_in, f_out):
    def body(fi, fo, oi, oo, send, recv):
        x, y, c = _mesh_pos()
        sib = (x, y, 1 - c)
        cps = [pltpu.make_async_remote_copy(src_ref=fi, dst_ref=oi, send_sem=send.at[0], recv_sem=recv.at[0],
                                            device_id=sib, device_id_type=MESH),
               pltpu.make_async_remote_copy(src_ref=fo, dst_ref=oo, send_sem=send.at[1], recv_sem=recv.at[1],
                                            device_id=sib, device_id_type=MESH)]
        for cp in cps:
            cp.start()
        for cp in cps:
            cp.wait()

    return pl.pallas_call(
        body, name=name, in_specs=[HBM, HBM], out_specs=[HBM, HBM],
        out_shape=[jax.ShapeDtypeStruct(f_in.shape, F32), jax.ShapeDtypeStruct(f_out.shape, F32)],
        scratch_shapes=[pltpu.SemaphoreType.DMA((2,)), pltpu.SemaphoreType.DMA((2,))],
        compiler_params=pltpu.CompilerParams(has_side_effects=True),
    )(f_in, f_out)


def gather_small(pack):
    def body(p, g, send, recv, lsem):
        x, y, c = _mesh_pos()
        me = 4 * x + 2 * y + c
        local = pltpu.make_async_copy(p, g.at[me], lsem)
        local.start()
        cps = []
        k = 0
        for fx in (0, 1):
            for fy in (0, 1):
                for fc in (0, 1):
                    if fx + fy + fc == 0:
                        continue
                    to = (x ^ fx, y ^ fy, c ^ fc)
                    cps.append((pltpu.make_async_remote_copy(src_ref=p, dst_ref=g.at[me], send_sem=send.at[k],
                                                             recv_sem=recv.at[k], device_id=to, device_id_type=MESH), to, k))
                    k += 1
        for cp, _, _ in cps:
            cp.start()
        for cp, to, k in cps:
            frm = 4 * to[0] + 2 * to[1] + to[2]
            pltpu.make_async_remote_copy(src_ref=p, dst_ref=g.at[frm], send_sem=send.at[k], recv_sem=recv.at[k],
                                         device_id=to, device_id_type=MESH).wait_recv()
        for cp, _, _ in cps:
            cp.wait_send()
        local.wait()

    return pl.pallas_call(
        body, name="gather_small", in_specs=[HBM], out_specs=HBM,
        out_shape=jax.ShapeDtypeStruct((N_DEV,) + pack.shape, pack.dtype),
        scratch_shapes=[pltpu.SemaphoreType.DMA((7,)), pltpu.SemaphoreType.DMA((7,)), pltpu.SemaphoreType.DMA],
        compiler_params=pltpu.CompilerParams(has_side_effects=True),
    )(pack)


def _adamw(w, g, m, v):
    m = ADAM_B1 * m + (1.0 - ADAM_B1) * g
    v = ADAM_B2 * v + (1.0 - ADAM_B2) * (g * g)
    m_hat = m / (1.0 - ADAM_B1 ** ADAM_STEP)
    v_hat = v / (1.0 - ADAM_B2 ** ADAM_STEP)
    delta = -ADAM_LR * (m_hat / (jnp.sqrt(v_hat) + ADAM_EPS) + ADAM_WD * w)
    return delta, m, v


def adamw_big(name, w, g, m, v):
    shp = w.shape
    flat = lambda a: a.reshape(shp[-2], shp[-1])
    (d, nm, nv), _ = rowcall(name, lambda w, g, m, v: (_adamw(w, g, m, v), ()), [flat(w), flat(g), flat(m), flat(v)], [],
                             [(shp[-1], F32)] * 3, tb=256)
    return d.reshape(shp), nm.reshape(shp), nv.reshape(shp)


def _pack_small(arrs):
    flat = jnp.concatenate([a.reshape(-1) for a in arrs])
    n = flat.shape[0]
    rows = -(-n // (SUBLANE * LANE)) * SUBLANE
    return jnp.pad(flat, (0, rows * LANE - n)).reshape(rows, LANE)


def _unpack_small(pack, shapes):
    flat = pack.reshape(-1)
    outs, off = [], 0
    for s in shapes:
        n = int(np.prod(s))
        outs.append(flat[off:off + n].reshape(s))
        off += n
    return outs


SMALL_NAMES = ("norm_g", "final_g", "rel_bias", "hgrn_lb", "ssd_conv_w", "ssd_conv_b", "ssd_dt_bias", "ssd_a_log",
               "ssd_d", "ssd_norm_g", "hg_norm_g", "at_q_norm_g", "at_k_norm_g")
ALL_NAMES = ("norm_g", "final_g", "rel_bias", "hgrn_lb", "ssd_w_in", "ssd_conv_w", "ssd_conv_b", "ssd_dt_bias",
             "ssd_a_log", "ssd_d", "ssd_norm_g", "ssd_w_out", "hg_w_in", "hg_norm_g", "hg_w_out", "at_w_in",
             "at_q_norm_g", "at_k_norm_g", "at_w_out", "dl_w_in", "dl_w_out")


def kernel(x, norm_g, final_g, rel_bias, hgrn_lb, ssd_w_in, ssd_conv_w, ssd_conv_b, ssd_dt_bias, ssd_a_log, ssd_d, ssd_norm_g, ssd_w_out, hg_w_in, hg_norm_g, hg_w_out, at_w_in, at_q_norm_g, at_k_norm_g, at_w_out, dl_w_in, dl_w_out, loss_target, m_norm_g, m_final_g, m_rel_bias, m_hgrn_lb, m_ssd_w_in, m_ssd_conv_w, m_ssd_conv_b, m_ssd_dt_bias, m_ssd_a_log, m_ssd_d, m_ssd_norm_g, m_ssd_w_out, m_hg_w_in, m_hg_norm_g, m_hg_w_out, m_at_w_in, m_at_q_norm_g, m_at_k_norm_g, m_at_w_out, m_dl_w_in, m_dl_w_out, v_norm_g, v_final_g, v_rel_bias, v_hgrn_lb, v_ssd_w_in, v_ssd_conv_w, v_ssd_conv_b, v_ssd_dt_bias, v_ssd_a_log, v_ssd_d, v_ssd_norm_g, v_ssd_w_out, v_hg_w_in, v_hg_norm_g, v_hg_w_out, v_at_w_in, v_at_q_norm_g, v_at_k_norm_g, v_at_w_out, v_dl_w_in, v_dl_w_out):
    args = locals()
    W = {n: args[n] for n in ALL_NAMES}
    M = {n: args["m_" + n] for n in ALL_NAMES}
    V = {n: args["v_" + n] for n in ALL_NAMES}
    xi, yi, ci = lax.axis_index("x"), lax.axis_index("y"), lax.axis_index("c")
    chip = 2 * xi + yi
    conv_shard = SSD_CONV_CH // N_CHIPS
    hgn_shard = HG_W // N_CHIPS

    p_in = [W[n][0].astype(BF16) for n in IN_NAMES]
    p_out = [W[n][0].astype(BF16) for n in OUT_NAMES]
    p_small = jnp.concatenate([
        jnp.pad(ssd_conv_w[0], ((0, 0), (0, D_MODEL - conv_shard))),
        jnp.pad(hg_norm_g, ((0, 0), (0, D_MODEL - hgn_shard)))], axis=0)
    c_idx = ci.astype(jnp.int32).reshape(1)
    me_idx = chip.astype(jnp.int32).reshape(1)

    def slot(stack, own, k):
        return jnp.where(chip == k, own, stack[k])

    def layer_weights(layer, got):
        s_in, s_out = got[0], got[1]
        return (jnp.concatenate([slot(s_in, p_in[layer], k) for k in range(N_CHIPS)], axis=1),
                jnp.concatenate([slot(s_out, p_out[layer], k) for k in range(N_CHIPS)], axis=0))

    def reduce_start(tag, layers, grads):
        gp_in = jnp.concatenate([grads[IN_NAMES[l]][0].reshape(D_MODEL, N_CHIPS, IN_COLS[l]).transpose(1, 0, 2)
                                 for l in layers], axis=2)
        gp_out = jnp.concatenate([grads[OUT_NAMES[l]][0].reshape(N_CHIPS, OUT_ROWS[l], D_MODEL) for l in layers], axis=1)
        r_in, r_out = swap_halves(f"swap_halves_{tag}", gp_in, gp_out)
        f_in, b_in = half_add(f"half_add_in_{tag}", gp_in, r_in, c_idx, 128)
        f_out, b_out = half_add(f"half_add_out_{tag}", gp_out, r_out, c_idx, 128)
        return (f_in, f_out), scatter_plan([b_in, b_out])

    def reduce_finish(tag, layers, halves, got, G):
        s_in = chip_sum(f"chip_sum_in_{tag}", halves[0], got[0], me_idx, 128)
        s_out = chip_sum(f"chip_sum_out_{tag}", halves[1], got[1], me_idx, 128)
        o_in, o_out = share_halves(f"share_halves_{tag}", s_in, s_out)
        red_in = jnp.where(ci == 0, jnp.concatenate([s_in, o_in], axis=0), jnp.concatenate([o_in, s_in], axis=0))
        red_out = jnp.where(ci == 0, jnp.concatenate([s_out, o_out], axis=0), jnp.concatenate([o_out, s_out], axis=0))
        off_c = off_r = 0
        for l in layers:
            G[IN_NAMES[l]] = red_in[:, off_c:off_c + IN_COLS[l]][None]
            G[OUT_NAMES[l]] = red_out[off_r:off_r + OUT_ROWS[l]][None]
            off_c += IN_COLS[l]
            off_r += OUT_ROWS[l]

    class Schedule:
        early = (2, 3)

        def fwd_plans(self, layer, w):
            if layer == 0:
                return dict(ssd_scan_f=gather_plan([p_in[1], p_out[1]]), ssd_scan_b=gather_plan([p_in[2], p_out[2]]))
            if layer == 1:
                return dict(hg_scan_f=gather_plan([p_in[3], p_out[3]]))
            return None

        def fwd_done(self, layer, got, w):
            if layer == 0:
                w["hg_w_in"], w["hg_w_out"] = layer_weights(1, got["ssd_scan_f"])
                w["at_w_in"], w["at_w_out"] = layer_weights(2, got["ssd_scan_b"])
            if layer == 1:
                w["dl_w_in"], w["dl_w_out"] = layer_weights(3, got["hg_scan_f"])

        def bwd_plans(self, layer, grads):
            if layer == 1:
                self.halves, plan = reduce_start("a", self.early, grads)
                return dict(hg_scan_f_bwd=plan)
            return None

        def bwd_done(self, layer, got):
            if layer == 1:
                self.got = got["hg_scan_f_bwd"]

    g0_in, g0_out, g_small = run_exchange("gather_w0", gather_plan([p_in[0], p_out[0]], whole=[p_small]))
    ssd_in_full, ssd_out_full = layer_weights(0, (g0_in, g0_out))
    conv_full = jnp.concatenate([slot(g_small, p_small, k)[:SSD_CONV, :conv_shard] for k in range(N_CHIPS)], axis=1)
    hgn_full = jnp.concatenate([slot(g_small, p_small, k)[SSD_CONV:SSD_CONV + 1, :hgn_shard] for k in range(N_CHIPS)], axis=1)
    w = dict(
        norm_g=norm_g, final_g=final_g, rel_bias=rel_bias, hgrn_lb=hgrn_lb,
        ssd_w_main=ssd_in_full[:, :SSD_MAIN], ssd_w_dt=ssd_in_full[:, SSD_MAIN:],
        ssd_conv_w8=jnp.concatenate([conv_full, jnp.zeros((1, SSD_CONV_CH), F32)], axis=0),
        ssd_conv_b=ssd_conv_b, ssd_dt_bias=ssd_dt_bias, ssd_a_log=ssd_a_log, ssd_d=ssd_d, ssd_norm_g=ssd_norm_g,
        ssd_w_out=ssd_out_full, hg_norm_g=hgn_full, at_q_norm_g=at_q_norm_g, at_k_norm_g=at_k_norm_g)

    sched = Schedule()
    loss_tile, grad_x, grads = local_step(x[0], loss_target[0], w, sched)
    loss = lax.psum(loss_tile[0, 0], ("x", "y", "c"))

    G = {}
    late = (0, 1)
    halves_b, plan_b = reduce_start("b", late, grads)
    got_b = run_exchange("scatter_b", plan_b)
    reduce_finish("a", sched.early, sched.halves, sched.got, G)
    reduce_finish("b", late, halves_b, got_b, G)

    small_full = [grads[n].reshape(-1) for n in SMALL_NAMES]
    shapes_full = [grads[n].shape for n in SMALL_NAMES]
    packs = gather_small(_pack_small(small_full))
    (red_small,) = smallcall("sum_small", lambda p: (functools.reduce(lambda a, b: a + b, [p[k] for k in range(N_DEV)]),),
                             [packs], [packs.shape[1:]])
    for n, g in zip(SMALL_NAMES, _unpack_small(red_small, shapes_full)):
        G[n] = g
    G["ssd_conv_w"] = lax.dynamic_slice_in_dim(G["ssd_conv_w"].reshape(1, SSD_CONV, SSD_CONV_CH), chip * conv_shard, conv_shard, axis=2)
    G["hg_norm_g"] = lax.dynamic_slice_in_dim(G["hg_norm_g"].reshape(1, HG_W), chip * hgn_shard, hgn_shard, axis=1)
    for n in SMALL_NAMES:
        G[n] = G[n].reshape(W[n].shape)

    D, NM, NV = {}, {}, {}
    for n in IN_NAMES + OUT_NAMES:
        D[n], NM[n], NV[n] = adamw_big("adamw_" + n, W[n], G[n], M[n], V[n])
    shapes = [W[n].shape for n in SMALL_NAMES]
    pk = [_pack_small([T[n] for n in SMALL_NAMES]) for T in (W, G, M, V)]
    outs = smallcall("adamw_small", lambda w, g, m, v: _adamw(w, g, m, v), pk, [pk[0].shape] * 3)
    for T, pack in zip((D, NM, NV), outs):
        for n, a in zip(SMALL_NAMES, _unpack_small(pack, shapes)):
            T[n] = a
    return (loss, grad_x[None], *[G[n] for n in ALL_NAMES], *[D[n] for n in ALL_NAMES],
            *[NM[n] for n in ALL_NAMES], *[NV[n] for n in ALL_NAMES])
```

```python
import functools
import math

import numpy as np
import jax
import jax.numpy as jnp
from jax import lax
from jax.experimental import pallas as pl
from jax.experimental.pallas import tpu as pltpu

F32 = jnp.float32
BF16 = jnp.bfloat16
MESH = pl.DeviceIdType.MESH

D_MODEL = 1024
DEPTH = 4
GRID_W = 64
EPS = 1e-6
NEG_BIG = -1e30

SSD_DI = 2048
SSD_HEADDIM = 64
SSD_HEADS = 32
SSD_GROUPS = 4
SSD_HPG = 8
SSD_STATE = 128
SSD_CONV = 7
SSD_CHUNK = 128
SSD_GPS = 4
SSD_CONV_CH = SSD_DI + 2 * SSD_GROUPS * SSD_STATE
SSD_MAIN = SSD_DI + SSD_CONV_CH
SSD_IN = SSD_MAIN + 2 * SSD_HEADS

HG_HEADS = 8
HG_EXPAND = 128
HG_W = 1024
HG_CHUNK = 32
HG_ROWS = 128
HG_HPS = 8
HG_IN = 5 * HG_W

AT_HEADS = 16
AT_KV = 8
AT_GRP = 2
AT_HD = 128
ROPE_THETA = 10000.0
ROPE_AXIS = 64
AT_QW = AT_HEADS * AT_HD
AT_KW = AT_KV * AT_HD
AT_IN = 2 * AT_QW + 2 * AT_KW

DL_PAIRS = ((128, 1), (512, 4), (2048, 16))
DL_HEADS = 16
DL_HD = 64
DL_W = 1024
DL_STEPS = 64
DL_IN = 10 * DL_W
REL_BUCKETS = 32
REL_MAX_DIST = 1024

ADAM_LR = 0.001
ADAM_B1 = 0.9
ADAM_B2 = 0.999
ADAM_EPS = 1e-08
ADAM_WD = 0.01
ADAM_STEP = 10

VMEM_LIMIT = 56 * 1024 * 1024
LANE = 128
SUBLANE = 8


def _cp(sem=None):
    return pltpu.CompilerParams(dimension_semantics=sem, vmem_limit_bytes=VMEM_LIMIT)


_NN, _NT, _TN = ((1,), (0,)), ((1,), (1,)), ((0,), (0,))


def _dot(a, b, dims):
    return lax.dot_general(a.astype(BF16), b.astype(BF16), (dims, ((), ())), preferred_element_type=F32)


def _dot_rule(dims, da_rule, db_rule):
    @jax.custom_vjp
    def f(a, b):
        return _dot(a, b, dims)

    def fwd(a, b):
        return _dot(a, b, dims), (a, b)

    def bwd(res, g):
        a, b = res
        return da_rule(a, b, g).astype(a.dtype), db_rule(a, b, g).astype(b.dtype)

    f.defvjp(fwd, bwd)
    return f


_mm = _dot_rule(_NN, lambda a, b, g: _dot(g, b, _NT), lambda a, b, g: _dot(a, g, _TN))
_mm_nt = _dot_rule(_NT, lambda a, b, g: _dot(g, b, _NN), lambda a, b, g: _dot(g, a, _TN))
_mm_tn = _dot_rule(_TN, lambda a, b, g: _dot(b, g, _NT), lambda a, b, g: _dot(a, g, _NN))


def _mm_exact(a, b):
    return jnp.dot(a, b, preferred_element_type=F32, precision=lax.Precision.HIGHEST)


def _dot3(t, a, dims):
    hi = a.astype(BF16)
    r1 = a - hi.astype(F32)
    mid = r1.astype(BF16)
    lo = r1 - mid.astype(F32)
    return _dot(t, hi, dims) + (_dot(t, mid, dims) + _dot(t, lo, dims))


@jax.custom_vjp
def _mm_tri(t, a):
    return _dot3(t, a, _NN)


def _mm_tri_fwd(t, a):
    return _dot3(t, a, _NN), t


def _mm_tri_bwd(t, g):
    return None, _dot3(t, g, _TN)


_mm_tri.defvjp(_mm_tri_fwd, _mm_tri_bwd)


def _mm_nt_exact(a, b):
    return lax.dot_general(a, b, (((1,), (1,)), ((), ())), preferred_element_type=F32,
                           precision=lax.Precision.HIGHEST)


def _silu(x):
    return x * jax.nn.sigmoid(x)


def _softplus(z):
    return jnp.maximum(z, 0.0) + jnp.log(1.0 + jnp.exp(-jnp.abs(z)))


def _pick(dim, pref):
    best = None
    t = LANE
    while t <= min(dim, pref):
        if dim % t == 0:
            best = t
        t += LANE
    return best if best is not None else dim


def _const_map(n):
    return lambda *_: (0,) * n


MM_BLOCK_BYTES = 8 * 1024 * 1024


def _mm_tiles(mode, M, N, K, a_bytes, b_bytes):
    if mode == "nn":
        tk = K if K <= 2048 else _pick(K, 1024)
        tm = _pick(M, max(512, MM_BLOCK_BYTES // (tk * a_bytes)))
        tn = _pick(N, 512)
    elif mode == "tn":
        tk = K if K <= 4096 else _pick(K, 1024)
        tm = _pick(M, MM_BLOCK_BYTES // (tk * a_bytes))
        tn = _pick(N, MM_BLOCK_BYTES // (tk * b_bytes))
    else:
        tk = _pick(K, 1024)
        tn = _pick(N, 1024)
        tm = _pick(M, MM_BLOCK_BYTES // (8 * tn))
    return tm, tn, tk


def matmul(name, a, b, mode="nn", res=None, out_dtype=F32):
    if mode == "tn":
        K, M = a.shape
    else:
        M, K = a.shape
    N = b.shape[0] if mode == "nt" else b.shape[1]
    tm, tn, tk = _mm_tiles(mode, M, N, K, a.dtype.itemsize, b.dtype.itemsize)
    nk = K // tk
    a_spec = (pl.BlockSpec((tk, tm), lambda i, j, k: (k, i)) if mode == "tn"
              else pl.BlockSpec((tm, tk), lambda i, j, k: (i, k)))
    b_spec = (pl.BlockSpec((tn, tk), lambda i, j, k: (j, k)) if mode == "nt"
              else pl.BlockSpec((tk, tn), lambda i, j, k: (k, j)))
    dot = {"nn": _mm, "nt": _mm_nt, "tn": _mm_tn}[mode]
    has_res = res is not None

    def body(*refs):
        a_ref, b_ref = refs[0], refs[1]
        r_ref = refs[2] if has_res else None
        o_ref = refs[3] if has_res else refs[2]

        def finish(out):
            if has_res:
                out = out + r_ref[...].astype(F32)
            o_ref[...] = out.astype(o_ref.dtype)

        if nk == 1:
            finish(dot(a_ref[...], b_ref[...]))
            return
        acc = refs[-1]
        k = pl.program_id(2)

        @pl.when(k == 0)
        def _():
            acc[...] = jnp.zeros_like(acc)

        acc[...] += dot(a_ref[...], b_ref[...])

        @pl.when(k == nk - 1)
        def _():
            finish(acc[...])

    in_specs = [a_spec, b_spec]
    args = [a, b]
    if has_res:
        in_specs.append(pl.BlockSpec((tm, tn), lambda i, j, k: (i, j)))
        args.append(res)
    return pl.pallas_call(
        body, name=name, grid=(M // tm, N // tn, nk), in_specs=in_specs,
        out_specs=pl.BlockSpec((tm, tn), lambda i, j, k: (i, j)),
        out_shape=jax.ShapeDtypeStruct((M, N), out_dtype),
        scratch_shapes=[pltpu.VMEM((tm, tn), F32)] if nk > 1 else [],
        compiler_params=_cp(("parallel", "parallel", "arbitrary")),
    )(*args)


def call_with_comm(body, comm, *, name, grid, in_specs, out_specs, out_shape, scratch_shapes, semantics, args):
    if comm is None:
        outs = pl.pallas_call(body, name=name, grid=grid, in_specs=in_specs, out_specs=out_specs, out_shape=out_shape,
                              scratch_shapes=scratch_shapes, compiler_params=_cp(semantics))(*args)
        return list(outs), []
    n_in, n_out, n_scr = len(in_specs), len(out_specs), len(scratch_shapes)
    c_in, c_out = len(comm["ins"]), len(comm["out_shape"])
    total = int(np.prod(grid))
    mid_step = (2 * total) // 3

    def wrapped(*refs):
        p = 0
        ins = refs[p:p + n_in]
        p += n_in
        cins = refs[p:p + c_in]
        p += c_in
        outs = refs[p:p + n_out]
        p += n_out
        couts = refs[p:p + c_out]
        p += c_out
        scr = refs[p:p + n_scr]
        send, recv = refs[p + n_scr], refs[p + n_scr + 1]
        step = pl.program_id(0)
        for ax in range(1, len(grid)):
            step = step * grid[ax] + pl.program_id(ax)

        @pl.when(step == 0)
        def _():
            comm["start"](cins, couts, send, recv)

        body(*ins, *outs, *scr)
        if comm["mid"] is not None:
            @pl.when(step == mid_step)
            def _():
                comm["mid"](cins, couts, send, recv)

        @pl.when(step == total - 1)
        def _():
            comm["finish"](cins, couts, send, recv)

    outs = pl.pallas_call(
        wrapped, name=name, grid=grid, in_specs=list(in_specs) + [HBM] * c_in,
        out_specs=list(out_specs) + [HBM] * c_out, out_shape=list(out_shape) + list(comm["out_shape"]),
        scratch_shapes=list(scratch_shapes) + [pltpu.SemaphoreType.DMA((comm["n_sems"],))] * 2,
        compiler_params=_cp(("arbitrary",) * len(grid)),
    )(*args, *comm["ins"])
    return list(outs[:n_out]), list(outs[n_out:])


def _col(arr, width, idx):
    return (arr, width, idx)


def _perm(arr, dil, width=None, idx=0):
    return (arr, arr.shape[1] if width is None else width, idx, dil)


def _from_perm(ref, scr, dil):
    n, w = ref.shape[1], ref.shape[2]
    for r in range(dil):
        for j in range(w // LANE):
            scr[j, pl.ds(r, n, stride=dil), :] = ref[r, :, j * LANE:(j + 1) * LANE].astype(F32)
    return jnp.concatenate([scr[j] for j in range(w // LANE)], axis=1)


def _to_perm(val, ref, scr, dil):
    n, w = ref.shape[1], ref.shape[2]
    for j in range(w // LANE):
        scr[j] = val[:, j * LANE:(j + 1) * LANE].astype(F32)
    for r in range(dil):
        ref[r] = jnp.concatenate([scr[j, pl.ds(r, n, stride=dil), :] for j in range(w // LANE)], axis=1).astype(ref.dtype)


def rowcall(name, fn, rows, bcs, row_outs, bc_outs=(), tb=256, halo=()):
    rows = [r if isinstance(r, tuple) else (r, r.shape[1], 0) for r in rows]
    rows = [r if len(r) == 4 else r + (1,) for r in rows]
    row_outs = [o if len(o) == 3 else o + (1,) for o in row_outs]
    S = rows[0][0].shape[0]
    tb = min(tb, S)
    nb = S // tb
    n_r, n_h, n_b, n_ro, n_bo = len(rows), len(halo), len(bcs), len(row_outs), len(bc_outs)
    hb = tb // SUBLANE
    last = S // SUBLANE - 1
    perm_w = max([w for (_, w, _, d) in rows if d > 1] + [w for (w, _, d) in row_outs if d > 1] + [0])

    def body(*refs):
        i = pl.program_id(0)
        scr = refs[-1] if perm_w else None
        pos = 0
        r_in = [r[...] if d == 1 else _from_perm(r, scr, d) for r, (_, _, _, d) in zip(refs[pos:pos + n_r], rows)]
        pos += n_r
        h_in = []
        for _ in range(n_h):
            prev = refs[pos][...] * (i > 0).astype(F32)
            nxt = refs[pos + 1][...] * (i < nb - 1).astype(F32)
            h_in += [prev, nxt]
            pos += 2
        b_in = [r[...] for r in refs[pos:pos + n_b]]
        pos += n_b
        ro = refs[pos:pos + n_ro]
        bo = refs[pos + n_ro:pos + n_ro + n_bo]
        outs_r, outs_b = fn(*r_in, *h_in, *b_in)
        for ref, val, (_, _, d) in zip(ro, outs_r, row_outs, strict=True):
            if d == 1:
                ref[...] = val.astype(ref.dtype)
            else:
                _to_perm(val, ref, scr, d)
        if n_bo:
            @pl.when(i == 0)
            def _():
                for ref in bo:
                    ref[...] = jnp.zeros_like(ref)

            for ref, val in zip(bo, outs_b, strict=True):
                ref[...] += val

    in_specs, args = [], []
    for (a, w, c, d) in rows:
        if d == 1:
            in_specs.append(pl.BlockSpec((tb, w), functools.partial(lambda i, c: (i, c), c=c)))
            args.append(a)
        else:
            in_specs.append(pl.BlockSpec((d, tb // d, w), functools.partial(lambda i, c: (0, i, c), c=c)))
            args.append(a.reshape(d, S // d, a.shape[1]))
    for h in halo:
        a, w, c, _ = rows[h]
        in_specs.append(pl.BlockSpec((SUBLANE, w), functools.partial(
            lambda i, c: (jnp.maximum(i * hb - 1, 0), c), c=c)))
        in_specs.append(pl.BlockSpec((SUBLANE, w), functools.partial(
            lambda i, c: (jnp.minimum((i + 1) * hb, last), c), c=c)))
        args += [a, a]
    for b in bcs:
        in_specs.append(pl.BlockSpec(b.shape, _const_map(b.ndim)))
        args.append(b)
    out_specs, out_shape = [], []
    for (w, dt, d) in row_outs:
        if d == 1:
            out_specs.append(pl.BlockSpec((tb, w), lambda i: (i, 0)))
            out_shape.append(jax.ShapeDtypeStruct((S, w), dt))
        else:
            out_specs.append(pl.BlockSpec((d, tb // d, w), lambda i: (0, i, 0)))
            out_shape.append(jax.ShapeDtypeStruct((d, S // d, w), dt))
    for shp in bc_outs:
        out_specs.append(pl.BlockSpec(shp, _const_map(len(shp))))
        out_shape.append(jax.ShapeDtypeStruct(shp, F32))
    outs = pl.pallas_call(
        body, name=name, grid=(nb,), in_specs=in_specs, out_specs=out_specs, out_shape=out_shape,
        scratch_shapes=[pltpu.VMEM((perm_w // LANE, tb, LANE), F32)] if perm_w else [],
        compiler_params=_cp(("arbitrary",) if n_bo else ("parallel",)),
    )(*args)
    row_res = [o if d == 1 else o.reshape(S, w) for o, (w, _, d) in zip(outs[:n_ro], row_outs)]
    return row_res, list(outs[n_ro:])


def smallcall(name, fn, ins, out_shapes):
    n_in = len(ins)

    def body(*refs):
        outs = fn(*[r[...] for r in refs[:n_in]])
        for ref, val in zip(refs[n_in:], outs, strict=True):
            ref[...] = val.astype(ref.dtype)

    return pl.pallas_call(
        body, name=name, out_shape=[jax.ShapeDtypeStruct(s, F32) for s in out_shapes],
        compiler_params=_cp(),
    )(*ins)


def _rms(x, g):
    return x * lax.rsqrt(jnp.mean(x * x, axis=-1, keepdims=True) + EPS) * g


def _rms_groups(y, g, width):
    outs = []
    for j in range(y.shape[1] // width):
        sl = slice(j * width, (j + 1) * width)
        outs.append(_rms(y[:, sl], g[:, sl]))
    return jnp.concatenate(outs, axis=1)


def norm_fwd(name, x, g, dils=(1,)):
    outs, _ = rowcall(name, lambda x, g: ((_rms(x, g),) * len(dils), ()), [x], [g],
                      [(D_MODEL, BF16, d) for d in dils], tb=512)
    return outs[0] if len(dils) == 1 else tuple(outs)


def norm_bwd(name, x, g, dhn, dres, dils=(1,)):
    parts = dhn if isinstance(dhn, tuple) else (dhn,)
    n = len(parts)

    def fn(x, *rest):
        dh = functools.reduce(lambda a, b: a + b, rest[:n])
        _, vjp = jax.vjp(_rms, x, rest[n + 1])
        dx, dg = vjp(dh)
        return (dx + rest[n],), (dg,)

    rows = [x] + [a if d == 1 else _perm(a, d) for a, d in zip(parts, dils)] + [dres]
    (dx,), (dg,) = rowcall(name, fn, rows, [g], [(D_MODEL, F32)], [(1, D_MODEL)], tb=512)
    return dx, dg


def loss_head(x, tgt, g):
    def fn(x, tgt, g):
        y, vjp = jax.vjp(_rms, x, g)
        diff = y - tgt
        loss = 0.5 * jnp.sum(jnp.mean(diff * diff, axis=-1, keepdims=True), axis=0, keepdims=True)
        dx, dg = vjp(diff * (1.0 / D_MODEL))
        return (dx,), (jnp.broadcast_to(loss, (1, LANE)), dg)

    (dx,), (loss, dg) = rowcall("loss_head", fn, [x, tgt], [g], [(D_MODEL, F32)],
                                [(1, LANE), (1, D_MODEL)], tb=512)
    return loss, dx, dg


def _shift_rows(x, s):
    if s == 0:
        return x
    return pltpu.roll(x, (-s) % x.shape[0], 0)


def _conv_ext(x, prev, nxt, w):
    xe = jnp.concatenate([prev, x, nxt], axis=0)
    pad = SSD_CONV // 2
    c = jnp.zeros_like(xe)
    for k in range(SSD_CONV):
        c = c + w[k:k + 1, :] * _shift_rows(xe, k - pad)
    return xe, c


def ssd_conv_fwd(u, conv_w, conv_b):
    def fn(x0, x1, x2, p0, n0, p1, n1, p2, n2, w, b):
        tb = x0.shape[0]
        outs = []
        for j, (x, p, n) in enumerate(((x0, p0, n0), (x1, p1, n1), (x2, p2, n2))):
            sl = slice(j * 1024, (j + 1) * 1024)
            _, c = _conv_ext(x, p, n, w[:, sl])
            outs.append(_silu(c[SUBLANE:SUBLANE + tb] + b[:, sl]))
        return (jnp.concatenate(outs, axis=1),), ()

    (xbc,), _ = rowcall("ssd_conv_fwd", fn, [_col(u, 1024, 2), _col(u, 1024, 3), _col(u, 1024, 4)],
                        [conv_w, conv_b], [(SSD_CONV_CH, F32)], tb=256, halo=(0, 1, 2))
    return xbc


def ssd_conv_bwd(u, dxbc, dz, conv_w, conv_b):
    pad = SSD_CONV // 2

    def fn(x0, x1, x2, g0, g1, g2, dz, xp0, xn0, xp1, xn1, xp2, xn2, gp0, gn0, gp1, gn1, gp2, gn2, w, b):
        tb = x0.shape[0]
        blk = slice(SUBLANE, SUBLANE + tb)
        dpre, dws, dbs = [], [], []
        xs = ((x0, xp0, xn0), (x1, xp1, xn1), (x2, xp2, xn2))
        gs = ((g0, gp0, gn0), (g1, gp1, gn1), (g2, gp2, gn2))
        for j in range(3):
            sl = slice(j * 1024, (j + 1) * 1024)
            wj = w[:, sl]
            xe, c = _conv_ext(*xs[j], wj)
            ce = c + b[:, sl]
            sig = jax.nn.sigmoid(ce)
            ge = jnp.concatenate([gs[j][1], gs[j][0], gs[j][2]], axis=0)
            dce = ge * (sig * (1.0 + ce * (1.0 - sig)))
            dx = jnp.zeros_like(xe)
            dw_rows = []
            for k in range(SSD_CONV):
                dx = dx + wj[k:k + 1, :] * _shift_rows(dce, pad - k)
                dw_rows.append(jnp.sum(dce[blk] * _shift_rows(xe, k - pad)[blk], axis=0, keepdims=True))
            dw_rows.append(jnp.zeros_like(dw_rows[0]))
            dpre.append(dx[blk])
            dws.append(jnp.concatenate(dw_rows, axis=0))
            dbs.append(jnp.sum(dce[blk], axis=0, keepdims=True))
        du = jnp.concatenate([dz] + dpre, axis=1)
        return (du,), (jnp.concatenate(dws, axis=1), jnp.concatenate(dbs, axis=1))

    rows = [_col(u, 1024, 2), _col(u, 1024, 3), _col(u, 1024, 4),
            _col(dxbc, 1024, 0), _col(dxbc, 1024, 1), _col(dxbc, 1024, 2), dz]
    (du,), (dw, db) = rowcall("ssd_conv_bwd", fn, rows, [conv_w, conv_b], [(SSD_MAIN, BF16)],
                              [(SUBLANE, SSD_CONV_CH), (1, SSD_CONV_CH)], tb=128, halo=(0, 1, 2, 3, 4, 5))
    return du, dw, db


def _expand_heads(v):
    return jnp.concatenate([jnp.broadcast_to(v[:, j:j + 1], (v.shape[0], SSD_HEADDIM)) for j in range(SSD_HPG)], axis=1)


def _ssd_chunk(rev, st_in, xs, udt, dtb, alog, B, C):
    Q = B.shape[0]
    P = SSD_HEADDIM
    dt = _softplus(udt + dtb)
    a = dt * (-jnp.exp(alog))
    r = lax.broadcasted_iota(jnp.int32, (Q, Q), 0)
    c = lax.broadcasted_iota(jnp.int32, (Q, Q), 1)
    mask = (r <= c) if rev else (r >= c)
    p = _mm_tri(mask, a)
    pT = p.T
    p_e = _expand_heads(p)
    tot_e = p_e[0:1] if rev else p_e[Q - 1:Q]
    xdt = xs * _expand_heads(dt)
    CB = _mm_nt(C, B)
    ys = []
    for j in range(SSD_HPG):
        L = jnp.exp(jnp.where(mask, p[:, j:j + 1] - pT[j:j + 1, :], NEG_BIG))
        ys.append(_mm(CB * L, xdt[:, j * P:(j + 1) * P]))
    y = jnp.concatenate(ys, axis=1) + _mm(C, st_in) * jnp.exp(p_e)
    st_out = st_in * jnp.exp(tot_e) + _mm_tn(B, xdt * jnp.exp(tot_e - p_e))
    return y, st_out


def _ssd_specs(nc, rev_order):
    Q = SSD_CHUNK
    N, P, H, GS = SSD_STATE, SSD_HEADDIM, SSD_HPG, SSD_GPS
    gw = H * P
    nbc = SSD_GROUPS // GS

    def cidx(s):
        return nc - 1 - s if rev_order else s

    xs = pl.BlockSpec((Q, GS * gw), lambda g, s: (cidx(s), g))
    Bs = pl.BlockSpec((Q, GS * N), lambda g, s: (cidx(s), SSD_DI // (GS * N) + g))
    Cs = pl.BlockSpec((Q, GS * N), lambda g, s: (cidx(s), SSD_DI // (GS * N) + nbc + g))
    BC_out = pl.BlockSpec((Q, GS * N), lambda g, s: (cidx(s), g))
    udt = pl.BlockSpec((GS, Q, H), lambda g, s: (g, cidx(s), 0))
    small = pl.BlockSpec((GS, 1, H), lambda g, s: (g, 0, 0))
    st = pl.BlockSpec((GS, 1, N, gw), lambda g, s: (g, cidx(s), 0, 0))
    return xs, Bs, Cs, BC_out, udt, small, st


def ssd_scan_fwd(name, xbc, udt, dtb, alog, rev, comm=None):
    S = xbc.shape[0]
    Q, N, P, H, GS = SSD_CHUNK, SSD_STATE, SSD_HEADDIM, SSD_HPG, SSD_GPS
    gw = H * P
    nc = S // Q
    xs_s, B_s, C_s, _, udt_s, small_s, st_s = _ssd_specs(nc, rev)

    def body(xs_ref, B_ref, C_ref, udt_ref, dtb_ref, alog_ref, y_ref, st_ref, state):
        @pl.when(pl.program_id(1) == 0)
        def _():
            state[...] = jnp.zeros_like(state)

        for g in range(GS):
            st_ref[g, 0] = state[g]
            y, st_out = _ssd_chunk(rev, state[g], xs_ref[:, g * gw:(g + 1) * gw], udt_ref[g], dtb_ref[g], alog_ref[g],
                                   B_ref[:, g * N:(g + 1) * N], C_ref[:, g * N:(g + 1) * N])
            y_ref[:, g * gw:(g + 1) * gw] = y
            state[g] = st_out

    (y, st), got = call_with_comm(
        body, comm, name=name, grid=(SSD_GROUPS // GS, nc),
        in_specs=[xs_s, B_s, C_s, udt_s, small_s, small_s],
        out_specs=[xs_s, st_s],
        out_shape=[jax.ShapeDtypeStruct((S, SSD_DI), F32),
                   jax.ShapeDtypeStruct((SSD_GROUPS, nc, N, gw), F32)],
        scratch_shapes=[pltpu.VMEM((GS, N, gw), F32)],
        semantics=("parallel", "arbitrary"), args=(xbc, xbc, xbc, udt, dtb, alog))
    return y, st, got


def ssd_scan_bwd(name, xbc, udt, dtb, alog, states, dy, rev):
    S = xbc.shape[0]
    Q, N, P, H, GS = SSD_CHUNK, SSD_STATE, SSD_HEADDIM, SSD_HPG, SSD_GPS
    gw = H * P
    nc = S // Q
    xs_s, B_s, C_s, BC_out, udt_s, small_s, st_s = _ssd_specs(nc, not rev)

    def body(xs_ref, B_ref, C_ref, udt_ref, dtb_ref, alog_ref, st_ref, dy_ref,
             dx_ref, dB_ref, dC_ref, dudt_ref, ddtb_ref, dalog_ref, dstate):
        @pl.when(pl.program_id(1) == 0)
        def _():
            dstate[...] = jnp.zeros_like(dstate)
            ddtb_ref[...] = jnp.zeros_like(ddtb_ref)
            dalog_ref[...] = jnp.zeros_like(dalog_ref)

        for g in range(GS):
            cols, bc = slice(g * gw, (g + 1) * gw), slice(g * N, (g + 1) * N)
            _, vjp = jax.vjp(functools.partial(_ssd_chunk, rev), st_ref[g, 0], xs_ref[:, cols], udt_ref[g], dtb_ref[g],
                             alog_ref[g], B_ref[:, bc], C_ref[:, bc])
            dst_in, dxs, dudt, ddtb, dalog, dB, dC = vjp((dy_ref[:, cols], dstate[g]))
            dx_ref[:, cols] = dxs
            dB_ref[:, bc] = dB
            dC_ref[:, bc] = dC
            dudt_ref[g] = dudt
            ddtb_ref[g] += ddtb
            dalog_ref[g] += dalog
            dstate[g] = dst_in

    return pl.pallas_call(
        body, name=name, grid=(SSD_GROUPS // GS, nc),
        in_specs=[xs_s, B_s, C_s, udt_s, small_s, small_s, st_s, xs_s],
        out_specs=[xs_s, BC_out, BC_out, udt_s, small_s, small_s],
        out_shape=[jax.ShapeDtypeStruct((S, SSD_DI), F32),
                   jax.ShapeDtypeStruct((S, SSD_GROUPS * N), F32),
                   jax.ShapeDtypeStruct((S, SSD_GROUPS * N), F32),
                   jax.ShapeDtypeStruct((SSD_GROUPS, S, H), F32),
                   jax.ShapeDtypeStruct((SSD_GROUPS, 1, H), F32),
                   jax.ShapeDtypeStruct((SSD_GROUPS, 1, H), F32)],
        scratch_shapes=[pltpu.VMEM((GS, N, gw), F32)],
        compiler_params=_cp(("parallel", "arbitrary")),
    )(xbc, xbc, xbc, udt, dtb, alog, states, dy)


def _ssd_combine(yf, yb, xs, z, dexp, ng):
    y = (yf + yb + xs * dexp) * _silu(z)
    return _rms_groups(y, ng, SSD_DI // SSD_GROUPS)


def ssd_forward(x, hn, w, comm=None):
    comm = comm or {}
    S = x.shape[0]
    u = matmul("ssd_in", hn, w["ssd_w_main"])
    udt = matmul("ssd_in_dt", hn, w["ssd_w_dt"])
    xbc = ssd_conv_fwd(u, w["ssd_conv_w8"], w["ssd_conv_b"])
    udt_t = udt.reshape(S, 2, SSD_GROUPS, SSD_HPG).transpose(1, 2, 0, 3)
    dtb = w["ssd_dt_bias"].reshape(2, SSD_GROUPS, 1, SSD_HPG)
    alog = w["ssd_a_log"].reshape(2, SSD_GROUPS, 1, SSD_HPG)
    yf, stf, got_f = ssd_scan_fwd("ssd_scan_f", xbc, udt_t[0], dtb[0], alog[0], False, comm.get("ssd_scan_f"))
    yb, stb, got_b = ssd_scan_fwd("ssd_scan_b", xbc, udt_t[1], dtb[1], alog[1], True, comm.get("ssd_scan_b"))
    dexp = jnp.repeat(w["ssd_d"].reshape(1, SSD_HEADS), SSD_HEADDIM, axis=1)
    (yn,), _ = rowcall("ssd_combine", lambda yf, yb, xs, z, d, g: ((_ssd_combine(yf, yb, xs, z, d, g),), ()),
                       [yf, yb, _col(xbc, SSD_DI, 0), _col(u, SSD_DI, 0)], [dexp, w["ssd_norm_g"]],
                       [(SSD_DI, BF16)], tb=256)
    out = matmul("ssd_out", yn, w["ssd_w_out"], res=x)
    saved = dict(hn=hn, u=u, xbc=xbc, udt_t=udt_t, dtb=dtb, alog=alog, yf=yf, yb=yb, stf=stf, stb=stb,
                 dexp=dexp, yn=yn, got=dict(ssd_scan_f=got_f, ssd_scan_b=got_b))
    return out, saved


def ssd_backward(dy, sv, w, comm=None):
    S = dy.shape[0]
    u, xbc = sv["u"], sv["xbc"]
    dyn = matmul("ssd_out_dx", dy, w["ssd_w_out"], mode="nt")
    g_w_out = matmul("ssd_out_dw", sv["yn"], dy, mode="tn")

    def comb_bwd(yf, yb, xs, z, dyn, dexp, ng):
        _, vjp = jax.vjp(_ssd_combine, yf, yb, xs, z, dexp, ng)
        dyf, _, dxs, dz, ddexp, dng = vjp(dyn)
        return (dyf, dxs, dz), (ddexp, dng)

    (dyc, dskip, dz), (ddexp, g_norm) = rowcall(
        "ssd_combine_bwd", comb_bwd, [sv["yf"], sv["yb"], _col(xbc, SSD_DI, 0), _col(u, SSD_DI, 0), dyn],
        [sv["dexp"], w["ssd_norm_g"]], [(SSD_DI, F32)] * 3, [(1, SSD_DI), (1, SSD_DI)], tb=256)
    udt_t, dtb, alog = sv["udt_t"], sv["dtb"], sv["alog"]
    dxf, dBf, dCf, dudt_f, ddtb_f, dalog_f = ssd_scan_bwd("ssd_scan_f_bwd", xbc, udt_t[0], dtb[0], alog[0],
                                                          sv["stf"], dyc, False)
    dxb, dBb, dCb, dudt_b, ddtb_b, dalog_b = ssd_scan_bwd("ssd_scan_b_bwd", xbc, udt_t[1], dtb[1], alog[1],
                                                          sv["stb"], dyc, True)

    def gather(dxf, dxb, dskip, dBf, dBb, dCf, dCb):
        return (jnp.concatenate([dxf + dxb + dskip, dBf + dBb, dCf + dCb], axis=1),), ()

    (dxbc,), _ = rowcall("ssd_dxbc", gather, [dxf, dxb, dskip, dBf, dBb, dCf, dCb], [], [(SSD_CONV_CH, F32)], tb=256)
    du, g_conv_w8, g_conv_b = ssd_conv_bwd(u, dxbc, dz, w["ssd_conv_w8"], w["ssd_conv_b"])
    dudt = jnp.stack([dudt_f, dudt_b]).transpose(2, 0, 1, 3).reshape(S, 2 * SSD_HEADS)
    hn = sv["hn"]
    g_main = matmul("ssd_in_dw", hn, du, mode="tn")
    g_dt = matmul("ssd_in_dt_dw", hn, dudt, mode="tn")
    dhn = matmul("ssd_in_dt_dx", dudt, w["ssd_w_dt"], mode="nt")
    dhn = matmul("ssd_in_dx", du, w["ssd_w_main"], mode="nt", res=dhn)
    grads = dict(
        ssd_w_in=jnp.concatenate([g_main, g_dt], axis=1)[None],
        ssd_conv_w=g_conv_w8[None, :SSD_CONV],
        ssd_conv_b=g_conv_b,
        ssd_dt_bias=jnp.stack([ddtb_f, ddtb_b]).reshape(1, 2, SSD_HEADS),
        ssd_a_log=jnp.stack([dalog_f, dalog_b]).reshape(1, 2, SSD_HEADS),
        ssd_d_exp=ddexp,
        ssd_norm_g=g_norm,
        ssd_w_out=g_w_out[None],
    )
    return dhn, grads


def _hg_block(rev, stTs, uq, uf, ui, lb):
    C = HG_CHUNK
    n = uq.shape[0] // C
    nh = uq.shape[1] // HG_EXPAND
    stTs = list(stTs)
    q = _silu(uq)
    f = lb + (1.0 - lb) * jax.nn.sigmoid(uf)
    k = 1.0 - f
    g = jnp.log(f)
    r = lax.broadcasted_iota(jnp.int32, (C, C), 0)
    c = lax.broadcasted_iota(jnp.int32, (C, C), 1)
    mask = (r <= c) if rev else (r >= c)
    Tm = mask.astype(F32)
    outs = [[None] * n for _ in range(nh)]
    for i in (reversed(range(n)) if rev else range(n)):
        sl = slice(i * C, (i + 1) * C)
        qi, ki, vi = q[sl], k[sl], ui[sl]
        G = _mm_tri(mask, g[sl])
        Gr = G[C // 2:C // 2 + 1]
        Gl = G[0:1] if rev else G[C - 1:C]
        q_in, k_in = qi * jnp.exp(G - Gr), ki * jnp.exp(Gr - G)
        q_st, k_st, e_l = qi * jnp.exp(G), ki * jnp.exp(Gl - G), jnp.exp(Gl)
        for h in range(nh):
            cs = slice(h * HG_EXPAND, (h + 1) * HG_EXPAND)
            att = jnp.where(mask, _mm_nt(q_in[:, cs], k_in[:, cs]), 0.0)
            outs[h][i] = _mm(att, vi[:, cs]) + _mm_nt(q_st[:, cs], stTs[h])
            stTs[h] = stTs[h] * e_l[:, cs] + _mm_tn(vi[:, cs], k_st[:, cs])
    o = jnp.concatenate([jnp.concatenate(outs[h], axis=0) for h in range(nh)], axis=1)
    return o, stTs


def _hg_specs(nb, rev_order, f_col):
    R = HG_ROWS
    gw = HG_HPS * HG_EXPAND
    ng = HG_HEADS // HG_HPS

    def bidx(s):
        return nb - 1 - s if rev_order else s

    def col(base):
        return pl.BlockSpec((R, gw), lambda h, s: (bidx(s), base * ng + h))

    out = pl.BlockSpec((R, gw), lambda h, s: (bidx(s), h))
    lb = pl.BlockSpec((1, gw), lambda h, s: (0, h))
    st = pl.BlockSpec((1, 1, HG_HPS, HG_EXPAND, HG_EXPAND), lambda h, s: (h, bidx(s), 0, 0, 0))
    return col(0), col(f_col), col(3), out, lb, st


def hg_scan_fwd(name, u, lb, rev, comm=None):
    S = u.shape[0]
    nb = S // HG_ROWS
    ng = HG_HEADS // HG_HPS
    q_s, f_s, i_s, o_s, lb_s, st_s = _hg_specs(nb, rev, 2 if rev else 1)

    def body(uq, uf, ui, lb_ref, o_ref, st_ref, state):
        @pl.when(pl.program_id(1) == 0)
        def _():
            state[...] = jnp.zeros_like(state)

        st_ref[0, 0] = state[...]
        o, st = _hg_block(rev, [state[h] for h in range(HG_HPS)], uq[...], uf[...], ui[...], lb_ref[...])
        o_ref[...] = o
        for h in range(HG_HPS):
            state[h] = st[h]

    (o, st), got = call_with_comm(
        body, comm, name=name, grid=(ng, nb), in_specs=[q_s, f_s, i_s, lb_s], out_specs=[o_s, st_s],
        out_shape=[jax.ShapeDtypeStruct((S, HG_W), F32),
                   jax.ShapeDtypeStruct((ng, nb, HG_HPS, HG_EXPAND, HG_EXPAND), F32)],
        scratch_shapes=[pltpu.VMEM((HG_HPS, HG_EXPAND, HG_EXPAND), F32)],
        semantics=("parallel", "arbitrary"), args=(u, u, u, lb))
    return o, st, got


def hg_scan_bwd(name, u, lb, states, do, rev, comm=None):
    S = u.shape[0]
    nb = S // HG_ROWS
    ng = HG_HEADS // HG_HPS
    q_s, f_s, i_s, o_s, lb_s, st_s = _hg_specs(nb, not rev, 2 if rev else 1)

    def body(uq, uf, ui, lb_ref, st_ref, do_ref, dq_ref, df_ref, di_ref, dlb_ref, dstate):
        @pl.when(pl.program_id(1) == 0)
        def _():
            dstate[...] = jnp.zeros_like(dstate)
            dlb_ref[...] = jnp.zeros_like(dlb_ref)

        _, vjp = jax.vjp(functools.partial(_hg_block, rev), [st_ref[0, 0, h] for h in range(HG_HPS)],
                         uq[...], uf[...], ui[...], lb_ref[...])
        dst, dq, df, di, dlb = vjp((do_ref[...], [dstate[h] for h in range(HG_HPS)]))
        dq_ref[...] = dq
        df_ref[...] = df
        di_ref[...] = di
        dlb_ref[...] += dlb
        for h in range(HG_HPS):
            dstate[h] = dst[h]

    outs, got = call_with_comm(
        body, comm, name=name, grid=(ng, nb), in_specs=[q_s, f_s, i_s, lb_s, st_s, o_s],
        out_specs=[o_s, o_s, o_s, lb_s],
        out_shape=[jax.ShapeDtypeStruct((S, HG_W), F32)] * 3 + [jax.ShapeDtypeStruct((1, HG_W), F32)],
        scratch_shapes=[pltpu.VMEM((HG_HPS, HG_EXPAND, HG_EXPAND), F32)],
        semantics=("parallel", "arbitrary"), args=(u, u, u, lb, states, do))
    return (*outs, got)


def _hg_lb(hgrn_lb, layer):
    m = jnp.max(hgrn_lb, axis=0, keepdims=True)
    e = jnp.exp(hgrn_lb - m)
    sm = e / jnp.sum(e, axis=0, keepdims=True)
    lb = jnp.zeros_like(sm[0:1])
    for i in range(1, layer + 1):
        lb = lb + sm[i:i + 1]
    return lb


def _hg_combine(of, ob, gate, ng):
    return _rms_groups(of + ob, ng, HG_EXPAND) * _silu(gate)


def hg_forward(x, hn, w, layer, comm=None):
    comm = comm or {}
    u = matmul("hg_in", hn, w["hg_w_in"])
    (lb,) = smallcall("hg_lb", lambda t: (_hg_lb(t, layer),), [w["hgrn_lb"]], [(1, HG_W)])
    of, stf, got_f = hg_scan_fwd("hg_scan_f", u, lb, False, comm.get("hg_scan_f"))
    ob, stb, _ = hg_scan_fwd("hg_scan_b", u, lb, True)
    (og,), _ = rowcall("hg_combine", lambda of, ob, gate, ng: ((_hg_combine(of, ob, gate, ng),), ()),
                       [of, ob, _col(u, HG_W, 4)], [w["hg_norm_g"]], [(HG_W, BF16)], tb=256)
    out = matmul("hg_out", og, w["hg_w_out"], res=x)
    return out, dict(hn=hn, u=u, lb=lb, of=of, ob=ob, stf=stf, stb=stb, og=og, got=dict(hg_scan_f=got_f))


def hg_backward(dy, sv, w, layer, comm=None):
    comm = comm or {}
    u, lb = sv["u"], sv["lb"]
    dog = matmul("hg_out_dx", dy, w["hg_w_out"], mode="nt")
    g_w_out = matmul("hg_out_dw", sv["og"], dy, mode="tn")

    def comb_bwd(of, ob, gate, dog, ng):
        _, vjp = jax.vjp(_hg_combine, of, ob, gate, ng)
        dof, _, dgate, dng = vjp(dog)
        return (dof, dgate), (dng,)

    (do, dgate), (g_norm,) = rowcall("hg_combine_bwd", comb_bwd, [sv["of"], sv["ob"], _col(u, HG_W, 4), dog],
                                     [w["hg_norm_g"]], [(HG_W, F32)] * 2, [(1, HG_W)], tb=256)
    dqf, dff, dif, dlbf, got_f = hg_scan_bwd("hg_scan_f_bwd", u, lb, sv["stf"], do, False, comm.get("hg_scan_f_bwd"))
    dqb, dfb, dib, dlbb, _ = hg_scan_bwd("hg_scan_b_bwd", u, lb, sv["stb"], do, True)

    def gather(dqf, dqb, dff, dfb, dif, dib, dgate):
        return (jnp.concatenate([dqf + dqb, dff, dfb, dif + dib, dgate], axis=1),), ()

    (du,), _ = rowcall("hg_du", gather, [dqf, dqb, dff, dfb, dif, dib, dgate], [], [(HG_IN, BF16)], tb=256)

    def lb_bwd(t, dlbf, dlbb):
        _, vjp = jax.vjp(lambda t: _hg_lb(t, layer), t)
        return vjp(dlbf + dlbb)

    (g_lb,) = smallcall("hg_lb_bwd", lb_bwd, [w["hgrn_lb"], dlbf, dlbb], [(DEPTH, HG_W)])
    hn = sv["hn"]
    g_w_in = matmul("hg_in_dw", hn, du, mode="tn")
    dhn = matmul("hg_in_dx", du, w["hg_w_in"], mode="nt")
    return dhn, dict(hg_w_in=g_w_in[None], hg_norm_g=g_norm, hg_w_out=g_w_out[None], hgrn_lb=g_lb,
                     got=dict(hg_scan_f_bwd=got_f))


def _rope_tables(S):
    t = np.arange(S)
    row = (t // GRID_W).astype(np.float32)
    col = (t % GRID_W).astype(np.float32)
    inv = (ROPE_THETA ** (-np.arange(0, ROPE_AXIS, 2, dtype=np.float32) / ROPE_AXIS)).astype(np.float32)
    ar = jnp.asarray(row)[:, None] * jnp.asarray(inv)[None, :]
    ac = jnp.asarray(col)[:, None] * jnp.asarray(inv)[None, :]
    cos = jnp.concatenate([jnp.cos(ar), jnp.cos(ar), jnp.cos(ac), jnp.cos(ac)], axis=1)
    sin = jnp.concatenate([-jnp.sin(ar), jnp.sin(ar), -jnp.sin(ac), jnp.sin(ac)], axis=1)
    return cos.astype(F32), sin.astype(F32)


def _rope(x, cos, sin):
    h = ROPE_AXIS // 2
    sw = jnp.concatenate([x[:, h:2 * h], x[:, 0:h], x[:, 3 * h:4 * h], x[:, 2 * h:3 * h]], axis=1)
    return x * cos + sw * sin


def _at_pre(uq, uk, cos, sin, qg, kg):
    qs, ks = [], []
    for h in range(AT_HEADS):
        qs.append(_rope(_rms(uq[:, h * AT_HD:(h + 1) * AT_HD], qg), cos, sin) * (AT_HD ** -0.5))
    for h in range(AT_KV):
        ks.append(_rope(_rms(uk[:, h * AT_HD:(h + 1) * AT_HD], kg), cos, sin))
    return jnp.concatenate(qs, axis=1), jnp.concatenate(ks, axis=1)


def _stack_heads(x):
    return jnp.concatenate([x[:, :AT_HD], x[:, AT_HD:]], axis=0)


def _unstack_heads(x):
    t = x.shape[0] // 2
    return jnp.concatenate([x[:t], x[t:]], axis=1)


def at_flash_fwd(q, k, u):
    S = q.shape[0]
    tq, tk = _pick(S, 512), _pick(S, 4096)
    nq, nk = S // tq, S // tk
    gw = AT_GRP * AT_HD

    def body(q_ref, k_ref, v_ref, o_ref, lse_ref, m_s, l_s, acc):
        j = pl.program_id(2)

        @pl.when(j == 0)
        def _():
            m_s[...] = jnp.full_like(m_s, NEG_BIG)
            l_s[...] = jnp.zeros_like(l_s)
            acc[...] = jnp.zeros_like(acc)

        s = _mm_nt(_stack_heads(q_ref[...]), k_ref[...])
        m_new = jnp.maximum(m_s[...], jnp.max(s, axis=-1, keepdims=True))
        alpha = jnp.exp(m_s[...] - m_new)
        p = jnp.exp(s - m_new)
        l_s[...] = alpha * l_s[...] + jnp.sum(p, axis=-1, keepdims=True)
        acc[...] = alpha * acc[...] + _mm(p, v_ref[...])
        m_s[...] = m_new

        @pl.when(j == nk - 1)
        def _():
            o_ref[...] = _unstack_heads(acc[...] / l_s[...])
            lse = m_s[...] + jnp.log(l_s[...])
            lse_ref[0, 0] = lse[:tq]
            lse_ref[0, 1] = lse[tq:]

    return pl.pallas_call(
        body, name="at_flash_fwd", grid=(AT_KV, nq, nk),
        in_specs=[pl.BlockSpec((tq, gw), lambda h, i, j: (i, h)),
                  pl.BlockSpec((tk, AT_HD), lambda h, i, j: (j, h)),
                  pl.BlockSpec((tk, AT_HD), lambda h, i, j: (j, (AT_QW + AT_KW) // AT_HD + h))],
        out_specs=[pl.BlockSpec((tq, gw), lambda h, i, j: (i, h)),
                   pl.BlockSpec((1, AT_GRP, tq, 1), lambda h, i, j: (h, 0, i, 0))],
        out_shape=[jax.ShapeDtypeStruct((S, AT_QW), F32), jax.ShapeDtypeStruct((AT_KV, AT_GRP, S, 1), F32)],
        scratch_shapes=[pltpu.VMEM((2 * tq, 1), F32), pltpu.VMEM((2 * tq, 1), F32), pltpu.VMEM((2 * tq, AT_HD), F32)],
        compiler_params=_cp(("parallel", "parallel", "arbitrary")),
    )(q, k, u)


def at_flash_bwd(q, k, u, o, lse, do):
    S = q.shape[0]
    tq, tk = _pick(S, 128), _pick(S, 4096)
    nq, nk = S // tq, S // tk
    gw = AT_GRP * AT_HD

    def body(q_ref, k_ref, v_ref, o_ref, lse_ref, do_ref, dq_ref, dk_ref, dv_ref, dk_acc, dv_acc):
        j, i = pl.program_id(1), pl.program_id(2)

        @pl.when(i == 0)
        def _():
            dk_acc[...] = jnp.zeros_like(dk_acc)
            dv_acc[...] = jnp.zeros_like(dv_acc)

        q2 = _stack_heads(q_ref[...])
        do_blk = do_ref[...]
        do2 = _stack_heads(do_blk)
        delta = _stack_heads(do_blk * o_ref[...])
        delta = jnp.sum(delta, axis=-1, keepdims=True)
        kb, vb = k_ref[...], v_ref[...]
        p = jnp.exp(_mm_nt(q2, kb) - jnp.concatenate([lse_ref[0, 0], lse_ref[0, 1]], axis=0))
        dv_acc[...] += _mm_tn(p, do2)
        ds = p * (_mm_nt(do2, vb) - delta)
        dk_acc[...] += _mm_tn(ds, q2)
        dq = _unstack_heads(_mm(ds, kb))
        rows = pl.ds(pl.multiple_of(i * tq, tq), tq)

        @pl.when(j == 0)
        def _():
            dq_ref[rows, :] = dq

        @pl.when(j > 0)
        def _():
            dq_ref[rows, :] += dq

        @pl.when(i == nq - 1)
        def _():
            dk_ref[...] = dk_acc[...]
            dv_ref[...] = dv_acc[...]

    return pl.pallas_call(
        body, name="at_flash_bwd", grid=(AT_KV, nk, nq),
        in_specs=[pl.BlockSpec((tq, gw), lambda h, j, i: (i, h)),
                  pl.BlockSpec((tk, AT_HD), lambda h, j, i: (j, h)),
                  pl.BlockSpec((tk, AT_HD), lambda h, j, i: (j, (AT_QW + AT_KW) // AT_HD + h)),
                  pl.BlockSpec((tq, gw), lambda h, j, i: (i, h)),
                  pl.BlockSpec((1, AT_GRP, tq, 1), lambda h, j, i: (h, 0, i, 0)),
                  pl.BlockSpec((tq, gw), lambda h, j, i: (i, h))],
        out_specs=[pl.BlockSpec((S, gw), lambda h, j, i: (0, h)),
                   pl.BlockSpec((tk, AT_HD), lambda h, j, i: (j, h)),
                   pl.BlockSpec((tk, AT_HD), lambda h, j, i: (j, h))],
        out_shape=[jax.ShapeDtypeStruct((S, AT_QW), F32), jax.ShapeDtypeStruct((S, AT_KW), F32),
                   jax.ShapeDtypeStruct((S, AT_KW), F32)],
        scratch_shapes=[pltpu.VMEM((tk, AT_HD), F32), pltpu.VMEM((tk, AT_HD), F32)],
        compiler_params=_cp(("parallel", "arbitrary", "arbitrary")),
    )(q, k, u, o, lse, do)


def at_forward(x, hn, w, comm=None):
    S = x.shape[0]
    u = matmul("at_in", hn, w["at_w_in"])
    cos, sin = _rope_tables(S)
    (q, k), _ = rowcall("at_pre", lambda uq, uk, c, s, qg, kg: (_at_pre(uq, uk, c, s, qg, kg), ()),
                        [_col(u, AT_QW, 0), _col(u, AT_KW, 2), cos, sin], [w["at_q_norm_g"], w["at_k_norm_g"]],
                        [(AT_QW, BF16), (AT_KW, BF16)], tb=256)
    o, lse = at_flash_fwd(q, k, u)
    (og,), _ = rowcall("at_gate", lambda o, gate: ((o * _silu(gate),), ()), [o, _col(u, AT_QW, 2)], [],
                       [(AT_QW, BF16)], tb=256)
    out = matmul("at_out", og, w["at_w_out"], res=x)
    return out, dict(hn=hn, u=u, cos=cos, sin=sin, q=q, k=k, o=o, lse=lse, og=og)


def at_backward(dy, sv, w, comm=None):
    u = sv["u"]
    dog = matmul("at_out_dx", dy, w["at_w_out"], mode="nt")
    g_w_out = matmul("at_out_dw", sv["og"], dy, mode="tn")

    def gate_bwd(o, gate, dog):
        _, vjp = jax.vjp(lambda o, gate: o * _silu(gate), o, gate)
        return vjp(dog), ()

    (do, dgate), _ = rowcall("at_gate_bwd", gate_bwd, [sv["o"], _col(u, AT_QW, 2), dog], [],
                             [(AT_QW, F32)] * 2, tb=256)
    dq, dk, dv = at_flash_bwd(sv["q"], sv["k"], u, sv["o"], sv["lse"], do)

    def pre_bwd(uq, uk, cos, sin, dq, dk, dv, dgate, qg, kg):
        _, vjp = jax.vjp(lambda uq, uk, qg, kg: _at_pre(uq, uk, cos, sin, qg, kg), uq, uk, qg, kg)
        duq, duk, dqg, dkg = vjp((dq, dk))
        return (jnp.concatenate([duq, duk, dv, dgate], axis=1),), (dqg, dkg)

    (du,), (g_qg, g_kg) = rowcall(
        "at_pre_bwd", pre_bwd, [_col(u, AT_QW, 0), _col(u, AT_KW, 2), sv["cos"], sv["sin"], dq, dk, dv, dgate],
        [w["at_q_norm_g"], w["at_k_norm_g"]], [(AT_IN, BF16)], [(1, AT_HD), (1, AT_HD)], tb=128)
    hn = sv["hn"]
    g_w_in = matmul("at_in_dw", hn, du, mode="tn")
    dhn = matmul("at_in_dx", du, w["at_w_in"], mode="nt")
    return dhn, dict(at_w_in=g_w_in[None], at_q_norm_g=g_qg, at_k_norm_g=g_kg, at_w_out=g_w_out[None])


def _t5_bucket_np(rel):
    half = REL_BUCKETS // 2
    exact = half // 2
    n = np.abs(rel)
    large = exact + (np.log(np.maximum(n, 1).astype(np.float32) / exact)
                     / math.log(REL_MAX_DIST / exact) * (half - exact)).astype(np.int32)
    large = np.minimum(large, half - 1)
    return np.where(rel > 0, half, 0) + np.where(n < exact, n, large)


def _dl_tq(S, dil):
    return min(128, S // dil)


def _dl_bias_maps(tq, dil):
    W = tq + 2 * DL_STEPS
    i = np.arange(tq)[:, None]
    wdx = np.arange(W)[None, :]
    dm = wdx - DL_STEPS - i
    bucket = _t5_bucket_np(dm * dil).reshape(-1).astype(np.int32)
    band = np.where(np.abs(dm) <= DL_STEPS, 0.0, NEG_BIG).reshape(1, -1).astype(np.float32)
    onehot = (jnp.asarray(bucket)[None, :] == jnp.arange(REL_BUCKETS, dtype=jnp.int32)[:, None]).astype(F32)
    return onehot, jnp.asarray(band)


def _dl_attend(q, kwin, vwin, T, valid):
    tq = q.shape[0]
    os, ls = [], []
    for h in range(DL_HEADS):
        sl = slice(h * DL_HD, (h + 1) * DL_HD)
        s = _mm_nt(q[:, sl] * (DL_HD ** -0.5), kwin[:, sl]) + T[h]
        s = jnp.where(valid, s, NEG_BIG)
        m = lax.stop_gradient(jnp.max(s, axis=-1, keepdims=True))
        lse = m + jnp.log(jnp.sum(jnp.exp(s - m), axis=-1, keepdims=True))
        p = jnp.exp(s - lse)
        os.append(_mm(p, vwin[:, sl]))
        ls.append(jnp.broadcast_to(lse, (tq, DL_HD)))
    return jnp.concatenate(os, axis=1), jnp.concatenate(ls, axis=1)


def _dl_specs(tq, Ls):
    nb = Ls // tq
    hs = DL_STEPS
    per = tq // hs
    nh = Ls // hs

    def main(c):
        return pl.BlockSpec((tq, DL_W), lambda r, i: (r * nb + i, c))

    def prev(c):
        return pl.BlockSpec((hs, DL_W), lambda r, i: (r * nh + jnp.maximum(i * per - 1, 0), c))

    def nxt(c):
        return pl.BlockSpec((hs, DL_W), lambda r, i: (r * nh + jnp.minimum((i + 1) * per, nh - 1), c))

    return nb, main, prev, nxt


def _dl_valid(i, tq, Ls):
    W = tq + 2 * DL_STEPS
    mk = i * tq - DL_STEPS + lax.broadcasted_iota(jnp.int32, (1, W), 1)
    return (mk >= 0) & (mk < Ls)


def dl_attn_fwd(gi, dil, u, T):
    S = u.shape[0]
    Ls = S // dil
    tq = _dl_tq(S, dil)
    nb, main, prev, nxt = _dl_specs(tq, Ls)
    out = main(0)

    def body(q_ref, kp, kc, kn, vp, vc, vn, T_ref, o_ref, l_ref):
        kwin = jnp.concatenate([kp[...], kc[...], kn[...]], axis=0)
        vwin = jnp.concatenate([vp[...], vc[...], vn[...]], axis=0)
        o, l = _dl_attend(q_ref[...], kwin, vwin, T_ref[...], _dl_valid(pl.program_id(1), tq, Ls))
        o_ref[...] = o
        l_ref[...] = l

    o, l = pl.pallas_call(
        body, name=f"dl_attn_fwd{gi}", grid=(dil, nb),
        in_specs=[main(0), prev(1), main(1), nxt(1), prev(2), main(2), nxt(2),
                  pl.BlockSpec(T.shape, _const_map(3))],
        out_specs=[out, out],
        out_shape=[jax.ShapeDtypeStruct((S, DL_W), F32)] * 2,
        compiler_params=_cp(("parallel", "parallel")),
    )(u, u, u, u, u, u, u, T)
    return o, l


def dl_attn_bwd(gi, dil, u, T, do, dl, dgate=None):
    S = u.shape[0]
    Ls = S // dil
    tq = _dl_tq(S, dil)
    hs = DL_STEPS
    W = tq + 2 * hs
    nb, main, prev, nxt = _dl_specs(tq, Ls)
    out = main(0)
    win = pl.BlockSpec((1, W, DL_W), lambda r, i: (r * nb + i, 0, 0))

    def body(q_ref, kp, kc, kn, vp, vc, vn, T_ref, do_ref, dl_ref, dq_ref, dkw_ref, dvw_ref, dT_ref):
        first = (pl.program_id(0) == 0) & (pl.program_id(1) == 0)

        @pl.when(first)
        def _():
            dT_ref[...] = jnp.zeros_like(dT_ref)

        kwin = jnp.concatenate([kp[...], kc[...], kn[...]], axis=0)
        vwin = jnp.concatenate([vp[...], vc[...], vn[...]], axis=0)
        valid = _dl_valid(pl.program_id(1), tq, Ls)
        _, vjp = jax.vjp(lambda q, k, v, T: _dl_attend(q, k, v, T, valid), q_ref[...].astype(F32), kwin.astype(F32),
                         vwin.astype(F32), T_ref[...])
        dq, dkw, dvw, dT = vjp((do_ref[...], dl_ref[...]))
        dq_ref[...] = dq
        dkw_ref[0] = dkw
        dvw_ref[0] = dvw
        dT_ref[...] += dT

    dq, dkw, dvw, dT = pl.pallas_call(
        body, name=f"dl_attn_bwd{gi}", grid=(dil, nb),
        in_specs=[main(0), prev(1), main(1), nxt(1), prev(2), main(2), nxt(2),
                  pl.BlockSpec(T.shape, _const_map(3)), out, out],
        out_specs=[out, win, win, pl.BlockSpec(T.shape, _const_map(3))],
        out_shape=[jax.ShapeDtypeStruct((S, DL_W), F32),
                   jax.ShapeDtypeStruct((dil * nb, W, DL_W), F32),
                   jax.ShapeDtypeStruct((dil * nb, W, DL_W), F32),
                   jax.ShapeDtypeStruct(T.shape, F32)],
        compiler_params=_cp(("arbitrary", "arbitrary")),
    )(u, u, u, u, u, u, u, T, do, dl)

    per = tq // hs
    n_out = 3 if dgate is None else 4

    def fold(*refs):
        dq_ref, kc, kp, kn, vc, vp, vn = refs[:7]
        du_ref = refs[-1]
        i = pl.program_id(1)
        has_p = (i > 0).astype(F32)
        has_n = (i < nb - 1).astype(F32)
        du_ref[:, 0:DL_W] = dq_ref[...].astype(BF16)
        for c, (c_ref, p_ref, n_ref) in enumerate(((kc, kp, kn), (vc, vp, vn)), start=1):
            mid = c_ref[0, hs:hs + tq, :]
            top = mid[0:hs] + p_ref[0] * has_p
            bot = mid[tq - hs:tq] + n_ref[0] * has_n
            parts = [top, bot] if tq == 2 * hs else ([top, mid[hs:tq - hs], bot] if tq > 2 * hs else [top + n_ref[0] * has_n])
            du_ref[:, c * DL_W:(c + 1) * DL_W] = jnp.concatenate(parts, axis=0).astype(BF16)
        if dgate is not None:
            du_ref[:, 3 * DL_W:4 * DL_W] = refs[7][...].astype(BF16)

    wfull = pl.BlockSpec((1, W, DL_W), lambda r, i: (r * nb + i, 0, 0))
    wprev = pl.BlockSpec((1, hs, DL_W), lambda r, i: (r * nb + jnp.maximum(i - 1, 0), per + 1, 0))
    wnext = pl.BlockSpec((1, hs, DL_W), lambda r, i: (r * nb + jnp.minimum(i + 1, nb - 1), 0, 0))
    extra_specs, extra_args = ([], []) if dgate is None else ([out], [dgate])
    du = pl.pallas_call(
        fold, name=f"dl_fold{gi}", grid=(dil, nb),
        in_specs=[out, wfull, wprev, wnext, wfull, wprev, wnext] + extra_specs,
        out_specs=pl.BlockSpec((tq, n_out * DL_W), lambda r, i: (r * nb + i, 0)),
        out_shape=jax.ShapeDtypeStruct((S, n_out * DL_W), BF16),
        compiler_params=_cp(("parallel", "parallel")),
    )(dq, dkw, dkw, dkw, dvw, dvw, dvw, *extra_args)
    return du, dT


def _dl_merge(o0, o1, o2, l0, l1, l2, gate):
    m = jnp.maximum(jnp.maximum(l0, l1), l2)
    e0, e1, e2 = jnp.exp(l0 - m), jnp.exp(l1 - m), jnp.exp(l2 - m)
    den = e0 + e1 + e2
    return ((e0 * o0 + e1 * o1 + e2 * o2) / den) * _silu(gate)


DL_DILS = tuple(d for _, d in DL_PAIRS)


def _dl_group_weights(w_in):
    g3 = 3 * DL_W
    return [jnp.concatenate([w_in[:, :g3], w_in[:, 3 * g3:]], axis=1), w_in[:, g3:2 * g3], w_in[:, 2 * g3:3 * g3]]


def dl_forward(x, hns, w, comm=None):
    S = x.shape[0]
    wg = _dl_group_weights(w["dl_w_in"])
    rbT = w["rel_bias"].T
    us, os, ls, Ts, maps = [], [], [], [], []
    for gi, dil in enumerate(DL_DILS):
        u = matmul(f"dl_in{gi}", hns[gi], wg[gi], out_dtype=F32 if gi == 0 else BF16)
        tq = _dl_tq(S, dil)
        W = tq + 2 * DL_STEPS
        onehot, band = _dl_bias_maps(tq, dil)
        (T,) = smallcall(f"dl_bias{gi}", lambda rbT, oh, band: (_mm_exact(rbT, oh) + band,), [rbT, onehot, band],
                         [(DL_HEADS, tq * W)])
        T = T.reshape(DL_HEADS, tq, W)
        o, l = dl_attn_fwd(gi, dil, u, T)
        us.append(u)
        os.append(o)
        ls.append(l)
        Ts.append(T)
        maps.append(onehot)
    rows = [a if d == 1 else _perm(a, d) for a, d in zip(os + ls, DL_DILS * 2)] + [_col(us[0], DL_W, 3)]
    (og,), _ = rowcall("dl_merge", lambda *a: ((_dl_merge(*a),), ()), rows, [], [(DL_W, BF16)], tb=256)
    out = matmul("dl_out", og, w["dl_w_out"], res=x)
    return out, dict(hns=hns, us=us, os=os, ls=ls, Ts=Ts, maps=maps, og=og, wg=wg)


def dl_backward(dy, sv, w, comm=None):
    us = sv["us"]
    dog = matmul("dl_out_dx", dy, w["dl_w_out"], mode="nt")
    g_w_out = matmul("dl_out_dw", sv["og"], dy, mode="tn")

    def merge_bwd(o0, o1, o2, l0, l1, l2, gate, dog):
        _, vjp = jax.vjp(_dl_merge, o0, o1, o2, l0, l1, l2, gate)
        return vjp(dog), ()

    rows = [a if d == 1 else _perm(a, d) for a, d in zip(sv["os"] + sv["ls"], DL_DILS * 2)] + [_col(us[0], DL_W, 3), dog]
    grads7, _ = rowcall("dl_merge_bwd", merge_bwd, rows, [], [(DL_W, F32, d) for d in DL_DILS * 2] + [(DL_W, F32)], tb=256)
    dos, dls, dgate = grads7[0:3], grads7[3:6], grads7[6]
    g_rbT, g_ws, dhns = None, [], []
    for gi, dil in enumerate(DL_DILS):
        du, dT = dl_attn_bwd(gi, dil, us[gi], sv["Ts"][gi], dos[gi], dls[gi], dgate if gi == 0 else None)
        (g,) = smallcall(f"dl_bias_bwd{gi}", lambda dT, oh: (_mm_nt_exact(dT, oh),),
                         [dT.reshape(DL_HEADS, -1), sv["maps"][gi]], [(DL_HEADS, REL_BUCKETS)])
        g_rbT = g if g_rbT is None else g_rbT + g
        g_ws.append(matmul(f"dl_in_dw{gi}", sv["hns"][gi], du, mode="tn"))
        dhns.append(matmul(f"dl_in_dx{gi}", du, sv["wg"][gi], mode="nt"))
    g3 = 3 * DL_W
    g_w_in = jnp.concatenate([g_ws[0][:, :g3], g_ws[1], g_ws[2], g_ws[0][:, g3:]], axis=1)
    return tuple(dhns), dict(dl_w_in=g_w_in[None], dl_w_out=g_w_out[None], rel_bias=g_rbT.T)


_FWD = (ssd_forward, hg_forward, at_forward, dl_forward)
_BWD = (ssd_backward, hg_backward, at_backward, dl_backward)


def _norm_dils(layer):
    return DL_DILS if layer % 4 == 3 else (1,)


class NoExchange:
    def fwd_plans(self, layer, w):
        return None

    def fwd_done(self, layer, got, w):
        pass

    def bwd_plans(self, layer, grads):
        return None

    def bwd_done(self, layer, got):
        pass


def local_step(x, tgt, w, sched=None):
    sched = sched or NoExchange()
    saved = []
    h = x
    for layer in range(DEPTH):
        hn = norm_fwd(f"norm{layer}", h, w["norm_g"][layer:layer + 1], _norm_dils(layer))
        extra = (layer,) if layer % 4 == 1 else ()
        h_next, sv = _FWD[layer % 4](h, hn, w, *extra, comm=sched.fwd_plans(layer, w))
        sched.fwd_done(layer, sv.get("got", {}), w)
        saved.append((h, sv))
        h = h_next
    loss, dh, g_final = loss_head(h, tgt, w["final_g"].reshape(1, D_MODEL))
    grads = {}
    g_norm = [None] * DEPTH
    for layer in reversed(range(DEPTH)):
        h_in, sv = saved[layer]
        extra = (layer,) if layer % 4 == 1 else ()
        dhn, g = _BWD[layer % 4](dh, sv, w, *extra, comm=sched.bwd_plans(layer, grads))
        sched.bwd_done(layer, g.pop("got", {}))
        grads.update(g)
        dh, g_norm[layer] = norm_bwd(f"norm{layer}_bwd", h_in, w["norm_g"][layer:layer + 1], dhn, dh, _norm_dils(layer))
    grads["norm_g"] = jnp.concatenate(g_norm, axis=0)
    grads["final_g"] = g_final.reshape(D_MODEL)
    grads["ssd_d"] = jnp.sum(grads.pop("ssd_d_exp").reshape(SSD_HEADS, SSD_HEADDIM), axis=1)[None]
    return loss, dh, grads


IN_NAMES = ("ssd_w_in", "hg_w_in", "at_w_in", "dl_w_in")
OUT_NAMES = ("ssd_w_out", "hg_w_out", "at_w_out", "dl_w_out")
IN_COLS = (SSD_IN // 4, HG_IN // 4, AT_IN // 4, DL_IN // 4)
OUT_ROWS = (SSD_DI // 4, HG_W // 4, AT_QW // 4, DL_W // 4)
PACK_IN = sum(IN_COLS)
PACK_OUT = sum(OUT_ROWS)
N_CHIPS = 4
N_DEV = 8
HBM = pl.BlockSpec(memory_space=pl.ANY)


def _mesh_pos():
    return lax.axis_index("x"), lax.axis_index("y"), lax.axis_index("c")


def _other_chips(x, y):
    return [(1 - x, y), (x, 1 - y), (1 - x, 1 - y)]


def _half_rows(half, n):
    return pl.ds(pl.multiple_of(half * n, n), n)


def _remote(src, dst, send, recv, k, to):
    return pltpu.make_async_remote_copy(src_ref=src, dst_ref=dst, send_sem=send.at[k], recv_sem=recv.at[k],
                                        device_id=to, device_id_type=MESH)


def gather_plan(packs, whole=()):
    arrs = list(packs) + list(whole)
    n_half = len(packs)

    def pieces(ins, outs):
        x, y, c = _mesh_pos()
        for a, (src, dst) in enumerate(zip(ins, outs)):
            h = src.shape[0] // 2 if a < n_half else None
            for j, (px, py) in enumerate(_other_chips(x, y)):
                yield a, j, src, dst, h, (x, y, c), (px, py)

    def start(ins, outs, send, recv):
        for a, j, src, dst, h, (x, y, c), (px, py) in pieces(ins, outs):
            me = 2 * x + y
            if h is None:
                _remote(src, dst.at[me], send, recv, 6 * a + j, (px, py, c)).start()
            else:
                _remote(src.at[_half_rows(c, h)], dst.at[me, _half_rows(c, h)], send, recv, 6 * a + j, (px, py, c)).start()

    def mid(ins, outs, send, recv):
        for a, j, src, dst, h, (x, y, c), (px, py) in pieces(ins, outs):
            kp = 2 * px + py
            if h is None:
                _remote(src, dst.at[kp], send, recv, 6 * a + j, (px, py, c)).wait_recv()
            else:
                got = dst.at[kp, _half_rows(c, h)]
                _remote(src.at[_half_rows(c, h)], got, send, recv, 6 * a + j, (px, py, c)).wait_recv()
                _remote(got, got, send, recv, 6 * a + 3 + j, (x, y, 1 - c)).start()

    def finish(ins, outs, send, recv):
        for a, j, src, dst, h, (x, y, c), (px, py) in pieces(ins, outs):
            me, kp = 2 * x + y, 2 * px + py
            if h is None:
                _remote(src, dst.at[me], send, recv, 6 * a + j, (px, py, c)).wait_send()
            else:
                theirs = dst.at[kp, _half_rows(1 - c, h)]
                _remote(theirs, theirs, send, recv, 6 * a + 3 + j, (x, y, 1 - c)).wait_recv()
                _remote(src.at[_half_rows(c, h)], dst.at[me, _half_rows(c, h)], send, recv, 6 * a + j, (px, py, c)).wait_send()
                mine = dst.at[kp, _half_rows(c, h)]
                _remote(mine, mine, send, recv, 6 * a + 3 + j, (x, y, 1 - c)).wait_send()

    return dict(ins=arrs, out_shape=[jax.ShapeDtypeStruct((N_CHIPS,) + a.shape, a.dtype) for a in arrs],
                n_sems=6 * len(arrs), start=start, mid=mid, finish=finish)


def scatter_plan(halves):
    def copies(ins, outs, send, recv):
        x, y, c = _mesh_pos()
        for a, (src, dst) in enumerate(zip(ins, outs)):
            for j, (px, py) in enumerate(_other_chips(x, y)):
                yield _remote(src.at[2 * px + py], dst.at[j], send, recv, 3 * a + j, (px, py, c))

    def start(ins, outs, send, recv):
        for cp in copies(ins, outs, send, recv):
            cp.start()

    def finish(ins, outs, send, recv):
        for cp in copies(ins, outs, send, recv):
            cp.wait()

    return dict(ins=list(halves), out_shape=[jax.ShapeDtypeStruct((3,) + a.shape[1:], a.dtype) for a in halves],
                n_sems=3 * len(halves), start=start, mid=None, finish=finish)


def run_exchange(name, plan):
    n_in = len(plan["ins"])

    def body(*refs):
        ins, outs = refs[:n_in], refs[n_in:-2]
        send, recv = refs[-2], refs[-1]
        plan["start"](ins, outs, send, recv)
        if plan["mid"] is not None:
            plan["mid"](ins, outs, send, recv)
        plan["finish"](ins, outs, send, recv)

    return pl.pallas_call(
        body, name=name, in_specs=[HBM] * n_in, out_specs=[HBM] * len(plan["out_shape"]), out_shape=plan["out_shape"],
        scratch_shapes=[pltpu.SemaphoreType.DMA((plan["n_sems"],))] * 2,
        compiler_params=pltpu.CompilerParams(has_side_effects=True),
    )(*plan["ins"])


def swap_halves(name, g_in, g_out):
    h_in, h_out = g_in.shape[1] // 2, g_out.shape[1] // 2

    def body(gi, go, ri, ro, send, recv):
        x, y, c = _mesh_pos()
        sib = (x, y, 1 - c)

        def rows(half, n):
            return pl.ds(pl.multiple_of(half * n, n), n)

        cps = [pltpu.make_async_remote_copy(src_ref=gi.at[:, rows(1 - c, h_in)], dst_ref=ri, send_sem=send.at[0],
                                            recv_sem=recv.at[0], device_id=sib, device_id_type=MESH),
               pltpu.make_async_remote_copy(src_ref=go.at[:, rows(1 - c, h_out)], dst_ref=ro, send_sem=send.at[1],
                                            recv_sem=recv.at[1], device_id=sib, device_id_type=MESH)]
        for cp in cps:
            cp.start()
        for cp in cps:
            cp.wait()

    return pl.pallas_call(
        body, name=name, in_specs=[HBM, HBM], out_specs=[HBM, HBM],
        out_shape=[jax.ShapeDtypeStruct((N_CHIPS, h_in, g_in.shape[2]), g_in.dtype),
                   jax.ShapeDtypeStruct((N_CHIPS, h_out, g_out.shape[2]), g_out.dtype)],
        scratch_shapes=[pltpu.SemaphoreType.DMA((2,)), pltpu.SemaphoreType.DMA((2,))],
        compiler_params=pltpu.CompilerParams(has_side_effects=True),
    )(g_in, g_out)


def half_add(name, g, r, c_idx, tb):
    _, rows2, C = g.shape
    h = rows2 // 2
    nb = h // tb

    def body(c_ref, g_ref, r_ref, f_ref, b_ref):
        s = g_ref[...] + r_ref[...]
        f_ref[...] = s
        b_ref[...] = s.astype(BF16)

    grid_spec = pltpu.PrefetchScalarGridSpec(
        num_scalar_prefetch=1, grid=(N_CHIPS, nb),
        in_specs=[pl.BlockSpec((1, tb, C), lambda k, i, c: (k, c[0] * nb + i, 0)),
                  pl.BlockSpec((1, tb, C), lambda k, i, c: (k, i, 0))],
        out_specs=[pl.BlockSpec((1, tb, C), lambda k, i, c: (k, i, 0))] * 2)
    return pl.pallas_call(
        body, name=name, grid_spec=grid_spec,
        out_shape=[jax.ShapeDtypeStruct((N_CHIPS, h, C), F32), jax.ShapeDtypeStruct((N_CHIPS, h, C), BF16)],
        compiler_params=_cp(("parallel", "parallel")),
    )(c_idx, g, r)


def chip_sum(name, f, r, me_idx, tb):
    _, h, C = f.shape
    nb = h // tb

    def body(me_ref, f_ref, r0, r1, r2, o_ref):
        o_ref[...] = ((f_ref[0] + r0[0].astype(F32)) + r1[0].astype(F32)) + r2[0].astype(F32)

    def slot(j):
        return pl.BlockSpec((1, tb, C), lambda i, me: (j, i, 0))

    grid_spec = pltpu.PrefetchScalarGridSpec(
        num_scalar_prefetch=1, grid=(nb,),
        in_specs=[pl.BlockSpec((1, tb, C), lambda i, me: (me[0], i, 0)), slot(0), slot(1), slot(2)],
        out_specs=pl.BlockSpec((tb, C), lambda i, me: (i, 0)))
    return pl.pallas_call(
        body, name=name, grid_spec=grid_spec, out_shape=jax.ShapeDtypeStruct((h, C), F32),
        compiler_params=_cp(("parallel",)),
    )(me_idx, f, r, r, r)


def share_halves(name, f_in, f_out):
    def body(fi, fo, oi, oo, send, recv):
        x, y, c = _mesh_pos()
        sib = (x, y, 1 - c)
        cps = [pltpu.make_async_remote_copy(src_ref=fi, dst_ref=oi, send_sem=send.at[0], recv_sem=recv.at[0],
                                            device_id=sib, device_id_type=MESH),
               pltpu.make_async_remote_copy(src_ref=fo, dst_ref=oo, send_sem=send.at[1], recv_sem=recv.at[1],
                                            device_id=sib, device_id_type=MESH)]
        for cp in cps:
            cp.start()
        for cp in cps:
            cp.wait()

    return pl.pallas_call(
        body, name=name, in_specs=[HBM, HBM], out_specs=[HBM, HBM],
        out_shape=[jax.ShapeDtypeStruct(f_in.shape, F32), jax.ShapeDtypeStruct(f_out.shape, F32)],
        scratch_shapes=[pltpu.SemaphoreType.DMA((2,)), pltpu.SemaphoreType.DMA((2,))],
        compiler_params=pltpu.CompilerParams(has_side_effects=True),
    )(f_in, f_out)


def gather_small(pack):
    def body(p, g, send, recv, lsem):
        x, y, c = _mesh_pos()
        me = 4 * x + 2 * y + c
        local = pltpu.make_async_copy(p, g.at[me], lsem)
        local.start()
        cps = []
        k = 0
        for fx in (0, 1):
            for fy in (0, 1):
                for fc in (0, 1):
                    if fx + fy + fc == 0:
                        continue
                    to = (x ^ fx, y ^ fy, c ^ fc)
                    cps.append((pltpu.make_async_remote_copy(src_ref=p, dst_ref=g.at[me], send_sem=send.at[k],
                                                             recv_sem=recv.at[k], device_id=to, device_id_type=MESH), to, k))
                    k += 1
        for cp, _, _ in cps:
            cp.start()
        for cp, to, k in cps:
            frm = 4 * to[0] + 2 * to[1] + to[2]
            pltpu.make_async_remote_copy(src_ref=p, dst_ref=g.at[frm], send_sem=send.at[k], recv_sem=recv.at[k],
                                         device_id=to, device_id_type=MESH).wait_recv()
        for cp, _, _ in cps:
            cp.wait_send()
        local.wait()

    return pl.pallas_call(
        body, name="gather_small", in_specs=[HBM], out_specs=HBM,
        out_shape=jax.ShapeDtypeStruct((N_DEV,) + pack.shape, pack.dtype),
        scratch_shapes=[pltpu.SemaphoreType.DMA((7,)), pltpu.SemaphoreType.DMA((7,)), pltpu.SemaphoreType.DMA],
        compiler_params=pltpu.CompilerParams(has_side_effects=True),
    )(pack)


def _adamw(w, g, m, v):
    m = ADAM_B1 * m + (1.0 - ADAM_B1) * g
    v = ADAM_B2 * v + (1.0 - ADAM_B2) * (g * g)
    m_hat = m / (1.0 - ADAM_B1 ** ADAM_STEP)
    v_hat = v / (1.0 - ADAM_B2 ** ADAM_STEP)
    delta = -ADAM_LR * (m_hat / (jnp.sqrt(v_hat) + ADAM_EPS) + ADAM_WD * w)
    return delta, m, v


def adamw_big(name, w, g, m, v):
    shp = w.shape
    flat = lambda a: a.reshape(shp[-2], shp[-1])
    (d, nm, nv), _ = rowcall(name, lambda w, g, m, v: (_adamw(w, g, m, v), ()), [flat(w), flat(g), flat(m), flat(v)], [],
                             [(shp[-1], F32)] * 3, tb=256)
    return d.reshape(shp), nm.reshape(shp), nv.reshape(shp)


def _pack_small(arrs):
    flat = jnp.concatenate([a.reshape(-1) for a in arrs])
    n = flat.shape[0]
    rows = -(-n // (SUBLANE * LANE)) * SUBLANE
    return jnp.pad(flat, (0, rows * LANE - n)).reshape(rows, LANE)


def _unpack_small(pack, shapes):
    flat = pack.reshape(-1)
    outs, off = [], 0
    for s in shapes:
        n = int(np.prod(s))
        outs.append(flat[off:off + n].reshape(s))
        off += n
    return outs


SMALL_NAMES = ("norm_g", "final_g", "rel_bias", "hgrn_lb", "ssd_conv_w", "ssd_conv_b", "ssd_dt_bias", "ssd_a_log",
               "ssd_d", "ssd_norm_g", "hg_norm_g", "at_q_norm_g", "at_k_norm_g")
ALL_NAMES = ("norm_g", "final_g", "rel_bias", "hgrn_lb", "ssd_w_in", "ssd_conv_w", "ssd_conv_b", "ssd_dt_bias",
             "ssd_a_log", "ssd_d", "ssd_norm_g", "ssd_w_out", "hg_w_in", "hg_norm_g", "hg_w_out", "at_w_in",
             "at_q_norm_g", "at_k_norm_g", "at_w_out", "dl_w_in", "dl_w_out")


def kernel(x, norm_g, final_g, rel_bias, hgrn_lb, ssd_w_in, ssd_conv_w, ssd_conv_b, ssd_dt_bias, ssd_a_log, ssd_d, ssd_norm_g, ssd_w_out, hg_w_in, hg_norm_g, hg_w_out, at_w_in, at_q_norm_g, at_k_norm_g, at_w_out, dl_w_in, dl_w_out, loss_target, m_norm_g, m_final_g, m_rel_bias, m_hgrn_lb, m_ssd_w_in, m_ssd_conv_w, m_ssd_conv_b, m_ssd_dt_bias, m_ssd_a_log, m_ssd_d, m_ssd_norm_g, m_ssd_w_out, m_hg_w_in, m_hg_norm_g, m_hg_w_out, m_at_w_in, m_at_q_norm_g, m_at_k_norm_g, m_at_w_out, m_dl_w_in, m_dl_w_out, v_norm_g, v_final_g, v_rel_bias, v_hgrn_lb, v_ssd_w_in, v_ssd_conv_w, v_ssd_conv_b, v_ssd_dt_bias, v_ssd_a_log, v_ssd_d, v_ssd_norm_g, v_ssd_w_out, v_hg_w_in, v_hg_norm_g, v_hg_w_out, v_at_w_in, v_at_q_norm_g, v_at_k_norm_g, v_at_w_out, v_dl_w_in, v_dl_w_out):
    args = locals()
    W = {n: args[n] for n in ALL_NAMES}
    M = {n: args["m_" + n] for n in ALL_NAMES}
    V = {n: args["v_" + n] for n in ALL_NAMES}
    xi, yi, ci = lax.axis_index("x"), lax.axis_index("y"), lax.axis_index("c")
    chip = 2 * xi + yi
    conv_shard = SSD_CONV_CH // N_CHIPS
    hgn_shard = HG_W // N_CHIPS

    p_in = [W[n][0].astype(BF16) for n in IN_NAMES]
    p_out = [W[n][0].astype(BF16) for n in OUT_NAMES]
    p_small = jnp.concatenate([
        jnp.pad(ssd_conv_w[0], ((0, 0), (0, D_MODEL - conv_shard))),
        jnp.pad(hg_norm_g, ((0, 0), (0, D_MODEL - hgn_shard)))], axis=0)
    c_idx = ci.astype(jnp.int32).reshape(1)
    me_idx = chip.astype(jnp.int32).reshape(1)

    def slot(stack, own, k):
        return jnp.where(chip == k, own, stack[k])

    def layer_weights(layer, got):
        s_in, s_out = got[0], got[1]
        return (jnp.concatenate([slot(s_in, p_in[layer], k) for k in range(N_CHIPS)], axis=1),
                jnp.concatenate([slot(s_out, p_out[layer], k) for k in range(N_CHIPS)], axis=0))

    def reduce_start(tag, layers, grads):
        gp_in = jnp.concatenate([grads[IN_NAMES[l]][0].reshape(D_MODEL, N_CHIPS, IN_COLS[l]).transpose(1, 0, 2)
                                 for l in layers], axis=2)
        gp_out = jnp.concatenate([grads[OUT_NAMES[l]][0].reshape(N_CHIPS, OUT_ROWS[l], D_MODEL) for l in layers], axis=1)
        r_in, r_out = swap_halves(f"swap_halves_{tag}", gp_in, gp_out)
        f_in, b_in = half_add(f"half_add_in_{tag}", gp_in, r_in, c_idx, 128)
        f_out, b_out = half_add(f"half_add_out_{tag}", gp_out, r_out, c_idx, 128)
        return (f_in, f_out), scatter_plan([b_in, b_out])

    def reduce_finish(tag, layers, halves, got, G):
        s_in = chip_sum(f"chip_sum_in_{tag}", halves[0], got[0], me_idx, 128)
        s_out = chip_sum(f"chip_sum_out_{tag}", halves[1], got[1], me_idx, 128)
        o_in, o_out = share_halves(f"share_halves_{tag}", s_in, s_out)
        red_in = jnp.where(ci == 0, jnp.concatenate([s_in, o_in], axis=0), jnp.concatenate([o_in, s_in], axis=0))
        red_out = jnp.where(ci == 0, jnp.concatenate([s_out, o_out], axis=0), jnp.concatenate([o_out, s_out], axis=0))
        off_c = off_r = 0
        for l in layers:
            G[IN_NAMES[l]] = red_in[:, off_c:off_c + IN_COLS[l]][None]
            G[OUT_NAMES[l]] = red_out[off_r:off_r + OUT_ROWS[l]][None]
            off_c += IN_COLS[l]
            off_r += OUT_ROWS[l]

    class Schedule:
        early = (2, 3)

        def fwd_plans(self, layer, w):
            if layer == 0:
                return dict(ssd_scan_f=gather_plan([p_in[1], p_out[1]]), ssd_scan_b=gather_plan([p_in[2], p_out[2]]))
            if layer == 1:
                return dict(hg_scan_f=gather_plan([p_in[3], p_out[3]]))
            return None

        def fwd_done(self, layer, got, w):
            if layer == 0:
                w["hg_w_in"], w["hg_w_out"] = layer_weights(1, got["ssd_scan_f"])
                w["at_w_in"], w["at_w_out"] = layer_weights(2, got["ssd_scan_b"])
            if layer == 1:
                w["dl_w_in"], w["dl_w_out"] = layer_weights(3, got["hg_scan_f"])

        def bwd_plans(self, layer, grads):
            if layer == 1:
                self.halves, plan = reduce_start("a", self.early, grads)
                return dict(hg_scan_f_bwd=plan)
            return None

        def bwd_done(self, layer, got):
            if layer == 1:
                self.got = got["hg_scan_f_bwd"]

    g0_in, g0_out, g_small = run_exchange("gather_w0", gather_plan([p_in[0], p_out[0]], whole=[p_small]))
    ssd_in_full, ssd_out_full = layer_weights(0, (g0_in, g0_out))
    conv_full = jnp.concatenate([slot(g_small, p_small, k)[:SSD_CONV, :conv_shard] for k in range(N_CHIPS)], axis=1)
    hgn_full = jnp.concatenate([slot(g_small, p_small, k)[SSD_CONV:SSD_CONV + 1, :hgn_shard] for k in range(N_CHIPS)], axis=1)
    w = dict(
        norm_g=norm_g, final_g=final_g, rel_bias=rel_bias, hgrn_lb=hgrn_lb,
        ssd_w_main=ssd_in_full[:, :SSD_MAIN], ssd_w_dt=ssd_in_full[:, SSD_MAIN:],
        ssd_conv_w8=jnp.concatenate([conv_full, jnp.zeros((1, SSD_CONV_CH), F32)], axis=0),
        ssd_conv_b=ssd_conv_b, ssd_dt_bias=ssd_dt_bias, ssd_a_log=ssd_a_log, ssd_d=ssd_d, ssd_norm_g=ssd_norm_g,
        ssd_w_out=ssd_out_full, hg_norm_g=hgn_full, at_q_norm_g=at_q_norm_g, at_k_norm_g=at_k_norm_g)

    sched = Schedule()
    loss_tile, grad_x, grads = local_step(x[0], loss_target[0], w, sched)
    loss = lax.psum(loss_tile[0, 0], ("x", "y", "c"))

    G = {}
    late = (0, 1)
    halves_b, plan_b = reduce_start("b", late, grads)
    got_b = run_exchange("scatter_b", plan_b)
    reduce_finish("a", sched.early, sched.halves, sched.got, G)
    reduce_finish("b", late, halves_b, got_b, G)

    small_full = [grads[n].reshape(-1) for n in SMALL_NAMES]
    shapes_full = [grads[n].shape for n in SMALL_NAMES]
    packs = gather_small(_pack_small(small_full))
    (red_small,) = smallcall("sum_small", lambda p: (functools.reduce(lambda a, b: a + b, [p[k] for k in range(N_DEV)]),),
                             [packs], [packs.shape[1:]])
    for n, g in zip(SMALL_NAMES, _unpack_small(red_small, shapes_full)):
        G[n] = g
    G["ssd_conv_w"] = lax.dynamic_slice_in_dim(G["ssd_conv_w"].reshape(1, SSD_CONV, SSD_CONV_CH), chip * conv_shard, conv_shard, axis=2)
    G["hg_norm_g"] = lax.dynamic_slice_in_dim(G["hg_norm_g"].reshape(1, HG_W), chip * hgn_shard, hgn_shard, axis=1)
    for n in SMALL_NAMES:
        G[n] = G[n].reshape(W[n].shape)

    D, NM, NV = {}, {}, {}
    for n in IN_NAMES + OUT_NAMES:
        D[n], NM[n], NV[n] = adamw_big("adamw_" + n, W[n], G[n], M[n], V[n])
    shapes = [W[n].shape for n in SMALL_NAMES]
    pk = [_pack_small([T[n] for n in SMALL_NAMES]) for T in (W, G, M, V)]
    outs = smallcall("adamw_small", lambda w, g, m, v: _adamw(w, g, m, v), pk, [pk[0].shape] * 3)
    for T, pack in zip((D, NM, NV), outs):
        for n, a in zip(SMALL_NAMES, _unpack_small(pack, shapes)):
            T[n] = a
    return (loss, grad_x[None], *[G[n] for n in ALL_NAMES], *[D[n] for n in ALL_NAMES],
            *[NM[n] for n in ALL_NAMES], *[NV[n] for n in ALL_NAMES])
```

```python
import functools
import math

import numpy as np
import jax
import jax.numpy as jnp
from jax import lax
from jax.experimental import pallas as pl
from jax.experimental.pallas import tpu as pltpu

F32 = jnp.float32
BF16 = jnp.bfloat16
MESH = pl.DeviceIdType.MESH

D_MODEL = 1024
DEPTH = 4
GRID_W = 64
EPS = 1e-6
NEG_BIG = -1e30

SSD_DI = 2048
SSD_HEADDIM = 64
SSD_HEADS = 32
SSD_GROUPS = 4
SSD_HPG = 8
SSD_STATE = 128
SSD_CONV = 7
SSD_CHUNK = 128
SSD_GPS = 4
SSD_CONV_CH = SSD_DI + 2 * SSD_GROUPS * SSD_STATE
SSD_MAIN = SSD_DI + SSD_CONV_CH
SSD_IN = SSD_MAIN + 2 * SSD_HEADS

HG_HEADS = 8
HG_EXPAND = 128
HG_W = 1024
HG_CHUNK = 32
HG_ROWS = 128
HG_HPS = 8
HG_IN = 5 * HG_W

AT_HEADS = 16
AT_KV = 8
AT_GRP = 2
AT_HD = 128
ROPE_THETA = 10000.0
ROPE_AXIS = 64
AT_QW = AT_HEADS * AT_HD
AT_KW = AT_KV * AT_HD
AT_IN = 2 * AT_QW + 2 * AT_KW

DL_PAIRS = ((128, 1), (512, 4), (2048, 16))
DL_HEADS = 16
DL_HD = 64
DL_W = 1024
DL_STEPS = 64
DL_IN = 10 * DL_W
REL_BUCKETS = 32
REL_MAX_DIST = 1024

ADAM_LR = 0.001
ADAM_B1 = 0.9
ADAM_B2 = 0.999
ADAM_EPS = 1e-08
ADAM_WD = 0.01
ADAM_STEP = 10

VMEM_LIMIT = 56 * 1024 * 1024
LANE = 128
SUBLANE = 8


def _cp(sem=None):
    return pltpu.CompilerParams(dimension_semantics=sem, vmem_limit_bytes=VMEM_LIMIT)


_NN, _NT, _TN = ((1,), (0,)), ((1,), (1,)), ((0,), (0,))


def _dot(a, b, dims):
    return lax.dot_general(a.astype(BF16), b.astype(BF16), (dims, ((), ())), preferred_element_type=F32)


def _dot_rule(dims, da_rule, db_rule):
    @jax.custom_vjp
    def f(a, b):
        return _dot(a, b, dims)

    def fwd(a, b):
        return _dot(a, b, dims), (a, b)

    def bwd(res, g):
        a, b = res
        return da_rule(a, b, g).astype(a.dtype), db_rule(a, b, g).astype(b.dtype)

    f.defvjp(fwd, bwd)
    return f


_mm = _dot_rule(_NN, lambda a, b, g: _dot(g, b, _NT), lambda a, b, g: _dot(a, g, _TN))
_mm_nt = _dot_rule(_NT, lambda a, b, g: _dot(g, b, _NN), lambda a, b, g: _dot(g, a, _TN))
_mm_tn = _dot_rule(_TN, lambda a, b, g: _dot(b, g, _NT), lambda a, b, g: _dot(a, g, _NN))


def _mm_exact(a, b):
    return jnp.dot(a, b, preferred_element_type=F32, precision=lax.Precision.HIGHEST)


def _dot3(t, a, dims):
    hi = a.astype(BF16)
    r1 = a - hi.astype(F32)
    mid = r1.astype(BF16)
    lo = r1 - mid.astype(F32)
    return _dot(t, hi, dims) + (_dot(t, mid, dims) + _dot(t, lo, dims))


@jax.custom_vjp
def _mm_tri(t, a):
    return _dot3(t, a, _NN)


def _mm_tri_fwd(t, a):
    return _dot3(t, a, _NN), t


def _mm_tri_bwd(t, g):
    return None, _dot3(t, g, _TN)


_mm_tri.defvjp(_mm_tri_fwd, _mm_tri_bwd)


def _mm_nt_exact(a, b):
    return lax.dot_general(a, b, (((1,), (1,)), ((), ())), preferred_element_type=F32,
                           precision=lax.Precision.HIGHEST)


def _silu(x):
    return x * jax.nn.sigmoid(x)


def _softplus(z):
    return jnp.maximum(z, 0.0) + jnp.log(1.0 + jnp.exp(-jnp.abs(z)))


def _pick(dim, pref):
    best = None
    t = LANE
    while t <= min(dim, pref):
        if dim % t == 0:
            best = t
        t += LANE
    return best if best is not None else dim


def _const_map(n):
    return lambda *_: (0,) * n


MM_BLOCK_BYTES = 8 * 1024 * 1024


def _mm_tiles(mode, M, N, K, a_bytes, b_bytes):
    if mode == "nn":
        tk = K if K <= 2048 else _pick(K, 1024)
        tm = _pick(M, max(512, MM_BLOCK_BYTES // (tk * a_bytes)))
        tn = _pick(N, 512)
    elif mode == "tn":
        tk = K if K <= 4096 else _pick(K, 1024)
        tm = _pick(M, MM_BLOCK_BYTES // (tk * a_bytes))
        tn = _pick(N, MM_BLOCK_BYTES // (tk * b_bytes))
    else:
        tk = _pick(K, 1024)
        tn = _pick(N, 1024)
        tm = _pick(M, MM_BLOCK_BYTES // (8 * tn))
    return tm, tn, tk


def matmul(name, a, b, mode="nn", res=None, out_dtype=F32):
    if mode == "tn":
        K, M = a.shape
    else:
        M, K = a.shape
    N = b.shape[0] if mode == "nt" else b.shape[1]
    tm, tn, tk = _mm_tiles(mode, M, N, K, a.dtype.itemsize, b.dtype.itemsize)
    nk = K // tk
    a_spec = (pl.BlockSpec((tk, tm), lambda i, j, k: (k, i)) if mode == "tn"
              else pl.BlockSpec((tm, tk), lambda i, j, k: (i, k)))
    b_spec = (pl.BlockSpec((tn, tk), lambda i, j, k: (j, k)) if mode == "nt"
              else pl.BlockSpec((tk, tn), lambda i, j, k: (k, j)))
    dot = {"nn": _mm, "nt": _mm_nt, "tn": _mm_tn}[mode]
    has_res = res is not None

    def body(*refs):
        a_ref, b_ref = refs[0], refs[1]
        r_ref = refs[2] if has_res else None
        o_ref = refs[3] if has_res else refs[2]

        def finish(out):
            if has_res:
                out = out + r_ref[...].astype(F32)
            o_ref[...] = out.astype(o_ref.dtype)

        if nk == 1:
            finish(dot(a_ref[...], b_ref[...]))
            return
        acc = refs[-1]
        k = pl.program_id(2)

        @pl.when(k == 0)
        def _():
            acc[...] = jnp.zeros_like(acc)

        acc[...] += dot(a_ref[...], b_ref[...])

        @pl.when(k == nk - 1)
        def _():
            finish(acc[...])

    in_specs = [a_spec, b_spec]
    args = [a, b]
    if has_res:
        in_specs.append(pl.BlockSpec((tm, tn), lambda i, j, k: (i, j)))
        args.append(res)
    return pl.pallas_call(
        body, name=name, grid=(M // tm, N // tn, nk), in_specs=in_specs,
        out_specs=pl.BlockSpec((tm, tn), lambda i, j, k: (i, j)),
        out_shape=jax.ShapeDtypeStruct((M, N), out_dtype),
        scratch_shapes=[pltpu.VMEM((tm, tn), F32)] if nk > 1 else [],
        compiler_params=_cp(("parallel", "parallel", "arbitrary")),
    )(*args)


def call_with_comm(body, comm, *, name, grid, in_specs, out_specs, out_shape, scratch_shapes, semantics, args):
    if comm is None:
        outs = pl.pallas_call(body, name=name, grid=grid, in_specs=in_specs, out_specs=out_specs, out_shape=out_shape,
                              scratch_shapes=scratch_shapes, compiler_params=_cp(semantics))(*args)
        return list(outs), []
    n_in, n_out, n_scr = len(in_specs), len(out_specs), len(scratch_shapes)
    c_in, c_out = len(comm["ins"]), len(comm["out_shape"])
    total = int(np.prod(grid))
    mid_step = (2 * total) // 3

    def wrapped(*refs):
        p = 0
        ins = refs[p:p + n_in]
        p += n_in
        cins = refs[p:p + c_in]
        p += c_in
        outs = refs[p:p + n_out]
        p += n_out
        couts = refs[p:p + c_out]
        p += c_out
        scr = refs[p:p + n_scr]
        send, recv = refs[p + n_scr], refs[p + n_scr + 1]
        step = pl.program_id(0)
        for ax in range(1, len(grid)):
            step = step * grid[ax] + pl.program_id(ax)

        @pl.when(step == 0)
        def _():
            comm["start"](cins, couts, send, recv)

        body(*ins, *outs, *scr)
        if comm["mid"] is not None:
            @pl.when(step == mid_step)
            def _():
                comm["mid"](cins, couts, send, recv)

        @pl.when(step == total - 1)
        def _():
            comm["finish"](cins, couts, send, recv)

    outs = pl.pallas_call(
        wrapped, name=name, grid=grid, in_specs=list(in_specs) + [HBM] * c_in,
        out_specs=list(out_specs) + [HBM] * c_out, out_shape=list(out_shape) + list(comm["out_shape"]),
        scratch_shapes=list(scratch_shapes) + [pltpu.SemaphoreType.DMA((comm["n_sems"],))] * 2,
        compiler_params=_cp(("arbitrary",) * len(grid)),
    )(*args, *comm["ins"])
    return list(outs[:n_out]), list(outs[n_out:])


def _col(arr, width, idx):
    return (arr, width, idx)


def _perm(arr, dil, width=None, idx=0):
    return (arr, arr.shape[1] if width is None else width, idx, dil)


def _from_perm(ref, scr, dil):
    n, w = ref.shape[1], ref.shape[2]
    for r in range(dil):
        for j in range(w // LANE):
            scr[j, pl.ds(r, n, stride=dil), :] = ref[r, :, j * LANE:(j + 1) * LANE].astype(F32)
    return jnp.concatenate([scr[j] for j in range(w // LANE)], axis=1)


def _to_perm(val, ref, scr, dil):
    n, w = ref.shape[1], ref.shape[2]
    for j in range(w // LANE):
        scr[j] = val[:, j * LANE:(j + 1) * LANE].astype(F32)
    for r in range(dil):
        ref[r] = jnp.concatenate([scr[j, pl.ds(r, n, stride=dil), :] for j in range(w // LANE)], axis=1).astype(ref.dtype)


def rowcall(name, fn, rows, bcs, row_outs, bc_outs=(), tb=256, halo=()):
    rows = [r if isinstance(r, tuple) else (r, r.shape[1], 0) for r in rows]
    rows = [r if len(r) == 4 else r + (1,) for r in rows]
    row_outs = [o if len(o) == 3 else o + (1,) for o in row_outs]
    S = rows[0][0].shape[0]
    tb = min(tb, S)
    nb = S // tb
    n_r, n_h, n_b, n_ro, n_bo = len(rows), len(halo), len(bcs), len(row_outs), len(bc_outs)
    hb = tb // SUBLANE
    last = S // SUBLANE - 1
    perm_w = max([w for (_, w, _, d) in rows if d > 1] + [w for (w, _, d) in row_outs if d > 1] + [0])

    def body(*refs):
        i = pl.program_id(0)
        scr = refs[-1] if perm_w else None
        pos = 0
        r_in = [r[...] if d == 1 else _from_perm(r, scr, d) for r, (_, _, _, d) in zip(refs[pos:pos + n_r], rows)]
        pos += n_r
        h_in = []
        for _ in range(n_h):
            prev = refs[pos][...] * (i > 0).astype(F32)
            nxt = refs[pos + 1][...] * (i < nb - 1).astype(F32)
            h_in += [prev, nxt]
            pos += 2
        b_in = [r[...] for r in refs[pos:pos + n_b]]
        pos += n_b
        ro = refs[pos:pos + n_ro]
        bo = refs[pos + n_ro:pos + n_ro + n_bo]
        outs_r, outs_b = fn(*r_in, *h_in, *b_in)
        for ref, val, (_, _, d) in zip(ro, outs_r, row_outs, strict=True):
            if d == 1:
                ref[...] = val.astype(ref.dtype)
            else:
                _to_perm(val, ref, scr, d)
        if n_bo:
            @pl.when(i == 0)
            def _():
                for ref in bo:
                    ref[...] = jnp.zeros_like(ref)

            for ref, val in zip(bo, outs_b, strict=True):
                ref[...] += val

    in_specs, args = [], []
    for (a, w, c, d) in rows:
        if d == 1:
            in_specs.append(pl.BlockSpec((tb, w), functools.partial(lambda i, c: (i, c), c=c)))
            args.append(a)
        else:
            in_specs.append(pl.BlockSpec((d, tb // d, w), functools.partial(lambda i, c: (0, i, c), c=c)))
            args.append(a.reshape(d, S // d, a.shape[1]))
    for h in halo:
        a, w, c, _ = rows[h]
        in_specs.append(pl.BlockSpec((SUBLANE, w), functools.partial(
            lambda i, c: (jnp.maximum(i * hb - 1, 0), c), c=c)))
        in_specs.append(pl.BlockSpec((SUBLANE, w), functools.partial(
            lambda i, c: (jnp.minimum((i + 1) * hb, last), c), c=c)))
        args += [a, a]
    for b in bcs:
        in_specs.append(pl.BlockSpec(b.shape, _const_map(b.ndim)))
        args.append(b)
    out_specs, out_shape = [], []
    for (w, dt, d) in row_outs:
        if d == 1:
            out_specs.append(pl.BlockSpec((tb, w), lambda i: (i, 0)))
            out_shape.append(jax.ShapeDtypeStruct((S, w), dt))
        else:
            out_specs.append(pl.BlockSpec((d, tb // d, w), lambda i: (0, i, 0)))
            out_shape.append(jax.ShapeDtypeStruct((d, S // d, w), dt))
    for shp in bc_outs:
        out_specs.append(pl.BlockSpec(shp, _const_map(len(shp))))
        out_shape.append(jax.ShapeDtypeStruct(shp, F32))
    outs = pl.pallas_call(
        body, name=name, grid=(nb,), in_specs=in_specs, out_specs=out_specs, out_shape=out_shape,
        scratch_shapes=[pltpu.VMEM((perm_w // LANE, tb, LANE), F32)] if perm_w else [],
        compiler_params=_cp(("arbitrary",) if n_bo else ("parallel",)),
    )(*args)
    row_res = [o if d == 1 else o.reshape(S, w) for o, (w, _, d) in zip(outs[:n_ro], row_outs)]
    return row_res, list(outs[n_ro:])


def smallcall(name, fn, ins, out_shapes):
    n_in = len(ins)

    def body(*refs):
        outs = fn(*[r[...] for r in refs[:n_in]])
        for ref, val in zip(refs[n_in:], outs, strict=True):
            ref[...] = val.astype(ref.dtype)

    return pl.pallas_call(
        body, name=name, out_shape=[jax.ShapeDtypeStruct(s, F32) for s in out_shapes],
        compiler_params=_cp(),
    )(*ins)


def _rms(x, g):
    return x * lax.rsqrt(jnp.mean(x * x, axis=-1, keepdims=True) + EPS) * g


def _rms_groups(y, g, width):
    outs = []
    for j in range(y.shape[1] // width):
        sl = slice(j * width, (j + 1) * width)
        outs.append(_rms(y[:, sl], g[:, sl]))
    return jnp.concatenate(outs, axis=1)


def norm_fwd(name, x, g, dils=(1,)):
    outs, _ = rowcall(name, lambda x, g: ((_rms(x, g),) * len(dils), ()), [x], [g],
                      [(D_MODEL, BF16, d) for d in dils], tb=512)
    return outs[0] if len(dils) == 1 else tuple(outs)


def norm_bwd(name, x, g, dhn, dres, dils=(1,)):
    parts = dhn if isinstance(dhn, tuple) else (dhn,)
    n = len(parts)

    def fn(x, *rest):
        dh = functools.reduce(lambda a, b: a + b, rest[:n])
        _, vjp = jax.vjp(_rms, x, rest[n + 1])
        dx, dg = vjp(dh)
        return (dx + rest[n],), (dg,)

    rows = [x] + [a if d == 1 else _perm(a, d) for a, d in zip(parts, dils)] + [dres]
    (dx,), (dg,) = rowcall(name, fn, rows, [g], [(D_MODEL, F32)], [(1, D_MODEL)], tb=512)
    return dx, dg


def loss_head(x, tgt, g):
    def fn(x, tgt, g):
        y, vjp = jax.vjp(_rms, x, g)
        diff = y - tgt
        loss = 0.5 * jnp.sum(jnp.mean(diff * diff, axis=-1, keepdims=True), axis=0, keepdims=True)
        dx, dg = vjp(diff * (1.0 / D_MODEL))
        return (dx,), (jnp.broadcast_to(loss, (1, LANE)), dg)

    (dx,), (loss, dg) = rowcall("loss_head", fn, [x, tgt], [g], [(D_MODEL, F32)],
                                [(1, LANE), (1, D_MODEL)], tb=512)
    return loss, dx, dg


def _shift_rows(x, s):
    if s == 0:
        return x
    return pltpu.roll(x, (-s) % x.shape[0], 0)


def _conv_ext(x, prev, nxt, w):
    xe = jnp.concatenate([prev, x, nxt], axis=0)
    pad = SSD_CONV // 2
    c = jnp.zeros_like(xe)
    for k in range(SSD_CONV):
        c = c + w[k:k + 1, :] * _shift_rows(xe, k - pad)
    return xe, c


def ssd_conv_fwd(u, conv_w, conv_b):
    def fn(x0, x1, x2, p0, n0, p1, n1, p2, n2, w, b):
        tb = x0.shape[0]
        outs = []
        for j, (x, p, n) in enumerate(((x0, p0, n0), (x1, p1, n1), (x2, p2, n2))):
            sl = slice(j * 1024, (j + 1) * 1024)
            _, c = _conv_ext(x, p, n, w[:, sl])
            outs.append(_silu(c[SUBLANE:SUBLANE + tb] + b[:, sl]))
        return (jnp.concatenate(outs, axis=1),), ()

    (xbc,), _ = rowcall("ssd_conv_fwd", fn, [_col(u, 1024, 2), _col(u, 1024, 3), _col(u, 1024, 4)],
                        [conv_w, conv_b], [(SSD_CONV_CH, F32)], tb=256, halo=(0, 1, 2))
    return xbc


def ssd_conv_bwd(u, dxbc, dz, conv_w, conv_b):
    pad = SSD_CONV // 2

    def fn(x0, x1, x2, g0, g1, g2, dz, xp0, xn0, xp1, xn1, xp2, xn2, gp0, gn0, gp1, gn1, gp2, gn2, w, b):
        tb = x0.shape[0]
        blk = slice(SUBLANE, SUBLANE + tb)
        dpre, dws, dbs = [], [], []
        xs = ((x0, xp0, xn0), (x1, xp1, xn1), (x2, xp2, xn2))
        gs = ((g0, gp0, gn0), (g1, gp1, gn1), (g2, gp2, gn2))
        for j in range(3):
            sl = slice(j * 1024, (j + 1) * 1024)
            wj = w[:, sl]
            xe, c = _conv_ext(*xs[j], wj)
            ce = c + b[:, sl]
            sig = jax.nn.sigmoid(ce)
            ge = jnp.concatenate([gs[j][1], gs[j][0], gs[j][2]], axis=0)
            dce = ge * (sig * (1.0 + ce * (1.0 - sig)))
            dx = jnp.zeros_like(xe)
            dw_rows = []
            for k in range(SSD_CONV):
                dx = dx + wj[k:k + 1, :] * _shift_rows(dce, pad - k)
                dw_rows.append(jnp.sum(dce[blk] * _shift_rows(xe, k - pad)[blk], axis=0, keepdims=True))
            dw_rows.append(jnp.zeros_like(dw_rows[0]))
            dpre.append(dx[blk])
            dws.append(jnp.concatenate(dw_rows, axis=0))
            dbs.append(jnp.sum(dce[blk], axis=0, keepdims=True))
        du = jnp.concatenate([dz] + dpre, axis=1)
        return (du,), (jnp.concatenate(dws, axis=1), jnp.concatenate(dbs, axis=1))

    rows = [_col(u, 1024, 2), _col(u, 1024, 3), _col(u, 1024, 4),
            _col(dxbc, 1024, 0), _col(dxbc, 1024, 1), _col(dxbc, 1024, 2), dz]
    (du,), (dw, db) = rowcall("ssd_conv_bwd", fn, rows, [conv_w, conv_b], [(SSD_MAIN, BF16)],
                              [(SUBLANE, SSD_CONV_CH), (1, SSD_CONV_CH)], tb=128, halo=(0, 1, 2, 3, 4, 5))
    return du, dw, db


def _expand_heads(v):
    return jnp.concatenate([jnp.broadcast_to(v[:, j:j + 1], (v.shape[0], SSD_HEADDIM)) for j in range(SSD_HPG)], axis=1)


def _ssd_chunk(rev, st_in, xs, udt, dtb, alog, B, C):
    Q = B.shape[0]
    P = SSD_HEADDIM
    dt = _softplus(udt + dtb)
    a = dt * (-jnp.exp(alog))
    r = lax.broadcasted_iota(jnp.int32, (Q, Q), 0)
    c = lax.broadcasted_iota(jnp.int32, (Q, Q), 1)
    mask = (r <= c) if rev else (r >= c)
    p = _mm_tri(mask, a)
    pT = p.T
    p_e = _expand_heads(p)
    tot_e = p_e[0:1] if rev else p_e[Q - 1:Q]
    xdt = xs * _expand_heads(dt)
    CB = _mm_nt(C, B)
    H = SSD_HPG
    p_cols = jnp.concatenate([jnp.broadcast_to(p[:, j:j + 1], (Q, Q)) for j in range(H)], axis=1)
    p_rows = jnp.concatenate([pT[j:j + 1, :] for j in range(H)], axis=1)
    decay = jnp.exp(jnp.where(jnp.concatenate([mask] * H, axis=1), p_cols - p_rows, NEG_BIG))
    col = lax.broadcasted_iota(jnp.int32, (1, H * P), 1)
    x_bd = jnp.concatenate([jnp.where((col >= j * P) & (col < (j + 1) * P), xdt, 0.0) for j in range(H)], axis=0)
    y = _mm(jnp.concatenate([CB] * H, axis=1) * decay, x_bd) + _mm(C, st_in) * jnp.exp(p_e)
    st_out = st_in * jnp.exp(tot_e) + _mm_tn(B, xdt * jnp.exp(tot_e - p_e))
    return y, st_out


def _ssd_specs(nc, rev_order):
    Q = SSD_CHUNK
    N, P, H, GS = SSD_STATE, SSD_HEADDIM, SSD_HPG, SSD_GPS
    gw = H * P
    nbc = SSD_GROUPS // GS

    def cidx(s):
        return nc - 1 - s if rev_order else s

    xs = pl.BlockSpec((Q, GS * gw), lambda g, s: (cidx(s), g))
    Bs = pl.BlockSpec((Q, GS * N), lambda g, s: (cidx(s), SSD_DI // (GS * N) + g))
    Cs = pl.BlockSpec((Q, GS * N), lambda g, s: (cidx(s), SSD_DI // (GS * N) + nbc + g))
    BC_out = pl.BlockSpec((Q, GS * N), lambda g, s: (cidx(s), g))
    udt = pl.BlockSpec((GS, Q, H), lambda g, s: (g, cidx(s), 0))
    small = pl.BlockSpec((GS, 1, H), lambda g, s: (g, 0, 0))
    st = pl.BlockSpec((GS, 1, N, gw), lambda g, s: (g, cidx(s), 0, 0))
    return xs, Bs, Cs, BC_out, udt, small, st


def ssd_scan_fwd(name, xbc, udt, dtb, alog, rev, comm=None):
    S = xbc.shape[0]
    Q, N, P, H, GS = SSD_CHUNK, SSD_STATE, SSD_HEADDIM, SSD_HPG, SSD_GPS
    gw = H * P
    nc = S // Q
    xs_s, B_s, C_s, _, udt_s, small_s, st_s = _ssd_specs(nc, rev)

    def body(xs_ref, B_ref, C_ref, udt_ref, dtb_ref, alog_ref, y_ref, st_ref, state):
        @pl.when(pl.program_id(1) == 0)
        def _():
            state[...] = jnp.zeros_like(state)

        for g in range(GS):
            st_ref[g, 0] = state[g]
            y, st_out = _ssd_chunk(rev, state[g], xs_ref[:, g * gw:(g + 1) * gw], udt_ref[g], dtb_ref[g], alog_ref[g],
                                   B_ref[:, g * N:(g + 1) * N], C_ref[:, g * N:(g + 1) * N])
            y_ref[:, g * gw:(g + 1) * gw] = y
            state[g] = st_out

    (y, st), got = call_with_comm(
        body, comm, name=name, grid=(SSD_GROUPS // GS, nc),
        in_specs=[xs_s, B_s, C_s, udt_s, small_s, small_s],
        out_specs=[xs_s, st_s],
        out_shape=[jax.ShapeDtypeStruct((S, SSD_DI), F32),
                   jax.ShapeDtypeStruct((SSD_GROUPS, nc, N, gw), F32)],
        scratch_shapes=[pltpu.VMEM((GS, N, gw), F32)],
        semantics=("parallel", "arbitrary"), args=(xbc, xbc, xbc, udt, dtb, alog))
    return y, st, got


def ssd_scan_bwd(name, xbc, udt, dtb, alog, states, dy, rev):
    S = xbc.shape[0]
    Q, N, P, H, GS = SSD_CHUNK, SSD_STATE, SSD_HEADDIM, SSD_HPG, SSD_GPS
    gw = H * P
    nc = S // Q
    xs_s, B_s, C_s, BC_out, udt_s, small_s, st_s = _ssd_specs(nc, not rev)

    def body(xs_ref, B_ref, C_ref, udt_ref, dtb_ref, alog_ref, st_ref, dy_ref,
             dx_ref, dB_ref, dC_ref, dudt_ref, ddtb_ref, dalog_ref, dstate):
        @pl.when(pl.program_id(1) == 0)
        def _():
            dstate[...] = jnp.zeros_like(dstate)
            ddtb_ref[...] = jnp.zeros_like(ddtb_ref)
            dalog_ref[...] = jnp.zeros_like(dalog_ref)

        for g in range(GS):
            cols, bc = slice(g * gw, (g + 1) * gw), slice(g * N, (g + 1) * N)
            _, vjp = jax.vjp(functools.partial(_ssd_chunk, rev), st_ref[g, 0], xs_ref[:, cols], udt_ref[g], dtb_ref[g],
                             alog_ref[g], B_ref[:, bc], C_ref[:, bc])
            dst_in, dxs, dudt, ddtb, dalog, dB, dC = vjp((dy_ref[:, cols], dstate[g]))
            dx_ref[:, cols] = dxs
            dB_ref[:, bc] = dB
            dC_ref[:, bc] = dC
            dudt_ref[g] = dudt
            ddtb_ref[g] += ddtb
            dalog_ref[g] += dalog
            dstate[g] = dst_in

    return pl.pallas_call(
        body, name=name, grid=(SSD_GROUPS // GS, nc),
        in_specs=[xs_s, B_s, C_s, udt_s, small_s, small_s, st_s, xs_s],
        out_specs=[xs_s, BC_out, BC_out, udt_s, small_s, small_s],
        out_shape=[jax.ShapeDtypeStruct((S, SSD_DI), F32),
                   jax.ShapeDtypeStruct((S, SSD_GROUPS * N), F32),
                   jax.ShapeDtypeStruct((S, SSD_GROUPS * N), F32),
                   jax.ShapeDtypeStruct((SSD_GROUPS, S, H), F32),
                   jax.ShapeDtypeStruct((SSD_GROUPS, 1, H), F32),
                   jax.ShapeDtypeStruct((SSD_GROUPS, 1, H), F32)],
        scratch_shapes=[pltpu.VMEM((GS, N, gw), F32)],
        compiler_params=_cp(("parallel", "arbitrary")),
    )(xbc, xbc, xbc, udt, dtb, alog, states, dy)


def _ssd_combine(yf, yb, xs, z, dexp, ng):
    y = (yf + yb + xs * dexp) * _silu(z)
    return _rms_groups(y, ng, SSD_DI // SSD_GROUPS)


def ssd_forward(x, hn, w, comm=None):
    comm = comm or {}
    S = x.shape[0]
    u = matmul("ssd_in", hn, w["ssd_w_main"])
    udt = matmul("ssd_in_dt", hn, w["ssd_w_dt"])
    xbc = ssd_conv_fwd(u, w["ssd_conv_w8"], w["ssd_conv_b"])
    udt_t = udt.reshape(S, 2, SSD_GROUPS, SSD_HPG).transpose(1, 2, 0, 3)
    dtb = w["ssd_dt_bias"].reshape(2, SSD_GROUPS, 1, SSD_HPG)
    alog = w["ssd_a_log"].reshape(2, SSD_GROUPS, 1, SSD_HPG)
    yf, stf, got_f = ssd_scan_fwd("ssd_scan_f", xbc, udt_t[0], dtb[0], alog[0], False, comm.get("ssd_scan_f"))
    yb, stb, got_b = ssd_scan_fwd("ssd_scan_b", xbc, udt_t[1], dtb[1], alog[1], True, comm.get("ssd_scan_b"))
    dexp = jnp.repeat(w["ssd_d"].reshape(1, SSD_HEADS), SSD_HEADDIM, axis=1)
    (yn,), _ = rowcall("ssd_combine", lambda yf, yb, xs, z, d, g: ((_ssd_combine(yf, yb, xs, z, d, g),), ()),
                       [yf, yb, _col(xbc, SSD_DI, 0), _col(u, SSD_DI, 0)], [dexp, w["ssd_norm_g"]],
                       [(SSD_DI, BF16)], tb=256)
    out = matmul("ssd_out", yn, w["ssd_w_out"], res=x)
    saved = dict(hn=hn, u=u, xbc=xbc, udt_t=udt_t, dtb=dtb, alog=alog, yf=yf, yb=yb, stf=stf, stb=stb,
                 dexp=dexp, yn=yn, got=dict(ssd_scan_f=got_f, ssd_scan_b=got_b))
    return out, saved


def ssd_backward(dy, sv, w, comm=None):
    S = dy.shape[0]
    u, xbc = sv["u"], sv["xbc"]
    dyn = matmul("ssd_out_dx", dy, w["ssd_w_out"], mode="nt")
    g_w_out = matmul("ssd_out_dw", sv["yn"], dy, mode="tn")

    def comb_bwd(yf, yb, xs, z, dyn, dexp, ng):
        _, vjp = jax.vjp(_ssd_combine, yf, yb, xs, z, dexp, ng)
        dyf, _, dxs, dz, ddexp, dng = vjp(dyn)
        return (dyf, dxs, dz), (ddexp, dng)

    (dyc, dskip, dz), (ddexp, g_norm) = rowcall(
        "ssd_combine_bwd", comb_bwd, [sv["yf"], sv["yb"], _col(xbc, SSD_DI, 0), _col(u, SSD_DI, 0), dyn],
        [sv["dexp"], w["ssd_norm_g"]], [(SSD_DI, F32)] * 3, [(1, SSD_DI), (1, SSD_DI)], tb=256)
    udt_t, dtb, alog = sv["udt_t"], sv["dtb"], sv["alog"]
    dxf, dBf, dCf, dudt_f, ddtb_f, dalog_f = ssd_scan_bwd("ssd_scan_f_bwd", xbc, udt_t[0], dtb[0], alog[0],
                                                          sv["stf"], dyc, False)
    dxb, dBb, dCb, dudt_b, ddtb_b, dalog_b = ssd_scan_bwd("ssd_scan_b_bwd", xbc, udt_t[1], dtb[1], alog[1],
                                                          sv["stb"], dyc, True)

    def gather(dxf, dxb, dskip, dBf, dBb, dCf, dCb):
        return (jnp.concatenate([dxf + dxb + dskip, dBf + dBb, dCf + dCb], axis=1),), ()

    (dxbc,), _ = rowcall("ssd_dxbc", gather, [dxf, dxb, dskip, dBf, dBb, dCf, dCb], [], [(SSD_CONV_CH, F32)], tb=256)
    du, g_conv_w8, g_conv_b = ssd_conv_bwd(u, dxbc, dz, w["ssd_conv_w8"], w["ssd_conv_b"])
    dudt = jnp.stack([dudt_f, dudt_b]).transpose(2, 0, 1, 3).reshape(S, 2 * SSD_HEADS)
    hn = sv["hn"]
    g_main = matmul("ssd_in_dw", hn, du, mode="tn")
    g_dt = matmul("ssd_in_dt_dw", hn, dudt, mode="tn")
    dhn = matmul("ssd_in_dt_dx", dudt, w["ssd_w_dt"], mode="nt")
    dhn = matmul("ssd_in_dx", du, w["ssd_w_main"], mode="nt", res=dhn)
    grads = dict(
        ssd_w_in=jnp.concatenate([g_main, g_dt], axis=1)[None],
        ssd_conv_w=g_conv_w8[None, :SSD_CONV],
        ssd_conv_b=g_conv_b,
        ssd_dt_bias=jnp.stack([ddtb_f, ddtb_b]).reshape(1, 2, SSD_HEADS),
        ssd_a_log=jnp.stack([dalog_f, dalog_b]).reshape(1, 2, SSD_HEADS),
        ssd_d_exp=ddexp,
        ssd_norm_g=g_norm,
        ssd_w_out=g_w_out[None],
    )
    return dhn, grads


def _hg_block(rev, stTs, uq, uf, ui, lb):
    C = HG_CHUNK
    n = uq.shape[0] // C
    nh = uq.shape[1] // HG_EXPAND
    stTs = list(stTs)
    q = _silu(uq)
    f = lb + (1.0 - lb) * jax.nn.sigmoid(uf)
    k = 1.0 - f
    g = jnp.log(f)
    r = lax.broadcasted_iota(jnp.int32, (C, C), 0)
    c = lax.broadcasted_iota(jnp.int32, (C, C), 1)
    mask = (r <= c) if rev else (r >= c)
    Tm = mask.astype(F32)
    outs = [[None] * n for _ in range(nh)]
    for i in (reversed(range(n)) if rev else range(n)):
        sl = slice(i * C, (i + 1) * C)
        qi, ki, vi = q[sl], k[sl], ui[sl]
        G = _mm_tri(mask, g[sl])
        Gr = G[C // 2:C // 2 + 1]
        Gl = G[0:1] if rev else G[C - 1:C]
        q_in, k_in = qi * jnp.exp(G - Gr), ki * jnp.exp(Gr - G)
        q_st, k_st, e_l = qi * jnp.exp(G), ki * jnp.exp(Gl - G), jnp.exp(Gl)
        for h in range(nh):
            cs = slice(h * HG_EXPAND, (h + 1) * HG_EXPAND)
            att = jnp.where(mask, _mm_nt(q_in[:, cs], k_in[:, cs]), 0.0)
            outs[h][i] = _mm(att, vi[:, cs]) + _mm_nt(q_st[:, cs], stTs[h])
            stTs[h] = stTs[h] * e_l[:, cs] + _mm_tn(vi[:, cs], k_st[:, cs])
    o = jnp.concatenate([jnp.concatenate(outs[h], axis=0) for h in range(nh)], axis=1)
    return o, stTs


def _hg_specs(nb, rev_order, f_col):
    R = HG_ROWS
    gw = HG_HPS * HG_EXPAND
    ng = HG_HEADS // HG_HPS

    def bidx(s):
        return nb - 1 - s if rev_order else s

    def col(base):
        return pl.BlockSpec((R, gw), lambda h, s: (bidx(s), base * ng + h))

    out = pl.BlockSpec((R, gw), lambda h, s: (bidx(s), h))
    lb = pl.BlockSpec((1, gw), lambda h, s: (0, h))
    st = pl.BlockSpec((1, 1, HG_HPS, HG_EXPAND, HG_EXPAND), lambda h, s: (h, bidx(s), 0, 0, 0))
    return col(0), col(f_col), col(3), out, lb, st


def hg_scan_fwd(name, u, lb, rev, comm=None):
    S = u.shape[0]
    nb = S // HG_ROWS
    ng = HG_HEADS // HG_HPS
    q_s, f_s, i_s, o_s, lb_s, st_s = _hg_specs(nb, rev, 2 if rev else 1)

    def body(uq, uf, ui, lb_ref, o_ref, st_ref, state):
        @pl.when(pl.program_id(1) == 0)
        def _():
            state[...] = jnp.zeros_like(state)

        st_ref[0, 0] = state[...]
        o, st = _hg_block(rev, [state[h] for h in range(HG_HPS)], uq[...], uf[...], ui[...], lb_ref[...])
        o_ref[...] = o
        for h in range(HG_HPS):
            state[h] = st[h]

    (o, st), got = call_with_comm(
        body, comm, name=name, grid=(ng, nb), in_specs=[q_s, f_s, i_s, lb_s], out_specs=[o_s, st_s],
        out_shape=[jax.ShapeDtypeStruct((S, HG_W), F32),
                   jax.ShapeDtypeStruct((ng, nb, HG_HPS, HG_EXPAND, HG_EXPAND), F32)],
        scratch_shapes=[pltpu.VMEM((HG_HPS, HG_EXPAND, HG_EXPAND), F32)],
        semantics=("parallel", "arbitrary"), args=(u, u, u, lb))
    return o, st, got


def hg_scan_bwd(name, u, lb, states, do, rev, comm=None):
    S = u.shape[0]
    nb = S // HG_ROWS
    ng = HG_HEADS // HG_HPS
    q_s, f_s, i_s, o_s, lb_s, st_s = _hg_specs(nb, not rev, 2 if rev else 1)

    def body(uq, uf, ui, lb_ref, st_ref, do_ref, dq_ref, df_ref, di_ref, dlb_ref, dstate):
        @pl.when(pl.program_id(1) == 0)
        def _():
            dstate[...] = jnp.zeros_like(dstate)
            dlb_ref[...] = jnp.zeros_like(dlb_ref)

        _, vjp = jax.vjp(functools.partial(_hg_block, rev), [st_ref[0, 0, h] for h in range(HG_HPS)],
                         uq[...], uf[...], ui[...], lb_ref[...])
        dst, dq, df, di, dlb = vjp((do_ref[...], [dstate[h] for h in range(HG_HPS)]))
        dq_ref[...] = dq
        df_ref[...] = df
        di_ref[...] = di
        dlb_ref[...] += dlb
        for h in range(HG_HPS):
            dstate[h] = dst[h]

    outs, got = call_with_comm(
        body, comm, name=name, grid=(ng, nb), in_specs=[q_s, f_s, i_s, lb_s, st_s, o_s],
        out_specs=[o_s, o_s, o_s, lb_s],
        out_shape=[jax.ShapeDtypeStruct((S, HG_W), F32)] * 3 + [jax.ShapeDtypeStruct((1, HG_W), F32)],
        scratch_shapes=[pltpu.VMEM((HG_HPS, HG_EXPAND, HG_EXPAND), F32)],
        semantics=("parallel", "arbitrary"), args=(u, u, u, lb, states, do))
    return (*outs, got)


def _hg_lb(hgrn_lb, layer):
    m = jnp.max(hgrn_lb, axis=0, keepdims=True)
    e = jnp.exp(hgrn_lb - m)
    sm = e / jnp.sum(e, axis=0, keepdims=True)
    lb = jnp.zeros_like(sm[0:1])
    for i in range(1, layer + 1):
        lb = lb + sm[i:i + 1]
    return lb


def _hg_combine(of, ob, gate, ng):
    return _rms_groups(of + ob, ng, HG_EXPAND) * _silu(gate)


def hg_forward(x, hn, w, layer, comm=None):
    comm = comm or {}
    u = matmul("hg_in", hn, w["hg_w_in"])
    (lb,) = smallcall("hg_lb", lambda t: (_hg_lb(t, layer),), [w["hgrn_lb"]], [(1, HG_W)])
    of, stf, got_f = hg_scan_fwd("hg_scan_f", u, lb, False, comm.get("hg_scan_f"))
    ob, stb, _ = hg_scan_fwd("hg_scan_b", u, lb, True)
    (og,), _ = rowcall("hg_combine", lambda of, ob, gate, ng: ((_hg_combine(of, ob, gate, ng),), ()),
                       [of, ob, _col(u, HG_W, 4)], [w["hg_norm_g"]], [(HG_W, BF16)], tb=256)
    out = matmul("hg_out", og, w["hg_w_out"], res=x)
    return out, dict(hn=hn, u=u, lb=lb, of=of, ob=ob, stf=stf, stb=stb, og=og, got=dict(hg_scan_f=got_f))


def hg_backward(dy, sv, w, layer, comm=None):
    comm = comm or {}
    u, lb = sv["u"], sv["lb"]
    dog = matmul("hg_out_dx", dy, w["hg_w_out"], mode="nt")
    g_w_out = matmul("hg_out_dw", sv["og"], dy, mode="tn")

    def comb_bwd(of, ob, gate, dog, ng):
        _, vjp = jax.vjp(_hg_combine, of, ob, gate, ng)
        dof, _, dgate, dng = vjp(dog)
        return (dof, dgate), (dng,)

    (do, dgate), (g_norm,) = rowcall("hg_combine_bwd", comb_bwd, [sv["of"], sv["ob"], _col(u, HG_W, 4), dog],
                                     [w["hg_norm_g"]], [(HG_W, F32)] * 2, [(1, HG_W)], tb=256)
    dqf, dff, dif, dlbf, got_f = hg_scan_bwd("hg_scan_f_bwd", u, lb, sv["stf"], do, False, comm.get("hg_scan_f_bwd"))
    dqb, dfb, dib, dlbb, _ = hg_scan_bwd("hg_scan_b_bwd", u, lb, sv["stb"], do, True)

    def gather(dqf, dqb, dff, dfb, dif, dib, dgate):
        return (jnp.concatenate([dqf + dqb, dff, dfb, dif + dib, dgate], axis=1),), ()

    (du,), _ = rowcall("hg_du", gather, [dqf, dqb, dff, dfb, dif, dib, dgate], [], [(HG_IN, BF16)], tb=256)

    def lb_bwd(t, dlbf, dlbb):
        _, vjp = jax.vjp(lambda t: _hg_lb(t, layer), t)
        return vjp(dlbf + dlbb)

    (g_lb,) = smallcall("hg_lb_bwd", lb_bwd, [w["hgrn_lb"], dlbf, dlbb], [(DEPTH, HG_W)])
    hn = sv["hn"]
    g_w_in = matmul("hg_in_dw", hn, du, mode="tn")
    dhn = matmul("hg_in_dx", du, w["hg_w_in"], mode="nt")
    return dhn, dict(hg_w_in=g_w_in[None], hg_norm_g=g_norm, hg_w_out=g_w_out[None], hgrn_lb=g_lb,
                     got=dict(hg_scan_f_bwd=got_f))


def _rope_tables(S):
    t = np.arange(S)
    row = (t // GRID_W).astype(np.float32)
    col = (t % GRID_W).astype(np.float32)
    inv = (ROPE_THETA ** (-np.arange(0, ROPE_AXIS, 2, dtype=np.float32) / ROPE_AXIS)).astype(np.float32)
    ar = jnp.asarray(row)[:, None] * jnp.asarray(inv)[None, :]
    ac = jnp.asarray(col)[:, None] * jnp.asarray(inv)[None, :]
    cos = jnp.concatenate([jnp.cos(ar), jnp.cos(ar), jnp.cos(ac), jnp.cos(ac)], axis=1)
    sin = jnp.concatenate([-jnp.sin(ar), jnp.sin(ar), -jnp.sin(ac), jnp.sin(ac)], axis=1)
    return cos.astype(F32), sin.astype(F32)


def _rope(x, cos, sin):
    h = ROPE_AXIS // 2
    sw = jnp.concatenate([x[:, h:2 * h], x[:, 0:h], x[:, 3 * h:4 * h], x[:, 2 * h:3 * h]], axis=1)
    return x * cos + sw * sin


def _at_pre(uq, uk, cos, sin, qg, kg):
    qs, ks = [], []
    for h in range(AT_HEADS):
        qs.append(_rope(_rms(uq[:, h * AT_HD:(h + 1) * AT_HD], qg), cos, sin) * (AT_HD ** -0.5))
    for h in range(AT_KV):
        ks.append(_rope(_rms(uk[:, h * AT_HD:(h + 1) * AT_HD], kg), cos, sin))
    return jnp.concatenate(qs, axis=1), jnp.concatenate(ks, axis=1)


def _stack_heads(x):
    return jnp.concatenate([x[:, :AT_HD], x[:, AT_HD:]], axis=0)


def _unstack_heads(x):
    t = x.shape[0] // 2
    return jnp.concatenate([x[:t], x[t:]], axis=1)


def at_flash_fwd(q, k, u):
    S = q.shape[0]
    tq, tk = _pick(S, 512), _pick(S, 4096)
    nq, nk = S // tq, S // tk
    gw = AT_GRP * AT_HD

    def body(q_ref, k_ref, v_ref, o_ref, lse_ref, m_s, l_s, acc):
        j = pl.program_id(2)

        @pl.when(j == 0)
        def _():
            m_s[...] = jnp.full_like(m_s, NEG_BIG)
            l_s[...] = jnp.zeros_like(l_s)
            acc[...] = jnp.zeros_like(acc)

        s = _mm_nt(_stack_heads(q_ref[...]), k_ref[...])
        m_new = jnp.maximum(m_s[...], jnp.max(s, axis=-1, keepdims=True))
        alpha = jnp.exp(m_s[...] - m_new)
        p = jnp.exp(s - m_new)
        l_s[...] = alpha * l_s[...] + jnp.sum(p, axis=-1, keepdims=True)
        acc[...] = alpha * acc[...] + _mm(p, v_ref[...])
        m_s[...] = m_new

        @pl.when(j == nk - 1)
        def _():
            o_ref[...] = _unstack_heads(acc[...] / l_s[...])
            lse = m_s[...] + jnp.log(l_s[...])
            lse_ref[0, 0] = lse[:tq]
            lse_ref[0, 1] = lse[tq:]

    return pl.pallas_call(
        body, name="at_flash_fwd", grid=(AT_KV, nq, nk),
        in_specs=[pl.BlockSpec((tq, gw), lambda h, i, j: (i, h)),
                  pl.BlockSpec((tk, AT_HD), lambda h, i, j: (j, h)),
                  pl.BlockSpec((tk, AT_HD), lambda h, i, j: (j, (AT_QW + AT_KW) // AT_HD + h))],
        out_specs=[pl.BlockSpec((tq, gw), lambda h, i, j: (i, h)),
                   pl.BlockSpec((1, AT_GRP, tq, 1), lambda h, i, j: (h, 0, i, 0))],
        out_shape=[jax.ShapeDtypeStruct((S, AT_QW), F32), jax.ShapeDtypeStruct((AT_KV, AT_GRP, S, 1), F32)],
        scratch_shapes=[pltpu.VMEM((2 * tq, 1), F32), pltpu.VMEM((2 * tq, 1), F32), pltpu.VMEM((2 * tq, AT_HD), F32)],
        compiler_params=_cp(("parallel", "parallel", "arbitrary")),
    )(q, k, u)


def at_flash_bwd(q, k, u, o, lse, do):
    S = q.shape[0]
    tq, tk = _pick(S, 128), _pick(S, 4096)
    nq, nk = S // tq, S // tk
    gw = AT_GRP * AT_HD

    def body(q_ref, k_ref, v_ref, o_ref, lse_ref, do_ref, dq_ref, dk_ref, dv_ref, dk_acc, dv_acc):
        j, i = pl.program_id(1), pl.program_id(2)

        @pl.when(i == 0)
        def _():
            dk_acc[...] = jnp.zeros_like(dk_acc)
            dv_acc[...] = jnp.zeros_like(dv_acc)

        q2 = _stack_heads(q_ref[...])
        do_blk = do_ref[...]
        do2 = _stack_heads(do_blk)
        delta = _stack_heads(do_blk * o_ref[...])
        delta = jnp.sum(delta, axis=-1, keepdims=True)
        kb, vb = k_ref[...], v_ref[...]
        p = jnp.exp(_mm_nt(q2, kb) - jnp.concatenate([lse_ref[0, 0], lse_ref[0, 1]], axis=0))
        dv_acc[...] += _mm_tn(p, do2)
        ds = p * (_mm_nt(do2, vb) - delta)
        dk_acc[...] += _mm_tn(ds, q2)
        dq = _unstack_heads(_mm(ds, kb))
        rows = pl.ds(pl.multiple_of(i * tq, tq), tq)

        @pl.when(j == 0)
        def _():
            dq_ref[rows, :] = dq

        @pl.when(j > 0)
        def _():
            dq_ref[rows, :] += dq

        @pl.when(i == nq - 1)
        def _():
            dk_ref[...] = dk_acc[...]
            dv_ref[...] = dv_acc[...]

    return pl.pallas_call(
        body, name="at_flash_bwd", grid=(AT_KV, nk, nq),
        in_specs=[pl.BlockSpec((tq, gw), lambda h, j, i: (i, h)),
                  pl.BlockSpec((tk, AT_HD), lambda h, j, i: (j, h)),
                  pl.BlockSpec((tk, AT_HD), lambda h, j, i: (j, (AT_QW + AT_KW) // AT_HD + h)),
                  pl.BlockSpec((tq, gw), lambda h, j, i: (i, h)),
                  pl.BlockSpec((1, AT_GRP, tq, 1), lambda h, j, i: (h, 0, i, 0)),
                  pl.BlockSpec((tq, gw), lambda h, j, i: (i, h))],
        out_specs=[pl.BlockSpec((S, gw), lambda h, j, i: (0, h)),
                   pl.BlockSpec((tk, AT_HD), lambda h, j, i: (j, h)),
                   pl.BlockSpec((tk, AT_HD), lambda h, j, i: (j, h))],
        out_shape=[jax.ShapeDtypeStruct((S, AT_QW), F32), jax.ShapeDtypeStruct((S, AT_KW), F32),
                   jax.ShapeDtypeStruct((S, AT_KW), F32)],
        scratch_shapes=[pltpu.VMEM((tk, AT_HD), F32), pltpu.VMEM((tk, AT_HD), F32)],
        compiler_params=_cp(("parallel", "arbitrary", "arbitrary")),
    )(q, k, u, o, lse, do)


def at_forward(x, hn, w, comm=None):
    S = x.shape[0]
    u = matmul("at_in", hn, w["at_w_in"])
    cos, sin = _rope_tables(S)
    (q, k), _ = rowcall("at_pre", lambda uq, uk, c, s, qg, kg: (_at_pre(uq, uk, c, s, qg, kg), ()),
                        [_col(u, AT_QW, 0), _col(u, AT_KW, 2), cos, sin], [w["at_q_norm_g"], w["at_k_norm_g"]],
                        [(AT_QW, BF16), (AT_KW, BF16)], tb=256)
    o, lse = at_flash_fwd(q, k, u)
    (og,), _ = rowcall("at_gate", lambda o, gate: ((o * _silu(gate),), ()), [o, _col(u, AT_QW, 2)], [],
                       [(AT_QW, BF16)], tb=256)
    out = matmul("at_out", og, w["at_w_out"], res=x)
    return out, dict(hn=hn, u=u, cos=cos, sin=sin, q=q, k=k, o=o, lse=lse, og=og)


def at_backward(dy, sv, w, comm=None):
    u = sv["u"]
    dog = matmul("at_out_dx", dy, w["at_w_out"], mode="nt")
    g_w_out = matmul("at_out_dw", sv["og"], dy, mode="tn")

    def gate_bwd(o, gate, dog):
        _, vjp = jax.vjp(lambda o, gate: o * _silu(gate), o, gate)
        return vjp(dog), ()

    (do, dgate), _ = rowcall("at_gate_bwd", gate_bwd, [sv["o"], _col(u, AT_QW, 2), dog], [],
                             [(AT_QW, F32)] * 2, tb=256)
    dq, dk, dv = at_flash_bwd(sv["q"], sv["k"], u, sv["o"], sv["lse"], do)

    def pre_bwd(uq, uk, cos, sin, dq, dk, dv, dgate, qg, kg):
        _, vjp = jax.vjp(lambda uq, uk, qg, kg: _at_pre(uq, uk, cos, sin, qg, kg), uq, uk, qg, kg)
        duq, duk, dqg, dkg = vjp((dq, dk))
        return (jnp.concatenate([duq, duk, dv, dgate], axis=1),), (dqg, dkg)

    (du,), (g_qg, g_kg) = rowcall(
        "at_pre_bwd", pre_bwd, [_col(u, AT_QW, 0), _col(u, AT_KW, 2), sv["cos"], sv["sin"], dq, dk, dv, dgate],
        [w["at_q_norm_g"], w["at_k_norm_g"]], [(AT_IN, BF16)], [(1, AT_HD), (1, AT_HD)], tb=128)
    hn = sv["hn"]
    g_w_in = matmul("at_in_dw", hn, du, mode="tn")
    dhn = matmul("at_in_dx", du, w["at_w_in"], mode="nt")
    return dhn, dict(at_w_in=g_w_in[None], at_q_norm_g=g_qg, at_k_norm_g=g_kg, at_w_out=g_w_out[None])


def _t5_bucket_np(rel):
    half = REL_BUCKETS // 2
    exact = half // 2
    n = np.abs(rel)
    large = exact + (np.log(np.maximum(n, 1).astype(np.float32) / exact)
                     / math.log(REL_MAX_DIST / exact) * (half - exact)).astype(np.int32)
    large = np.minimum(large, half - 1)
    return np.where(rel > 0, half, 0) + np.where(n < exact, n, large)


def _dl_tq(S, dil):
    return min(128, S // dil)


def _dl_bias_maps(tq, dil):
    W = tq + 2 * DL_STEPS
    i = np.arange(tq)[:, None]
    wdx = np.arange(W)[None, :]
    dm = wdx - DL_STEPS - i
    bucket = _t5_bucket_np(dm * dil).reshape(-1).astype(np.int32)
    band = np.where(np.abs(dm) <= DL_STEPS, 0.0, NEG_BIG).reshape(1, -1).astype(np.float32)
    onehot = (jnp.asarray(bucket)[None, :] == jnp.arange(REL_BUCKETS, dtype=jnp.int32)[:, None]).astype(F32)
    return onehot, jnp.asarray(band)


def _dl_attend(q, kwin, vwin, T, valid):
    tq = q.shape[0]
    os, ls = [], []
    for h in range(DL_HEADS):
        sl = slice(h * DL_HD, (h + 1) * DL_HD)
        s = _mm_nt(q[:, sl] * (DL_HD ** -0.5), kwin[:, sl]) + T[h]
        s = jnp.where(valid, s, NEG_BIG)
        m = lax.stop_gradient(jnp.max(s, axis=-1, keepdims=True))
        e = jnp.exp(s - m)
        den = jnp.sum(e, axis=-1, keepdims=True)
        lse = m + jnp.log(den)
        p = e * (1.0 / den)
        os.append(_mm(p, vwin[:, sl]))
        ls.append(jnp.broadcast_to(lse, (tq, DL_HD)))
    return jnp.concatenate(os, axis=1), jnp.concatenate(ls, axis=1)


def _dl_specs(tq, Ls):
    nb = Ls // tq
    hs = DL_STEPS
    per = tq // hs
    nh = Ls // hs

    def main(c):
        return pl.BlockSpec((tq, DL_W), lambda r, i: (r * nb + i, c))

    def prev(c):
        return pl.BlockSpec((hs, DL_W), lambda r, i: (r * nh + jnp.maximum(i * per - 1, 0), c))

    def nxt(c):
        return pl.BlockSpec((hs, DL_W), lambda r, i: (r * nh + jnp.minimum((i + 1) * per, nh - 1), c))

    return nb, main, prev, nxt


def _dl_valid(i, tq, Ls):
    W = tq + 2 * DL_STEPS
    mk = i * tq - DL_STEPS + lax.broadcasted_iota(jnp.int32, (1, W), 1)
    return (mk >= 0) & (mk < Ls)


def dl_attn_fwd(gi, dil, u, T):
    S = u.shape[0]
    Ls = S // dil
    tq = _dl_tq(S, dil)
    nb, main, prev, nxt = _dl_specs(tq, Ls)
    out = main(0)

    def body(q_ref, kp, kc, kn, vp, vc, vn, T_ref, o_ref, l_ref):
        kwin = jnp.concatenate([kp[...], kc[...], kn[...]], axis=0)
        vwin = jnp.concatenate([vp[...], vc[...], vn[...]], axis=0)
        o, l = _dl_attend(q_ref[...], kwin, vwin, T_ref[...], _dl_valid(pl.program_id(1), tq, Ls))
        o_ref[...] = o
        l_ref[...] = l

    o, l = pl.pallas_call(
        body, name=f"dl_attn_fwd{gi}", grid=(dil, nb),
        in_specs=[main(0), prev(1), main(1), nxt(1), prev(2), main(2), nxt(2),
                  pl.BlockSpec(T.shape, _const_map(3))],
        out_specs=[out, out],
        out_shape=[jax.ShapeDtypeStruct((S, DL_W), F32)] * 2,
        compiler_params=_cp(("parallel", "parallel")),
    )(u, u, u, u, u, u, u, T)
    return o, l


def dl_attn_bwd(gi, dil, u, T, do, dl, dgate=None):
    S = u.shape[0]
    Ls = S // dil
    tq = _dl_tq(S, dil)
    hs = DL_STEPS
    W = tq + 2 * hs
    nb, main, prev, nxt = _dl_specs(tq, Ls)
    out = main(0)
    win = pl.BlockSpec((1, W, DL_W), lambda r, i: (r * nb + i, 0, 0))

    def body(q_ref, kp, kc, kn, vp, vc, vn, T_ref, do_ref, dl_ref, dq_ref, dkw_ref, dvw_ref, dT_ref):
        first = (pl.program_id(0) == 0) & (pl.program_id(1) == 0)

        @pl.when(first)
        def _():
            dT_ref[...] = jnp.zeros_like(dT_ref)

        kwin = jnp.concatenate([kp[...], kc[...], kn[...]], axis=0)
        vwin = jnp.concatenate([vp[...], vc[...], vn[...]], axis=0)
        valid = _dl_valid(pl.program_id(1), tq, Ls)
        _, vjp = jax.vjp(lambda q, k, v, T: _dl_attend(q, k, v, T, valid), q_ref[...].astype(F32), kwin.astype(F32),
                         vwin.astype(F32), T_ref[...])
        dq, dkw, dvw, dT = vjp((do_ref[...], dl_ref[...]))
        dq_ref[...] = dq
        dkw_ref[0] = dkw
        dvw_ref[0] = dvw
        dT_ref[...] += dT

    dq, dkw, dvw, dT = pl.pallas_call(
        body, name=f"dl_attn_bwd{gi}", grid=(dil, nb),
        in_specs=[main(0), prev(1), main(1), nxt(1), prev(2), main(2), nxt(2),
                  pl.BlockSpec(T.shape, _const_map(3)), out, out],
        out_specs=[out, win, win, pl.BlockSpec(T.shape, _const_map(3))],
        out_shape=[jax.ShapeDtypeStruct((S, DL_W), F32),
                   jax.ShapeDtypeStruct((dil * nb, W, DL_W), F32),
                   jax.ShapeDtypeStruct((dil * nb, W, DL_W), F32),
                   jax.ShapeDtypeStruct(T.shape, F32)],
        compiler_params=_cp(("arbitrary", "arbitrary")),
    )(u, u, u, u, u, u, u, T, do, dl)

    per = tq // hs
    n_out = 3 if dgate is None else 4

    def fold(*refs):
        dq_ref, kc, kp, kn, vc, vp, vn = refs[:7]
        du_ref = refs[-1]
        i = pl.program_id(1)
        has_p = (i > 0).astype(F32)
        has_n = (i < nb - 1).astype(F32)
        du_ref[:, 0:DL_W] = dq_ref[...].astype(BF16)
        for c, (c_ref, p_ref, n_ref) in enumerate(((kc, kp, kn), (vc, vp, vn)), start=1):
            mid = c_ref[0, hs:hs + tq, :]
            top = mid[0:hs] + p_ref[0] * has_p
            bot = mid[tq - hs:tq] + n_ref[0] * has_n
            parts = [top, bot] if tq == 2 * hs else ([top, mid[hs:tq - hs], bot] if tq > 2 * hs else [top + n_ref[0] * has_n])
            du_ref[:, c * DL_W:(c + 1) * DL_W] = jnp.concatenate(parts, axis=0).astype(BF16)
        if dgate is not None:
            du_ref[:, 3 * DL_W:4 * DL_W] = refs[7][...].astype(BF16)

    wfull = pl.BlockSpec((1, W, DL_W), lambda r, i: (r * nb + i, 0, 0))
    wprev = pl.BlockSpec((1, hs, DL_W), lambda r, i: (r * nb + jnp.maximum(i - 1, 0), per + 1, 0))
    wnext = pl.BlockSpec((1, hs, DL_W), lambda r, i: (r * nb + jnp.minimum(i + 1, nb - 1), 0, 0))
    extra_specs, extra_args = ([], []) if dgate is None else ([out], [dgate])
    du = pl.pallas_call(
        fold, name=f"dl_fold{gi}", grid=(dil, nb),
        in_specs=[out, wfull, wprev, wnext, wfull, wprev, wnext] + extra_specs,
        out_specs=pl.BlockSpec((tq, n_out * DL_W), lambda r, i: (r * nb + i, 0)),
        out_shape=jax.ShapeDtypeStruct((S, n_out * DL_W), BF16),
        compiler_params=_cp(("parallel", "parallel")),
    )(dq, dkw, dkw, dkw, dvw, dvw, dvw, *extra_args)
    return du, dT


def _dl_merge(o0, o1, o2, l0, l1, l2, gate):
    m = jnp.maximum(jnp.maximum(l0, l1), l2)
    e0, e1, e2 = jnp.exp(l0 - m), jnp.exp(l1 - m), jnp.exp(l2 - m)
    den = e0 + e1 + e2
    return ((e0 * o0 + e1 * o1 + e2 * o2) / den) * _silu(gate)


DL_DILS = tuple(d for _, d in DL_PAIRS)


def _dl_group_weights(w_in):
    g3 = 3 * DL_W
    return [jnp.concatenate([w_in[:, :g3], w_in[:, 3 * g3:]], axis=1), w_in[:, g3:2 * g3], w_in[:, 2 * g3:3 * g3]]


def dl_forward(x, hns, w, comm=None):
    S = x.shape[0]
    wg = _dl_group_weights(w["dl_w_in"])
    rbT = w["rel_bias"].T
    us, os, ls, Ts, maps = [], [], [], [], []
    for gi, dil in enumerate(DL_DILS):
        u = matmul(f"dl_in{gi}", hns[gi], wg[gi], out_dtype=F32 if gi == 0 else BF16)
        tq = _dl_tq(S, dil)
        W = tq + 2 * DL_STEPS
        onehot, band = _dl_bias_maps(tq, dil)
        (T,) = smallcall(f"dl_bias{gi}", lambda rbT, oh, band: (_mm_exact(rbT, oh) + band,), [rbT, onehot, band],
                         [(DL_HEADS, tq * W)])
        T = T.reshape(DL_HEADS, tq, W)
        o, l = dl_attn_fwd(gi, dil, u, T)
        us.append(u)
        os.append(o)
        ls.append(l)
        Ts.append(T)
        maps.append(onehot)
    rows = [a if d == 1 else _perm(a, d) for a, d in zip(os + ls, DL_DILS * 2)] + [_col(us[0], DL_W, 3)]
    (og,), _ = rowcall("dl_merge", lambda *a: ((_dl_merge(*a),), ()), rows, [], [(DL_W, BF16)], tb=256)
    out = matmul("dl_out", og, w["dl_w_out"], res=x)
    return out, dict(hns=hns, us=us, os=os, ls=ls, Ts=Ts, maps=maps, og=og, wg=wg)


def dl_backward(dy, sv, w, comm=None):
    us = sv["us"]
    dog = matmul("dl_out_dx", dy, w["dl_w_out"], mode="nt")
    g_w_out = matmul("dl_out_dw", sv["og"], dy, mode="tn")

    def merge_bwd(o0, o1, o2, l0, l1, l2, gate, dog):
        _, vjp = jax.vjp(_dl_merge, o0, o1, o2, l0, l1, l2, gate)
        return vjp(dog), ()

    rows = [a if d == 1 else _perm(a, d) for a, d in zip(sv["os"] + sv["ls"], DL_DILS * 2)] + [_col(us[0], DL_W, 3), dog]
    grads7, _ = rowcall("dl_merge_bwd", merge_bwd, rows, [], [(DL_W, F32, d) for d in DL_DILS * 2] + [(DL_W, F32)], tb=256)
    dos, dls, dgate = grads7[0:3], grads7[3:6], grads7[6]
    g_rbT, g_ws, dhns = None, [], []
    for gi, dil in enumerate(DL_DILS):
        du, dT = dl_attn_bwd(gi, dil, us[gi], sv["Ts"][gi], dos[gi], dls[gi], dgate if gi == 0 else None)
        (g,) = smallcall(f"dl_bias_bwd{gi}", lambda dT, oh: (_mm_nt_exact(dT, oh),),
                         [dT.reshape(DL_HEADS, -1), sv["maps"][gi]], [(DL_HEADS, REL_BUCKETS)])
        g_rbT = g if g_rbT is None else g_rbT + g
        g_ws.append(matmul(f"dl_in_dw{gi}", sv["hns"][gi], du, mode="tn"))
        dhns.append(matmul(f"dl_in_dx{gi}", du, sv["wg"][gi], mode="nt"))
    g3 = 3 * DL_W
    g_w_in = jnp.concatenate([g_ws[0][:, :g3], g_ws[1], g_ws[2], g_ws[0][:, g3:]], axis=1)
    return tuple(dhns), dict(dl_w_in=g_w_in[None], dl_w_out=g_w_out[None], rel_bias=g_rbT.T)


_FWD = (ssd_forward, hg_forward, at_forward, dl_forward)
_BWD = (ssd_backward, hg_backward, at_backward, dl_backward)


def _norm_dils(layer):
    return DL_DILS if layer % 4 == 3 else (1,)


class NoExchange:
    def fwd_plans(self, layer, w):
        return None

    def fwd_done(self, layer, got, w):
        pass

    def bwd_plans(self, layer, grads):
        return None

    def bwd_done(self, layer, got):
        pass


def local_step(x, tgt, w, sched=None):
    sched = sched or NoExchange()
    saved = []
    h = x
    for layer in range(DEPTH):
        hn = norm_fwd(f"norm{layer}", h, w["norm_g"][layer:layer + 1], _norm_dils(layer))
        extra = (layer,) if layer % 4 == 1 else ()
        h_next, sv = _FWD[layer % 4](h, hn, w, *extra, comm=sched.fwd_plans(layer, w))
        sched.fwd_done(layer, sv.get("got", {}), w)
        saved.append((h, sv))
        h = h_next
    loss, dh, g_final = loss_head(h, tgt, w["final_g"].reshape(1, D_MODEL))
    grads = {}
    g_norm = [None] * DEPTH
    for layer in reversed(range(DEPTH)):
        h_in, sv = saved[layer]
        extra = (layer,) if layer % 4 == 1 else ()
        dhn, g = _BWD[layer % 4](dh, sv, w, *extra, comm=sched.bwd_plans(layer, grads))
        sched.bwd_done(layer, g.pop("got", {}))
        grads.update(g)
        dh, g_norm[layer] = norm_bwd(f"norm{layer}_bwd", h_in, w["norm_g"][layer:layer + 1], dhn, dh, _norm_dils(layer))
    grads["norm_g"] = jnp.concatenate(g_norm, axis=0)
    grads["final_g"] = g_final.reshape(D_MODEL)
    grads["ssd_d"] = jnp.sum(grads.pop("ssd_d_exp").reshape(SSD_HEADS, SSD_HEADDIM), axis=1)[None]
    return loss, dh, grads


IN_NAMES = ("ssd_w_in", "hg_w_in", "at_w_in", "dl_w_in")
OUT_NAMES = ("ssd_w_out", "hg_w_out", "at_w_out", "dl_w_out")
IN_COLS = (SSD_IN // 4, HG_IN // 4, AT_IN // 4, DL_IN // 4)
OUT_ROWS = (SSD_DI // 4, HG_W // 4, AT_QW // 4, DL_W // 4)
PACK_IN = sum(IN_COLS)
PACK_OUT = sum(OUT_ROWS)
N_CHIPS = 4
N_DEV = 8
HBM = pl.BlockSpec(memory_space=pl.ANY)


def _mesh_pos():
    return lax.axis_index("x"), lax.axis_index("y"), lax.axis_index("c")


def _other_chips(x, y):
    return [(1 - x, y), (x, 1 - y), (1 - x, 1 - y)]


def _half_rows(half, n):
    return pl.ds(pl.multiple_of(half * n, n), n)


def _remote(src, dst, send, recv, k, to):
    return pltpu.make_async_remote_copy(src_ref=src, dst_ref=dst, send_sem=send.at[k], recv_sem=recv.at[k],
                                        device_id=to, device_id_type=MESH)


def gather_plan(packs, whole=()):
    arrs = list(packs) + list(whole)
    n_half = len(packs)

    def pieces(ins, outs):
        x, y, c = _mesh_pos()
        for a, (src, dst) in enumerate(zip(ins, outs)):
            h = src.shape[0] // 2 if a < n_half else None
            for j, (px, py) in enumerate(_other_chips(x, y)):
                yield a, j, src, dst, h, (x, y, c), (px, py)

    def start(ins, outs, send, recv):
        for a, j, src, dst, h, (x, y, c), (px, py) in pieces(ins, outs):
            me = 2 * x + y
            if h is None:
                _remote(src, dst.at[me], send, recv, 6 * a + j, (px, py, c)).start()
            else:
                _remote(src.at[_half_rows(c, h)], dst.at[me, _half_rows(c, h)], send, recv, 6 * a + j, (px, py, c)).start()

    def mid(ins, outs, send, recv):
        for a, j, src, dst, h, (x, y, c), (px, py) in pieces(ins, outs):
            kp = 2 * px + py
            if h is None:
                _remote(src, dst.at[kp], send, recv, 6 * a + j, (px, py, c)).wait_recv()
            else:
                got = dst.at[kp, _half_rows(c, h)]
                _remote(src.at[_half_rows(c, h)], got, send, recv, 6 * a + j, (px, py, c)).wait_recv()
                _remote(got, got, send, recv, 6 * a + 3 + j, (x, y, 1 - c)).start()

    def finish(ins, outs, send, recv):
        for a, j, src, dst, h, (x, y, c), (px, py) in pieces(ins, outs):
            me, kp = 2 * x + y, 2 * px + py
            if h is None:
                _remote(src, dst.at[me], send, recv, 6 * a + j, (px, py, c)).wait_send()
            else:
                theirs = dst.at[kp, _half_rows(1 - c, h)]
                _remote(theirs, theirs, send, recv, 6 * a + 3 + j, (x, y, 1 - c)).wait_recv()
                _remote(src.at[_half_rows(c, h)], dst.at[me, _half_rows(c, h)], send, recv, 6 * a + j, (px, py, c)).wait_send()
                mine = dst.at[kp, _half_rows(c, h)]
                _remote(mine, mine, send, recv, 6 * a + 3 + j, (x, y, 1 - c)).wait_send()

    return dict(ins=arrs, out_shape=[jax.ShapeDtypeStruct((N_CHIPS,) + a.shape, a.dtype) for a in arrs],
                n_sems=6 * len(arrs), start=start, mid=mid, finish=finish)


def scatter_plan(halves):
    def copies(ins, outs, send, recv):
        x, y, c = _mesh_pos()
        for a, (src, dst) in enumerate(zip(ins, outs)):
            for j, (px, py) in enumerate(_other_chips(x, y)):
                yield _remote(src.at[2 * px + py], dst.at[j], send, recv, 3 * a + j, (px, py, c))

    def start(ins, outs, send, recv):
        for cp in copies(ins, outs, send, recv):
            cp.start()

    def finish(ins, outs, send, recv):
        for cp in copies(ins, outs, send, recv):
            cp.wait()

    return dict(ins=list(halves), out_shape=[jax.ShapeDtypeStruct((3,) + a.shape[1:], a.dtype) for a in halves],
                n_sems=3 * len(halves), start=start, mid=None, finish=finish)


def run_exchange(name, plan):
    n_in = len(plan["ins"])

    def body(*refs):
        ins, outs = refs[:n_in], refs[n_in:-2]
        send, recv = refs[-2], refs[-1]
        plan["start"](ins, outs, send, recv)
        if plan["mid"] is not None:
            plan["mid"](ins, outs, send, recv)
        plan["finish"](ins, outs, send, recv)

    return pl.pallas_call(
        body, name=name, in_specs=[HBM] * n_in, out_specs=[HBM] * len(plan["out_shape"]), out_shape=plan["out_shape"],
        scratch_shapes=[pltpu.SemaphoreType.DMA((plan["n_sems"],))] * 2,
        compiler_params=pltpu.CompilerParams(has_side_effects=True),
    )(*plan["ins"])


def swap_halves(name, g_in, g_out):
    h_in, h_out = g_in.shape[1] // 2, g_out.shape[1] // 2

    def body(gi, go, ri, ro, send, recv):
        x, y, c = _mesh_pos()
        sib = (x, y, 1 - c)

        def rows(half, n):
            return pl.ds(pl.multiple_of(half * n, n), n)

        cps = [pltpu.make_async_remote_copy(src_ref=gi.at[:, rows(1 - c, h_in)], dst_ref=ri, send_sem=send.at[0],
                                            recv_sem=recv.at[0], device_id=sib, device_id_type=MESH),
               pltpu.make_async_remote_copy(src_ref=go.at[:, rows(1 - c, h_out)], dst_ref=ro, send_sem=send.at[1],
                                            recv_sem=recv.at[1], device_id=sib, device_id_type=MESH)]
        for cp in cps:
            cp.start()
        for cp in cps:
            cp.wait()

    return pl.pallas_call(
        body, name=name, in_specs=[HBM, HBM], out_specs=[HBM, HBM],
        out_shape=[jax.ShapeDtypeStruct((N_CHIPS, h_in, g_in.shape[2]), g_in.dtype),
                   jax.ShapeDtypeStruct((N_CHIPS, h_out, g_out.shape[2]), g_out.dtype)],
        scratch_shapes=[pltpu.SemaphoreType.DMA((2,)), pltpu.SemaphoreType.DMA((2,))],
        compiler_params=pltpu.CompilerParams(has_side_effects=True),
    )(g_in, g_out)


def half_add(name, g, r, c_idx, tb):
    _, rows2, C = g.shape
    h = rows2 // 2
    nb = h // tb

    def body(c_ref, g_ref, r_ref, f_ref, b_ref):
        s = g_ref[...] + r_ref[...]
        f_ref[...] = s
        b_ref[...] = s.astype(BF16)

    grid_spec = pltpu.PrefetchScalarGridSpec(
        num_scalar_prefetch=1, grid=(N_CHIPS, nb),
        in_specs=[pl.BlockSpec((1, tb, C), lambda k, i, c: (k, c[0] * nb + i, 0)),
                  pl.BlockSpec((1, tb, C), lambda k, i, c: (k, i, 0))],
        out_specs=[pl.BlockSpec((1, tb, C), lambda k, i, c: (k, i, 0))] * 2)
    return pl.pallas_call(
        body, name=name, grid_spec=grid_spec,
        out_shape=[jax.ShapeDtypeStruct((N_CHIPS, h, C), F32), jax.ShapeDtypeStruct((N_CHIPS, h, C), BF16)],
        compiler_params=_cp(("parallel", "parallel")),
    )(c_idx, g, r)


def chip_sum(name, f, r, me_idx, tb):
    _, h, C = f.shape
    nb = h // tb

    def body(me_ref, f_ref, r0, r1, r2, o_ref):
        o_ref[...] = ((f_ref[0] + r0[0].astype(F32)) + r1[0].astype(F32)) + r2[0].astype(F32)

    def slot(j):
        return pl.BlockSpec((1, tb, C), lambda i, me: (j, i, 0))

    grid_spec = pltpu.PrefetchScalarGridSpec(
        num_scalar_prefetch=1, grid=(nb,),
        in_specs=[pl.BlockSpec((1, tb, C), lambda i, me: (me[0], i, 0)), slot(0), slot(1), slot(2)],
        out_specs=pl.BlockSpec((tb, C), lambda i, me: (i, 0)))
    return pl.pallas_call(
        body, name=name, grid_spec=grid_spec, out_shape=jax.ShapeDtypeStruct((h, C), F32),
        compiler_params=_cp(("parallel",)),
    )(me_idx, f, r, r, r)


def share_halves(name, f_in, f_out):
    def body(fi, fo, oi, oo, send, recv):
        x, y, c = _mesh_pos()
        sib = (x, y, 1 - c)
        cps = [pltpu.make_async_remote_copy(src_ref=fi, dst_ref=oi, send_sem=send.at[0], recv_sem=recv.at[0],
                                            device_id=sib, device_id_type=MESH),
               pltpu.make_async_remote_copy(src_ref=fo, dst_ref=oo, send_sem=send.at[1], recv_sem=recv.at[1],
                                            device_id=sib, device_id_type=MESH)]
        for cp in cps:
            cp.start()
        for cp in cps:
            cp.wait()

    return pl.pallas_call(
        body, name=name, in_specs=[HBM, HBM], out_specs=[HBM, HBM],
        out_shape=[jax.ShapeDtypeStruct(f_in.shape, F32), jax.ShapeDtypeStruct(f_out.shape, F32)],
        scratch_shapes=[pltpu.SemaphoreType.DMA((2,)), pltpu.SemaphoreType.DMA((2,))],
        compiler_params=pltpu.CompilerParams(has_side_effects=True),
    )(f_in, f_out)


def gather_small(pack):
    def body(p, g, send, recv, lsem):
        x, y, c = _mesh_pos()
        me = 4 * x + 2 * y + c
        local = pltpu.make_async_copy(p, g.at[me], lsem)
        local.start()
        cps = []
        k = 0
        for fx in (0, 1):
            for fy in (0, 1):
                for fc in (0, 1):
                    if fx + fy + fc == 0:
                        continue
                    to = (x ^ fx, y ^ fy, c ^ fc)
                    cps.append((pltpu.make_async_remote_copy(src_ref=p, dst_ref=g.at[me], send_sem=send.at[k],
                                                             recv_sem=recv.at[k], device_id=to, device_id_type=MESH), to, k))
                    k += 1
        for cp, _, _ in cps:
            cp.start()
        for cp, to, k in cps:
            frm = 4 * to[0] + 2 * to[1] + to[2]
            pltpu.make_async_remote_copy(src_ref=p, dst_ref=g.at[frm], send_sem=send.at[k], recv_sem=recv.at[k],
                                         device_id=to, device_id_type=MESH).wait_recv()
        for cp, _, _ in cps:
            cp.wait_send()
        local.wait()

    return pl.pallas_call(
        body, name="gather_small", in_specs=[HBM], out_specs=HBM,
        out_shape=jax.ShapeDtypeStruct((N_DEV,) + pack.shape, pack.dtype),
        scratch_shapes=[pltpu.SemaphoreType.DMA((7,)), pltpu.SemaphoreType.DMA((7,)), pltpu.SemaphoreType.DMA],
        compiler_params=pltpu.CompilerParams(has_side_effects=True),
    )(pack)


def _adamw(w, g, m, v):
    m = ADAM_B1 * m + (1.0 - ADAM_B1) * g
    v = ADAM_B2 * v + (1.0 - ADAM_B2) * (g * g)
    m_hat = m / (1.0 - ADAM_B1 ** ADAM_STEP)
    v_hat = v / (1.0 - ADAM_B2 ** ADAM_STEP)
    delta = -ADAM_LR * (m_hat / (jnp.sqrt(v_hat) + ADAM_EPS) + ADAM_WD * w)
    return delta, m, v


def adamw_big(name, w, g, m, v):
    shp = w.shape
    flat = lambda a: a.reshape(shp[-2], shp[-1])
    (d, nm, nv), _ = rowcall(name, lambda w, g, m, v: (_adamw(w, g, m, v), ()), [flat(w), flat(g), flat(m), flat(v)], [],
                             [(shp[-1], F32)] * 3, tb=256)
    return d.reshape(shp), nm.reshape(shp), nv.reshape(shp)


def _pack_small(arrs):
    flat = jnp.concatenate([a.reshape(-1) for a in arrs])
    n = flat.shape[0]
    rows = -(-n // (SUBLANE * LANE)) * SUBLANE
    return jnp.pad(flat, (0, rows * LANE - n)).reshape(rows, LANE)


def _unpack_small(pack, shapes):
    flat = pack.reshape(-1)
    outs, off = [], 0
    for s in shapes:
        n = int(np.prod(s))
        outs.append(flat[off:off + n].reshape(s))
        off += n
    return outs


SMALL_NAMES = ("norm_g", "final_g", "rel_bias", "hgrn_lb", "ssd_conv_w", "ssd_conv_b", "ssd_dt_bias", "ssd_a_log",
               "ssd_d", "ssd_norm_g", "hg_norm_g", "at_q_norm_g", "at_k_norm_g")
ALL_NAMES = ("norm_g", "final_g", "rel_bias", "hgrn_lb", "ssd_w_in", "ssd_conv_w", "ssd_conv_b", "ssd_dt_bias",
             "ssd_a_log", "ssd_d", "ssd_norm_g", "ssd_w_out", "hg_w_in", "hg_norm_g", "hg_w_out", "at_w_in",
             "at_q_norm_g", "at_k_norm_g", "at_w_out", "dl_w_in", "dl_w_out")


def kernel(x, norm_g, final_g, rel_bias, hgrn_lb, ssd_w_in, ssd_conv_w, ssd_conv_b, ssd_dt_bias, ssd_a_log, ssd_d, ssd_norm_g, ssd_w_out, hg_w_in, hg_norm_g, hg_w_out, at_w_in, at_q_norm_g, at_k_norm_g, at_w_out, dl_w_in, dl_w_out, loss_target, m_norm_g, m_final_g, m_rel_bias, m_hgrn_lb, m_ssd_w_in, m_ssd_conv_w, m_ssd_conv_b, m_ssd_dt_bias, m_ssd_a_log, m_ssd_d, m_ssd_norm_g, m_ssd_w_out, m_hg_w_in, m_hg_norm_g, m_hg_w_out, m_at_w_in, m_at_q_norm_g, m_at_k_norm_g, m_at_w_out, m_dl_w_in, m_dl_w_out, v_norm_g, v_final_g, v_rel_bias, v_hgrn_lb, v_ssd_w_in, v_ssd_conv_w, v_ssd_conv_b, v_ssd_dt_bias, v_ssd_a_log, v_ssd_d, v_ssd_norm_g, v_ssd_w_out, v_hg_w_in, v_hg_norm_g, v_hg_w_out, v_at_w_in, v_at_q_norm_g, v_at_k_norm_g, v_at_w_out, v_dl_w_in, v_dl_w_out):
    args = locals()
    W = {n: args[n] for n in ALL_NAMES}
    M = {n: args["m_" + n] for n in ALL_NAMES}
    V = {n: args["v_" + n] for n in ALL_NAMES}
    xi, yi, ci = lax.axis_index("x"), lax.axis_index("y"), lax.axis_index("c")
    chip = 2 * xi + yi
    conv_shard = SSD_CONV_CH // N_CHIPS
    hgn_shard = HG_W // N_CHIPS

    p_in = [W[n][0].astype(BF16) for n in IN_NAMES]
    p_out = [W[n][0].astype(BF16) for n in OUT_NAMES]
    p_small = jnp.concatenate([
        jnp.pad(ssd_conv_w[0], ((0, 0), (0, D_MODEL - conv_shard))),
        jnp.pad(hg_norm_g, ((0, 0), (0, D_MODEL - hgn_shard)))], axis=0)
    c_idx = ci.astype(jnp.int32).reshape(1)
    me_idx = chip.astype(jnp.int32).reshape(1)

    def slot(stack, own, k):
        return jnp.where(chip == k, own, stack[k])

    def layer_weights(layer, got):
        s_in, s_out = got[0], got[1]
        return (jnp.concatenate([slot(s_in, p_in[layer], k) for k in range(N_CHIPS)], axis=1),
                jnp.concatenate([slot(s_out, p_out[layer], k) for k in range(N_CHIPS)], axis=0))

    def reduce_start(tag, layers, grads):
        gp_in = jnp.concatenate([grads[IN_NAMES[l]][0].reshape(D_MODEL, N_CHIPS, IN_COLS[l]).transpose(1, 0, 2)
                                 for l in layers], axis=2)
        gp_out = jnp.concatenate([grads[OUT_NAMES[l]][0].reshape(N_CHIPS, OUT_ROWS[l], D_MODEL) for l in layers], axis=1)
        r_in, r_out = swap_halves(f"swap_halves_{tag}", gp_in, gp_out)
        f_in, b_in = half_add(f"half_add_in_{tag}", gp_in, r_in, c_idx, 128)
        f_out, b_out = half_add(f"half_add_out_{tag}", gp_out, r_out, c_idx, 128)
        return (f_in, f_out), scatter_plan([b_in, b_out])

    def reduce_finish(tag, layers, halves, got, G):
        s_in = chip_sum(f"chip_sum_in_{tag}", halves[0], got[0], me_idx, 128)
        s_out = chip_sum(f"chip_sum_out_{tag}", halves[1], got[1], me_idx, 128)
        o_in, o_out = share_halves(f"share_halves_{tag}", s_in, s_out)
        red_in = jnp.where(ci == 0, jnp.concatenate([s_in, o_in], axis=0), jnp.concatenate([o_in, s_in], axis=0))
        red_out = jnp.where(ci == 0, jnp.concatenate([s_out, o_out], axis=0), jnp.concatenate([o_out, s_out], axis=0))
        off_c = off_r = 0
        for l in layers:
            G[IN_NAMES[l]] = red_in[:, off_c:off_c + IN_COLS[l]][None]
            G[OUT_NAMES[l]] = red_out[off_r:off_r + OUT_ROWS[l]][None]
            off_c += IN_COLS[l]
            off_r += OUT_ROWS[l]

    class Schedule:
        early = (2, 3)

        def fwd_plans(self, layer, w):
            if layer == 0:
                return dict(ssd_scan_f=gather_plan([p_in[1], p_out[1]]), ssd_scan_b=gather_plan([p_in[2], p_out[2]]))
            if layer == 1:
                return dict(hg_scan_f=gather_plan([p_in[3], p_out[3]]))
            return None

        def fwd_done(self, layer, got, w):
            if layer == 0:
                w["hg_w_in"], w["hg_w_out"] = layer_weights(1, got["ssd_scan_f"])
                w["at_w_in"], w["at_w_out"] = layer_weights(2, got["ssd_scan_b"])
            if layer == 1:
                w["dl_w_in"], w["dl_w_out"] = layer_weights(3, got["hg_scan_f"])

        def bwd_plans(self, layer, grads):
            if layer == 1:
                self.halves, plan = reduce_start("a", self.early, grads)
                return dict(hg_scan_f_bwd=plan)
            return None

        def bwd_done(self, layer, got):
            if layer == 1:
                self.got = got["hg_scan_f_bwd"]

    g0_in, g0_out, g_small = run_exchange("gather_w0", gather_plan([p_in[0], p_out[0]], whole=[p_small]))
    ssd_in_full, ssd_out_full = layer_weights(0, (g0_in, g0_out))
    conv_full = jnp.concatenate([slot(g_small, p_small, k)[:SSD_CONV, :conv_shard] for k in range(N_CHIPS)], axis=1)
    hgn_full = jnp.concatenate([slot(g_small, p_small, k)[SSD_CONV:SSD_CONV + 1, :hgn_shard] for k in range(N_CHIPS)], axis=1)
    w = dict(
        norm_g=norm_g, final_g=final_g, rel_bias=rel_bias, hgrn_lb=hgrn_lb,
        ssd_w_main=ssd_in_full[:, :SSD_MAIN], ssd_w_dt=ssd_in_full[:, SSD_MAIN:],
        ssd_conv_w8=jnp.concatenate([conv_full, jnp.zeros((1, SSD_CONV_CH), F32)], axis=0),
        ssd_conv_b=ssd_conv_b, ssd_dt_bias=ssd_dt_bias, ssd_a_log=ssd_a_log, ssd_d=ssd_d, ssd_norm_g=ssd_norm_g,
        ssd_w_out=ssd_out_full, hg_norm_g=hgn_full, at_q_norm_g=at_q_norm_g, at_k_norm_g=at_k_norm_g)

    sched = Schedule()
    loss_tile, grad_x, grads = local_step(x[0], loss_target[0], w, sched)
    loss = lax.psum(loss_tile[0, 0], ("x", "y", "c"))

    G = {}
    late = (0, 1)
    halves_b, plan_b = reduce_start("b", late, grads)
    got_b = run_exchange("scatter_b", plan_b)
    reduce_finish("a", sched.early, sched.halves, sched.got, G)
    reduce_finish("b", late, halves_b, got_b, G)

    small_full = [grads[n].reshape(-1) for n in SMALL_NAMES]
    shapes_full = [grads[n].shape for n in SMALL_NAMES]
    packs = gather_small(_pack_small(small_full))
    (red_small,) = smallcall("sum_small", lambda p: (functools.reduce(lambda a, b: a + b, [p[k] for k in range(N_DEV)]),),
                             [packs], [packs.shape[1:]])
    for n, g in zip(SMALL_NAMES, _unpack_small(red_small, shapes_full)):
        G[n] = g
    G["ssd_conv_w"] = lax.dynamic_slice_in_dim(G["ssd_conv_w"].reshape(1, SSD_CONV, SSD_CONV_CH), chip * conv_shard, conv_shard, axis=2)
    G["hg_norm_g"] = lax.dynamic_slice_in_dim(G["hg_norm_g"].reshape(1, HG_W), chip * hgn_shard, hgn_shard, axis=1)
    for n in SMALL_NAMES:
        G[n] = G[n].reshape(W[n].shape)

    D, NM, NV = {}, {}, {}
    for n in IN_NAMES + OUT_NAMES:
        D[n], NM[n], NV[n] = adamw_big("adamw_" + n, W[n], G[n], M[n], V[n])
    shapes = [W[n].shape for n in SMALL_NAMES]
    pk = [_pack_small([T[n] for n in SMALL_NAMES]) for T in (W, G, M, V)]
    outs = smallcall("adamw_small", lambda w, g, m, v: _adamw(w, g, m, v), pk, [pk[0].shape] * 3)
    for T, pack in zip((D, NM, NV), outs):
        for n, a in zip(SMALL_NAMES, _unpack_small(pack, shapes)):
            T[n] = a
    return (loss, grad_x[None], *[G[n] for n in ALL_NAMES], *[D[n] for n in ALL_NAMES],
            *[NM[n] for n in ALL_NAMES], *[NV[n] for n in ALL_NAMES])
```

```python
import functools
import math

import numpy as np
import jax
import jax.numpy as jnp
from jax import lax
from jax.experimental import pallas as pl
from jax.experimental.pallas import tpu as pltpu

F32 = jnp.float32
BF16 = jnp.bfloat16
MESH = pl.DeviceIdType.MESH

D_MODEL = 1024
DEPTH = 4
GRID_W = 64
EPS = 1e-6
NEG_BIG = -1e30

SSD_DI = 2048
SSD_HEADDIM = 64
SSD_HEADS = 32
SSD_GROUPS = 4
SSD_HPG = 8
SSD_STATE = 128
SSD_CONV = 7
SSD_CHUNK = 128
SSD_GPS = 4
SSD_CONV_CH = SSD_DI + 2 * SSD_GROUPS * SSD_STATE
SSD_MAIN = SSD_DI + SSD_CONV_CH
SSD_IN = SSD_MAIN + 2 * SSD_HEADS

HG_HEADS = 8
HG_EXPAND = 128
HG_W = 1024
HG_CHUNK = 32
HG_ROWS = 256
HG_HPS = 8
HG_IN = 5 * HG_W

AT_HEADS = 16
AT_KV = 8
AT_GRP = 2
AT_HD = 128
ROPE_THETA = 10000.0
ROPE_AXIS = 64
AT_QW = AT_HEADS * AT_HD
AT_KW = AT_KV * AT_HD
AT_IN = 2 * AT_QW + 2 * AT_KW

DL_PAIRS = ((128, 1), (512, 4), (2048, 16))
DL_HEADS = 16
DL_HD = 64
DL_W = 1024
DL_STEPS = 64
DL_IN = 10 * DL_W
REL_BUCKETS = 32
REL_MAX_DIST = 1024

ADAM_LR = 0.001
ADAM_B1 = 0.9
ADAM_B2 = 0.999
ADAM_EPS = 1e-08
ADAM_WD = 0.01
ADAM_STEP = 10

VMEM_LIMIT = 56 * 1024 * 1024
LANE = 128
SUBLANE = 8


def _cp(sem=None):
    return pltpu.CompilerParams(dimension_semantics=sem, vmem_limit_bytes=VMEM_LIMIT)


_NN, _NT, _TN = ((1,), (0,)), ((1,), (1,)), ((0,), (0,))


def _dot(a, b, dims):
    return lax.dot_general(a.astype(BF16), b.astype(BF16), (dims, ((), ())), preferred_element_type=F32)


def _dot_rule(dims, da_rule, db_rule):
    @jax.custom_vjp
    def f(a, b):
        return _dot(a, b, dims)

    def fwd(a, b):
        return _dot(a, b, dims), (a, b)

    def bwd(res, g):
        a, b = res
        return da_rule(a, b, g).astype(a.dtype), db_rule(a, b, g).astype(b.dtype)

    f.defvjp(fwd, bwd)
    return f


_mm = _dot_rule(_NN, lambda a, b, g: _dot(g, b, _NT), lambda a, b, g: _dot(a, g, _TN))
_mm_nt = _dot_rule(_NT, lambda a, b, g: _dot(g, b, _NN), lambda a, b, g: _dot(g, a, _TN))
_mm_tn = _dot_rule(_TN, lambda a, b, g: _dot(b, g, _NT), lambda a, b, g: _dot(a, g, _NN))


def _mm_exact(a, b):
    return jnp.dot(a, b, preferred_element_type=F32, precision=lax.Precision.HIGHEST)


def _dot3(t, a, dims):
    hi = a.astype(BF16)
    r1 = a - hi.astype(F32)
    mid = r1.astype(BF16)
    lo = r1 - mid.astype(F32)
    return _dot(t, hi, dims) + (_dot(t, mid, dims) + _dot(t, lo, dims))


@jax.custom_vjp
def _mm_tri(t, a):
    return _dot3(t, a, _NN)


def _mm_tri_fwd(t, a):
    return _dot3(t, a, _NN), t


def _mm_tri_bwd(t, g):
    return None, _dot3(t, g, _TN)


_mm_tri.defvjp(_mm_tri_fwd, _mm_tri_bwd)


def _mm_nt_exact(a, b):
    return lax.dot_general(a, b, (((1,), (1,)), ((), ())), preferred_element_type=F32,
                           precision=lax.Precision.HIGHEST)


def _silu(x):
    return x * jax.nn.sigmoid(x)


def _softplus(z):
    return jnp.maximum(z, 0.0) + jnp.log(1.0 + jnp.exp(-jnp.abs(z)))


def _pick(dim, pref):
    best = None
    t = LANE
    while t <= min(dim, pref):
        if dim % t == 0:
            best = t
        t += LANE
    return best if best is not None else dim


def _const_map(n):
    return lambda *_: (0,) * n


MM_BLOCK_BYTES = 8 * 1024 * 1024


def _mm_tiles(mode, M, N, K, a_bytes, b_bytes):
    if mode == "nn":
        tk = K if K <= 2048 else _pick(K, 1024)
        tm = _pick(M, max(512, MM_BLOCK_BYTES // (tk * a_bytes)))
        tn = _pick(N, 512)
    elif mode == "tn":
        tk = K if K <= 4096 else _pick(K, 1024)
        tm = _pick(M, MM_BLOCK_BYTES // (tk * a_bytes))
        tn = _pick(N, MM_BLOCK_BYTES // (tk * b_bytes))
    else:
        tk = _pick(K, 1024)
        tn = _pick(N, 1024)
        tm = _pick(M, MM_BLOCK_BYTES // (8 * tn))
    return tm, tn, tk


def matmul(name, a, b, mode="nn", res=None, out_dtype=F32):
    if mode == "tn":
        K, M = a.shape
    else:
        M, K = a.shape
    N = b.shape[0] if mode == "nt" else b.shape[1]
    tm, tn, tk = _mm_tiles(mode, M, N, K, a.dtype.itemsize, b.dtype.itemsize)
    nk = K // tk
    a_spec = (pl.BlockSpec((tk, tm), lambda i, j, k: (k, i)) if mode == "tn"
              else pl.BlockSpec((tm, tk), lambda i, j, k: (i, k)))
    b_spec = (pl.BlockSpec((tn, tk), lambda i, j, k: (j, k)) if mode == "nt"
              else pl.BlockSpec((tk, tn), lambda i, j, k: (k, j)))
    dot = {"nn": _mm, "nt": _mm_nt, "tn": _mm_tn}[mode]
    has_res = res is not None

    def body(*refs):
        a_ref, b_ref = refs[0], refs[1]
        r_ref = refs[2] if has_res else None
        o_ref = refs[3] if has_res else refs[2]

        def finish(out):
            if has_res:
                out = out + r_ref[...].astype(F32)
            o_ref[...] = out.astype(o_ref.dtype)

        if nk == 1:
            finish(dot(a_ref[...], b_ref[...]))
            return
        acc = refs[-1]
        k = pl.program_id(2)

        @pl.when(k == 0)
        def _():
            acc[...] = jnp.zeros_like(acc)

        acc[...] += dot(a_ref[...], b_ref[...])

        @pl.when(k == nk - 1)
        def _():
            finish(acc[...])

    in_specs = [a_spec, b_spec]
    args = [a, b]
    if has_res:
        in_specs.append(pl.BlockSpec((tm, tn), lambda i, j, k: (i, j)))
        args.append(res)
    return pl.pallas_call(
        body, name=name, grid=(M // tm, N // tn, nk), in_specs=in_specs,
        out_specs=pl.BlockSpec((tm, tn), lambda i, j, k: (i, j)),
        out_shape=jax.ShapeDtypeStruct((M, N), out_dtype),
        scratch_shapes=[pltpu.VMEM((tm, tn), F32)] if nk > 1 else [],
        compiler_params=_cp(("parallel", "parallel", "arbitrary")),
    )(*args)


def call_with_comm(body, comm, *, name, grid, in_specs, out_specs, out_shape, scratch_shapes, semantics, args):
    if comm is None:
        outs = pl.pallas_call(body, name=name, grid=grid, in_specs=in_specs, out_specs=out_specs, out_shape=out_shape,
                              scratch_shapes=scratch_shapes, compiler_params=_cp(semantics))(*args)
        return list(outs), []
    n_in, n_out, n_scr = len(in_specs), len(out_specs), len(scratch_shapes)
    c_in, c_out = len(comm["ins"]), len(comm["out_shape"])
    total = int(np.prod(grid))
    mid_step = (2 * total) // 3

    def wrapped(*refs):
        p = 0
        ins = refs[p:p + n_in]
        p += n_in
        cins = refs[p:p + c_in]
        p += c_in
        outs = refs[p:p + n_out]
        p += n_out
        couts = refs[p:p + c_out]
        p += c_out
        scr = refs[p:p + n_scr]
        send, recv = refs[p + n_scr], refs[p + n_scr + 1]
        step = pl.program_id(0)
        for ax in range(1, len(grid)):
            step = step * grid[ax] + pl.program_id(ax)

        @pl.when(step == 0)
        def _():
            comm["start"](cins, couts, send, recv)

        body(*ins, *outs, *scr)
        if comm["mid"] is not None:
            @pl.when(step == mid_step)
            def _():
                comm["mid"](cins, couts, send, recv)

        @pl.when(step == total - 1)
        def _():
            comm["finish"](cins, couts, send, recv)

    outs = pl.pallas_call(
        wrapped, name=name, grid=grid, in_specs=list(in_specs) + [HBM] * c_in,
        out_specs=list(out_specs) + [HBM] * c_out, out_shape=list(out_shape) + list(comm["out_shape"]),
        scratch_shapes=list(scratch_shapes) + [pltpu.SemaphoreType.DMA((comm["n_sems"],))] * 2,
        compiler_params=_cp(("arbitrary",) * len(grid)),
    )(*args, *comm["ins"])
    return list(outs[:n_out]), list(outs[n_out:])


def _col(arr, width, idx):
    return (arr, width, idx)


def _perm(arr, dil, width=None, idx=0):
    return (arr, arr.shape[1] if width is None else width, idx, dil)


def _from_perm(ref, scr, dil):
    n, w = ref.shape[1], ref.shape[2]
    for r in range(dil):
        for j in range(w // LANE):
            scr[j, pl.ds(r, n, stride=dil), :] = ref[r, :, j * LANE:(j + 1) * LANE].astype(F32)
    return jnp.concatenate([scr[j] for j in range(w // LANE)], axis=1)


def _to_perm(val, ref, scr, dil):
    n, w = ref.shape[1], ref.shape[2]
    for j in range(w // LANE):
        scr[j] = val[:, j * LANE:(j + 1) * LANE].astype(F32)
    for r in range(dil):
        ref[r] = jnp.concatenate([scr[j, pl.ds(r, n, stride=dil), :] for j in range(w // LANE)], axis=1).astype(ref.dtype)


def rowcall(name, fn, rows, bcs, row_outs, bc_outs=(), tb=256, halo=()):
    rows = [r if isinstance(r, tuple) else (r, r.shape[1], 0) for r in rows]
    rows = [r if len(r) == 4 else r + (1,) for r in rows]
    row_outs = [o if len(o) == 3 else o + (1,) for o in row_outs]
    S = rows[0][0].shape[0]
    tb = min(tb, S)
    nb = S // tb
    n_r, n_h, n_b, n_ro, n_bo = len(rows), len(halo), len(bcs), len(row_outs), len(bc_outs)
    hb = tb // SUBLANE
    last = S // SUBLANE - 1
    perm_w = max([w for (_, w, _, d) in rows if d > 1] + [w for (w, _, d) in row_outs if d > 1] + [0])

    def body(*refs):
        i = pl.program_id(0)
        scr = refs[-1] if perm_w else None
        pos = 0
        r_in = [r[...] if d == 1 else _from_perm(r, scr, d) for r, (_, _, _, d) in zip(refs[pos:pos + n_r], rows)]
        pos += n_r
        h_in = []
        for _ in range(n_h):
            prev = refs[pos][...] * (i > 0).astype(F32)
            nxt = refs[pos + 1][...] * (i < nb - 1).astype(F32)
            h_in += [prev, nxt]
            pos += 2
        b_in = [r[...] for r in refs[pos:pos + n_b]]
        pos += n_b
        ro = refs[pos:pos + n_ro]
        bo = refs[pos + n_ro:pos + n_ro + n_bo]
        outs_r, outs_b = fn(*r_in, *h_in, *b_in)
        for ref, val, (_, _, d) in zip(ro, outs_r, row_outs, strict=True):
            if d == 1:
                ref[...] = val.astype(ref.dtype)
            else:
                _to_perm(val, ref, scr, d)
        if n_bo:
            @pl.when(i == 0)
            def _():
                for ref in bo:
                    ref[...] = jnp.zeros_like(ref)

            for ref, val in zip(bo, outs_b, strict=True):
                ref[...] += val

    in_specs, args = [], []
    for (a, w, c, d) in rows:
        if d == 1:
            in_specs.append(pl.BlockSpec((tb, w), functools.partial(lambda i, c: (i, c), c=c)))
            args.append(a)
        else:
            in_specs.append(pl.BlockSpec((d, tb // d, w), functools.partial(lambda i, c: (0, i, c), c=c)))
            args.append(a.reshape(d, S // d, a.shape[1]))
    for h in halo:
        a, w, c, _ = rows[h]
        in_specs.append(pl.BlockSpec((SUBLANE, w), functools.partial(
            lambda i, c: (jnp.maximum(i * hb - 1, 0), c), c=c)))
        in_specs.append(pl.BlockSpec((SUBLANE, w), functools.partial(
            lambda i, c: (jnp.minimum((i + 1) * hb, last), c), c=c)))
        args += [a, a]
    for b in bcs:
        in_specs.append(pl.BlockSpec(b.shape, _const_map(b.ndim)))
        args.append(b)
    out_specs, out_shape = [], []
    for (w, dt, d) in row_outs:
        if d == 1:
            out_specs.append(pl.BlockSpec((tb, w), lambda i: (i, 0)))
            out_shape.append(jax.ShapeDtypeStruct((S, w), dt))
        else:
            out_specs.append(pl.BlockSpec((d, tb // d, w), lambda i: (0, i, 0)))
            out_shape.append(jax.ShapeDtypeStruct((d, S // d, w), dt))
    for shp in bc_outs:
        out_specs.append(pl.BlockSpec(shp, _const_map(len(shp))))
        out_shape.append(jax.ShapeDtypeStruct(shp, F32))
    outs = pl.pallas_call(
        body, name=name, grid=(nb,), in_specs=in_specs, out_specs=out_specs, out_shape=out_shape,
        scratch_shapes=[pltpu.VMEM((perm_w // LANE, tb, LANE), F32)] if perm_w else [],
        compiler_params=_cp(("arbitrary",) if n_bo else ("parallel",)),
    )(*args)
    row_res = [o if d == 1 else o.reshape(S, w) for o, (w, _, d) in zip(outs[:n_ro], row_outs)]
    return row_res, list(outs[n_ro:])


def smallcall(name, fn, ins, out_shapes):
    n_in = len(ins)

    def body(*refs):
        outs = fn(*[r[...] for r in refs[:n_in]])
        for ref, val in zip(refs[n_in:], outs, strict=True):
            ref[...] = val.astype(ref.dtype)

    return pl.pallas_call(
        body, name=name, out_shape=[jax.ShapeDtypeStruct(s, F32) for s in out_shapes],
        compiler_params=_cp(),
    )(*ins)


def _rms(x, g):
    return x * lax.rsqrt(jnp.mean(x * x, axis=-1, keepdims=True) + EPS) * g


def _rms_groups(y, g, width):
    outs = []
    for j in range(y.shape[1] // width):
        sl = slice(j * width, (j + 1) * width)
        outs.append(_rms(y[:, sl], g[:, sl]))
    return jnp.concatenate(outs, axis=1)


def norm_fwd(name, x, g, dils=(1,)):
    outs, _ = rowcall(name, lambda x, g: ((_rms(x, g),) * len(dils), ()), [x], [g],
                      [(D_MODEL, BF16, d) for d in dils], tb=512)
    return outs[0] if len(dils) == 1 else tuple(outs)


def norm_bwd(name, x, g, dhn, dres, dils=(1,)):
    parts = dhn if isinstance(dhn, tuple) else (dhn,)
    n = len(parts)

    def fn(x, *rest):
        dh = functools.reduce(lambda a, b: a + b, rest[:n])
        _, vjp = jax.vjp(_rms, x, rest[n + 1])
        dx, dg = vjp(dh)
        return (dx + rest[n],), (dg,)

    rows = [x] + [a if d == 1 else _perm(a, d) for a, d in zip(parts, dils)] + [dres]
    (dx,), (dg,) = rowcall(name, fn, rows, [g], [(D_MODEL, F32)], [(1, D_MODEL)], tb=512)
    return dx, dg


def loss_head(x, tgt, g):
    def fn(x, tgt, g):
        y, vjp = jax.vjp(_rms, x, g)
        diff = y - tgt
        loss = 0.5 * jnp.sum(jnp.mean(diff * diff, axis=-1, keepdims=True), axis=0, keepdims=True)
        dx, dg = vjp(diff * (1.0 / D_MODEL))
        return (dx,), (jnp.broadcast_to(loss, (1, LANE)), dg)

    (dx,), (loss, dg) = rowcall("loss_head", fn, [x, tgt], [g], [(D_MODEL, F32)],
                                [(1, LANE), (1, D_MODEL)], tb=512)
    return loss, dx, dg


def _shift_rows(x, s):
    if s == 0:
        return x
    return pltpu.roll(x, (-s) % x.shape[0], 0)


def _conv_ext(x, prev, nxt, w):
    xe = jnp.concatenate([prev, x, nxt], axis=0)
    pad = SSD_CONV // 2
    c = jnp.zeros_like(xe)
    for k in range(SSD_CONV):
        c = c + w[k:k + 1, :] * _shift_rows(xe, k - pad)
    return xe, c


def ssd_conv_fwd(u, conv_w, conv_b):
    def fn(x0, x1, x2, p0, n0, p1, n1, p2, n2, w, b):
        tb = x0.shape[0]
        outs = []
        for j, (x, p, n) in enumerate(((x0, p0, n0), (x1, p1, n1), (x2, p2, n2))):
            sl = slice(j * 1024, (j + 1) * 1024)
            _, c = _conv_ext(x, p, n, w[:, sl])
            outs.append(_silu(c[SUBLANE:SUBLANE + tb] + b[:, sl]))
        return (jnp.concatenate(outs, axis=1),), ()

    (xbc,), _ = rowcall("ssd_conv_fwd", fn, [_col(u, 1024, 2), _col(u, 1024, 3), _col(u, 1024, 4)],
                        [conv_w, conv_b], [(SSD_CONV_CH, F32)], tb=256, halo=(0, 1, 2))
    return xbc


def ssd_conv_bwd(u, dxbc, dz, conv_w, conv_b):
    pad = SSD_CONV // 2

    def fn(x0, x1, x2, g0, g1, g2, dz, xp0, xn0, xp1, xn1, xp2, xn2, gp0, gn0, gp1, gn1, gp2, gn2, w, b):
        tb = x0.shape[0]
        blk = slice(SUBLANE, SUBLANE + tb)
        dpre, dws, dbs = [], [], []
        xs = ((x0, xp0, xn0), (x1, xp1, xn1), (x2, xp2, xn2))
        gs = ((g0, gp0, gn0), (g1, gp1, gn1), (g2, gp2, gn2))
        for j in range(3):
            sl = slice(j * 1024, (j + 1) * 1024)
            wj = w[:, sl]
            xe, c = _conv_ext(*xs[j], wj)
            ce = c + b[:, sl]
            sig = jax.nn.sigmoid(ce)
            ge = jnp.concatenate([gs[j][1], gs[j][0], gs[j][2]], axis=0)
            dce = ge * (sig * (1.0 + ce * (1.0 - sig)))
            dx = jnp.zeros_like(xe)
            dw_rows = []
            for k in range(SSD_CONV):
                dx = dx + wj[k:k + 1, :] * _shift_rows(dce, pad - k)
                dw_rows.append(jnp.sum(dce[blk] * _shift_rows(xe, k - pad)[blk], axis=0, keepdims=True))
            dw_rows.append(jnp.zeros_like(dw_rows[0]))
            dpre.append(dx[blk])
            dws.append(jnp.concatenate(dw_rows, axis=0))
            dbs.append(jnp.sum(dce[blk], axis=0, keepdims=True))
        du = jnp.concatenate([dz] + dpre, axis=1)
        return (du,), (jnp.concatenate(dws, axis=1), jnp.concatenate(dbs, axis=1))

    rows = [_col(u, 1024, 2), _col(u, 1024, 3), _col(u, 1024, 4),
            _col(dxbc, 1024, 0), _col(dxbc, 1024, 1), _col(dxbc, 1024, 2), dz]
    (du,), (dw, db) = rowcall("ssd_conv_bwd", fn, rows, [conv_w, conv_b], [(SSD_MAIN, BF16)],
                              [(SUBLANE, SSD_CONV_CH), (1, SSD_CONV_CH)], tb=128, halo=(0, 1, 2, 3, 4, 5))
    return du, dw, db


def _expand_heads(v):
    return jnp.concatenate([jnp.broadcast_to(v[:, j:j + 1], (v.shape[0], SSD_HEADDIM)) for j in range(SSD_HPG)], axis=1)


def _ssd_chunk(rev, st_in, xs, udt, dtb, alog, B, C):
    Q = B.shape[0]
    P = SSD_HEADDIM
    dt = _softplus(udt + dtb)
    a = dt * (-jnp.exp(alog))
    r = lax.broadcasted_iota(jnp.int32, (Q, Q), 0)
    c = lax.broadcasted_iota(jnp.int32, (Q, Q), 1)
    mask = (r <= c) if rev else (r >= c)
    p = _mm_tri(mask, a)
    pT = p.T
    p_e = _expand_heads(p)
    tot_e = p_e[0:1] if rev else p_e[Q - 1:Q]
    xdt = xs * _expand_heads(dt)
    CB = _mm_nt(C, B)
    H = SSD_HPG
    p_cols = jnp.concatenate([jnp.broadcast_to(p[:, j:j + 1], (Q, Q)) for j in range(H)], axis=1)
    p_rows = jnp.concatenate([pT[j:j + 1, :] for j in range(H)], axis=1)
    decay = jnp.exp(jnp.where(jnp.concatenate([mask] * H, axis=1), p_cols - p_rows, NEG_BIG))
    col = lax.broadcasted_iota(jnp.int32, (1, H * P), 1)
    x_bd = jnp.concatenate([jnp.where((col >= j * P) & (col < (j + 1) * P), xdt, 0.0) for j in range(H)], axis=0)
    y = _mm(jnp.concatenate([CB] * H, axis=1) * decay, x_bd) + _mm(C, st_in) * jnp.exp(p_e)
    st_out = st_in * jnp.exp(tot_e) + _mm_tn(B, xdt * jnp.exp(tot_e - p_e))
    return y, st_out


def _ssd_specs(nc, rev_order):
    Q = SSD_CHUNK
    N, P, H, GS = SSD_STATE, SSD_HEADDIM, SSD_HPG, SSD_GPS
    gw = H * P
    nbc = SSD_GROUPS // GS

    def cidx(s):
        return nc - 1 - s if rev_order else s

    xs = pl.BlockSpec((Q, GS * gw), lambda g, s: (cidx(s), g))
    Bs = pl.BlockSpec((Q, GS * N), lambda g, s: (cidx(s), SSD_DI // (GS * N) + g))
    Cs = pl.BlockSpec((Q, GS * N), lambda g, s: (cidx(s), SSD_DI // (GS * N) + nbc + g))
    BC_out = pl.BlockSpec((Q, GS * N), lambda g, s: (cidx(s), g))
    udt = pl.BlockSpec((GS, Q, H), lambda g, s: (g, cidx(s), 0))
    small = pl.BlockSpec((GS, 1, H), lambda g, s: (g, 0, 0))
    st = pl.BlockSpec((GS, 1, N, gw), lambda g, s: (g, cidx(s), 0, 0))
    return xs, Bs, Cs, BC_out, udt, small, st


def ssd_scan_fwd(name, xbc, udt, dtb, alog, rev, comm=None):
    S = xbc.shape[0]
    Q, N, P, H, GS = SSD_CHUNK, SSD_STATE, SSD_HEADDIM, SSD_HPG, SSD_GPS
    gw = H * P
    nc = S // Q
    xs_s, B_s, C_s, _, udt_s, small_s, st_s = _ssd_specs(nc, rev)

    def body(xs_ref, B_ref, C_ref, udt_ref, dtb_ref, alog_ref, y_ref, st_ref, state):
        @pl.when(pl.program_id(1) == 0)
        def _():
            state[...] = jnp.zeros_like(state)

        for g in range(GS):
            st_ref[g, 0] = state[g]
            y, st_out = _ssd_chunk(rev, state[g], xs_ref[:, g * gw:(g + 1) * gw], udt_ref[g], dtb_ref[g], alog_ref[g],
                                   B_ref[:, g * N:(g + 1) * N], C_ref[:, g * N:(g + 1) * N])
            y_ref[:, g * gw:(g + 1) * gw] = y
            state[g] = st_out

    (y, st), got = call_with_comm(
        body, comm, name=name, grid=(SSD_GROUPS // GS, nc),
        in_specs=[xs_s, B_s, C_s, udt_s, small_s, small_s],
        out_specs=[xs_s, st_s],
        out_shape=[jax.ShapeDtypeStruct((S, SSD_DI), F32),
                   jax.ShapeDtypeStruct((SSD_GROUPS, nc, N, gw), F32)],
        scratch_shapes=[pltpu.VMEM((GS, N, gw), F32)],
        semantics=("parallel", "arbitrary"), args=(xbc, xbc, xbc, udt, dtb, alog))
    return y, st, got


def ssd_scan_bwd(name, xbc, udt, dtb, alog, states, dy, rev):
    S = xbc.shape[0]
    Q, N, P, H, GS = SSD_CHUNK, SSD_STATE, SSD_HEADDIM, SSD_HPG, SSD_GPS
    gw = H * P
    nc = S // Q
    xs_s, B_s, C_s, BC_out, udt_s, small_s, st_s = _ssd_specs(nc, not rev)

    def body(xs_ref, B_ref, C_ref, udt_ref, dtb_ref, alog_ref, st_ref, dy_ref,
             dx_ref, dB_ref, dC_ref, dudt_ref, ddtb_ref, dalog_ref, dstate):
        @pl.when(pl.program_id(1) == 0)
        def _():
            dstate[...] = jnp.zeros_like(dstate)
            ddtb_ref[...] = jnp.zeros_like(ddtb_ref)
            dalog_ref[...] = jnp.zeros_like(dalog_ref)

        for g in range(GS):
            cols, bc = slice(g * gw, (g + 1) * gw), slice(g * N, (g + 1) * N)
            _, vjp = jax.vjp(functools.partial(_ssd_chunk, rev), st_ref[g, 0], xs_ref[:, cols], udt_ref[g], dtb_ref[g],
                             alog_ref[g], B_ref[:, bc], C_ref[:, bc])
            dst_in, dxs, dudt, ddtb, dalog, dB, dC = vjp((dy_ref[:, cols], dstate[g]))
            dx_ref[:, cols] = dxs
            dB_ref[:, bc] = dB
            dC_ref[:, bc] = dC
            dudt_ref[g] = dudt
            ddtb_ref[g] += ddtb
            dalog_ref[g] += dalog
            dstate[g] = dst_in

    return pl.pallas_call(
        body, name=name, grid=(SSD_GROUPS // GS, nc),
        in_specs=[xs_s, B_s, C_s, udt_s, small_s, small_s, st_s, xs_s],
        out_specs=[xs_s, BC_out, BC_out, udt_s, small_s, small_s],
        out_shape=[jax.ShapeDtypeStruct((S, SSD_DI), F32),
                   jax.ShapeDtypeStruct((S, SSD_GROUPS * N), F32),
                   jax.ShapeDtypeStruct((S, SSD_GROUPS * N), F32),
                   jax.ShapeDtypeStruct((SSD_GROUPS, S, H), F32),
                   jax.ShapeDtypeStruct((SSD_GROUPS, 1, H), F32),
                   jax.ShapeDtypeStruct((SSD_GROUPS, 1, H), F32)],
        scratch_shapes=[pltpu.VMEM((GS, N, gw), F32)],
        compiler_params=_cp(("parallel", "arbitrary")),
    )(xbc, xbc, xbc, udt, dtb, alog, states, dy)


def _ssd_combine(yf, yb, xs, z, dexp, ng):
    y = (yf + yb + xs * dexp) * _silu(z)
    return _rms_groups(y, ng, SSD_DI // SSD_GROUPS)


def ssd_forward(x, hn, w, comm=None):
    comm = comm or {}
    S = x.shape[0]
    u = matmul("ssd_in", hn, w["ssd_w_main"])
    udt = matmul("ssd_in_dt", hn, w["ssd_w_dt"])
    xbc = ssd_conv_fwd(u, w["ssd_conv_w8"], w["ssd_conv_b"])
    udt_t = udt.reshape(S, 2, SSD_GROUPS, SSD_HPG).transpose(1, 2, 0, 3)
    dtb = w["ssd_dt_bias"].reshape(2, SSD_GROUPS, 1, SSD_HPG)
    alog = w["ssd_a_log"].reshape(2, SSD_GROUPS, 1, SSD_HPG)
    yf, stf, got_f = ssd_scan_fwd("ssd_scan_f", xbc, udt_t[0], dtb[0], alog[0], False, comm.get("ssd_scan_f"))
    yb, stb, got_b = ssd_scan_fwd("ssd_scan_b", xbc, udt_t[1], dtb[1], alog[1], True, comm.get("ssd_scan_b"))
    dexp = jnp.repeat(w["ssd_d"].reshape(1, SSD_HEADS), SSD_HEADDIM, axis=1)
    (yn,), _ = rowcall("ssd_combine", lambda yf, yb, xs, z, d, g: ((_ssd_combine(yf, yb, xs, z, d, g),), ()),
                       [yf, yb, _col(xbc, SSD_DI, 0), _col(u, SSD_DI, 0)], [dexp, w["ssd_norm_g"]],
                       [(SSD_DI, BF16)], tb=256)
    out = matmul("ssd_out", yn, w["ssd_w_out"], res=x)
    saved = dict(hn=hn, u=u, xbc=xbc, udt_t=udt_t, dtb=dtb, alog=alog, yf=yf, yb=yb, stf=stf, stb=stb,
                 dexp=dexp, yn=yn, got=dict(ssd_scan_f=got_f, ssd_scan_b=got_b))
    return out, saved


def ssd_backward(dy, sv, w, comm=None):
    S = dy.shape[0]
    u, xbc = sv["u"], sv["xbc"]
    dyn = matmul("ssd_out_dx", dy, w["ssd_w_out"], mode="nt")
    g_w_out = matmul("ssd_out_dw", sv["yn"], dy, mode="tn")

    def comb_bwd(yf, yb, xs, z, dyn, dexp, ng):
        _, vjp = jax.vjp(_ssd_combine, yf, yb, xs, z, dexp, ng)
        dyf, _, dxs, dz, ddexp, dng = vjp(dyn)
        return (dyf, dxs, dz), (ddexp, dng)

    (dyc, dskip, dz), (ddexp, g_norm) = rowcall(
        "ssd_combine_bwd", comb_bwd, [sv["yf"], sv["yb"], _col(xbc, SSD_DI, 0), _col(u, SSD_DI, 0), dyn],
        [sv["dexp"], w["ssd_norm_g"]], [(SSD_DI, F32)] * 3, [(1, SSD_DI), (1, SSD_DI)], tb=256)
    udt_t, dtb, alog = sv["udt_t"], sv["dtb"], sv["alog"]
    dxf, dBf, dCf, dudt_f, ddtb_f, dalog_f = ssd_scan_bwd("ssd_scan_f_bwd", xbc, udt_t[0], dtb[0], alog[0],
                                                          sv["stf"], dyc, False)
    dxb, dBb, dCb, dudt_b, ddtb_b, dalog_b = ssd_scan_bwd("ssd_scan_b_bwd", xbc, udt_t[1], dtb[1], alog[1],
                                                          sv["stb"], dyc, True)

    def gather(dxf, dxb, dskip, dBf, dBb, dCf, dCb):
        return (jnp.concatenate([dxf + dxb + dskip, dBf + dBb, dCf + dCb], axis=1),), ()

    (dxbc,), _ = rowcall("ssd_dxbc", gather, [dxf, dxb, dskip, dBf, dBb, dCf, dCb], [], [(SSD_CONV_CH, F32)], tb=256)
    du, g_conv_w8, g_conv_b = ssd_conv_bwd(u, dxbc, dz, w["ssd_conv_w8"], w["ssd_conv_b"])
    dudt = jnp.stack([dudt_f, dudt_b]).transpose(2, 0, 1, 3).reshape(S, 2 * SSD_HEADS)
    hn = sv["hn"]
    g_main = matmul("ssd_in_dw", hn, du, mode="tn")
    g_dt = matmul("ssd_in_dt_dw", hn, dudt, mode="tn")
    dhn = matmul("ssd_in_dt_dx", dudt, w["ssd_w_dt"], mode="nt")
    dhn = matmul("ssd_in_dx", du, w["ssd_w_main"], mode="nt", res=dhn)
    grads = dict(
        ssd_w_in=jnp.concatenate([g_main, g_dt], axis=1)[None],
        ssd_conv_w=g_conv_w8[None, :SSD_CONV],
        ssd_conv_b=g_conv_b,
        ssd_dt_bias=jnp.stack([ddtb_f, ddtb_b]).reshape(1, 2, SSD_HEADS),
        ssd_a_log=jnp.stack([dalog_f, dalog_b]).reshape(1, 2, SSD_HEADS),
        ssd_d_exp=ddexp,
        ssd_norm_g=g_norm,
        ssd_w_out=g_w_out[None],
    )
    return dhn, grads


def _hg_block(rev, stTs, uq, uf, ui, lb):
    C = HG_CHUNK
    n = uq.shape[0] // C
    nh = uq.shape[1] // HG_EXPAND
    stTs = list(stTs)
    q = _silu(uq)
    f = lb + (1.0 - lb) * jax.nn.sigmoid(uf)
    k = 1.0 - f
    g = jnp.log(f)
    r = lax.broadcasted_iota(jnp.int32, (C, C), 0)
    c = lax.broadcasted_iota(jnp.int32, (C, C), 1)
    mask = (r <= c) if rev else (r >= c)
    Tm = mask.astype(F32)
    outs = [[None] * n for _ in range(nh)]
    for i in (reversed(range(n)) if rev else range(n)):
        sl = slice(i * C, (i + 1) * C)
        qi, ki, vi = q[sl], k[sl], ui[sl]
        G = _mm_tri(mask, g[sl])
        Gr = G[C // 2:C // 2 + 1]
        Gl = G[0:1] if rev else G[C - 1:C]
        q_in, k_in = qi * jnp.exp(G - Gr), ki * jnp.exp(Gr - G)
        q_st, k_st, e_l = qi * jnp.exp(G), ki * jnp.exp(Gl - G), jnp.exp(Gl)
        for h in range(nh):
            cs = slice(h * HG_EXPAND, (h + 1) * HG_EXPAND)
            att = jnp.where(mask, _mm_nt(q_in[:, cs], k_in[:, cs]), 0.0)
            outs[h][i] = _mm(att, vi[:, cs]) + _mm_nt(q_st[:, cs], stTs[h])
            stTs[h] = stTs[h] * e_l[:, cs] + _mm_tn(vi[:, cs], k_st[:, cs])
    o = jnp.concatenate([jnp.concatenate(outs[h], axis=0) for h in range(nh)], axis=1)
    return o, stTs


def _hg_specs(nb, rev_order, f_col):
    R = HG_ROWS
    gw = HG_HPS * HG_EXPAND
    ng = HG_HEADS // HG_HPS

    def bidx(s):
        return nb - 1 - s if rev_order else s

    def col(base):
        return pl.BlockSpec((R, gw), lambda h, s: (bidx(s), base * ng + h))

    out = pl.BlockSpec((R, gw), lambda h, s: (bidx(s), h))
    lb = pl.BlockSpec((1, gw), lambda h, s: (0, h))
    st = pl.BlockSpec((1, 1, HG_HPS, HG_EXPAND, HG_EXPAND), lambda h, s: (h, bidx(s), 0, 0, 0))
    return col(0), col(f_col), col(3), out, lb, st


def hg_scan_fwd(name, u, lb, rev, comm=None):
    S = u.shape[0]
    nb = S // HG_ROWS
    ng = HG_HEADS // HG_HPS
    q_s, f_s, i_s, o_s, lb_s, st_s = _hg_specs(nb, rev, 2 if rev else 1)

    def body(uq, uf, ui, lb_ref, o_ref, st_ref, state):
        @pl.when(pl.program_id(1) == 0)
        def _():
            state[...] = jnp.zeros_like(state)

        st_ref[0, 0] = state[...]
        o, st = _hg_block(rev, [state[h] for h in range(HG_HPS)], uq[...], uf[...], ui[...], lb_ref[...])
        o_ref[...] = o
        for h in range(HG_HPS):
            state[h] = st[h]

    (o, st), got = call_with_comm(
        body, comm, name=name, grid=(ng, nb), in_specs=[q_s, f_s, i_s, lb_s], out_specs=[o_s, st_s],
        out_shape=[jax.ShapeDtypeStruct((S, HG_W), F32),
                   jax.ShapeDtypeStruct((ng, nb, HG_HPS, HG_EXPAND, HG_EXPAND), F32)],
        scratch_shapes=[pltpu.VMEM((HG_HPS, HG_EXPAND, HG_EXPAND), F32)],
        semantics=("parallel", "arbitrary"), args=(u, u, u, lb))
    return o, st, got


def hg_scan_bwd(name, u, lb, states, do, rev, comm=None):
    S = u.shape[0]
    nb = S // HG_ROWS
    ng = HG_HEADS // HG_HPS
    q_s, f_s, i_s, o_s, lb_s, st_s = _hg_specs(nb, not rev, 2 if rev else 1)

    def body(uq, uf, ui, lb_ref, st_ref, do_ref, dq_ref, df_ref, di_ref, dlb_ref, dstate):
        @pl.when(pl.program_id(1) == 0)
        def _():
            dstate[...] = jnp.zeros_like(dstate)
            dlb_ref[...] = jnp.zeros_like(dlb_ref)

        _, vjp = jax.vjp(functools.partial(_hg_block, rev), [st_ref[0, 0, h] for h in range(HG_HPS)],
                         uq[...], uf[...], ui[...], lb_ref[...])
        dst, dq, df, di, dlb = vjp((do_ref[...], [dstate[h] for h in range(HG_HPS)]))
        dq_ref[...] = dq
        df_ref[...] = df
        di_ref[...] = di
        dlb_ref[...] += dlb
        for h in range(HG_HPS):
            dstate[h] = dst[h]

    outs, got = call_with_comm(
        body, comm, name=name, grid=(ng, nb), in_specs=[q_s, f_s, i_s, lb_s, st_s, o_s],
        out_specs=[o_s, o_s, o_s, lb_s],
        out_shape=[jax.ShapeDtypeStruct((S, HG_W), F32)] * 3 + [jax.ShapeDtypeStruct((1, HG_W), F32)],
        scratch_shapes=[pltpu.VMEM((HG_HPS, HG_EXPAND, HG_EXPAND), F32)],
        semantics=("parallel", "arbitrary"), args=(u, u, u, lb, states, do))
    return (*outs, got)


def _hg_lb(hgrn_lb, layer):
    m = jnp.max(hgrn_lb, axis=0, keepdims=True)
    e = jnp.exp(hgrn_lb - m)
    sm = e / jnp.sum(e, axis=0, keepdims=True)
    lb = jnp.zeros_like(sm[0:1])
    for i in range(1, layer + 1):
        lb = lb + sm[i:i + 1]
    return lb


def _hg_combine(of, ob, gate, ng):
    return _rms_groups(of + ob, ng, HG_EXPAND) * _silu(gate)


def hg_forward(x, hn, w, layer, comm=None):
    comm = comm or {}
    u = matmul("hg_in", hn, w["hg_w_in"])
    (lb,) = smallcall("hg_lb", lambda t: (_hg_lb(t, layer),), [w["hgrn_lb"]], [(1, HG_W)])
    of, stf, got_f = hg_scan_fwd("hg_scan_f", u, lb, False, comm.get("hg_scan_f"))
    ob, stb, got_b = hg_scan_fwd("hg_scan_b", u, lb, True, comm.get("hg_scan_b"))
    (og,), _ = rowcall("hg_combine", lambda of, ob, gate, ng: ((_hg_combine(of, ob, gate, ng),), ()),
                       [of, ob, _col(u, HG_W, 4)], [w["hg_norm_g"]], [(HG_W, BF16)], tb=256)
    out = matmul("hg_out", og, w["hg_w_out"], res=x)
    return out, dict(hn=hn, u=u, lb=lb, of=of, ob=ob, stf=stf, stb=stb, og=og, got=dict(hg_scan_f=got_f, hg_scan_b=got_b))


def hg_backward(dy, sv, w, layer, comm=None):
    comm = comm or {}
    u, lb = sv["u"], sv["lb"]
    dog = matmul("hg_out_dx", dy, w["hg_w_out"], mode="nt")
    g_w_out = matmul("hg_out_dw", sv["og"], dy, mode="tn")

    def comb_bwd(of, ob, gate, dog, ng):
        _, vjp = jax.vjp(_hg_combine, of, ob, gate, ng)
        dof, _, dgate, dng = vjp(dog)
        return (dof, dgate), (dng,)

    (do, dgate), (g_norm,) = rowcall("hg_combine_bwd", comb_bwd, [sv["of"], sv["ob"], _col(u, HG_W, 4), dog],
                                     [w["hg_norm_g"]], [(HG_W, F32)] * 2, [(1, HG_W)], tb=256)
    dqf, dff, dif, dlbf, got_f = hg_scan_bwd("hg_scan_f_bwd", u, lb, sv["stf"], do, False, comm.get("hg_scan_f_bwd"))
    dqb, dfb, dib, dlbb, _ = hg_scan_bwd("hg_scan_b_bwd", u, lb, sv["stb"], do, True)

    def gather(dqf, dqb, dff, dfb, dif, dib, dgate):
        return (jnp.concatenate([dqf + dqb, dff, dfb, dif + dib, dgate], axis=1),), ()

    (du,), _ = rowcall("hg_du", gather, [dqf, dqb, dff, dfb, dif, dib, dgate], [], [(HG_IN, BF16)], tb=256)

    def lb_bwd(t, dlbf, dlbb):
        _, vjp = jax.vjp(lambda t: _hg_lb(t, layer), t)
        return vjp(dlbf + dlbb)

    (g_lb,) = smallcall("hg_lb_bwd", lb_bwd, [w["hgrn_lb"], dlbf, dlbb], [(DEPTH, HG_W)])
    hn = sv["hn"]
    g_w_in = matmul("hg_in_dw", hn, du, mode="tn")
    dhn = matmul("hg_in_dx", du, w["hg_w_in"], mode="nt")
    return dhn, dict(hg_w_in=g_w_in[None], hg_norm_g=g_norm, hg_w_out=g_w_out[None], hgrn_lb=g_lb,
                     got=dict(hg_scan_f_bwd=got_f))


def _rope_tables(S):
    t = np.arange(S)
    row = (t // GRID_W).astype(np.float32)
    col = (t % GRID_W).astype(np.float32)
    inv = (ROPE_THETA ** (-np.arange(0, ROPE_AXIS, 2, dtype=np.float32) / ROPE_AXIS)).astype(np.float32)
    ar = jnp.asarray(row)[:, None] * jnp.asarray(inv)[None, :]
    ac = jnp.asarray(col)[:, None] * jnp.asarray(inv)[None, :]
    cos = jnp.concatenate([jnp.cos(ar), jnp.cos(ar), jnp.cos(ac), jnp.cos(ac)], axis=1)
    sin = jnp.concatenate([-jnp.sin(ar), jnp.sin(ar), -jnp.sin(ac), jnp.sin(ac)], axis=1)
    return cos.astype(F32), sin.astype(F32)


@jax.custom_vjp
def _swap_halves_of_axes(x):
    h = ROPE_AXIS // 2
    lane = lax.broadcasted_iota(jnp.int32, x.shape, 1)
    return jnp.where((lane & h) == 0, pltpu.roll(x, AT_HD - h, 1), pltpu.roll(x, h, 1))


_swap_halves_of_axes.defvjp(lambda x: (_swap_halves_of_axes(x), None), lambda _, g: (_swap_halves_of_axes(g),))


def _rope(x, cos, sin):
    return x * cos + _swap_halves_of_axes(x) * sin


def _at_pre(uq, uk, cos, sin, qg, kg):
    qs, ks = [], []
    for h in range(AT_HEADS):
        qs.append(_rope(_rms(uq[:, h * AT_HD:(h + 1) * AT_HD], qg), cos, sin) * (AT_HD ** -0.5))
    for h in range(AT_KV):
        ks.append(_rope(_rms(uk[:, h * AT_HD:(h + 1) * AT_HD], kg), cos, sin))
    return jnp.concatenate(qs, axis=1), jnp.concatenate(ks, axis=1)


def _stack_heads(x):
    return jnp.concatenate([x[:, :AT_HD], x[:, AT_HD:]], axis=0)


def _unstack_heads(x):
    t = x.shape[0] // 2
    return jnp.concatenate([x[:t], x[t:]], axis=1)


def at_flash_fwd(q, k, u):
    S = q.shape[0]
    tq, tk = _pick(S, 512), _pick(S, 4096)
    nq, nk = S // tq, S // tk
    gw = AT_GRP * AT_HD

    def body(q_ref, k_ref, v_ref, o_ref, lse_ref, m_s, l_s, acc):
        j = pl.program_id(2)

        @pl.when(j == 0)
        def _():
            m_s[...] = jnp.full_like(m_s, NEG_BIG)
            l_s[...] = jnp.zeros_like(l_s)
            acc[...] = jnp.zeros_like(acc)

        s = _mm_nt(_stack_heads(q_ref[...]), k_ref[...])
        m_new = jnp.maximum(m_s[...], jnp.max(s, axis=-1, keepdims=True))
        alpha = jnp.exp(m_s[...] - m_new)
        p = jnp.exp(s - m_new)
        l_s[...] = alpha * l_s[...] + jnp.sum(p, axis=-1, keepdims=True)
        acc[...] = alpha * acc[...] + _mm(p, v_ref[...])
        m_s[...] = m_new

        @pl.when(j == nk - 1)
        def _():
            o_ref[...] = _unstack_heads(acc[...] / l_s[...])
            lse = m_s[...] + jnp.log(l_s[...])
            lse_ref[0, 0] = lse[:tq]
            lse_ref[0, 1] = lse[tq:]

    return pl.pallas_call(
        body, name="at_flash_fwd", grid=(AT_KV, nq, nk),
        in_specs=[pl.BlockSpec((tq, gw), lambda h, i, j: (i, h)),
                  pl.BlockSpec((tk, AT_HD), lambda h, i, j: (j, h)),
                  pl.BlockSpec((tk, AT_HD), lambda h, i, j: (j, (AT_QW + AT_KW) // AT_HD + h))],
        out_specs=[pl.BlockSpec((tq, gw), lambda h, i, j: (i, h)),
                   pl.BlockSpec((1, AT_GRP, tq, 1), lambda h, i, j: (h, 0, i, 0))],
        out_shape=[jax.ShapeDtypeStruct((S, AT_QW), F32), jax.ShapeDtypeStruct((AT_KV, AT_GRP, S, 1), F32)],
        scratch_shapes=[pltpu.VMEM((2 * tq, 1), F32), pltpu.VMEM((2 * tq, 1), F32), pltpu.VMEM((2 * tq, AT_HD), F32)],
        compiler_params=_cp(("parallel", "parallel", "arbitrary")),
    )(q, k, u)


def at_flash_bwd(q, k, u, o, lse, do):
    S = q.shape[0]
    tq, tk = _pick(S, 128), _pick(S, 4096)
    nq, nk = S // tq, S // tk
    gw = AT_GRP * AT_HD

    def body(q_ref, k_ref, v_ref, o_ref, lse_ref, do_ref, dq_ref, dk_ref, dv_ref, dk_acc, dv_acc):
        j, i = pl.program_id(1), pl.program_id(2)

        @pl.when(i == 0)
        def _():
            dk_acc[...] = jnp.zeros_like(dk_acc)
            dv_acc[...] = jnp.zeros_like(dv_acc)

        q2 = _stack_heads(q_ref[...])
        do_blk = do_ref[...]
        do2 = _stack_heads(do_blk)
        delta = _stack_heads(do_blk * o_ref[...])
        delta = jnp.sum(delta, axis=-1, keepdims=True)
        kb, vb = k_ref[...], v_ref[...]
        p = jnp.exp(_mm_nt(q2, kb) - jnp.concatenate([lse_ref[0, 0], lse_ref[0, 1]], axis=0))
        dv_acc[...] += _mm_tn(p, do2)
        ds = p * (_mm_nt(do2, vb) - delta)
        dk_acc[...] += _mm_tn(ds, q2)
        dq = _unstack_heads(_mm(ds, kb))
        rows = pl.ds(pl.multiple_of(i * tq, tq), tq)

        @pl.when(j == 0)
        def _():
            dq_ref[rows, :] = dq

        @pl.when(j > 0)
        def _():
            dq_ref[rows, :] += dq

        @pl.when(i == nq - 1)
        def _():
            dk_ref[...] = dk_acc[...]
            dv_ref[...] = dv_acc[...]

    return pl.pallas_call(
        body, name="at_flash_bwd", grid=(AT_KV, nk, nq),
        in_specs=[pl.BlockSpec((tq, gw), lambda h, j, i: (i, h)),
                  pl.BlockSpec((tk, AT_HD), lambda h, j, i: (j, h)),
                  pl.BlockSpec((tk, AT_HD), lambda h, j, i: (j, (AT_QW + AT_KW) // AT_HD + h)),
                  pl.BlockSpec((tq, gw), lambda h, j, i: (i, h)),
                  pl.BlockSpec((1, AT_GRP, tq, 1), lambda h, j, i: (h, 0, i, 0)),
                  pl.BlockSpec((tq, gw), lambda h, j, i: (i, h))],
        out_specs=[pl.BlockSpec((S, gw), lambda h, j, i: (0, h)),
                   pl.BlockSpec((tk, AT_HD), lambda h, j, i: (j, h)),
                   pl.BlockSpec((tk, AT_HD), lambda h, j, i: (j, h))],
        out_shape=[jax.ShapeDtypeStruct((S, AT_QW), F32), jax.ShapeDtypeStruct((S, AT_KW), F32),
                   jax.ShapeDtypeStruct((S, AT_KW), F32)],
        scratch_shapes=[pltpu.VMEM((tk, AT_HD), F32), pltpu.VMEM((tk, AT_HD), F32)],
        compiler_params=_cp(("parallel", "arbitrary", "arbitrary")),
    )(q, k, u, o, lse, do)


def at_forward(x, hn, w, comm=None):
    S = x.shape[0]
    u = matmul("at_in", hn, w["at_w_in"])
    cos, sin = _rope_tables(S)
    (q, k), _ = rowcall("at_pre", lambda uq, uk, c, s, qg, kg: (_at_pre(uq, uk, c, s, qg, kg), ()),
                        [_col(u, AT_QW, 0), _col(u, AT_KW, 2), cos, sin], [w["at_q_norm_g"], w["at_k_norm_g"]],
                        [(AT_QW, BF16), (AT_KW, BF16)], tb=256)
    o, lse = at_flash_fwd(q, k, u)
    (og,), _ = rowcall("at_gate", lambda o, gate: ((o * _silu(gate),), ()), [o, _col(u, AT_QW, 2)], [],
                       [(AT_QW, BF16)], tb=256)
    out = matmul("at_out", og, w["at_w_out"], res=x)
    return out, dict(hn=hn, u=u, cos=cos, sin=sin, q=q, k=k, o=o, lse=lse, og=og)


def at_backward(dy, sv, w, comm=None):
    u = sv["u"]
    dog = matmul("at_out_dx", dy, w["at_w_out"], mode="nt")
    g_w_out = matmul("at_out_dw", sv["og"], dy, mode="tn")

    def gate_bwd(o, gate, dog):
        _, vjp = jax.vjp(lambda o, gate: o * _silu(gate), o, gate)
        return vjp(dog), ()

    (do, dgate), _ = rowcall("at_gate_bwd", gate_bwd, [sv["o"], _col(u, AT_QW, 2), dog], [],
                             [(AT_QW, F32)] * 2, tb=256)
    dq, dk, dv = at_flash_bwd(sv["q"], sv["k"], u, sv["o"], sv["lse"], do)

    def pre_bwd(uq, uk, cos, sin, dq, dk, dv, dgate, qg, kg):
        _, vjp = jax.vjp(lambda uq, uk, qg, kg: _at_pre(uq, uk, cos, sin, qg, kg), uq, uk, qg, kg)
        duq, duk, dqg, dkg = vjp((dq, dk))
        return (jnp.concatenate([duq, duk, dv, dgate], axis=1),), (dqg, dkg)

    (du,), (g_qg, g_kg) = rowcall(
        "at_pre_bwd", pre_bwd, [_col(u, AT_QW, 0), _col(u, AT_KW, 2), sv["cos"], sv["sin"], dq, dk, dv, dgate],
        [w["at_q_norm_g"], w["at_k_norm_g"]], [(AT_IN, BF16)], [(1, AT_HD), (1, AT_HD)], tb=128)
    hn = sv["hn"]
    g_w_in = matmul("at_in_dw", hn, du, mode="tn")
    dhn = matmul("at_in_dx", du, w["at_w_in"], mode="nt")
    return dhn, dict(at_w_in=g_w_in[None], at_q_norm_g=g_qg, at_k_norm_g=g_kg, at_w_out=g_w_out[None])


def _t5_bucket_np(rel):
    half = REL_BUCKETS // 2
    exact = half // 2
    n = np.abs(rel)
    large = exact + (np.log(np.maximum(n, 1).astype(np.float32) / exact)
                     / math.log(REL_MAX_DIST / exact) * (half - exact)).astype(np.int32)
    large = np.minimum(large, half - 1)
    return np.where(rel > 0, half, 0) + np.where(n < exact, n, large)


def _dl_tq(S, dil):
    return min(128, S // dil)


def _dl_bias_maps(tq, dil):
    W = tq + 2 * DL_STEPS
    i = np.arange(tq)[:, None]
    wdx = np.arange(W)[None, :]
    dm = wdx - DL_STEPS - i
    bucket = _t5_bucket_np(dm * dil).reshape(-1).astype(np.int32)
    band = np.where(np.abs(dm) <= DL_STEPS, 0.0, NEG_BIG).reshape(1, -1).astype(np.float32)
    onehot = (jnp.asarray(bucket)[None, :] == jnp.arange(REL_BUCKETS, dtype=jnp.int32)[:, None]).astype(F32)
    return onehot, jnp.asarray(band)


def _dl_attend(q, kwin, vwin, T, valid):
    tq = q.shape[0]
    os, ls = [], []
    for h in range(DL_HEADS):
        sl = slice(h * DL_HD, (h + 1) * DL_HD)
        s = _mm_nt(q[:, sl] * (DL_HD ** -0.5), kwin[:, sl]) + T[h]
        s = jnp.where(valid, s, NEG_BIG)
        m = lax.stop_gradient(jnp.max(s, axis=-1, keepdims=True))
        e = jnp.exp(s - m)
        den = jnp.sum(e, axis=-1, keepdims=True)
        lse = m + jnp.log(den)
        p = e * (1.0 / den)
        os.append(_mm(p, vwin[:, sl]))
        ls.append(jnp.broadcast_to(lse, (tq, DL_HD)))
    return jnp.concatenate(os, axis=1), jnp.concatenate(ls, axis=1)


def _dl_specs(tq, Ls):
    nb = Ls // tq
    hs = DL_STEPS
    per = tq // hs
    nh = Ls // hs

    def main(c):
        return pl.BlockSpec((tq, DL_W), lambda r, i: (r * nb + i, c))

    def prev(c):
        return pl.BlockSpec((hs, DL_W), lambda r, i: (r * nh + jnp.maximum(i * per - 1, 0), c))

    def nxt(c):
        return pl.BlockSpec((hs, DL_W), lambda r, i: (r * nh + jnp.minimum((i + 1) * per, nh - 1), c))

    return nb, main, prev, nxt


def _dl_valid(i, tq, Ls):
    W = tq + 2 * DL_STEPS
    mk = i * tq - DL_STEPS + lax.broadcasted_iota(jnp.int32, (1, W), 1)
    return (mk >= 0) & (mk < Ls)


def dl_attn_fwd(gi, dil, u, T):
    S = u.shape[0]
    Ls = S // dil
    tq = _dl_tq(S, dil)
    nb, main, prev, nxt = _dl_specs(tq, Ls)
    out = main(0)

    def body(q_ref, kp, kc, kn, vp, vc, vn, T_ref, o_ref, l_ref):
        kwin = jnp.concatenate([kp[...], kc[...], kn[...]], axis=0)
        vwin = jnp.concatenate([vp[...], vc[...], vn[...]], axis=0)
        o, l = _dl_attend(q_ref[...], kwin, vwin, T_ref[...], _dl_valid(pl.program_id(1), tq, Ls))
        o_ref[...] = o
        l_ref[...] = l

    o, l = pl.pallas_call(
        body, name=f"dl_attn_fwd{gi}", grid=(dil, nb),
        in_specs=[main(0), prev(1), main(1), nxt(1), prev(2), main(2), nxt(2),
                  pl.BlockSpec(T.shape, _const_map(3))],
        out_specs=[out, out],
        out_shape=[jax.ShapeDtypeStruct((S, DL_W), F32)] * 2,
        compiler_params=_cp(("parallel", "parallel")),
    )(u, u, u, u, u, u, u, T)
    return o, l


def dl_attn_bwd(gi, dil, u, T, do, dl, dgate=None):
    S = u.shape[0]
    Ls = S // dil
    tq = _dl_tq(S, dil)
    hs = DL_STEPS
    W = tq + 2 * hs
    nb, main, prev, nxt = _dl_specs(tq, Ls)
    out = main(0)
    win = pl.BlockSpec((1, W, DL_W), lambda r, i: (r * nb + i, 0, 0))

    def body(q_ref, kp, kc, kn, vp, vc, vn, T_ref, do_ref, dl_ref, dq_ref, dkw_ref, dvw_ref, dT_ref):
        first = (pl.program_id(0) == 0) & (pl.program_id(1) == 0)

        @pl.when(first)
        def _():
            dT_ref[...] = jnp.zeros_like(dT_ref)

        kwin = jnp.concatenate([kp[...], kc[...], kn[...]], axis=0)
        vwin = jnp.concatenate([vp[...], vc[...], vn[...]], axis=0)
        valid = _dl_valid(pl.program_id(1), tq, Ls)
        _, vjp = jax.vjp(lambda q, k, v, T: _dl_attend(q, k, v, T, valid), q_ref[...].astype(F32), kwin.astype(F32),
                         vwin.astype(F32), T_ref[...])
        dq, dkw, dvw, dT = vjp((do_ref[...], dl_ref[...]))
        dq_ref[...] = dq
        dkw_ref[0] = dkw
        dvw_ref[0] = dvw
        dT_ref[...] += dT

    dq, dkw, dvw, dT = pl.pallas_call(
        body, name=f"dl_attn_bwd{gi}", grid=(dil, nb),
        in_specs=[main(0), prev(1), main(1), nxt(1), prev(2), main(2), nxt(2),
                  pl.BlockSpec(T.shape, _const_map(3)), out, out],
        out_specs=[out, win, win, pl.BlockSpec(T.shape, _const_map(3))],
        out_shape=[jax.ShapeDtypeStruct((S, DL_W), F32),
                   jax.ShapeDtypeStruct((dil * nb, W, DL_W), F32),
                   jax.ShapeDtypeStruct((dil * nb, W, DL_W), F32),
                   jax.ShapeDtypeStruct(T.shape, F32)],
        compiler_params=_cp(("arbitrary", "arbitrary")),
    )(u, u, u, u, u, u, u, T, do, dl)

    per = tq // hs
    n_out = 3 if dgate is None else 4

    def fold(*refs):
        dq_ref, kc, kp, kn, vc, vp, vn = refs[:7]
        du_ref = refs[-1]
        i = pl.program_id(1)
        has_p = (i > 0).astype(F32)
        has_n = (i < nb - 1).astype(F32)
        du_ref[:, 0:DL_W] = dq_ref[...].astype(BF16)
        for c, (c_ref, p_ref, n_ref) in enumerate(((kc, kp, kn), (vc, vp, vn)), start=1):
            mid = c_ref[0, hs:hs + tq, :]
            top = mid[0:hs] + p_ref[0] * has_p
            bot = mid[tq - hs:tq] + n_ref[0] * has_n
            parts = [top, bot] if tq == 2 * hs else ([top, mid[hs:tq - hs], bot] if tq > 2 * hs else [top + n_ref[0] * has_n])
            du_ref[:, c * DL_W:(c + 1) * DL_W] = jnp.concatenate(parts, axis=0).astype(BF16)
        if dgate is not None:
            du_ref[:, 3 * DL_W:4 * DL_W] = refs[7][...].astype(BF16)

    wfull = pl.BlockSpec((1, W, DL_W), lambda r, i: (r * nb + i, 0, 0))
    wprev = pl.BlockSpec((1, hs, DL_W), lambda r, i: (r * nb + jnp.maximum(i - 1, 0), per + 1, 0))
    wnext = pl.BlockSpec((1, hs, DL_W), lambda r, i: (r * nb + jnp.minimum(i + 1, nb - 1), 0, 0))
    extra_specs, extra_args = ([], []) if dgate is None else ([out], [dgate])
    du = pl.pallas_call(
        fold, name=f"dl_fold{gi}", grid=(dil, nb),
        in_specs=[out, wfull, wprev, wnext, wfull, wprev, wnext] + extra_specs,
        out_specs=pl.BlockSpec((tq, n_out * DL_W), lambda r, i: (r * nb + i, 0)),
        out_shape=jax.ShapeDtypeStruct((S, n_out * DL_W), BF16),
        compiler_params=_cp(("parallel", "parallel")),
    )(dq, dkw, dkw, dkw, dvw, dvw, dvw, *extra_args)
    return du, dT


def _dl_merge(o0, o1, o2, l0, l1, l2, gate):
    m = jnp.maximum(jnp.maximum(l0, l1), l2)
    e0, e1, e2 = jnp.exp(l0 - m), jnp.exp(l1 - m), jnp.exp(l2 - m)
    den = e0 + e1 + e2
    return ((e0 * o0 + e1 * o1 + e2 * o2) / den) * _silu(gate)


DL_DILS = tuple(d for _, d in DL_PAIRS)


def _dl_group_weights(w_in):
    g3 = 3 * DL_W
    return [jnp.concatenate([w_in[:, :g3], w_in[:, 3 * g3:]], axis=1), w_in[:, g3:2 * g3], w_in[:, 2 * g3:3 * g3]]


def dl_forward(x, hns, w, comm=None):
    S = x.shape[0]
    wg = _dl_group_weights(w["dl_w_in"])
    rbT = w["rel_bias"].T
    us, os, ls, Ts, maps = [], [], [], [], []
    for gi, dil in enumerate(DL_DILS):
        u = matmul(f"dl_in{gi}", hns[gi], wg[gi], out_dtype=F32 if gi == 0 else BF16)
        tq = _dl_tq(S, dil)
        W = tq + 2 * DL_STEPS
        onehot, band = _dl_bias_maps(tq, dil)
        (T,) = smallcall(f"dl_bias{gi}", lambda rbT, oh, band: (_mm_exact(rbT, oh) + band,), [rbT, onehot, band],
                         [(DL_HEADS, tq * W)])
        T = T.reshape(DL_HEADS, tq, W)
        o, l = dl_attn_fwd(gi, dil, u, T)
        us.append(u)
        os.append(o)
        ls.append(l)
        Ts.append(T)
        maps.append(onehot)
    rows = [a if d == 1 else _perm(a, d) for a, d in zip(os + ls, DL_DILS * 2)] + [_col(us[0], DL_W, 3)]
    (og,), _ = rowcall("dl_merge", lambda *a: ((_dl_merge(*a),), ()), rows, [], [(DL_W, BF16)], tb=256)
    out = matmul("dl_out", og, w["dl_w_out"], res=x)
    return out, dict(hns=hns, us=us, os=os, ls=ls, Ts=Ts, maps=maps, og=og, wg=wg)


def dl_backward(dy, sv, w, comm=None):
    us = sv["us"]
    dog = matmul("dl_out_dx", dy, w["dl_w_out"], mode="nt")
    g_w_out = matmul("dl_out_dw", sv["og"], dy, mode="tn")

    def merge_bwd(o0, o1, o2, l0, l1, l2, gate, dog):
        _, vjp = jax.vjp(_dl_merge, o0, o1, o2, l0, l1, l2, gate)
        return vjp(dog), ()

    rows = [a if d == 1 else _perm(a, d) for a, d in zip(sv["os"] + sv["ls"], DL_DILS * 2)] + [_col(us[0], DL_W, 3), dog]
    grads7, _ = rowcall("dl_merge_bwd", merge_bwd, rows, [], [(DL_W, F32, d) for d in DL_DILS * 2] + [(DL_W, F32)], tb=256)
    dos, dls, dgate = grads7[0:3], grads7[3:6], grads7[6]
    g_rbT, g_ws, dhns = None, [], []
    for gi, dil in enumerate(DL_DILS):
        du, dT = dl_attn_bwd(gi, dil, us[gi], sv["Ts"][gi], dos[gi], dls[gi], dgate if gi == 0 else None)
        (g,) = smallcall(f"dl_bias_bwd{gi}", lambda dT, oh: (_mm_nt_exact(dT, oh),),
                         [dT.reshape(DL_HEADS, -1), sv["maps"][gi]], [(DL_HEADS, REL_BUCKETS)])
        g_rbT = g if g_rbT is None else g_rbT + g
        g_ws.append(matmul(f"dl_in_dw{gi}", sv["hns"][gi], du, mode="tn"))
        dhns.append(matmul(f"dl_in_dx{gi}", du, sv["wg"][gi], mode="nt"))
    g3 = 3 * DL_W
    g_w_in = jnp.concatenate([g_ws[0][:, :g3], g_ws[1], g_ws[2], g_ws[0][:, g3:]], axis=1)
    return tuple(dhns), dict(dl_w_in=g_w_in[None], dl_w_out=g_w_out[None], rel_bias=g_rbT.T)


_FWD = (ssd_forward, hg_forward, at_forward, dl_forward)
_BWD = (ssd_backward, hg_backward, at_backward, dl_backward)


def _norm_dils(layer):
    return DL_DILS if layer % 4 == 3 else (1,)


class NoExchange:
    def fwd_plans(self, layer, w):
        return None

    def fwd_done(self, layer, got, w):
        pass

    def bwd_plans(self, layer, grads):
        return None

    def bwd_done(self, layer, got):
        pass


def local_step(x, tgt, w, sched=None):
    sched = sched or NoExchange()
    saved = []
    h = x
    for layer in range(DEPTH):
        hn = norm_fwd(f"norm{layer}", h, w["norm_g"][layer:layer + 1], _norm_dils(layer))
        extra = (layer,) if layer % 4 == 1 else ()
        h_next, sv = _FWD[layer % 4](h, hn, w, *extra, comm=sched.fwd_plans(layer, w))
        sched.fwd_done(layer, sv.get("got", {}), w)
        saved.append((h, sv))
        h = h_next
    loss, dh, g_final = loss_head(h, tgt, w["final_g"].reshape(1, D_MODEL))
    grads = {}
    g_norm = [None] * DEPTH
    for layer in reversed(range(DEPTH)):
        h_in, sv = saved[layer]
        extra = (layer,) if layer % 4 == 1 else ()
        dhn, g = _BWD[layer % 4](dh, sv, w, *extra, comm=sched.bwd_plans(layer, grads))
        sched.bwd_done(layer, g.pop("got", {}))
        grads.update(g)
        dh, g_norm[layer] = norm_bwd(f"norm{layer}_bwd", h_in, w["norm_g"][layer:layer + 1], dhn, dh, _norm_dils(layer))
    grads["norm_g"] = jnp.concatenate(g_norm, axis=0)
    grads["final_g"] = g_final.reshape(D_MODEL)
    grads["ssd_d"] = jnp.sum(grads.pop("ssd_d_exp").reshape(SSD_HEADS, SSD_HEADDIM), axis=1)[None]
    return loss, dh, grads


IN_NAMES = ("ssd_w_in", "hg_w_in", "at_w_in", "dl_w_in")
OUT_NAMES = ("ssd_w_out", "hg_w_out", "at_w_out", "dl_w_out")
IN_COLS = (SSD_IN // 4, HG_IN // 4, AT_IN // 4, DL_IN // 4)
OUT_ROWS = (SSD_DI // 4, HG_W // 4, AT_QW // 4, DL_W // 4)
PACK_IN = sum(IN_COLS)
PACK_OUT = sum(OUT_ROWS)
N_CHIPS = 4
N_DEV = 8
HBM = pl.BlockSpec(memory_space=pl.ANY)


def _mesh_pos():
    return lax.axis_index("x"), lax.axis_index("y"), lax.axis_index("c")


def _other_chips(x, y):
    return [(1 - x, y), (x, 1 - y), (1 - x, 1 - y)]


def _half_rows(half, n):
    return pl.ds(pl.multiple_of(half * n, n), n)


def _remote(src, dst, send, recv, k, to):
    return pltpu.make_async_remote_copy(src_ref=src, dst_ref=dst, send_sem=send.at[k], recv_sem=recv.at[k],
                                        device_id=to, device_id_type=MESH)


def gather_plan(packs, whole=()):
    arrs = list(packs) + list(whole)
    n_half = len(packs)

    def pieces(ins, outs):
        x, y, c = _mesh_pos()
        for a, (src, dst) in enumerate(zip(ins, outs)):
            h = src.shape[0] // 2 if a < n_half else None
            for j, (px, py) in enumerate(_other_chips(x, y)):
                yield a, j, src, dst, h, (x, y, c), (px, py)

    def start(ins, outs, send, recv):
        for a, j, src, dst, h, (x, y, c), (px, py) in pieces(ins, outs):
            me = 2 * x + y
            if h is None:
                _remote(src, dst.at[me], send, recv, 6 * a + j, (px, py, c)).start()
            else:
                _remote(src.at[_half_rows(c, h)], dst.at[me, _half_rows(c, h)], send, recv, 6 * a + j, (px, py, c)).start()

    def mid(ins, outs, send, recv):
        for a, j, src, dst, h, (x, y, c), (px, py) in pieces(ins, outs):
            kp = 2 * px + py
            if h is None:
                _remote(src, dst.at[kp], send, recv, 6 * a + j, (px, py, c)).wait_recv()
            else:
                got = dst.at[kp, _half_rows(c, h)]
                _remote(src.at[_half_rows(c, h)], got, send, recv, 6 * a + j, (px, py, c)).wait_recv()
                _remote(got, got, send, recv, 6 * a + 3 + j, (x, y, 1 - c)).start()

    def finish(ins, outs, send, recv):
        for a, j, src, dst, h, (x, y, c), (px, py) in pieces(ins, outs):
            me, kp = 2 * x + y, 2 * px + py
            if h is None:
                _remote(src, dst.at[me], send, recv, 6 * a + j, (px, py, c)).wait_send()
            else:
                theirs = dst.at[kp, _half_rows(1 - c, h)]
                _remote(theirs, theirs, send, recv, 6 * a + 3 + j, (x, y, 1 - c)).wait_recv()
                _remote(src.at[_half_rows(c, h)], dst.at[me, _half_rows(c, h)], send, recv, 6 * a + j, (px, py, c)).wait_send()
                mine = dst.at[kp, _half_rows(c, h)]
                _remote(mine, mine, send, recv, 6 * a + 3 + j, (x, y, 1 - c)).wait_send()

    return dict(ins=arrs, out_shape=[jax.ShapeDtypeStruct((N_CHIPS,) + a.shape, a.dtype) for a in arrs],
                n_sems=6 * len(arrs), start=start, mid=mid, finish=finish)


def scatter_plan(halves):
    def copies(ins, outs, send, recv):
        x, y, c = _mesh_pos()
        for a, (src, dst) in enumerate(zip(ins, outs)):
            for j, (px, py) in enumerate(_other_chips(x, y)):
                yield _remote(src.at[2 * px + py], dst.at[j], send, recv, 3 * a + j, (px, py, c))

    def start(ins, outs, send, recv):
        for cp in copies(ins, outs, send, recv):
            cp.start()

    def finish(ins, outs, send, recv):
        for cp in copies(ins, outs, send, recv):
            cp.wait()

    return dict(ins=list(halves), out_shape=[jax.ShapeDtypeStruct((3,) + a.shape[1:], a.dtype) for a in halves],
                n_sems=3 * len(halves), start=start, mid=None, finish=finish)


def run_exchange(name, plan):
    n_in = len(plan["ins"])

    def body(*refs):
        ins, outs = refs[:n_in], refs[n_in:-2]
        send, recv = refs[-2], refs[-1]
        plan["start"](ins, outs, send, recv)
        if plan["mid"] is not None:
            plan["mid"](ins, outs, send, recv)
        plan["finish"](ins, outs, send, recv)

    return pl.pallas_call(
        body, name=name, in_specs=[HBM] * n_in, out_specs=[HBM] * len(plan["out_shape"]), out_shape=plan["out_shape"],
        scratch_shapes=[pltpu.SemaphoreType.DMA((plan["n_sems"],))] * 2,
        compiler_params=pltpu.CompilerParams(has_side_effects=True),
    )(*plan["ins"])


def swap_halves(name, g_in, g_out):
    h_in, h_out = g_in.shape[1] // 2, g_out.shape[1] // 2

    def body(gi, go, ri, ro, send, recv):
        x, y, c = _mesh_pos()
        sib = (x, y, 1 - c)

        def rows(half, n):
            return pl.ds(pl.multiple_of(half * n, n), n)

        cps = [pltpu.make_async_remote_copy(src_ref=gi.at[:, rows(1 - c, h_in)], dst_ref=ri, send_sem=send.at[0],
                                            recv_sem=recv.at[0], device_id=sib, device_id_type=MESH),
               pltpu.make_async_remote_copy(src_ref=go.at[:, rows(1 - c, h_out)], dst_ref=ro, send_sem=send.at[1],
                                            recv_sem=recv.at[1], device_id=sib, device_id_type=MESH)]
        for cp in cps:
            cp.start()
        for cp in cps:
            cp.wait()

    return pl.pallas_call(
        body, name=name, in_specs=[HBM, HBM], out_specs=[HBM, HBM],
        out_shape=[jax.ShapeDtypeStruct((N_CHIPS, h_in, g_in.shape[2]), g_in.dtype),
                   jax.ShapeDtypeStruct((N_CHIPS, h_out, g_out.shape[2]), g_out.dtype)],
        scratch_shapes=[pltpu.SemaphoreType.DMA((2,)), pltpu.SemaphoreType.DMA((2,))],
        compiler_params=pltpu.CompilerParams(has_side_effects=True),
    )(g_in, g_out)


def half_add(name, g, r, c_idx, tb):
    _, rows2, C = g.shape
    h = rows2 // 2
    nb = h // tb

    def body(c_ref, g_ref, r_ref, f_ref, b_ref):
        s = g_ref[...] + r_ref[...]
        f_ref[...] = s
        b_ref[...] = s.astype(BF16)

    grid_spec = pltpu.PrefetchScalarGridSpec(
        num_scalar_prefetch=1, grid=(N_CHIPS, nb),
        in_specs=[pl.BlockSpec((1, tb, C), lambda k, i, c: (k, c[0] * nb + i, 0)),
                  pl.BlockSpec((1, tb, C), lambda k, i, c: (k, i, 0))],
        out_specs=[pl.BlockSpec((1, tb, C), lambda k, i, c: (k, i, 0))] * 2)
    return pl.pallas_call(
        body, name=name, grid_spec=grid_spec,
        out_shape=[jax.ShapeDtypeStruct((N_CHIPS, h, C), F32), jax.ShapeDtypeStruct((N_CHIPS, h, C), BF16)],
        compiler_params=_cp(("parallel", "parallel")),
    )(c_idx, g, r)


def chip_sum(name, f, r, me_idx, tb):
    _, h, C = f.shape
    nb = h // tb

    def body(me_ref, f_ref, r0, r1, r2, o_ref):
        o_ref[...] = ((f_ref[0] + r0[0].astype(F32)) + r1[0].astype(F32)) + r2[0].astype(F32)

    def slot(j):
        return pl.BlockSpec((1, tb, C), lambda i, me: (j, i, 0))

    grid_spec = pltpu.PrefetchScalarGridSpec(
        num_scalar_prefetch=1, grid=(nb,),
        in_specs=[pl.BlockSpec((1, tb, C), lambda i, me: (me[0], i, 0)), slot(0), slot(1), slot(2)],
        out_specs=pl.BlockSpec((tb, C), lambda i, me: (i, 0)))
    return pl.pallas_call(
        body, name=name, grid_spec=grid_spec, out_shape=jax.ShapeDtypeStruct((h, C), F32),
        compiler_params=_cp(("parallel",)),
    )(me_idx, f, r, r, r)


def share_halves(name, f_in, f_out):
    def body(fi, fo, oi, oo, send, recv):
        x, y, c = _mesh_pos()
        sib = (x, y, 1 - c)
        cps = [pltpu.make_async_remote_copy(src_ref=fi, dst_ref=oi, send_sem=send.at[0], recv_sem=recv.at[0],
                                            device_id=sib, device_id_type=MESH),
               pltpu.make_async_remote_copy(src_ref=fo, dst_ref=oo, send_sem=send.at[1], recv_sem=recv.at[1],
                                            device_id=sib, device_id_type=MESH)]
        for cp in cps:
            cp.start()
        for cp in cps:
            cp.wait()

    return pl.pallas_call(
        body, name=name, in_specs=[HBM, HBM], out_specs=[HBM, HBM],
        out_shape=[jax.ShapeDtypeStruct(f_in.shape, F32), jax.ShapeDtypeStruct(f_out.shape, F32)],
        scratch_shapes=[pltpu.SemaphoreType.DMA((2,)), pltpu.SemaphoreType.DMA((2,))],
        compiler_params=pltpu.CompilerParams(has_side_effects=True),
    )(f_in, f_out)


def gather_small(pack):
    def body(p, g, send, recv, lsem):
        x, y, c = _mesh_pos()
        me = 4 * x + 2 * y + c
        local = pltpu.make_async_copy(p, g.at[me], lsem)
        local.start()
        cps = []
        k = 0
        for fx in (0, 1):
            for fy in (0, 1):
                for fc in (0, 1):
                    if fx + fy + fc == 0:
                        continue
                    to = (x ^ fx, y ^ fy, c ^ fc)
                    cps.append((pltpu.make_async_remote_copy(src_ref=p, dst_ref=g.at[me], send_sem=send.at[k],
                                                             recv_sem=recv.at[k], device_id=to, device_id_type=MESH), to, k))
                    k += 1
        for cp, _, _ in cps:
            cp.start()
        for cp, to, k in cps:
            frm = 4 * to[0] + 2 * to[1] + to[2]
            pltpu.make_async_remote_copy(src_ref=p, dst_ref=g.at[frm], send_sem=send.at[k], recv_sem=recv.at[k],
                                         device_id=to, device_id_type=MESH).wait_recv()
        for cp, _, _ in cps:
            cp.wait_send()
        local.wait()

    return pl.pallas_call(
        body, name="gather_small", in_specs=[HBM], out_specs=HBM,
        out_shape=jax.ShapeDtypeStruct((N_DEV,) + pack.shape, pack.dtype),
        scratch_shapes=[pltpu.SemaphoreType.DMA((7,)), pltpu.SemaphoreType.DMA((7,)), pltpu.SemaphoreType.DMA],
        compiler_params=pltpu.CompilerParams(has_side_effects=True),
    )(pack)


def _adamw(w, g, m, v):
    m = ADAM_B1 * m + (1.0 - ADAM_B1) * g
    v = ADAM_B2 * v + (1.0 - ADAM_B2) * (g * g)
    m_hat = m / (1.0 - ADAM_B1 ** ADAM_STEP)
    v_hat = v / (1.0 - ADAM_B2 ** ADAM_STEP)
    delta = -ADAM_LR * (m_hat / (jnp.sqrt(v_hat) + ADAM_EPS) + ADAM_WD * w)
    return delta, m, v


def adamw_big(name, w, g, m, v):
    shp = w.shape
    flat = lambda a: a.reshape(shp[-2], shp[-1])
    (d, nm, nv), _ = rowcall(name, lambda w, g, m, v: (_adamw(w, g, m, v), ()), [flat(w), flat(g), flat(m), flat(v)], [],
                             [(shp[-1], F32)] * 3, tb=256)
    return d.reshape(shp), nm.reshape(shp), nv.reshape(shp)


def _pack_small(arrs):
    flat = jnp.concatenate([a.reshape(-1) for a in arrs])
    n = flat.shape[0]
    rows = -(-n // (SUBLANE * LANE)) * SUBLANE
    return jnp.pad(flat, (0, rows * LANE - n)).reshape(rows, LANE)


def _unpack_small(pack, shapes):
    flat = pack.reshape(-1)
    outs, off = [], 0
    for s in shapes:
        n = int(np.prod(s))
        outs.append(flat[off:off + n].reshape(s))
        off += n
    return outs


SMALL_NAMES = ("norm_g", "final_g", "rel_bias", "hgrn_lb", "ssd_conv_w", "ssd_conv_b", "ssd_dt_bias", "ssd_a_log",
               "ssd_d", "ssd_norm_g", "hg_norm_g", "at_q_norm_g", "at_k_norm_g")
ALL_NAMES = ("norm_g", "final_g", "rel_bias", "hgrn_lb", "ssd_w_in", "ssd_conv_w", "ssd_conv_b", "ssd_dt_bias",
             "ssd_a_log", "ssd_d", "ssd_norm_g", "ssd_w_out", "hg_w_in", "hg_norm_g", "hg_w_out", "at_w_in",
             "at_q_norm_g", "at_k_norm_g", "at_w_out", "dl_w_in", "dl_w_out")


def kernel(x, norm_g, final_g, rel_bias, hgrn_lb, ssd_w_in, ssd_conv_w, ssd_conv_b, ssd_dt_bias, ssd_a_log, ssd_d, ssd_norm_g, ssd_w_out, hg_w_in, hg_norm_g, hg_w_out, at_w_in, at_q_norm_g, at_k_norm_g, at_w_out, dl_w_in, dl_w_out, loss_target, m_norm_g, m_final_g, m_rel_bias, m_hgrn_lb, m_ssd_w_in, m_ssd_conv_w, m_ssd_conv_b, m_ssd_dt_bias, m_ssd_a_log, m_ssd_d, m_ssd_norm_g, m_ssd_w_out, m_hg_w_in, m_hg_norm_g, m_hg_w_out, m_at_w_in, m_at_q_norm_g, m_at_k_norm_g, m_at_w_out, m_dl_w_in, m_dl_w_out, v_norm_g, v_final_g, v_rel_bias, v_hgrn_lb, v_ssd_w_in, v_ssd_conv_w, v_ssd_conv_b, v_ssd_dt_bias, v_ssd_a_log, v_ssd_d, v_ssd_norm_g, v_ssd_w_out, v_hg_w_in, v_hg_norm_g, v_hg_w_out, v_at_w_in, v_at_q_norm_g, v_at_k_norm_g, v_at_w_out, v_dl_w_in, v_dl_w_out):
    args = locals()
    W = {n: args[n] for n in ALL_NAMES}
    M = {n: args["m_" + n] for n in ALL_NAMES}
    V = {n: args["v_" + n] for n in ALL_NAMES}
    xi, yi, ci = lax.axis_index("x"), lax.axis_index("y"), lax.axis_index("c")
    chip = 2 * xi + yi
    conv_shard = SSD_CONV_CH // N_CHIPS
    hgn_shard = HG_W // N_CHIPS

    p_in = [W[n][0].astype(BF16) for n in IN_NAMES]
    p_out = [W[n][0].astype(BF16) for n in OUT_NAMES]
    p_small = jnp.concatenate([
        jnp.pad(ssd_conv_w[0], ((0, 0), (0, D_MODEL - conv_shard))),
        jnp.pad(hg_norm_g, ((0, 0), (0, D_MODEL - hgn_shard)))], axis=0)
    c_idx = ci.astype(jnp.int32).reshape(1)
    me_idx = chip.astype(jnp.int32).reshape(1)

    def slot(stack, own, k):
        return jnp.where(chip == k, own, stack[k])

    def layer_weights(layer, got):
        s_in, s_out = got[0], got[1]
        return (jnp.concatenate([slot(s_in, p_in[layer], k) for k in range(N_CHIPS)], axis=1),
                jnp.concatenate([slot(s_out, p_out[layer], k) for k in range(N_CHIPS)], axis=0))

    def reduce_start(tag, layers, grads):
        gp_in = jnp.concatenate([grads[IN_NAMES[l]][0].reshape(D_MODEL, N_CHIPS, IN_COLS[l]).transpose(1, 0, 2)
                                 for l in layers], axis=2)
        gp_out = jnp.concatenate([grads[OUT_NAMES[l]][0].reshape(N_CHIPS, OUT_ROWS[l], D_MODEL) for l in layers], axis=1)
        r_in, r_out = swap_halves(f"swap_halves_{tag}", gp_in, gp_out)
        f_in, b_in = half_add(f"half_add_in_{tag}", gp_in, r_in, c_idx, 128)
        f_out, b_out = half_add(f"half_add_out_{tag}", gp_out, r_out, c_idx, 128)
        return (f_in, f_out), scatter_plan([b_in, b_out])

    def reduce_finish(tag, layers, halves, got, G):
        s_in = chip_sum(f"chip_sum_in_{tag}", halves[0], got[0], me_idx, 128)
        s_out = chip_sum(f"chip_sum_out_{tag}", halves[1], got[1], me_idx, 128)
        o_in, o_out = share_halves(f"share_halves_{tag}", s_in, s_out)
        red_in = jnp.where(ci == 0, jnp.concatenate([s_in, o_in], axis=0), jnp.concatenate([o_in, s_in], axis=0))
        red_out = jnp.where(ci == 0, jnp.concatenate([s_out, o_out], axis=0), jnp.concatenate([o_out, s_out], axis=0))
        off_c = off_r = 0
        for l in layers:
            G[IN_NAMES[l]] = red_in[:, off_c:off_c + IN_COLS[l]][None]
            G[OUT_NAMES[l]] = red_out[off_r:off_r + OUT_ROWS[l]][None]
            off_c += IN_COLS[l]
            off_r += OUT_ROWS[l]

    class Schedule:
        early = (2, 3)

        def fwd_plans(self, layer, w):
            if layer == 0:
                return dict(ssd_scan_f=gather_plan([p_in[1], p_out[1]]), ssd_scan_b=gather_plan([p_in[2], p_out[2]]))
            if layer == 1:
                return dict(hg_scan_f=gather_plan([p_in[3]]), hg_scan_b=gather_plan([p_out[3]]))
            return None

        def fwd_done(self, layer, got, w):
            if layer == 0:
                w["hg_w_in"], w["hg_w_out"] = layer_weights(1, got["ssd_scan_f"])
                w["at_w_in"], w["at_w_out"] = layer_weights(2, got["ssd_scan_b"])
            if layer == 1:
                w["dl_w_in"], w["dl_w_out"] = layer_weights(3, got["hg_scan_f"] + got["hg_scan_b"])

        def bwd_plans(self, layer, grads):
            if layer == 1:
                self.halves, plan = reduce_start("a", self.early, grads)
                return dict(hg_scan_f_bwd=plan)
            return None

        def bwd_done(self, layer, got):
            if layer == 1:
                self.got = got["hg_scan_f_bwd"]

    g0_in, g0_out, g_small = run_exchange("gather_w0", gather_plan([p_in[0], p_out[0]], whole=[p_small]))
    ssd_in_full, ssd_out_full = layer_weights(0, (g0_in, g0_out))
    conv_full = jnp.concatenate([slot(g_small, p_small, k)[:SSD_CONV, :conv_shard] for k in range(N_CHIPS)], axis=1)
    hgn_full = jnp.concatenate([slot(g_small, p_small, k)[SSD_CONV:SSD_CONV + 1, :hgn_shard] for k in range(N_CHIPS)], axis=1)
    w = dict(
        norm_g=norm_g, final_g=final_g, rel_bias=rel_bias, hgrn_lb=hgrn_lb,
        ssd_w_main=ssd_in_full[:, :SSD_MAIN], ssd_w_dt=ssd_in_full[:, SSD_MAIN:],
        ssd_conv_w8=jnp.concatenate([conv_full, jnp.zeros((1, SSD_CONV_CH), F32)], axis=0),
        ssd_conv_b=ssd_conv_b, ssd_dt_bias=ssd_dt_bias, ssd_a_log=ssd_a_log, ssd_d=ssd_d, ssd_norm_g=ssd_norm_g,
        ssd_w_out=ssd_out_full, hg_norm_g=hgn_full, at_q_norm_g=at_q_norm_g, at_k_norm_g=at_k_norm_g)

    sched = Schedule()
    loss_tile, grad_x, grads = local_step(x[0], loss_target[0], w, sched)
    loss = lax.psum(loss_tile[0, 0], ("x", "y", "c"))

    G = {}
    late = (0, 1)
    halves_b, plan_b = reduce_start("b", late, grads)
    got_b = run_exchange("scatter_b", plan_b)
    reduce_finish("a", sched.early, sched.halves, sched.got, G)
    reduce_finish("b", late, halves_b, got_b, G)

    small_full = [grads[n].reshape(-1) for n in SMALL_NAMES]
    shapes_full = [grads[n].shape for n in SMALL_NAMES]
    packs = gather_small(_pack_small(small_full))
    (red_small,) = smallcall("sum_small", lambda p: (functools.reduce(lambda a, b: a + b, [p[k] for k in range(N_DEV)]),),
                             [packs], [packs.shape[1:]])
    for n, g in zip(SMALL_NAMES, _unpack_small(red_small, shapes_full)):
        G[n] = g
    G["ssd_conv_w"] = lax.dynamic_slice_in_dim(G["ssd_conv_w"].reshape(1, SSD_CONV, SSD_CONV_CH), chip * conv_shard, conv_shard, axis=2)
    G["hg_norm_g"] = lax.dynamic_slice_in_dim(G["hg_norm_g"].reshape(1, HG_W), chip * hgn_shard, hgn_shard, axis=1)
    for n in SMALL_NAMES:
        G[n] = G[n].reshape(W[n].shape)

    D, NM, NV = {}, {}, {}
    for n in IN_NAMES + OUT_NAMES:
        D[n], NM[n], NV[n] = adamw_big("adamw_" + n, W[n], G[n], M[n], V[n])
    shapes = [W[n].shape for n in SMALL_NAMES]
    pk = [_pack_small([T[n] for n in SMALL_NAMES]) for T in (W, G, M, V)]
    outs = smallcall("adamw_small", lambda w, g, m, v: _adamw(w, g, m, v), pk, [pk[0].shape] * 3)
    for T, pack in zip((D, NM, NV), outs):
        for n, a in zip(SMALL_NAMES, _unpack_small(pack, shapes)):
            T[n] = a
    return (loss, grad_x[None], *[G[n] for n in ALL_NAMES], *[D[n] for n in ALL_NAMES],
            *[NM[n] for n in ALL_NAMES], *[NV[n] for n in ALL_NAMES])
```

```python
import functools
import math

import numpy as np
import jax
import jax.numpy as jnp
from jax import lax
from jax.experimental import pallas as pl
from jax.experimental.pallas import tpu as pltpu

F32 = jnp.float32
BF16 = jnp.bfloat16
MESH = pl.DeviceIdType.MESH

D_MODEL = 1024
DEPTH = 4
GRID_W = 64
EPS = 1e-6
NEG_BIG = -1e30

SSD_DI = 2048
SSD_HEADDIM = 64
SSD_HEADS = 32
SSD_GROUPS = 4
SSD_HPG = 8
SSD_STATE = 128
SSD_CONV = 7
SSD_CHUNK = 128
SSD_GPS = 4
SSD_CONV_CH = SSD_DI + 2 * SSD_GROUPS * SSD_STATE
SSD_MAIN = SSD_DI + SSD_CONV_CH
SSD_IN = SSD_MAIN + 2 * SSD_HEADS

HG_HEADS = 8
HG_EXPAND = 128
HG_W = 1024
HG_CHUNK = 32
HG_ROWS = 256
HG_HPS = 8
HG_IN = 5 * HG_W

AT_HEADS = 16
AT_KV = 8
AT_GRP = 2
AT_HD = 128
ROPE_THETA = 10000.0
ROPE_AXIS = 64
AT_QW = AT_HEADS * AT_HD
AT_KW = AT_KV * AT_HD
AT_IN = 2 * AT_QW + 2 * AT_KW

DL_PAIRS = ((128, 1), (512, 4), (2048, 16))
DL_HEADS = 16
DL_HD = 64
DL_W = 1024
DL_STEPS = 64
DL_IN = 10 * DL_W
REL_BUCKETS = 32
REL_MAX_DIST = 1024

ADAM_LR = 0.001
ADAM_B1 = 0.9
ADAM_B2 = 0.999
ADAM_EPS = 1e-08
ADAM_WD = 0.01
ADAM_STEP = 10

VMEM_LIMIT = 56 * 1024 * 1024
LANE = 128
SUBLANE = 8


def _cp(sem=None):
    return pltpu.CompilerParams(dimension_semantics=sem, vmem_limit_bytes=VMEM_LIMIT)


_NN, _NT, _TN = ((1,), (0,)), ((1,), (1,)), ((0,), (0,))


def _dot(a, b, dims):
    return lax.dot_general(a.astype(BF16), b.astype(BF16), (dims, ((), ())), preferred_element_type=F32)


def _dot_rule(dims, da_rule, db_rule):
    @jax.custom_vjp
    def f(a, b):
        return _dot(a, b, dims)

    def fwd(a, b):
        return _dot(a, b, dims), (a, b)

    def bwd(res, g):
        a, b = res
        return da_rule(a, b, g).astype(a.dtype), db_rule(a, b, g).astype(b.dtype)

    f.defvjp(fwd, bwd)
    return f


_mm = _dot_rule(_NN, lambda a, b, g: _dot(g, b, _NT), lambda a, b, g: _dot(a, g, _TN))
_mm_nt = _dot_rule(_NT, lambda a, b, g: _dot(g, b, _NN), lambda a, b, g: _dot(g, a, _TN))
_mm_tn = _dot_rule(_TN, lambda a, b, g: _dot(b, g, _NT), lambda a, b, g: _dot(a, g, _NN))


def _mm_exact(a, b):
    return jnp.dot(a, b, preferred_element_type=F32, precision=lax.Precision.HIGHEST)


def _dot3(t, a, dims):
    hi = a.astype(BF16)
    r1 = a - hi.astype(F32)
    mid = r1.astype(BF16)
    lo = r1 - mid.astype(F32)
    return _dot(t, hi, dims) + (_dot(t, mid, dims) + _dot(t, lo, dims))


@jax.custom_vjp
def _mm_tri(t, a):
    return _dot3(t, a, _NN)


def _mm_tri_fwd(t, a):
    return _dot3(t, a, _NN), t


def _mm_tri_bwd(t, g):
    return None, _dot3(t, g, _TN)


_mm_tri.defvjp(_mm_tri_fwd, _mm_tri_bwd)


def _mm_nt_exact(a, b):
    return lax.dot_general(a, b, (((1,), (1,)), ((), ())), preferred_element_type=F32,
                           precision=lax.Precision.HIGHEST)


def _silu(x):
    return x * jax.nn.sigmoid(x)


def _softplus(z):
    return jnp.maximum(z, 0.0) + jnp.log(1.0 + jnp.exp(-jnp.abs(z)))


def _pick(dim, pref):
    best = None
    t = LANE
    while t <= min(dim, pref):
        if dim % t == 0:
            best = t
        t += LANE
    return best if best is not None else dim


def _const_map(n):
    return lambda *_: (0,) * n


MM_BLOCK_BYTES = 8 * 1024 * 1024


def _mm_tiles(mode, M, N, K, a_bytes, b_bytes):
    if mode == "nn":
        tk = K if K <= 2048 else _pick(K, 1024)
        tm = _pick(M, max(512, MM_BLOCK_BYTES // (tk * a_bytes)))
        tn = _pick(N, 512)
    elif mode == "tn":
        tk = K if K <= 4096 else _pick(K, 1024)
        tm = _pick(M, MM_BLOCK_BYTES // (tk * a_bytes))
        tn = _pick(N, MM_BLOCK_BYTES // (tk * b_bytes))
    else:
        tk = _pick(K, 1024)
        tn = _pick(N, 1024)
        tm = _pick(M, MM_BLOCK_BYTES // (8 * tn))
    return tm, tn, tk


def matmul(name, a, b, mode="nn", res=None, out_dtype=F32):
    if mode == "tn":
        K, M = a.shape
    else:
        M, K = a.shape
    N = b.shape[0] if mode == "nt" else b.shape[1]
    tm, tn, tk = _mm_tiles(mode, M, N, K, a.dtype.itemsize, b.dtype.itemsize)
    nk = K // tk
    a_spec = (pl.BlockSpec((tk, tm), lambda i, j, k: (k, i)) if mode == "tn"
              else pl.BlockSpec((tm, tk), lambda i, j, k: (i, k)))
    b_spec = (pl.BlockSpec((tn, tk), lambda i, j, k: (j, k)) if mode == "nt"
              else pl.BlockSpec((tk, tn), lambda i, j, k: (k, j)))
    dot = {"nn": _mm, "nt": _mm_nt, "tn": _mm_tn}[mode]
    has_res = res is not None

    def body(*refs):
        a_ref, b_ref = refs[0], refs[1]
        r_ref = refs[2] if has_res else None
        o_ref = refs[3] if has_res else refs[2]

        def finish(out):
            if has_res:
                out = out + r_ref[...].astype(F32)
            o_ref[...] = out.astype(o_ref.dtype)

        if nk == 1:
            finish(dot(a_ref[...], b_ref[...]))
            return
        acc = refs[-1]
        k = pl.program_id(2)

        @pl.when(k == 0)
        def _():
            acc[...] = jnp.zeros_like(acc)

        acc[...] += dot(a_ref[...], b_ref[...])

        @pl.when(k == nk - 1)
        def _():
            finish(acc[...])

    in_specs = [a_spec, b_spec]
    args = [a, b]
    if has_res:
        in_specs.append(pl.BlockSpec((tm, tn), lambda i, j, k: (i, j)))
        args.append(res)
    return pl.pallas_call(
        body, name=name, grid=(M // tm, N // tn, nk), in_specs=in_specs,
        out_specs=pl.BlockSpec((tm, tn), lambda i, j, k: (i, j)),
        out_shape=jax.ShapeDtypeStruct((M, N), out_dtype),
        scratch_shapes=[pltpu.VMEM((tm, tn), F32)] if nk > 1 else [],
        compiler_params=_cp(("parallel", "parallel", "arbitrary")),
    )(*args)


def call_with_comm(body, comm, *, name, grid, in_specs, out_specs, out_shape, scratch_shapes, semantics, args):
    if comm is None:
        outs = pl.pallas_call(body, name=name, grid=grid, in_specs=in_specs, out_specs=out_specs, out_shape=out_shape,
                              scratch_shapes=scratch_shapes, compiler_params=_cp(semantics))(*args)
        return list(outs), []
    n_in, n_out, n_scr = len(in_specs), len(out_specs), len(scratch_shapes)
    c_in, c_out = len(comm["ins"]), len(comm["out_shape"])
    total = int(np.prod(grid))
    mid_step = (2 * total) // 3

    def wrapped(*refs):
        p = 0
        ins = refs[p:p + n_in]
        p += n_in
        cins = refs[p:p + c_in]
        p += c_in
        outs = refs[p:p + n_out]
        p += n_out
        couts = refs[p:p + c_out]
        p += c_out
        scr = refs[p:p + n_scr]
        send, recv = refs[p + n_scr], refs[p + n_scr + 1]
        step = pl.program_id(0)
        for ax in range(1, len(grid)):
            step = step * grid[ax] + pl.program_id(ax)

        @pl.when(step == 0)
        def _():
            comm["start"](cins, couts, send, recv)

        body(*ins, *outs, *scr)
        if comm["mid"] is not None:
            @pl.when(step == mid_step)
            def _():
                comm["mid"](cins, couts, send, recv)

        @pl.when(step == total - 1)
        def _():
            comm["finish"](cins, couts, send, recv)

    outs = pl.pallas_call(
        wrapped, name=name, grid=grid, in_specs=list(in_specs) + [HBM] * c_in,
        out_specs=list(out_specs) + [HBM] * c_out, out_shape=list(out_shape) + list(comm["out_shape"]),
        scratch_shapes=list(scratch_shapes) + [pltpu.SemaphoreType.DMA((comm["n_sems"],))] * 2,
        compiler_params=_cp(("arbitrary",) * len(grid)),
    )(*args, *comm["ins"])
    return list(outs[:n_out]), list(outs[n_out:])


def _col(arr, width, idx):
    return (arr, width, idx)


def _perm(arr, dil, width=None, idx=0):
    return (arr, arr.shape[1] if width is None else width, idx, dil)


def _from_perm(ref, scr, dil):
    n, w = ref.shape[1], ref.shape[2]
    for r in range(dil):
        for j in range(w // LANE):
            scr[j, pl.ds(r, n, stride=dil), :] = ref[r, :, j * LANE:(j + 1) * LANE].astype(F32)
    return jnp.concatenate([scr[j] for j in range(w // LANE)], axis=1)


def _to_perm(val, ref, scr, dil):
    n, w = ref.shape[1], ref.shape[2]
    for j in range(w // LANE):
        scr[j] = val[:, j * LANE:(j + 1) * LANE].astype(F32)
    for r in range(dil):
        ref[r] = jnp.concatenate([scr[j, pl.ds(r, n, stride=dil), :] for j in range(w // LANE)], axis=1).astype(ref.dtype)


def rowcall(name, fn, rows, bcs, row_outs, bc_outs=(), tb=256, halo=()):
    rows = [r if isinstance(r, tuple) else (r, r.shape[1], 0) for r in rows]
    rows = [r if len(r) == 4 else r + (1,) for r in rows]
    row_outs = [o if len(o) == 3 else o + (1,) for o in row_outs]
    S = rows[0][0].shape[0]
    tb = min(tb, S)
    nb = S // tb
    n_r, n_h, n_b, n_ro, n_bo = len(rows), len(halo), len(bcs), len(row_outs), len(bc_outs)
    hb = tb // SUBLANE
    last = S // SUBLANE - 1
    perm_w = max([w for (_, w, _, d) in rows if d > 1] + [w for (w, _, d) in row_outs if d > 1] + [0])

    def body(*refs):
        i = pl.program_id(0)
        scr = refs[-1] if perm_w else None
        pos = 0
        r_in = [r[...] if d == 1 else _from_perm(r, scr, d) for r, (_, _, _, d) in zip(refs[pos:pos + n_r], rows)]
        pos += n_r
        h_in = []
        for _ in range(n_h):
            prev = refs[pos][...] * (i > 0).astype(F32)
            nxt = refs[pos + 1][...] * (i < nb - 1).astype(F32)
            h_in += [prev, nxt]
            pos += 2
        b_in = [r[...] for r in refs[pos:pos + n_b]]
        pos += n_b
        ro = refs[pos:pos + n_ro]
        bo = refs[pos + n_ro:pos + n_ro + n_bo]
        outs_r, outs_b = fn(*r_in, *h_in, *b_in)
        for ref, val, (_, _, d) in zip(ro, outs_r, row_outs, strict=True):
            if d == 1:
                ref[...] = val.astype(ref.dtype)
            else:
                _to_perm(val, ref, scr, d)
        if n_bo:
            @pl.when(i == 0)
            def _():
                for ref in bo:
                    ref[...] = jnp.zeros_like(ref)

            for ref, val in zip(bo, outs_b, strict=True):
                ref[...] += val

    in_specs, args = [], []
    for (a, w, c, d) in rows:
        if d == 1:
            in_specs.append(pl.BlockSpec((tb, w), functools.partial(lambda i, c: (i, c), c=c)))
            args.append(a)
        else:
            in_specs.append(pl.BlockSpec((d, tb // d, w), functools.partial(lambda i, c: (0, i, c), c=c)))
            args.append(a.reshape(d, S // d, a.shape[1]))
    for h in halo:
        a, w, c, _ = rows[h]
        in_specs.append(pl.BlockSpec((SUBLANE, w), functools.partial(
            lambda i, c: (jnp.maximum(i * hb - 1, 0), c), c=c)))
        in_specs.append(pl.BlockSpec((SUBLANE, w), functools.partial(
            lambda i, c: (jnp.minimum((i + 1) * hb, last), c), c=c)))
        args += [a, a]
    for b in bcs:
        in_specs.append(pl.BlockSpec(b.shape, _const_map(b.ndim)))
        args.append(b)
    out_specs, out_shape = [], []
    for (w, dt, d) in row_outs:
        if d == 1:
            out_specs.append(pl.BlockSpec((tb, w), lambda i: (i, 0)))
            out_shape.append(jax.ShapeDtypeStruct((S, w), dt))
        else:
            out_specs.append(pl.BlockSpec((d, tb // d, w), lambda i: (0, i, 0)))
            out_shape.append(jax.ShapeDtypeStruct((d, S // d, w), dt))
    for shp in bc_outs:
        out_specs.append(pl.BlockSpec(shp, _const_map(len(shp))))
        out_shape.append(jax.ShapeDtypeStruct(shp, F32))
    outs = pl.pallas_call(
        body, name=name, grid=(nb,), in_specs=in_specs, out_specs=out_specs, out_shape=out_shape,
        scratch_shapes=[pltpu.VMEM((perm_w // LANE, tb, LANE), F32)] if perm_w else [],
        compiler_params=_cp(("arbitrary",) if n_bo else ("parallel",)),
    )(*args)
    row_res = [o if d == 1 else o.reshape(S, w) for o, (w, _, d) in zip(outs[:n_ro], row_outs)]
    return row_res, list(outs[n_ro:])


def smallcall(name, fn, ins, out_shapes):
    n_in = len(ins)

    def body(*refs):
        outs = fn(*[r[...] for r in refs[:n_in]])
        for ref, val in zip(refs[n_in:], outs, strict=True):
            ref[...] = val.astype(ref.dtype)

    return pl.pallas_call(
        body, name=name, out_shape=[jax.ShapeDtypeStruct(s, F32) for s in out_shapes],
        compiler_params=_cp(),
    )(*ins)


def _rms(x, g):
    return x * lax.rsqrt(jnp.mean(x * x, axis=-1, keepdims=True) + EPS) * g


def _rms_groups(y, g, width):
    outs = []
    for j in range(y.shape[1] // width):
        sl = slice(j * width, (j + 1) * width)
        outs.append(_rms(y[:, sl], g[:, sl]))
    return jnp.concatenate(outs, axis=1)


def norm_fwd(name, x, g, dils=(1,)):
    outs, _ = rowcall(name, lambda x, g: ((_rms(x, g),) * len(dils), ()), [x], [g],
                      [(D_MODEL, BF16, d) for d in dils], tb=512)
    return outs[0] if len(dils) == 1 else tuple(outs)


def norm_bwd(name, x, g, dhn, dres, dils=(1,)):
    parts = dhn if isinstance(dhn, tuple) else (dhn,)
    n = len(parts)

    def fn(x, *rest):
        dh = functools.reduce(lambda a, b: a + b, rest[:n])
        _, vjp = jax.vjp(_rms, x, rest[n + 1])
        dx, dg = vjp(dh)
        return (dx + rest[n],), (dg,)

    rows = [x] + [a if d == 1 else _perm(a, d) for a, d in zip(parts, dils)] + [dres]
    (dx,), (dg,) = rowcall(name, fn, rows, [g], [(D_MODEL, F32)], [(1, D_MODEL)], tb=512)
    return dx, dg


def loss_head(x, tgt, g):
    def fn(x, tgt, g):
        y, vjp = jax.vjp(_rms, x, g)
        diff = y - tgt
        loss = 0.5 * jnp.sum(jnp.mean(diff * diff, axis=-1, keepdims=True), axis=0, keepdims=True)
        dx, dg = vjp(diff * (1.0 / D_MODEL))
        return (dx,), (jnp.broadcast_to(loss, (1, LANE)), dg)

    (dx,), (loss, dg) = rowcall("loss_head", fn, [x, tgt], [g], [(D_MODEL, F32)],
                                [(1, LANE), (1, D_MODEL)], tb=512)
    return loss, dx, dg


def _shift_rows(x, s):
    if s == 0:
        return x
    return pltpu.roll(x, (-s) % x.shape[0], 0)


def _conv_ext(x, prev, nxt, w):
    xe = jnp.concatenate([prev, x, nxt], axis=0)
    pad = SSD_CONV // 2
    c = jnp.zeros_like(xe)
    for k in range(SSD_CONV):
        c = c + w[k:k + 1, :] * _shift_rows(xe, k - pad)
    return xe, c


def ssd_conv_fwd(u, conv_w, conv_b):
    def fn(x0, x1, x2, p0, n0, p1, n1, p2, n2, w, b):
        tb = x0.shape[0]
        outs = []
        for j, (x, p, n) in enumerate(((x0, p0, n0), (x1, p1, n1), (x2, p2, n2))):
            sl = slice(j * 1024, (j + 1) * 1024)
            _, c = _conv_ext(x, p, n, w[:, sl])
            outs.append(_silu(c[SUBLANE:SUBLANE + tb] + b[:, sl]))
        return (jnp.concatenate(outs, axis=1),), ()

    (xbc,), _ = rowcall("ssd_conv_fwd", fn, [_col(u, 1024, 2), _col(u, 1024, 3), _col(u, 1024, 4)],
                        [conv_w, conv_b], [(SSD_CONV_CH, F32)], tb=256, halo=(0, 1, 2))
    return xbc


def ssd_conv_bwd(u, dxbc, dz, conv_w, conv_b):
    pad = SSD_CONV // 2

    def fn(x0, x1, x2, g0, g1, g2, dz, xp0, xn0, xp1, xn1, xp2, xn2, gp0, gn0, gp1, gn1, gp2, gn2, w, b):
        tb = x0.shape[0]
        blk = slice(SUBLANE, SUBLANE + tb)
        dpre, dws, dbs = [], [], []
        xs = ((x0, xp0, xn0), (x1, xp1, xn1), (x2, xp2, xn2))
        gs = ((g0, gp0, gn0), (g1, gp1, gn1), (g2, gp2, gn2))
        for j in range(3):
            sl = slice(j * 1024, (j + 1) * 1024)
            wj = w[:, sl]
            xe, c = _conv_ext(*xs[j], wj)
            ce = c + b[:, sl]
            sig = jax.nn.sigmoid(ce)
            ge = jnp.concatenate([gs[j][1], gs[j][0], gs[j][2]], axis=0)
            dce = ge * (sig * (1.0 + ce * (1.0 - sig)))
            dx = jnp.zeros_like(xe)
            dw_rows = []
            for k in range(SSD_CONV):
                dx = dx + wj[k:k + 1, :] * _shift_rows(dce, pad - k)
                dw_rows.append(jnp.sum(dce[blk] * _shift_rows(xe, k - pad)[blk], axis=0, keepdims=True))
            dw_rows.append(jnp.zeros_like(dw_rows[0]))
            dpre.append(dx[blk])
            dws.append(jnp.concatenate(dw_rows, axis=0))
            dbs.append(jnp.sum(dce[blk], axis=0, keepdims=True))
        du = jnp.concatenate([dz] + dpre, axis=1)
        return (du,), (jnp.concatenate(dws, axis=1), jnp.concatenate(dbs, axis=1))

    rows = [_col(u, 1024, 2), _col(u, 1024, 3), _col(u, 1024, 4),
            _col(dxbc, 1024, 0), _col(dxbc, 1024, 1), _col(dxbc, 1024, 2), dz]
    (du,), (dw, db) = rowcall("ssd_conv_bwd", fn, rows, [conv_w, conv_b], [(SSD_MAIN, BF16)],
                              [(SUBLANE, SSD_CONV_CH), (1, SSD_CONV_CH)], tb=128, halo=(0, 1, 2, 3, 4, 5))
    return du, dw, db


def _expand_heads(v):
    return jnp.concatenate([jnp.broadcast_to(v[:, j:j + 1], (v.shape[0], SSD_HEADDIM)) for j in range(SSD_HPG)], axis=1)


def _ssd_chunk(rev, st_in, xs, udt, dtb, alog, B, C):
    Q = B.shape[0]
    P = SSD_HEADDIM
    dt = _softplus(udt + dtb)
    a = dt * (-jnp.exp(alog))
    r = lax.broadcasted_iota(jnp.int32, (Q, Q), 0)
    c = lax.broadcasted_iota(jnp.int32, (Q, Q), 1)
    mask = (r <= c) if rev else (r >= c)
    p = _mm_tri(mask, a)
    pT = p.T
    p_e = _expand_heads(p)
    tot_e = p_e[0:1] if rev else p_e[Q - 1:Q]
    xdt = xs * _expand_heads(dt)
    CB = _mm_nt(C, B)
    H = SSD_HPG
    p_cols = jnp.concatenate([jnp.broadcast_to(p[:, j:j + 1], (Q, Q)) for j in range(H)], axis=1)
    p_rows = jnp.concatenate([pT[j:j + 1, :] for j in range(H)], axis=1)
    decay = jnp.exp(jnp.where(jnp.concatenate([mask] * H, axis=1), p_cols - p_rows, NEG_BIG))
    col = lax.broadcasted_iota(jnp.int32, (1, H * P), 1)
    x_bd = jnp.concatenate([jnp.where((col >= j * P) & (col < (j + 1) * P), xdt, 0.0) for j in range(H)], axis=0)
    y = _mm(jnp.concatenate([CB] * H, axis=1) * decay, x_bd) + _mm(C, st_in) * jnp.exp(p_e)
    st_out = st_in * jnp.exp(tot_e) + _mm_tn(B, xdt * jnp.exp(tot_e - p_e))
    return y, st_out


def _ssd_specs(nc, rev_order):
    Q = SSD_CHUNK
    N, P, H, GS = SSD_STATE, SSD_HEADDIM, SSD_HPG, SSD_GPS
    gw = H * P
    nbc = SSD_GROUPS // GS

    def cidx(s):
        return nc - 1 - s if rev_order else s

    xs = pl.BlockSpec((Q, GS * gw), lambda g, s: (cidx(s), g))
    Bs = pl.BlockSpec((Q, GS * N), lambda g, s: (cidx(s), SSD_DI // (GS * N) + g))
    Cs = pl.BlockSpec((Q, GS * N), lambda g, s: (cidx(s), SSD_DI // (GS * N) + nbc + g))
    BC_out = pl.BlockSpec((Q, GS * N), lambda g, s: (cidx(s), g))
    udt = pl.BlockSpec((GS, Q, H), lambda g, s: (g, cidx(s), 0))
    small = pl.BlockSpec((GS, 1, H), lambda g, s: (g, 0, 0))
    st = pl.BlockSpec((GS, 1, N, gw), lambda g, s: (g, cidx(s), 0, 0))
    return xs, Bs, Cs, BC_out, udt, small, st


def ssd_scan_fwd(name, xbc, udt, dtb, alog, rev, comm=None):
    S = xbc.shape[0]
    Q, N, P, H, GS = SSD_CHUNK, SSD_STATE, SSD_HEADDIM, SSD_HPG, SSD_GPS
    gw = H * P
    nc = S // Q
    xs_s, B_s, C_s, _, udt_s, small_s, st_s = _ssd_specs(nc, rev)

    def body(xs_ref, B_ref, C_ref, udt_ref, dtb_ref, alog_ref, y_ref, st_ref, state):
        @pl.when(pl.program_id(1) == 0)
        def _():
            state[...] = jnp.zeros_like(state)

        for g in range(GS):
            st_ref[g, 0] = state[g]
            y, st_out = _ssd_chunk(rev, state[g], xs_ref[:, g * gw:(g + 1) * gw], udt_ref[g], dtb_ref[g], alog_ref[g],
                                   B_ref[:, g * N:(g + 1) * N], C_ref[:, g * N:(g + 1) * N])
            y_ref[:, g * gw:(g + 1) * gw] = y
            state[g] = st_out

    (y, st), got = call_with_comm(
        body, comm, name=name, grid=(SSD_GROUPS // GS, nc),
        in_specs=[xs_s, B_s, C_s, udt_s, small_s, small_s],
        out_specs=[xs_s, st_s],
        out_shape=[jax.ShapeDtypeStruct((S, SSD_DI), F32),
                   jax.ShapeDtypeStruct((SSD_GROUPS, nc, N, gw), F32)],
        scratch_shapes=[pltpu.VMEM((GS, N, gw), F32)],
        semantics=("parallel", "arbitrary"), args=(xbc, xbc, xbc, udt, dtb, alog))
    return y, st, got


def ssd_scan_bwd(name, xbc, udt, dtb, alog, states, dy, rev, comm=None):
    S = xbc.shape[0]
    Q, N, P, H, GS = SSD_CHUNK, SSD_STATE, SSD_HEADDIM, SSD_HPG, SSD_GPS
    gw = H * P
    nc = S // Q
    xs_s, B_s, C_s, BC_out, udt_s, small_s, st_s = _ssd_specs(nc, not rev)

    def body(xs_ref, B_ref, C_ref, udt_ref, dtb_ref, alog_ref, st_ref, dy_ref,
             dx_ref, dB_ref, dC_ref, dudt_ref, ddtb_ref, dalog_ref, dstate):
        @pl.when(pl.program_id(1) == 0)
        def _():
            dstate[...] = jnp.zeros_like(dstate)
            ddtb_ref[...] = jnp.zeros_like(ddtb_ref)
            dalog_ref[...] = jnp.zeros_like(dalog_ref)

        for g in range(GS):
            cols, bc = slice(g * gw, (g + 1) * gw), slice(g * N, (g + 1) * N)
            _, vjp = jax.vjp(functools.partial(_ssd_chunk, rev), st_ref[g, 0], xs_ref[:, cols], udt_ref[g], dtb_ref[g],
                             alog_ref[g], B_ref[:, bc], C_ref[:, bc])
            dst_in, dxs, dudt, ddtb, dalog, dB, dC = vjp((dy_ref[:, cols], dstate[g]))
            dx_ref[:, cols] = dxs
            dB_ref[:, bc] = dB
            dC_ref[:, bc] = dC
            dudt_ref[g] = dudt
            ddtb_ref[g] += ddtb
            dalog_ref[g] += dalog
            dstate[g] = dst_in

    outs, got = call_with_comm(
        body, comm, name=name, grid=(SSD_GROUPS // GS, nc),
        in_specs=[xs_s, B_s, C_s, udt_s, small_s, small_s, st_s, xs_s],
        out_specs=[xs_s, BC_out, BC_out, udt_s, small_s, small_s],
        out_shape=[jax.ShapeDtypeStruct((S, SSD_DI), F32),
                   jax.ShapeDtypeStruct((S, SSD_GROUPS * N), F32),
                   jax.ShapeDtypeStruct((S, SSD_GROUPS * N), F32),
                   jax.ShapeDtypeStruct((SSD_GROUPS, S, H), F32),
                   jax.ShapeDtypeStruct((SSD_GROUPS, 1, H), F32),
                   jax.ShapeDtypeStruct((SSD_GROUPS, 1, H), F32)],
        scratch_shapes=[pltpu.VMEM((GS, N, gw), F32)],
        semantics=("parallel", "arbitrary"), args=(xbc, xbc, xbc, udt, dtb, alog, states, dy))
    return (*outs, got)


def _ssd_combine(yf, yb, xs, z, dexp, ng):
    y = (yf + yb + xs * dexp) * _silu(z)
    return _rms_groups(y, ng, SSD_DI // SSD_GROUPS)


def ssd_forward(x, hn, w, comm=None):
    comm = comm or {}
    S = x.shape[0]
    u = matmul("ssd_in", hn, w["ssd_w_main"])
    udt = matmul("ssd_in_dt", hn, w["ssd_w_dt"])
    xbc = ssd_conv_fwd(u, w["ssd_conv_w8"], w["ssd_conv_b"])
    udt_t = udt.reshape(S, 2, SSD_GROUPS, SSD_HPG).transpose(1, 2, 0, 3)
    dtb = w["ssd_dt_bias"].reshape(2, SSD_GROUPS, 1, SSD_HPG)
    alog = w["ssd_a_log"].reshape(2, SSD_GROUPS, 1, SSD_HPG)
    yf, stf, got_f = ssd_scan_fwd("ssd_scan_f", xbc, udt_t[0], dtb[0], alog[0], False, comm.get("ssd_scan_f"))
    yb, stb, got_b = ssd_scan_fwd("ssd_scan_b", xbc, udt_t[1], dtb[1], alog[1], True, comm.get("ssd_scan_b"))
    dexp = jnp.repeat(w["ssd_d"].reshape(1, SSD_HEADS), SSD_HEADDIM, axis=1)
    (yn,), _ = rowcall("ssd_combine", lambda yf, yb, xs, z, d, g: ((_ssd_combine(yf, yb, xs, z, d, g),), ()),
                       [yf, yb, _col(xbc, SSD_DI, 0), _col(u, SSD_DI, 0)], [dexp, w["ssd_norm_g"]],
                       [(SSD_DI, BF16)], tb=256)
    if "late_weights" in comm:
        w.update(comm["late_weights"](dict(ssd_scan_f=got_f, ssd_scan_b=got_b)))
    out = matmul("ssd_out", yn, w["ssd_w_out"], res=x)
    saved = dict(hn=hn, u=u, xbc=xbc, udt_t=udt_t, dtb=dtb, alog=alog, yf=yf, yb=yb, stf=stf, stb=stb,
                 dexp=dexp, yn=yn, got=dict(ssd_scan_f=got_f, ssd_scan_b=got_b))
    return out, saved


def ssd_backward(dy, sv, w, comm=None):
    S = dy.shape[0]
    u, xbc = sv["u"], sv["xbc"]
    dyn = matmul("ssd_out_dx", dy, w["ssd_w_out"], mode="nt")
    g_w_out = matmul("ssd_out_dw", sv["yn"], dy, mode="tn")

    def comb_bwd(yf, yb, xs, z, dyn, dexp, ng):
        _, vjp = jax.vjp(_ssd_combine, yf, yb, xs, z, dexp, ng)
        dyf, _, dxs, dz, ddexp, dng = vjp(dyn)
        return (dyf, dxs, dz), (ddexp, dng)

    (dyc, dskip, dz), (ddexp, g_norm) = rowcall(
        "ssd_combine_bwd", comb_bwd, [sv["yf"], sv["yb"], _col(xbc, SSD_DI, 0), _col(u, SSD_DI, 0), dyn],
        [sv["dexp"], w["ssd_norm_g"]], [(SSD_DI, F32)] * 3, [(1, SSD_DI), (1, SSD_DI)], tb=256)
    udt_t, dtb, alog = sv["udt_t"], sv["dtb"], sv["alog"]
    comm = comm or {}
    dxf, dBf, dCf, dudt_f, ddtb_f, dalog_f, got_f = ssd_scan_bwd("ssd_scan_f_bwd", xbc, udt_t[0], dtb[0], alog[0],
                                                                 sv["stf"], dyc, False, comm.get("ssd_scan_f_bwd"))
    dxb, dBb, dCb, dudt_b, ddtb_b, dalog_b, _ = ssd_scan_bwd("ssd_scan_b_bwd", xbc, udt_t[1], dtb[1], alog[1],
                                                             sv["stb"], dyc, True)

    def gather(dxf, dxb, dskip, dBf, dBb, dCf, dCb):
        return (jnp.concatenate([dxf + dxb + dskip, dBf + dBb, dCf + dCb], axis=1),), ()

    (dxbc,), _ = rowcall("ssd_dxbc", gather, [dxf, dxb, dskip, dBf, dBb, dCf, dCb], [], [(SSD_CONV_CH, F32)], tb=256)
    du, g_conv_w8, g_conv_b = ssd_conv_bwd(u, dxbc, dz, w["ssd_conv_w8"], w["ssd_conv_b"])
    dudt = jnp.stack([dudt_f, dudt_b]).transpose(2, 0, 1, 3).reshape(S, 2 * SSD_HEADS)
    hn = sv["hn"]
    g_main = matmul("ssd_in_dw", hn, du, mode="tn")
    g_dt = matmul("ssd_in_dt_dw", hn, dudt, mode="tn")
    dhn = matmul("ssd_in_dt_dx", dudt, w["ssd_w_dt"], mode="nt")
    dhn = matmul("ssd_in_dx", du, w["ssd_w_main"], mode="nt", res=dhn)
    grads = dict(
        ssd_w_in=jnp.concatenate([g_main, g_dt], axis=1)[None],
        ssd_conv_w=g_conv_w8[None, :SSD_CONV],
        ssd_conv_b=g_conv_b,
        ssd_dt_bias=jnp.stack([ddtb_f, ddtb_b]).reshape(1, 2, SSD_HEADS),
        ssd_a_log=jnp.stack([dalog_f, dalog_b]).reshape(1, 2, SSD_HEADS),
        ssd_d_exp=ddexp,
        ssd_norm_g=g_norm,
        ssd_w_out=g_w_out[None],
        got=dict(ssd_scan_f_bwd=got_f),
    )
    return dhn, grads


def _hg_block(rev, stTs, uq, uf, ui, lb):
    C = HG_CHUNK
    n = uq.shape[0] // C
    nh = uq.shape[1] // HG_EXPAND
    stTs = list(stTs)
    q = _silu(uq)
    f = lb + (1.0 - lb) * jax.nn.sigmoid(uf)
    k = 1.0 - f
    g = jnp.log(f)
    r = lax.broadcasted_iota(jnp.int32, (C, C), 0)
    c = lax.broadcasted_iota(jnp.int32, (C, C), 1)
    mask = (r <= c) if rev else (r >= c)
    Tm = mask.astype(F32)
    outs = [[None] * n for _ in range(nh)]
    for i in (reversed(range(n)) if rev else range(n)):
        sl = slice(i * C, (i + 1) * C)
        qi, ki, vi = q[sl], k[sl], ui[sl]
        G = _mm_tri(mask, g[sl])
        Gr = G[C // 2:C // 2 + 1]
        Gl = G[0:1] if rev else G[C - 1:C]
        q_in, k_in = qi * jnp.exp(G - Gr), ki * jnp.exp(Gr - G)
        q_st, k_st, e_l = qi * jnp.exp(G), ki * jnp.exp(Gl - G), jnp.exp(Gl)
        for h in range(nh):
            cs = slice(h * HG_EXPAND, (h + 1) * HG_EXPAND)
            att = jnp.where(mask, _mm_nt(q_in[:, cs], k_in[:, cs]), 0.0)
            outs[h][i] = _mm(att, vi[:, cs]) + _mm_nt(q_st[:, cs], stTs[h])
            stTs[h] = stTs[h] * e_l[:, cs] + _mm_tn(vi[:, cs], k_st[:, cs])
    o = jnp.concatenate([jnp.concatenate(outs[h], axis=0) for h in range(nh)], axis=1)
    return o, stTs


def _hg_specs(nb, rev_order, f_col):
    R = HG_ROWS
    gw = HG_HPS * HG_EXPAND
    ng = HG_HEADS // HG_HPS

    def bidx(s):
        return nb - 1 - s if rev_order else s

    def col(base):
        return pl.BlockSpec((R, gw), lambda h, s: (bidx(s), base * ng + h))

    out = pl.BlockSpec((R, gw), lambda h, s: (bidx(s), h))
    lb = pl.BlockSpec((1, gw), lambda h, s: (0, h))
    st = pl.BlockSpec((1, 1, HG_HPS, HG_EXPAND, HG_EXPAND), lambda h, s: (h, bidx(s), 0, 0, 0))
    return col(0), col(f_col), col(3), out, lb, st


def hg_scan_fwd(name, u, lb, rev, comm=None):
    S = u.shape[0]
    nb = S // HG_ROWS
    ng = HG_HEADS // HG_HPS
    q_s, f_s, i_s, o_s, lb_s, st_s = _hg_specs(nb, rev, 2 if rev else 1)

    def body(uq, uf, ui, lb_ref, o_ref, st_ref, state):
        @pl.when(pl.program_id(1) == 0)
        def _():
            state[...] = jnp.zeros_like(state)

        st_ref[0, 0] = state[...]
        o, st = _hg_block(rev, [state[h] for h in range(HG_HPS)], uq[...], uf[...], ui[...], lb_ref[...])
        o_ref[...] = o
        for h in range(HG_HPS):
            state[h] = st[h]

    (o, st), got = call_with_comm(
        body, comm, name=name, grid=(ng, nb), in_specs=[q_s, f_s, i_s, lb_s], out_specs=[o_s, st_s],
        out_shape=[jax.ShapeDtypeStruct((S, HG_W), F32),
                   jax.ShapeDtypeStruct((ng, nb, HG_HPS, HG_EXPAND, HG_EXPAND), F32)],
        scratch_shapes=[pltpu.VMEM((HG_HPS, HG_EXPAND, HG_EXPAND), F32)],
        semantics=("parallel", "arbitrary"), args=(u, u, u, lb))
    return o, st, got


def hg_scan_bwd(name, u, lb, states, do, rev, comm=None):
    S = u.shape[0]
    nb = S // HG_ROWS
    ng = HG_HEADS // HG_HPS
    q_s, f_s, i_s, o_s, lb_s, st_s = _hg_specs(nb, not rev, 2 if rev else 1)

    def body(uq, uf, ui, lb_ref, st_ref, do_ref, dq_ref, df_ref, di_ref, dlb_ref, dstate):
        @pl.when(pl.program_id(1) == 0)
        def _():
            dstate[...] = jnp.zeros_like(dstate)
            dlb_ref[...] = jnp.zeros_like(dlb_ref)

        _, vjp = jax.vjp(functools.partial(_hg_block, rev), [st_ref[0, 0, h] for h in range(HG_HPS)],
                         uq[...], uf[...], ui[...], lb_ref[...])
        dst, dq, df, di, dlb = vjp((do_ref[...], [dstate[h] for h in range(HG_HPS)]))
        dq_ref[...] = dq
        df_ref[...] = df
        di_ref[...] = di
        dlb_ref[...] += dlb
        for h in range(HG_HPS):
            dstate[h] = dst[h]

    outs, got = call_with_comm(
        body, comm, name=name, grid=(ng, nb), in_specs=[q_s, f_s, i_s, lb_s, st_s, o_s],
        out_specs=[o_s, o_s, o_s, lb_s],
        out_shape=[jax.ShapeDtypeStruct((S, HG_W), F32)] * 3 + [jax.ShapeDtypeStruct((1, HG_W), F32)],
        scratch_shapes=[pltpu.VMEM((HG_HPS, HG_EXPAND, HG_EXPAND), F32)],
        semantics=("parallel", "arbitrary"), args=(u, u, u, lb, states, do))
    return (*outs, got)


def _hg_lb(hgrn_lb, layer):
    m = jnp.max(hgrn_lb, axis=0, keepdims=True)
    e = jnp.exp(hgrn_lb - m)
    sm = e / jnp.sum(e, axis=0, keepdims=True)
    lb = jnp.zeros_like(sm[0:1])
    for i in range(1, layer + 1):
        lb = lb + sm[i:i + 1]
    return lb


def _hg_combine(of, ob, gate, ng):
    return _rms_groups(of + ob, ng, HG_EXPAND) * _silu(gate)


def hg_forward(x, hn, w, layer, comm=None):
    comm = comm or {}
    u = matmul("hg_in", hn, w["hg_w_in"])
    (lb,) = smallcall("hg_lb", lambda t: (_hg_lb(t, layer),), [w["hgrn_lb"]], [(1, HG_W)])
    of, stf, got_f = hg_scan_fwd("hg_scan_f", u, lb, False, comm.get("hg_scan_f"))
    ob, stb, got_b = hg_scan_fwd("hg_scan_b", u, lb, True, comm.get("hg_scan_b"))
    (og,), _ = rowcall("hg_combine", lambda of, ob, gate, ng: ((_hg_combine(of, ob, gate, ng),), ()),
                       [of, ob, _col(u, HG_W, 4)], [w["hg_norm_g"]], [(HG_W, BF16)], tb=256)
    out = matmul("hg_out", og, w["hg_w_out"], res=x)
    return out, dict(hn=hn, u=u, lb=lb, of=of, ob=ob, stf=stf, stb=stb, og=og, got=dict(hg_scan_f=got_f, hg_scan_b=got_b))


def hg_backward(dy, sv, w, layer, comm=None):
    comm = comm or {}
    u, lb = sv["u"], sv["lb"]
    dog = matmul("hg_out_dx", dy, w["hg_w_out"], mode="nt")
    g_w_out = matmul("hg_out_dw", sv["og"], dy, mode="tn")

    def comb_bwd(of, ob, gate, dog, ng):
        _, vjp = jax.vjp(_hg_combine, of, ob, gate, ng)
        dof, _, dgate, dng = vjp(dog)
        return (dof, dgate), (dng,)

    (do, dgate), (g_norm,) = rowcall("hg_combine_bwd", comb_bwd, [sv["of"], sv["ob"], _col(u, HG_W, 4), dog],
                                     [w["hg_norm_g"]], [(HG_W, F32)] * 2, [(1, HG_W)], tb=256)
    dqf, dff, dif, dlbf, got_f = hg_scan_bwd("hg_scan_f_bwd", u, lb, sv["stf"], do, False, comm.get("hg_scan_f_bwd"))
    dqb, dfb, dib, dlbb, _ = hg_scan_bwd("hg_scan_b_bwd", u, lb, sv["stb"], do, True)

    def gather(dqf, dqb, dff, dfb, dif, dib, dgate):
        return (jnp.concatenate([dqf + dqb, dff, dfb, dif + dib, dgate], axis=1),), ()

    (du,), _ = rowcall("hg_du", gather, [dqf, dqb, dff, dfb, dif, dib, dgate], [], [(HG_IN, BF16)], tb=256)

    def lb_bwd(t, dlbf, dlbb):
        _, vjp = jax.vjp(lambda t: _hg_lb(t, layer), t)
        return vjp(dlbf + dlbb)

    (g_lb,) = smallcall("hg_lb_bwd", lb_bwd, [w["hgrn_lb"], dlbf, dlbb], [(DEPTH, HG_W)])
    hn = sv["hn"]
    g_w_in = matmul("hg_in_dw", hn, du, mode="tn")
    dhn = matmul("hg_in_dx", du, w["hg_w_in"], mode="nt")
    return dhn, dict(hg_w_in=g_w_in[None], hg_norm_g=g_norm, hg_w_out=g_w_out[None], hgrn_lb=g_lb,
                     got=dict(hg_scan_f_bwd=got_f))


def _rope_tables(S):
    t = np.arange(S)
    row = (t // GRID_W).astype(np.float32)
    col = (t % GRID_W).astype(np.float32)
    inv = (ROPE_THETA ** (-np.arange(0, ROPE_AXIS, 2, dtype=np.float32) / ROPE_AXIS)).astype(np.float32)
    ar = jnp.asarray(row)[:, None] * jnp.asarray(inv)[None, :]
    ac = jnp.asarray(col)[:, None] * jnp.asarray(inv)[None, :]
    cos = jnp.concatenate([jnp.cos(ar), jnp.cos(ar), jnp.cos(ac), jnp.cos(ac)], axis=1)
    sin = jnp.concatenate([-jnp.sin(ar), jnp.sin(ar), -jnp.sin(ac), jnp.sin(ac)], axis=1)
    return cos.astype(F32), sin.astype(F32)


@jax.custom_vjp
def _swap_halves_of_axes(x):
    h = ROPE_AXIS // 2
    lane = lax.broadcasted_iota(jnp.int32, x.shape, 1)
    return jnp.where((lane & h) == 0, pltpu.roll(x, AT_HD - h, 1), pltpu.roll(x, h, 1))


_swap_halves_of_axes.defvjp(lambda x: (_swap_halves_of_axes(x), None), lambda _, g: (_swap_halves_of_axes(g),))


def _rope(x, cos, sin):
    return x * cos + _swap_halves_of_axes(x) * sin


def _at_pre(uq, uk, cos, sin, qg, kg):
    qs, ks = [], []
    for h in range(AT_HEADS):
        qs.append(_rope(_rms(uq[:, h * AT_HD:(h + 1) * AT_HD], qg), cos, sin) * (AT_HD ** -0.5))
    for h in range(AT_KV):
        ks.append(_rope(_rms(uk[:, h * AT_HD:(h + 1) * AT_HD], kg), cos, sin))
    return jnp.concatenate(qs, axis=1), jnp.concatenate(ks, axis=1)


def _stack_heads(x):
    return jnp.concatenate([x[:, :AT_HD], x[:, AT_HD:]], axis=0)


def _unstack_heads(x):
    t = x.shape[0] // 2
    return jnp.concatenate([x[:t], x[t:]], axis=1)


def at_flash_fwd(q, k, u):
    S = q.shape[0]
    tq, tk = _pick(S, 512), _pick(S, 4096)
    nq, nk = S // tq, S // tk
    gw = AT_GRP * AT_HD

    def body(q_ref, k_ref, v_ref, o_ref, lse_ref, m_s, l_s, acc):
        j = pl.program_id(2)

        @pl.when(j == 0)
        def _():
            m_s[...] = jnp.full_like(m_s, NEG_BIG)
            l_s[...] = jnp.zeros_like(l_s)
            acc[...] = jnp.zeros_like(acc)

        s = _mm_nt(_stack_heads(q_ref[...]), k_ref[...])
        m_new = jnp.maximum(m_s[...], jnp.max(s, axis=-1, keepdims=True))
        alpha = jnp.exp(m_s[...] - m_new)
        p = jnp.exp(s - m_new)
        l_s[...] = alpha * l_s[...] + jnp.sum(p, axis=-1, keepdims=True)
        acc[...] = alpha * acc[...] + _mm(p, v_ref[...])
        m_s[...] = m_new

        @pl.when(j == nk - 1)
        def _():
            o_ref[...] = _unstack_heads(acc[...] / l_s[...])
            lse = m_s[...] + jnp.log(l_s[...])
            lse_ref[0, 0] = lse[:tq]
            lse_ref[0, 1] = lse[tq:]

    return pl.pallas_call(
        body, name="at_flash_fwd", grid=(AT_KV, nq, nk),
        in_specs=[pl.BlockSpec((tq, gw), lambda h, i, j: (i, h)),
                  pl.BlockSpec((tk, AT_HD), lambda h, i, j: (j, h)),
                  pl.BlockSpec((tk, AT_HD), lambda h, i, j: (j, (AT_QW + AT_KW) // AT_HD + h))],
        out_specs=[pl.BlockSpec((tq, gw), lambda h, i, j: (i, h)),
                   pl.BlockSpec((1, AT_GRP, tq, 1), lambda h, i, j: (h, 0, i, 0))],
        out_shape=[jax.ShapeDtypeStruct((S, AT_QW), F32), jax.ShapeDtypeStruct((AT_KV, AT_GRP, S, 1), F32)],
        scratch_shapes=[pltpu.VMEM((2 * tq, 1), F32), pltpu.VMEM((2 * tq, 1), F32), pltpu.VMEM((2 * tq, AT_HD), F32)],
        compiler_params=_cp(("parallel", "parallel", "arbitrary")),
    )(q, k, u)


def at_flash_bwd(q, k, u, o, lse, do):
    S = q.shape[0]
    tq, tk = _pick(S, 128), _pick(S, 4096)
    nq, nk = S // tq, S // tk
    gw = AT_GRP * AT_HD

    def body(q_ref, k_ref, v_ref, o_ref, lse_ref, do_ref, dq_ref, dk_ref, dv_ref, dk_acc, dv_acc):
        j, i = pl.program_id(1), pl.program_id(2)

        @pl.when(i == 0)
        def _():
            dk_acc[...] = jnp.zeros_like(dk_acc)
            dv_acc[...] = jnp.zeros_like(dv_acc)

        q2 = _stack_heads(q_ref[...])
        do_blk = do_ref[...]
        do2 = _stack_heads(do_blk)
        delta = _stack_heads(do_blk * o_ref[...])
        delta = jnp.sum(delta, axis=-1, keepdims=True)
        kb, vb = k_ref[...], v_ref[...]
        p = jnp.exp(_mm_nt(q2, kb) - jnp.concatenate([lse_ref[0, 0], lse_ref[0, 1]], axis=0))
        dv_acc[...] += _mm_tn(p, do2)
        ds = p * (_mm_nt(do2, vb) - delta)
        dk_acc[...] += _mm_tn(ds, q2)
        dq = _unstack_heads(_mm(ds, kb))
        rows = pl.ds(pl.multiple_of(i * tq, tq), tq)

        @pl.when(j == 0)
        def _():
            dq_ref[rows, :] = dq

        @pl.when(j > 0)
        def _():
            dq_ref[rows, :] += dq

        @pl.when(i == nq - 1)
        def _():
            dk_ref[...] = dk_acc[...]
            dv_ref[...] = dv_acc[...]

    return pl.pallas_call(
        body, name="at_flash_bwd", grid=(AT_KV, nk, nq),
        in_specs=[pl.BlockSpec((tq, gw), lambda h, j, i: (i, h)),
                  pl.BlockSpec((tk, AT_HD), lambda h, j, i: (j, h)),
                  pl.BlockSpec((tk, AT_HD), lambda h, j, i: (j, (AT_QW + AT_KW) // AT_HD + h)),
                  pl.BlockSpec((tq, gw), lambda h, j, i: (i, h)),
                  pl.BlockSpec((1, AT_GRP, tq, 1), lambda h, j, i: (h, 0, i, 0)),
                  pl.BlockSpec((tq, gw), lambda h, j, i: (i, h))],
        out_specs=[pl.BlockSpec((S, gw), lambda h, j, i: (0, h)),
                   pl.BlockSpec((tk, AT_HD), lambda h, j, i: (j, h)),
                   pl.BlockSpec((tk, AT_HD), lambda h, j, i: (j, h))],
        out_shape=[jax.ShapeDtypeStruct((S, AT_QW), F32), jax.ShapeDtypeStruct((S, AT_KW), F32),
                   jax.ShapeDtypeStruct((S, AT_KW), F32)],
        scratch_shapes=[pltpu.VMEM((tk, AT_HD), F32), pltpu.VMEM((tk, AT_HD), F32)],
        compiler_params=_cp(("parallel", "arbitrary", "arbitrary")),
    )(q, k, u, o, lse, do)


def at_forward(x, hn, w, comm=None):
    S = x.shape[0]
    u = matmul("at_in", hn, w["at_w_in"])
    cos, sin = _rope_tables(S)
    (q, k), _ = rowcall("at_pre", lambda uq, uk, c, s, qg, kg: (_at_pre(uq, uk, c, s, qg, kg), ()),
                        [_col(u, AT_QW, 0), _col(u, AT_KW, 2), cos, sin], [w["at_q_norm_g"], w["at_k_norm_g"]],
                        [(AT_QW, BF16), (AT_KW, BF16)], tb=256)
    o, lse = at_flash_fwd(q, k, u)
    (og,), _ = rowcall("at_gate", lambda o, gate: ((o * _silu(gate),), ()), [o, _col(u, AT_QW, 2)], [],
                       [(AT_QW, BF16)], tb=256)
    out = matmul("at_out", og, w["at_w_out"], res=x)
    return out, dict(hn=hn, u=u, cos=cos, sin=sin, q=q, k=k, o=o, lse=lse, og=og)


def at_backward(dy, sv, w, comm=None):
    u = sv["u"]
    dog = matmul("at_out_dx", dy, w["at_w_out"], mode="nt")
    g_w_out = matmul("at_out_dw", sv["og"], dy, mode="tn")

    def gate_bwd(o, gate, dog):
        _, vjp = jax.vjp(lambda o, gate: o * _silu(gate), o, gate)
        return vjp(dog), ()

    (do, dgate), _ = rowcall("at_gate_bwd", gate_bwd, [sv["o"], _col(u, AT_QW, 2), dog], [],
                             [(AT_QW, F32)] * 2, tb=256)
    dq, dk, dv = at_flash_bwd(sv["q"], sv["k"], u, sv["o"], sv["lse"], do)

    def pre_bwd(uq, uk, cos, sin, dq, dk, dv, dgate, qg, kg):
        _, vjp = jax.vjp(lambda uq, uk, qg, kg: _at_pre(uq, uk, cos, sin, qg, kg), uq, uk, qg, kg)
        duq, duk, dqg, dkg = vjp((dq, dk))
        return (jnp.concatenate([duq, duk, dv, dgate], axis=1),), (dqg, dkg)

    (du,), (g_qg, g_kg) = rowcall(
        "at_pre_bwd", pre_bwd, [_col(u, AT_QW, 0), _col(u, AT_KW, 2), sv["cos"], sv["sin"], dq, dk, dv, dgate],
        [w["at_q_norm_g"], w["at_k_norm_g"]], [(AT_IN, BF16)], [(1, AT_HD), (1, AT_HD)], tb=128)
    hn = sv["hn"]
    g_w_in = matmul("at_in_dw", hn, du, mode="tn")
    dhn = matmul("at_in_dx", du, w["at_w_in"], mode="nt")
    return dhn, dict(at_w_in=g_w_in[None], at_q_norm_g=g_qg, at_k_norm_g=g_kg, at_w_out=g_w_out[None])


def _t5_bucket_np(rel):
    half = REL_BUCKETS // 2
    exact = half // 2
    n = np.abs(rel)
    large = exact + (np.log(np.maximum(n, 1).astype(np.float32) / exact)
                     / math.log(REL_MAX_DIST / exact) * (half - exact)).astype(np.int32)
    large = np.minimum(large, half - 1)
    return np.where(rel > 0, half, 0) + np.where(n < exact, n, large)


def _dl_tq(S, dil):
    return min(128, S // dil)


def _dl_bias_maps(tq, dil):
    W = tq + 2 * DL_STEPS
    i = np.arange(tq)[:, None]
    wdx = np.arange(W)[None, :]
    dm = wdx - DL_STEPS - i
    bucket = _t5_bucket_np(dm * dil).reshape(-1).astype(np.int32)
    band = np.where(np.abs(dm) <= DL_STEPS, 0.0, NEG_BIG).reshape(1, -1).astype(np.float32)
    onehot = (jnp.asarray(bucket)[None, :] == jnp.arange(REL_BUCKETS, dtype=jnp.int32)[:, None]).astype(F32)
    return onehot, jnp.asarray(band)


def _dl_attend(q, kwin, vwin, T, valid):
    tq = q.shape[0]
    os, ls = [], []
    for h in range(DL_HEADS):
        sl = slice(h * DL_HD, (h + 1) * DL_HD)
        s = _mm_nt(q[:, sl] * (DL_HD ** -0.5), kwin[:, sl]) + T[h]
        s = jnp.where(valid, s, NEG_BIG)
        m = lax.stop_gradient(jnp.max(s, axis=-1, keepdims=True))
        e = jnp.exp(s - m)
        den = jnp.sum(e, axis=-1, keepdims=True)
        lse = m + jnp.log(den)
        p = e * (1.0 / den)
        os.append(_mm(p, vwin[:, sl]))
        ls.append(jnp.broadcast_to(lse, (tq, DL_HD)))
    return jnp.concatenate(os, axis=1), jnp.concatenate(ls, axis=1)


def _dl_specs(tq, Ls):
    nb = Ls // tq
    hs = DL_STEPS
    per = tq // hs
    nh = Ls // hs

    def main(c):
        return pl.BlockSpec((tq, DL_W), lambda r, i: (r * nb + i, c))

    def prev(c):
        return pl.BlockSpec((hs, DL_W), lambda r, i: (r * nh + jnp.maximum(i * per - 1, 0), c))

    def nxt(c):
        return pl.BlockSpec((hs, DL_W), lambda r, i: (r * nh + jnp.minimum((i + 1) * per, nh - 1), c))

    return nb, main, prev, nxt


def _dl_valid(i, tq, Ls):
    W = tq + 2 * DL_STEPS
    mk = i * tq - DL_STEPS + lax.broadcasted_iota(jnp.int32, (1, W), 1)
    return (mk >= 0) & (mk < Ls)


def dl_attn_fwd(gi, dil, u, T):
    S = u.shape[0]
    Ls = S // dil
    tq = _dl_tq(S, dil)
    nb, main, prev, nxt = _dl_specs(tq, Ls)
    out = main(0)

    def body(q_ref, kp, kc, kn, vp, vc, vn, T_ref, o_ref, l_ref):
        kwin = jnp.concatenate([kp[...], kc[...], kn[...]], axis=0)
        vwin = jnp.concatenate([vp[...], vc[...], vn[...]], axis=0)
        o, l = _dl_attend(q_ref[...], kwin, vwin, T_ref[...], _dl_valid(pl.program_id(1), tq, Ls))
        o_ref[...] = o
        l_ref[...] = l

    o, l = pl.pallas_call(
        body, name=f"dl_attn_fwd{gi}", grid=(dil, nb),
        in_specs=[main(0), prev(1), main(1), nxt(1), prev(2), main(2), nxt(2),
                  pl.BlockSpec(T.shape, _const_map(3))],
        out_specs=[out, out],
        out_shape=[jax.ShapeDtypeStruct((S, DL_W), F32)] * 2,
        compiler_params=_cp(("parallel", "parallel")),
    )(u, u, u, u, u, u, u, T)
    return o, l


def dl_attn_bwd(gi, dil, u, T, do, dl, dgate=None):
    S = u.shape[0]
    Ls = S // dil
    tq = _dl_tq(S, dil)
    hs = DL_STEPS
    W = tq + 2 * hs
    nb, main, prev, nxt = _dl_specs(tq, Ls)
    out = main(0)
    win = pl.BlockSpec((1, W, DL_W), lambda r, i: (r * nb + i, 0, 0))

    def body(q_ref, kp, kc, kn, vp, vc, vn, T_ref, do_ref, dl_ref, dq_ref, dkw_ref, dvw_ref, dT_ref):
        first = (pl.program_id(0) == 0) & (pl.program_id(1) == 0)

        @pl.when(first)
        def _():
            dT_ref[...] = jnp.zeros_like(dT_ref)

        kwin = jnp.concatenate([kp[...], kc[...], kn[...]], axis=0)
        vwin = jnp.concatenate([vp[...], vc[...], vn[...]], axis=0)
        valid = _dl_valid(pl.program_id(1), tq, Ls)
        _, vjp = jax.vjp(lambda q, k, v, T: _dl_attend(q, k, v, T, valid), q_ref[...], kwin, vwin, T_ref[...])
        dq, dkw, dvw, dT = vjp((do_ref[...], dl_ref[...]))
        dq_ref[...] = dq
        dkw_ref[0] = dkw
        dvw_ref[0] = dvw
        dT_ref[...] += dT

    dq, dkw, dvw, dT = pl.pallas_call(
        body, name=f"dl_attn_bwd{gi}", grid=(dil, nb),
        in_specs=[main(0), prev(1), main(1), nxt(1), prev(2), main(2), nxt(2),
                  pl.BlockSpec(T.shape, _const_map(3)), out, out],
        out_specs=[out, win, win, pl.BlockSpec(T.shape, _const_map(3))],
        out_shape=[jax.ShapeDtypeStruct((S, DL_W), F32),
                   jax.ShapeDtypeStruct((dil * nb, W, DL_W), F32),
                   jax.ShapeDtypeStruct((dil * nb, W, DL_W), F32),
                   jax.ShapeDtypeStruct(T.shape, F32)],
        compiler_params=_cp(("arbitrary", "arbitrary")),
    )(u, u, u, u, u, u, u, T, do, dl)

    per = tq // hs
    n_out = 3 if dgate is None else 4

    def fold(*refs):
        dq_ref, kc, kp, kn, vc, vp, vn = refs[:7]
        du_ref = refs[-1]
        i = pl.program_id(1)
        has_p = (i > 0).astype(F32)
        has_n = (i < nb - 1).astype(F32)
        du_ref[:, 0:DL_W] = dq_ref[...].astype(BF16)
        for c, (c_ref, p_ref, n_ref) in enumerate(((kc, kp, kn), (vc, vp, vn)), start=1):
            mid = c_ref[0, hs:hs + tq, :]
            top = mid[0:hs] + p_ref[0] * has_p
            bot = mid[tq - hs:tq] + n_ref[0] * has_n
            parts = [top, bot] if tq == 2 * hs else ([top, mid[hs:tq - hs], bot] if tq > 2 * hs else [top + n_ref[0] * has_n])
            du_ref[:, c * DL_W:(c + 1) * DL_W] = jnp.concatenate(parts, axis=0).astype(BF16)
        if dgate is not None:
            du_ref[:, 3 * DL_W:4 * DL_W] = refs[7][...].astype(BF16)

    wfull = pl.BlockSpec((1, W, DL_W), lambda r, i: (r * nb + i, 0, 0))
    wprev = pl.BlockSpec((1, hs, DL_W), lambda r, i: (r * nb + jnp.maximum(i - 1, 0), per + 1, 0))
    wnext = pl.BlockSpec((1, hs, DL_W), lambda r, i: (r * nb + jnp.minimum(i + 1, nb - 1), 0, 0))
    extra_specs, extra_args = ([], []) if dgate is None else ([out], [dgate])
    du = pl.pallas_call(
        fold, name=f"dl_fold{gi}", grid=(dil, nb),
        in_specs=[out, wfull, wprev, wnext, wfull, wprev, wnext] + extra_specs,
        out_specs=pl.BlockSpec((tq, n_out * DL_W), lambda r, i: (r * nb + i, 0)),
        out_shape=jax.ShapeDtypeStruct((S, n_out * DL_W), BF16),
        compiler_params=_cp(("parallel", "parallel")),
    )(dq, dkw, dkw, dkw, dvw, dvw, dvw, *extra_args)
    return du, dT


def _dl_merge(o0, o1, o2, l0, l1, l2, gate):
    m = jnp.maximum(jnp.maximum(l0, l1), l2)
    e0, e1, e2 = jnp.exp(l0 - m), jnp.exp(l1 - m), jnp.exp(l2 - m)
    den = e0 + e1 + e2
    return ((e0 * o0 + e1 * o1 + e2 * o2) / den) * _silu(gate)


DL_DILS = tuple(d for _, d in DL_PAIRS)


def _dl_group_weights(w_in):
    g3 = 3 * DL_W
    return [jnp.concatenate([w_in[:, :g3], w_in[:, 3 * g3:]], axis=1), w_in[:, g3:2 * g3], w_in[:, 2 * g3:3 * g3]]


def dl_forward(x, hns, w, comm=None):
    S = x.shape[0]
    wg = _dl_group_weights(w["dl_w_in"])
    rbT = w["rel_bias"].T
    us, os, ls, Ts, maps = [], [], [], [], []
    for gi, dil in enumerate(DL_DILS):
        u = matmul(f"dl_in{gi}", hns[gi], wg[gi])
        tq = _dl_tq(S, dil)
        W = tq + 2 * DL_STEPS
        onehot, band = _dl_bias_maps(tq, dil)
        (T,) = smallcall(f"dl_bias{gi}", lambda rbT, oh, band: (_mm_exact(rbT, oh) + band,), [rbT, onehot, band],
                         [(DL_HEADS, tq * W)])
        T = T.reshape(DL_HEADS, tq, W)
        o, l = dl_attn_fwd(gi, dil, u, T)
        us.append(u)
        os.append(o)
        ls.append(l)
        Ts.append(T)
        maps.append(onehot)
    rows = [a if d == 1 else _perm(a, d) for a, d in zip(os + ls, DL_DILS * 2)] + [_col(us[0], DL_W, 3)]
    (og,), _ = rowcall("dl_merge", lambda *a: ((_dl_merge(*a),), ()), rows, [], [(DL_W, BF16)], tb=256)
    out = matmul("dl_out", og, w["dl_w_out"], res=x)
    return out, dict(hns=hns, us=us, os=os, ls=ls, Ts=Ts, maps=maps, og=og, wg=wg)


def dl_backward(dy, sv, w, comm=None):
    us = sv["us"]
    dog = matmul("dl_out_dx", dy, w["dl_w_out"], mode="nt")
    g_w_out = matmul("dl_out_dw", sv["og"], dy, mode="tn")

    def merge_bwd(o0, o1, o2, l0, l1, l2, gate, dog):
        _, vjp = jax.vjp(_dl_merge, o0, o1, o2, l0, l1, l2, gate)
        return vjp(dog), ()

    rows = [a if d == 1 else _perm(a, d) for a, d in zip(sv["os"] + sv["ls"], DL_DILS * 2)] + [_col(us[0], DL_W, 3), dog]
    grads7, _ = rowcall("dl_merge_bwd", merge_bwd, rows, [], [(DL_W, F32, d) for d in DL_DILS * 2] + [(DL_W, F32)], tb=256)
    dos, dls, dgate = grads7[0:3], grads7[3:6], grads7[6]
    g_rbT, g_ws, dhns = None, [], []
    for gi, dil in enumerate(DL_DILS):
        du, dT = dl_attn_bwd(gi, dil, us[gi], sv["Ts"][gi], dos[gi], dls[gi], dgate if gi == 0 else None)
        (g,) = smallcall(f"dl_bias_bwd{gi}", lambda dT, oh: (_mm_nt_exact(dT, oh),),
                         [dT.reshape(DL_HEADS, -1), sv["maps"][gi]], [(DL_HEADS, REL_BUCKETS)])
        g_rbT = g if g_rbT is None else g_rbT + g
        g_ws.append(matmul(f"dl_in_dw{gi}", sv["hns"][gi], du, mode="tn"))
        dhns.append(matmul(f"dl_in_dx{gi}", du, sv["wg"][gi], mode="nt"))
    g3 = 3 * DL_W
    g_w_in = jnp.concatenate([g_ws[0][:, :g3], g_ws[1], g_ws[2], g_ws[0][:, g3:]], axis=1)
    return tuple(dhns), dict(dl_w_in=g_w_in[None], dl_w_out=g_w_out[None], rel_bias=g_rbT.T)


_FWD = (ssd_forward, hg_forward, at_forward, dl_forward)
_BWD = (ssd_backward, hg_backward, at_backward, dl_backward)


def _norm_dils(layer):
    return DL_DILS if layer % 4 == 3 else (1,)


class NoExchange:
    def fwd_plans(self, layer, w):
        return None

    def fwd_done(self, layer, got, w):
        pass

    def bwd_plans(self, layer, grads):
        return None

    def bwd_done(self, layer, got):
        pass


def local_step(x, tgt, w, sched=None):
    sched = sched or NoExchange()
    saved = []
    h = x
    for layer in range(DEPTH):
        hn = norm_fwd(f"norm{layer}", h, w["norm_g"][layer:layer + 1], _norm_dils(layer))
        extra = (layer,) if layer % 4 == 1 else ()
        h_next, sv = _FWD[layer % 4](h, hn, w, *extra, comm=sched.fwd_plans(layer, w))
        sched.fwd_done(layer, sv.get("got", {}), w)
        saved.append((h, sv))
        h = h_next
    loss, dh, g_final = loss_head(h, tgt, w["final_g"].reshape(1, D_MODEL))
    grads = {}
    g_norm = [None] * DEPTH
    for layer in reversed(range(DEPTH)):
        h_in, sv = saved[layer]
        extra = (layer,) if layer % 4 == 1 else ()
        dhn, g = _BWD[layer % 4](dh, sv, w, *extra, comm=sched.bwd_plans(layer, grads))
        sched.bwd_done(layer, g.pop("got", {}))
        grads.update(g)
        dh, g_norm[layer] = norm_bwd(f"norm{layer}_bwd", h_in, w["norm_g"][layer:layer + 1], dhn, dh, _norm_dils(layer))
    grads["norm_g"] = jnp.concatenate(g_norm, axis=0)
    grads["final_g"] = g_final.reshape(D_MODEL)
    grads["ssd_d"] = jnp.sum(grads.pop("ssd_d_exp").reshape(SSD_HEADS, SSD_HEADDIM), axis=1)[None]
    return loss, dh, grads


IN_NAMES = ("ssd_w_in", "hg_w_in", "at_w_in", "dl_w_in")
OUT_NAMES = ("ssd_w_out", "hg_w_out", "at_w_out", "dl_w_out")
IN_COLS = (SSD_IN // 4, HG_IN // 4, AT_IN // 4, DL_IN // 4)
OUT_ROWS = (SSD_DI // 4, HG_W // 4, AT_QW // 4, DL_W // 4)
PACK_IN = sum(IN_COLS)
PACK_OUT = sum(OUT_ROWS)
N_CHIPS = 4
N_DEV = 8
HBM = pl.BlockSpec(memory_space=pl.ANY)


def _mesh_pos():
    return lax.axis_index("x"), lax.axis_index("y"), lax.axis_index("c")


def _other_chips(x, y):
    return [(1 - x, y), (x, 1 - y), (1 - x, 1 - y)]


def _half_rows(half, n):
    return pl.ds(pl.multiple_of(half * n, n), n)


def _remote(src, dst, send, recv, k, to):
    return pltpu.make_async_remote_copy(src_ref=src, dst_ref=dst, send_sem=send.at[k], recv_sem=recv.at[k],
                                        device_id=to, device_id_type=MESH)


def gather_plan(packs, whole=()):
    arrs = list(packs) + list(whole)
    n_half = len(packs)

    def pieces(ins, outs):
        x, y, c = _mesh_pos()
        for a, (src, dst) in enumerate(zip(ins, outs)):
            h = src.shape[0] // 2 if a < n_half else None
            for j, (px, py) in enumerate(_other_chips(x, y)):
                yield a, j, src, dst, h, (x, y, c), (px, py)

    def start(ins, outs, send, recv):
        for a, j, src, dst, h, (x, y, c), (px, py) in pieces(ins, outs):
            me = 2 * x + y
            if h is None:
                _remote(src, dst.at[me], send, recv, 6 * a + j, (px, py, c)).start()
            else:
                _remote(src.at[_half_rows(c, h)], dst.at[me, _half_rows(c, h)], send, recv, 6 * a + j, (px, py, c)).start()

    def mid(ins, outs, send, recv):
        for a, j, src, dst, h, (x, y, c), (px, py) in pieces(ins, outs):
            kp = 2 * px + py
            if h is None:
                _remote(src, dst.at[kp], send, recv, 6 * a + j, (px, py, c)).wait_recv()
            else:
                got = dst.at[kp, _half_rows(c, h)]
                _remote(src.at[_half_rows(c, h)], got, send, recv, 6 * a + j, (px, py, c)).wait_recv()
                _remote(got, got, send, recv, 6 * a + 3 + j, (x, y, 1 - c)).start()

    def finish(ins, outs, send, recv):
        for a, j, src, dst, h, (x, y, c), (px, py) in pieces(ins, outs):
            me, kp = 2 * x + y, 2 * px + py
            if h is None:
                _remote(src, dst.at[me], send, recv, 6 * a + j, (px, py, c)).wait_send()
            else:
                theirs = dst.at[kp, _half_rows(1 - c, h)]
                _remote(theirs, theirs, send, recv, 6 * a + 3 + j, (x, y, 1 - c)).wait_recv()
                _remote(src.at[_half_rows(c, h)], dst.at[me, _half_rows(c, h)], send, recv, 6 * a + j, (px, py, c)).wait_send()
                mine = dst.at[kp, _half_rows(c, h)]
                _remote(mine, mine, send, recv, 6 * a + 3 + j, (x, y, 1 - c)).wait_send()

    return dict(ins=arrs, out_shape=[jax.ShapeDtypeStruct((N_CHIPS,) + a.shape, a.dtype) for a in arrs],
                n_sems=6 * len(arrs), start=start, mid=mid, finish=finish)


def scatter_plan(halves):
    def copies(ins, outs, send, recv):
        x, y, c = _mesh_pos()
        for a, (src, dst) in enumerate(zip(ins, outs)):
            for j, (px, py) in enumerate(_other_chips(x, y)):
                yield _remote(src.at[2 * px + py], dst.at[j], send, recv, 3 * a + j, (px, py, c))

    def start(ins, outs, send, recv):
        for cp in copies(ins, outs, send, recv):
            cp.start()

    def finish(ins, outs, send, recv):
        for cp in copies(ins, outs, send, recv):
            cp.wait()

    return dict(ins=list(halves), out_shape=[jax.ShapeDtypeStruct((3,) + a.shape[1:], a.dtype) for a in halves],
                n_sems=3 * len(halves), start=start, mid=None, finish=finish)


def run_exchange(name, plan):
    n_in = len(plan["ins"])

    def body(*refs):
        ins, outs = refs[:n_in], refs[n_in:-2]
        send, recv = refs[-2], refs[-1]
        plan["start"](ins, outs, send, recv)
        if plan["mid"] is not None:
            plan["mid"](ins, outs, send, recv)
        plan["finish"](ins, outs, send, recv)

    return pl.pallas_call(
        body, name=name, in_specs=[HBM] * n_in, out_specs=[HBM] * len(plan["out_shape"]), out_shape=plan["out_shape"],
        scratch_shapes=[pltpu.SemaphoreType.DMA((plan["n_sems"],))] * 2,
        compiler_params=pltpu.CompilerParams(has_side_effects=True),
    )(*plan["ins"])


def swap_halves(name, g_in, g_out):
    h_in, h_out = g_in.shape[1] // 2, g_out.shape[1] // 2

    def body(gi, go, ri, ro, send, recv):
        x, y, c = _mesh_pos()
        sib = (x, y, 1 - c)

        def rows(half, n):
            return pl.ds(pl.multiple_of(half * n, n), n)

        cps = [pltpu.make_async_remote_copy(src_ref=gi.at[:, rows(1 - c, h_in)], dst_ref=ri, send_sem=send.at[0],
                                            recv_sem=recv.at[0], device_id=sib, device_id_type=MESH),
               pltpu.make_async_remote_copy(src_ref=go.at[:, rows(1 - c, h_out)], dst_ref=ro, send_sem=send.at[1],
                                            recv_sem=recv.at[1], device_id=sib, device_id_type=MESH)]
        for cp in cps:
            cp.start()
        for cp in cps:
            cp.wait()

    return pl.pallas_call(
        body, name=name, in_specs=[HBM, HBM], out_specs=[HBM, HBM],
        out_shape=[jax.ShapeDtypeStruct((N_CHIPS, h_in, g_in.shape[2]), g_in.dtype),
                   jax.ShapeDtypeStruct((N_CHIPS, h_out, g_out.shape[2]), g_out.dtype)],
        scratch_shapes=[pltpu.SemaphoreType.DMA((2,)), pltpu.SemaphoreType.DMA((2,))],
        compiler_params=pltpu.CompilerParams(has_side_effects=True),
    )(g_in, g_out)


def half_add(name, g, r, c_idx, tb):
    _, rows2, C = g.shape
    h = rows2 // 2
    nb = h // tb

    def body(c_ref, g_ref, r_ref, f_ref, b_ref):
        s = g_ref[...] + r_ref[...]
        f_ref[...] = s
        b_ref[...] = s.astype(BF16)

    grid_spec = pltpu.PrefetchScalarGridSpec(
        num_scalar_prefetch=1, grid=(N_CHIPS, nb),
        in_specs=[pl.BlockSpec((1, tb, C), lambda k, i, c: (k, c[0] * nb + i, 0)),
                  pl.BlockSpec((1, tb, C), lambda k, i, c: (k, i, 0))],
        out_specs=[pl.BlockSpec((1, tb, C), lambda k, i, c: (k, i, 0))] * 2)
    return pl.pallas_call(
        body, name=name, grid_spec=grid_spec,
        out_shape=[jax.ShapeDtypeStruct((N_CHIPS, h, C), F32), jax.ShapeDtypeStruct((N_CHIPS, h, C), BF16)],
        compiler_params=_cp(("parallel", "parallel")),
    )(c_idx, g, r)


def chip_sum(name, f, r, me_idx, tb):
    _, h, C = f.shape
    nb = h // tb

    def body(me_ref, f_ref, r0, r1, r2, o_ref):
        o_ref[...] = ((f_ref[0] + r0[0].astype(F32)) + r1[0].astype(F32)) + r2[0].astype(F32)

    def slot(j):
        return pl.BlockSpec((1, tb, C), lambda i, me: (j, i, 0))

    grid_spec = pltpu.PrefetchScalarGridSpec(
        num_scalar_prefetch=1, grid=(nb,),
        in_specs=[pl.BlockSpec((1, tb, C), lambda i, me: (me[0], i, 0)), slot(0), slot(1), slot(2)],
        out_specs=pl.BlockSpec((tb, C), lambda i, me: (i, 0)))
    return pl.pallas_call(
        body, name=name, grid_spec=grid_spec, out_shape=jax.ShapeDtypeStruct((h, C), F32),
        compiler_params=_cp(("parallel",)),
    )(me_idx, f, r, r, r)


def share_halves(name, f_in, f_out):
    def body(fi, fo, oi, oo, send, recv):
        x, y, c = _mesh_pos()
        sib = (x, y, 1 - c)
        cps = [pltpu.make_async_remote_copy(src_ref=fi, dst_ref=oi, send_sem=send.at[0], recv_sem=recv.at[0],
                                            device_id=sib, device_id_type=MESH),
               pltpu.make_async_remote_copy(src_ref=fo, dst_ref=oo, send_sem=send.at[1], recv_sem=recv.at[1],
                                            device_id=sib, device_id_type=MESH)]
        for cp in cps:
            cp.start()
        for cp in cps:
            cp.wait()

    return pl.pallas_call(
        body, name=name, in_specs=[HBM, HBM], out_specs=[HBM, HBM],
        out_shape=[jax.ShapeDtypeStruct(f_in.shape, F32), jax.ShapeDtypeStruct(f_out.shape, F32)],
        scratch_shapes=[pltpu.SemaphoreType.DMA((2,)), pltpu.SemaphoreType.DMA((2,))],
        compiler_params=pltpu.CompilerParams(has_side_effects=True),
    )(f_in, f_out)


def gather_small(pack):
    def body(p, g, send, recv, lsem):
        x, y, c = _mesh_pos()
        me = 4 * x + 2 * y + c
        local = pltpu.make_async_copy(p, g.at[me], lsem)
        local.start()
        cps = []
        k = 0
        for fx in (0, 1):
            for fy in (0, 1):
                for fc in (0, 1):
                    if fx + fy + fc == 0:
                        continue
                    to = (x ^ fx, y ^ fy, c ^ fc)
                    cps.append((pltpu.make_async_remote_copy(src_ref=p, dst_ref=g.at[me], send_sem=send.at[k],
                                                             recv_sem=recv.at[k], device_id=to, device_id_type=MESH), to, k))
                    k += 1
        for cp, _, _ in cps:
            cp.start()
        for cp, to, k in cps:
            frm = 4 * to[0] + 2 * to[1] + to[2]
            pltpu.make_async_remote_copy(src_ref=p, dst_ref=g.at[frm], send_sem=send.at[k], recv_sem=recv.at[k],
                                         device_id=to, device_id_type=MESH).wait_recv()
        for cp, _, _ in cps:
            cp.wait_send()
        local.wait()

    return pl.pallas_call(
        body, name="gather_small", in_specs=[HBM], out_specs=HBM,
        out_shape=jax.ShapeDtypeStruct((N_DEV,) + pack.shape, pack.dtype),
        scratch_shapes=[pltpu.SemaphoreType.DMA((7,)), pltpu.SemaphoreType.DMA((7,)), pltpu.SemaphoreType.DMA],
        compiler_params=pltpu.CompilerParams(has_side_effects=True),
    )(pack)


def _adamw(w, g, m, v):
    m = ADAM_B1 * m + (1.0 - ADAM_B1) * g
    v = ADAM_B2 * v + (1.0 - ADAM_B2) * (g * g)
    m_hat = m / (1.0 - ADAM_B1 ** ADAM_STEP)
    v_hat = v / (1.0 - ADAM_B2 ** ADAM_STEP)
    delta = -ADAM_LR * (m_hat / (jnp.sqrt(v_hat) + ADAM_EPS) + ADAM_WD * w)
    return delta, m, v


def adamw_big(name, w, g, m, v):
    shp = w.shape
    flat = lambda a: a.reshape(shp[-2], shp[-1])
    (d, nm, nv), _ = rowcall(name, lambda w, g, m, v: (_adamw(w, g, m, v), ()), [flat(w), flat(g), flat(m), flat(v)], [],
                             [(shp[-1], F32)] * 3, tb=256)
    return d.reshape(shp), nm.reshape(shp), nv.reshape(shp)


def _pack_small(arrs):
    flat = jnp.concatenate([a.reshape(-1) for a in arrs])
    n = flat.shape[0]
    rows = -(-n // (SUBLANE * LANE)) * SUBLANE
    return jnp.pad(flat, (0, rows * LANE - n)).reshape(rows, LANE)


def _unpack_small(pack, shapes):
    flat = pack.reshape(-1)
    outs, off = [], 0
    for s in shapes:
        n = int(np.prod(s))
        outs.append(flat[off:off + n].reshape(s))
        off += n
    return outs


SMALL_NAMES = ("norm_g", "final_g", "rel_bias", "hgrn_lb", "ssd_conv_w", "ssd_conv_b", "ssd_dt_bias", "ssd_a_log",
               "ssd_d", "ssd_norm_g", "hg_norm_g", "at_q_norm_g", "at_k_norm_g")
ALL_NAMES = ("norm_g", "final_g", "rel_bias", "hgrn_lb", "ssd_w_in", "ssd_conv_w", "ssd_conv_b", "ssd_dt_bias",
             "ssd_a_log", "ssd_d", "ssd_norm_g", "ssd_w_out", "hg_w_in", "hg_norm_g", "hg_w_out", "at_w_in",
             "at_q_norm_g", "at_k_norm_g", "at_w_out", "dl_w_in", "dl_w_out")


def kernel(x, norm_g, final_g, rel_bias, hgrn_lb, ssd_w_in, ssd_conv_w, ssd_conv_b, ssd_dt_bias, ssd_a_log, ssd_d, ssd_norm_g, ssd_w_out, hg_w_in, hg_norm_g, hg_w_out, at_w_in, at_q_norm_g, at_k_norm_g, at_w_out, dl_w_in, dl_w_out, loss_target, m_norm_g, m_final_g, m_rel_bias, m_hgrn_lb, m_ssd_w_in, m_ssd_conv_w, m_ssd_conv_b, m_ssd_dt_bias, m_ssd_a_log, m_ssd_d, m_ssd_norm_g, m_ssd_w_out, m_hg_w_in, m_hg_norm_g, m_hg_w_out, m_at_w_in, m_at_q_norm_g, m_at_k_norm_g, m_at_w_out, m_dl_w_in, m_dl_w_out, v_norm_g, v_final_g, v_rel_bias, v_hgrn_lb, v_ssd_w_in, v_ssd_conv_w, v_ssd_conv_b, v_ssd_dt_bias, v_ssd_a_log, v_ssd_d, v_ssd_norm_g, v_ssd_w_out, v_hg_w_in, v_hg_norm_g, v_hg_w_out, v_at_w_in, v_at_q_norm_g, v_at_k_norm_g, v_at_w_out, v_dl_w_in, v_dl_w_out):
    args = locals()
    W = {n: args[n] for n in ALL_NAMES}
    M = {n: args["m_" + n] for n in ALL_NAMES}
    V = {n: args["v_" + n] for n in ALL_NAMES}
    xi, yi, ci = lax.axis_index("x"), lax.axis_index("y"), lax.axis_index("c")
    chip = 2 * xi + yi
    conv_shard = SSD_CONV_CH // N_CHIPS
    hgn_shard = HG_W // N_CHIPS

    p_in = [W[n][0].astype(BF16) for n in IN_NAMES]
    p_out = [W[n][0].astype(BF16) for n in OUT_NAMES]
    p_small = jnp.concatenate([
        jnp.pad(ssd_conv_w[0], ((0, 0), (0, D_MODEL - conv_shard))),
        jnp.pad(hg_norm_g, ((0, 0), (0, D_MODEL - hgn_shard)))], axis=0)
    c_idx = ci.astype(jnp.int32).reshape(1)
    me_idx = chip.astype(jnp.int32).reshape(1)

    def slot(stack, own, k):
        return jnp.where(chip == k, own, stack[k])

    def layer_weights(layer, got):
        s_in, s_out = got[0], got[1]
        return (jnp.concatenate([slot(s_in, p_in[layer], k) for k in range(N_CHIPS)], axis=1),
                jnp.concatenate([slot(s_out, p_out[layer], k) for k in range(N_CHIPS)], axis=0))

    def reduce_start(tag, layers, grads):
        gp_in = jnp.concatenate([grads[IN_NAMES[l]][0].reshape(D_MODEL, N_CHIPS, IN_COLS[l]).transpose(1, 0, 2)
                                 for l in layers], axis=2)
        gp_out = jnp.concatenate([grads[OUT_NAMES[l]][0].reshape(N_CHIPS, OUT_ROWS[l], D_MODEL) for l in layers], axis=1)
        r_in, r_out = swap_halves(f"swap_halves_{tag}", gp_in, gp_out)
        f_in, b_in = half_add(f"half_add_in_{tag}", gp_in, r_in, c_idx, 128)
        f_out, b_out = half_add(f"half_add_out_{tag}", gp_out, r_out, c_idx, 128)
        return (f_in, f_out), scatter_plan([b_in, b_out])

    def reduce_finish(tag, layers, halves, got, G):
        s_in = chip_sum(f"chip_sum_in_{tag}", halves[0], got[0], me_idx, 128)
        s_out = chip_sum(f"chip_sum_out_{tag}", halves[1], got[1], me_idx, 128)
        o_in, o_out = share_halves(f"share_halves_{tag}", s_in, s_out)
        red_in = jnp.where(ci == 0, jnp.concatenate([s_in, o_in], axis=0), jnp.concatenate([o_in, s_in], axis=0))
        red_out = jnp.where(ci == 0, jnp.concatenate([s_out, o_out], axis=0), jnp.concatenate([o_out, s_out], axis=0))
        off_c = off_r = 0
        for l in layers:
            G[IN_NAMES[l]] = red_in[:, off_c:off_c + IN_COLS[l]][None]
            G[OUT_NAMES[l]] = red_out[off_r:off_r + OUT_ROWS[l]][None]
            off_c += IN_COLS[l]
            off_r += OUT_ROWS[l]

    class Schedule:
        early = (2, 3)
        middle = (1,)

        def fwd_plans(self, layer, w):
            if layer == 0:
                def ssd_out(got):
                    stack = got["ssd_scan_f"][2]
                    return dict(ssd_w_out=jnp.concatenate([slot(stack, p_out[0], k) for k in range(N_CHIPS)], axis=0))

                return dict(ssd_scan_f=gather_plan([p_in[1], p_out[1], p_out[0]]),
                            ssd_scan_b=gather_plan([p_in[2], p_out[2]]), late_weights=ssd_out)
            if layer == 1:
                return dict(hg_scan_f=gather_plan([p_in[3]]), hg_scan_b=gather_plan([p_out[3]]))
            return None

        def fwd_done(self, layer, got, w):
            if layer == 0:
                w["hg_w_in"], w["hg_w_out"] = layer_weights(1, got["ssd_scan_f"])
                w["at_w_in"], w["at_w_out"] = layer_weights(2, got["ssd_scan_b"])
            if layer == 1:
                w["dl_w_in"], w["dl_w_out"] = layer_weights(3, got["hg_scan_f"] + got["hg_scan_b"])

        def bwd_plans(self, layer, grads):
            if layer == 1:
                self.halves_a, plan = reduce_start("a", self.early, grads)
                return dict(hg_scan_f_bwd=plan)
            if layer == 0:
                self.halves_c, plan = reduce_start("c", self.middle, grads)
                return dict(ssd_scan_f_bwd=plan)
            return None

        def bwd_done(self, layer, got):
            if layer == 1:
                self.got_a = got["hg_scan_f_bwd"]
            if layer == 0:
                self.got_c = got["ssd_scan_f_bwd"]

    g0_in, g_small = run_exchange("gather_w0", gather_plan([p_in[0]], whole=[p_small]))
    ssd_in_full = jnp.concatenate([slot(g0_in, p_in[0], k) for k in range(N_CHIPS)], axis=1)
    conv_full = jnp.concatenate([slot(g_small, p_small, k)[:SSD_CONV, :conv_shard] for k in range(N_CHIPS)], axis=1)
    hgn_full = jnp.concatenate([slot(g_small, p_small, k)[SSD_CONV:SSD_CONV + 1, :hgn_shard] for k in range(N_CHIPS)], axis=1)
    w = dict(
        norm_g=norm_g, final_g=final_g, rel_bias=rel_bias, hgrn_lb=hgrn_lb,
        ssd_w_main=ssd_in_full[:, :SSD_MAIN], ssd_w_dt=ssd_in_full[:, SSD_MAIN:],
        ssd_conv_w8=jnp.concatenate([conv_full, jnp.zeros((1, SSD_CONV_CH), F32)], axis=0),
        ssd_conv_b=ssd_conv_b, ssd_dt_bias=ssd_dt_bias, ssd_a_log=ssd_a_log, ssd_d=ssd_d, ssd_norm_g=ssd_norm_g,
        hg_norm_g=hgn_full, at_q_norm_g=at_q_norm_g, at_k_norm_g=at_k_norm_g)

    sched = Schedule()
    loss_tile, grad_x, grads = local_step(x[0], loss_target[0], w, sched)
    loss = lax.psum(loss_tile[0, 0], ("x", "y", "c"))

    G = {}
    late = (0,)
    halves_b, plan_b = reduce_start("b", late, grads)
    got_b = run_exchange("scatter_b", plan_b)
    reduce_finish("a", sched.early, sched.halves_a, sched.got_a, G)
    reduce_finish("c", sched.middle, sched.halves_c, sched.got_c, G)
    reduce_finish("b", late, halves_b, got_b, G)

    small_full = [grads[n].reshape(-1) for n in SMALL_NAMES]
    shapes_full = [grads[n].shape for n in SMALL_NAMES]
    packs = gather_small(_pack_small(small_full))
    (red_small,) = smallcall("sum_small", lambda p: (functools.reduce(lambda a, b: a + b, [p[k] for k in range(N_DEV)]),),
                             [packs], [packs.shape[1:]])
    for n, g in zip(SMALL_NAMES, _unpack_small(red_small, shapes_full)):
        G[n] = g
    G["ssd_conv_w"] = lax.dynamic_slice_in_dim(G["ssd_conv_w"].reshape(1, SSD_CONV, SSD_CONV_CH), chip * conv_shard, conv_shard, axis=2)
    G["hg_norm_g"] = lax.dynamic_slice_in_dim(G["hg_norm_g"].reshape(1, HG_W), chip * hgn_shard, hgn_shard, axis=1)
    for n in SMALL_NAMES:
        G[n] = G[n].reshape(W[n].shape)

    D, NM, NV = {}, {}, {}
    for n in IN_NAMES + OUT_NAMES:
        D[n], NM[n], NV[n] = adamw_big("adamw_" + n, W[n], G[n], M[n], V[n])
    shapes = [W[n].shape for n in SMALL_NAMES]
    pk = [_pack_small([T[n] for n in SMALL_NAMES]) for T in (W, G, M, V)]
    outs = smallcall("adamw_small", lambda w, g, m, v: _adamw(w, g, m, v), pk, [pk[0].shape] * 3)
    for T, pack in zip((D, NM, NV), outs):
        for n, a in zip(SMALL_NAMES, _unpack_small(pack, shapes)):
            T[n] = a
    return (loss, grad_x[None], *[G[n] for n in ALL_NAMES], *[D[n] for n in ALL_NAMES],
            *[NM[n] for n in ALL_NAMES], *[NV[n] for n in ALL_NAMES])
```

```python
import functools
import math

import numpy as np
import jax
import jax.numpy as jnp
from jax import lax
from jax.experimental import pallas as pl
from jax.experimental.pallas import tpu as pltpu

F32 = jnp.float32
BF16 = jnp.bfloat16
MESH = pl.DeviceIdType.MESH

D_MODEL = 1024
DEPTH = 4
GRID_W = 64
EPS = 1e-6
NEG_BIG = -1e30

SSD_DI = 2048
SSD_HEADDIM = 64
SSD_HEADS = 32
SSD_GROUPS = 4
SSD_HPG = 8
SSD_STATE = 128
SSD_CONV = 7
SSD_CHUNK = 128
SSD_GPS = 4
SSD_CONV_CH = SSD_DI + 2 * SSD_GROUPS * SSD_STATE
SSD_MAIN = SSD_DI + SSD_CONV_CH
SSD_IN = SSD_MAIN + 2 * SSD_HEADS

HG_HEADS = 8
HG_EXPAND = 128
HG_W = 1024
HG_CHUNK = 32
HG_ROWS = 256
HG_HPS = 8
HG_IN = 5 * HG_W

AT_HEADS = 16
AT_KV = 8
AT_GRP = 2
AT_HD = 128
ROPE_THETA = 10000.0
ROPE_AXIS = 64
AT_QW = AT_HEADS * AT_HD
AT_KW = AT_KV * AT_HD
AT_IN = 2 * AT_QW + 2 * AT_KW

DL_PAIRS = ((128, 1), (512, 4), (2048, 16))
DL_HEADS = 16
DL_HD = 64
DL_W = 1024
DL_STEPS = 64
DL_IN = 10 * DL_W
REL_BUCKETS = 32
REL_MAX_DIST = 1024

ADAM_LR = 0.001
ADAM_B1 = 0.9
ADAM_B2 = 0.999
ADAM_EPS = 1e-08
ADAM_WD = 0.01
ADAM_STEP = 10

VMEM_LIMIT = 56 * 1024 * 1024
LANE = 128
SUBLANE = 8


def _cp(sem=None):
    return pltpu.CompilerParams(dimension_semantics=sem, vmem_limit_bytes=VMEM_LIMIT)


_NN, _NT, _TN = ((1,), (0,)), ((1,), (1,)), ((0,), (0,))


def _dot(a, b, dims):
    return lax.dot_general(a.astype(BF16), b.astype(BF16), (dims, ((), ())), preferred_element_type=F32)


def _dot_rule(dims, da_rule, db_rule):
    @jax.custom_vjp
    def f(a, b):
        return _dot(a, b, dims)

    def fwd(a, b):
        return _dot(a, b, dims), (a, b)

    def bwd(res, g):
        a, b = res
        return da_rule(a, b, g).astype(a.dtype), db_rule(a, b, g).astype(b.dtype)

    f.defvjp(fwd, bwd)
    return f


_mm = _dot_rule(_NN, lambda a, b, g: _dot(g, b, _NT), lambda a, b, g: _dot(a, g, _TN))
_mm_nt = _dot_rule(_NT, lambda a, b, g: _dot(g, b, _NN), lambda a, b, g: _dot(g, a, _TN))
_mm_tn = _dot_rule(_TN, lambda a, b, g: _dot(b, g, _NT), lambda a, b, g: _dot(a, g, _NN))


def _mm_exact(a, b):
    return jnp.dot(a, b, preferred_element_type=F32, precision=lax.Precision.HIGHEST)


def _dot3(t, a, dims):
    hi = a.astype(BF16)
    r1 = a - hi.astype(F32)
    mid = r1.astype(BF16)
    lo = r1 - mid.astype(F32)
    return _dot(t, hi, dims) + (_dot(t, mid, dims) + _dot(t, lo, dims))


@jax.custom_vjp
def _mm_tri(t, a):
    return _dot3(t, a, _NN)


def _mm_tri_fwd(t, a):
    return _dot3(t, a, _NN), t


def _mm_tri_bwd(t, g):
    return None, _dot3(t, g, _TN)


_mm_tri.defvjp(_mm_tri_fwd, _mm_tri_bwd)


def _mm_nt_exact(a, b):
    return lax.dot_general(a, b, (((1,), (1,)), ((), ())), preferred_element_type=F32,
                           precision=lax.Precision.HIGHEST)


def _silu(x):
    return x * jax.nn.sigmoid(x)


def _softplus(z):
    return jnp.maximum(z, 0.0) + jnp.log(1.0 + jnp.exp(-jnp.abs(z)))


def _pick(dim, pref):
    best = None
    t = LANE
    while t <= min(dim, pref):
        if dim % t == 0:
            best = t
        t += LANE
    return best if best is not None else dim


def _const_map(n):
    return lambda *_: (0,) * n


MM_BLOCK_BYTES = 8 * 1024 * 1024


def _mm_tiles(mode, M, N, K, a_bytes, b_bytes, has_res=False):
    if mode == "nn":
        tk = K if K <= 2048 else _pick(K, 1024)
        tm = _pick(M, max(512, MM_BLOCK_BYTES // (tk * a_bytes)))
        tn = _pick(N, 512)
    elif mode == "tn":
        tk = K if K <= 4096 else _pick(K, 1024)
        tm = _pick(M, MM_BLOCK_BYTES // (tk * a_bytes))
        tn = _pick(N, MM_BLOCK_BYTES // (tk * b_bytes))
    else:
        tk = _pick(K, 1024)
        tn = _pick(N, 1024)
        tm = _pick(M, MM_BLOCK_BYTES // ((8 if has_res else 4) * tn))
    return tm, tn, tk


def matmul(name, a, b, mode="nn", res=None, out_dtype=F32):
    if mode == "tn":
        K, M = a.shape
    else:
        M, K = a.shape
    N = b.shape[0] if mode == "nt" else b.shape[1]
    tm, tn, tk = _mm_tiles(mode, M, N, K, a.dtype.itemsize, b.dtype.itemsize, res is not None)
    nk = K // tk
    a_spec = (pl.BlockSpec((tk, tm), lambda i, j, k: (k, i)) if mode == "tn"
              else pl.BlockSpec((tm, tk), lambda i, j, k: (i, k)))
    b_spec = (pl.BlockSpec((tn, tk), lambda i, j, k: (j, k)) if mode == "nt"
              else pl.BlockSpec((tk, tn), lambda i, j, k: (k, j)))
    dot = {"nn": _mm, "nt": _mm_nt, "tn": _mm_tn}[mode]
    has_res = res is not None

    def body(*refs):
        a_ref, b_ref = refs[0], refs[1]
        r_ref = refs[2] if has_res else None
        o_ref = refs[3] if has_res else refs[2]

        def finish(out):
            if has_res:
                out = out + r_ref[...].astype(F32)
            o_ref[...] = out.astype(o_ref.dtype)

        if nk == 1:
            finish(dot(a_ref[...], b_ref[...]))
            return
        acc = refs[-1]
        k = pl.program_id(2)

        @pl.when(k == 0)
        def _():
            acc[...] = jnp.zeros_like(acc)

        acc[...] += dot(a_ref[...], b_ref[...])

        @pl.when(k == nk - 1)
        def _():
            finish(acc[...])

    in_specs = [a_spec, b_spec]
    args = [a, b]
    if has_res:
        in_specs.append(pl.BlockSpec((tm, tn), lambda i, j, k: (i, j)))
        args.append(res)
    return pl.pallas_call(
        body, name=name, grid=(M // tm, N // tn, nk), in_specs=in_specs,
        out_specs=pl.BlockSpec((tm, tn), lambda i, j, k: (i, j)),
        out_shape=jax.ShapeDtypeStruct((M, N), out_dtype),
        scratch_shapes=[pltpu.VMEM((tm, tn), F32)] if nk > 1 else [],
        compiler_params=_cp(("parallel", "parallel", "arbitrary")),
    )(*args)


def call_with_comm(body, comm, *, name, grid, in_specs, out_specs, out_shape, scratch_shapes, semantics, args):
    if comm is None:
        outs = pl.pallas_call(body, name=name, grid=grid, in_specs=in_specs, out_specs=out_specs, out_shape=out_shape,
                              scratch_shapes=scratch_shapes, compiler_params=_cp(semantics))(*args)
        return list(outs), []
    n_in, n_out, n_scr = len(in_specs), len(out_specs), len(scratch_shapes)
    c_in, c_out = len(comm["ins"]), len(comm["out_shape"])
    total = int(np.prod(grid))
    mid_step = (2 * total) // 3

    def wrapped(*refs):
        p = 0
        ins = refs[p:p + n_in]
        p += n_in
        cins = refs[p:p + c_in]
        p += c_in
        outs = refs[p:p + n_out]
        p += n_out
        couts = refs[p:p + c_out]
        p += c_out
        scr = refs[p:p + n_scr]
        send, recv = refs[p + n_scr], refs[p + n_scr + 1]
        step = pl.program_id(0)
        for ax in range(1, len(grid)):
            step = step * grid[ax] + pl.program_id(ax)

        @pl.when(step == 0)
        def _():
            comm["start"](cins, couts, send, recv)

        body(*ins, *outs, *scr)
        if comm["mid"] is not None:
            @pl.when(step == mid_step)
            def _():
                comm["mid"](cins, couts, send, recv)

        @pl.when(step == total - 1)
        def _():
            comm["finish"](cins, couts, send, recv)

    outs = pl.pallas_call(
        wrapped, name=name, grid=grid, in_specs=list(in_specs) + [HBM] * c_in,
        out_specs=list(out_specs) + [HBM] * c_out, out_shape=list(out_shape) + list(comm["out_shape"]),
        scratch_shapes=list(scratch_shapes) + [pltpu.SemaphoreType.DMA((comm["n_sems"],))] * 2,
        compiler_params=_cp(("arbitrary",) * len(grid)),
    )(*args, *comm["ins"])
    return list(outs[:n_out]), list(outs[n_out:])


def _col(arr, width, idx):
    return (arr, width, idx)


def _perm(arr, dil, width=None, idx=0):
    return (arr, arr.shape[1] if width is None else width, idx, dil)


def _from_perm(ref, scr, dil):
    n, w = ref.shape[1], ref.shape[2]
    for r in range(dil):
        for j in range(w // LANE):
            scr[j, pl.ds(r, n, stride=dil), :] = ref[r, :, j * LANE:(j + 1) * LANE].astype(F32)
    return jnp.concatenate([scr[j] for j in range(w // LANE)], axis=1)


def _to_perm(val, ref, scr, dil):
    n, w = ref.shape[1], ref.shape[2]
    for j in range(w // LANE):
        scr[j] = val[:, j * LANE:(j + 1) * LANE].astype(F32)
    for r in range(dil):
        ref[r] = jnp.concatenate([scr[j, pl.ds(r, n, stride=dil), :] for j in range(w // LANE)], axis=1).astype(ref.dtype)


def rowcall(name, fn, rows, bcs, row_outs, bc_outs=(), tb=256, halo=()):
    rows = [r if isinstance(r, tuple) else (r, r.shape[1], 0) for r in rows]
    rows = [r if len(r) == 4 else r + (1,) for r in rows]
    row_outs = [o if len(o) == 3 else o + (1,) for o in row_outs]
    S = rows[0][0].shape[0]
    tb = min(tb, S)
    nb = S // tb
    n_r, n_h, n_b, n_ro, n_bo = len(rows), len(halo), len(bcs), len(row_outs), len(bc_outs)
    hb = tb // SUBLANE
    last = S // SUBLANE - 1
    perm_w = max([w for (_, w, _, d) in rows if d > 1] + [w for (w, _, d) in row_outs if d > 1] + [0])

    def body(*refs):
        i = pl.program_id(0)
        scr = refs[-1] if perm_w else None
        pos = 0
        r_in = [r[...] if d == 1 else _from_perm(r, scr, d) for r, (_, _, _, d) in zip(refs[pos:pos + n_r], rows)]
        pos += n_r
        h_in = []
        for _ in range(n_h):
            prev = refs[pos][...] * (i > 0).astype(F32)
            nxt = refs[pos + 1][...] * (i < nb - 1).astype(F32)
            h_in += [prev, nxt]
            pos += 2
        b_in = [r[...] for r in refs[pos:pos + n_b]]
        pos += n_b
        ro = refs[pos:pos + n_ro]
        bo = refs[pos + n_ro:pos + n_ro + n_bo]
        outs_r, outs_b = fn(*r_in, *h_in, *b_in)
        for ref, val, (_, _, d) in zip(ro, outs_r, row_outs, strict=True):
            if d == 1:
                ref[...] = val.astype(ref.dtype)
            else:
                _to_perm(val, ref, scr, d)
        if n_bo:
            @pl.when(i == 0)
            def _():
                for ref in bo:
                    ref[...] = jnp.zeros_like(ref)

            for ref, val in zip(bo, outs_b, strict=True):
                ref[...] += val

    in_specs, args = [], []
    for (a, w, c, d) in rows:
        if d == 1:
            in_specs.append(pl.BlockSpec((tb, w), functools.partial(lambda i, c: (i, c), c=c)))
            args.append(a)
        else:
            in_specs.append(pl.BlockSpec((d, tb // d, w), functools.partial(lambda i, c: (0, i, c), c=c)))
            args.append(a.reshape(d, S // d, a.shape[1]))
    for h in halo:
        a, w, c, _ = rows[h]
        in_specs.append(pl.BlockSpec((SUBLANE, w), functools.partial(
            lambda i, c: (jnp.maximum(i * hb - 1, 0), c), c=c)))
        in_specs.append(pl.BlockSpec((SUBLANE, w), functools.partial(
            lambda i, c: (jnp.minimum((i + 1) * hb, last), c), c=c)))
        args += [a, a]
    for b in bcs:
        in_specs.append(pl.BlockSpec(b.shape, _const_map(b.ndim)))
        args.append(b)
    out_specs, out_shape = [], []
    for (w, dt, d) in row_outs:
        if d == 1:
            out_specs.append(pl.BlockSpec((tb, w), lambda i: (i, 0)))
            out_shape.append(jax.ShapeDtypeStruct((S, w), dt))
        else:
            out_specs.append(pl.BlockSpec((d, tb // d, w), lambda i: (0, i, 0)))
            out_shape.append(jax.ShapeDtypeStruct((d, S // d, w), dt))
    for shp in bc_outs:
        out_specs.append(pl.BlockSpec(shp, _const_map(len(shp))))
        out_shape.append(jax.ShapeDtypeStruct(shp, F32))
    outs = pl.pallas_call(
        body, name=name, grid=(nb,), in_specs=in_specs, out_specs=out_specs, out_shape=out_shape,
        scratch_shapes=[pltpu.VMEM((perm_w // LANE, tb, LANE), F32)] if perm_w else [],
        compiler_params=_cp(("arbitrary",) if n_bo else ("parallel",)),
    )(*args)
    row_res = [o if d == 1 else o.reshape(S, w) for o, (w, _, d) in zip(outs[:n_ro], row_outs)]
    return row_res, list(outs[n_ro:])


def smallcall(name, fn, ins, out_shapes):
    n_in = len(ins)

    def body(*refs):
        outs = fn(*[r[...] for r in refs[:n_in]])
        for ref, val in zip(refs[n_in:], outs, strict=True):
            ref[...] = val.astype(ref.dtype)

    return pl.pallas_call(
        body, name=name, out_shape=[jax.ShapeDtypeStruct(s, F32) for s in out_shapes],
        compiler_params=_cp(),
    )(*ins)


def _rms(x, g):
    return x * lax.rsqrt(jnp.mean(x * x, axis=-1, keepdims=True) + EPS) * g


def _rms_groups(y, g, width):
    outs = []
    for j in range(y.shape[1] // width):
        sl = slice(j * width, (j + 1) * width)
        outs.append(_rms(y[:, sl], g[:, sl]))
    return jnp.concatenate(outs, axis=1)


def norm_fwd(name, x, g, dils=(1,)):
    outs, _ = rowcall(name, lambda x, g: ((_rms(x, g),) * len(dils), ()), [x], [g],
                      [(D_MODEL, BF16, d) for d in dils], tb=512)
    return outs[0] if len(dils) == 1 else tuple(outs)


def norm_bwd(name, x, g, dhn, dres, dils=(1,)):
    parts = dhn if isinstance(dhn, tuple) else (dhn,)
    n = len(parts)

    def fn(x, *rest):
        dh = functools.reduce(lambda a, b: a + b, rest[:n])
        _, vjp = jax.vjp(_rms, x, rest[n + 1])
        dx, dg = vjp(dh)
        return (dx + rest[n],), (dg,)

    rows = [x] + [a if d == 1 else _perm(a, d) for a, d in zip(parts, dils)] + [dres]
    (dx,), (dg,) = rowcall(name, fn, rows, [g], [(D_MODEL, F32)], [(1, D_MODEL)], tb=512)
    return dx, dg


def loss_head(x, tgt, g):
    def fn(x, tgt, g):
        y, vjp = jax.vjp(_rms, x, g)
        diff = y - tgt
        loss = 0.5 * jnp.sum(jnp.mean(diff * diff, axis=-1, keepdims=True), axis=0, keepdims=True)
        dx, dg = vjp(diff * (1.0 / D_MODEL))
        return (dx,), (jnp.broadcast_to(loss, (1, LANE)), dg)

    (dx,), (loss, dg) = rowcall("loss_head", fn, [x, tgt], [g], [(D_MODEL, F32)],
                                [(1, LANE), (1, D_MODEL)], tb=512)
    return loss, dx, dg


def _shift_rows(x, s):
    if s == 0:
        return x
    return pltpu.roll(x, (-s) % x.shape[0], 0)


def _conv_ext(x, prev, nxt, w):
    xe = jnp.concatenate([prev, x, nxt], axis=0)
    pad = SSD_CONV // 2
    c = jnp.zeros_like(xe)
    for k in range(SSD_CONV):
        c = c + w[k:k + 1, :] * _shift_rows(xe, k - pad)
    return xe, c


def ssd_conv_fwd(u, conv_w, conv_b):
    def fn(x0, x1, x2, p0, n0, p1, n1, p2, n2, w, b):
        tb = x0.shape[0]
        outs = []
        for j, (x, p, n) in enumerate(((x0, p0, n0), (x1, p1, n1), (x2, p2, n2))):
            sl = slice(j * 1024, (j + 1) * 1024)
            _, c = _conv_ext(x, p, n, w[:, sl])
            outs.append(_silu(c[SUBLANE:SUBLANE + tb] + b[:, sl]))
        return (jnp.concatenate(outs, axis=1),), ()

    (xbc,), _ = rowcall("ssd_conv_fwd", fn, [_col(u, 1024, 2), _col(u, 1024, 3), _col(u, 1024, 4)],
                        [conv_w, conv_b], [(SSD_CONV_CH, F32)], tb=256, halo=(0, 1, 2))
    return xbc


def ssd_conv_bwd(u, dxbc, dz, conv_w, conv_b):
    pad = SSD_CONV // 2

    def fn(x0, x1, x2, g0, g1, g2, dz, xp0, xn0, xp1, xn1, xp2, xn2, gp0, gn0, gp1, gn1, gp2, gn2, w, b):
        tb = x0.shape[0]
        blk = slice(SUBLANE, SUBLANE + tb)
        dpre, dws, dbs = [], [], []
        xs = ((x0, xp0, xn0), (x1, xp1, xn1), (x2, xp2, xn2))
        gs = ((g0, gp0, gn0), (g1, gp1, gn1), (g2, gp2, gn2))
        for j in range(3):
            sl = slice(j * 1024, (j + 1) * 1024)
            wj = w[:, sl]
            xe, c = _conv_ext(*xs[j], wj)
            ce = c + b[:, sl]
            sig = jax.nn.sigmoid(ce)
            ge = jnp.concatenate([gs[j][1], gs[j][0], gs[j][2]], axis=0)
            dce = ge * (sig * (1.0 + ce * (1.0 - sig)))
            dx = jnp.zeros_like(xe)
            dw_rows = []
            for k in range(SSD_CONV):
                dx = dx + wj[k:k + 1, :] * _shift_rows(dce, pad - k)
                dw_rows.append(jnp.sum(dce[blk] * _shift_rows(xe, k - pad)[blk], axis=0, keepdims=True))
            dw_rows.append(jnp.zeros_like(dw_rows[0]))
            dpre.append(dx[blk])
            dws.append(jnp.concatenate(dw_rows, axis=0))
            dbs.append(jnp.sum(dce[blk], axis=0, keepdims=True))
        du = jnp.concatenate([dz] + dpre, axis=1)
        return (du,), (jnp.concatenate(dws, axis=1), jnp.concatenate(dbs, axis=1))

    rows = [_col(u, 1024, 2), _col(u, 1024, 3), _col(u, 1024, 4),
            _col(dxbc, 1024, 0), _col(dxbc, 1024, 1), _col(dxbc, 1024, 2), dz]
    (du,), (dw, db) = rowcall("ssd_conv_bwd", fn, rows, [conv_w, conv_b], [(SSD_MAIN, BF16)],
                              [(SUBLANE, SSD_CONV_CH), (1, SSD_CONV_CH)], tb=128, halo=(0, 1, 2, 3, 4, 5))
    return du, dw, db


def _expand_heads(v):
    return jnp.concatenate([jnp.broadcast_to(v[:, j:j + 1], (v.shape[0], SSD_HEADDIM)) for j in range(SSD_HPG)], axis=1)


def _ssd_chunk(rev, st_in, xs, udt, dtb, alog, B, C):
    Q = B.shape[0]
    P = SSD_HEADDIM
    dt = _softplus(udt + dtb)
    a = dt * (-jnp.exp(alog))
    r = lax.broadcasted_iota(jnp.int32, (Q, Q), 0)
    c = lax.broadcasted_iota(jnp.int32, (Q, Q), 1)
    mask = (r <= c) if rev else (r >= c)
    p = _mm_tri(mask, a)
    pT = p.T
    p_e = _expand_heads(p)
    tot_e = p_e[0:1] if rev else p_e[Q - 1:Q]
    xdt = xs * _expand_heads(dt)
    CB = _mm_nt(C, B)
    H = SSD_HPG
    p_cols = jnp.concatenate([jnp.broadcast_to(p[:, j:j + 1], (Q, Q)) for j in range(H)], axis=1)
    p_rows = jnp.concatenate([pT[j:j + 1, :] for j in range(H)], axis=1)
    decay = jnp.exp(jnp.where(jnp.concatenate([mask] * H, axis=1), p_cols - p_rows, NEG_BIG))
    col = lax.broadcasted_iota(jnp.int32, (1, H * P), 1)
    x_bd = jnp.concatenate([jnp.where((col >= j * P) & (col < (j + 1) * P), xdt, 0.0) for j in range(H)], axis=0)
    y = _mm(jnp.concatenate([CB] * H, axis=1) * decay, x_bd) + _mm(C, st_in) * jnp.exp(p_e)
    st_out = st_in * jnp.exp(tot_e) + _mm_tn(B, xdt * jnp.exp(tot_e - p_e))
    return y, st_out


def _ssd_specs(nc, rev_order):
    Q = SSD_CHUNK
    N, P, H, GS = SSD_STATE, SSD_HEADDIM, SSD_HPG, SSD_GPS
    gw = H * P
    nbc = SSD_GROUPS // GS

    def cidx(s):
        return nc - 1 - s if rev_order else s

    xs = pl.BlockSpec((Q, GS * gw), lambda g, s: (cidx(s), g))
    Bs = pl.BlockSpec((Q, GS * N), lambda g, s: (cidx(s), SSD_DI // (GS * N) + g))
    Cs = pl.BlockSpec((Q, GS * N), lambda g, s: (cidx(s), SSD_DI // (GS * N) + nbc + g))
    BC_out = pl.BlockSpec((Q, GS * N), lambda g, s: (cidx(s), g))
    udt = pl.BlockSpec((GS, Q, H), lambda g, s: (g, cidx(s), 0))
    small = pl.BlockSpec((GS, 1, H), lambda g, s: (g, 0, 0))
    st = pl.BlockSpec((GS, 1, N, gw), lambda g, s: (g, cidx(s), 0, 0))
    return xs, Bs, Cs, BC_out, udt, small, st


def ssd_scan_fwd(name, xbc, udt, dtb, alog, rev, comm=None):
    S = xbc.shape[0]
    Q, N, P, H, GS = SSD_CHUNK, SSD_STATE, SSD_HEADDIM, SSD_HPG, SSD_GPS
    gw = H * P
    nc = S // Q
    xs_s, B_s, C_s, _, udt_s, small_s, st_s = _ssd_specs(nc, rev)

    def body(xs_ref, B_ref, C_ref, udt_ref, dtb_ref, alog_ref, y_ref, st_ref, state):
        @pl.when(pl.program_id(1) == 0)
        def _():
            state[...] = jnp.zeros_like(state)

        for g in range(GS):
            st_ref[g, 0] = state[g]
            y, st_out = _ssd_chunk(rev, state[g], xs_ref[:, g * gw:(g + 1) * gw], udt_ref[g], dtb_ref[g], alog_ref[g],
                                   B_ref[:, g * N:(g + 1) * N], C_ref[:, g * N:(g + 1) * N])
            y_ref[:, g * gw:(g + 1) * gw] = y
            state[g] = st_out

    (y, st), got = call_with_comm(
        body, comm, name=name, grid=(SSD_GROUPS // GS, nc),
        in_specs=[xs_s, B_s, C_s, udt_s, small_s, small_s],
        out_specs=[xs_s, st_s],
        out_shape=[jax.ShapeDtypeStruct((S, SSD_DI), F32),
                   jax.ShapeDtypeStruct((SSD_GROUPS, nc, N, gw), F32)],
        scratch_shapes=[pltpu.VMEM((GS, N, gw), F32)],
        semantics=("parallel", "arbitrary"), args=(xbc, xbc, xbc, udt, dtb, alog))
    return y, st, got


def ssd_scan_bwd(name, xbc, udt, dtb, alog, states, dy, rev, comm=None):
    S = xbc.shape[0]
    Q, N, P, H, GS = SSD_CHUNK, SSD_STATE, SSD_HEADDIM, SSD_HPG, SSD_GPS
    gw = H * P
    nc = S // Q
    xs_s, B_s, C_s, BC_out, udt_s, small_s, st_s = _ssd_specs(nc, not rev)

    def body(xs_ref, B_ref, C_ref, udt_ref, dtb_ref, alog_ref, st_ref, dy_ref,
             dx_ref, dB_ref, dC_ref, dudt_ref, ddtb_ref, dalog_ref, dstate):
        @pl.when(pl.program_id(1) == 0)
        def _():
            dstate[...] = jnp.zeros_like(dstate)
            ddtb_ref[...] = jnp.zeros_like(ddtb_ref)
            dalog_ref[...] = jnp.zeros_like(dalog_ref)

        for g in range(GS):
            cols, bc = slice(g * gw, (g + 1) * gw), slice(g * N, (g + 1) * N)
            _, vjp = jax.vjp(functools.partial(_ssd_chunk, rev), st_ref[g, 0], xs_ref[:, cols], udt_ref[g], dtb_ref[g],
                             alog_ref[g], B_ref[:, bc], C_ref[:, bc])
            dst_in, dxs, dudt, ddtb, dalog, dB, dC = vjp((dy_ref[:, cols], dstate[g]))
            dx_ref[:, cols] = dxs
            dB_ref[:, bc] = dB
            dC_ref[:, bc] = dC
            dudt_ref[g] = dudt
            ddtb_ref[g] += ddtb
            dalog_ref[g] += dalog
            dstate[g] = dst_in

    outs, got = call_with_comm(
        body, comm, name=name, grid=(SSD_GROUPS // GS, nc),
        in_specs=[xs_s, B_s, C_s, udt_s, small_s, small_s, st_s, xs_s],
        out_specs=[xs_s, BC_out, BC_out, udt_s, small_s, small_s],
        out_shape=[jax.ShapeDtypeStruct((S, SSD_DI), F32),
                   jax.ShapeDtypeStruct((S, SSD_GROUPS * N), F32),
                   jax.ShapeDtypeStruct((S, SSD_GROUPS * N), F32),
                   jax.ShapeDtypeStruct((SSD_GROUPS, S, H), F32),
                   jax.ShapeDtypeStruct((SSD_GROUPS, 1, H), F32),
                   jax.ShapeDtypeStruct((SSD_GROUPS, 1, H), F32)],
        scratch_shapes=[pltpu.VMEM((GS, N, gw), F32)],
        semantics=("parallel", "arbitrary"), args=(xbc, xbc, xbc, udt, dtb, alog, states, dy))
    return (*outs, got)


def _ssd_combine(yf, yb, xs, z, dexp, ng):
    y = (yf + yb + xs * dexp) * _silu(z)
    return _rms_groups(y, ng, SSD_DI // SSD_GROUPS)


def ssd_forward(x, hn, w, comm=None):
    comm = comm or {}
    S = x.shape[0]
    u = matmul("ssd_in", hn, w["ssd_w_main"])
    udt = matmul("ssd_in_dt", hn, w["ssd_w_dt"])
    xbc = ssd_conv_fwd(u, w["ssd_conv_w8"], w["ssd_conv_b"])
    udt_t = udt.reshape(S, 2, SSD_GROUPS, SSD_HPG).transpose(1, 2, 0, 3)
    dtb = w["ssd_dt_bias"].reshape(2, SSD_GROUPS, 1, SSD_HPG)
    alog = w["ssd_a_log"].reshape(2, SSD_GROUPS, 1, SSD_HPG)
    yf, stf, got_f = ssd_scan_fwd("ssd_scan_f", xbc, udt_t[0], dtb[0], alog[0], False, comm.get("ssd_scan_f"))
    yb, stb, got_b = ssd_scan_fwd("ssd_scan_b", xbc, udt_t[1], dtb[1], alog[1], True, comm.get("ssd_scan_b"))
    dexp = jnp.repeat(w["ssd_d"].reshape(1, SSD_HEADS), SSD_HEADDIM, axis=1)
    (yn,), _ = rowcall("ssd_combine", lambda yf, yb, xs, z, d, g: ((_ssd_combine(yf, yb, xs, z, d, g),), ()),
                       [yf, yb, _col(xbc, SSD_DI, 0), _col(u, SSD_DI, 0)], [dexp, w["ssd_norm_g"]],
                       [(SSD_DI, BF16)], tb=256)
    if "late_weights" in comm:
        w.update(comm["late_weights"](dict(ssd_scan_f=got_f, ssd_scan_b=got_b)))
    out = matmul("ssd_out", yn, w["ssd_w_out"], res=x)
    saved = dict(hn=hn, u=u, xbc=xbc, udt_t=udt_t, dtb=dtb, alog=alog, yf=yf, yb=yb, stf=stf, stb=stb,
                 dexp=dexp, yn=yn, got=dict(ssd_scan_f=got_f, ssd_scan_b=got_b))
    return out, saved


def ssd_backward(dy, sv, w, comm=None):
    S = dy.shape[0]
    u, xbc = sv["u"], sv["xbc"]
    dyn = matmul("ssd_out_dx", dy, w["ssd_w_out"], mode="nt")
    g_w_out = matmul("ssd_out_dw", sv["yn"], dy, mode="tn")

    def comb_bwd(yf, yb, xs, z, dyn, dexp, ng):
        _, vjp = jax.vjp(_ssd_combine, yf, yb, xs, z, dexp, ng)
        dyf, _, dxs, dz, ddexp, dng = vjp(dyn)
        return (dyf, dxs, dz), (ddexp, dng)

    (dyc, dskip, dz), (ddexp, g_norm) = rowcall(
        "ssd_combine_bwd", comb_bwd, [sv["yf"], sv["yb"], _col(xbc, SSD_DI, 0), _col(u, SSD_DI, 0), dyn],
        [sv["dexp"], w["ssd_norm_g"]], [(SSD_DI, F32)] * 3, [(1, SSD_DI), (1, SSD_DI)], tb=256)
    udt_t, dtb, alog = sv["udt_t"], sv["dtb"], sv["alog"]
    comm = comm or {}
    dxf, dBf, dCf, dudt_f, ddtb_f, dalog_f, got_f = ssd_scan_bwd("ssd_scan_f_bwd", xbc, udt_t[0], dtb[0], alog[0],
                                                                 sv["stf"], dyc, False, comm.get("ssd_scan_f_bwd"))
    dxb, dBb, dCb, dudt_b, ddtb_b, dalog_b, _ = ssd_scan_bwd("ssd_scan_b_bwd", xbc, udt_t[1], dtb[1], alog[1],
                                                             sv["stb"], dyc, True)

    def gather(dxf, dxb, dskip, dBf, dBb, dCf, dCb):
        return (jnp.concatenate([dxf + dxb + dskip, dBf + dBb, dCf + dCb], axis=1),), ()

    (dxbc,), _ = rowcall("ssd_dxbc", gather, [dxf, dxb, dskip, dBf, dBb, dCf, dCb], [], [(SSD_CONV_CH, F32)], tb=256)
    du, g_conv_w8, g_conv_b = ssd_conv_bwd(u, dxbc, dz, w["ssd_conv_w8"], w["ssd_conv_b"])
    dudt = jnp.stack([dudt_f, dudt_b]).transpose(2, 0, 1, 3).reshape(S, 2 * SSD_HEADS)
    hn = sv["hn"]
    g_main = matmul("ssd_in_dw", hn, du, mode="tn")
    g_dt = matmul("ssd_in_dt_dw", hn, dudt, mode="tn")
    dhn = matmul("ssd_in_dt_dx", dudt, w["ssd_w_dt"], mode="nt")
    dhn = matmul("ssd_in_dx", du, w["ssd_w_main"], mode="nt", res=dhn)
    grads = dict(
        ssd_w_in=jnp.concatenate([g_main, g_dt], axis=1)[None],
        ssd_conv_w=g_conv_w8[None, :SSD_CONV],
        ssd_conv_b=g_conv_b,
        ssd_dt_bias=jnp.stack([ddtb_f, ddtb_b]).reshape(1, 2, SSD_HEADS),
        ssd_a_log=jnp.stack([dalog_f, dalog_b]).reshape(1, 2, SSD_HEADS),
        ssd_d_exp=ddexp,
        ssd_norm_g=g_norm,
        ssd_w_out=g_w_out[None],
        got=dict(ssd_scan_f_bwd=got_f),
    )
    return dhn, grads


def _hg_block(rev, stTs, uq, uf, ui, lb):
    C = HG_CHUNK
    n = uq.shape[0] // C
    nh = uq.shape[1] // HG_EXPAND
    stTs = list(stTs)
    q = _silu(uq)
    f = lb + (1.0 - lb) * jax.nn.sigmoid(uf)
    k = 1.0 - f
    g = jnp.log(f)
    r = lax.broadcasted_iota(jnp.int32, (C, C), 0)
    c = lax.broadcasted_iota(jnp.int32, (C, C), 1)
    mask = (r <= c) if rev else (r >= c)
    Tm = mask.astype(F32)
    outs = [[None] * n for _ in range(nh)]
    for i in (reversed(range(n)) if rev else range(n)):
        sl = slice(i * C, (i + 1) * C)
        qi, ki, vi = q[sl], k[sl], ui[sl]
        G = _mm_tri(mask, g[sl])
        Gr = G[C // 2:C // 2 + 1]
        Gl = G[0:1] if rev else G[C - 1:C]
        q_in, k_in = qi * jnp.exp(G - Gr), ki * jnp.exp(Gr - G)
        q_st, k_st, e_l = qi * jnp.exp(G), ki * jnp.exp(Gl - G), jnp.exp(Gl)
        for h in range(nh):
            cs = slice(h * HG_EXPAND, (h + 1) * HG_EXPAND)
            att = jnp.where(mask, _mm_nt(q_in[:, cs], k_in[:, cs]), 0.0)
            outs[h][i] = _mm(att, vi[:, cs]) + _mm_nt(q_st[:, cs], stTs[h])
            stTs[h] = stTs[h] * e_l[:, cs] + _mm_tn(vi[:, cs], k_st[:, cs])
    o = jnp.concatenate([jnp.concatenate(outs[h], axis=0) for h in range(nh)], axis=1)
    return o, stTs


def _hg_specs(nb, rev_order, f_col):
    R = HG_ROWS
    gw = HG_HPS * HG_EXPAND
    ng = HG_HEADS // HG_HPS

    def bidx(s):
        return nb - 1 - s if rev_order else s

    def col(base):
        return pl.BlockSpec((R, gw), lambda h, s: (bidx(s), base * ng + h))

    out = pl.BlockSpec((R, gw), lambda h, s: (bidx(s), h))
    lb = pl.BlockSpec((1, gw), lambda h, s: (0, h))
    st = pl.BlockSpec((1, 1, HG_HPS, HG_EXPAND, HG_EXPAND), lambda h, s: (h, bidx(s), 0, 0, 0))
    return col(0), col(f_col), col(3), out, lb, st


def hg_scan_fwd(name, u, lb, rev, comm=None):
    S = u.shape[0]
    nb = S // HG_ROWS
    ng = HG_HEADS // HG_HPS
    q_s, f_s, i_s, o_s, lb_s, st_s = _hg_specs(nb, rev, 2 if rev else 1)

    def body(uq, uf, ui, lb_ref, o_ref, st_ref, state):
        @pl.when(pl.program_id(1) == 0)
        def _():
            state[...] = jnp.zeros_like(state)

        st_ref[0, 0] = state[...]
        o, st = _hg_block(rev, [state[h] for h in range(HG_HPS)], uq[...], uf[...], ui[...], lb_ref[...])
        o_ref[...] = o
        for h in range(HG_HPS):
            state[h] = st[h]

    (o, st), got = call_with_comm(
        body, comm, name=name, grid=(ng, nb), in_specs=[q_s, f_s, i_s, lb_s], out_specs=[o_s, st_s],
        out_shape=[jax.ShapeDtypeStruct((S, HG_W), F32),
                   jax.ShapeDtypeStruct((ng, nb, HG_HPS, HG_EXPAND, HG_EXPAND), F32)],
        scratch_shapes=[pltpu.VMEM((HG_HPS, HG_EXPAND, HG_EXPAND), F32)],
        semantics=("parallel", "arbitrary"), args=(u, u, u, lb))
    return o, st, got


def hg_scan_bwd(name, u, lb, states, do, rev, comm=None):
    S = u.shape[0]
    nb = S // HG_ROWS
    ng = HG_HEADS // HG_HPS
    q_s, f_s, i_s, o_s, lb_s, st_s = _hg_specs(nb, not rev, 2 if rev else 1)

    def body(uq, uf, ui, lb_ref, st_ref, do_ref, dq_ref, df_ref, di_ref, dlb_ref, dstate):
        @pl.when(pl.program_id(1) == 0)
        def _():
            dstate[...] = jnp.zeros_like(dstate)
            dlb_ref[...] = jnp.zeros_like(dlb_ref)

        _, vjp = jax.vjp(functools.partial(_hg_block, rev), [st_ref[0, 0, h] for h in range(HG_HPS)],
                         uq[...], uf[...], ui[...], lb_ref[...])
        dst, dq, df, di, dlb = vjp((do_ref[...], [dstate[h] for h in range(HG_HPS)]))
        dq_ref[...] = dq
        df_ref[...] = df
        di_ref[...] = di
        dlb_ref[...] += dlb
        for h in range(HG_HPS):
            dstate[h] = dst[h]

    outs, got = call_with_comm(
        body, comm, name=name, grid=(ng, nb), in_specs=[q_s, f_s, i_s, lb_s, st_s, o_s],
        out_specs=[o_s, o_s, o_s, lb_s],
        out_shape=[jax.ShapeDtypeStruct((S, HG_W), F32)] * 3 + [jax.ShapeDtypeStruct((1, HG_W), F32)],
        scratch_shapes=[pltpu.VMEM((HG_HPS, HG_EXPAND, HG_EXPAND), F32)],
        semantics=("parallel", "arbitrary"), args=(u, u, u, lb, states, do))
    return (*outs, got)


def _hg_lb(hgrn_lb, layer):
    m = jnp.max(hgrn_lb, axis=0, keepdims=True)
    e = jnp.exp(hgrn_lb - m)
    sm = e / jnp.sum(e, axis=0, keepdims=True)
    lb = jnp.zeros_like(sm[0:1])
    for i in range(1, layer + 1):
        lb = lb + sm[i:i + 1]
    return lb


def _hg_combine(of, ob, gate, ng):
    return _rms_groups(of + ob, ng, HG_EXPAND) * _silu(gate)


def hg_forward(x, hn, w, layer, comm=None):
    comm = comm or {}
    u = matmul("hg_in", hn, w["hg_w_in"])
    (lb,) = smallcall("hg_lb", lambda t: (_hg_lb(t, layer),), [w["hgrn_lb"]], [(1, HG_W)])
    of, stf, got_f = hg_scan_fwd("hg_scan_f", u, lb, False, comm.get("hg_scan_f"))
    ob, stb, got_b = hg_scan_fwd("hg_scan_b", u, lb, True, comm.get("hg_scan_b"))
    (og,), _ = rowcall("hg_combine", lambda of, ob, gate, ng: ((_hg_combine(of, ob, gate, ng),), ()),
                       [of, ob, _col(u, HG_W, 4)], [w["hg_norm_g"]], [(HG_W, BF16)], tb=256)
    out = matmul("hg_out", og, w["hg_w_out"], res=x)
    return out, dict(hn=hn, u=u, lb=lb, of=of, ob=ob, stf=stf, stb=stb, og=og, got=dict(hg_scan_f=got_f, hg_scan_b=got_b))


def hg_backward(dy, sv, w, layer, comm=None):
    comm = comm or {}
    u, lb = sv["u"], sv["lb"]
    dog = matmul("hg_out_dx", dy, w["hg_w_out"], mode="nt")
    g_w_out = matmul("hg_out_dw", sv["og"], dy, mode="tn")

    def comb_bwd(of, ob, gate, dog, ng):
        _, vjp = jax.vjp(_hg_combine, of, ob, gate, ng)
        dof, _, dgate, dng = vjp(dog)
        return (dof, dgate), (dng,)

    (do, dgate), (g_norm,) = rowcall("hg_combine_bwd", comb_bwd, [sv["of"], sv["ob"], _col(u, HG_W, 4), dog],
                                     [w["hg_norm_g"]], [(HG_W, F32)] * 2, [(1, HG_W)], tb=256)
    dqf, dff, dif, dlbf, got_f = hg_scan_bwd("hg_scan_f_bwd", u, lb, sv["stf"], do, False, comm.get("hg_scan_f_bwd"))
    dqb, dfb, dib, dlbb, _ = hg_scan_bwd("hg_scan_b_bwd", u, lb, sv["stb"], do, True)

    def gather(dqf, dqb, dff, dfb, dif, dib, dgate):
        return (jnp.concatenate([dqf + dqb, dff, dfb, dif + dib, dgate], axis=1),), ()

    (du,), _ = rowcall("hg_du", gather, [dqf, dqb, dff, dfb, dif, dib, dgate], [], [(HG_IN, BF16)], tb=256)

    def lb_bwd(t, dlbf, dlbb):
        _, vjp = jax.vjp(lambda t: _hg_lb(t, layer), t)
        return vjp(dlbf + dlbb)

    (g_lb,) = smallcall("hg_lb_bwd", lb_bwd, [w["hgrn_lb"], dlbf, dlbb], [(DEPTH, HG_W)])
    hn = sv["hn"]
    g_w_in = matmul("hg_in_dw", hn, du, mode="tn")
    dhn = matmul("hg_in_dx", du, w["hg_w_in"], mode="nt")
    return dhn, dict(hg_w_in=g_w_in[None], hg_norm_g=g_norm, hg_w_out=g_w_out[None], hgrn_lb=g_lb,
                     got=dict(hg_scan_f_bwd=got_f))


def _rope_tables(S):
    t = np.arange(S)
    row = (t // GRID_W).astype(np.float32)
    col = (t % GRID_W).astype(np.float32)
    inv = (ROPE_THETA ** (-np.arange(0, ROPE_AXIS, 2, dtype=np.float32) / ROPE_AXIS)).astype(np.float32)
    ar = jnp.asarray(row)[:, None] * jnp.asarray(inv)[None, :]
    ac = jnp.asarray(col)[:, None] * jnp.asarray(inv)[None, :]
    cos = jnp.concatenate([jnp.cos(ar), jnp.cos(ar), jnp.cos(ac), jnp.cos(ac)], axis=1)
    sin = jnp.concatenate([-jnp.sin(ar), jnp.sin(ar), -jnp.sin(ac), jnp.sin(ac)], axis=1)
    return cos.astype(F32), sin.astype(F32)


@jax.custom_vjp
def _swap_halves_of_axes(x):
    h = ROPE_AXIS // 2
    lane = lax.broadcasted_iota(jnp.int32, x.shape, 1)
    return jnp.where((lane & h) == 0, pltpu.roll(x, AT_HD - h, 1), pltpu.roll(x, h, 1))


_swap_halves_of_axes.defvjp(lambda x: (_swap_halves_of_axes(x), None), lambda _, g: (_swap_halves_of_axes(g),))


def _rope(x, cos, sin):
    return x * cos + _swap_halves_of_axes(x) * sin


def _at_pre(uq, uk, cos, sin, qg, kg):
    qs, ks = [], []
    for h in range(AT_HEADS):
        qs.append(_rope(_rms(uq[:, h * AT_HD:(h + 1) * AT_HD], qg), cos, sin) * (AT_HD ** -0.5))
    for h in range(AT_KV):
        ks.append(_rope(_rms(uk[:, h * AT_HD:(h + 1) * AT_HD], kg), cos, sin))
    return jnp.concatenate(qs, axis=1), jnp.concatenate(ks, axis=1)


def _stack_heads(x):
    return jnp.concatenate([x[:, :AT_HD], x[:, AT_HD:]], axis=0)


def _unstack_heads(x):
    t = x.shape[0] // 2
    return jnp.concatenate([x[:t], x[t:]], axis=1)


def at_flash_fwd(q, k, u):
    S = q.shape[0]
    tq, tk = _pick(S, 512), _pick(S, 4096)
    nq, nk = S // tq, S // tk
    gw = AT_GRP * AT_HD

    def body(q_ref, k_ref, v_ref, o_ref, lse_ref, m_s, l_s, acc):
        j = pl.program_id(2)

        @pl.when(j == 0)
        def _():
            m_s[...] = jnp.full_like(m_s, NEG_BIG)
            l_s[...] = jnp.zeros_like(l_s)
            acc[...] = jnp.zeros_like(acc)

        s = _mm_nt(_stack_heads(q_ref[...]), k_ref[...])
        m_new = jnp.maximum(m_s[...], jnp.max(s, axis=-1, keepdims=True))
        alpha = jnp.exp(m_s[...] - m_new)
        p = jnp.exp(s - m_new)
        l_s[...] = alpha * l_s[...] + jnp.sum(p, axis=-1, keepdims=True)
        acc[...] = alpha * acc[...] + _mm(p, v_ref[...])
        m_s[...] = m_new

        @pl.when(j == nk - 1)
        def _():
            o_ref[...] = _unstack_heads(acc[...] / l_s[...])
            lse = m_s[...] + jnp.log(l_s[...])
            lse_ref[0, 0] = lse[:tq]
            lse_ref[0, 1] = lse[tq:]

    return pl.pallas_call(
        body, name="at_flash_fwd", grid=(AT_KV, nq, nk),
        in_specs=[pl.BlockSpec((tq, gw), lambda h, i, j: (i, h)),
                  pl.BlockSpec((tk, AT_HD), lambda h, i, j: (j, h)),
                  pl.BlockSpec((tk, AT_HD), lambda h, i, j: (j, (AT_QW + AT_KW) // AT_HD + h))],
        out_specs=[pl.BlockSpec((tq, gw), lambda h, i, j: (i, h)),
                   pl.BlockSpec((1, AT_GRP, tq, 1), lambda h, i, j: (h, 0, i, 0))],
        out_shape=[jax.ShapeDtypeStruct((S, AT_QW), F32), jax.ShapeDtypeStruct((AT_KV, AT_GRP, S, 1), F32)],
        scratch_shapes=[pltpu.VMEM((2 * tq, 1), F32), pltpu.VMEM((2 * tq, 1), F32), pltpu.VMEM((2 * tq, AT_HD), F32)],
        compiler_params=_cp(("parallel", "parallel", "arbitrary")),
    )(q, k, u)


def at_flash_bwd(q, k, u, o, lse, do):
    S = q.shape[0]
    tq, tk = _pick(S, 128), _pick(S, 4096)
    nq, nk = S // tq, S // tk
    gw = AT_GRP * AT_HD

    def body(q_ref, k_ref, v_ref, o_ref, lse_ref, do_ref, dq_ref, dk_ref, dv_ref, dk_acc, dv_acc):
        j, i = pl.program_id(1), pl.program_id(2)

        @pl.when(i == 0)
        def _():
            dk_acc[...] = jnp.zeros_like(dk_acc)
            dv_acc[...] = jnp.zeros_like(dv_acc)

        q2 = _stack_heads(q_ref[...])
        do_blk = do_ref[...]
        do2 = _stack_heads(do_blk)
        delta = _stack_heads(do_blk * o_ref[...])
        delta = jnp.sum(delta, axis=-1, keepdims=True)
        kb, vb = k_ref[...], v_ref[...]
        p = jnp.exp(_mm_nt(q2, kb) - jnp.concatenate([lse_ref[0, 0], lse_ref[0, 1]], axis=0))
        dv_acc[...] += _mm_tn(p, do2)
        ds = p * (_mm_nt(do2, vb) - delta)
        dk_acc[...] += _mm_tn(ds, q2)
        dq = _unstack_heads(_mm(ds, kb))
        rows = pl.ds(pl.multiple_of(i * tq, tq), tq)

        @pl.when(j == 0)
        def _():
            dq_ref[rows, :] = dq

        @pl.when(j > 0)
        def _():
            dq_ref[rows, :] += dq

        @pl.when(i == nq - 1)
        def _():
            dk_ref[...] = dk_acc[...]
            dv_ref[...] = dv_acc[...]

    return pl.pallas_call(
        body, name="at_flash_bwd", grid=(AT_KV, nk, nq),
        in_specs=[pl.BlockSpec((tq, gw), lambda h, j, i: (i, h)),
                  pl.BlockSpec((tk, AT_HD), lambda h, j, i: (j, h)),
                  pl.BlockSpec((tk, AT_HD), lambda h, j, i: (j, (AT_QW + AT_KW) // AT_HD + h)),
                  pl.BlockSpec((tq, gw), lambda h, j, i: (i, h)),
                  pl.BlockSpec((1, AT_GRP, tq, 1), lambda h, j, i: (h, 0, i, 0)),
                  pl.BlockSpec((tq, gw), lambda h, j, i: (i, h))],
        out_specs=[pl.BlockSpec((S, gw), lambda h, j, i: (0, h)),
                   pl.BlockSpec((tk, AT_HD), lambda h, j, i: (j, h)),
                   pl.BlockSpec((tk, AT_HD), lambda h, j, i: (j, h))],
        out_shape=[jax.ShapeDtypeStruct((S, AT_QW), F32), jax.ShapeDtypeStruct((S, AT_KW), F32),
                   jax.ShapeDtypeStruct((S, AT_KW), F32)],
        scratch_shapes=[pltpu.VMEM((tk, AT_HD), F32), pltpu.VMEM((tk, AT_HD), F32)],
        compiler_params=_cp(("parallel", "arbitrary", "arbitrary")),
    )(q, k, u, o, lse, do)


def at_forward(x, hn, w, comm=None):
    S = x.shape[0]
    u = matmul("at_in", hn, w["at_w_in"])
    cos, sin = _rope_tables(S)
    (q, k), _ = rowcall("at_pre", lambda uq, uk, c, s, qg, kg: (_at_pre(uq, uk, c, s, qg, kg), ()),
                        [_col(u, AT_QW, 0), _col(u, AT_KW, 2), cos, sin], [w["at_q_norm_g"], w["at_k_norm_g"]],
                        [(AT_QW, BF16), (AT_KW, BF16)], tb=256)
    o, lse = at_flash_fwd(q, k, u)
    (og,), _ = rowcall("at_gate", lambda o, gate: ((o * _silu(gate),), ()), [o, _col(u, AT_QW, 2)], [],
                       [(AT_QW, BF16)], tb=256)
    out = matmul("at_out", og, w["at_w_out"], res=x)
    return out, dict(hn=hn, u=u, cos=cos, sin=sin, q=q, k=k, o=o, lse=lse, og=og)


def at_backward(dy, sv, w, comm=None):
    u = sv["u"]
    dog = matmul("at_out_dx", dy, w["at_w_out"], mode="nt")
    g_w_out = matmul("at_out_dw", sv["og"], dy, mode="tn")

    def gate_bwd(o, gate, dog):
        _, vjp = jax.vjp(lambda o, gate: o * _silu(gate), o, gate)
        return vjp(dog), ()

    (do, dgate), _ = rowcall("at_gate_bwd", gate_bwd, [sv["o"], _col(u, AT_QW, 2), dog], [],
                             [(AT_QW, F32)] * 2, tb=256)
    dq, dk, dv = at_flash_bwd(sv["q"], sv["k"], u, sv["o"], sv["lse"], do)

    def pre_bwd(uq, uk, cos, sin, dq, dk, dv, dgate, qg, kg):
        _, vjp = jax.vjp(lambda uq, uk, qg, kg: _at_pre(uq, uk, cos, sin, qg, kg), uq, uk, qg, kg)
        duq, duk, dqg, dkg = vjp((dq, dk))
        return (jnp.concatenate([duq, duk, dv, dgate], axis=1),), (dqg, dkg)

    (du,), (g_qg, g_kg) = rowcall(
        "at_pre_bwd", pre_bwd, [_col(u, AT_QW, 0), _col(u, AT_KW, 2), sv["cos"], sv["sin"], dq, dk, dv, dgate],
        [w["at_q_norm_g"], w["at_k_norm_g"]], [(AT_IN, BF16)], [(1, AT_HD), (1, AT_HD)], tb=128)
    hn = sv["hn"]
    g_w_in = matmul("at_in_dw", hn, du, mode="tn")
    dhn = matmul("at_in_dx", du, w["at_w_in"], mode="nt")
    return dhn, dict(at_w_in=g_w_in[None], at_q_norm_g=g_qg, at_k_norm_g=g_kg, at_w_out=g_w_out[None])


def _t5_bucket_np(rel):
    half = REL_BUCKETS // 2
    exact = half // 2
    n = np.abs(rel)
    large = exact + (np.log(np.maximum(n, 1).astype(np.float32) / exact)
                     / math.log(REL_MAX_DIST / exact) * (half - exact)).astype(np.int32)
    large = np.minimum(large, half - 1)
    return np.where(rel > 0, half, 0) + np.where(n < exact, n, large)


def _dl_tq(S, dil):
    return min(128, S // dil)


def _dl_bias_maps(tq, dil):
    W = tq + 2 * DL_STEPS
    i = np.arange(tq)[:, None]
    wdx = np.arange(W)[None, :]
    dm = wdx - DL_STEPS - i
    bucket = _t5_bucket_np(dm * dil).reshape(-1).astype(np.int32)
    band = np.where(np.abs(dm) <= DL_STEPS, 0.0, NEG_BIG).reshape(1, -1).astype(np.float32)
    onehot = (jnp.asarray(bucket)[None, :] == jnp.arange(REL_BUCKETS, dtype=jnp.int32)[:, None]).astype(F32)
    return onehot, jnp.asarray(band)


def _dl_attend(q, kwin, vwin, T, valid):
    tq = q.shape[0]
    os, ls = [], []
    for h in range(DL_HEADS):
        sl = slice(h * DL_HD, (h + 1) * DL_HD)
        s = _mm_nt(q[:, sl] * (DL_HD ** -0.5), kwin[:, sl]) + T[h]
        s = jnp.where(valid, s, NEG_BIG)
        m = lax.stop_gradient(jnp.max(s, axis=-1, keepdims=True))
        e = jnp.exp(s - m)
        den = jnp.sum(e, axis=-1, keepdims=True)
        lse = m + jnp.log(den)
        p = e * (1.0 / den)
        os.append(_mm(p, vwin[:, sl]))
        ls.append(jnp.broadcast_to(lse, (tq, DL_HD)))
    return jnp.concatenate(os, axis=1), jnp.concatenate(ls, axis=1)


def _dl_specs(tq, Ls):
    nb = Ls // tq
    hs = DL_STEPS
    per = tq // hs
    nh = Ls // hs

    def main(c):
        return pl.BlockSpec((tq, DL_W), lambda r, i: (r * nb + i, c))

    def prev(c):
        return pl.BlockSpec((hs, DL_W), lambda r, i: (r * nh + jnp.maximum(i * per - 1, 0), c))

    def nxt(c):
        return pl.BlockSpec((hs, DL_W), lambda r, i: (r * nh + jnp.minimum((i + 1) * per, nh - 1), c))

    return nb, main, prev, nxt


def _dl_valid(i, tq, Ls):
    W = tq + 2 * DL_STEPS
    mk = i * tq - DL_STEPS + lax.broadcasted_iota(jnp.int32, (1, W), 1)
    return (mk >= 0) & (mk < Ls)


def dl_attn_fwd(gi, dil, u, T):
    S = u.shape[0]
    Ls = S // dil
    tq = _dl_tq(S, dil)
    nb, main, prev, nxt = _dl_specs(tq, Ls)
    out = main(0)

    def body(q_ref, kp, kc, kn, vp, vc, vn, T_ref, o_ref, l_ref):
        kwin = jnp.concatenate([kp[...], kc[...], kn[...]], axis=0)
        vwin = jnp.concatenate([vp[...], vc[...], vn[...]], axis=0)
        o, l = _dl_attend(q_ref[...], kwin, vwin, T_ref[...], _dl_valid(pl.program_id(1), tq, Ls))
        o_ref[...] = o
        l_ref[...] = l

    o, l = pl.pallas_call(
        body, name=f"dl_attn_fwd{gi}", grid=(dil, nb),
        in_specs=[main(0), prev(1), main(1), nxt(1), prev(2), main(2), nxt(2),
                  pl.BlockSpec(T.shape, _const_map(3))],
        out_specs=[out, out],
        out_shape=[jax.ShapeDtypeStruct((S, DL_W), F32)] * 2,
        compiler_params=_cp(("parallel", "parallel")),
    )(u, u, u, u, u, u, u, T)
    return o, l


def dl_attn_bwd(gi, dil, u, T, do, dl, dgate=None):
    S = u.shape[0]
    Ls = S // dil
    tq = _dl_tq(S, dil)
    hs = DL_STEPS
    W = tq + 2 * hs
    nb, main, prev, nxt = _dl_specs(tq, Ls)
    out = main(0)
    win = pl.BlockSpec((1, W, DL_W), lambda r, i: (r * nb + i, 0, 0))

    def body(q_ref, kp, kc, kn, vp, vc, vn, T_ref, do_ref, dl_ref, dq_ref, dkw_ref, dvw_ref, dT_ref):
        first = (pl.program_id(0) == 0) & (pl.program_id(1) == 0)

        @pl.when(first)
        def _():
            dT_ref[...] = jnp.zeros_like(dT_ref)

        kwin = jnp.concatenate([kp[...], kc[...], kn[...]], axis=0)
        vwin = jnp.concatenate([vp[...], vc[...], vn[...]], axis=0)
        valid = _dl_valid(pl.program_id(1), tq, Ls)
        _, vjp = jax.vjp(lambda q, k, v, T: _dl_attend(q, k, v, T, valid), q_ref[...], kwin, vwin, T_ref[...])
        dq, dkw, dvw, dT = vjp((do_ref[...], dl_ref[...]))
        dq_ref[...] = dq
        dkw_ref[0] = dkw
        dvw_ref[0] = dvw
        dT_ref[...] += dT

    dq, dkw, dvw, dT = pl.pallas_call(
        body, name=f"dl_attn_bwd{gi}", grid=(dil, nb),
        in_specs=[main(0), prev(1), main(1), nxt(1), prev(2), main(2), nxt(2),
                  pl.BlockSpec(T.shape, _const_map(3)), out, out],
        out_specs=[out, win, win, pl.BlockSpec(T.shape, _const_map(3))],
        out_shape=[jax.ShapeDtypeStruct((S, DL_W), F32),
                   jax.ShapeDtypeStruct((dil * nb, W, DL_W), F32),
                   jax.ShapeDtypeStruct((dil * nb, W, DL_W), F32),
                   jax.ShapeDtypeStruct(T.shape, F32)],
        compiler_params=_cp(("arbitrary", "arbitrary")),
    )(u, u, u, u, u, u, u, T, do, dl)

    per = tq // hs
    n_out = 3 if dgate is None else 4

    def fold(*refs):
        dq_ref, kc, kp, kn, vc, vp, vn = refs[:7]
        du_ref = refs[-1]
        i = pl.program_id(1)
        has_p = (i > 0).astype(F32)
        has_n = (i < nb - 1).astype(F32)
        du_ref[:, 0:DL_W] = dq_ref[...].astype(BF16)
        for c, (c_ref, p_ref, n_ref) in enumerate(((kc, kp, kn), (vc, vp, vn)), start=1):
            mid = c_ref[0, hs:hs + tq, :]
            top = mid[0:hs] + p_ref[0] * has_p
            bot = mid[tq - hs:tq] + n_ref[0] * has_n
            parts = [top, bot] if tq == 2 * hs else ([top, mid[hs:tq - hs], bot] if tq > 2 * hs else [top + n_ref[0] * has_n])
            du_ref[:, c * DL_W:(c + 1) * DL_W] = jnp.concatenate(parts, axis=0).astype(BF16)
        if dgate is not None:
            du_ref[:, 3 * DL_W:4 * DL_W] = refs[7][...].astype(BF16)

    wfull = pl.BlockSpec((1, W, DL_W), lambda r, i: (r * nb + i, 0, 0))
    wprev = pl.BlockSpec((1, hs, DL_W), lambda r, i: (r * nb + jnp.maximum(i - 1, 0), per + 1, 0))
    wnext = pl.BlockSpec((1, hs, DL_W), lambda r, i: (r * nb + jnp.minimum(i + 1, nb - 1), 0, 0))
    extra_specs, extra_args = ([], []) if dgate is None else ([out], [dgate])
    du = pl.pallas_call(
        fold, name=f"dl_fold{gi}", grid=(dil, nb),
        in_specs=[out, wfull, wprev, wnext, wfull, wprev, wnext] + extra_specs,
        out_specs=pl.BlockSpec((tq, n_out * DL_W), lambda r, i: (r * nb + i, 0)),
        out_shape=jax.ShapeDtypeStruct((S, n_out * DL_W), BF16),
        compiler_params=_cp(("parallel", "parallel")),
    )(dq, dkw, dkw, dkw, dvw, dvw, dvw, *extra_args)
    return du, dT


def _dl_merge(o0, o1, o2, l0, l1, l2, gate):
    m = jnp.maximum(jnp.maximum(l0, l1), l2)
    e0, e1, e2 = jnp.exp(l0 - m), jnp.exp(l1 - m), jnp.exp(l2 - m)
    den = e0 + e1 + e2
    return ((e0 * o0 + e1 * o1 + e2 * o2) / den) * _silu(gate)


DL_DILS = tuple(d for _, d in DL_PAIRS)


def _dl_group_weights(w_in):
    g3 = 3 * DL_W
    return [jnp.concatenate([w_in[:, :g3], w_in[:, 3 * g3:]], axis=1), w_in[:, g3:2 * g3], w_in[:, 2 * g3:3 * g3]]


def dl_forward(x, hns, w, comm=None):
    S = x.shape[0]
    wg = _dl_group_weights(w["dl_w_in"])
    rbT = w["rel_bias"].T
    us, os, ls, Ts, maps = [], [], [], [], []
    for gi, dil in enumerate(DL_DILS):
        u = matmul(f"dl_in{gi}", hns[gi], wg[gi])
        tq = _dl_tq(S, dil)
        W = tq + 2 * DL_STEPS
        onehot, band = _dl_bias_maps(tq, dil)
        (T,) = smallcall(f"dl_bias{gi}", lambda rbT, oh, band: (_mm_exact(rbT, oh) + band,), [rbT, onehot, band],
                         [(DL_HEADS, tq * W)])
        T = T.reshape(DL_HEADS, tq, W)
        o, l = dl_attn_fwd(gi, dil, u, T)
        us.append(u)
        os.append(o)
        ls.append(l)
        Ts.append(T)
        maps.append(onehot)
    rows = [a if d == 1 else _perm(a, d) for a, d in zip(os + ls, DL_DILS * 2)] + [_col(us[0], DL_W, 3)]
    (og,), _ = rowcall("dl_merge", lambda *a: ((_dl_merge(*a),), ()), rows, [], [(DL_W, BF16)], tb=256)
    out = matmul("dl_out", og, w["dl_w_out"], res=x)
    return out, dict(hns=hns, us=us, os=os, ls=ls, Ts=Ts, maps=maps, og=og, wg=wg)


def dl_backward(dy, sv, w, comm=None):
    us = sv["us"]
    dog = matmul("dl_out_dx", dy, w["dl_w_out"], mode="nt")
    g_w_out = matmul("dl_out_dw", sv["og"], dy, mode="tn")

    def merge_bwd(o0, o1, o2, l0, l1, l2, gate, dog):
        _, vjp = jax.vjp(_dl_merge, o0, o1, o2, l0, l1, l2, gate)
        return vjp(dog), ()

    rows = [a if d == 1 else _perm(a, d) for a, d in zip(sv["os"] + sv["ls"], DL_DILS * 2)] + [_col(us[0], DL_W, 3), dog]
    grads7, _ = rowcall("dl_merge_bwd", merge_bwd, rows, [], [(DL_W, F32, d) for d in DL_DILS * 2] + [(DL_W, F32)], tb=256)
    dos, dls, dgate = grads7[0:3], grads7[3:6], grads7[6]
    g_rbT, g_ws, dhns = None, [], []
    for gi, dil in enumerate(DL_DILS):
        du, dT = dl_attn_bwd(gi, dil, us[gi], sv["Ts"][gi], dos[gi], dls[gi], dgate if gi == 0 else None)
        (g,) = smallcall(f"dl_bias_bwd{gi}", lambda dT, oh: (_mm_nt_exact(dT, oh),),
                         [dT.reshape(DL_HEADS, -1), sv["maps"][gi]], [(DL_HEADS, REL_BUCKETS)])
        g_rbT = g if g_rbT is None else g_rbT + g
        g_ws.append(matmul(f"dl_in_dw{gi}", sv["hns"][gi], du, mode="tn"))
        dhns.append(matmul(f"dl_in_dx{gi}", du, sv["wg"][gi], mode="nt"))
    g3 = 3 * DL_W
    g_w_in = jnp.concatenate([g_ws[0][:, :g3], g_ws[1], g_ws[2], g_ws[0][:, g3:]], axis=1)
    return tuple(dhns), dict(dl_w_in=g_w_in[None], dl_w_out=g_w_out[None], rel_bias=g_rbT.T)


_FWD = (ssd_forward, hg_forward, at_forward, dl_forward)
_BWD = (ssd_backward, hg_backward, at_backward, dl_backward)


def _norm_dils(layer):
    return DL_DILS if layer % 4 == 3 else (1,)


class NoExchange:
    def fwd_plans(self, layer, w):
        return None

    def fwd_done(self, layer, got, w):
        pass

    def bwd_plans(self, layer, grads):
        return None

    def bwd_done(self, layer, got):
        pass


def local_step(x, tgt, w, sched=None):
    sched = sched or NoExchange()
    saved = []
    h = x
    for layer in range(DEPTH):
        hn = norm_fwd(f"norm{layer}", h, w["norm_g"][layer:layer + 1], _norm_dils(layer))
        extra = (layer,) if layer % 4 == 1 else ()
        h_next, sv = _FWD[layer % 4](h, hn, w, *extra, comm=sched.fwd_plans(layer, w))
        sched.fwd_done(layer, sv.get("got", {}), w)
        saved.append((h, sv))
        h = h_next
    loss, dh, g_final = loss_head(h, tgt, w["final_g"].reshape(1, D_MODEL))
    grads = {}
    g_norm = [None] * DEPTH
    for layer in reversed(range(DEPTH)):
        h_in, sv = saved[layer]
        extra = (layer,) if layer % 4 == 1 else ()
        dhn, g = _BWD[layer % 4](dh, sv, w, *extra, comm=sched.bwd_plans(layer, grads))
        sched.bwd_done(layer, g.pop("got", {}))
        grads.update(g)
        dh, g_norm[layer] = norm_bwd(f"norm{layer}_bwd", h_in, w["norm_g"][layer:layer + 1], dhn, dh, _norm_dils(layer))
    grads["norm_g"] = jnp.concatenate(g_norm, axis=0)
    grads["final_g"] = g_final.reshape(D_MODEL)
    grads["ssd_d"] = jnp.sum(grads.pop("ssd_d_exp").reshape(SSD_HEADS, SSD_HEADDIM), axis=1)[None]
    return loss, dh, grads


IN_NAMES = ("ssd_w_in", "hg_w_in", "at_w_in", "dl_w_in")
OUT_NAMES = ("ssd_w_out", "hg_w_out", "at_w_out", "dl_w_out")
IN_COLS = (SSD_IN // 4, HG_IN // 4, AT_IN // 4, DL_IN // 4)
OUT_ROWS = (SSD_DI // 4, HG_W // 4, AT_QW // 4, DL_W // 4)
PACK_IN = sum(IN_COLS)
PACK_OUT = sum(OUT_ROWS)
N_CHIPS = 4
N_DEV = 8
HBM = pl.BlockSpec(memory_space=pl.ANY)


def _mesh_pos():
    return lax.axis_index("x"), lax.axis_index("y"), lax.axis_index("c")


def _other_chips(x, y):
    return [(1 - x, y), (x, 1 - y), (1 - x, 1 - y)]


def _half_rows(half, n):
    return pl.ds(pl.multiple_of(half * n, n), n)


def _remote(src, dst, send, recv, k, to):
    return pltpu.make_async_remote_copy(src_ref=src, dst_ref=dst, send_sem=send.at[k], recv_sem=recv.at[k],
                                        device_id=to, device_id_type=MESH)


def gather_plan(packs, whole=()):
    arrs = list(packs) + list(whole)
    n_half = len(packs)

    def pieces(ins, outs):
        x, y, c = _mesh_pos()
        for a, (src, dst) in enumerate(zip(ins, outs)):
            h = src.shape[0] // 2 if a < n_half else None
            for j, (px, py) in enumerate(_other_chips(x, y)):
                yield a, j, src, dst, h, (x, y, c), (px, py)

    def start(ins, outs, send, recv):
        for a, j, src, dst, h, (x, y, c), (px, py) in pieces(ins, outs):
            me = 2 * x + y
            if h is None:
                _remote(src, dst.at[me], send, recv, 6 * a + j, (px, py, c)).start()
            else:
                _remote(src.at[_half_rows(c, h)], dst.at[me, _half_rows(c, h)], send, recv, 6 * a + j, (px, py, c)).start()

    def mid(ins, outs, send, recv):
        for a, j, src, dst, h, (x, y, c), (px, py) in pieces(ins, outs):
            kp = 2 * px + py
            if h is None:
                _remote(src, dst.at[kp], send, recv, 6 * a + j, (px, py, c)).wait_recv()
            else:
                got = dst.at[kp, _half_rows(c, h)]
                _remote(src.at[_half_rows(c, h)], got, send, recv, 6 * a + j, (px, py, c)).wait_recv()
                _remote(got, got, send, recv, 6 * a + 3 + j, (x, y, 1 - c)).start()

    def finish(ins, outs, send, recv):
        for a, j, src, dst, h, (x, y, c), (px, py) in pieces(ins, outs):
            me, kp = 2 * x + y, 2 * px + py
            if h is None:
                _remote(src, dst.at[me], send, recv, 6 * a + j, (px, py, c)).wait_send()
            else:
                theirs = dst.at[kp, _half_rows(1 - c, h)]
                _remote(theirs, theirs, send, recv, 6 * a + 3 + j, (x, y, 1 - c)).wait_recv()
                _remote(src.at[_half_rows(c, h)], dst.at[me, _half_rows(c, h)], send, recv, 6 * a + j, (px, py, c)).wait_send()
                mine = dst.at[kp, _half_rows(c, h)]
                _remote(mine, mine, send, recv, 6 * a + 3 + j, (x, y, 1 - c)).wait_send()

    return dict(ins=arrs, out_shape=[jax.ShapeDtypeStruct((N_CHIPS,) + a.shape, a.dtype) for a in arrs],
                n_sems=6 * len(arrs), start=start, mid=mid, finish=finish)


def scatter_plan(halves):
    def copies(ins, outs, send, recv):
        x, y, c = _mesh_pos()
        for a, (src, dst) in enumerate(zip(ins, outs)):
            for j, (px, py) in enumerate(_other_chips(x, y)):
                yield _remote(src.at[2 * px + py], dst.at[j], send, recv, 3 * a + j, (px, py, c))

    def start(ins, outs, send, recv):
        for cp in copies(ins, outs, send, recv):
            cp.start()

    def finish(ins, outs, send, recv):
        for cp in copies(ins, outs, send, recv):
            cp.wait()

    return dict(ins=list(halves), out_shape=[jax.ShapeDtypeStruct((3,) + a.shape[1:], a.dtype) for a in halves],
                n_sems=3 * len(halves), start=start, mid=None, finish=finish)


def run_exchange(name, plan):
    n_in = len(plan["ins"])

    def body(*refs):
        ins, outs = refs[:n_in], refs[n_in:-2]
        send, recv = refs[-2], refs[-1]
        plan["start"](ins, outs, send, recv)
        if plan["mid"] is not None:
            plan["mid"](ins, outs, send, recv)
        plan["finish"](ins, outs, send, recv)

    return pl.pallas_call(
        body, name=name, in_specs=[HBM] * n_in, out_specs=[HBM] * len(plan["out_shape"]), out_shape=plan["out_shape"],
        scratch_shapes=[pltpu.SemaphoreType.DMA((plan["n_sems"],))] * 2,
        compiler_params=pltpu.CompilerParams(has_side_effects=True),
    )(*plan["ins"])


def swap_halves(name, g_in, g_out):
    h_in, h_out = g_in.shape[1] // 2, g_out.shape[1] // 2

    def body(gi, go, ri, ro, send, recv):
        x, y, c = _mesh_pos()
        sib = (x, y, 1 - c)

        def rows(half, n):
            return pl.ds(pl.multiple_of(half * n, n), n)

        cps = [pltpu.make_async_remote_copy(src_ref=gi.at[:, rows(1 - c, h_in)], dst_ref=ri, send_sem=send.at[0],
                                            recv_sem=recv.at[0], device_id=sib, device_id_type=MESH),
               pltpu.make_async_remote_copy(src_ref=go.at[:, rows(1 - c, h_out)], dst_ref=ro, send_sem=send.at[1],
                                            recv_sem=recv.at[1], device_id=sib, device_id_type=MESH)]
        for cp in cps:
            cp.start()
        for cp in cps:
            cp.wait()

    return pl.pallas_call(
        body, name=name, in_specs=[HBM, HBM], out_specs=[HBM, HBM],
        out_shape=[jax.ShapeDtypeStruct((N_CHIPS, h_in, g_in.shape[2]), g_in.dtype),
                   jax.ShapeDtypeStruct((N_CHIPS, h_out, g_out.shape[2]), g_out.dtype)],
        scratch_shapes=[pltpu.SemaphoreType.DMA((2,)), pltpu.SemaphoreType.DMA((2,))],
        compiler_params=pltpu.CompilerParams(has_side_effects=True),
    )(g_in, g_out)


def half_add(name, g, r, c_idx, tb):
    _, rows2, C = g.shape
    h = rows2 // 2
    nb = h // tb

    def body(c_ref, g_ref, r_ref, f_ref, b_ref):
        s = g_ref[...] + r_ref[...]
        f_ref[...] = s
        b_ref[...] = s.astype(BF16)

    grid_spec = pltpu.PrefetchScalarGridSpec(
        num_scalar_prefetch=1, grid=(N_CHIPS, nb),
        in_specs=[pl.BlockSpec((1, tb, C), lambda k, i, c: (k, c[0] * nb + i, 0)),
                  pl.BlockSpec((1, tb, C), lambda k, i, c: (k, i, 0))],
        out_specs=[pl.BlockSpec((1, tb, C), lambda k, i, c: (k, i, 0))] * 2)
    return pl.pallas_call(
        body, name=name, grid_spec=grid_spec,
        out_shape=[jax.ShapeDtypeStruct((N_CHIPS, h, C), F32), jax.ShapeDtypeStruct((N_CHIPS, h, C), BF16)],
        compiler_params=_cp(("parallel", "parallel")),
    )(c_idx, g, r)


def chip_sum(name, f, r, me_idx, tb):
    _, h, C = f.shape
    nb = h // tb

    def body(me_ref, f_ref, r0, r1, r2, o_ref):
        o_ref[...] = ((f_ref[0] + r0[0].astype(F32)) + r1[0].astype(F32)) + r2[0].astype(F32)

    def slot(j):
        return pl.BlockSpec((1, tb, C), lambda i, me: (j, i, 0))

    grid_spec = pltpu.PrefetchScalarGridSpec(
        num_scalar_prefetch=1, grid=(nb,),
        in_specs=[pl.BlockSpec((1, tb, C), lambda i, me: (me[0], i, 0)), slot(0), slot(1), slot(2)],
        out_specs=pl.BlockSpec((tb, C), lambda i, me: (i, 0)))
    return pl.pallas_call(
        body, name=name, grid_spec=grid_spec, out_shape=jax.ShapeDtypeStruct((h, C), F32),
        compiler_params=_cp(("parallel",)),
    )(me_idx, f, r, r, r)


def share_halves(name, f_in, f_out):
    def body(fi, fo, oi, oo, send, recv):
        x, y, c = _mesh_pos()
        sib = (x, y, 1 - c)
        cps = [pltpu.make_async_remote_copy(src_ref=fi, dst_ref=oi, send_sem=send.at[0], recv_sem=recv.at[0],
                                            device_id=sib, device_id_type=MESH),
               pltpu.make_async_remote_copy(src_ref=fo, dst_ref=oo, send_sem=send.at[1], recv_sem=recv.at[1],
                                            device_id=sib, device_id_type=MESH)]
        for cp in cps:
            cp.start()
        for cp in cps:
            cp.wait()

    return pl.pallas_call(
        body, name=name, in_specs=[HBM, HBM], out_specs=[HBM, HBM],
        out_shape=[jax.ShapeDtypeStruct(f_in.shape, F32), jax.ShapeDtypeStruct(f_out.shape, F32)],
        scratch_shapes=[pltpu.SemaphoreType.DMA((2,)), pltpu.SemaphoreType.DMA((2,))],
        compiler_params=pltpu.CompilerParams(has_side_effects=True),
    )(f_in, f_out)


def gather_small(pack):
    def body(p, g, send, recv, lsem):
        x, y, c = _mesh_pos()
        me = 4 * x + 2 * y + c
        local = pltpu.make_async_copy(p, g.at[me], lsem)
        local.start()
        cps = []
        k = 0
        for fx in (0, 1):
            for fy in (0, 1):
                for fc in (0, 1):
                    if fx + fy + fc == 0:
                        continue
                    to = (x ^ fx, y ^ fy, c ^ fc)
                    cps.append((pltpu.make_async_remote_copy(src_ref=p, dst_ref=g.at[me], send_sem=send.at[k],
                                                             recv_sem=recv.at[k], device_id=to, device_id_type=MESH), to, k))
                    k += 1
        for cp, _, _ in cps:
            cp.start()
        for cp, to, k in cps:
            frm = 4 * to[0] + 2 * to[1] + to[2]
            pltpu.make_async_remote_copy(src_ref=p, dst_ref=g.at[frm], send_sem=send.at[k], recv_sem=recv.at[k],
                                         device_id=to, device_id_type=MESH).wait_recv()
        for cp, _, _ in cps:
            cp.wait_send()
        local.wait()

    return pl.pallas_call(
        body, name="gather_small", in_specs=[HBM], out_specs=HBM,
        out_shape=jax.ShapeDtypeStruct((N_DEV,) + pack.shape, pack.dtype),
        scratch_shapes=[pltpu.SemaphoreType.DMA((7,)), pltpu.SemaphoreType.DMA((7,)), pltpu.SemaphoreType.DMA],
        compiler_params=pltpu.CompilerParams(has_side_effects=True),
    )(pack)


def _adamw(w, g, m, v):
    m = ADAM_B1 * m + (1.0 - ADAM_B1) * g
    v = ADAM_B2 * v + (1.0 - ADAM_B2) * (g * g)
    m_hat = m / (1.0 - ADAM_B1 ** ADAM_STEP)
    v_hat = v / (1.0 - ADAM_B2 ** ADAM_STEP)
    delta = -ADAM_LR * (m_hat / (jnp.sqrt(v_hat) + ADAM_EPS) + ADAM_WD * w)
    return delta, m, v


def adamw_big(name, w, g, m, v):
    shp = w.shape
    flat = lambda a: a.reshape(shp[-2], shp[-1])
    (d, nm, nv), _ = rowcall(name, lambda w, g, m, v: (_adamw(w, g, m, v), ()), [flat(w), flat(g), flat(m), flat(v)], [],
                             [(shp[-1], F32)] * 3, tb=256)
    return d.reshape(shp), nm.reshape(shp), nv.reshape(shp)


def _pack_small(arrs):
    flat = jnp.concatenate([a.reshape(-1) for a in arrs])
    n = flat.shape[0]
    rows = -(-n // (SUBLANE * LANE)) * SUBLANE
    return jnp.pad(flat, (0, rows * LANE - n)).reshape(rows, LANE)


def _unpack_small(pack, shapes):
    flat = pack.reshape(-1)
    outs, off = [], 0
    for s in shapes:
        n = int(np.prod(s))
        outs.append(flat[off:off + n].reshape(s))
        off += n
    return outs


SMALL_NAMES = ("norm_g", "final_g", "rel_bias", "hgrn_lb", "ssd_conv_w", "ssd_conv_b", "ssd_dt_bias", "ssd_a_log",
               "ssd_d", "ssd_norm_g", "hg_norm_g", "at_q_norm_g", "at_k_norm_g")
ALL_NAMES = ("norm_g", "final_g", "rel_bias", "hgrn_lb", "ssd_w_in", "ssd_conv_w", "ssd_conv_b", "ssd_dt_bias",
             "ssd_a_log", "ssd_d", "ssd_norm_g", "ssd_w_out", "hg_w_in", "hg_norm_g", "hg_w_out", "at_w_in",
             "at_q_norm_g", "at_k_norm_g", "at_w_out", "dl_w_in", "dl_w_out")


def kernel(x, norm_g, final_g, rel_bias, hgrn_lb, ssd_w_in, ssd_conv_w, ssd_conv_b, ssd_dt_bias, ssd_a_log, ssd_d, ssd_norm_g, ssd_w_out, hg_w_in, hg_norm_g, hg_w_out, at_w_in, at_q_norm_g, at_k_norm_g, at_w_out, dl_w_in, dl_w_out, loss_target, m_norm_g, m_final_g, m_rel_bias, m_hgrn_lb, m_ssd_w_in, m_ssd_conv_w, m_ssd_conv_b, m_ssd_dt_bias, m_ssd_a_log, m_ssd_d, m_ssd_norm_g, m_ssd_w_out, m_hg_w_in, m_hg_norm_g, m_hg_w_out, m_at_w_in, m_at_q_norm_g, m_at_k_norm_g, m_at_w_out, m_dl_w_in, m_dl_w_out, v_norm_g, v_final_g, v_rel_bias, v_hgrn_lb, v_ssd_w_in, v_ssd_conv_w, v_ssd_conv_b, v_ssd_dt_bias, v_ssd_a_log, v_ssd_d, v_ssd_norm_g, v_ssd_w_out, v_hg_w_in, v_hg_norm_g, v_hg_w_out, v_at_w_in, v_at_q_norm_g, v_at_k_norm_g, v_at_w_out, v_dl_w_in, v_dl_w_out):
    args = locals()
    W = {n: args[n] for n in ALL_NAMES}
    M = {n: args["m_" + n] for n in ALL_NAMES}
    V = {n: args["v_" + n] for n in ALL_NAMES}
    xi, yi, ci = lax.axis_index("x"), lax.axis_index("y"), lax.axis_index("c")
    chip = 2 * xi + yi
    conv_shard = SSD_CONV_CH // N_CHIPS
    hgn_shard = HG_W // N_CHIPS

    p_in = [W[n][0].astype(BF16) for n in IN_NAMES]
    p_out = [W[n][0].astype(BF16) for n in OUT_NAMES]
    p_small = jnp.concatenate([
        jnp.pad(ssd_conv_w[0], ((0, 0), (0, D_MODEL - conv_shard))),
        jnp.pad(hg_norm_g, ((0, 0), (0, D_MODEL - hgn_shard)))], axis=0)
    c_idx = ci.astype(jnp.int32).reshape(1)
    me_idx = chip.astype(jnp.int32).reshape(1)

    def slot(stack, own, k):
        return jnp.where(chip == k, own, stack[k])

    def layer_weights(layer, got):
        s_in, s_out = got[0], got[1]
        return (jnp.concatenate([slot(s_in, p_in[layer], k) for k in range(N_CHIPS)], axis=1),
                jnp.concatenate([slot(s_out, p_out[layer], k) for k in range(N_CHIPS)], axis=0))

    def reduce_start(tag, layers, grads):
        gp_in = jnp.concatenate([grads[IN_NAMES[l]][0].reshape(D_MODEL, N_CHIPS, IN_COLS[l]).transpose(1, 0, 2)
                                 for l in layers], axis=2)
        gp_out = jnp.concatenate([grads[OUT_NAMES[l]][0].reshape(N_CHIPS, OUT_ROWS[l], D_MODEL) for l in layers], axis=1)
        r_in, r_out = swap_halves(f"swap_halves_{tag}", gp_in, gp_out)
        f_in, b_in = half_add(f"half_add_in_{tag}", gp_in, r_in, c_idx, 128)
        f_out, b_out = half_add(f"half_add_out_{tag}", gp_out, r_out, c_idx, 128)
        return (f_in, f_out), scatter_plan([b_in, b_out])

    def reduce_finish(tag, layers, halves, got, G):
        s_in = chip_sum(f"chip_sum_in_{tag}", halves[0], got[0], me_idx, 128)
        s_out = chip_sum(f"chip_sum_out_{tag}", halves[1], got[1], me_idx, 128)
        o_in, o_out = share_halves(f"share_halves_{tag}", s_in, s_out)
        red_in = jnp.where(ci == 0, jnp.concatenate([s_in, o_in], axis=0), jnp.concatenate([o_in, s_in], axis=0))
        red_out = jnp.where(ci == 0, jnp.concatenate([s_out, o_out], axis=0), jnp.concatenate([o_out, s_out], axis=0))
        off_c = off_r = 0
        for l in layers:
            G[IN_NAMES[l]] = red_in[:, off_c:off_c + IN_COLS[l]][None]
            G[OUT_NAMES[l]] = red_out[off_r:off_r + OUT_ROWS[l]][None]
            off_c += IN_COLS[l]
            off_r += OUT_ROWS[l]

    class Schedule:
        early = (2, 3)
        middle = (1,)

        def fwd_plans(self, layer, w):
            if layer == 0:
                def ssd_out(got):
                    stack = got["ssd_scan_b"][2]
                    return dict(ssd_w_out=jnp.concatenate([slot(stack, p_out[0], k) for k in range(N_CHIPS)], axis=0))

                return dict(ssd_scan_f=gather_plan([p_in[1], p_out[1]]),
                            ssd_scan_b=gather_plan([p_in[2], p_out[2], p_out[0]]), late_weights=ssd_out)
            if layer == 1:
                return dict(hg_scan_f=gather_plan([p_in[3]]), hg_scan_b=gather_plan([p_out[3]]))
            return None

        def fwd_done(self, layer, got, w):
            if layer == 0:
                w["hg_w_in"], w["hg_w_out"] = layer_weights(1, got["ssd_scan_f"])
                w["at_w_in"], w["at_w_out"] = layer_weights(2, got["ssd_scan_b"])
            if layer == 1:
                w["dl_w_in"], w["dl_w_out"] = layer_weights(3, got["hg_scan_f"] + got["hg_scan_b"])

        def bwd_plans(self, layer, grads):
            if layer == 1:
                self.halves_a, plan = reduce_start("a", self.early, grads)
                return dict(hg_scan_f_bwd=plan)
            if layer == 0:
                self.halves_c, plan = reduce_start("c", self.middle, grads)
                return dict(ssd_scan_f_bwd=plan)
            return None

        def bwd_done(self, layer, got):
            if layer == 1:
                self.got_a = got["hg_scan_f_bwd"]
            if layer == 0:
                self.got_c = got["ssd_scan_f_bwd"]

    g0_in, g_small = run_exchange("gather_w0", gather_plan([p_in[0]], whole=[p_small]))
    ssd_in_full = jnp.concatenate([slot(g0_in, p_in[0], k) for k in range(N_CHIPS)], axis=1)
    conv_full = jnp.concatenate([slot(g_small, p_small, k)[:SSD_CONV, :conv_shard] for k in range(N_CHIPS)], axis=1)
    hgn_full = jnp.concatenate([slot(g_small, p_small, k)[SSD_CONV:SSD_CONV + 1, :hgn_shard] for k in range(N_CHIPS)], axis=1)
    w = dict(
        norm_g=norm_g, final_g=final_g, rel_bias=rel_bias, hgrn_lb=hgrn_lb,
        ssd_w_main=ssd_in_full[:, :SSD_MAIN], ssd_w_dt=ssd_in_full[:, SSD_MAIN:],
        ssd_conv_w8=jnp.concatenate([conv_full, jnp.zeros((1, SSD_CONV_CH), F32)], axis=0),
        ssd_conv_b=ssd_conv_b, ssd_dt_bias=ssd_dt_bias, ssd_a_log=ssd_a_log, ssd_d=ssd_d, ssd_norm_g=ssd_norm_g,
        hg_norm_g=hgn_full, at_q_norm_g=at_q_norm_g, at_k_norm_g=at_k_norm_g)

    sched = Schedule()
    loss_tile, grad_x, grads = local_step(x[0], loss_target[0], w, sched)
    loss = lax.psum(loss_tile[0, 0], ("x", "y", "c"))

    G = {}
    late = (0,)
    halves_b, plan_b = reduce_start("b", late, grads)
    got_b = run_exchange("scatter_b", plan_b)
    reduce_finish("a", sched.early, sched.halves_a, sched.got_a, G)
    reduce_finish("c", sched.middle, sched.halves_c, sched.got_c, G)
    reduce_finish("b", late, halves_b, got_b, G)

    small_full = [grads[n].reshape(-1) for n in SMALL_NAMES]
    shapes_full = [grads[n].shape for n in SMALL_NAMES]
    packs = gather_small(_pack_small(small_full))
    (red_small,) = smallcall("sum_small", lambda p: (functools.reduce(lambda a, b: a + b, [p[k] for k in range(N_DEV)]),),
                             [packs], [packs.shape[1:]])
    for n, g in zip(SMALL_NAMES, _unpack_small(red_small, shapes_full)):
        G[n] = g
    G["ssd_conv_w"] = lax.dynamic_slice_in_dim(G["ssd_conv_w"].reshape(1, SSD_CONV, SSD_CONV_CH), chip * conv_shard, conv_shard, axis=2)
    G["hg_norm_g"] = lax.dynamic_slice_in_dim(G["hg_norm_g"].reshape(1, HG_W), chip * hgn_shard, hgn_shard, axis=1)
    for n in SMALL_NAMES:
        G[n] = G[n].reshape(W[n].shape)

    D, NM, NV = {}, {}, {}
    for n in IN_NAMES + OUT_NAMES:
        D[n], NM[n], NV[n] = adamw_big("adamw_" + n, W[n], G[n], M[n], V[n])
    shapes = [W[n].shape for n in SMALL_NAMES]
    pk = [_pack_small([T[n] for n in SMALL_NAMES]) for T in (W, G, M, V)]
    outs = smallcall("adamw_small", lambda w, g, m, v: _adamw(w, g, m, v), pk, [pk[0].shape] * 3)
    for T, pack in zip((D, NM, NV), outs):
        for n, a in zip(SMALL_NAMES, _unpack_small(pack, shapes)):
            T[n] = a
    return (loss, grad_x[None], *[G[n] for n in ALL_NAMES], *[D[n] for n in ALL_NAMES],
            *[NM[n] for n in ALL_NAMES], *[NV[n] for n in ALL_NAMES])
```

```python
import functools
import math

import numpy as np
import jax
import jax.numpy as jnp
from jax import lax
from jax.experimental import pallas as pl
from jax.experimental.pallas import tpu as pltpu

F32 = jnp.float32
BF16 = jnp.bfloat16
MESH = pl.DeviceIdType.MESH

D_MODEL = 1024
DEPTH = 4
GRID_W = 64
EPS = 1e-6
NEG_BIG = -1e30

SSD_DI = 2048
SSD_HEADDIM = 64
SSD_HEADS = 32
SSD_GROUPS = 4
SSD_HPG = 8
SSD_STATE = 128
SSD_CONV = 7
SSD_CHUNK = 128
SSD_GPS = 4
SSD_CONV_CH = SSD_DI + 2 * SSD_GROUPS * SSD_STATE
SSD_MAIN = SSD_DI + SSD_CONV_CH
SSD_IN = SSD_MAIN + 2 * SSD_HEADS

HG_HEADS = 8
HG_EXPAND = 128
HG_W = 1024
HG_CHUNK = 32
HG_ROWS = 256
HG_HPS = 8
HG_IN = 5 * HG_W

AT_HEADS = 16
AT_KV = 8
AT_GRP = 2
AT_HD = 128
ROPE_THETA = 10000.0
ROPE_AXIS = 64
AT_QW = AT_HEADS * AT_HD
AT_KW = AT_KV * AT_HD
AT_IN = 2 * AT_QW + 2 * AT_KW

DL_PAIRS = ((128, 1), (512, 4), (2048, 16))
DL_HEADS = 16
DL_HD = 64
DL_W = 1024
DL_STEPS = 64
DL_IN = 10 * DL_W
REL_BUCKETS = 32
REL_MAX_DIST = 1024

ADAM_LR = 0.001
ADAM_B1 = 0.9
ADAM_B2 = 0.999
ADAM_EPS = 1e-08
ADAM_WD = 0.01
ADAM_STEP = 10

VMEM_LIMIT = 56 * 1024 * 1024
LANE = 128
SUBLANE = 8


def _cp(sem=None):
    return pltpu.CompilerParams(dimension_semantics=sem, vmem_limit_bytes=VMEM_LIMIT)


_NN, _NT, _TN = ((1,), (0,)), ((1,), (1,)), ((0,), (0,))


def _dot(a, b, dims):
    return lax.dot_general(a.astype(BF16), b.astype(BF16), (dims, ((), ())), preferred_element_type=F32)


def _dot_rule(dims, da_rule, db_rule):
    @jax.custom_vjp
    def f(a, b):
        return _dot(a, b, dims)

    def fwd(a, b):
        return _dot(a, b, dims), (a, b)

    def bwd(res, g):
        a, b = res
        return da_rule(a, b, g).astype(a.dtype), db_rule(a, b, g).astype(b.dtype)

    f.defvjp(fwd, bwd)
    return f


_mm = _dot_rule(_NN, lambda a, b, g: _dot(g, b, _NT), lambda a, b, g: _dot(a, g, _TN))
_mm_nt = _dot_rule(_NT, lambda a, b, g: _dot(g, b, _NN), lambda a, b, g: _dot(g, a, _TN))
_mm_tn = _dot_rule(_TN, lambda a, b, g: _dot(b, g, _NT), lambda a, b, g: _dot(a, g, _NN))


def _mm_exact(a, b):
    return jnp.dot(a, b, preferred_element_type=F32, precision=lax.Precision.HIGHEST)


def _dot3(t, a, dims):
    hi = a.astype(BF16)
    r1 = a - hi.astype(F32)
    mid = r1.astype(BF16)
    lo = r1 - mid.astype(F32)
    return _dot(t, hi, dims) + (_dot(t, mid, dims) + _dot(t, lo, dims))


@jax.custom_vjp
def _mm_tri(t, a):
    return _dot3(t, a, _NN)


def _mm_tri_fwd(t, a):
    return _dot3(t, a, _NN), t


def _mm_tri_bwd(t, g):
    return None, _dot3(t, g, _TN)


_mm_tri.defvjp(_mm_tri_fwd, _mm_tri_bwd)


def _mm_nt_exact(a, b):
    return lax.dot_general(a, b, (((1,), (1,)), ((), ())), preferred_element_type=F32,
                           precision=lax.Precision.HIGHEST)


def _silu(x):
    return x * jax.nn.sigmoid(x)


def _softplus(z):
    return jnp.maximum(z, 0.0) + jnp.log(1.0 + jnp.exp(-jnp.abs(z)))


def _pick(dim, pref):
    best = None
    t = LANE
    while t <= min(dim, pref):
        if dim % t == 0:
            best = t
        t += LANE
    return best if best is not None else dim


def _const_map(n):
    return lambda *_: (0,) * n


MM_BLOCK_BYTES = 8 * 1024 * 1024


def _mm_tiles(mode, M, N, K, a_bytes, b_bytes, has_res=False):
    if mode == "nn":
        tk = K if K <= 2048 else _pick(K, 1024)
        tm = _pick(M, max(512, MM_BLOCK_BYTES // (tk * a_bytes)))
        tn = _pick(N, 512)
    elif mode == "tn":
        tk = K if K <= 4096 else _pick(K, 1024)
        tm = _pick(M, MM_BLOCK_BYTES // (tk * a_bytes))
        tn = _pick(N, MM_BLOCK_BYTES // (tk * b_bytes))
    else:
        tk = _pick(K, 1024)
        tn = _pick(N, 1024)
        tm = _pick(M, MM_BLOCK_BYTES // ((8 if has_res else 4) * tn))
    return tm, tn, tk


def matmul(name, a, b, mode="nn", res=None, out_dtype=F32):
    if mode == "tn":
        K, M = a.shape
    else:
        M, K = a.shape
    N = b.shape[0] if mode == "nt" else b.shape[1]
    tm, tn, tk = _mm_tiles(mode, M, N, K, a.dtype.itemsize, b.dtype.itemsize, res is not None)
    nk = K // tk
    a_spec = (pl.BlockSpec((tk, tm), lambda i, j, k: (k, i)) if mode == "tn"
              else pl.BlockSpec((tm, tk), lambda i, j, k: (i, k)))
    b_spec = (pl.BlockSpec((tn, tk), lambda i, j, k: (j, k)) if mode == "nt"
              else pl.BlockSpec((tk, tn), lambda i, j, k: (k, j)))
    dot = {"nn": _mm, "nt": _mm_nt, "tn": _mm_tn}[mode]
    has_res = res is not None

    def body(*refs):
        a_ref, b_ref = refs[0], refs[1]
        r_ref = refs[2] if has_res else None
        o_ref = refs[3] if has_res else refs[2]

        def finish(out):
            if has_res:
                out = out + r_ref[...].astype(F32)
            o_ref[...] = out.astype(o_ref.dtype)

        if nk == 1:
            finish(dot(a_ref[...], b_ref[...]))
            return
        acc = refs[-1]
        k = pl.program_id(2)

        @pl.when(k == 0)
        def _():
            acc[...] = jnp.zeros_like(acc)

        acc[...] += dot(a_ref[...], b_ref[...])

        @pl.when(k == nk - 1)
        def _():
            finish(acc[...])

    in_specs = [a_spec, b_spec]
    args = [a, b]
    if has_res:
        in_specs.append(pl.BlockSpec((tm, tn), lambda i, j, k: (i, j)))
        args.append(res)
    return pl.pallas_call(
        body, name=name, grid=(M // tm, N // tn, nk), in_specs=in_specs,
        out_specs=pl.BlockSpec((tm, tn), lambda i, j, k: (i, j)),
        out_shape=jax.ShapeDtypeStruct((M, N), out_dtype),
        scratch_shapes=[pltpu.VMEM((tm, tn), F32)] if nk > 1 else [],
        compiler_params=_cp(("parallel", "parallel", "arbitrary")),
    )(*args)


def call_with_comm(body, comm, *, name, grid, in_specs, out_specs, out_shape, scratch_shapes, semantics, args):
    if comm is None:
        outs = pl.pallas_call(body, name=name, grid=grid, in_specs=in_specs, out_specs=out_specs, out_shape=out_shape,
                              scratch_shapes=scratch_shapes, compiler_params=_cp(semantics))(*args)
        return list(outs), []
    n_in, n_out, n_scr = len(in_specs), len(out_specs), len(scratch_shapes)
    c_in, c_out = len(comm["ins"]), len(comm["out_shape"])
    total = int(np.prod(grid))
    mid_step = (2 * total) // 3

    def wrapped(*refs):
        p = 0
        ins = refs[p:p + n_in]
        p += n_in
        cins = refs[p:p + c_in]
        p += c_in
        outs = refs[p:p + n_out]
        p += n_out
        couts = refs[p:p + c_out]
        p += c_out
        scr = refs[p:p + n_scr]
        send, recv = refs[p + n_scr], refs[p + n_scr + 1]
        step = pl.program_id(0)
        for ax in range(1, len(grid)):
            step = step * grid[ax] + pl.program_id(ax)

        @pl.when(step == 0)
        def _():
            comm["start"](cins, couts, send, recv)

        body(*ins, *outs, *scr)
        if comm["mid"] is not None:
            @pl.when(step == mid_step)
            def _():
                comm["mid"](cins, couts, send, recv)

        @pl.when(step == total - 1)
        def _():
            comm["finish"](cins, couts, send, recv)

    outs = pl.pallas_call(
        wrapped, name=name, grid=grid, in_specs=list(in_specs) + [HBM] * c_in,
        out_specs=list(out_specs) + [HBM] * c_out, out_shape=list(out_shape) + list(comm["out_shape"]),
        scratch_shapes=list(scratch_shapes) + [pltpu.SemaphoreType.DMA((comm["n_sems"],))] * 2,
        compiler_params=_cp(("arbitrary",) * len(grid)),
    )(*args, *comm["ins"])
    return list(outs[:n_out]), list(outs[n_out:])


def _col(arr, width, idx):
    return (arr, width, idx)


def _perm(arr, dil, width=None, idx=0):
    return (arr, arr.shape[1] if width is None else width, idx, dil)


def _from_perm(ref, scr, dil):
    n, w = ref.shape[1], ref.shape[2]
    for r in range(dil):
        for j in range(w // LANE):
            scr[j, pl.ds(r, n, stride=dil), :] = ref[r, :, j * LANE:(j + 1) * LANE].astype(F32)
    return jnp.concatenate([scr[j] for j in range(w // LANE)], axis=1)


def _to_perm(val, ref, scr, dil):
    n, w = ref.shape[1], ref.shape[2]
    for j in range(w // LANE):
        scr[j] = val[:, j * LANE:(j + 1) * LANE].astype(F32)
    for r in range(dil):
        ref[r] = jnp.concatenate([scr[j, pl.ds(r, n, stride=dil), :] for j in range(w // LANE)], axis=1).astype(ref.dtype)


def rowcall(name, fn, rows, bcs, row_outs, bc_outs=(), tb=256, halo=()):
    rows = [r if isinstance(r, tuple) else (r, r.shape[1], 0) for r in rows]
    rows = [r if len(r) == 4 else r + (1,) for r in rows]
    row_outs = [o if len(o) == 3 else o + (1,) for o in row_outs]
    S = rows[0][0].shape[0]
    tb = min(tb, S)
    nb = S // tb
    n_r, n_h, n_b, n_ro, n_bo = len(rows), len(halo), len(bcs), len(row_outs), len(bc_outs)
    hb = tb // SUBLANE
    last = S // SUBLANE - 1
    perm_w = max([w for (_, w, _, d) in rows if d > 1] + [w for (w, _, d) in row_outs if d > 1] + [0])

    def body(*refs):
        i = pl.program_id(0)
        scr = refs[-1] if perm_w else None
        pos = 0
        r_in = [r[...] if d == 1 else _from_perm(r, scr, d) for r, (_, _, _, d) in zip(refs[pos:pos + n_r], rows)]
        pos += n_r
        h_in = []
        for _ in range(n_h):
            prev = refs[pos][...] * (i > 0).astype(F32)
            nxt = refs[pos + 1][...] * (i < nb - 1).astype(F32)
            h_in += [prev, nxt]
            pos += 2
        b_in = [r[...] for r in refs[pos:pos + n_b]]
        pos += n_b
        ro = refs[pos:pos + n_ro]
        bo = refs[pos + n_ro:pos + n_ro + n_bo]
        outs_r, outs_b = fn(*r_in, *h_in, *b_in)
        for ref, val, (_, _, d) in zip(ro, outs_r, row_outs, strict=True):
            if d == 1:
                ref[...] = val.astype(ref.dtype)
            else:
                _to_perm(val, ref, scr, d)
        if n_bo:
            @pl.when(i == 0)
            def _():
                for ref in bo:
                    ref[...] = jnp.zeros_like(ref)

            for ref, val in zip(bo, outs_b, strict=True):
                ref[...] += val

    in_specs, args = [], []
    for (a, w, c, d) in rows:
        if d == 1:
            in_specs.append(pl.BlockSpec((tb, w), functools.partial(lambda i, c: (i, c), c=c)))
            args.append(a)
        else:
            in_specs.append(pl.BlockSpec((d, tb // d, w), functools.partial(lambda i, c: (0, i, c), c=c)))
            args.append(a.reshape(d, S // d, a.shape[1]))
    for h in halo:
        a, w, c, _ = rows[h]
        in_specs.append(pl.BlockSpec((SUBLANE, w), functools.partial(
            lambda i, c: (jnp.maximum(i * hb - 1, 0), c), c=c)))
        in_specs.append(pl.BlockSpec((SUBLANE, w), functools.partial(
            lambda i, c: (jnp.minimum((i + 1) * hb, last), c), c=c)))
        args += [a, a]
    for b in bcs:
        in_specs.append(pl.BlockSpec(b.shape, _const_map(b.ndim)))
        args.append(b)
    out_specs, out_shape = [], []
    for (w, dt, d) in row_outs:
        if d == 1:
            out_specs.append(pl.BlockSpec((tb, w), lambda i: (i, 0)))
            out_shape.append(jax.ShapeDtypeStruct((S, w), dt))
        else:
            out_specs.append(pl.BlockSpec((d, tb // d, w), lambda i: (0, i, 0)))
            out_shape.append(jax.ShapeDtypeStruct((d, S // d, w), dt))
    for shp in bc_outs:
        out_specs.append(pl.BlockSpec(shp, _const_map(len(shp))))
        out_shape.append(jax.ShapeDtypeStruct(shp, F32))
    outs = pl.pallas_call(
        body, name=name, grid=(nb,), in_specs=in_specs, out_specs=out_specs, out_shape=out_shape,
        scratch_shapes=[pltpu.VMEM((perm_w // LANE, tb, LANE), F32)] if perm_w else [],
        compiler_params=_cp(("arbitrary",) if n_bo else ("parallel",)),
    )(*args)
    row_res = [o if d == 1 else o.reshape(S, w) for o, (w, _, d) in zip(outs[:n_ro], row_outs)]
    return row_res, list(outs[n_ro:])


def smallcall(name, fn, ins, out_shapes):
    n_in = len(ins)

    def body(*refs):
        outs = fn(*[r[...] for r in refs[:n_in]])
        for ref, val in zip(refs[n_in:], outs, strict=True):
            ref[...] = val.astype(ref.dtype)

    return pl.pallas_call(
        body, name=name, out_shape=[jax.ShapeDtypeStruct(s, F32) for s in out_shapes],
        compiler_params=_cp(),
    )(*ins)


def _rms(x, g):
    return x * lax.rsqrt(jnp.mean(x * x, axis=-1, keepdims=True) + EPS) * g


def _rms_groups(y, g, width):
    outs = []
    for j in range(y.shape[1] // width):
        sl = slice(j * width, (j + 1) * width)
        outs.append(_rms(y[:, sl], g[:, sl]))
    return jnp.concatenate(outs, axis=1)


def norm_fwd(name, x, g, dils=(1,)):
    outs, _ = rowcall(name, lambda x, g: ((_rms(x, g),) * len(dils), ()), [x], [g],
                      [(D_MODEL, BF16, d) for d in dils], tb=512)
    return outs[0] if len(dils) == 1 else tuple(outs)


def norm_bwd(name, x, g, dhn, dres, dils=(1,)):
    parts = dhn if isinstance(dhn, tuple) else (dhn,)
    n = len(parts)

    def fn(x, *rest):
        dh = functools.reduce(lambda a, b: a + b, rest[:n])
        _, vjp = jax.vjp(_rms, x, rest[n + 1])
        dx, dg = vjp(dh)
        return (dx + rest[n],), (dg,)

    rows = [x] + [a if d == 1 else _perm(a, d) for a, d in zip(parts, dils)] + [dres]
    (dx,), (dg,) = rowcall(name, fn, rows, [g], [(D_MODEL, F32)], [(1, D_MODEL)], tb=512)
    return dx, dg


def loss_head(x, tgt, g):
    def fn(x, tgt, g):
        y, vjp = jax.vjp(_rms, x, g)
        diff = y - tgt
        loss = 0.5 * jnp.sum(jnp.mean(diff * diff, axis=-1, keepdims=True), axis=0, keepdims=True)
        dx, dg = vjp(diff * (1.0 / D_MODEL))
        return (dx,), (jnp.broadcast_to(loss, (1, LANE)), dg)

    (dx,), (loss, dg) = rowcall("loss_head", fn, [x, tgt], [g], [(D_MODEL, F32)],
                                [(1, LANE), (1, D_MODEL)], tb=512)
    return loss, dx, dg


def _shift_rows(x, s):
    if s == 0:
        return x
    return pltpu.roll(x, (-s) % x.shape[0], 0)


def _conv_ext(x, prev, nxt, w):
    xe = jnp.concatenate([prev, x, nxt], axis=0)
    pad = SSD_CONV // 2
    c = jnp.zeros_like(xe)
    for k in range(SSD_CONV):
        c = c + w[k:k + 1, :] * _shift_rows(xe, k - pad)
    return xe, c


def ssd_conv_fwd(u, conv_w, conv_b):
    def fn(x0, x1, x2, p0, n0, p1, n1, p2, n2, w, b):
        tb = x0.shape[0]
        outs = []
        for j, (x, p, n) in enumerate(((x0, p0, n0), (x1, p1, n1), (x2, p2, n2))):
            sl = slice(j * 1024, (j + 1) * 1024)
            _, c = _conv_ext(x, p, n, w[:, sl])
            outs.append(_silu(c[SUBLANE:SUBLANE + tb] + b[:, sl]))
        return (jnp.concatenate(outs, axis=1),), ()

    (xbc,), _ = rowcall("ssd_conv_fwd", fn, [_col(u, 1024, 2), _col(u, 1024, 3), _col(u, 1024, 4)],
                        [conv_w, conv_b], [(SSD_CONV_CH, F32)], tb=256, halo=(0, 1, 2))
    return xbc


def ssd_conv_bwd(u, dxbc, dz, conv_w, conv_b):
    pad = SSD_CONV // 2

    def fn(x0, x1, x2, g0, g1, g2, dz, xp0, xn0, xp1, xn1, xp2, xn2, gp0, gn0, gp1, gn1, gp2, gn2, w, b):
        tb = x0.shape[0]
        blk = slice(SUBLANE, SUBLANE + tb)
        dpre, dws, dbs = [], [], []
        xs = ((x0, xp0, xn0), (x1, xp1, xn1), (x2, xp2, xn2))
        gs = ((g0, gp0, gn0), (g1, gp1, gn1), (g2, gp2, gn2))
        for j in range(3):
            sl = slice(j * 1024, (j + 1) * 1024)
            wj = w[:, sl]
            xe, c = _conv_ext(*xs[j], wj)
            ce = c + b[:, sl]
            sig = jax.nn.sigmoid(ce)
            ge = jnp.concatenate([gs[j][1], gs[j][0], gs[j][2]], axis=0)
            dce = ge * (sig * (1.0 + ce * (1.0 - sig)))
            dx = jnp.zeros_like(xe)
            dw_rows = []
            for k in range(SSD_CONV):
                dx = dx + wj[k:k + 1, :] * _shift_rows(dce, pad - k)
                dw_rows.append(jnp.sum(dce[blk] * _shift_rows(xe, k - pad)[blk], axis=0, keepdims=True))
            dw_rows.append(jnp.zeros_like(dw_rows[0]))
            dpre.append(dx[blk])
            dws.append(jnp.concatenate(dw_rows, axis=0))
            dbs.append(jnp.sum(dce[blk], axis=0, keepdims=True))
        du = jnp.concatenate([dz] + dpre, axis=1)
        return (du,), (jnp.concatenate(dws, axis=1), jnp.concatenate(dbs, axis=1))

    rows = [_col(u, 1024, 2), _col(u, 1024, 3), _col(u, 1024, 4),
            _col(dxbc, 1024, 0), _col(dxbc, 1024, 1), _col(dxbc, 1024, 2), dz]
    (du,), (dw, db) = rowcall("ssd_conv_bwd", fn, rows, [conv_w, conv_b], [(SSD_MAIN, BF16)],
                              [(SUBLANE, SSD_CONV_CH), (1, SSD_CONV_CH)], tb=128, halo=(0, 1, 2, 3, 4, 5))
    return du, dw, db


def _expand_heads(v):
    return jnp.concatenate([jnp.broadcast_to(v[:, j:j + 1], (v.shape[0], SSD_HEADDIM)) for j in range(SSD_HPG)], axis=1)


def _ssd_chunk(rev, st_in, xs, udt, dtb, alog, B, C):
    Q = B.shape[0]
    P = SSD_HEADDIM
    dt = _softplus(udt + dtb)
    a = dt * (-jnp.exp(alog))
    r = lax.broadcasted_iota(jnp.int32, (Q, Q), 0)
    c = lax.broadcasted_iota(jnp.int32, (Q, Q), 1)
    mask = (r <= c) if rev else (r >= c)
    p = _mm_tri(mask, a)
    pT = p.T
    p_e = _expand_heads(p)
    tot_e = p_e[0:1] if rev else p_e[Q - 1:Q]
    xdt = xs * _expand_heads(dt)
    CB = _mm_nt(C, B)
    H = SSD_HPG
    p_cols = jnp.concatenate([jnp.broadcast_to(p[:, j:j + 1], (Q, Q)) for j in range(H)], axis=1)
    p_rows = jnp.concatenate([pT[j:j + 1, :] for j in range(H)], axis=1)
    decay = jnp.exp(jnp.where(jnp.concatenate([mask] * H, axis=1), p_cols - p_rows, NEG_BIG))
    col = lax.broadcasted_iota(jnp.int32, (1, H * P), 1)
    x_bd = jnp.concatenate([jnp.where((col >= j * P) & (col < (j + 1) * P), xdt, 0.0) for j in range(H)], axis=0)
    y = _mm(jnp.concatenate([CB] * H, axis=1) * decay, x_bd) + _mm(C, st_in) * jnp.exp(p_e)
    st_out = st_in * jnp.exp(tot_e) + _mm_tn(B, xdt * jnp.exp(tot_e - p_e))
    return y, st_out


def _ssd_specs(nc, rev_order):
    Q = SSD_CHUNK
    N, P, H, GS = SSD_STATE, SSD_HEADDIM, SSD_HPG, SSD_GPS
    gw = H * P
    nbc = SSD_GROUPS // GS

    def cidx(s):
        return nc - 1 - s if rev_order else s

    xs = pl.BlockSpec((Q, GS * gw), lambda g, s: (cidx(s), g))
    Bs = pl.BlockSpec((Q, GS * N), lambda g, s: (cidx(s), SSD_DI // (GS * N) + g))
    Cs = pl.BlockSpec((Q, GS * N), lambda g, s: (cidx(s), SSD_DI // (GS * N) + nbc + g))
    BC_out = pl.BlockSpec((Q, GS * N), lambda g, s: (cidx(s), g))
    udt = pl.BlockSpec((GS, Q, H), lambda g, s: (g, cidx(s), 0))
    small = pl.BlockSpec((GS, 1, H), lambda g, s: (g, 0, 0))
    st = pl.BlockSpec((GS, 1, N, gw), lambda g, s: (g, cidx(s), 0, 0))
    return xs, Bs, Cs, BC_out, udt, small, st


def ssd_scan_fwd(name, xbc, udt, dtb, alog, rev, comm=None):
    S = xbc.shape[0]
    Q, N, P, H, GS = SSD_CHUNK, SSD_STATE, SSD_HEADDIM, SSD_HPG, SSD_GPS
    gw = H * P
    nc = S // Q
    xs_s, B_s, C_s, _, udt_s, small_s, st_s = _ssd_specs(nc, rev)

    def body(xs_ref, B_ref, C_ref, udt_ref, dtb_ref, alog_ref, y_ref, st_ref, state):
        @pl.when(pl.program_id(1) == 0)
        def _():
            state[...] = jnp.zeros_like(state)

        for g in range(GS):
            st_ref[g, 0] = state[g]
            y, st_out = _ssd_chunk(rev, state[g], xs_ref[:, g * gw:(g + 1) * gw], udt_ref[g], dtb_ref[g], alog_ref[g],
                                   B_ref[:, g * N:(g + 1) * N], C_ref[:, g * N:(g + 1) * N])
            y_ref[:, g * gw:(g + 1) * gw] = y
            state[g] = st_out

    (y, st), got = call_with_comm(
        body, comm, name=name, grid=(SSD_GROUPS // GS, nc),
        in_specs=[xs_s, B_s, C_s, udt_s, small_s, small_s],
        out_specs=[xs_s, st_s],
        out_shape=[jax.ShapeDtypeStruct((S, SSD_DI), F32),
                   jax.ShapeDtypeStruct((SSD_GROUPS, nc, N, gw), F32)],
        scratch_shapes=[pltpu.VMEM((GS, N, gw), F32)],
        semantics=("parallel", "arbitrary"), args=(xbc, xbc, xbc, udt, dtb, alog))
    return y, st, got


def ssd_scan_bwd(name, xbc, udt, dtb, alog, states, dy, rev, comm=None):
    S = xbc.shape[0]
    Q, N, P, H, GS = SSD_CHUNK, SSD_STATE, SSD_HEADDIM, SSD_HPG, SSD_GPS
    gw = H * P
    nc = S // Q
    xs_s, B_s, C_s, BC_out, udt_s, small_s, st_s = _ssd_specs(nc, not rev)

    def body(xs_ref, B_ref, C_ref, udt_ref, dtb_ref, alog_ref, st_ref, dy_ref,
             dx_ref, dB_ref, dC_ref, dudt_ref, ddtb_ref, dalog_ref, dstate):
        @pl.when(pl.program_id(1) == 0)
        def _():
            dstate[...] = jnp.zeros_like(dstate)
            ddtb_ref[...] = jnp.zeros_like(ddtb_ref)
            dalog_ref[...] = jnp.zeros_like(dalog_ref)

        for g in range(GS):
            cols, bc = slice(g * gw, (g + 1) * gw), slice(g * N, (g + 1) * N)
            _, vjp = jax.vjp(functools.partial(_ssd_chunk, rev), st_ref[g, 0], xs_ref[:, cols], udt_ref[g], dtb_ref[g],
                             alog_ref[g], B_ref[:, bc], C_ref[:, bc])
            dst_in, dxs, dudt, ddtb, dalog, dB, dC = vjp((dy_ref[:, cols], dstate[g]))
            dx_ref[:, cols] = dxs
            dB_ref[:, bc] = dB
            dC_ref[:, bc] = dC
            dudt_ref[g] = dudt
            ddtb_ref[g] += ddtb
            dalog_ref[g] += dalog
            dstate[g] = dst_in

    outs, got = call_with_comm(
        body, comm, name=name, grid=(SSD_GROUPS // GS, nc),
        in_specs=[xs_s, B_s, C_s, udt_s, small_s, small_s, st_s, xs_s],
        out_specs=[xs_s, BC_out, BC_out, udt_s, small_s, small_s],
        out_shape=[jax.ShapeDtypeStruct((S, SSD_DI), F32),
                   jax.ShapeDtypeStruct((S, SSD_GROUPS * N), F32),
                   jax.ShapeDtypeStruct((S, SSD_GROUPS * N), F32),
                   jax.ShapeDtypeStruct((SSD_GROUPS, S, H), F32),
                   jax.ShapeDtypeStruct((SSD_GROUPS, 1, H), F32),
                   jax.ShapeDtypeStruct((SSD_GROUPS, 1, H), F32)],
        scratch_shapes=[pltpu.VMEM((GS, N, gw), F32)],
        semantics=("parallel", "arbitrary"), args=(xbc, xbc, xbc, udt, dtb, alog, states, dy))
    return (*outs, got)


def _ssd_combine(yf, yb, xs, z, dexp, ng):
    y = (yf + yb + xs * dexp) * _silu(z)
    return _rms_groups(y, ng, SSD_DI // SSD_GROUPS)


def ssd_forward(x, hn, w, comm=None):
    comm = comm or {}
    S = x.shape[0]
    u = matmul("ssd_in", hn, w["ssd_w_main"])
    udt = matmul("ssd_in_dt", hn, w["ssd_w_dt"])
    xbc = ssd_conv_fwd(u, w["ssd_conv_w8"], w["ssd_conv_b"])
    udt_t = udt.reshape(S, 2, SSD_GROUPS, SSD_HPG).transpose(1, 2, 0, 3)
    dtb = w["ssd_dt_bias"].reshape(2, SSD_GROUPS, 1, SSD_HPG)
    alog = w["ssd_a_log"].reshape(2, SSD_GROUPS, 1, SSD_HPG)
    yf, stf, got_f = ssd_scan_fwd("ssd_scan_f", xbc, udt_t[0], dtb[0], alog[0], False, comm.get("ssd_scan_f"))
    yb, stb, got_b = ssd_scan_fwd("ssd_scan_b", xbc, udt_t[1], dtb[1], alog[1], True, comm.get("ssd_scan_b"))
    dexp = jnp.repeat(w["ssd_d"].reshape(1, SSD_HEADS), SSD_HEADDIM, axis=1)
    (yn,), _ = rowcall("ssd_combine", lambda yf, yb, xs, z, d, g: ((_ssd_combine(yf, yb, xs, z, d, g),), ()),
                       [yf, yb, _col(xbc, SSD_DI, 0), _col(u, SSD_DI, 0)], [dexp, w["ssd_norm_g"]],
                       [(SSD_DI, BF16)], tb=256)
    if "late_weights" in comm:
        w.update(comm["late_weights"](dict(ssd_scan_f=got_f, ssd_scan_b=got_b)))
    out = matmul("ssd_out", yn, w["ssd_w_out"], res=x)
    saved = dict(hn=hn, u=u, xbc=xbc, udt_t=udt_t, dtb=dtb, alog=alog, yf=yf, yb=yb, stf=stf, stb=stb,
                 dexp=dexp, yn=yn, got=dict(ssd_scan_f=got_f, ssd_scan_b=got_b))
    return out, saved


def ssd_backward(dy, sv, w, comm=None):
    S = dy.shape[0]
    u, xbc = sv["u"], sv["xbc"]
    dyn = matmul("ssd_out_dx", dy, w["ssd_w_out"], mode="nt")
    g_w_out = matmul("ssd_out_dw", sv["yn"], dy, mode="tn")

    def comb_bwd(yf, yb, xs, z, dyn, dexp, ng):
        _, vjp = jax.vjp(_ssd_combine, yf, yb, xs, z, dexp, ng)
        dyf, _, dxs, dz, ddexp, dng = vjp(dyn)
        return (dyf, dxs, dz), (ddexp, dng)

    (dyc, dskip, dz), (ddexp, g_norm) = rowcall(
        "ssd_combine_bwd", comb_bwd, [sv["yf"], sv["yb"], _col(xbc, SSD_DI, 0), _col(u, SSD_DI, 0), dyn],
        [sv["dexp"], w["ssd_norm_g"]], [(SSD_DI, F32)] * 3, [(1, SSD_DI), (1, SSD_DI)], tb=256)
    udt_t, dtb, alog = sv["udt_t"], sv["dtb"], sv["alog"]
    comm = comm or {}
    dxf, dBf, dCf, dudt_f, ddtb_f, dalog_f, got_f = ssd_scan_bwd("ssd_scan_f_bwd", xbc, udt_t[0], dtb[0], alog[0],
                                                                 sv["stf"], dyc, False, comm.get("ssd_scan_f_bwd"))
    dxb, dBb, dCb, dudt_b, ddtb_b, dalog_b, _ = ssd_scan_bwd("ssd_scan_b_bwd", xbc, udt_t[1], dtb[1], alog[1],
                                                             sv["stb"], dyc, True)

    def gather(dxf, dxb, dskip, dBf, dBb, dCf, dCb):
        return (jnp.concatenate([dxf + dxb + dskip, dBf + dBb, dCf + dCb], axis=1),), ()

    (dxbc,), _ = rowcall("ssd_dxbc", gather, [dxf, dxb, dskip, dBf, dBb, dCf, dCb], [], [(SSD_CONV_CH, F32)], tb=256)
    du, g_conv_w8, g_conv_b = ssd_conv_bwd(u, dxbc, dz, w["ssd_conv_w8"], w["ssd_conv_b"])
    dudt = jnp.stack([dudt_f, dudt_b]).transpose(2, 0, 1, 3).reshape(S, 2 * SSD_HEADS)
    hn = sv["hn"]
    g_main = matmul("ssd_in_dw", hn, du, mode="tn")
    g_dt = matmul("ssd_in_dt_dw", hn, dudt, mode="tn")
    dhn = matmul("ssd_in_dt_dx", dudt, w["ssd_w_dt"], mode="nt")
    dhn = matmul("ssd_in_dx", du, w["ssd_w_main"], mode="nt", res=dhn)
    grads = dict(
        ssd_w_in=jnp.concatenate([g_main, g_dt], axis=1)[None],
        ssd_conv_w=g_conv_w8[None, :SSD_CONV],
        ssd_conv_b=g_conv_b,
        ssd_dt_bias=jnp.stack([ddtb_f, ddtb_b]).reshape(1, 2, SSD_HEADS),
        ssd_a_log=jnp.stack([dalog_f, dalog_b]).reshape(1, 2, SSD_HEADS),
        ssd_d_exp=ddexp,
        ssd_norm_g=g_norm,
        ssd_w_out=g_w_out[None],
        got=dict(ssd_scan_f_bwd=got_f),
    )
    return dhn, grads


def _hg_block(rev, stTs, uq, uf, ui, lb):
    C = HG_CHUNK
    n = uq.shape[0] // C
    nh = uq.shape[1] // HG_EXPAND
    stTs = list(stTs)
    q = _silu(uq)
    f = lb + (1.0 - lb) * jax.nn.sigmoid(uf)
    k = 1.0 - f
    g = jnp.log(f)
    r = lax.broadcasted_iota(jnp.int32, (C, C), 0)
    c = lax.broadcasted_iota(jnp.int32, (C, C), 1)
    mask = (r <= c) if rev else (r >= c)
    Tm = mask.astype(F32)
    outs = [[None] * n for _ in range(nh)]
    for i in (reversed(range(n)) if rev else range(n)):
        sl = slice(i * C, (i + 1) * C)
        qi, ki, vi = q[sl], k[sl], ui[sl]
        G = _mm_tri(mask, g[sl])
        Gr = G[C // 2:C // 2 + 1]
        Gl = G[0:1] if rev else G[C - 1:C]
        q_in, k_in = qi * jnp.exp(G - Gr), ki * jnp.exp(Gr - G)
        q_st, k_st, e_l = qi * jnp.exp(G), ki * jnp.exp(Gl - G), jnp.exp(Gl)
        for h in range(nh):
            cs = slice(h * HG_EXPAND, (h + 1) * HG_EXPAND)
            att = jnp.where(mask, _mm_nt(q_in[:, cs], k_in[:, cs]), 0.0)
            outs[h][i] = _mm(att, vi[:, cs]) + _mm_nt(q_st[:, cs], stTs[h])
            stTs[h] = stTs[h] * e_l[:, cs] + _mm_tn(vi[:, cs], k_st[:, cs])
    o = jnp.concatenate([jnp.concatenate(outs[h], axis=0) for h in range(nh)], axis=1)
    return o, stTs


def _hg_specs(nb, rev_order, f_col):
    R = HG_ROWS
    gw = HG_HPS * HG_EXPAND
    ng = HG_HEADS // HG_HPS

    def bidx(s):
        return nb - 1 - s if rev_order else s

    def col(base):
        return pl.BlockSpec((R, gw), lambda h, s: (bidx(s), base * ng + h))

    out = pl.BlockSpec((R, gw), lambda h, s: (bidx(s), h))
    lb = pl.BlockSpec((1, gw), lambda h, s: (0, h))
    st = pl.BlockSpec((1, 1, HG_HPS, HG_EXPAND, HG_EXPAND), lambda h, s: (h, bidx(s), 0, 0, 0))
    return col(0), col(f_col), col(3), out, lb, st


def hg_scan_fwd(name, u, lb, rev, comm=None):
    S = u.shape[0]
    nb = S // HG_ROWS
    ng = HG_HEADS // HG_HPS
    q_s, f_s, i_s, o_s, lb_s, st_s = _hg_specs(nb, rev, 2 if rev else 1)

    def body(uq, uf, ui, lb_ref, o_ref, st_ref, state):
        @pl.when(pl.program_id(1) == 0)
        def _():
            state[...] = jnp.zeros_like(state)

        st_ref[0, 0] = state[...]
        o, st = _hg_block(rev, [state[h] for h in range(HG_HPS)], uq[...], uf[...], ui[...], lb_ref[...])
        o_ref[...] = o
        for h in range(HG_HPS):
            state[h] = st[h]

    (o, st), got = call_with_comm(
        body, comm, name=name, grid=(ng, nb), in_specs=[q_s, f_s, i_s, lb_s], out_specs=[o_s, st_s],
        out_shape=[jax.ShapeDtypeStruct((S, HG_W), F32),
                   jax.ShapeDtypeStruct((ng, nb, HG_HPS, HG_EXPAND, HG_EXPAND), F32)],
        scratch_shapes=[pltpu.VMEM((HG_HPS, HG_EXPAND, HG_EXPAND), F32)],
        semantics=("parallel", "arbitrary"), args=(u, u, u, lb))
    return o, st, got


def hg_scan_bwd(name, u, lb, states, do, rev, comm=None):
    S = u.shape[0]
    nb = S // HG_ROWS
    ng = HG_HEADS // HG_HPS
    q_s, f_s, i_s, o_s, lb_s, st_s = _hg_specs(nb, not rev, 2 if rev else 1)

    def body(uq, uf, ui, lb_ref, st_ref, do_ref, dq_ref, df_ref, di_ref, dlb_ref, dstate):
        @pl.when(pl.program_id(1) == 0)
        def _():
            dstate[...] = jnp.zeros_like(dstate)
            dlb_ref[...] = jnp.zeros_like(dlb_ref)

        _, vjp = jax.vjp(functools.partial(_hg_block, rev), [st_ref[0, 0, h] for h in range(HG_HPS)],
                         uq[...], uf[...], ui[...], lb_ref[...])
        dst, dq, df, di, dlb = vjp((do_ref[...], [dstate[h] for h in range(HG_HPS)]))
        dq_ref[...] = dq
        df_ref[...] = df
        di_ref[...] = di
        dlb_ref[...] += dlb
        for h in range(HG_HPS):
            dstate[h] = dst[h]

    outs, got = call_with_comm(
        body, comm, name=name, grid=(ng, nb), in_specs=[q_s, f_s, i_s, lb_s, st_s, o_s],
        out_specs=[o_s, o_s, o_s, lb_s],
        out_shape=[jax.ShapeDtypeStruct((S, HG_W), F32)] * 3 + [jax.ShapeDtypeStruct((1, HG_W), F32)],
        scratch_shapes=[pltpu.VMEM((HG_HPS, HG_EXPAND, HG_EXPAND), F32)],
        semantics=("parallel", "arbitrary"), args=(u, u, u, lb, states, do))
    return (*outs, got)


def _hg_lb(hgrn_lb, layer):
    m = jnp.max(hgrn_lb, axis=0, keepdims=True)
    e = jnp.exp(hgrn_lb - m)
    sm = e / jnp.sum(e, axis=0, keepdims=True)
    lb = jnp.zeros_like(sm[0:1])
    for i in range(1, layer + 1):
        lb = lb + sm[i:i + 1]
    return lb


def _hg_combine(of, ob, gate, ng):
    return _rms_groups(of + ob, ng, HG_EXPAND) * _silu(gate)


def hg_forward(x, hn, w, layer, comm=None):
    comm = comm or {}
    u = matmul("hg_in", hn, w["hg_w_in"])
    (lb,) = smallcall("hg_lb", lambda t: (_hg_lb(t, layer),), [w["hgrn_lb"]], [(1, HG_W)])
    of, stf, got_f = hg_scan_fwd("hg_scan_f", u, lb, False, comm.get("hg_scan_f"))
    ob, stb, got_b = hg_scan_fwd("hg_scan_b", u, lb, True, comm.get("hg_scan_b"))
    (og,), _ = rowcall("hg_combine", lambda of, ob, gate, ng: ((_hg_combine(of, ob, gate, ng),), ()),
                       [of, ob, _col(u, HG_W, 4)], [w["hg_norm_g"]], [(HG_W, BF16)], tb=256)
    out = matmul("hg_out", og, w["hg_w_out"], res=x)
    return out, dict(hn=hn, u=u, lb=lb, of=of, ob=ob, stf=stf, stb=stb, og=og, got=dict(hg_scan_f=got_f, hg_scan_b=got_b))


def hg_backward(dy, sv, w, layer, comm=None):
    comm = comm or {}
    u, lb = sv["u"], sv["lb"]
    dog = matmul("hg_out_dx", dy, w["hg_w_out"], mode="nt")
    g_w_out = matmul("hg_out_dw", sv["og"], dy, mode="tn")

    def comb_bwd(of, ob, gate, dog, ng):
        _, vjp = jax.vjp(_hg_combine, of, ob, gate, ng)
        dof, _, dgate, dng = vjp(dog)
        return (dof, dgate), (dng,)

    (do, dgate), (g_norm,) = rowcall("hg_combine_bwd", comb_bwd, [sv["of"], sv["ob"], _col(u, HG_W, 4), dog],
                                     [w["hg_norm_g"]], [(HG_W, F32)] * 2, [(1, HG_W)], tb=256)
    dqf, dff, dif, dlbf, got_f = hg_scan_bwd("hg_scan_f_bwd", u, lb, sv["stf"], do, False, comm.get("hg_scan_f_bwd"))
    dqb, dfb, dib, dlbb, _ = hg_scan_bwd("hg_scan_b_bwd", u, lb, sv["stb"], do, True)

    def gather(dqf, dqb, dff, dfb, dif, dib, dgate):
        return (jnp.concatenate([dqf + dqb, dff, dfb, dif + dib, dgate], axis=1),), ()

    (du,), _ = rowcall("hg_du", gather, [dqf, dqb, dff, dfb, dif, dib, dgate], [], [(HG_IN, BF16)], tb=256)

    def lb_bwd(t, dlbf, dlbb):
        _, vjp = jax.vjp(lambda t: _hg_lb(t, layer), t)
        return vjp(dlbf + dlbb)

    (g_lb,) = smallcall("hg_lb_bwd", lb_bwd, [w["hgrn_lb"], dlbf, dlbb], [(DEPTH, HG_W)])
    hn = sv["hn"]
    g_w_in = matmul("hg_in_dw", hn, du, mode="tn")
    dhn = matmul("hg_in_dx", du, w["hg_w_in"], mode="nt")
    return dhn, dict(hg_w_in=g_w_in[None], hg_norm_g=g_norm, hg_w_out=g_w_out[None], hgrn_lb=g_lb,
                     got=dict(hg_scan_f_bwd=got_f))


def _rope_tables(S):
    t = np.arange(S)
    row = (t // GRID_W).astype(np.float32)
    col = (t % GRID_W).astype(np.float32)
    inv = (ROPE_THETA ** (-np.arange(0, ROPE_AXIS, 2, dtype=np.float32) / ROPE_AXIS)).astype(np.float32)
    ar = jnp.asarray(row)[:, None] * jnp.asarray(inv)[None, :]
    ac = jnp.asarray(col)[:, None] * jnp.asarray(inv)[None, :]
    cos = jnp.concatenate([jnp.cos(ar), jnp.cos(ar), jnp.cos(ac), jnp.cos(ac)], axis=1)
    sin = jnp.concatenate([-jnp.sin(ar), jnp.sin(ar), -jnp.sin(ac), jnp.sin(ac)], axis=1)
    return cos.astype(F32), sin.astype(F32)


@jax.custom_vjp
def _swap_halves_of_axes(x):
    h = ROPE_AXIS // 2
    lane = lax.broadcasted_iota(jnp.int32, x.shape, 1)
    return jnp.where((lane & h) == 0, pltpu.roll(x, AT_HD - h, 1), pltpu.roll(x, h, 1))


_swap_halves_of_axes.defvjp(lambda x: (_swap_halves_of_axes(x), None), lambda _, g: (_swap_halves_of_axes(g),))


def _rope(x, cos, sin):
    return x * cos + _swap_halves_of_axes(x) * sin


def _at_pre(uq, uk, cos, sin, qg, kg):
    qs, ks = [], []
    for h in range(AT_HEADS):
        qs.append(_rope(_rms(uq[:, h * AT_HD:(h + 1) * AT_HD], qg), cos, sin) * (AT_HD ** -0.5))
    for h in range(AT_KV):
        ks.append(_rope(_rms(uk[:, h * AT_HD:(h + 1) * AT_HD], kg), cos, sin))
    return jnp.concatenate(qs, axis=1), jnp.concatenate(ks, axis=1)


def _stack_heads(x):
    return jnp.concatenate([x[:, :AT_HD], x[:, AT_HD:]], axis=0)


def _unstack_heads(x):
    t = x.shape[0] // 2
    return jnp.concatenate([x[:t], x[t:]], axis=1)


def at_flash_fwd(q, k, u):
    S = q.shape[0]
    tq, tk = _pick(S, 512), _pick(S, 4096)
    nq, nk = S // tq, S // tk
    gw = AT_GRP * AT_HD

    def body(q_ref, k_ref, v_ref, o_ref, lse_ref, m_s, l_s, acc):
        j = pl.program_id(2)

        @pl.when(j == 0)
        def _():
            m_s[...] = jnp.full_like(m_s, NEG_BIG)
            l_s[...] = jnp.zeros_like(l_s)
            acc[...] = jnp.zeros_like(acc)

        s = _mm_nt(_stack_heads(q_ref[...]), k_ref[...])
        m_new = jnp.maximum(m_s[...], jnp.max(s, axis=-1, keepdims=True))
        alpha = jnp.exp(m_s[...] - m_new)
        p = jnp.exp(s - m_new)
        l_s[...] = alpha * l_s[...] + jnp.sum(p, axis=-1, keepdims=True)
        acc[...] = alpha * acc[...] + _mm(p, v_ref[...])
        m_s[...] = m_new

        @pl.when(j == nk - 1)
        def _():
            o_ref[...] = _unstack_heads(acc[...] / l_s[...])
            lse = m_s[...] + jnp.log(l_s[...])
            lse_ref[0, 0] = lse[:tq]
            lse_ref[0, 1] = lse[tq:]

    return pl.pallas_call(
        body, name="at_flash_fwd", grid=(AT_KV, nq, nk),
        in_specs=[pl.BlockSpec((tq, gw), lambda h, i, j: (i, h)),
                  pl.BlockSpec((tk, AT_HD), lambda h, i, j: (j, h)),
                  pl.BlockSpec((tk, AT_HD), lambda h, i, j: (j, (AT_QW + AT_KW) // AT_HD + h))],
        out_specs=[pl.BlockSpec((tq, gw), lambda h, i, j: (i, h)),
                   pl.BlockSpec((1, AT_GRP, tq, 1), lambda h, i, j: (h, 0, i, 0))],
        out_shape=[jax.ShapeDtypeStruct((S, AT_QW), F32), jax.ShapeDtypeStruct((AT_KV, AT_GRP, S, 1), F32)],
        scratch_shapes=[pltpu.VMEM((2 * tq, 1), F32), pltpu.VMEM((2 * tq, 1), F32), pltpu.VMEM((2 * tq, AT_HD), F32)],
        compiler_params=_cp(("parallel", "parallel", "arbitrary")),
    )(q, k, u)


def at_flash_bwd(q, k, u, o, lse, do):
    S = q.shape[0]
    tq, tk = _pick(S, 128), _pick(S, 4096)
    nq, nk = S // tq, S // tk
    gw = AT_GRP * AT_HD

    def body(q_ref, k_ref, v_ref, o_ref, lse_ref, do_ref, dq_ref, dk_ref, dv_ref, dk_acc, dv_acc):
        j, i = pl.program_id(1), pl.program_id(2)

        @pl.when(i == 0)
        def _():
            dk_acc[...] = jnp.zeros_like(dk_acc)
            dv_acc[...] = jnp.zeros_like(dv_acc)

        q2 = _stack_heads(q_ref[...])
        do_blk = do_ref[...]
        do2 = _stack_heads(do_blk)
        delta = _stack_heads(do_blk * o_ref[...])
        delta = jnp.sum(delta, axis=-1, keepdims=True)
        kb, vb = k_ref[...], v_ref[...]
        p = jnp.exp(_mm_nt(q2, kb) - jnp.concatenate([lse_ref[0, 0], lse_ref[0, 1]], axis=0))
        dv_acc[...] += _mm_tn(p, do2)
        ds = p * (_mm_nt(do2, vb) - delta)
        dk_acc[...] += _mm_tn(ds, q2)
        dq = _unstack_heads(_mm(ds, kb))
        rows = pl.ds(pl.multiple_of(i * tq, tq), tq)

        @pl.when(j == 0)
        def _():
            dq_ref[rows, :] = dq

        @pl.when(j > 0)
        def _():
            dq_ref[rows, :] += dq

        @pl.when(i == nq - 1)
        def _():
            dk_ref[...] = dk_acc[...]
            dv_ref[...] = dv_acc[...]

    return pl.pallas_call(
        body, name="at_flash_bwd", grid=(AT_KV, nk, nq),
        in_specs=[pl.BlockSpec((tq, gw), lambda h, j, i: (i, h)),
                  pl.BlockSpec((tk, AT_HD), lambda h, j, i: (j, h)),
                  pl.BlockSpec((tk, AT_HD), lambda h, j, i: (j, (AT_QW + AT_KW) // AT_HD + h)),
                  pl.BlockSpec((tq, gw), lambda h, j, i: (i, h)),
                  pl.BlockSpec((1, AT_GRP, tq, 1), lambda h, j, i: (h, 0, i, 0)),
                  pl.BlockSpec((tq, gw), lambda h, j, i: (i, h))],
        out_specs=[pl.BlockSpec((S, gw), lambda h, j, i: (0, h)),
                   pl.BlockSpec((tk, AT_HD), lambda h, j, i: (j, h)),
                   pl.BlockSpec((tk, AT_HD), lambda h, j, i: (j, h))],
        out_shape=[jax.ShapeDtypeStruct((S, AT_QW), F32), jax.ShapeDtypeStruct((S, AT_KW), F32),
                   jax.ShapeDtypeStruct((S, AT_KW), F32)],
        scratch_shapes=[pltpu.VMEM((tk, AT_HD), F32), pltpu.VMEM((tk, AT_HD), F32)],
        compiler_params=_cp(("parallel", "arbitrary", "arbitrary")),
    )(q, k, u, o, lse, do)


def at_forward(x, hn, w, comm=None):
    S = x.shape[0]
    u = matmul("at_in", hn, w["at_w_in"])
    cos, sin = _rope_tables(S)
    (q, k), _ = rowcall("at_pre", lambda uq, uk, c, s, qg, kg: (_at_pre(uq, uk, c, s, qg, kg), ()),
                        [_col(u, AT_QW, 0), _col(u, AT_KW, 2), cos, sin], [w["at_q_norm_g"], w["at_k_norm_g"]],
                        [(AT_QW, BF16), (AT_KW, BF16)], tb=256)
    o, lse = at_flash_fwd(q, k, u)
    (og,), _ = rowcall("at_gate", lambda o, gate: ((o * _silu(gate),), ()), [o, _col(u, AT_QW, 2)], [],
                       [(AT_QW, BF16)], tb=256)
    out = matmul("at_out", og, w["at_w_out"], res=x)
    return out, dict(hn=hn, u=u, cos=cos, sin=sin, q=q, k=k, o=o, lse=lse, og=og)


def at_backward(dy, sv, w, comm=None):
    u = sv["u"]
    dog = matmul("at_out_dx", dy, w["at_w_out"], mode="nt")
    g_w_out = matmul("at_out_dw", sv["og"], dy, mode="tn")

    def gate_bwd(o, gate, dog):
        _, vjp = jax.vjp(lambda o, gate: o * _silu(gate), o, gate)
        return vjp(dog), ()

    (do, dgate), _ = rowcall("at_gate_bwd", gate_bwd, [sv["o"], _col(u, AT_QW, 2), dog], [],
                             [(AT_QW, F32)] * 2, tb=256)
    dq, dk, dv = at_flash_bwd(sv["q"], sv["k"], u, sv["o"], sv["lse"], do)

    def pre_bwd(uq, uk, cos, sin, dq, dk, dv, dgate, qg, kg):
        _, vjp = jax.vjp(lambda uq, uk, qg, kg: _at_pre(uq, uk, cos, sin, qg, kg), uq, uk, qg, kg)
        duq, duk, dqg, dkg = vjp((dq, dk))
        return (jnp.concatenate([duq, duk, dv, dgate], axis=1),), (dqg, dkg)

    (du,), (g_qg, g_kg) = rowcall(
        "at_pre_bwd", pre_bwd, [_col(u, AT_QW, 0), _col(u, AT_KW, 2), sv["cos"], sv["sin"], dq, dk, dv, dgate],
        [w["at_q_norm_g"], w["at_k_norm_g"]], [(AT_IN, BF16)], [(1, AT_HD), (1, AT_HD)], tb=128)
    hn = sv["hn"]
    g_w_in = matmul("at_in_dw", hn, du, mode="tn")
    dhn = matmul("at_in_dx", du, w["at_w_in"], mode="nt")
    return dhn, dict(at_w_in=g_w_in[None], at_q_norm_g=g_qg, at_k_norm_g=g_kg, at_w_out=g_w_out[None])


def _t5_bucket_np(rel):
    half = REL_BUCKETS // 2
    exact = half // 2
    n = np.abs(rel)
    large = exact + (np.log(np.maximum(n, 1).astype(np.float32) / exact)
                     / math.log(REL_MAX_DIST / exact) * (half - exact)).astype(np.int32)
    large = np.minimum(large, half - 1)
    return np.where(rel > 0, half, 0) + np.where(n < exact, n, large)


def _dl_tq(S, dil):
    return min(128, S // dil)


def _dl_bias_maps(tq, dil):
    W = tq + 2 * DL_STEPS
    i = np.arange(tq)[:, None]
    wdx = np.arange(W)[None, :]
    dm = wdx - DL_STEPS - i
    bucket = _t5_bucket_np(dm * dil).reshape(-1).astype(np.int32)
    band = np.where(np.abs(dm) <= DL_STEPS, 0.0, NEG_BIG).reshape(1, -1).astype(np.float32)
    onehot = (jnp.asarray(bucket)[None, :] == jnp.arange(REL_BUCKETS, dtype=jnp.int32)[:, None]).astype(F32)
    return onehot, jnp.asarray(band)


def _dl_attend(q, kwin, vwin, T, valid):
    tq = q.shape[0]
    os, ls = [], []
    for h in range(DL_HEADS):
        sl = slice(h * DL_HD, (h + 1) * DL_HD)
        s = _mm_nt(q[:, sl] * (DL_HD ** -0.5), kwin[:, sl]) + T[h]
        s = jnp.where(valid, s, NEG_BIG)
        m = lax.stop_gradient(jnp.max(s, axis=-1, keepdims=True))
        e = jnp.exp(s - m)
        den = jnp.sum(e, axis=-1, keepdims=True)
        lse = m + jnp.log(den)
        p = e * (1.0 / den)
        os.append(_mm(p, vwin[:, sl]))
        ls.append(jnp.broadcast_to(lse, (tq, DL_HD)))
    return jnp.concatenate(os, axis=1), jnp.concatenate(ls, axis=1)


def _dl_specs(tq, Ls):
    nb = Ls // tq
    hs = DL_STEPS
    per = tq // hs
    nh = Ls // hs

    def main(c):
        return pl.BlockSpec((tq, DL_W), lambda r, i: (r * nb + i, c))

    def prev(c):
        return pl.BlockSpec((hs, DL_W), lambda r, i: (r * nh + jnp.maximum(i * per - 1, 0), c))

    def nxt(c):
        return pl.BlockSpec((hs, DL_W), lambda r, i: (r * nh + jnp.minimum((i + 1) * per, nh - 1), c))

    return nb, main, prev, nxt


def _dl_valid(i, tq, Ls):
    W = tq + 2 * DL_STEPS
    mk = i * tq - DL_STEPS + lax.broadcasted_iota(jnp.int32, (1, W), 1)
    return (mk >= 0) & (mk < Ls)


def dl_attn_fwd(gi, dil, u, T):
    S = u.shape[0]
    Ls = S // dil
    tq = _dl_tq(S, dil)
    nb, main, prev, nxt = _dl_specs(tq, Ls)
    out = main(0)

    def body(q_ref, kp, kc, kn, vp, vc, vn, T_ref, o_ref, l_ref):
        kwin = jnp.concatenate([kp[...], kc[...], kn[...]], axis=0)
        vwin = jnp.concatenate([vp[...], vc[...], vn[...]], axis=0)
        o, l = _dl_attend(q_ref[...], kwin, vwin, T_ref[...], _dl_valid(pl.program_id(1), tq, Ls))
        o_ref[...] = o
        l_ref[...] = l

    o, l = pl.pallas_call(
        body, name=f"dl_attn_fwd{gi}", grid=(dil, nb),
        in_specs=[main(0), prev(1), main(1), nxt(1), prev(2), main(2), nxt(2),
                  pl.BlockSpec(T.shape, _const_map(3))],
        out_specs=[out, out],
        out_shape=[jax.ShapeDtypeStruct((S, DL_W), F32)] * 2,
        compiler_params=_cp(("parallel", "parallel")),
    )(u, u, u, u, u, u, u, T)
    return o, l


def dl_attn_bwd(gi, dil, u, T, do, dl, dgate=None):
    S = u.shape[0]
    Ls = S // dil
    tq = _dl_tq(S, dil)
    hs = DL_STEPS
    W = tq + 2 * hs
    nb, main, prev, nxt = _dl_specs(tq, Ls)
    out = main(0)
    win = pl.BlockSpec((1, W, DL_W), lambda r, i: (r * nb + i, 0, 0))

    def body(q_ref, kp, kc, kn, vp, vc, vn, T_ref, do_ref, dl_ref, dq_ref, dkw_ref, dvw_ref, dT_ref):
        first = (pl.program_id(0) == 0) & (pl.program_id(1) == 0)

        @pl.when(first)
        def _():
            dT_ref[...] = jnp.zeros_like(dT_ref)

        kwin = jnp.concatenate([kp[...], kc[...], kn[...]], axis=0)
        vwin = jnp.concatenate([vp[...], vc[...], vn[...]], axis=0)
        valid = _dl_valid(pl.program_id(1), tq, Ls)
        _, vjp = jax.vjp(lambda q, k, v, T: _dl_attend(q, k, v, T, valid), q_ref[...], kwin, vwin, T_ref[...])
        dq, dkw, dvw, dT = vjp((do_ref[...], dl_ref[...]))
        dq_ref[...] = dq
        dkw_ref[0] = dkw
        dvw_ref[0] = dvw
        dT_ref[...] += dT

    dq, dkw, dvw, dT = pl.pallas_call(
        body, name=f"dl_attn_bwd{gi}", grid=(dil, nb),
        in_specs=[main(0), prev(1), main(1), nxt(1), prev(2), main(2), nxt(2),
                  pl.BlockSpec(T.shape, _const_map(3)), out, out],
        out_specs=[out, win, win, pl.BlockSpec(T.shape, _const_map(3))],
        out_shape=[jax.ShapeDtypeStruct((S, DL_W), F32),
                   jax.ShapeDtypeStruct((dil * nb, W, DL_W), F32),
                   jax.ShapeDtypeStruct((dil * nb, W, DL_W), F32),
                   jax.ShapeDtypeStruct(T.shape, F32)],
        compiler_params=_cp(("arbitrary", "arbitrary")),
    )(u, u, u, u, u, u, u, T, do, dl)

    per = tq // hs
    n_out = 3 if dgate is None else 4

    def fold(*refs):
        dq_ref, kc, kp, kn, vc, vp, vn = refs[:7]
        du_ref = refs[-1]
        i = pl.program_id(1)
        has_p = (i > 0).astype(F32)
        has_n = (i < nb - 1).astype(F32)
        du_ref[:, 0:DL_W] = dq_ref[...].astype(BF16)
        for c, (c_ref, p_ref, n_ref) in enumerate(((kc, kp, kn), (vc, vp, vn)), start=1):
            mid = c_ref[0, hs:hs + tq, :]
            top = mid[0:hs] + p_ref[0] * has_p
            bot = mid[tq - hs:tq] + n_ref[0] * has_n
            parts = [top, bot] if tq == 2 * hs else ([top, mid[hs:tq - hs], bot] if tq > 2 * hs else [top + n_ref[0] * has_n])
            du_ref[:, c * DL_W:(c + 1) * DL_W] = jnp.concatenate(parts, axis=0).astype(BF16)
        if dgate is not None:
            du_ref[:, 3 * DL_W:4 * DL_W] = refs[7][...].astype(BF16)

    wfull = pl.BlockSpec((1, W, DL_W), lambda r, i: (r * nb + i, 0, 0))
    wprev = pl.BlockSpec((1, hs, DL_W), lambda r, i: (r * nb + jnp.maximum(i - 1, 0), per + 1, 0))
    wnext = pl.BlockSpec((1, hs, DL_W), lambda r, i: (r * nb + jnp.minimum(i + 1, nb - 1), 0, 0))
    extra_specs, extra_args = ([], []) if dgate is None else ([out], [dgate])
    du = pl.pallas_call(
        fold, name=f"dl_fold{gi}", grid=(dil, nb),
        in_specs=[out, wfull, wprev, wnext, wfull, wprev, wnext] + extra_specs,
        out_specs=pl.BlockSpec((tq, n_out * DL_W), lambda r, i: (r * nb + i, 0)),
        out_shape=jax.ShapeDtypeStruct((S, n_out * DL_W), BF16),
        compiler_params=_cp(("parallel", "parallel")),
    )(dq, dkw, dkw, dkw, dvw, dvw, dvw, *extra_args)
    return du, dT


def _dl_merge(o0, o1, o2, l0, l1, l2, gate):
    m = jnp.maximum(jnp.maximum(l0, l1), l2)
    e0, e1, e2 = jnp.exp(l0 - m), jnp.exp(l1 - m), jnp.exp(l2 - m)
    den = e0 + e1 + e2
    return ((e0 * o0 + e1 * o1 + e2 * o2) / den) * _silu(gate)


DL_DILS = tuple(d for _, d in DL_PAIRS)


def _dl_group_weights(w_in):
    g3 = 3 * DL_W
    return [jnp.concatenate([w_in[:, :g3], w_in[:, 3 * g3:]], axis=1), w_in[:, g3:2 * g3], w_in[:, 2 * g3:3 * g3]]


def dl_forward(x, hns, w, comm=None):
    S = x.shape[0]
    wg = _dl_group_weights(w["dl_w_in"])
    rbT = w["rel_bias"].T
    us, os, ls, Ts, maps = [], [], [], [], []
    for gi, dil in enumerate(DL_DILS):
        u = matmul(f"dl_in{gi}", hns[gi], wg[gi])
        tq = _dl_tq(S, dil)
        W = tq + 2 * DL_STEPS
        onehot, band = _dl_bias_maps(tq, dil)
        (T,) = smallcall(f"dl_bias{gi}", lambda rbT, oh, band: (_mm_exact(rbT, oh) + band,), [rbT, onehot, band],
                         [(DL_HEADS, tq * W)])
        T = T.reshape(DL_HEADS, tq, W)
        o, l = dl_attn_fwd(gi, dil, u, T)
        us.append(u)
        os.append(o)
        ls.append(l)
        Ts.append(T)
        maps.append(onehot)
    rows = [a if d == 1 else _perm(a, d) for a, d in zip(os + ls, DL_DILS * 2)] + [_col(us[0], DL_W, 3)]
    (og,), _ = rowcall("dl_merge", lambda *a: ((_dl_merge(*a),), ()), rows, [], [(DL_W, BF16)], tb=256)
    out = matmul("dl_out", og, w["dl_w_out"], res=x)
    return out, dict(hns=hns, us=us, os=os, ls=ls, Ts=Ts, maps=maps, og=og, wg=wg)


def dl_backward(dy, sv, w, comm=None):
    us = sv["us"]
    dog = matmul("dl_out_dx", dy, w["dl_w_out"], mode="nt")
    g_w_out = matmul("dl_out_dw", sv["og"], dy, mode="tn")

    def merge_bwd(o0, o1, o2, l0, l1, l2, gate, dog):
        _, vjp = jax.vjp(_dl_merge, o0, o1, o2, l0, l1, l2, gate)
        return vjp(dog), ()

    rows = [a if d == 1 else _perm(a, d) for a, d in zip(sv["os"] + sv["ls"], DL_DILS * 2)] + [_col(us[0], DL_W, 3), dog]
    grads7, _ = rowcall("dl_merge_bwd", merge_bwd, rows, [], [(DL_W, F32, d) for d in DL_DILS * 2] + [(DL_W, F32)], tb=256)
    dos, dls, dgate = grads7[0:3], grads7[3:6], grads7[6]
    g_rbT, g_ws, dhns = None, [], []
    for gi, dil in enumerate(DL_DILS):
        du, dT = dl_attn_bwd(gi, dil, us[gi], sv["Ts"][gi], dos[gi], dls[gi], dgate if gi == 0 else None)
        (g,) = smallcall(f"dl_bias_bwd{gi}", lambda dT, oh: (_mm_nt_exact(dT, oh),),
                         [dT.reshape(DL_HEADS, -1), sv["maps"][gi]], [(DL_HEADS, REL_BUCKETS)])
        g_rbT = g if g_rbT is None else g_rbT + g
        g_ws.append(matmul(f"dl_in_dw{gi}", sv["hns"][gi], du, mode="tn"))
        dhns.append(matmul(f"dl_in_dx{gi}", du, sv["wg"][gi], mode="nt"))
    g3 = 3 * DL_W
    g_w_in = jnp.concatenate([g_ws[0][:, :g3], g_ws[1], g_ws[2], g_ws[0][:, g3:]], axis=1)
    return tuple(dhns), dict(dl_w_in=g_w_in[None], dl_w_out=g_w_out[None], rel_bias=g_rbT.T)


_FWD = (ssd_forward, hg_forward, at_forward, dl_forward)
_BWD = (ssd_backward, hg_backward, at_backward, dl_backward)


def _norm_dils(layer):
    return DL_DILS if layer % 4 == 3 else (1,)


class NoExchange:
    def fwd_plans(self, layer, w):
        return None

    def fwd_done(self, layer, got, w):
        pass

    def bwd_plans(self, layer, grads):
        return None

    def bwd_done(self, layer, got):
        pass


def local_step(x, tgt, w, sched=None):
    sched = sched or NoExchange()
    saved = []
    h = x
    for layer in range(DEPTH):
        hn = norm_fwd(f"norm{layer}", h, w["norm_g"][layer:layer + 1], _norm_dils(layer))
        extra = (layer,) if layer % 4 == 1 else ()
        h_next, sv = _FWD[layer % 4](h, hn, w, *extra, comm=sched.fwd_plans(layer, w))
        sched.fwd_done(layer, sv.get("got", {}), w)
        saved.append((h, sv))
        h = h_next
    loss, dh, g_final = loss_head(h, tgt, w["final_g"].reshape(1, D_MODEL))
    grads = {}
    g_norm = [None] * DEPTH
    for layer in reversed(range(DEPTH)):
        h_in, sv = saved[layer]
        extra = (layer,) if layer % 4 == 1 else ()
        dhn, g = _BWD[layer % 4](dh, sv, w, *extra, comm=sched.bwd_plans(layer, grads))
        sched.bwd_done(layer, g.pop("got", {}))
        grads.update(g)
        dh, g_norm[layer] = norm_bwd(f"norm{layer}_bwd", h_in, w["norm_g"][layer:layer + 1], dhn, dh, _norm_dils(layer))
    grads["norm_g"] = jnp.concatenate(g_norm, axis=0)
    grads["final_g"] = g_final.reshape(D_MODEL)
    grads["ssd_d"] = jnp.sum(grads.pop("ssd_d_exp").reshape(SSD_HEADS, SSD_HEADDIM), axis=1)[None]
    return loss, dh, grads


IN_NAMES = ("ssd_w_in", "hg_w_in", "at_w_in", "dl_w_in")
OUT_NAMES = ("ssd_w_out", "hg_w_out", "at_w_out", "dl_w_out")
IN_COLS = (SSD_IN // 4, HG_IN // 4, AT_IN // 4, DL_IN // 4)
OUT_ROWS = (SSD_DI // 4, HG_W // 4, AT_QW // 4, DL_W // 4)
PACK_IN = sum(IN_COLS)
PACK_OUT = sum(OUT_ROWS)
N_CHIPS = 4
N_DEV = 8
HBM = pl.BlockSpec(memory_space=pl.ANY)


def _mesh_pos():
    return lax.axis_index("x"), lax.axis_index("y"), lax.axis_index("c")


def _other_chips(x, y):
    return [(1 - x, y), (x, 1 - y), (1 - x, 1 - y)]


def _half_rows(half, n):
    return pl.ds(pl.multiple_of(half * n, n), n)


def _remote(src, dst, send, recv, k, to):
    return pltpu.make_async_remote_copy(src_ref=src, dst_ref=dst, send_sem=send.at[k], recv_sem=recv.at[k],
                                        device_id=to, device_id_type=MESH)


def gather_plan(packs, whole=()):
    arrs = list(packs) + list(whole)
    n_half = len(packs)

    def pieces(ins, outs):
        x, y, c = _mesh_pos()
        for a, (src, dst) in enumerate(zip(ins, outs)):
            h = src.shape[0] // 2 if a < n_half else None
            for j, (px, py) in enumerate(_other_chips(x, y)):
                yield a, j, src, dst, h, (x, y, c), (px, py)

    def start(ins, outs, send, recv):
        for a, j, src, dst, h, (x, y, c), (px, py) in pieces(ins, outs):
            me = 2 * x + y
            if h is None:
                _remote(src, dst.at[me], send, recv, 6 * a + j, (px, py, c)).start()
            else:
                _remote(src.at[_half_rows(c, h)], dst.at[me, _half_rows(c, h)], send, recv, 6 * a + j, (px, py, c)).start()

    def mid(ins, outs, send, recv):
        for a, j, src, dst, h, (x, y, c), (px, py) in pieces(ins, outs):
            kp = 2 * px + py
            if h is None:
                _remote(src, dst.at[kp], send, recv, 6 * a + j, (px, py, c)).wait_recv()
            else:
                got = dst.at[kp, _half_rows(c, h)]
                _remote(src.at[_half_rows(c, h)], got, send, recv, 6 * a + j, (px, py, c)).wait_recv()
                _remote(got, got, send, recv, 6 * a + 3 + j, (x, y, 1 - c)).start()

    def finish(ins, outs, send, recv):
        for a, j, src, dst, h, (x, y, c), (px, py) in pieces(ins, outs):
            me, kp = 2 * x + y, 2 * px + py
            if h is None:
                _remote(src, dst.at[me], send, recv, 6 * a + j, (px, py, c)).wait_send()
            else:
                theirs = dst.at[kp, _half_rows(1 - c, h)]
                _remote(theirs, theirs, send, recv, 6 * a + 3 + j, (x, y, 1 - c)).wait_recv()
                _remote(src.at[_half_rows(c, h)], dst.at[me, _half_rows(c, h)], send, recv, 6 * a + j, (px, py, c)).wait_send()
                mine = dst.at[kp, _half_rows(c, h)]
                _remote(mine, mine, send, recv, 6 * a + 3 + j, (x, y, 1 - c)).wait_send()

    return dict(ins=arrs, out_shape=[jax.ShapeDtypeStruct((N_CHIPS,) + a.shape, a.dtype) for a in arrs],
                n_sems=6 * len(arrs), start=start, mid=mid, finish=finish)


def scatter_plan(halves):
    def copies(ins, outs, send, recv):
        x, y, c = _mesh_pos()
        for a, (src, dst) in enumerate(zip(ins, outs)):
            for j, (px, py) in enumerate(_other_chips(x, y)):
                yield _remote(src.at[2 * px + py], dst.at[j], send, recv, 3 * a + j, (px, py, c))

    def start(ins, outs, send, recv):
        for cp in copies(ins, outs, send, recv):
            cp.start()

    def finish(ins, outs, send, recv):
        for cp in copies(ins, outs, send, recv):
            cp.wait()

    return dict(ins=list(halves), out_shape=[jax.ShapeDtypeStruct((3,) + a.shape[1:], a.dtype) for a in halves],
                n_sems=3 * len(halves), start=start, mid=None, finish=finish)


def run_exchange(name, plan):
    n_in = len(plan["ins"])

    def body(*refs):
        ins, outs = refs[:n_in], refs[n_in:-2]
        send, recv = refs[-2], refs[-1]
        plan["start"](ins, outs, send, recv)
        if plan["mid"] is not None:
            plan["mid"](ins, outs, send, recv)
        plan["finish"](ins, outs, send, recv)

    return pl.pallas_call(
        body, name=name, in_specs=[HBM] * n_in, out_specs=[HBM] * len(plan["out_shape"]), out_shape=plan["out_shape"],
        scratch_shapes=[pltpu.SemaphoreType.DMA((plan["n_sems"],))] * 2,
        compiler_params=pltpu.CompilerParams(has_side_effects=True),
    )(*plan["ins"])


def swap_halves(name, g_in, g_out):
    h_in, h_out = g_in.shape[1] // 2, g_out.shape[1] // 2

    def body(gi, go, ri, ro, send, recv):
        x, y, c = _mesh_pos()
        sib = (x, y, 1 - c)

        def rows(half, n):
            return pl.ds(pl.multiple_of(half * n, n), n)

        cps = [pltpu.make_async_remote_copy(src_ref=gi.at[:, rows(1 - c, h_in)], dst_ref=ri, send_sem=send.at[0],
                                            recv_sem=recv.at[0], device_id=sib, device_id_type=MESH),
               pltpu.make_async_remote_copy(src_ref=go.at[:, rows(1 - c, h_out)], dst_ref=ro, send_sem=send.at[1],
                                            recv_sem=recv.at[1], device_id=sib, device_id_type=MESH)]
        for cp in cps:
            cp.start()
        for cp in cps:
            cp.wait()

    return pl.pallas_call(
        body, name=name, in_specs=[HBM, HBM], out_specs=[HBM, HBM],
        out_shape=[jax.ShapeDtypeStruct((N_CHIPS, h_in, g_in.shape[2]), g_in.dtype),
                   jax.ShapeDtypeStruct((N_CHIPS, h_out, g_out.shape[2]), g_out.dtype)],
        scratch_shapes=[pltpu.SemaphoreType.DMA((2,)), pltpu.SemaphoreType.DMA((2,))],
        compiler_params=pltpu.CompilerParams(has_side_effects=True),
    )(g_in, g_out)


def half_add(name, g, r, c_idx, tb):
    _, rows2, C = g.shape
    h = rows2 // 2
    nb = h // tb

    def body(c_ref, g_ref, r_ref, f_ref, b_ref):
        s = g_ref[...] + r_ref[...]
        f_ref[...] = s
        b_ref[...] = s.astype(BF16)

    grid_spec = pltpu.PrefetchScalarGridSpec(
        num_scalar_prefetch=1, grid=(N_CHIPS, nb),
        in_specs=[pl.BlockSpec((1, tb, C), lambda k, i, c: (k, c[0] * nb + i, 0)),
                  pl.BlockSpec((1, tb, C), lambda k, i, c: (k, i, 0))],
        out_specs=[pl.BlockSpec((1, tb, C), lambda k, i, c: (k, i, 0))] * 2)
    return pl.pallas_call(
        body, name=name, grid_spec=grid_spec,
        out_shape=[jax.ShapeDtypeStruct((N_CHIPS, h, C), F32), jax.ShapeDtypeStruct((N_CHIPS, h, C), BF16)],
        compiler_params=_cp(("parallel", "parallel")),
    )(c_idx, g, r)


def chip_sum(name, f, r, me_idx, tb):
    _, h, C = f.shape
    nb = h // tb

    def body(me_ref, f_ref, r0, r1, r2, o_ref):
        o_ref[...] = ((f_ref[0] + r0[0].astype(F32)) + r1[0].astype(F32)) + r2[0].astype(F32)

    def slot(j):
        return pl.BlockSpec((1, tb, C), lambda i, me: (j, i, 0))

    grid_spec = pltpu.PrefetchScalarGridSpec(
        num_scalar_prefetch=1, grid=(nb,),
        in_specs=[pl.BlockSpec((1, tb, C), lambda i, me: (me[0], i, 0)), slot(0), slot(1), slot(2)],
        out_specs=pl.BlockSpec((tb, C), lambda i, me: (i, 0)))
    return pl.pallas_call(
        body, name=name, grid_spec=grid_spec, out_shape=jax.ShapeDtypeStruct((h, C), F32),
        compiler_params=_cp(("parallel",)),
    )(me_idx, f, r, r, r)


def share_halves(name, f_in, f_out):
    def body(fi, fo, oi, oo, send, recv):
        x, y, c = _mesh_pos()
        sib = (x, y, 1 - c)
        cps = [pltpu.make_async_remote_copy(src_ref=fi, dst_ref=oi, send_sem=send.at[0], recv_sem=recv.at[0],
                                            device_id=sib, device_id_type=MESH),
               pltpu.make_async_remote_copy(src_ref=fo, dst_ref=oo, send_sem=send.at[1], recv_sem=recv.at[1],
                                            device_id=sib, device_id_type=MESH)]
        for cp in cps:
            cp.start()
        for cp in cps:
            cp.wait()

    return pl.pallas_call(
        body, name=name, in_specs=[HBM, HBM], out_specs=[HBM, HBM],
        out_shape=[jax.ShapeDtypeStruct(f_in.shape, F32), jax.ShapeDtypeStruct(f_out.shape, F32)],
        scratch_shapes=[pltpu.SemaphoreType.DMA((2,)), pltpu.SemaphoreType.DMA((2,))],
        compiler_params=pltpu.CompilerParams(has_side_effects=True),
    )(f_in, f_out)


def gather_small(pack):
    def body(p, g, send, recv, lsem):
        x, y, c = _mesh_pos()
        me = 4 * x + 2 * y + c
        local = pltpu.make_async_copy(p, g.at[me], lsem)
        local.start()
        cps = []
        k = 0
        for fx in (0, 1):
            for fy in (0, 1):
                for fc in (0, 1):
                    if fx + fy + fc == 0:
                        continue
                    to = (x ^ fx, y ^ fy, c ^ fc)
                    cps.append((pltpu.make_async_remote_copy(src_ref=p, dst_ref=g.at[me], send_sem=send.at[k],
                                                             recv_sem=recv.at[k], device_id=to, device_id_type=MESH), to, k))
                    k += 1
        for cp, _, _ in cps:
            cp.start()
        for cp, to, k in cps:
            frm = 4 * to[0] + 2 * to[1] + to[2]
            pltpu.make_async_remote_copy(src_ref=p, dst_ref=g.at[frm], send_sem=send.at[k], recv_sem=recv.at[k],
                                         device_id=to, device_id_type=MESH).wait_recv()
        for cp, _, _ in cps:
            cp.wait_send()
        local.wait()

    return pl.pallas_call(
        body, name="gather_small", in_specs=[HBM], out_specs=HBM,
        out_shape=jax.ShapeDtypeStruct((N_DEV,) + pack.shape, pack.dtype),
        scratch_shapes=[pltpu.SemaphoreType.DMA((7,)), pltpu.SemaphoreType.DMA((7,)), pltpu.SemaphoreType.DMA],
        compiler_params=pltpu.CompilerParams(has_side_effects=True),
    )(pack)


def _adamw(w, g, m, v):
    m = ADAM_B1 * m + (1.0 - ADAM_B1) * g
    v = ADAM_B2 * v + (1.0 - ADAM_B2) * (g * g)
    m_hat = m / (1.0 - ADAM_B1 ** ADAM_STEP)
    v_hat = v / (1.0 - ADAM_B2 ** ADAM_STEP)
    delta = -ADAM_LR * (m_hat / (jnp.sqrt(v_hat) + ADAM_EPS) + ADAM_WD * w)
    return delta, m, v


def adamw_big(name, w, g, m, v):
    shp = w.shape
    flat = lambda a: a.reshape(shp[-2], shp[-1])
    (d, nm, nv), _ = rowcall(name, lambda w, g, m, v: (_adamw(w, g, m, v), ()), [flat(w), flat(g), flat(m), flat(v)], [],
                             [(shp[-1], F32)] * 3, tb=256)
    return d.reshape(shp), nm.reshape(shp), nv.reshape(shp)


def _pack_small(arrs):
    flat = jnp.concatenate([a.reshape(-1) for a in arrs])
    n = flat.shape[0]
    rows = -(-n // (SUBLANE * LANE)) * SUBLANE
    return jnp.pad(flat, (0, rows * LANE - n)).reshape(rows, LANE)


def _unpack_small(pack, shapes):
    flat = pack.reshape(-1)
    outs, off = [], 0
    for s in shapes:
        n = int(np.prod(s))
        outs.append(flat[off:off + n].reshape(s))
        off += n
    return outs


SMALL_NAMES = ("norm_g", "final_g", "rel_bias", "hgrn_lb", "ssd_conv_w", "ssd_conv_b", "ssd_dt_bias", "ssd_a_log",
               "ssd_d", "ssd_norm_g", "hg_norm_g", "at_q_norm_g", "at_k_norm_g")
ALL_NAMES = ("norm_g", "final_g", "rel_bias", "hgrn_lb", "ssd_w_in", "ssd_conv_w", "ssd_conv_b", "ssd_dt_bias",
             "ssd_a_log", "ssd_d", "ssd_norm_g", "ssd_w_out", "hg_w_in", "hg_norm_g", "hg_w_out", "at_w_in",
             "at_q_norm_g", "at_k_norm_g", "at_w_out", "dl_w_in", "dl_w_out")


def kernel(x, norm_g, final_g, rel_bias, hgrn_lb, ssd_w_in, ssd_conv_w, ssd_conv_b, ssd_dt_bias, ssd_a_log, ssd_d, ssd_norm_g, ssd_w_out, hg_w_in, hg_norm_g, hg_w_out, at_w_in, at_q_norm_g, at_k_norm_g, at_w_out, dl_w_in, dl_w_out, loss_target, m_norm_g, m_final_g, m_rel_bias, m_hgrn_lb, m_ssd_w_in, m_ssd_conv_w, m_ssd_conv_b, m_ssd_dt_bias, m_ssd_a_log, m_ssd_d, m_ssd_norm_g, m_ssd_w_out, m_hg_w_in, m_hg_norm_g, m_hg_w_out, m_at_w_in, m_at_q_norm_g, m_at_k_norm_g, m_at_w_out, m_dl_w_in, m_dl_w_out, v_norm_g, v_final_g, v_rel_bias, v_hgrn_lb, v_ssd_w_in, v_ssd_conv_w, v_ssd_conv_b, v_ssd_dt_bias, v_ssd_a_log, v_ssd_d, v_ssd_norm_g, v_ssd_w_out, v_hg_w_in, v_hg_norm_g, v_hg_w_out, v_at_w_in, v_at_q_norm_g, v_at_k_norm_g, v_at_w_out, v_dl_w_in, v_dl_w_out):
    args = locals()
    W = {n: args[n] for n in ALL_NAMES}
    M = {n: args["m_" + n] for n in ALL_NAMES}
    V = {n: args["v_" + n] for n in ALL_NAMES}
    xi, yi, ci = lax.axis_index("x"), lax.axis_index("y"), lax.axis_index("c")
    chip = 2 * xi + yi
    conv_shard = SSD_CONV_CH // N_CHIPS
    hgn_shard = HG_W // N_CHIPS

    p_in = [W[n][0].astype(BF16) for n in IN_NAMES]
    p_out = [W[n][0].astype(BF16) for n in OUT_NAMES]
    p_small = jnp.concatenate([
        jnp.pad(ssd_conv_w[0], ((0, 0), (0, D_MODEL - conv_shard))),
        jnp.pad(hg_norm_g, ((0, 0), (0, D_MODEL - hgn_shard)))], axis=0)
    c_idx = ci.astype(jnp.int32).reshape(1)
    me_idx = chip.astype(jnp.int32).reshape(1)

    def slot(stack, own, k):
        return jnp.where(chip == k, own, stack[k])

    def layer_weights(layer, got):
        s_in, s_out = got[0], got[1]
        return (jnp.concatenate([slot(s_in, p_in[layer], k) for k in range(N_CHIPS)], axis=1),
                jnp.concatenate([slot(s_out, p_out[layer], k) for k in range(N_CHIPS)], axis=0))

    def reduce_start(tag, layers, grads):
        gp_in = jnp.concatenate([grads[IN_NAMES[l]][0].reshape(D_MODEL, N_CHIPS, IN_COLS[l]).transpose(1, 0, 2)
                                 for l in layers], axis=2)
        gp_out = jnp.concatenate([grads[OUT_NAMES[l]][0].reshape(N_CHIPS, OUT_ROWS[l], D_MODEL) for l in layers], axis=1)
        r_in, r_out = swap_halves(f"swap_halves_{tag}", gp_in, gp_out)
        f_in, b_in = half_add(f"half_add_in_{tag}", gp_in, r_in, c_idx, 128)
        f_out, b_out = half_add(f"half_add_out_{tag}", gp_out, r_out, c_idx, 128)
        return (f_in, f_out), scatter_plan([b_in, b_out])

    def reduce_finish(tag, layers, halves, got, G):
        s_in = chip_sum(f"chip_sum_in_{tag}", halves[0], got[0], me_idx, 128)
        s_out = chip_sum(f"chip_sum_out_{tag}", halves[1], got[1], me_idx, 128)
        o_in, o_out = share_halves(f"share_halves_{tag}", s_in, s_out)
        red_in = jnp.where(ci == 0, jnp.concatenate([s_in, o_in], axis=0), jnp.concatenate([o_in, s_in], axis=0))
        red_out = jnp.where(ci == 0, jnp.concatenate([s_out, o_out], axis=0), jnp.concatenate([o_out, s_out], axis=0))
        off_c = off_r = 0
        for l in layers:
            G[IN_NAMES[l]] = red_in[:, off_c:off_c + IN_COLS[l]][None]
            G[OUT_NAMES[l]] = red_out[off_r:off_r + OUT_ROWS[l]][None]
            off_c += IN_COLS[l]
            off_r += OUT_ROWS[l]

    class Schedule:
        early = (2, 3)
        middle = (1,)

        def fwd_plans(self, layer, w):
            if layer == 0:
                def ssd_out(got):
                    stack = got["ssd_scan_b"][1]
                    return dict(ssd_w_out=jnp.concatenate([slot(stack, p_out[0], k) for k in range(N_CHIPS)], axis=0))

                return dict(ssd_scan_f=gather_plan([p_in[1], p_out[1], p_out[2]]),
                            ssd_scan_b=gather_plan([p_in[2], p_out[0]]), late_weights=ssd_out)
            if layer == 1:
                return dict(hg_scan_f=gather_plan([p_in[3]]), hg_scan_b=gather_plan([p_out[3]]))
            return None

        def fwd_done(self, layer, got, w):
            if layer == 0:
                w["hg_w_in"], w["hg_w_out"] = layer_weights(1, got["ssd_scan_f"])
                w["at_w_in"], w["at_w_out"] = layer_weights(2, (got["ssd_scan_b"][0], got["ssd_scan_f"][2]))
            if layer == 1:
                w["dl_w_in"], w["dl_w_out"] = layer_weights(3, got["hg_scan_f"] + got["hg_scan_b"])

        def bwd_plans(self, layer, grads):
            if layer == 1:
                self.halves_a, plan = reduce_start("a", self.early, grads)
                return dict(hg_scan_f_bwd=plan)
            if layer == 0:
                self.halves_c, plan = reduce_start("c", self.middle, grads)
                return dict(ssd_scan_f_bwd=plan)
            return None

        def bwd_done(self, layer, got):
            if layer == 1:
                self.got_a = got["hg_scan_f_bwd"]
            if layer == 0:
                self.got_c = got["ssd_scan_f_bwd"]

    g0_in, g_small = run_exchange("gather_w0", gather_plan([p_in[0]], whole=[p_small]))
    ssd_in_full = jnp.concatenate([slot(g0_in, p_in[0], k) for k in range(N_CHIPS)], axis=1)
    conv_full = jnp.concatenate([slot(g_small, p_small, k)[:SSD_CONV, :conv_shard] for k in range(N_CHIPS)], axis=1)
    hgn_full = jnp.concatenate([slot(g_small, p_small, k)[SSD_CONV:SSD_CONV + 1, :hgn_shard] for k in range(N_CHIPS)], axis=1)
    w = dict(
        norm_g=norm_g, final_g=final_g, rel_bias=rel_bias, hgrn_lb=hgrn_lb,
        ssd_w_main=ssd_in_full[:, :SSD_MAIN], ssd_w_dt=ssd_in_full[:, SSD_MAIN:],
        ssd_conv_w8=jnp.concatenate([conv_full, jnp.zeros((1, SSD_CONV_CH), F32)], axis=0),
        ssd_conv_b=ssd_conv_b, ssd_dt_bias=ssd_dt_bias, ssd_a_log=ssd_a_log, ssd_d=ssd_d, ssd_norm_g=ssd_norm_g,
        hg_norm_g=hgn_full, at_q_norm_g=at_q_norm_g, at_k_norm_g=at_k_norm_g)

    sched = Schedule()
    loss_tile, grad_x, grads = local_step(x[0], loss_target[0], w, sched)
    loss = lax.psum(loss_tile[0, 0], ("x", "y", "c"))

    G = {}
    late = (0,)
    halves_b, plan_b = reduce_start("b", late, grads)
    got_b = run_exchange("scatter_b", plan_b)
    reduce_finish("a", sched.early, sched.halves_a, sched.got_a, G)
    reduce_finish("c", sched.middle, sched.halves_c, sched.got_c, G)
    reduce_finish("b", late, halves_b, got_b, G)

    small_full = [grads[n].reshape(-1) for n in SMALL_NAMES]
    shapes_full = [grads[n].shape for n in SMALL_NAMES]
    packs = gather_small(_pack_small(small_full))
    (red_small,) = smallcall("sum_small", lambda p: (functools.reduce(lambda a, b: a + b, [p[k] for k in range(N_DEV)]),),
                             [packs], [packs.shape[1:]])
    for n, g in zip(SMALL_NAMES, _unpack_small(red_small, shapes_full)):
        G[n] = g
    G["ssd_conv_w"] = lax.dynamic_slice_in_dim(G["ssd_conv_w"].reshape(1, SSD_CONV, SSD_CONV_CH), chip * conv_shard, conv_shard, axis=2)
    G["hg_norm_g"] = lax.dynamic_slice_in_dim(G["hg_norm_g"].reshape(1, HG_W), chip * hgn_shard, hgn_shard, axis=1)
    for n in SMALL_NAMES:
        G[n] = G[n].reshape(W[n].shape)

    D, NM, NV = {}, {}, {}
    for n in IN_NAMES + OUT_NAMES:
        D[n], NM[n], NV[n] = adamw_big("adamw_" + n, W[n], G[n], M[n], V[n])
    shapes = [W[n].shape for n in SMALL_NAMES]
    pk = [_pack_small([T[n] for n in SMALL_NAMES]) for T in (W, G, M, V)]
    outs = smallcall("adamw_small", lambda w, g, m, v: _adamw(w, g, m, v), pk, [pk[0].shape] * 3)
    for T, pack in zip((D, NM, NV), outs):
        for n, a in zip(SMALL_NAMES, _unpack_small(pack, shapes)):
            T[n] = a
    return (loss, grad_x[None], *[G[n] for n in ALL_NAMES], *[D[n] for n in ALL_NAMES],
            *[NM[n] for n in ALL_NAMES], *[NV[n] for n in ALL_NAMES])
```
